```python
import jax, jax.numpy as jnp
from jax import lax
import numpy as np

D_MODEL = 1024
BATCH = 8
SEQ = 2048
DEPTH = 4

CTX_LEN = 256
GRID_W = 64
HEAD_DIM = 64
ROPE_THETA = 10000.0
NORM_EPS = 1e-6
NEG_INF = -1e30
Q_BLOCK = 128

NA_HEADS = 4
NA_KH = 8
NA_KW = 16
GB_Q_HEADS = 4
GB_KV_HEADS = 2
WC_Q_HEADS = 4
WC_KV_HEADS = 2
WC_WINDOW = 128
MLA_HEADS = 4
MLA_Q_RANK = 256
MLA_KV_RANK = 128
MLA_NOPE = 64
MLA_ROPE = 32
MLA_V = 64

N_BRANCH = 4
BRANCH_W = 256
MIX_W = N_BRANCH * BRANCH_W

D_FF = 3584
N_EXPERTS = 8
TOP_K = 2
N_DENSE = (DEPTH + 1) // 2
N_MOE = DEPTH // 2
ADA_W = 6 * D_MODEL

IN_SIZES = (NA_HEADS * HEAD_DIM, NA_HEADS * HEAD_DIM, NA_HEADS * HEAD_DIM,
            GB_Q_HEADS * HEAD_DIM, GB_KV_HEADS * HEAD_DIM, GB_KV_HEADS * HEAD_DIM,
            WC_Q_HEADS * HEAD_DIM, WC_KV_HEADS * HEAD_DIM, WC_KV_HEADS * HEAD_DIM,
            MLA_Q_RANK, MLA_KV_RANK + MLA_ROPE,
            N_BRANCH * D_MODEL)
IN_SPLITS = tuple(sum(IN_SIZES[: i + 1]) for i in range(len(IN_SIZES) - 1))
IN_W = sum(IN_SIZES)

kernel_name = 'hybrid_gated_mixer_dit_block'


def rmsnorm(x, g):
    xf = x.astype(jnp.float32)
    y = xf * lax.rsqrt(jnp.mean(xf * xf, axis=-1, keepdims=True) + NORM_EPS)
    return (y * g.astype(jnp.float32)).astype(x.dtype)


def modulate(xn, shift, scale):
    return xn * (1 + scale) + shift


def _axis_angles(pos, dim):
    inv = ROPE_THETA ** (-jnp.arange(0, dim, 2, dtype=jnp.float32) / dim)
    return pos.astype(jnp.float32)[:, None] * inv[None, :]


def axial_rope(n_tokens, rot_dim):
    t = jnp.arange(n_tokens)
    half = rot_dim // 2
    ang = jnp.concatenate([_axis_angles(t // GRID_W, half), _axis_angles(t % GRID_W, half)], axis=-1)
    return jnp.cos(ang), jnp.sin(ang)


def apply_rope(x, cos, sin):
    d2 = x.shape[-1] // 2
    x1, x2 = x[..., :d2], x[..., d2:]
    c, s = cos.astype(x.dtype), sin.astype(x.dtype)
    return jnp.concatenate([x1 * c - x2 * s, x1 * s + x2 * c], axis=-1)


def group_heads(q, n_kv):
    B, T, H, d = q.shape
    return q.reshape(B, T, n_kv, H // n_kv, d)


def dense_blocked_attn(q, k, v, scale):
    B, T, Hkv, G, dq = q.shape
    nb = T // Q_BLOCK
    qb = jnp.moveaxis(q.reshape(B, nb, Q_BLOCK, Hkv, G, dq), 1, 0)

    def one_block(qblk):
        s = jnp.einsum('bqhgd,bkhd->bhgqk', qblk, k, preferred_element_type=jnp.float32) * scale
        p = jax.nn.softmax(s, axis=-1).astype(v.dtype)
        return jnp.einsum('bhgqk,bkhd->bqhgd', p, v)

    out = lax.map(one_block, qb)
    return jnp.moveaxis(out, 0, 1).reshape(B, T, -1)


def sink_attn(q, k, v, sink, scale):
    B, T, Hkv, G, d = q.shape
    s = jnp.einsum('bqhgd,bkhd->bhgqk', q, k, preferred_element_type=jnp.float32) * scale
    s_sink = jnp.broadcast_to(sink.astype(jnp.float32).reshape(Hkv, G)[:, :, None, None], s.shape[:-1] + (1,))
    p = jax.nn.softmax(jnp.concatenate([s, s_sink], axis=-1), axis=-1)[..., :-1].astype(v.dtype)
    return jnp.einsum('bhgqk,bkhd->bqhgd', p, v).reshape(B, T, -1)


def window_sink_attn(q, k, v, kc, vc, sink, scale):
    B, S, Hkv, G, d = q.shape
    nb = S // Q_BLOCK
    nk = 3 * Q_BLOCK
    pad = ((0, 0), (Q_BLOCK, Q_BLOCK), (0, 0), (0, 0))
    kp = jnp.pad(k, pad).reshape(B, nb + 2, Q_BLOCK, Hkv, d)
    vp = jnp.pad(v, pad).reshape(B, nb + 2, Q_BLOCK, Hkv, d)
    kband = jnp.concatenate([kp[:, :-2], kp[:, 1:-1], kp[:, 2:]], axis=2)
    vband = jnp.concatenate([vp[:, :-2], vp[:, 1:-1], vp[:, 2:]], axis=2)
    qb = q.reshape(B, nb, Q_BLOCK, Hkv, G, d)
    qpos = jnp.arange(nb)[:, None] * Q_BLOCK + jnp.arange(Q_BLOCK)[None, :]
    kpos = jnp.arange(nb)[:, None] * Q_BLOCK - Q_BLOCK + jnp.arange(nk)[None, :]
    ok = ((jnp.abs(qpos[:, :, None] - kpos[:, None, :]) <= WC_WINDOW)
          & (kpos >= 0)[:, None, :] & (kpos < S)[:, None, :])
    s_loc = jnp.einsum('bnqhgd,bnkhd->bnhgqk', qb, kband, preferred_element_type=jnp.float32) * scale
    s_loc = jnp.where(ok[None, :, None, None], s_loc, NEG_INF)
    s_ctx = jnp.einsum('bnqhgd,bchd->bnhgqc', qb, kc, preferred_element_type=jnp.float32) * scale
    s_sink = jnp.broadcast_to(sink.astype(jnp.float32).reshape(Hkv, G)[:, :, None, None], s_loc.shape[:-1] + (1,))
    p = jax.nn.softmax(jnp.concatenate([s_loc, s_ctx, s_sink], axis=-1), axis=-1).astype(v.dtype)
    o = (jnp.einsum('bnhgqk,bnkhd->bnqhgd', p[..., :nk], vband)
         + jnp.einsum('bnhgqc,bchd->bnqhgd', p[..., nk:nk + kc.shape[1]], vc))
    return o.reshape(B, S, -1)


def natten_attn(q, k, v, kc, vc, rpb, scale):
    B, S, H, d = q.shape
    rows = S // GRID_W
    kh = min(NA_KH, rows)
    r = jnp.arange(rows)
    krow = jnp.clip(r - kh // 2, 0, rows - kh)[:, None] + jnp.arange(kh)[None, :]
    col = jnp.arange(GRID_W)
    cstart = jnp.clip(col - NA_KW // 2, 0, GRID_W - NA_KW)
    col_ok = (col[None, :] >= cstart[:, None]) & (col[None, :] < cstart[:, None] + NA_KW)
    dr = krow - r[:, None] + NA_KH - 1
    dc = jnp.clip(col[None, :] - col[:, None] + NA_KW - 1, 0, 2 * NA_KW - 2)
    bias = rpb.astype(jnp.float32)[:, dr[:, None, :, None], dc[None, :, None, :]]
    qg = q.reshape(B, rows, GRID_W, H, d)
    kwin = k.reshape(B, rows, GRID_W, H, d)[:, krow]
    vwin = v.reshape(B, rows, GRID_W, H, d)[:, krow]
    s_loc = jnp.einsum('brqhd,brjkhd->bhrqjk', qg, kwin, preferred_element_type=jnp.float32) * scale + bias[None]
    s_loc = jnp.where(col_ok[:, None, :], s_loc, NEG_INF)
    n_loc = kh * GRID_W
    s_loc = s_loc.reshape(B, H, rows, GRID_W, n_loc)
    s_ctx = jnp.einsum('brqhd,bchd->bhrqc', qg, kc, preferred_element_type=jnp.float32) * scale
    p = jax.nn.softmax(jnp.concatenate([s_loc, s_ctx], axis=-1), axis=-1).astype(v.dtype)
    p_loc = p[..., :n_loc].reshape(B, H, rows, GRID_W, kh, GRID_W)
    o = (jnp.einsum('bhrqjk,brjkhd->brqhd', p_loc, vwin)
         + jnp.einsum('bhrqc,bchd->brqhd', p[..., n_loc:], vc))
    return o.reshape(B, S, H * d)


def mixer_features(h, w_in_l, gb_qn, gb_kn, mla_qn, mla_kvn, mla_wqb_l, mla_wkvb_l, rope):
    B, T, _ = h.shape
    p = jnp.einsum('btd,de->bte', h, w_in_l)
    aq, ak, av, bq, bk, bv, cq, ck, cv, dqa, dkva, gate = jnp.split(p, IN_SPLITS, axis=-1)
    aq, ak, av = (t.reshape(B, T, NA_HEADS, HEAD_DIM) for t in (aq, ak, av))
    bq = rmsnorm(bq.reshape(B, T, GB_Q_HEADS, HEAD_DIM), gb_qn)
    bk = rmsnorm(bk.reshape(B, T, GB_KV_HEADS, HEAD_DIM), gb_kn)
    bv = bv.reshape(B, T, GB_KV_HEADS, HEAD_DIM)
    cq = cq.reshape(B, T, WC_Q_HEADS, HEAD_DIM)
    ck = ck.reshape(B, T, WC_KV_HEADS, HEAD_DIM)
    cv = cv.reshape(B, T, WC_KV_HEADS, HEAD_DIM)
    dq = jnp.einsum('btr,re->bte', rmsnorm(dqa, mla_qn), mla_wqb_l).reshape(B, T, MLA_HEADS, MLA_NOPE + MLA_ROPE)
    dkv_c, dk_pe = jnp.split(dkva, [MLA_KV_RANK], axis=-1)
    dkv = jnp.einsum('btr,re->bte', rmsnorm(dkv_c, mla_kvn), mla_wkvb_l).reshape(B, T, MLA_HEADS, MLA_NOPE + MLA_V)
    dk_nope, dv = jnp.split(dkv, [MLA_NOPE], axis=-1)
    dq_nope, dq_pe = jnp.split(dq, [MLA_NOPE], axis=-1)
    if rope is not None:
        cos_h, sin_h, cos_r, sin_r = rope
        ch, sh = cos_h[:, None, :], sin_h[:, None, :]
        bq, bk = apply_rope(bq, ch, sh), apply_rope(bk, ch, sh)
        cq, ck = apply_rope(cq, ch, sh), apply_rope(ck, ch, sh)
        dq_pe = apply_rope(dq_pe, cos_r[:, None, :], sin_r[:, None, :])
        dk_pe = apply_rope(dk_pe, cos_r, sin_r)
    dq = jnp.concatenate([dq_nope, dq_pe], axis=-1)
    dk = jnp.concatenate([dk_nope, jnp.broadcast_to(dk_pe[:, :, None, :], (B, T, MLA_HEADS, MLA_ROPE))], axis=-1)
    return (aq, ak, av), (bq, bk, bv), (cq, ck, cv), (dq, dk, dv), gate


def merge_branches(outs, gate, w_branch_l, w_out_l):
    B, T = gate.shape[:2]
    o = jnp.stack([t.reshape(B, T, BRANCH_W) for t in outs], axis=2)
    y = jnp.einsum('btnw,nwd->btnd', o, w_branch_l)
    g = jax.nn.sigmoid(gate.reshape(B, T, N_BRANCH, D_MODEL))
    return jnp.einsum('btd,de->bte', jnp.sum(g * y, axis=2), w_out_l)


def token_mixers(h, hc, w_in_l, rpb_l, gb_qn, gb_kn, sink_l, mla_qn, mla_kvn, mla_wqb_l, mla_wkvb_l,
                 w_branch_l, w_out_l, rope, with_ctx_out):
    scale = HEAD_DIM ** -0.5
    mla_scale = (MLA_NOPE + MLA_ROPE) ** -0.5
    (aq, ak, av), (bq, bk, bv), (cq, ck, cv), (dq, dk, dv), gate = mixer_features(
        h, w_in_l, gb_qn, gb_kn, mla_qn, mla_kvn, mla_wqb_l, mla_wkvb_l, rope)
    (aqc, akc, avc), (bqc, bkc, bvc), (cqc, ckc, cvc), (dqc, dkc, dvc), gate_c = mixer_features(
        hc, w_in_l, gb_qn, gb_kn, mla_qn, mla_kvn, mla_wqb_l, mla_wkvb_l, None)
    cat = lambda a, b: jnp.concatenate([a, b], axis=1)
    oa = natten_attn(aq, ak, av, akc, avc, rpb_l, scale)
    ob = dense_blocked_attn(group_heads(bq, GB_KV_HEADS), cat(bk, bkc), cat(bv, bvc), scale)
    oc = window_sink_attn(group_heads(cq, WC_KV_HEADS), ck, cv, ckc, cvc, sink_l, scale)
    od = dense_blocked_attn(group_heads(dq, MLA_HEADS), cat(dk, dkc), cat(dv, dvc), mla_scale)
    mix = merge_branches((oa, ob, oc, od), gate, w_branch_l, w_out_l)
    if not with_ctx_out:
        return mix, None
    oac = dense_blocked_attn(group_heads(aqc, NA_HEADS), akc, avc, scale)
    obc = dense_blocked_attn(group_heads(bqc, GB_KV_HEADS), bkc, bvc, scale)
    occ = sink_attn(group_heads(cqc, WC_KV_HEADS), ckc, cvc, sink_l, scale)
    odc = dense_blocked_attn(group_heads(dqc, MLA_HEADS), dkc, dvc, mla_scale)
    mix_c = merge_branches((oac, obc, occ, odc), gate_c, w_branch_l, w_out_l)
    return mix, mix_c


def swiglu(h, w1, w3, w2):
    a = jnp.einsum('btd,df->btf', h, w1)
    b = jnp.einsum('btd,df->btf', h, w3)
    return jnp.einsum('btf,fd->btd', jax.nn.silu(a) * b, w2)


def moe_swiglu(h, router, w1, w3, w2):
    logits = jnp.einsum('btd,de->bte', h, router, preferred_element_type=jnp.float32)
    top_val, top_idx = lax.top_k(logits, TOP_K)
    top_w = jax.nn.softmax(top_val, axis=-1)
    combine = jnp.einsum('btk,btke->bte', top_w,
                         jax.nn.one_hot(top_idx, N_EXPERTS, dtype=jnp.float32)).astype(h.dtype)
    out = jnp.zeros_like(h)
    for e in range(N_EXPERTS):
        out = out + combine[..., e:e + 1] * swiglu(h, w1[e], w3[e], w2[e])
    return out


def setup_inputs(seed: int = 0) -> dict:
    key = jax.random.key(seed)
    ks = jax.random.split(key, 32)
    D = D_MODEL

    def nrm(k, shape, s):
        return jax.random.normal(k, shape, jnp.float32) * s

    return {
        'x': nrm(ks[0], (BATCH, SEQ, D), 1.0),
        'c': nrm(ks[1], (BATCH, D), 1.0),
        'ctx': nrm(ks[2], (BATCH, CTX_LEN, D), 1.0),
        'c_ctx': nrm(ks[3], (D,), 1.0),
        'norm1_g': 1.0 + nrm(ks[4], (DEPTH, D), 0.02),
        'norm2_g': 1.0 + nrm(ks[5], (DEPTH, D), 0.02),
        'w_ada': nrm(ks[6], (DEPTH, D, ADA_W), 0.5 * D ** -0.5),
        'b_ada': nrm(ks[7], (DEPTH, ADA_W), 0.02),
        'w_in': nrm(ks[8], (DEPTH, D, IN_W), D ** -0.5),
        'na_rpb': nrm(ks[9], (DEPTH, NA_HEADS, 2 * NA_KH - 1, 2 * NA_KW - 1), 0.1),
        'gb_qnorm': 1.0 + nrm(ks[10], (DEPTH, HEAD_DIM), 0.02),
        'gb_knorm': 1.0 + nrm(ks[11], (DEPTH, HEAD_DIM), 0.02),
        'wc_sink': nrm(ks[12], (DEPTH, WC_Q_HEADS), 0.5),
        'mla_qnorm': 1.0 + nrm(ks[13], (DEPTH, MLA_Q_RANK), 0.02),
        'mla_kvnorm': 1.0 + nrm(ks[14], (DEPTH, MLA_KV_RANK), 0.02),
        'mla_wqb': nrm(ks[15], (DEPTH, MLA_Q_RANK, MLA_HEADS * (MLA_NOPE + MLA_ROPE)), MLA_Q_RANK ** -0.5),
        'mla_wkvb': nrm(ks[16], (DEPTH, MLA_KV_RANK, MLA_HEADS * (MLA_NOPE + MLA_V)), MLA_KV_RANK ** -0.5),
        'w_branch': nrm(ks[17], (DEPTH, N_BRANCH, BRANCH_W, D), BRANCH_W ** -0.5),
        'w_out': nrm(ks[18], (DEPTH, D, D), D ** -0.5),
        'ffn_w1': nrm(ks[19], (N_DENSE, D, D_FF), D ** -0.5),
        'ffn_w3': nrm(ks[20], (N_DENSE, D, D_FF), D ** -0.5),
        'ffn_w2': nrm(ks[21], (N_DENSE, D_FF, D), D_FF ** -0.5),
        'moe_router': nrm(ks[22], (N_MOE, D, N_EXPERTS), D ** -0.5),
        'moe_w1': nrm(ks[23], (N_MOE, N_EXPERTS, D, D_FF), D ** -0.5),
        'moe_w3': nrm(ks[24], (N_MOE, N_EXPERTS, D, D_FF), D ** -0.5),
        'moe_w2': nrm(ks[25], (N_MOE, N_EXPERTS, D_FF, D), D_FF ** -0.5),
        'final_g': 1.0 + nrm(ks[26], (D,), 0.02),
    }


def reference(x, c, ctx, c_ctx, norm1_g, norm2_g, w_ada, b_ada, w_in, na_rpb, gb_qnorm, gb_knorm, wc_sink,
              mla_qnorm, mla_kvnorm, mla_wqb, mla_wkvb, w_branch, w_out, ffn_w1, ffn_w3, ffn_w2,
              moe_router, moe_w1, moe_w3, moe_w2, final_g):
    S = x.shape[1]
    C = ctx.shape[1]
    rope = (*axial_rope(S, HEAD_DIM), *axial_rope(S, MLA_ROPE))
    silu_c = jax.nn.silu(c)
    silu_cc = jax.nn.silu(c_ctx)
    xc = ctx
    for l in range(DEPTH):
        with_ctx = l < DEPTH - 1
        mod = jnp.einsum('bd,de->be', silu_c, w_ada[l]) + b_ada[l]
        mod_c = jnp.einsum('d,de->e', silu_cc, w_ada[l]) + b_ada[l]
        sh1, sc1, g1, sh2, sc2, g2 = jnp.split(mod[:, None, :], 6, axis=-1)
        sh1c, sc1c, g1c, sh2c, sc2c, g2c = jnp.split(mod_c, 6, axis=-1)
        h = modulate(rmsnorm(x, norm1_g[l]), sh1, sc1)
        hc = modulate(rmsnorm(xc, norm1_g[l]), sh1c, sc1c)
        mix, mix_c = token_mixers(h, hc, w_in[l], na_rpb[l], gb_qnorm[l], gb_knorm[l], wc_sink[l],
                                  mla_qnorm[l], mla_kvnorm[l], mla_wqb[l], mla_wkvb[l],
                                  w_branch[l], w_out[l], rope, with_ctx)
        x = x + g1 * mix
        h2 = modulate(rmsnorm(x, norm2_g[l]), sh2, sc2)
        if with_ctx:
            xc = xc + g1c * mix_c
            h2c = modulate(rmsnorm(xc, norm2_g[l]), sh2c, sc2c)
            h2 = jnp.concatenate([h2c, h2], axis=1)
        if l % 2 == 0:
            f = swiglu(h2, ffn_w1[l // 2], ffn_w3[l // 2], ffn_w2[l // 2])
        else:
            f = moe_swiglu(h2, moe_router[l // 2], moe_w1[l // 2], moe_w3[l // 2], moe_w2[l // 2])
        if with_ctx:
            xc = xc + g2c * f[:, :C]
            f = f[:, C:]
        x = x + g2 * f
    return rmsnorm(x, final_g)
```

```python
import functools

import numpy as np
import jax
import jax.numpy as jnp
from jax import lax
from jax.experimental import pallas as pl
from jax.experimental.pallas import tpu as pltpu

F32 = jnp.float32
BF16 = jnp.bfloat16

D_MODEL = 1024
SEQ = 2048
DEPTH = 4
CTX_LEN = 256
GRID_W = 64
GRID_ROWS = SEQ // GRID_W
HEAD_DIM = 64
ROPE_THETA = 10000.0
NORM_EPS = 1e-6
NEG_INF = -1e30

NA_HEADS = 4
NA_KH = 8
NA_KW = 16
GB_Q_HEADS = 4
GB_KV_HEADS = 2
WC_Q_HEADS = 4
WC_KV_HEADS = 2
WC_WINDOW = 128
MLA_HEADS = 4
MLA_Q_RANK = 256
MLA_KV_RANK = 128
MLA_NOPE = 64
MLA_ROPE = 32
MLA_V = 64
MLA_PAD = 128
N_BRANCH = 4
BRANCH_W = 256
D_FF = 3584
N_EXPERTS = 8
ADA_CHUNKS = 6

VMEM_LIMIT_BYTES = 56 * 1024 * 1024
LANES = 128

CTX_ROWS = 2048
ROW_TILE = 512
Q_TILE = 256
NA_Q_ROWS = Q_TILE // GRID_W
NA_WIN_ROWS = 12
NA_WIN = NA_WIN_ROWS * GRID_W
WC_WIN = Q_TILE + 2 * WC_WINDOW

COL_AQ, COL_AK, COL_AV = 0, 256, 512
COL_BQ, COL_BK, COL_BV = 768, 1024, 1152
COL_CQ, COL_CK, COL_CV = 1280, 1536, 1664
COL_DQA, COL_DKVA, COL_GATE = 1792, 2048, 2304
GATE_W = N_BRANCH * D_MODEL
P_WIDTH = COL_GATE + GATE_W
QKV_CHUNK = COL_GATE // 2
FF_CHUNK = 512


def _params(*sem):
    return pltpu.CompilerParams(dimension_semantics=sem, vmem_limit_bytes=VMEM_LIMIT_BYTES)


def _dot(a, b):
    return jnp.dot(a, b, preferred_element_type=F32)


def _dot_nt(a, b):
    return lax.dot_general(a, b, (((1,), (1,)), ((), ())), preferred_element_type=F32)


def _split3(x):
    hi = x.astype(BF16)
    r1 = x - hi.astype(F32)
    mid = r1.astype(BF16)
    lo = (r1 - mid.astype(F32)).astype(BF16)
    return hi, mid, lo


def _sigmoid(x):
    return 1.0 / (1.0 + jnp.exp(-x))


ADA_TN = 1536


def _mod_kernel(c_ref, w_ref, b_ref, o_ref):
    c = c_ref[...]
    sc = (c * _sigmoid(c)).astype(BF16)
    o_ref[0] = _dot(sc, w_ref[0].astype(BF16)) + b_ref[0]


def _mod_call(cvec, w_ada, b_ada):
    n = ADA_CHUNKS * D_MODEL
    return pl.pallas_call(
        _mod_kernel,
        grid=(DEPTH, n // ADA_TN),
        in_specs=[
            pl.BlockSpec((16, D_MODEL), lambda l, j: (0, 0)),
            pl.BlockSpec((1, D_MODEL, ADA_TN), lambda l, j: (l, 0, j)),
            pl.BlockSpec((1, 1, ADA_TN), lambda l, j: (l, 0, j)),
        ],
        out_specs=pl.BlockSpec((1, 16, ADA_TN), lambda l, j: (l, 0, j)),
        out_shape=jax.ShapeDtypeStruct((DEPTH, 16, n), F32),
        compiler_params=_params("arbitrary", "arbitrary"),
        name="adaln_mod",
    )(cvec, w_ada, b_ada.reshape(DEPTH, 1, n))


def _mod_row(i, first_tile):
    n_ctx = CTX_ROWS // ROW_TILE
    t = i + first_tile
    return jnp.where(t < n_ctx, 8, (t - n_ctx) // (SEQ // ROW_TILE))


def _norm_mod(x, g, shift, scale):
    ms = jnp.mean(x * x, axis=-1, keepdims=True)
    y = x * lax.rsqrt(ms + NORM_EPS) * g
    return y * (1.0 + scale) + shift


def _inproj_kernel(x_ref, mod_ref, g_ref, w_ref, p_ref, gate_ref):
    m = mod_ref[0]
    hb = _norm_mod(x_ref[...], g_ref[...], m[0:1], m[1:2]).astype(BF16)
    for c0 in range(0, COL_GATE, QKV_CHUNK):
        p_ref[:, c0:c0 + QKV_CHUNK] = _dot(hb, w_ref[:, c0:c0 + QKV_CHUNK]).astype(BF16)
    for c0 in range(0, GATE_W, D_MODEL):
        gate_ref[:, c0:c0 + D_MODEL] = _dot(hb, w_ref[:, COL_GATE + c0:COL_GATE + c0 + D_MODEL]).astype(BF16)


def _inproj_call(x, mod_l, g, w_in_p):
    t = x.shape[0]
    return pl.pallas_call(
        _inproj_kernel,
        grid=(t // ROW_TILE,),
        in_specs=[
            pl.BlockSpec((ROW_TILE, D_MODEL), lambda i: (i, 0)),
            pl.BlockSpec((1, ADA_CHUNKS, D_MODEL), lambda i: (_mod_row(i, 0), 0, 0)),
            pl.BlockSpec((1, D_MODEL), lambda i: (0, 0)),
            pl.BlockSpec((D_MODEL, P_WIDTH), lambda i: (0, 0), pipeline_mode=pl.Buffered(1)),
        ],
        out_specs=[pl.BlockSpec((ROW_TILE, COL_GATE), lambda i: (i, 0)),
                   pl.BlockSpec((ROW_TILE, GATE_W), lambda i: (i, 0))],
        out_shape=[jax.ShapeDtypeStruct((t, COL_GATE), BF16), jax.ShapeDtypeStruct((t, GATE_W), BF16)],
        compiler_params=_params("arbitrary"),
        name="inproj",
    )(x, mod_l, g, w_in_p)


def _rope(x, cos, sin, half, first_mask):
    w = x.shape[-1]
    fwd = pltpu.roll(x, w - half, 1)
    bwd = pltpu.roll(x, half, 1)
    rot = jnp.where(first_mask, -fwd, bwd)
    return x * cos + rot * sin


def _head_rms(x, gain, blockdiag):
    hi, mid, lo = _split3(x * x)
    ss = _dot(hi, blockdiag) + _dot(mid, blockdiag) + _dot(lo, blockdiag)
    return x * lax.rsqrt(ss * (1.0 / HEAD_DIM) + NORM_EPS) * gain


def _prep_kernel(bq_ref, bk_ref, cq_ref, ck_ref, dqa_ref, dkva_ref,
                 cos64_ref, sin64_ref, cosm_ref, sinm_ref,
                 gq_ref, gk_ref, bd_ref, qn_ref, kvn_ref, wqb_ref, wk_ref, wv_ref, pe_ref,
                 qb_ref, kb_ref, qc_ref, kc_ref, qd_ref, kd_ref, vd_ref):
    scale = HEAD_DIM ** -0.5
    cos64, sin64 = cos64_ref[...], sin64_ref[...]
    lane = lax.broadcasted_iota(jnp.int32, (1, 256), 1)
    first64 = (lane % HEAD_DIM) < (HEAD_DIM // 2)
    bd = bd_ref[...]

    bq = _head_rms(bq_ref[...].astype(F32), gq_ref[...], bd)
    qb_ref[...] = (_rope(bq, cos64, sin64, HEAD_DIM // 2, first64) * scale).astype(BF16)
    bk = _head_rms(bk_ref[...].astype(F32), gk_ref[...], bd[:128, :128])
    kb_ref[...] = _rope(bk, cos64[:, :128], sin64[:, :128], HEAD_DIM // 2, first64[:, :128]).astype(BF16)
    qc_ref[...] = (_rope(cq_ref[...].astype(F32), cos64, sin64, HEAD_DIM // 2, first64) * scale).astype(BF16)
    kc_ref[...] = _rope(ck_ref[...].astype(F32), cos64[:, :128], sin64[:, :128], HEAD_DIM // 2,
                        first64[:, :128]).astype(BF16)

    cosm, sinm = cosm_ref[...], sinm_ref[...]
    lane_m = lax.broadcasted_iota(jnp.int32, (1, MLA_HEADS * MLA_PAD), 1) % MLA_PAD
    first_m = lane_m < (MLA_NOPE + MLA_ROPE // 2)
    dqa = dqa_ref[...].astype(F32)
    qn = dqa * lax.rsqrt(jnp.mean(dqa * dqa, axis=-1, keepdims=True) + NORM_EPS) * qn_ref[...]
    dq = _dot(qn.astype(BF16), wqb_ref[...])
    qd_ref[...] = _rope(dq, cosm, sinm, MLA_ROPE // 2, first_m).astype(BF16)

    dkva = dkva_ref[...]
    kvc = dkva[:, :MLA_KV_RANK].astype(F32)
    kvn = (kvc * lax.rsqrt(jnp.mean(kvc * kvc, axis=-1, keepdims=True) + NORM_EPS) * kvn_ref[...]).astype(BF16)
    dk = _dot(kvn, wk_ref[...]) + _dot(dkva, pe_ref[...])
    kd_ref[...] = _rope(dk, cosm, sinm, MLA_ROPE // 2, first_m).astype(BF16)
    vd_ref[...] = _dot(kvn, wv_ref[...]).astype(BF16)


def _rope_row_block(i):
    n_ctx = CTX_ROWS // ROW_TILE
    return jnp.where(i < n_ctx, 0, 1 + (i - n_ctx) % (SEQ // ROW_TILE))


def _prep_call(p, tabs, lw):
    t = p.shape[0]
    mw = MLA_HEADS * MLA_PAD

    def pcol(width, col):
        return pl.BlockSpec((ROW_TILE, width), lambda i: (i, col // width))

    def full(shape):
        return pl.BlockSpec(shape, lambda i: (0,) * len(shape))

    def rows(width):
        return pl.BlockSpec((ROW_TILE, width), lambda i: (i, 0))

    def tab(width):
        return pl.BlockSpec((ROW_TILE, width), lambda i: (_rope_row_block(i), 0))

    outs = [(256, BF16), (128, BF16), (256, BF16), (128, BF16), (mw, BF16), (mw, BF16), (256, BF16)]
    return pl.pallas_call(
        _prep_kernel,
        grid=(t // ROW_TILE,),
        in_specs=[
            pcol(256, COL_BQ), pcol(128, COL_BK), pcol(256, COL_CQ), pcol(128, COL_CK),
            pcol(256, COL_DQA), pcol(256, COL_DKVA),
            tab(256), tab(256), tab(mw), tab(mw),
            full((1, 256)), full((1, 128)), full((256, 256)), full((1, MLA_Q_RANK)), full((1, MLA_KV_RANK)),
            full((MLA_Q_RANK, mw)), full((MLA_KV_RANK, mw)), full((MLA_KV_RANK, 256)), full((256, mw)),
        ],
        out_specs=[rows(w) for w, _ in outs],
        out_shape=[jax.ShapeDtypeStruct((t, w), dt) for w, dt in outs],
        compiler_params=_params("arbitrary"),
        name="mixer_prep",
    )(p, p, p, p, p, p, tabs["cos64"], tabs["sin64"], tabs["cosm"], tabs["sinm"],
      lw["gq"], lw["gk"], tabs["blockdiag"], lw["qn"], lw["kvn"], lw["wqb"], lw["wk"], lw["wv"], tabs["pe_place"])


def _attend(q, segs, sink=None, scale=None):
    scores = []
    for k, _, bias in segs:
        s = _dot_nt(q, k)
        if scale is not None:
            s = s * scale
        if bias is not None:
            s = s + bias
        scores.append(s)
    m = functools.reduce(jnp.maximum, [jnp.max(s, axis=-1, keepdims=True) for s in scores])
    if sink is not None:
        m = jnp.maximum(m, sink)
    denom = None
    out = None
    for s, (_, v, _) in zip(scores, segs):
        p = jnp.exp(s - m)
        ps = jnp.sum(p, axis=-1, keepdims=True)
        pv = _dot(p.astype(BF16), v)
        denom = ps if denom is None else denom + ps
        out = pv if out is None else out + pv
    if sink is not None:
        denom = denom + jnp.exp(sink - m)
    return out / denom


def _gqa_heads(q_ref, o_ref, sink_ref, n_kv, group, dq, dv, scale, seg_fn):
    tq = q_ref.shape[0]
    for h in range(n_kv):
        heads = [h * group + g for g in range(group)]
        q = jnp.concatenate([q_ref[:, a * dq:(a + 1) * dq] for a in heads], axis=0) if group > 1 \
            else q_ref[:, h * dq:(h + 1) * dq]
        sink = None
        if sink_ref is not None:
            sink = jnp.concatenate([jnp.full((tq, 1), sink_ref[a], F32) for a in heads], axis=0)
        o = _attend(q, seg_fn(h), sink=sink, scale=scale)
        for g, a in enumerate(heads):
            o_ref[:, a * dv:(a + 1) * dv] = o[g * tq:(g + 1) * tq].astype(BF16)


def _dense_attn_kernel(*refs, n_kv, group, dq, dv, scale, has_sink, ctx_tile, only_ctx_keys):
    if has_sink:
        sink_ref, refs = refs[0], refs[1:]
    else:
        sink_ref = None
    q_ref, kl_ref, vl_ref, kc_ref, vc_ref, o_ref = refs

    def run(with_latent):
        def segs(h):
            out = []
            if with_latent:
                out.append((kl_ref[:, h * dq:(h + 1) * dq], vl_ref[:, h * dv:(h + 1) * dv], None))
            out.append((kc_ref[:, h * dq:(h + 1) * dq], vc_ref[:, h * dv:(h + 1) * dv], None))
            return out
        _gqa_heads(q_ref, o_ref, sink_ref, n_kv, group, dq, dv, scale, segs)

    if only_ctx_keys:
        run(False)
    elif ctx_tile:
        j = pl.program_id(1)
        pl.when(j == 0)(lambda: run(False))
        pl.when(j > 0)(lambda: run(True))
    else:
        run(True)


def _q_row_block(b, j, n_batch, with_ctx):
    per_batch = SEQ // Q_TILE
    lat0 = CTX_ROWS // Q_TILE
    if with_ctx:
        return jnp.where(j == 0, b, lat0 + b * per_batch + j - 1)
    return lat0 + b * per_batch + j


def _attn_specs(n_batch, with_ctx, q, k, v, o_width, order_bj=True):
    def ix(f):
        return (lambda b, j: f(b, j)) if order_bj else (lambda j, b: f(b, j))

    def qspec(width, col):
        return pl.BlockSpec((Q_TILE, width), ix(lambda b, j: (_q_row_block(b, j, n_batch, with_ctx), col // width)))

    def lat(width, col):
        return pl.BlockSpec((SEQ, width), ix(lambda b, j: (CTX_ROWS // SEQ + b, col // width)))

    def ctx(width, col):
        return pl.BlockSpec((CTX_LEN, width), ix(lambda b, j: (b, col // width)))

    in_specs = [qspec(q[1], q[2]), lat(k[1], k[2]), lat(v[1], v[2]), ctx(k[1], k[2]), ctx(v[1], v[2])]
    args = [q[0], k[0], v[0], k[0], v[0]]
    return in_specs, args, qspec(o_width, 0)


def _dense_attn_call(n_batch, with_ctx, q, k, v, *, n_kv, group, dq, dv, scale=None, sink=None, name):
    t = q[0].shape[0]
    in_specs, args, out_spec = _attn_specs(n_batch, with_ctx, q, k, v, 256)
    if sink is not None:
        in_specs = [pl.BlockSpec(memory_space=pltpu.SMEM)] + in_specs
        args = [sink] + args
    kern = functools.partial(_dense_attn_kernel, n_kv=n_kv, group=group, dq=dq, dv=dv, scale=scale,
                             has_sink=sink is not None, ctx_tile=with_ctx, only_ctx_keys=False)
    return pl.pallas_call(
        kern,
        grid=(n_batch, SEQ // Q_TILE + (1 if with_ctx else 0)),
        in_specs=in_specs,
        out_specs=out_spec,
        out_shape=jax.ShapeDtypeStruct((t, 256), BF16),
        compiler_params=_params("arbitrary", "arbitrary"),
        name=name,
    )(*args)


def _window_attn_kernel(sink_ref, q_ref, kl_ref, vl_ref, kc_ref, vc_ref, o_ref, *, ctx_tile):
    j = pl.program_id(1)
    hd = HEAD_DIM
    group = WC_Q_HEADS // WC_KV_HEADS

    def ctx_run():
        def segs(h):
            return [(kc_ref[:, h * hd:(h + 1) * hd], vc_ref[:, h * hd:(h + 1) * hd], None)]
        _gqa_heads(q_ref, o_ref, sink_ref, WC_KV_HEADS, group, hd, hd, None, segs)

    def lat_run():
        i = j - 1 if ctx_tile else j
        start = pl.multiple_of(jnp.clip(i * Q_TILE - WC_WINDOW, 0, SEQ - WC_WIN), WC_WINDOW)
        qpos = i * Q_TILE + lax.broadcasted_iota(jnp.int32, (group * Q_TILE, WC_WIN), 0) % Q_TILE
        kpos = start + lax.broadcasted_iota(jnp.int32, (group * Q_TILE, WC_WIN), 1)
        bias = jnp.where(jnp.abs(qpos - kpos) <= WC_WINDOW, 0.0, NEG_INF).astype(F32)

        def segs(h):
            return [(kl_ref[pl.ds(start, WC_WIN), h * hd:(h + 1) * hd],
                     vl_ref[pl.ds(start, WC_WIN), h * hd:(h + 1) * hd], bias),
                    (kc_ref[:, h * hd:(h + 1) * hd], vc_ref[:, h * hd:(h + 1) * hd], None)]
        _gqa_heads(q_ref, o_ref, sink_ref, WC_KV_HEADS, group, hd, hd, None, segs)

    if ctx_tile:
        pl.when(j == 0)(ctx_run)
        pl.when(j > 0)(lat_run)
    else:
        lat_run()


def _window_attn_call(n_batch, with_ctx, q, k, v, sink):
    t = q[0].shape[0]
    in_specs, args, out_spec = _attn_specs(n_batch, with_ctx, q, k, v, 256)
    return pl.pallas_call(
        functools.partial(_window_attn_kernel, ctx_tile=with_ctx),
        grid=(n_batch, SEQ // Q_TILE + (1 if with_ctx else 0)),
        in_specs=[pl.BlockSpec(memory_space=pltpu.SMEM)] + in_specs,
        out_specs=out_spec,
        out_shape=jax.ShapeDtypeStruct((t, 256), BF16),
        compiler_params=_params("arbitrary", "arbitrary"),
        name="attn_window",
    )(sink, *args)


def _natten_kernel(q_ref, kl_ref, vl_ref, kc_ref, vc_ref, bias_ref, o_ref, *, ctx_tile):
    j = pl.program_id(0)
    hd = HEAD_DIM
    scale = HEAD_DIM ** -0.5

    def ctx_run():
        for h in range(NA_HEADS):
            sl = slice(h * hd, (h + 1) * hd)
            o = _attend(q_ref[:, sl] * scale, [(kc_ref[:, sl], vc_ref[:, sl], None)])
            o_ref[:, sl] = o.astype(BF16)

    def lat_run():
        i = j - 1 if ctx_tile else j
        ws = jnp.clip(NA_Q_ROWS * i - NA_KH // 2, 0, GRID_ROWS - NA_WIN_ROWS)
        start = pl.multiple_of(ws * GRID_W, GRID_W)
        for h in range(NA_HEADS):
            sl = slice(h * hd, (h + 1) * hd)
            segs = [(kl_ref[pl.ds(start, NA_WIN), sl], vl_ref[pl.ds(start, NA_WIN), sl], bias_ref[0, h]),
                    (kc_ref[:, sl], vc_ref[:, sl], None)]
            o_ref[:, sl] = _attend(q_ref[:, sl] * scale, segs).astype(BF16)

    if ctx_tile:
        pl.when(j == 0)(ctx_run)
        pl.when(j > 0)(lat_run)
    else:
        lat_run()


_NA_VARIANT = (0, 1, 2, 2, 2, 2, 3, 4)
_NA_VARIANT_TILE = (0, 1, 2, 6, 7)


def _natten_call(n_batch, with_ctx, p, bias_tab):
    t = p.shape[0]
    q, k, v = (p, 256, COL_AQ), (p, 256, COL_AK), (p, 256, COL_AV)
    in_specs, args, out_spec = _attn_specs(n_batch, with_ctx, q, k, v, 256, order_bj=False)

    def bias_ix(j, b):
        i = jnp.maximum(j - 1, 0) if with_ctx else j
        v_ix = sum(jnp.where(i == n, _NA_VARIANT[n], 0) for n in range(len(_NA_VARIANT)))
        return (v_ix, 0, 0, 0)

    in_specs.append(pl.BlockSpec((1, NA_HEADS, Q_TILE, NA_WIN), bias_ix))
    return pl.pallas_call(
        functools.partial(_natten_kernel, ctx_tile=with_ctx),
        grid=(SEQ // Q_TILE + (1 if with_ctx else 0), n_batch),
        in_specs=in_specs,
        out_specs=out_spec,
        out_shape=jax.ShapeDtypeStruct((t, 256), BF16),
        compiler_params=_params("arbitrary", "arbitrary"),
        name="attn_natten",
    )(*args, bias_tab)


def _natten_bias_table(rpb):
    tiles = np.asarray(_NA_VARIANT_TILE)
    a = np.arange(Q_TILE) // GRID_W
    qc = np.arange(Q_TILE) % GRID_W
    w = np.arange(NA_WIN) // GRID_W
    kc = np.arange(NA_WIN) % GRID_W
    r = NA_Q_ROWS * tiles[:, None] + a[None, :]
    ws = np.clip(NA_Q_ROWS * tiles - NA_KH // 2, 0, GRID_ROWS - NA_WIN_ROWS)
    kr = ws[:, None] + w[None, :]
    krow0 = np.clip(r - NA_KH // 2, 0, GRID_ROWS - NA_KH)
    row_ok = (kr[:, None, :] >= krow0[:, :, None]) & (kr[:, None, :] < krow0[:, :, None] + NA_KH)
    cstart = np.clip(qc - NA_KW // 2, 0, GRID_W - NA_KW)
    col_ok = (kc[None, :] >= cstart[:, None]) & (kc[None, :] < cstart[:, None] + NA_KW)
    ok = row_ok & col_ok[None]
    dr = np.clip(kr[:, None, :] - r[:, :, None] + NA_KH - 1, 0, 2 * NA_KH - 2)
    dc = np.clip(kc[None, :] - qc[:, None] + NA_KW - 1, 0, 2 * NA_KW - 2)
    dc = np.broadcast_to(dc[None], dr.shape)
    bias = rpb.astype(F32)[:, dr, dc]
    bias = jnp.where(ok[None], bias, NEG_INF)
    return jnp.transpose(bias, (1, 0, 2, 3))


def _merge_kernel(*refs, with_router):
    if with_router:
        (oa_ref, ob_ref, oc_ref, od_ref, gate_ref, x_ref, mod_ref, g2_ref, wb_ref, wo_ref, rt_ref,
         xo_ref, h2_ref, comb_ref) = refs
    else:
        (oa_ref, ob_ref, oc_ref, od_ref, gate_ref, x_ref, mod_ref, g2_ref, wb_ref, wo_ref,
         xo_ref, h2_ref) = refs
    m = mod_ref[0]
    acc = None
    for n, o_ref in enumerate((oa_ref, ob_ref, oc_ref, od_ref)):
        y = _dot(o_ref[...], wb_ref[n])
        g = _sigmoid(gate_ref[:, n * D_MODEL:(n + 1) * D_MODEL].astype(F32))
        acc = g * y if acc is None else acc + g * y
    mix = _dot(acc.astype(BF16), wo_ref[...])
    x = x_ref[...] + m[2:3] * mix
    xo_ref[...] = x
    h2 = _norm_mod(x, g2_ref[...], m[3:4], m[4:5])
    h2_ref[...] = h2.astype(BF16)
    if with_router:
        hh, hm, hl = _split3(h2)
        rh, rm, rl = rt_ref[0], rt_ref[1], rt_ref[2]
        logits = (_dot(hh, rh) + _dot(hh, rm) + _dot(hm, rh)) + (_dot(hh, rl) + _dot(hm, rm) + _dot(hl, rh))
        lane = lax.broadcasted_iota(jnp.int32, logits.shape, 1).astype(F32)
        logits = jnp.where(lane < N_EXPERTS, logits, NEG_INF)
        m1 = jnp.max(logits, axis=-1, keepdims=True)
        i1 = jnp.min(jnp.where(logits == m1, lane, float(LANES)), axis=-1, keepdims=True)
        rest = jnp.where(lane == i1, NEG_INF, logits)
        m2 = jnp.max(rest, axis=-1, keepdims=True)
        i2 = jnp.min(jnp.where(rest == m2, lane, float(LANES)), axis=-1, keepdims=True)
        e = jnp.exp(m2 - m1)
        w1 = 1.0 / (1.0 + e)
        w2 = e / (1.0 + e)
        comb_ref[...] = jnp.where(lane == i1, w1, 0.0) + jnp.where(lane == i2, w2, 0.0)


def _merge_call(outs, gate, x, mod_l, g2, wb, wo, router3, first_tile):
    t = x.shape[0]
    n_rows = t - first_tile * ROW_TILE

    def rows(width, col=0):
        return pl.BlockSpec((ROW_TILE, width), lambda i: (i + first_tile, col // width))

    def orow(width):
        return pl.BlockSpec((ROW_TILE, width), lambda i: (i, 0))

    def full(shape):
        return pl.BlockSpec(shape, lambda i: (0,) * len(shape))

    in_specs = [rows(256)] * 4 + [
        rows(GATE_W),
        rows(D_MODEL),
        pl.BlockSpec((1, ADA_CHUNKS, D_MODEL), lambda i: (_mod_row(i, first_tile), 0, 0)),
        full((1, D_MODEL)), full((N_BRANCH, BRANCH_W, D_MODEL)), full((D_MODEL, D_MODEL)),
    ]
    args = list(outs) + [gate, x, mod_l, g2, wb, wo]
    out_specs = [orow(D_MODEL), orow(D_MODEL)]
    out_shape = [jax.ShapeDtypeStruct((n_rows, D_MODEL), F32), jax.ShapeDtypeStruct((n_rows, D_MODEL), BF16)]
    if router3 is not None:
        in_specs.append(full((3, D_MODEL, LANES)))
        args.append(router3)
        out_specs.append(orow(LANES))
        out_shape.append(jax.ShapeDtypeStruct((n_rows, LANES), F32))
    return pl.pallas_call(
        functools.partial(_merge_kernel, with_router=router3 is not None),
        grid=(n_rows // ROW_TILE,),
        in_specs=in_specs,
        out_specs=out_specs,
        out_shape=out_shape,
        compiler_params=_params("arbitrary"),
        name="merge",
    )(*args)


FFN_ROWS = 1024


def _ffn_kernel(*refs, n_experts, final_norm):
    if n_experts > 1:
        h_ref, comb_ref, w1_ref, w3_ref, w2_ref, x_ref, mod_ref, fg_ref, o_ref, acc_ref = refs
    else:
        h_ref, w1_ref, w3_ref, w2_ref, x_ref, mod_ref, fg_ref, o_ref, acc_ref = refs
    e = pl.program_id(1)
    f = pl.program_id(2)

    @pl.when((e == 0) & (f == 0))
    def _():
        acc_ref[...] = jnp.zeros_like(acc_ref)

    h = h_ref[...]
    a = _dot(h, w1_ref[0])
    b = _dot(h, w3_ref[0])
    act = (a * _sigmoid(a) * b).astype(BF16)
    y = _dot(act, w2_ref[0])
    if n_experts > 1:
        comb = comb_ref[...]
        lane = lax.broadcasted_iota(jnp.int32, comb.shape, 1)
        y = y * jnp.sum(jnp.where(lane == e, comb, 0.0), axis=-1, keepdims=True)
    acc_ref[...] += y

    @pl.when((e == n_experts - 1) & (f == pl.num_programs(2) - 1))
    def _():
        m = mod_ref[0]
        x = x_ref[...] + m[5:6] * acc_ref[...]
        if final_norm:
            ms = jnp.mean(x * x, axis=-1, keepdims=True)
            x = x * lax.rsqrt(ms + NORM_EPS) * fg_ref[...]
        o_ref[...] = x


def _ffn_mod_row(i, first_row_tile):
    n_ctx = CTX_ROWS // FFN_ROWS
    t = i + first_row_tile
    return jnp.where(t < n_ctx, 8, (t - n_ctx) // (SEQ // FFN_ROWS))


def _ffn_call(h2, comb, w1, w3, w2, x, mod_l, final_g, first_row_tile, final_norm):
    t = x.shape[0]
    n_experts = w1.shape[0]
    rows = lambda width: pl.BlockSpec((FFN_ROWS, width), lambda i, e, f: (i, 0))
    in_specs = [rows(D_MODEL)]
    args = [h2]
    if n_experts > 1:
        in_specs.append(rows(LANES))
        args.append(comb)
    in_specs += [
        pl.BlockSpec((1, D_MODEL, FF_CHUNK), lambda i, e, f: (e, 0, f)),
        pl.BlockSpec((1, D_MODEL, FF_CHUNK), lambda i, e, f: (e, 0, f)),
        pl.BlockSpec((1, FF_CHUNK, D_MODEL), lambda i, e, f: (e, f, 0)),
        rows(D_MODEL),
        pl.BlockSpec((1, ADA_CHUNKS, D_MODEL), lambda i, e, f: (_ffn_mod_row(i, first_row_tile), 0, 0)),
        pl.BlockSpec((1, D_MODEL), lambda i, e, f: (0, 0)),
    ]
    args += [w1, w3, w2, x, mod_l, final_g]
    return pl.pallas_call(
        functools.partial(_ffn_kernel, n_experts=n_experts, final_norm=final_norm),
        grid=(t // FFN_ROWS, n_experts, D_FF // FF_CHUNK),
        in_specs=in_specs,
        out_specs=rows(D_MODEL),
        out_shape=jax.ShapeDtypeStruct((t, D_MODEL), F32),
        scratch_shapes=[pltpu.VMEM((FFN_ROWS, D_MODEL), F32)],
        compiler_params=_params("arbitrary", "arbitrary", "arbitrary"),
        name="ffn",
    )(*args)


def _rope_tables():
    t = np.arange(SEQ)

    def angles(rot_dim):
        half = rot_dim // 2
        inv = ROPE_THETA ** (-jnp.arange(0, half, 2, dtype=F32) / half)
        ang = jnp.concatenate([jnp.asarray(t // GRID_W, F32)[:, None] * inv[None, :],
                               jnp.asarray(t % GRID_W, F32)[:, None] * inv[None, :]], axis=-1)
        return jnp.cos(ang), jnp.sin(ang)

    def pad_rows(a, fill):
        return jnp.concatenate([jnp.full((ROW_TILE, a.shape[1]), fill, F32), a], axis=0)

    c, s = angles(HEAD_DIM)
    cos64 = jnp.tile(jnp.concatenate([c, c], axis=-1), (1, 4))
    sin64 = jnp.tile(jnp.concatenate([s, s], axis=-1), (1, 4))
    c, s = angles(MLA_ROPE)
    one = jnp.ones((SEQ, MLA_NOPE), F32)
    zero = jnp.zeros((SEQ, MLA_NOPE), F32)
    tail1 = jnp.ones((SEQ, MLA_PAD - MLA_NOPE - MLA_ROPE), F32)
    tail0 = jnp.zeros((SEQ, MLA_PAD - MLA_NOPE - MLA_ROPE), F32)
    cosm = jnp.tile(jnp.concatenate([one, c, c, tail1], axis=-1), (1, MLA_HEADS))
    sinm = jnp.tile(jnp.concatenate([zero, s, s, tail0], axis=-1), (1, MLA_HEADS))
    blockdiag = jnp.asarray(np.kron(np.eye(4), np.ones((HEAD_DIM, HEAD_DIM))), BF16)
    place = np.zeros((256, MLA_HEADS * MLA_PAD), np.float32)
    for h in range(MLA_HEADS):
        for r in range(MLA_ROPE):
            place[MLA_KV_RANK + r, h * MLA_PAD + MLA_NOPE + r] = 1.0
    return {
        "cos64": pad_rows(cos64, 1.0), "sin64": pad_rows(sin64, 0.0),
        "cosm": pad_rows(cosm, 1.0), "sinm": pad_rows(sinm, 0.0),
        "blockdiag": blockdiag, "pe_place": jnp.asarray(place, BF16),
    }


def _pack_w_in(w):
    pad = jnp.zeros((D_MODEL, COL_GATE - (COL_DKVA + MLA_KV_RANK + MLA_ROPE)), w.dtype)
    n_qkv = COL_DKVA + MLA_KV_RANK + MLA_ROPE
    return jnp.concatenate([w[:, :n_qkv], pad, w[:, n_qkv:]], axis=1).astype(BF16)


def _pack_mla(wqb, wkvb):
    dqh = MLA_NOPE + MLA_ROPE
    q = wqb.reshape(MLA_Q_RANK, MLA_HEADS, dqh)
    q = jnp.pad(q, ((0, 0), (0, 0), (0, MLA_PAD - dqh))).reshape(MLA_Q_RANK, MLA_HEADS * MLA_PAD)
    kv = wkvb.reshape(MLA_KV_RANK, MLA_HEADS, MLA_NOPE + MLA_V)
    k = jnp.pad(kv[:, :, :MLA_NOPE], ((0, 0), (0, 0), (0, MLA_PAD - MLA_NOPE)))
    k = k.reshape(MLA_KV_RANK, MLA_HEADS * MLA_PAD)
    v = kv[:, :, MLA_NOPE:].reshape(MLA_KV_RANK, MLA_HEADS * MLA_V)
    return q.astype(BF16), k.astype(BF16), v.astype(BF16)


def _split3_host(w):
    hi = w.astype(BF16)
    r1 = w - hi.astype(F32)
    mid = r1.astype(BF16)
    lo = (r1 - mid.astype(F32)).astype(BF16)
    return jnp.stack([hi, mid, lo])


def kernel(x, c, ctx, c_ctx, norm1_g, norm2_g, w_ada, b_ada, w_in, na_rpb, gb_qnorm, gb_knorm, wc_sink,
           mla_qnorm, mla_kvnorm, mla_wqb, mla_wkvb, w_branch, w_out, ffn_w1, ffn_w3, ffn_w2,
           moe_router, moe_w1, moe_w3, moe_w2, final_g):
    n_batch = x.shape[0]
    assert x.shape[1:] == (SEQ, D_MODEL) and ctx.shape[1:] == (CTX_LEN, D_MODEL)
    assert n_batch * CTX_LEN <= CTX_ROWS and n_batch <= 8

    ctx_rows = ctx.reshape(n_batch * CTX_LEN, D_MODEL)
    if ctx_rows.shape[0] < CTX_ROWS:
        ctx_rows = jnp.pad(ctx_rows, ((0, CTX_ROWS - ctx_rows.shape[0]), (0, 0)))
    xt = jnp.concatenate([ctx_rows, x.reshape(n_batch * SEQ, D_MODEL)], axis=0)

    cvec = jnp.zeros((16, D_MODEL), F32).at[:n_batch].set(c).at[8].set(c_ctx)
    mod = _mod_call(cvec, w_ada, b_ada).reshape(DEPTH, 16, ADA_CHUNKS, D_MODEL)
    tabs = _rope_tables()
    mla_scale = (MLA_NOPE + MLA_ROPE) ** -0.5
    lat_tile0 = CTX_ROWS // ROW_TILE

    for l in range(DEPTH):
        with_ctx = l < DEPTH - 1
        wq, wk, wv = _pack_mla(mla_wqb[l], mla_wkvb[l])
        lw = {
            "gq": jnp.tile(gb_qnorm[l], 4)[None, :], "gk": jnp.tile(gb_knorm[l], 2)[None, :],
            "qn": mla_qnorm[l][None, :], "kvn": mla_kvnorm[l][None, :], "wqb": wq, "wk": wk, "wv": wv,
        }
        p, gate = _inproj_call(xt, mod[l], norm1_g[l][None, :], _pack_w_in(w_in[l]))
        qb, kb, qc, kc, qd, kd, vd = _prep_call(p, tabs, lw)

        oa = _natten_call(n_batch, with_ctx, p, _natten_bias_table(na_rpb[l]))
        ob = _dense_attn_call(n_batch, with_ctx, (qb, 256, 0), (kb, 128, 0), (p, 128, COL_BV),
                              n_kv=GB_KV_HEADS, group=GB_Q_HEADS // GB_KV_HEADS, dq=HEAD_DIM, dv=HEAD_DIM,
                              name="attn_global")
        oc = _window_attn_call(n_batch, with_ctx, (qc, 256, 0), (kc, 128, 0), (p, 128, COL_CV), wc_sink[l])
        od = _dense_attn_call(n_batch, with_ctx, (qd, MLA_HEADS * MLA_PAD, 0), (kd, MLA_HEADS * MLA_PAD, 0),
                              (vd, 256, 0), n_kv=MLA_HEADS, group=1, dq=MLA_PAD, dv=MLA_V, scale=mla_scale,
                              name="attn_mla")

        is_moe = l % 2 == 1
        router3 = None
        if is_moe:
            router3 = _split3_host(jnp.pad(moe_router[l // 2], ((0, 0), (0, LANES - N_EXPERTS))))
        first_tile = 0 if with_ctx else lat_tile0
        res = _merge_call((oa, ob, oc, od), gate, xt, mod[l], norm2_g[l][None, :],
                          w_branch[l].astype(BF16), w_out[l].astype(BF16), router3, first_tile)
        x_mid, h2 = res[0], res[1]
        comb = res[2] if is_moe else None
        if is_moe:
            w1, w3, w2 = (w[l // 2].astype(BF16) for w in (moe_w1, moe_w3, moe_w2))
        else:
            w1, w3, w2 = (w[l // 2][None].astype(BF16) for w in (ffn_w1, ffn_w3, ffn_w2))
        first_ffn_tile = 0 if with_ctx else CTX_ROWS // FFN_ROWS
        xt = _ffn_call(h2, comb, w1, w3, w2, x_mid, mod[l], final_g[None, :], first_ffn_tile,
                       final_norm=(l == DEPTH - 1))

    return xt.reshape(n_batch, SEQ, D_MODEL)
```

```python
import functools

import numpy as np
import jax
import jax.numpy as jnp
from jax import lax
from jax.experimental import pallas as pl
from jax.experimental.pallas import tpu as pltpu

F32 = jnp.float32
BF16 = jnp.bfloat16

D_MODEL = 1024
SEQ = 2048
DEPTH = 4
CTX_LEN = 256
GRID_W = 64
GRID_ROWS = SEQ // GRID_W
HEAD_DIM = 64
ROPE_THETA = 10000.0
NORM_EPS = 1e-6
NEG_INF = -1e30

NA_HEADS = 4
NA_KH = 8
NA_KW = 16
GB_Q_HEADS = 4
GB_KV_HEADS = 2
WC_Q_HEADS = 4
WC_KV_HEADS = 2
WC_WINDOW = 128
MLA_HEADS = 4
MLA_Q_RANK = 256
MLA_KV_RANK = 128
MLA_NOPE = 64
MLA_ROPE = 32
MLA_V = 64
MLA_PAD = 128
N_BRANCH = 4
BRANCH_W = 256
D_FF = 3584
N_EXPERTS = 8
ADA_CHUNKS = 6

VMEM_LIMIT_BYTES = 56 * 1024 * 1024
LANES = 128

CTX_ROWS = 2048
ROW_TILE = 512
Q_TILE = 256
NA_Q_ROWS = Q_TILE // GRID_W
NA_WIN_ROWS = 12
NA_WIN = NA_WIN_ROWS * GRID_W
WC_WIN = Q_TILE + 2 * WC_WINDOW

COL_AQ, COL_AK, COL_AV = 0, 256, 512
COL_BQ, COL_BK, COL_BV = 768, 1024, 1152
COL_CQ, COL_CK, COL_CV = 1280, 1536, 1664
COL_DQA, COL_DKVA, COL_GATE = 1792, 2048, 2304
GATE_W = N_BRANCH * D_MODEL
P_WIDTH = COL_GATE + GATE_W
QKV_CHUNK = COL_GATE // 2
FF_CHUNK = 512


def _params(*sem):
    return pltpu.CompilerParams(dimension_semantics=sem, vmem_limit_bytes=VMEM_LIMIT_BYTES)


def _dot(a, b):
    return jnp.dot(a, b, preferred_element_type=F32)


def _dot_nt(a, b):
    return lax.dot_general(a, b, (((1,), (1,)), ((), ())), preferred_element_type=F32)


def _split3(x):
    hi = x.astype(BF16)
    r1 = x - hi.astype(F32)
    mid = r1.astype(BF16)
    lo = (r1 - mid.astype(F32)).astype(BF16)
    return hi, mid, lo


def _sigmoid(x):
    return 1.0 / (1.0 + jnp.exp(-x))


ADA_TN = 1536


def _mod_kernel(c_ref, w_ref, b_ref, o_ref):
    c = c_ref[...]
    sc = (c * _sigmoid(c)).astype(BF16)
    o_ref[0] = _dot(sc, w_ref[0].astype(BF16)) + b_ref[0]


def _mod_call(cvec, w_ada, b_ada):
    n = ADA_CHUNKS * D_MODEL
    return pl.pallas_call(
        _mod_kernel,
        grid=(DEPTH, n // ADA_TN),
        in_specs=[
            pl.BlockSpec((16, D_MODEL), lambda l, j: (0, 0)),
            pl.BlockSpec((1, D_MODEL, ADA_TN), lambda l, j: (l, 0, j)),
            pl.BlockSpec((1, 1, ADA_TN), lambda l, j: (l, 0, j)),
        ],
        out_specs=pl.BlockSpec((1, 16, ADA_TN), lambda l, j: (l, 0, j)),
        out_shape=jax.ShapeDtypeStruct((DEPTH, 16, n), F32),
        compiler_params=_params("arbitrary", "arbitrary"),
        name="adaln_mod",
    )(cvec, w_ada, b_ada.reshape(DEPTH, 1, n))


def _mod_row(i, first_tile):
    n_ctx = CTX_ROWS // ROW_TILE
    t = i + first_tile
    return jnp.where(t < n_ctx, 8, (t - n_ctx) // (SEQ // ROW_TILE))


def _norm_mod(x, g, shift, scale):
    ms = jnp.mean(x * x, axis=-1, keepdims=True)
    y = x * lax.rsqrt(ms + NORM_EPS) * g
    return y * (1.0 + scale) + shift


def _inproj_kernel(x_ref, mod_ref, g_ref, w_ref, p_ref, gate_ref):
    m = mod_ref[0]
    hb = _norm_mod(x_ref[...], g_ref[...], m[0:1], m[1:2]).astype(BF16)
    for c0 in range(0, COL_GATE, QKV_CHUNK):
        p_ref[:, c0:c0 + QKV_CHUNK] = _dot(hb, w_ref[:, c0:c0 + QKV_CHUNK]).astype(BF16)
    for c0 in range(0, GATE_W, D_MODEL):
        gate_ref[:, c0:c0 + D_MODEL] = _dot(hb, w_ref[:, COL_GATE + c0:COL_GATE + c0 + D_MODEL]).astype(BF16)


def _inproj_call(x, mod_l, g, w_in_p):
    t = x.shape[0]
    return pl.pallas_call(
        _inproj_kernel,
        grid=(t // ROW_TILE,),
        in_specs=[
            pl.BlockSpec((ROW_TILE, D_MODEL), lambda i: (i, 0)),
            pl.BlockSpec((1, ADA_CHUNKS, D_MODEL), lambda i: (_mod_row(i, 0), 0, 0)),
            pl.BlockSpec((1, D_MODEL), lambda i: (0, 0)),
            pl.BlockSpec((D_MODEL, P_WIDTH), lambda i: (0, 0), pipeline_mode=pl.Buffered(1)),
        ],
        out_specs=[pl.BlockSpec((ROW_TILE, COL_GATE), lambda i: (i, 0)),
                   pl.BlockSpec((ROW_TILE, GATE_W), lambda i: (i, 0))],
        out_shape=[jax.ShapeDtypeStruct((t, COL_GATE), BF16), jax.ShapeDtypeStruct((t, GATE_W), BF16)],
        compiler_params=_params("arbitrary"),
        name="inproj",
    )(x, mod_l, g, w_in_p)


def _rope(x, cos, sin, half, first_mask):
    w = x.shape[-1]
    fwd = pltpu.roll(x, w - half, 1)
    bwd = pltpu.roll(x, half, 1)
    rot = jnp.where(first_mask, -fwd, bwd)
    return x * cos + rot * sin


def _head_rms(x, gain, blockdiag):
    hi, mid, lo = _split3(x * x)
    ss = _dot(hi, blockdiag) + _dot(mid, blockdiag) + _dot(lo, blockdiag)
    return x * lax.rsqrt(ss * (1.0 / HEAD_DIM) + NORM_EPS) * gain


def _prep_kernel(bq_ref, bk_ref, cq_ref, ck_ref, dqa_ref, dkva_ref,
                 cos64_ref, sin64_ref, cosm_ref, sinm_ref,
                 gq_ref, gk_ref, bd_ref, qn_ref, kvn_ref, wqb_ref, wk_ref, wv_ref, pe_ref,
                 qb_ref, kb_ref, qc_ref, kc_ref, qd_ref, kd_ref, vd_ref):
    scale = HEAD_DIM ** -0.5
    cos64, sin64 = cos64_ref[...], sin64_ref[...]
    lane = lax.broadcasted_iota(jnp.int32, (1, 256), 1)
    first64 = (lane % HEAD_DIM) < (HEAD_DIM // 2)
    bd = bd_ref[...]

    bq = _head_rms(bq_ref[...].astype(F32), gq_ref[...], bd)
    qb_ref[...] = (_rope(bq, cos64, sin64, HEAD_DIM // 2, first64) * scale).astype(BF16)
    bk = _head_rms(bk_ref[...].astype(F32), gk_ref[...], bd[:128, :128])
    kb_ref[...] = _rope(bk, cos64[:, :128], sin64[:, :128], HEAD_DIM // 2, first64[:, :128]).astype(BF16)
    qc_ref[...] = (_rope(cq_ref[...].astype(F32), cos64, sin64, HEAD_DIM // 2, first64) * scale).astype(BF16)
    kc_ref[...] = _rope(ck_ref[...].astype(F32), cos64[:, :128], sin64[:, :128], HEAD_DIM // 2,
                        first64[:, :128]).astype(BF16)

    cosm, sinm = cosm_ref[...], sinm_ref[...]
    lane_m = lax.broadcasted_iota(jnp.int32, (1, MLA_HEADS * MLA_PAD), 1) % MLA_PAD
    first_m = lane_m < (MLA_NOPE + MLA_ROPE // 2)
    dqa = dqa_ref[...].astype(F32)
    qn = dqa * lax.rsqrt(jnp.mean(dqa * dqa, axis=-1, keepdims=True) + NORM_EPS) * qn_ref[...]
    dq = _dot(qn.astype(BF16), wqb_ref[...])
    qd_ref[...] = _rope(dq, cosm, sinm, MLA_ROPE // 2, first_m).astype(BF16)

    dkva = dkva_ref[...]
    kvc = dkva[:, :MLA_KV_RANK].astype(F32)
    kvn = (kvc * lax.rsqrt(jnp.mean(kvc * kvc, axis=-1, keepdims=True) + NORM_EPS) * kvn_ref[...]).astype(BF16)
    dk = _dot(kvn, wk_ref[...]) + _dot(dkva, pe_ref[...])
    kd_ref[...] = _rope(dk, cosm, sinm, MLA_ROPE // 2, first_m).astype(BF16)
    vd_ref[...] = _dot(kvn, wv_ref[...]).astype(BF16)


def _rope_row_block(i):
    n_ctx = CTX_ROWS // ROW_TILE
    return jnp.where(i < n_ctx, 0, 1 + (i - n_ctx) % (SEQ // ROW_TILE))


def _prep_call(p, tabs, lw):
    t = p.shape[0]
    mw = MLA_HEADS * MLA_PAD

    def pcol(width, col):
        return pl.BlockSpec((ROW_TILE, width), lambda i: (i, col // width))

    def full(shape):
        return pl.BlockSpec(shape, lambda i: (0,) * len(shape))

    def rows(width):
        return pl.BlockSpec((ROW_TILE, width), lambda i: (i, 0))

    def tab(width):
        return pl.BlockSpec((ROW_TILE, width), lambda i: (_rope_row_block(i), 0))

    outs = [(256, BF16), (128, BF16), (256, BF16), (128, BF16), (mw, BF16), (mw, BF16), (256, BF16)]
    return pl.pallas_call(
        _prep_kernel,
        grid=(t // ROW_TILE,),
        in_specs=[
            pcol(256, COL_BQ), pcol(128, COL_BK), pcol(256, COL_CQ), pcol(128, COL_CK),
            pcol(256, COL_DQA), pcol(256, COL_DKVA),
            tab(256), tab(256), tab(mw), tab(mw),
            full((1, 256)), full((1, 128)), full((256, 256)), full((1, MLA_Q_RANK)), full((1, MLA_KV_RANK)),
            full((MLA_Q_RANK, mw)), full((MLA_KV_RANK, mw)), full((MLA_KV_RANK, 256)), full((256, mw)),
        ],
        out_specs=[rows(w) for w, _ in outs],
        out_shape=[jax.ShapeDtypeStruct((t, w), dt) for w, dt in outs],
        compiler_params=_params("arbitrary"),
        name="mixer_prep",
    )(p, p, p, p, p, p, tabs["cos64"], tabs["sin64"], tabs["cosm"], tabs["sinm"],
      lw["gq"], lw["gk"], tabs["blockdiag"], lw["qn"], lw["kvn"], lw["wqb"], lw["wk"], lw["wv"], tabs["pe_place"])


def _attend(q, segs, sink=None, scale=None):
    scores = []
    for k, _, bias in segs:
        s = _dot_nt(q, k)
        if scale is not None:
            s = s * scale
        if bias is not None:
            s = s + bias
        scores.append(s)
    m = functools.reduce(jnp.maximum, [jnp.max(s, axis=-1, keepdims=True) for s in scores])
    if sink is not None:
        m = jnp.maximum(m, sink)
    denom = None
    out = None
    for s, (_, v, _) in zip(scores, segs):
        p = jnp.exp(s - m)
        ps = jnp.sum(p, axis=-1, keepdims=True)
        pv = _dot(p.astype(BF16), v)
        denom = ps if denom is None else denom + ps
        out = pv if out is None else out + pv
    if sink is not None:
        denom = denom + jnp.exp(sink - m)
    return out / denom


def _gqa_heads(q_ref, o_ref, sink_ref, n_kv, group, dq, dv, scale, seg_fn):
    tq = q_ref.shape[0]
    for h in range(n_kv):
        heads = [h * group + g for g in range(group)]
        q = jnp.concatenate([q_ref[:, a * dq:(a + 1) * dq] for a in heads], axis=0) if group > 1 \
            else q_ref[:, h * dq:(h + 1) * dq]
        sink = None
        if sink_ref is not None:
            sink = jnp.concatenate([jnp.full((tq, 1), sink_ref[a], F32) for a in heads], axis=0)
        o = _attend(q, seg_fn(h), sink=sink, scale=scale)
        for g, a in enumerate(heads):
            o_ref[:, a * dv:(a + 1) * dv] = o[g * tq:(g + 1) * tq].astype(BF16)


def _dense_attn_kernel(*refs, n_kv, group, dq, dv, scale, has_sink, ctx_tile, only_ctx_keys):
    if has_sink:
        sink_ref, refs = refs[0], refs[1:]
    else:
        sink_ref = None
    q_ref, kl_ref, vl_ref, kc_ref, vc_ref, o_ref = refs

    def run(with_latent):
        def segs(h):
            out = []
            if with_latent:
                out.append((kl_ref[:, h * dq:(h + 1) * dq], vl_ref[:, h * dv:(h + 1) * dv], None))
            out.append((kc_ref[:, h * dq:(h + 1) * dq], vc_ref[:, h * dv:(h + 1) * dv], None))
            return out
        _gqa_heads(q_ref, o_ref, sink_ref, n_kv, group, dq, dv, scale, segs)

    if only_ctx_keys:
        run(False)
    elif ctx_tile:
        j = pl.program_id(1)
        pl.when(j == 0)(lambda: run(False))
        pl.when(j > 0)(lambda: run(True))
    else:
        run(True)


def _q_row_block(b, j, n_batch, with_ctx):
    per_batch = SEQ // Q_TILE
    lat0 = CTX_ROWS // Q_TILE
    if with_ctx:
        return jnp.where(j == 0, b, lat0 + b * per_batch + j - 1)
    return lat0 + b * per_batch + j


def _attn_specs(n_batch, with_ctx, q, k, v, o_width, order_bj=True):
    def ix(f):
        return (lambda b, j: f(b, j)) if order_bj else (lambda j, b: f(b, j))

    def qspec(width, col):
        return pl.BlockSpec((Q_TILE, width), ix(lambda b, j: (_q_row_block(b, j, n_batch, with_ctx), col // width)))

    def lat(width, col):
        return pl.BlockSpec((SEQ, width), ix(lambda b, j: (CTX_ROWS // SEQ + b, col // width)))

    def ctx(width, col):
        return pl.BlockSpec((CTX_LEN, width), ix(lambda b, j: (b, col // width)))

    in_specs = [qspec(q[1], q[2]), lat(k[1], k[2]), lat(v[1], v[2]), ctx(k[1], k[2]), ctx(v[1], v[2])]
    args = [q[0], k[0], v[0], k[0], v[0]]
    return in_specs, args, qspec(o_width, 0)


def _dense_attn_call(n_batch, with_ctx, q, k, v, *, n_kv, group, dq, dv, scale=None, sink=None, name):
    t = q[0].shape[0]
    in_specs, args, out_spec = _attn_specs(n_batch, with_ctx, q, k, v, 256)
    if sink is not None:
        in_specs = [pl.BlockSpec(memory_space=pltpu.SMEM)] + in_specs
        args = [sink] + args
    kern = functools.partial(_dense_attn_kernel, n_kv=n_kv, group=group, dq=dq, dv=dv, scale=scale,
                             has_sink=sink is not None, ctx_tile=with_ctx, only_ctx_keys=False)
    return pl.pallas_call(
        kern,
        grid=(n_batch, SEQ // Q_TILE + (1 if with_ctx else 0)),
        in_specs=in_specs,
        out_specs=out_spec,
        out_shape=jax.ShapeDtypeStruct((t, 256), BF16),
        compiler_params=_params("arbitrary", "arbitrary"),
        name=name,
    )(*args)


def _window_attn_kernel(sink_ref, q_ref, kl_ref, vl_ref, kc_ref, vc_ref, o_ref, *, ctx_tile):
    j = pl.program_id(1)
    hd = HEAD_DIM
    group = WC_Q_HEADS // WC_KV_HEADS

    def ctx_run():
        def segs(h):
            return [(kc_ref[:, h * hd:(h + 1) * hd], vc_ref[:, h * hd:(h + 1) * hd], None)]
        _gqa_heads(q_ref, o_ref, sink_ref, WC_KV_HEADS, group, hd, hd, None, segs)

    def lat_run():
        i = j - 1 if ctx_tile else j
        start = pl.multiple_of(jnp.clip(i * Q_TILE - WC_WINDOW, 0, SEQ - WC_WIN), WC_WINDOW)
        qpos = i * Q_TILE + lax.broadcasted_iota(jnp.int32, (group * Q_TILE, WC_WIN), 0) % Q_TILE
        kpos = start + lax.broadcasted_iota(jnp.int32, (group * Q_TILE, WC_WIN), 1)
        bias = jnp.where(jnp.abs(qpos - kpos) <= WC_WINDOW, 0.0, NEG_INF).astype(F32)

        def segs(h):
            return [(kl_ref[pl.ds(start, WC_WIN), h * hd:(h + 1) * hd],
                     vl_ref[pl.ds(start, WC_WIN), h * hd:(h + 1) * hd], bias),
                    (kc_ref[:, h * hd:(h + 1) * hd], vc_ref[:, h * hd:(h + 1) * hd], None)]
        _gqa_heads(q_ref, o_ref, sink_ref, WC_KV_HEADS, group, hd, hd, None, segs)

    if ctx_tile:
        pl.when(j == 0)(ctx_run)
        pl.when(j > 0)(lat_run)
    else:
        lat_run()


def _window_attn_call(n_batch, with_ctx, q, k, v, sink):
    t = q[0].shape[0]
    in_specs, args, out_spec = _attn_specs(n_batch, with_ctx, q, k, v, 256)
    return pl.pallas_call(
        functools.partial(_window_attn_kernel, ctx_tile=with_ctx),
        grid=(n_batch, SEQ // Q_TILE + (1 if with_ctx else 0)),
        in_specs=[pl.BlockSpec(memory_space=pltpu.SMEM)] + in_specs,
        out_specs=out_spec,
        out_shape=jax.ShapeDtypeStruct((t, 256), BF16),
        compiler_params=_params("arbitrary", "arbitrary"),
        name="attn_window",
    )(sink, *args)


def _natten_kernel(q_ref, kl_ref, vl_ref, kc_ref, vc_ref, bias_ref, o_ref, *, ctx_tile):
    j = pl.program_id(0)
    hd = HEAD_DIM
    scale = HEAD_DIM ** -0.5

    def ctx_run():
        for h in range(NA_HEADS):
            sl = slice(h * hd, (h + 1) * hd)
            o = _attend(q_ref[:, sl] * scale, [(kc_ref[:, sl], vc_ref[:, sl], None)])
            o_ref[:, sl] = o.astype(BF16)

    def lat_run():
        i = j - 1 if ctx_tile else j
        ws = jnp.clip(NA_Q_ROWS * i - NA_KH // 2, 0, GRID_ROWS - NA_WIN_ROWS)
        start = pl.multiple_of(ws * GRID_W, GRID_W)
        for h in range(NA_HEADS):
            sl = slice(h * hd, (h + 1) * hd)
            segs = [(kl_ref[pl.ds(start, NA_WIN), sl], vl_ref[pl.ds(start, NA_WIN), sl], bias_ref[0, h]),
                    (kc_ref[:, sl], vc_ref[:, sl], None)]
            o_ref[:, sl] = _attend(q_ref[:, sl] * scale, segs).astype(BF16)

    if ctx_tile:
        pl.when(j == 0)(ctx_run)
        pl.when(j > 0)(lat_run)
    else:
        lat_run()


_NA_VARIANT = (0, 1, 2, 2, 2, 2, 3, 4)
_NA_VARIANT_TILE = (0, 1, 2, 6, 7)


def _natten_call(n_batch, with_ctx, p, bias_tab):
    t = p.shape[0]
    q, k, v = (p, 256, COL_AQ), (p, 256, COL_AK), (p, 256, COL_AV)
    in_specs, args, out_spec = _attn_specs(n_batch, with_ctx, q, k, v, 256, order_bj=False)

    def bias_ix(j, b):
        i = jnp.maximum(j - 1, 0) if with_ctx else j
        v_ix = sum(jnp.where(i == n, _NA_VARIANT[n], 0) for n in range(len(_NA_VARIANT)))
        return (v_ix, 0, 0, 0)

    in_specs.append(pl.BlockSpec((1, NA_HEADS, Q_TILE, NA_WIN), bias_ix))
    return pl.pallas_call(
        functools.partial(_natten_kernel, ctx_tile=with_ctx),
        grid=(SEQ // Q_TILE + (1 if with_ctx else 0), n_batch),
        in_specs=in_specs,
        out_specs=out_spec,
        out_shape=jax.ShapeDtypeStruct((t, 256), BF16),
        compiler_params=_params("arbitrary", "arbitrary"),
        name="attn_natten",
    )(*args, bias_tab)


def _natten_bias_table(rpb):
    n_dr, n_dc = 2 * NA_KH - 1, 2 * NA_KW - 1
    col = np.arange(GRID_W)
    dc = np.clip(col[None, :] - col[:, None] + NA_KW - 1, 0, n_dc - 1)
    onehot = jnp.asarray(dc[None] == np.arange(n_dc)[:, None, None], F32)
    blocks = jnp.einsum('hdc,cqk->hdqk', rpb.astype(F32), onehot, precision=lax.Precision.HIGHEST)
    cstart = np.clip(col - NA_KW // 2, 0, GRID_W - NA_KW)
    col_ok = (col[None, :] >= cstart[:, None]) & (col[None, :] < cstart[:, None] + NA_KW)
    blocks = jnp.where(col_ok[None, None], blocks, NEG_INF)
    blocks = jnp.concatenate([blocks, jnp.full((NA_HEADS, 1, GRID_W, GRID_W), NEG_INF, F32)], axis=1)
    tiles = np.asarray(_NA_VARIANT_TILE)
    r = NA_Q_ROWS * tiles[:, None] + np.arange(NA_Q_ROWS)[None, :]
    ws = np.clip(NA_Q_ROWS * tiles - NA_KH // 2, 0, GRID_ROWS - NA_WIN_ROWS)
    kr = ws[:, None] + np.arange(NA_WIN_ROWS)[None, :]
    krow0 = np.clip(r - NA_KH // 2, 0, GRID_ROWS - NA_KH)
    row_ok = (kr[:, None, :] >= krow0[:, :, None]) & (kr[:, None, :] < krow0[:, :, None] + NA_KH)
    idx = np.where(row_ok, kr[:, None, :] - r[:, :, None] + NA_KH - 1, n_dr)
    tab = jnp.take(blocks, jnp.asarray(idx.reshape(-1), jnp.int32), axis=1)
    tab = tab.reshape(NA_HEADS, len(tiles), NA_Q_ROWS, NA_WIN_ROWS, GRID_W, GRID_W)
    tab = jnp.transpose(tab, (1, 0, 2, 4, 3, 5))
    return tab.reshape(len(tiles), NA_HEADS, Q_TILE, NA_WIN)


def _merge_kernel(*refs, with_router):
    if with_router:
        (oa_ref, ob_ref, oc_ref, od_ref, gate_ref, x_ref, mod_ref, g2_ref, wb_ref, wo_ref, rt_ref,
         xo_ref, h2_ref, comb_ref) = refs
    else:
        (oa_ref, ob_ref, oc_ref, od_ref, gate_ref, x_ref, mod_ref, g2_ref, wb_ref, wo_ref,
         xo_ref, h2_ref) = refs
    m = mod_ref[0]
    acc = None
    for n, o_ref in enumerate((oa_ref, ob_ref, oc_ref, od_ref)):
        y = _dot(o_ref[...], wb_ref[n])
        g = _sigmoid(gate_ref[:, n * D_MODEL:(n + 1) * D_MODEL].astype(F32))
        acc = g * y if acc is None else acc + g * y
    mix = _dot(acc.astype(BF16), wo_ref[...])
    x = x_ref[...] + m[2:3] * mix
    xo_ref[...] = x
    h2 = _norm_mod(x, g2_ref[...], m[3:4], m[4:5])
    h2_ref[...] = h2.astype(BF16)
    if with_router:
        hh, hm, hl = _split3(h2)
        rh, rm, rl = rt_ref[0], rt_ref[1], rt_ref[2]
        logits = (_dot(hh, rh) + _dot(hh, rm) + _dot(hm, rh)) + (_dot(hh, rl) + _dot(hm, rm) + _dot(hl, rh))
        lane = lax.broadcasted_iota(jnp.int32, logits.shape, 1).astype(F32)
        logits = jnp.where(lane < N_EXPERTS, logits, NEG_INF)
        m1 = jnp.max(logits, axis=-1, keepdims=True)
        i1 = jnp.min(jnp.where(logits == m1, lane, float(LANES)), axis=-1, keepdims=True)
        rest = jnp.where(lane == i1, NEG_INF, logits)
        m2 = jnp.max(rest, axis=-1, keepdims=True)
        i2 = jnp.min(jnp.where(rest == m2, lane, float(LANES)), axis=-1, keepdims=True)
        e = jnp.exp(m2 - m1)
        w1 = 1.0 / (1.0 + e)
        w2 = e / (1.0 + e)
        comb_ref[...] = (jnp.where(lane == 0.0, i1, 0.0) + jnp.where(lane == 1.0, i2, 0.0)
                         + jnp.where(lane == 2.0, w1, 0.0) + jnp.where(lane == 3.0, w2, 0.0))


def _merge_call(outs, gate, x, mod_l, g2, wb, wo, router3, first_tile):
    t = x.shape[0]
    n_rows = t - first_tile * ROW_TILE

    def rows(width, col=0):
        return pl.BlockSpec((ROW_TILE, width), lambda i: (i + first_tile, col // width))

    def orow(width):
        return pl.BlockSpec((ROW_TILE, width), lambda i: (i, 0))

    def full(shape):
        return pl.BlockSpec(shape, lambda i: (0,) * len(shape))

    in_specs = [rows(256)] * 4 + [
        rows(GATE_W),
        rows(D_MODEL),
        pl.BlockSpec((1, ADA_CHUNKS, D_MODEL), lambda i: (_mod_row(i, first_tile), 0, 0)),
        full((1, D_MODEL)), full((N_BRANCH, BRANCH_W, D_MODEL)), full((D_MODEL, D_MODEL)),
    ]
    args = list(outs) + [gate, x, mod_l, g2, wb, wo]
    out_specs = [orow(D_MODEL), orow(D_MODEL)]
    out_shape = [jax.ShapeDtypeStruct((n_rows, D_MODEL), F32), jax.ShapeDtypeStruct((n_rows, D_MODEL), BF16)]
    if router3 is not None:
        in_specs.append(full((3, D_MODEL, LANES)))
        args.append(router3)
        out_specs.append(orow(LANES))
        out_shape.append(jax.ShapeDtypeStruct((n_rows, LANES), F32))
    return pl.pallas_call(
        functools.partial(_merge_kernel, with_router=router3 is not None),
        grid=(n_rows // ROW_TILE,),
        in_specs=in_specs,
        out_specs=out_specs,
        out_shape=out_shape,
        compiler_params=_params("arbitrary"),
        name="merge",
    )(*args)


FFN_ROWS = 1024


def _swiglu_chunk(h, w1, w3, w2):
    a = _dot(h, w1)
    b = _dot(h, w3)
    return _dot((a * _sigmoid(a) * b).astype(BF16), w2)


def _residual_out(x, gate, update, final_norm, final_g):
    x = x + gate * update
    if final_norm:
        ms = jnp.mean(x * x, axis=-1, keepdims=True)
        x = x * lax.rsqrt(ms + NORM_EPS) * final_g
    return x


def _ffn_kernel(h_ref, w1_ref, w3_ref, w2_ref, x_ref, mod_ref, fg_ref, o_ref, acc_ref, *, final_norm):
    f = pl.program_id(1)

    @pl.when(f == 0)
    def _():
        acc_ref[...] = jnp.zeros_like(acc_ref)

    acc_ref[...] += _swiglu_chunk(h_ref[...], w1_ref[...], w3_ref[...], w2_ref[...])

    @pl.when(f == pl.num_programs(1) - 1)
    def _():
        o_ref[...] = _residual_out(x_ref[...], mod_ref[0][5:6], acc_ref[...], final_norm, fg_ref[...])


def _tile_mod_row(i, first_tile, tile_rows):
    n_ctx = CTX_ROWS // tile_rows
    t = i + first_tile
    return jnp.where(t < n_ctx, 8, (t - n_ctx) // (SEQ // tile_rows))


def _ffn_call(h2, w1, w3, w2, x, mod_l, final_g, first_row_tile, final_norm):
    t = x.shape[0]
    rows = lambda width: pl.BlockSpec((FFN_ROWS, width), lambda i, f: (i, 0))
    return pl.pallas_call(
        functools.partial(_ffn_kernel, final_norm=final_norm),
        grid=(t // FFN_ROWS, D_FF // FF_CHUNK),
        in_specs=[
            rows(D_MODEL),
            pl.BlockSpec((D_MODEL, FF_CHUNK), lambda i, f: (0, f)),
            pl.BlockSpec((D_MODEL, FF_CHUNK), lambda i, f: (0, f)),
            pl.BlockSpec((FF_CHUNK, D_MODEL), lambda i, f: (f, 0)),
            rows(D_MODEL),
            pl.BlockSpec((1, ADA_CHUNKS, D_MODEL),
                         lambda i, f: (_tile_mod_row(i, first_row_tile, FFN_ROWS), 0, 0)),
            pl.BlockSpec((1, D_MODEL), lambda i, f: (0, 0)),
        ],
        out_specs=rows(D_MODEL),
        out_shape=jax.ShapeDtypeStruct((t, D_MODEL), F32),
        scratch_shapes=[pltpu.VMEM((FFN_ROWS, D_MODEL), F32)],
        compiler_params=_params("arbitrary", "arbitrary"),
        name="ffn",
    )(h2, w1, w3, w2, x, mod_l, final_g)


MOE_TILE = 512
MOE_SRC = 256
MOE_CMB = 512
TOP_K = 2


def _moe_route(sel, n_tok):
    n_tiles = TOP_K * n_tok // MOE_TILE + N_EXPERTS
    e = sel[:, 0:TOP_K].astype(jnp.int32)
    w = sel[:, TOP_K:2 * TOP_K]
    flat_e = e.reshape(-1)
    onehot = (flat_e[:, None] == jnp.arange(N_EXPERTS)[None, :]).astype(jnp.int32)
    csum = jnp.cumsum(onehot, axis=0)
    rank = jnp.sum((csum - onehot) * onehot, axis=1)
    counts = csum[-1]
    padded = (counts + MOE_TILE - 1) // MOE_TILE * MOE_TILE
    seg_end = jnp.cumsum(padded)
    seg_start = seg_end - padded
    pos = jnp.sum(onehot * seg_start[None, :], axis=1) + rank
    pos2 = pos.reshape(n_tok, TOP_K)
    tile_ix = jnp.arange(n_tiles)
    tile_expert = jnp.minimum(jnp.sum(tile_ix[:, None] >= (seg_end // MOE_TILE)[None, :], axis=1), N_EXPERTS - 1)
    n_valid = seg_end[-1] // MOE_TILE
    tok_chunk = (jnp.arange(TOP_K * n_tok) // TOP_K) // MOE_SRC
    tile_of = pos // MOE_TILE
    c_lo = jax.ops.segment_min(tok_chunk, tile_of, num_segments=n_tiles)
    c_hi = jax.ops.segment_max(tok_chunk, tile_of, num_segments=n_tiles)
    n_src = n_tok // MOE_SRC
    c_lo = jnp.clip(c_lo, 0, n_src).astype(jnp.int32)
    c_hi = jnp.clip(c_hi, -1, n_src - 1).astype(jnp.int32)
    slot_w = jnp.zeros((n_tiles * MOE_TILE,), F32).at[pos].set(w.reshape(-1))
    n_cmb = n_tok // MOE_CMB
    per_chunk = MOE_CMB * TOP_K
    before = jnp.concatenate([jnp.zeros((1, N_EXPERTS), jnp.int32), csum[per_chunk - 1::per_chunk]], axis=0)
    run_start = seg_start[None, :] + before[:-1]
    run_cnt = before[1:] - before[:-1]
    b0 = run_start // MOE_TILE
    b1 = (run_start + jnp.maximum(run_cnt, 1) - 1) // MOE_TILE
    blk = jnp.clip(jnp.stack([b0, b1], axis=-1), 0, n_tiles - 1)
    blk_ok = jnp.stack([run_cnt > 0, (run_cnt > 0) & (b1 != b0)], axis=-1)
    pos_rows = jnp.zeros((n_src, 8, MOE_SRC), jnp.int32).at[:, :TOP_K, :].set(
        jnp.transpose(pos2.reshape(n_src, MOE_SRC, TOP_K), (0, 2, 1)))
    return {
        "n_tiles": n_tiles, "tile_expert": tile_expert.astype(jnp.int32), "n_valid": n_valid.astype(jnp.int32),
        "c_lo": c_lo, "c_hi": c_hi, "slot_w": slot_w[:, None], "pos_rows": pos_rows,
        "pos_cols": pos2.astype(jnp.int32), "blk": blk.reshape(-1).astype(jnp.int32),
        "blk_ok": blk_ok.reshape(-1).astype(jnp.int32),
    }


def _dispatch_kernel(clo_ref, chi_ref, h_ref, pos_ref, o_ref, acc_ref):
    i = pl.program_id(0)
    slot = i * MOE_TILE + lax.broadcasted_iota(jnp.int32, (MOE_TILE, MOE_SRC), 0)
    acc_ref[...] = jnp.zeros_like(acc_ref)

    def body(c, carry):
        pos = pos_ref[c]
        hit = (pos[0:1, :] == slot) | (pos[1:2, :] == slot)
        g = jnp.where(hit, 1.0, 0.0).astype(BF16)
        start = pl.multiple_of(c * MOE_SRC, MOE_SRC)
        acc_ref[...] += _dot(g, h_ref[pl.ds(start, MOE_SRC), :])
        return carry

    lax.fori_loop(clo_ref[i], chi_ref[i] + 1, body, 0)
    o_ref[...] = acc_ref[...].astype(BF16)


def _dispatch_call(h2, route):
    n_tok = h2.shape[0]
    n_tiles = route["n_tiles"]
    grid_spec = pltpu.PrefetchScalarGridSpec(
        num_scalar_prefetch=2,
        grid=(n_tiles,),
        in_specs=[
            pl.BlockSpec((n_tok, D_MODEL), lambda i, lo, hi: (0, 0), pipeline_mode=pl.Buffered(1)),
            pl.BlockSpec((n_tok // MOE_SRC, 8, MOE_SRC), lambda i, lo, hi: (0, 0, 0), pipeline_mode=pl.Buffered(1)),
        ],
        out_specs=pl.BlockSpec((MOE_TILE, D_MODEL), lambda i, lo, hi: (i, 0)),
        scratch_shapes=[pltpu.VMEM((MOE_TILE, D_MODEL), F32)],
    )
    return pl.pallas_call(
        _dispatch_kernel,
        grid_spec=grid_spec,
        out_shape=jax.ShapeDtypeStruct((n_tiles * MOE_TILE, D_MODEL), BF16),
        compiler_params=_params("arbitrary"),
        name="moe_dispatch",
    )(route["c_lo"], route["c_hi"], h2, route["pos_rows"])


def _expert_kernel(te_ref, nv_ref, x_ref, w1_ref, w3_ref, w2_ref, sw_ref, o_ref, acc_ref):
    i = pl.program_id(0)
    f = pl.program_id(1)
    last = pl.num_programs(1) - 1
    valid = i < nv_ref[0]

    @pl.when(valid & (f == 0))
    def _():
        acc_ref[...] = jnp.zeros_like(acc_ref)

    @pl.when(valid)
    def _():
        acc_ref[...] += _swiglu_chunk(x_ref[...], w1_ref[0], w3_ref[0], w2_ref[0])

    @pl.when(valid & (f == last))
    def _():
        o_ref[...] = (acc_ref[...] * sw_ref[...]).astype(BF16)

    @pl.when(jnp.logical_not(valid) & (f == last))
    def _():
        o_ref[...] = jnp.zeros_like(o_ref)


def _expert_call(xs, route, w1, w3, w2):
    n_tiles = route["n_tiles"]
    grid_spec = pltpu.PrefetchScalarGridSpec(
        num_scalar_prefetch=2,
        grid=(n_tiles, D_FF // FF_CHUNK),
        in_specs=[
            pl.BlockSpec((MOE_TILE, D_MODEL), lambda i, f, te, nv: (i, 0)),
            pl.BlockSpec((1, D_MODEL, FF_CHUNK), lambda i, f, te, nv: (te[i], 0, f)),
            pl.BlockSpec((1, D_MODEL, FF_CHUNK), lambda i, f, te, nv: (te[i], 0, f)),
            pl.BlockSpec((1, FF_CHUNK, D_MODEL), lambda i, f, te, nv: (te[i], f, 0)),
            pl.BlockSpec((MOE_TILE, 1), lambda i, f, te, nv: (i, 0)),
        ],
        out_specs=pl.BlockSpec((MOE_TILE, D_MODEL), lambda i, f, te, nv: (i, 0)),
        scratch_shapes=[pltpu.VMEM((MOE_TILE, D_MODEL), F32)],
    )
    return pl.pallas_call(
        _expert_kernel,
        grid_spec=grid_spec,
        out_shape=jax.ShapeDtypeStruct((n_tiles * MOE_TILE, D_MODEL), BF16),
        compiler_params=_params("arbitrary", "arbitrary"),
        name="moe_experts",
    )(route["tile_expert"], route["n_valid"].reshape(1), xs, w1, w3, w2, route["slot_w"])


def _combine_kernel(blk_ref, ok_ref, y_ref, pos_ref, x_ref, mod_ref, fg_ref, o_ref, acc_ref, *, final_norm):
    c = pl.program_id(0)
    e = pl.program_id(1)
    j = pl.program_id(2)
    ix = (c * N_EXPERTS + e) * 2 + j

    @pl.when((e == 0) & (j == 0))
    def _():
        acc_ref[...] = jnp.zeros_like(acc_ref)

    @pl.when(ok_ref[ix] > 0)
    def _():
        slot = blk_ref[ix] * MOE_TILE + lax.broadcasted_iota(jnp.int32, (MOE_CMB, MOE_TILE), 1)
        pos = pos_ref[...]
        hit = (pos[:, 0:1] == slot) | (pos[:, 1:2] == slot)
        acc_ref[...] += _dot(jnp.where(hit, 1.0, 0.0).astype(BF16), y_ref[...])

    @pl.when((e == N_EXPERTS - 1) & (j == 1))
    def _():
        o_ref[...] = _residual_out(x_ref[...], mod_ref[0][5:6], acc_ref[...], final_norm, fg_ref[...])


def _combine_call(ys, route, x, mod_l, final_g, first_tile, final_norm):
    n_tok = x.shape[0]
    rows = lambda width: pl.BlockSpec((MOE_CMB, width), lambda c, e, j, blk, ok: (c, 0))
    grid_spec = pltpu.PrefetchScalarGridSpec(
        num_scalar_prefetch=2,
        grid=(n_tok // MOE_CMB, N_EXPERTS, 2),
        in_specs=[
            pl.BlockSpec((MOE_TILE, D_MODEL), lambda c, e, j, blk, ok: (blk[(c * N_EXPERTS + e) * 2 + j], 0)),
            rows(TOP_K),
            rows(D_MODEL),
            pl.BlockSpec((1, ADA_CHUNKS, D_MODEL),
                         lambda c, e, j, blk, ok: (_tile_mod_row(c, first_tile, MOE_CMB), 0, 0)),
            pl.BlockSpec((1, D_MODEL), lambda c, e, j, blk, ok: (0, 0)),
        ],
        out_specs=rows(D_MODEL),
        scratch_shapes=[pltpu.VMEM((MOE_CMB, D_MODEL), F32)],
    )
    return pl.pallas_call(
        functools.partial(_combine_kernel, final_norm=final_norm),
        grid_spec=grid_spec,
        out_shape=jax.ShapeDtypeStruct((n_tok, D_MODEL), F32),
        compiler_params=_params("arbitrary", "arbitrary", "arbitrary"),
        name="moe_combine",
    )(route["blk"], route["blk_ok"], ys, route["pos_cols"], x, mod_l, final_g)


def _rope_tables():
    t = np.arange(SEQ)

    def angles(rot_dim):
        half = rot_dim // 2
        inv = ROPE_THETA ** (-jnp.arange(0, half, 2, dtype=F32) / half)
        ang = jnp.concatenate([jnp.asarray(t // GRID_W, F32)[:, None] * inv[None, :],
                               jnp.asarray(t % GRID_W, F32)[:, None] * inv[None, :]], axis=-1)
        return jnp.cos(ang), jnp.sin(ang)

    def pad_rows(a, fill):
        return jnp.concatenate([jnp.full((ROW_TILE, a.shape[1]), fill, F32), a], axis=0)

    c, s = angles(HEAD_DIM)
    cos64 = jnp.tile(jnp.concatenate([c, c], axis=-1), (1, 4))
    sin64 = jnp.tile(jnp.concatenate([s, s], axis=-1), (1, 4))
    c, s = angles(MLA_ROPE)
    one = jnp.ones((SEQ, MLA_NOPE), F32)
    zero = jnp.zeros((SEQ, MLA_NOPE), F32)
    tail1 = jnp.ones((SEQ, MLA_PAD - MLA_NOPE - MLA_ROPE), F32)
    tail0 = jnp.zeros((SEQ, MLA_PAD - MLA_NOPE - MLA_ROPE), F32)
    cosm = jnp.tile(jnp.concatenate([one, c, c, tail1], axis=-1), (1, MLA_HEADS))
    sinm = jnp.tile(jnp.concatenate([zero, s, s, tail0], axis=-1), (1, MLA_HEADS))
    blockdiag = jnp.asarray(np.kron(np.eye(4), np.ones((HEAD_DIM, HEAD_DIM))), BF16)
    place = np.zeros((256, MLA_HEADS * MLA_PAD), np.float32)
    for h in range(MLA_HEADS):
        for r in range(MLA_ROPE):
            place[MLA_KV_RANK + r, h * MLA_PAD + MLA_NOPE + r] = 1.0
    return {
        "cos64": pad_rows(cos64, 1.0), "sin64": pad_rows(sin64, 0.0),
        "cosm": pad_rows(cosm, 1.0), "sinm": pad_rows(sinm, 0.0),
        "blockdiag": blockdiag, "pe_place": jnp.asarray(place, BF16),
    }


def _pack_w_in(w):
    pad = jnp.zeros((D_MODEL, COL_GATE - (COL_DKVA + MLA_KV_RANK + MLA_ROPE)), w.dtype)
    n_qkv = COL_DKVA + MLA_KV_RANK + MLA_ROPE
    return jnp.concatenate([w[:, :n_qkv], pad, w[:, n_qkv:]], axis=1).astype(BF16)


def _pack_mla(wqb, wkvb):
    dqh = MLA_NOPE + MLA_ROPE
    q = wqb.reshape(MLA_Q_RANK, MLA_HEADS, dqh)
    q = jnp.pad(q, ((0, 0), (0, 0), (0, MLA_PAD - dqh))).reshape(MLA_Q_RANK, MLA_HEADS * MLA_PAD)
    kv = wkvb.reshape(MLA_KV_RANK, MLA_HEADS, MLA_NOPE + MLA_V)
    k = jnp.pad(kv[:, :, :MLA_NOPE], ((0, 0), (0, 0), (0, MLA_PAD - MLA_NOPE)))
    k = k.reshape(MLA_KV_RANK, MLA_HEADS * MLA_PAD)
    v = kv[:, :, MLA_NOPE:].reshape(MLA_KV_RANK, MLA_HEADS * MLA_V)
    return q.astype(BF16), k.astype(BF16), v.astype(BF16)


def _split3_host(w):
    hi = w.astype(BF16)
    r1 = w - hi.astype(F32)
    mid = r1.astype(BF16)
    lo = (r1 - mid.astype(F32)).astype(BF16)
    return jnp.stack([hi, mid, lo])


def kernel(x, c, ctx, c_ctx, norm1_g, norm2_g, w_ada, b_ada, w_in, na_rpb, gb_qnorm, gb_knorm, wc_sink,
           mla_qnorm, mla_kvnorm, mla_wqb, mla_wkvb, w_branch, w_out, ffn_w1, ffn_w3, ffn_w2,
           moe_router, moe_w1, moe_w3, moe_w2, final_g):
    n_batch = x.shape[0]
    assert x.shape[1:] == (SEQ, D_MODEL) and ctx.shape[1:] == (CTX_LEN, D_MODEL)
    assert n_batch * CTX_LEN <= CTX_ROWS and n_batch <= 8

    ctx_rows = ctx.reshape(n_batch * CTX_LEN, D_MODEL)
    if ctx_rows.shape[0] < CTX_ROWS:
        ctx_rows = jnp.pad(ctx_rows, ((0, CTX_ROWS - ctx_rows.shape[0]), (0, 0)))
    xt = jnp.concatenate([ctx_rows, x.reshape(n_batch * SEQ, D_MODEL)], axis=0)

    cvec = jnp.zeros((16, D_MODEL), F32).at[:n_batch].set(c).at[8].set(c_ctx)
    mod = _mod_call(cvec, w_ada, b_ada).reshape(DEPTH, 16, ADA_CHUNKS, D_MODEL)
    tabs = _rope_tables()
    mla_scale = (MLA_NOPE + MLA_ROPE) ** -0.5
    lat_tile0 = CTX_ROWS // ROW_TILE

    for l in range(DEPTH):
        with_ctx = l < DEPTH - 1
        wq, wk, wv = _pack_mla(mla_wqb[l], mla_wkvb[l])
        lw = {
            "gq": jnp.tile(gb_qnorm[l], 4)[None, :], "gk": jnp.tile(gb_knorm[l], 2)[None, :],
            "qn": mla_qnorm[l][None, :], "kvn": mla_kvnorm[l][None, :], "wqb": wq, "wk": wk, "wv": wv,
        }
        p, gate = _inproj_call(xt, mod[l], norm1_g[l][None, :], _pack_w_in(w_in[l]))
        qb, kb, qc, kc, qd, kd, vd = _prep_call(p, tabs, lw)

        oa = _natten_call(n_batch, with_ctx, p, _natten_bias_table(na_rpb[l]))
        ob = _dense_attn_call(n_batch, with_ctx, (qb, 256, 0), (kb, 128, 0), (p, 128, COL_BV),
                              n_kv=GB_KV_HEADS, group=GB_Q_HEADS // GB_KV_HEADS, dq=HEAD_DIM, dv=HEAD_DIM,
                              name="attn_global")
        oc = _window_attn_call(n_batch, with_ctx, (qc, 256, 0), (kc, 128, 0), (p, 128, COL_CV), wc_sink[l])
        od = _dense_attn_call(n_batch, with_ctx, (qd, MLA_HEADS * MLA_PAD, 0), (kd, MLA_HEADS * MLA_PAD, 0),
                              (vd, 256, 0), n_kv=MLA_HEADS, group=1, dq=MLA_PAD, dv=MLA_V, scale=mla_scale,
                              name="attn_mla")

        is_moe = l % 2 == 1
        router3 = None
        if is_moe:
            router3 = _split3_host(jnp.pad(moe_router[l // 2], ((0, 0), (0, LANES - N_EXPERTS))))
        first_tile = 0 if with_ctx else lat_tile0
        res = _merge_call((oa, ob, oc, od), gate, xt, mod[l], norm2_g[l][None, :],
                          w_branch[l].astype(BF16), w_out[l].astype(BF16), router3, first_tile)
        x_mid, h2 = res[0], res[1]
        final_norm = l == DEPTH - 1
        if is_moe:
            w1, w3, w2 = (w[l // 2].astype(BF16) for w in (moe_w1, moe_w3, moe_w2))
            route = _moe_route(res[2], h2.shape[0])
            xs = _dispatch_call(h2, route)
            ys = _expert_call(xs, route, w1, w3, w2)
            xt = _combine_call(ys, route, x_mid, mod[l], final_g[None, :],
                               0 if with_ctx else CTX_ROWS // MOE_CMB, final_norm)
        else:
            w1, w3, w2 = (w[l // 2].astype(BF16) for w in (ffn_w1, ffn_w3, ffn_w2))
            xt = _ffn_call(h2, w1, w3, w2, x_mid, mod[l], final_g[None, :],
                           0 if with_ctx else CTX_ROWS // FFN_ROWS, final_norm)

    return xt.reshape(n_batch, SEQ, D_MODEL)
```

```python
import functools

import numpy as np
import jax
import jax.numpy as jnp
from jax import lax
from jax.experimental import pallas as pl
from jax.experimental.pallas import tpu as pltpu

F32 = jnp.float32
BF16 = jnp.bfloat16

D_MODEL = 1024
SEQ = 2048
DEPTH = 4
CTX_LEN = 256
GRID_W = 64
GRID_ROWS = SEQ // GRID_W
HEAD_DIM = 64
ROPE_THETA = 10000.0
NORM_EPS = 1e-6
NEG_INF = -1e30

NA_HEADS = 4
NA_KH = 8
NA_KW = 16
GB_Q_HEADS = 4
GB_KV_HEADS = 2
WC_Q_HEADS = 4
WC_KV_HEADS = 2
WC_WINDOW = 128
MLA_HEADS = 4
MLA_Q_RANK = 256
MLA_KV_RANK = 128
MLA_NOPE = 64
MLA_ROPE = 32
MLA_V = 64
MLA_PAD = 128
N_BRANCH = 4
BRANCH_W = 256
D_FF = 3584
N_EXPERTS = 8
ADA_CHUNKS = 6

VMEM_LIMIT_BYTES = 56 * 1024 * 1024
LANES = 128

CTX_ROWS = 2048
ROW_TILE = 512
Q_TILE = 256
NA_Q_ROWS = Q_TILE // GRID_W
NA_WIN_ROWS = 12
NA_WIN = NA_WIN_ROWS * GRID_W
WC_WIN = Q_TILE + 2 * WC_WINDOW

COL_AQ, COL_AK, COL_AV = 0, 256, 512
COL_BQ, COL_BK, COL_BV = 768, 1024, 1152
COL_CQ, COL_CK, COL_CV = 1280, 1536, 1664
COL_DQA, COL_DKVA, COL_GATE = 1792, 2048, 2304
GATE_W = N_BRANCH * D_MODEL
P_WIDTH = COL_GATE + GATE_W
QKV_CHUNK = COL_GATE // 2
FF_CHUNK = 512


def _params(*sem):
    return pltpu.CompilerParams(dimension_semantics=sem, vmem_limit_bytes=VMEM_LIMIT_BYTES)


def _dot(a, b):
    return jnp.dot(a, b, preferred_element_type=F32)


def _dot_nt(a, b):
    return lax.dot_general(a, b, (((1,), (1,)), ((), ())), preferred_element_type=F32)


def _split3(x):
    hi = x.astype(BF16)
    r1 = x - hi.astype(F32)
    mid = r1.astype(BF16)
    lo = (r1 - mid.astype(F32)).astype(BF16)
    return hi, mid, lo


def _sigmoid(x):
    return 1.0 / (1.0 + jnp.exp(-x))


ADA_TN = 1536


def _mod_kernel(c_ref, w_ref, b_ref, o_ref):
    c = c_ref[...]
    sc = (c * _sigmoid(c)).astype(BF16)
    o_ref[0] = _dot(sc, w_ref[0].astype(BF16)) + b_ref[0]


def _mod_call(cvec, w_ada, b_ada):
    n = ADA_CHUNKS * D_MODEL
    return pl.pallas_call(
        _mod_kernel,
        grid=(DEPTH, n // ADA_TN),
        in_specs=[
            pl.BlockSpec((16, D_MODEL), lambda l, j: (0, 0)),
            pl.BlockSpec((1, D_MODEL, ADA_TN), lambda l, j: (l, 0, j)),
            pl.BlockSpec((1, 1, ADA_TN), lambda l, j: (l, 0, j)),
        ],
        out_specs=pl.BlockSpec((1, 16, ADA_TN), lambda l, j: (l, 0, j)),
        out_shape=jax.ShapeDtypeStruct((DEPTH, 16, n), F32),
        compiler_params=_params("arbitrary", "arbitrary"),
        name="adaln_mod",
    )(cvec, w_ada, b_ada.reshape(DEPTH, 1, n))


def _mod_row(i, first_tile):
    n_ctx = CTX_ROWS // ROW_TILE
    t = i + first_tile
    return jnp.where(t < n_ctx, 8, (t - n_ctx) // (SEQ // ROW_TILE))


def _norm_mod(x, g, shift, scale):
    ms = jnp.mean(x * x, axis=-1, keepdims=True)
    y = x * lax.rsqrt(ms + NORM_EPS) * g
    return y * (1.0 + scale) + shift


def _inproj_kernel(x_ref, mod_ref, g_ref, w_ref, p_ref, gate_ref):
    m = mod_ref[0]
    hb = _norm_mod(x_ref[...], g_ref[...], m[0:1], m[1:2]).astype(BF16)
    for c0 in range(0, COL_GATE, QKV_CHUNK):
        p_ref[:, c0:c0 + QKV_CHUNK] = _dot(hb, w_ref[0, :, c0:c0 + QKV_CHUNK]).astype(BF16)
    for c0 in range(0, GATE_W, D_MODEL):
        gate_ref[:, c0:c0 + D_MODEL] = _dot(hb, w_ref[0, :, COL_GATE + c0:COL_GATE + c0 + D_MODEL]).astype(BF16)


def _inproj_call(x, mod_l, g, w_in_p, layer):
    t = x.shape[0]
    return pl.pallas_call(
        _inproj_kernel,
        grid=(t // ROW_TILE,),
        in_specs=[
            pl.BlockSpec((ROW_TILE, D_MODEL), lambda i: (i, 0)),
            pl.BlockSpec((1, ADA_CHUNKS, D_MODEL), lambda i: (_mod_row(i, 0), 0, 0)),
            pl.BlockSpec((1, D_MODEL), lambda i: (0, 0)),
            pl.BlockSpec((1, D_MODEL, P_WIDTH), lambda i: (layer, 0, 0), pipeline_mode=pl.Buffered(1)),
        ],
        out_specs=[pl.BlockSpec((ROW_TILE, COL_GATE), lambda i: (i, 0)),
                   pl.BlockSpec((ROW_TILE, GATE_W), lambda i: (i, 0))],
        out_shape=[jax.ShapeDtypeStruct((t, COL_GATE), BF16), jax.ShapeDtypeStruct((t, GATE_W), BF16)],
        compiler_params=_params("arbitrary"),
        name="inproj",
    )(x, mod_l, g, w_in_p)


def _rope(x, cos, sin, half, first_mask):
    w = x.shape[-1]
    fwd = pltpu.roll(x, w - half, 1)
    bwd = pltpu.roll(x, half, 1)
    rot = jnp.where(first_mask, -fwd, bwd)
    return x * cos + rot * sin


def _head_rms(x, gain, blockdiag):
    hi, mid, lo = _split3(x * x)
    ss = _dot(hi, blockdiag) + _dot(mid, blockdiag) + _dot(lo, blockdiag)
    return x * lax.rsqrt(ss * (1.0 / HEAD_DIM) + NORM_EPS) * gain


def _prep_kernel(bq_ref, bk_ref, cq_ref, ck_ref, dqa_ref, dkva_ref,
                 cos64_ref, sin64_ref, cosm_ref, sinm_ref,
                 gq_ref, gk_ref, bd_ref, qn_ref, kvn_ref, wqb_ref, wk_ref, wv_ref, pe_ref,
                 qb_ref, kb_ref, qc_ref, kc_ref, qd_ref, kd_ref, vd_ref):
    scale = HEAD_DIM ** -0.5
    cos64, sin64 = cos64_ref[...], sin64_ref[...]
    lane = lax.broadcasted_iota(jnp.int32, (1, 256), 1)
    first64 = (lane % HEAD_DIM) < (HEAD_DIM // 2)
    bd = bd_ref[...]

    bq = _head_rms(bq_ref[...].astype(F32), gq_ref[...], bd)
    qb_ref[...] = (_rope(bq, cos64, sin64, HEAD_DIM // 2, first64) * scale).astype(BF16)
    bk = _head_rms(bk_ref[...].astype(F32), gk_ref[...], bd[:128, :128])
    kb_ref[...] = _rope(bk, cos64[:, :128], sin64[:, :128], HEAD_DIM // 2, first64[:, :128]).astype(BF16)
    qc_ref[...] = (_rope(cq_ref[...].astype(F32), cos64, sin64, HEAD_DIM // 2, first64) * scale).astype(BF16)
    kc_ref[...] = _rope(ck_ref[...].astype(F32), cos64[:, :128], sin64[:, :128], HEAD_DIM // 2,
                        first64[:, :128]).astype(BF16)

    cosm, sinm = cosm_ref[...], sinm_ref[...]
    lane_m = lax.broadcasted_iota(jnp.int32, (1, MLA_HEADS * MLA_PAD), 1) % MLA_PAD
    first_m = lane_m < (MLA_NOPE + MLA_ROPE // 2)
    dqa = dqa_ref[...].astype(F32)
    qn = dqa * lax.rsqrt(jnp.mean(dqa * dqa, axis=-1, keepdims=True) + NORM_EPS) * qn_ref[...]
    dq = _dot(qn.astype(BF16), wqb_ref[...])
    qd_ref[...] = _rope(dq, cosm, sinm, MLA_ROPE // 2, first_m).astype(BF16)

    dkva = dkva_ref[...]
    kvc = dkva[:, :MLA_KV_RANK].astype(F32)
    kvn = (kvc * lax.rsqrt(jnp.mean(kvc * kvc, axis=-1, keepdims=True) + NORM_EPS) * kvn_ref[...]).astype(BF16)
    dk = _dot(kvn, wk_ref[...]) + _dot(dkva, pe_ref[...])
    kd_ref[...] = _rope(dk, cosm, sinm, MLA_ROPE // 2, first_m).astype(BF16)
    vd_ref[...] = _dot(kvn, wv_ref[...]).astype(BF16)


def _rope_row_block(i):
    n_ctx = CTX_ROWS // ROW_TILE
    return jnp.where(i < n_ctx, 0, 1 + (i - n_ctx) % (SEQ // ROW_TILE))


def _prep_call(p, tabs, lw):
    t = p.shape[0]
    mw = MLA_HEADS * MLA_PAD

    def pcol(width, col):
        return pl.BlockSpec((ROW_TILE, width), lambda i: (i, col // width))

    def full(shape):
        return pl.BlockSpec(shape, lambda i: (0,) * len(shape))

    def rows(width):
        return pl.BlockSpec((ROW_TILE, width), lambda i: (i, 0))

    def tab(width):
        return pl.BlockSpec((ROW_TILE, width), lambda i: (_rope_row_block(i), 0))

    outs = [(256, BF16), (128, BF16), (256, BF16), (128, BF16), (mw, BF16), (mw, BF16), (256, BF16)]
    return pl.pallas_call(
        _prep_kernel,
        grid=(t // ROW_TILE,),
        in_specs=[
            pcol(256, COL_BQ), pcol(128, COL_BK), pcol(256, COL_CQ), pcol(128, COL_CK),
            pcol(256, COL_DQA), pcol(256, COL_DKVA),
            tab(256), tab(256), tab(mw), tab(mw),
            full((1, 256)), full((1, 128)), full((256, 256)), full((1, MLA_Q_RANK)), full((1, MLA_KV_RANK)),
            full((MLA_Q_RANK, mw)), full((MLA_KV_RANK, mw)), full((MLA_KV_RANK, 256)), full((256, mw)),
        ],
        out_specs=[rows(w) for w, _ in outs],
        out_shape=[jax.ShapeDtypeStruct((t, w), dt) for w, dt in outs],
        compiler_params=_params("arbitrary"),
        name="mixer_prep",
    )(p, p, p, p, p, p, tabs["cos64"], tabs["sin64"], tabs["cosm"], tabs["sinm"],
      lw["gq"], lw["gk"], tabs["blockdiag"], lw["qn"], lw["kvn"], lw["wqb"], lw["wk"], lw["wv"], tabs["pe_place"])


def _attend(q, segs, sink=None, scale=None):
    scores = []
    for k, _, bias in segs:
        s = _dot_nt(q, k)
        if scale is not None:
            s = s * scale
        if bias is not None:
            s = s + bias
        scores.append(s)
    m = functools.reduce(jnp.maximum, [jnp.max(s, axis=-1, keepdims=True) for s in scores])
    if sink is not None:
        m = jnp.maximum(m, sink)
    denom = None
    out = None
    for s, (_, v, _) in zip(scores, segs):
        p = jnp.exp(s - m)
        ps = jnp.sum(p, axis=-1, keepdims=True)
        pv = _dot(p.astype(BF16), v)
        denom = ps if denom is None else denom + ps
        out = pv if out is None else out + pv
    if sink is not None:
        denom = denom + jnp.exp(sink - m)
    return out / denom


def _gqa_heads(q_ref, o_ref, sink_ref, n_kv, group, dq, dv, scale, seg_fn):
    tq = q_ref.shape[0]
    for h in range(n_kv):
        heads = [h * group + g for g in range(group)]
        q = jnp.concatenate([q_ref[:, a * dq:(a + 1) * dq] for a in heads], axis=0) if group > 1 \
            else q_ref[:, h * dq:(h + 1) * dq]
        sink = None
        if sink_ref is not None:
            sink = jnp.concatenate([jnp.full((tq, 1), sink_ref[a], F32) for a in heads], axis=0)
        o = _attend(q, seg_fn(h), sink=sink, scale=scale)
        for g, a in enumerate(heads):
            o_ref[:, a * dv:(a + 1) * dv] = o[g * tq:(g + 1) * tq].astype(BF16)


def _dense_attn_kernel(*refs, n_kv, group, dq, dv, scale, has_sink, ctx_tile, only_ctx_keys):
    if has_sink:
        sink_ref, refs = refs[0], refs[1:]
    else:
        sink_ref = None
    q_ref, kl_ref, vl_ref, kc_ref, vc_ref, o_ref = refs

    def run(with_latent):
        def segs(h):
            out = []
            if with_latent:
                out.append((kl_ref[:, h * dq:(h + 1) * dq], vl_ref[:, h * dv:(h + 1) * dv], None))
            out.append((kc_ref[:, h * dq:(h + 1) * dq], vc_ref[:, h * dv:(h + 1) * dv], None))
            return out
        _gqa_heads(q_ref, o_ref, sink_ref, n_kv, group, dq, dv, scale, segs)

    if only_ctx_keys:
        run(False)
    elif ctx_tile:
        j = pl.program_id(1)
        pl.when(j == 0)(lambda: run(False))
        pl.when(j > 0)(lambda: run(True))
    else:
        run(True)


def _q_row_block(b, j, n_batch, with_ctx):
    per_batch = SEQ // Q_TILE
    lat0 = CTX_ROWS // Q_TILE
    if with_ctx:
        return jnp.where(j == 0, b, lat0 + b * per_batch + j - 1)
    return lat0 + b * per_batch + j


def _attn_specs(n_batch, with_ctx, q, k, v, o_width, order_bj=True):
    def ix(f):
        return (lambda b, j: f(b, j)) if order_bj else (lambda j, b: f(b, j))

    def qspec(width, col):
        return pl.BlockSpec((Q_TILE, width), ix(lambda b, j: (_q_row_block(b, j, n_batch, with_ctx), col // width)))

    def lat(width, col):
        return pl.BlockSpec((SEQ, width), ix(lambda b, j: (CTX_ROWS // SEQ + b, col // width)))

    def ctx(width, col):
        return pl.BlockSpec((CTX_LEN, width), ix(lambda b, j: (b, col // width)))

    in_specs = [qspec(q[1], q[2]), lat(k[1], k[2]), lat(v[1], v[2]), ctx(k[1], k[2]), ctx(v[1], v[2])]
    args = [q[0], k[0], v[0], k[0], v[0]]
    if with_ctx:
        return in_specs, args, qspec(o_width, 0)
    out_spec = pl.BlockSpec((Q_TILE, o_width), ix(lambda b, j: (b * (SEQ // Q_TILE) + j, 0)))
    return in_specs, args, out_spec


def _attn_out_rows(t, with_ctx):
    return t if with_ctx else t - CTX_ROWS


def _dense_attn_call(n_batch, with_ctx, q, k, v, *, n_kv, group, dq, dv, scale=None, sink=None, name):
    t = q[0].shape[0]
    in_specs, args, out_spec = _attn_specs(n_batch, with_ctx, q, k, v, 256)
    if sink is not None:
        in_specs = [pl.BlockSpec(memory_space=pltpu.SMEM)] + in_specs
        args = [sink] + args
    kern = functools.partial(_dense_attn_kernel, n_kv=n_kv, group=group, dq=dq, dv=dv, scale=scale,
                             has_sink=sink is not None, ctx_tile=with_ctx, only_ctx_keys=False)
    return pl.pallas_call(
        kern,
        grid=(n_batch, SEQ // Q_TILE + (1 if with_ctx else 0)),
        in_specs=in_specs,
        out_specs=out_spec,
        out_shape=jax.ShapeDtypeStruct((_attn_out_rows(t, with_ctx), 256), BF16),
        compiler_params=_params("arbitrary", "arbitrary"),
        name=name,
    )(*args)


def _window_attn_kernel(sink_ref, q_ref, kl_ref, vl_ref, kc_ref, vc_ref, o_ref, *, ctx_tile):
    j = pl.program_id(1)
    hd = HEAD_DIM
    group = WC_Q_HEADS // WC_KV_HEADS

    def ctx_run():
        def segs(h):
            return [(kc_ref[:, h * hd:(h + 1) * hd], vc_ref[:, h * hd:(h + 1) * hd], None)]
        _gqa_heads(q_ref, o_ref, sink_ref, WC_KV_HEADS, group, hd, hd, None, segs)

    def lat_run():
        i = j - 1 if ctx_tile else j
        start = pl.multiple_of(jnp.clip(i * Q_TILE - WC_WINDOW, 0, SEQ - WC_WIN), WC_WINDOW)
        qpos = i * Q_TILE + lax.broadcasted_iota(jnp.int32, (group * Q_TILE, WC_WIN), 0) % Q_TILE
        kpos = start + lax.broadcasted_iota(jnp.int32, (group * Q_TILE, WC_WIN), 1)
        bias = jnp.where(jnp.abs(qpos - kpos) <= WC_WINDOW, 0.0, NEG_INF).astype(F32)

        def segs(h):
            return [(kl_ref[pl.ds(start, WC_WIN), h * hd:(h + 1) * hd],
                     vl_ref[pl.ds(start, WC_WIN), h * hd:(h + 1) * hd], bias),
                    (kc_ref[:, h * hd:(h + 1) * hd], vc_ref[:, h * hd:(h + 1) * hd], None)]
        _gqa_heads(q_ref, o_ref, sink_ref, WC_KV_HEADS, group, hd, hd, None, segs)

    if ctx_tile:
        pl.when(j == 0)(ctx_run)
        pl.when(j > 0)(lat_run)
    else:
        lat_run()


def _window_attn_call(n_batch, with_ctx, q, k, v, sink):
    t = q[0].shape[0]
    in_specs, args, out_spec = _attn_specs(n_batch, with_ctx, q, k, v, 256)
    return pl.pallas_call(
        functools.partial(_window_attn_kernel, ctx_tile=with_ctx),
        grid=(n_batch, SEQ // Q_TILE + (1 if with_ctx else 0)),
        in_specs=[pl.BlockSpec(memory_space=pltpu.SMEM)] + in_specs,
        out_specs=out_spec,
        out_shape=jax.ShapeDtypeStruct((_attn_out_rows(t, with_ctx), 256), BF16),
        compiler_params=_params("arbitrary", "arbitrary"),
        name="attn_window",
    )(sink, *args)


def _natten_kernel(q_ref, kl_ref, vl_ref, kc_ref, vc_ref, bias_ref, o_ref, *, ctx_tile):
    j = pl.program_id(0)
    hd = HEAD_DIM
    scale = HEAD_DIM ** -0.5

    def ctx_run():
        for h in range(NA_HEADS):
            sl = slice(h * hd, (h + 1) * hd)
            o = _attend(q_ref[:, sl] * scale, [(kc_ref[:, sl], vc_ref[:, sl], None)])
            o_ref[:, sl] = o.astype(BF16)

    def lat_run():
        i = j - 1 if ctx_tile else j
        ws = jnp.clip(NA_Q_ROWS * i - NA_KH // 2, 0, GRID_ROWS - NA_WIN_ROWS)
        start = pl.multiple_of(ws * GRID_W, GRID_W)
        for h in range(NA_HEADS):
            sl = slice(h * hd, (h + 1) * hd)
            segs = [(kl_ref[pl.ds(start, NA_WIN), sl], vl_ref[pl.ds(start, NA_WIN), sl], bias_ref[0, h]),
                    (kc_ref[:, sl], vc_ref[:, sl], None)]
            o_ref[:, sl] = _attend(q_ref[:, sl] * scale, segs).astype(BF16)

    if ctx_tile:
        pl.when(j == 0)(ctx_run)
        pl.when(j > 0)(lat_run)
    else:
        lat_run()


_NA_VARIANT = (0, 1, 2, 2, 2, 2, 3, 4)
_NA_VARIANT_TILE = (0, 1, 2, 6, 7)


def _natten_call(n_batch, with_ctx, p, bias_tab):
    t = p.shape[0]
    q, k, v = (p, 256, COL_AQ), (p, 256, COL_AK), (p, 256, COL_AV)
    in_specs, args, out_spec = _attn_specs(n_batch, with_ctx, q, k, v, 256, order_bj=False)

    def bias_ix(j, b):
        i = jnp.maximum(j - 1, 0) if with_ctx else j
        v_ix = sum(jnp.where(i == n, _NA_VARIANT[n], 0) for n in range(len(_NA_VARIANT)))
        return (v_ix, 0, 0, 0)

    in_specs.append(pl.BlockSpec((1, NA_HEADS, Q_TILE, NA_WIN), bias_ix))
    return pl.pallas_call(
        functools.partial(_natten_kernel, ctx_tile=with_ctx),
        grid=(SEQ // Q_TILE + (1 if with_ctx else 0), n_batch),
        in_specs=in_specs,
        out_specs=out_spec,
        out_shape=jax.ShapeDtypeStruct((_attn_out_rows(t, with_ctx), 256), BF16),
        compiler_params=_params("arbitrary", "arbitrary"),
        name="attn_natten",
    )(*args, bias_tab)


def _natten_bias_table(rpb):
    n_dr, n_dc = 2 * NA_KH - 1, 2 * NA_KW - 1
    col = np.arange(GRID_W)
    dc = np.clip(col[None, :] - col[:, None] + NA_KW - 1, 0, n_dc - 1)
    onehot = jnp.asarray(dc[None] == np.arange(n_dc)[:, None, None], F32)
    blocks = jnp.einsum('hdc,cqk->hdqk', rpb.astype(F32), onehot, precision=lax.Precision.HIGHEST)
    cstart = np.clip(col - NA_KW // 2, 0, GRID_W - NA_KW)
    col_ok = (col[None, :] >= cstart[:, None]) & (col[None, :] < cstart[:, None] + NA_KW)
    blocks = jnp.where(col_ok[None, None], blocks, NEG_INF)
    blocks = jnp.concatenate([blocks, jnp.full((NA_HEADS, 1, GRID_W, GRID_W), NEG_INF, F32)], axis=1)
    tiles = np.asarray(_NA_VARIANT_TILE)
    r = NA_Q_ROWS * tiles[:, None] + np.arange(NA_Q_ROWS)[None, :]
    ws = np.clip(NA_Q_ROWS * tiles - NA_KH // 2, 0, GRID_ROWS - NA_WIN_ROWS)
    kr = ws[:, None] + np.arange(NA_WIN_ROWS)[None, :]
    krow0 = np.clip(r - NA_KH // 2, 0, GRID_ROWS - NA_KH)
    row_ok = (kr[:, None, :] >= krow0[:, :, None]) & (kr[:, None, :] < krow0[:, :, None] + NA_KH)
    idx = np.where(row_ok, kr[:, None, :] - r[:, :, None] + NA_KH - 1, n_dr)
    tab = jnp.take(blocks, jnp.asarray(idx.reshape(-1), jnp.int32), axis=1)
    tab = tab.reshape(NA_HEADS, len(tiles), NA_Q_ROWS, NA_WIN_ROWS, GRID_W, GRID_W)
    tab = jnp.transpose(tab, (1, 0, 2, 4, 3, 5))
    return tab.reshape(len(tiles), NA_HEADS, Q_TILE, NA_WIN)


def _merge_kernel(*refs, with_router):
    if with_router:
        (oa_ref, ob_ref, oc_ref, od_ref, gate_ref, x_ref, mod_ref, g2_ref, wb_ref, wo_ref, rt_ref,
         xo_ref, h2_ref, comb_ref) = refs
    else:
        (oa_ref, ob_ref, oc_ref, od_ref, gate_ref, x_ref, mod_ref, g2_ref, wb_ref, wo_ref,
         xo_ref, h2_ref) = refs
    m = mod_ref[0]
    acc = None
    for n, o_ref in enumerate((oa_ref, ob_ref, oc_ref, od_ref)):
        y = _dot(o_ref[...], wb_ref[n])
        g = _sigmoid(gate_ref[:, n * D_MODEL:(n + 1) * D_MODEL].astype(F32))
        acc = g * y if acc is None else acc + g * y
    mix = _dot(acc.astype(BF16), wo_ref[...])
    x = x_ref[...] + m[2:3] * mix
    xo_ref[...] = x
    h2 = _norm_mod(x, g2_ref[...], m[3:4], m[4:5])
    h2_ref[...] = h2.astype(BF16)
    if with_router:
        hh, hm, _ = _split3(h2)
        rh, rm = rt_ref[0], rt_ref[1]
        logits = _dot(hh, rh) + (_dot(hh, rm) + _dot(hm, rh))
        lane = lax.broadcasted_iota(jnp.int32, logits.shape, 1).astype(F32)
        logits = jnp.where(lane < N_EXPERTS, logits, NEG_INF)
        m1 = jnp.max(logits, axis=-1, keepdims=True)
        i1 = jnp.min(jnp.where(logits == m1, lane, float(LANES)), axis=-1, keepdims=True)
        rest = jnp.where(lane == i1, NEG_INF, logits)
        m2 = jnp.max(rest, axis=-1, keepdims=True)
        i2 = jnp.min(jnp.where(rest == m2, lane, float(LANES)), axis=-1, keepdims=True)
        e = jnp.exp(m2 - m1)
        w1 = 1.0 / (1.0 + e)
        w2 = e / (1.0 + e)
        comb_ref[...] = (jnp.where(lane == 0.0, i1, 0.0) + jnp.where(lane == 1.0, i2, 0.0)
                         + jnp.where(lane == 2.0, w1, 0.0) + jnp.where(lane == 3.0, w2, 0.0))


def _merge_call(outs, gate, x, mod_l, g2, wb, wo, router3, first_tile):
    t = x.shape[0]
    n_rows = t - first_tile * ROW_TILE

    def rows(width, col=0):
        return pl.BlockSpec((ROW_TILE, width), lambda i: (i + first_tile, col // width))

    def orow(width):
        return pl.BlockSpec((ROW_TILE, width), lambda i: (i, 0))

    def full(shape):
        return pl.BlockSpec(shape, lambda i: (0,) * len(shape))

    in_specs = [orow(256)] * 4 + [
        rows(GATE_W),
        rows(D_MODEL),
        pl.BlockSpec((1, ADA_CHUNKS, D_MODEL), lambda i: (_mod_row(i, first_tile), 0, 0)),
        full((1, D_MODEL)), full((N_BRANCH, BRANCH_W, D_MODEL)), full((D_MODEL, D_MODEL)),
    ]
    args = list(outs) + [gate, x, mod_l, g2, wb, wo]
    out_specs = [orow(D_MODEL), orow(D_MODEL)]
    out_shape = [jax.ShapeDtypeStruct((n_rows, D_MODEL), F32), jax.ShapeDtypeStruct((n_rows, D_MODEL), BF16)]
    if router3 is not None:
        in_specs.append(full((2, D_MODEL, LANES)))
        args.append(router3)
        out_specs.append(orow(LANES))
        out_shape.append(jax.ShapeDtypeStruct((n_rows, LANES), F32))
    return pl.pallas_call(
        functools.partial(_merge_kernel, with_router=router3 is not None),
        grid=(n_rows // ROW_TILE,),
        in_specs=in_specs,
        out_specs=out_specs,
        out_shape=out_shape,
        compiler_params=_params("arbitrary"),
        name="merge",
    )(*args)


FFN_ROWS = 1024


def _swiglu_chunk(h, w1, w3, w2):
    a = _dot(h, w1)
    b = _dot(h, w3)
    return _dot((a * _sigmoid(a) * b).astype(BF16), w2)


def _residual_out(x, gate, update, final_norm, final_g):
    x = x + gate * update
    if final_norm:
        ms = jnp.mean(x * x, axis=-1, keepdims=True)
        x = x * lax.rsqrt(ms + NORM_EPS) * final_g
    return x


def _ffn_kernel(h_ref, w1_ref, w3_ref, w2_ref, x_ref, mod_ref, fg_ref, o_ref, acc_ref, *, final_norm):
    f = pl.program_id(1)

    @pl.when(f == 0)
    def _():
        acc_ref[...] = jnp.zeros_like(acc_ref)

    acc_ref[...] += _swiglu_chunk(h_ref[...], w1_ref[...], w3_ref[...], w2_ref[...])

    @pl.when(f == pl.num_programs(1) - 1)
    def _():
        o_ref[...] = _residual_out(x_ref[...], mod_ref[0][5:6], acc_ref[...], final_norm, fg_ref[...])


def _tile_mod_row(i, first_tile, tile_rows):
    n_ctx = CTX_ROWS // tile_rows
    t = i + first_tile
    return jnp.where(t < n_ctx, 8, (t - n_ctx) // (SEQ // tile_rows))


def _ffn_call(h2, w1, w3, w2, x, mod_l, final_g, first_row_tile, final_norm):
    t = x.shape[0]
    rows = lambda width: pl.BlockSpec((FFN_ROWS, width), lambda i, f: (i, 0))
    return pl.pallas_call(
        functools.partial(_ffn_kernel, final_norm=final_norm),
        grid=(t // FFN_ROWS, D_FF // FF_CHUNK),
        in_specs=[
            rows(D_MODEL),
            pl.BlockSpec((D_MODEL, FF_CHUNK), lambda i, f: (0, f)),
            pl.BlockSpec((D_MODEL, FF_CHUNK), lambda i, f: (0, f)),
            pl.BlockSpec((FF_CHUNK, D_MODEL), lambda i, f: (f, 0)),
            rows(D_MODEL),
            pl.BlockSpec((1, ADA_CHUNKS, D_MODEL),
                         lambda i, f: (_tile_mod_row(i, first_row_tile, FFN_ROWS), 0, 0)),
            pl.BlockSpec((1, D_MODEL), lambda i, f: (0, 0)),
        ],
        out_specs=rows(D_MODEL),
        out_shape=jax.ShapeDtypeStruct((t, D_MODEL), F32),
        scratch_shapes=[pltpu.VMEM((FFN_ROWS, D_MODEL), F32)],
        compiler_params=_params("arbitrary", "arbitrary"),
        name="ffn",
    )(h2, w1, w3, w2, x, mod_l, final_g)


MOE_TILE = 512
MOE_SRC = 256
MOE_CMB_WIN = MOE_SRC + 16
MOE_DSP_SMALL = 144
MOE_DSP_LARGE = MOE_SRC + 16
TOP_K = 2


def _moe_route(sel, n_tok):
    n_tiles = TOP_K * n_tok // MOE_TILE + N_EXPERTS
    e = sel[:, 0:TOP_K].astype(jnp.int32)
    flat_e = e.reshape(-1)
    onehot = (flat_e[:, None] == jnp.arange(N_EXPERTS)[None, :]).astype(jnp.int32)
    csum = jnp.cumsum(onehot, axis=0)
    rank = jnp.sum((csum - onehot) * onehot, axis=1)
    counts = csum[-1]
    padded = (counts + MOE_TILE - 1) // MOE_TILE * MOE_TILE
    seg_end = jnp.cumsum(padded)
    seg_start = seg_end - padded
    pos = jnp.sum(onehot * seg_start[None, :], axis=1) + rank
    pos2 = pos.reshape(n_tok, TOP_K)
    tile_ix = jnp.arange(n_tiles)
    tile_expert = jnp.minimum(jnp.sum(tile_ix[:, None] >= (seg_end // MOE_TILE)[None, :], axis=1), N_EXPERTS - 1)
    n_valid = seg_end[-1] // MOE_TILE
    n_src = n_tok // MOE_SRC
    per_chunk = MOE_SRC * TOP_K
    before = jnp.concatenate([jnp.zeros((1, N_EXPERTS), jnp.int32), csum[per_chunk - 1::per_chunk]], axis=0)
    run_start = seg_start[None, :] + before[:-1]
    run_cnt = before[1:] - before[:-1]
    cum = jnp.take(before, tile_expert, axis=1)
    local0 = tile_ix * MOE_TILE - jnp.take(seg_start, tile_expert)
    local1 = jnp.minimum(local0 + MOE_TILE, jnp.take(counts, tile_expert))
    c_lo = jnp.sum(cum[1:] <= local0[None, :], axis=0)
    c_hi = jnp.sum(cum[:-1] < local1[None, :], axis=0) - 1
    win = jnp.minimum(run_start // 16 * 16, n_tiles * MOE_TILE - MOE_CMB_WIN)
    pos_rows = jnp.zeros((n_src, 8, MOE_SRC), jnp.int32).at[:, :TOP_K, :].set(
        jnp.transpose(pos2.reshape(n_src, MOE_SRC, TOP_K), (0, 2, 1)))
    i32 = lambda a: a.astype(jnp.int32)
    return {
        "n_tiles": n_tiles, "tile_expert": i32(tile_expert), "n_valid": i32(n_valid).reshape(1),
        "c_lo": i32(c_lo), "c_hi": i32(c_hi), "run_start": i32(run_start.reshape(-1)),
        "run_cnt": i32(run_cnt.reshape(-1)), "win": i32(win.reshape(-1)),
        "pos_rows": pos_rows, "pos_cols": i32(pos2),
    }


def _dispatch_kernel(clo_ref, chi_ref, te_ref, rs_ref, rc_ref, h_ref, pos_ref, o_ref, acc_ref):
    i = pl.program_id(0)
    base = i * MOE_TILE
    expert = te_ref[i]
    acc_ref[...] = jnp.zeros_like(acc_ref)

    def body(c, carry):
        run0 = rs_ref[c * N_EXPERTS + expert]
        lo = jnp.maximum(run0, base) - base
        hi = jnp.minimum(run0 + rc_ref[c * N_EXPERTS + expert], base + MOE_TILE) - base
        pos = pos_ref[c]
        h = h_ref[pl.ds(pl.multiple_of(c * MOE_SRC, MOE_SRC), MOE_SRC), :]

        def window(rows):
            ws = pl.multiple_of(jnp.minimum(lo // 16 * 16, MOE_TILE - rows), 16)
            slot = base + ws + lax.broadcasted_iota(jnp.int32, (rows, MOE_SRC), 0)
            hit = (pos[0:1, :] == slot) | (pos[1:2, :] == slot)
            acc_ref[pl.ds(ws, rows), :] += _dot(jnp.where(hit, 1.0, 0.0).astype(BF16), h)

        small = hi - jnp.minimum(lo // 16 * 16, MOE_TILE - MOE_DSP_SMALL) <= MOE_DSP_SMALL
        pl.when((hi > lo) & small)(lambda: window(MOE_DSP_SMALL))
        pl.when((hi > lo) & jnp.logical_not(small))(lambda: window(MOE_DSP_LARGE))
        return carry

    lax.fori_loop(clo_ref[i], chi_ref[i] + 1, body, 0)
    o_ref[...] = acc_ref[...].astype(BF16)


def _dispatch_call(h2, route):
    n_tok = h2.shape[0]
    n_tiles = route["n_tiles"]
    grid_spec = pltpu.PrefetchScalarGridSpec(
        num_scalar_prefetch=5,
        grid=(n_tiles,),
        in_specs=[
            pl.BlockSpec((n_tok, D_MODEL), lambda i, *_: (0, 0), pipeline_mode=pl.Buffered(1)),
            pl.BlockSpec((n_tok // MOE_SRC, 8, MOE_SRC), lambda i, *_: (0, 0, 0), pipeline_mode=pl.Buffered(1)),
        ],
        out_specs=pl.BlockSpec((MOE_TILE, D_MODEL), lambda i, *_: (i, 0)),
        scratch_shapes=[pltpu.VMEM((MOE_TILE, D_MODEL), F32)],
    )
    return pl.pallas_call(
        _dispatch_kernel,
        grid_spec=grid_spec,
        out_shape=jax.ShapeDtypeStruct((n_tiles * MOE_TILE, D_MODEL), BF16),
        compiler_params=_params("arbitrary"),
        name="moe_dispatch",
    )(route["c_lo"], route["c_hi"], route["tile_expert"], route["run_start"], route["run_cnt"],
      h2, route["pos_rows"])


def _expert_kernel(te_ref, nv_ref, x_ref, w1_ref, w3_ref, w2_ref, o_ref, acc_ref):
    i = pl.program_id(0)
    f = pl.program_id(1)
    last = pl.num_programs(1) - 1
    valid = i < nv_ref[0]

    @pl.when(valid & (f == 0))
    def _():
        acc_ref[...] = jnp.zeros_like(acc_ref)

    @pl.when(valid)
    def _():
        acc_ref[...] += _swiglu_chunk(x_ref[...], w1_ref[0], w3_ref[0], w2_ref[0])

    @pl.when(valid & (f == last))
    def _():
        o_ref[...] = acc_ref[...].astype(BF16)

    @pl.when(jnp.logical_not(valid) & (f == last))
    def _():
        o_ref[...] = jnp.zeros_like(o_ref)


def _expert_call(xs, route, w1, w3, w2):
    n_tiles = route["n_tiles"]
    grid_spec = pltpu.PrefetchScalarGridSpec(
        num_scalar_prefetch=2,
        grid=(n_tiles, D_FF // FF_CHUNK),
        in_specs=[
            pl.BlockSpec((MOE_TILE, D_MODEL), lambda i, f, te, nv: (i, 0)),
            pl.BlockSpec((1, D_MODEL, FF_CHUNK), lambda i, f, te, nv: (te[i], 0, f)),
            pl.BlockSpec((1, D_MODEL, FF_CHUNK), lambda i, f, te, nv: (te[i], 0, f)),
            pl.BlockSpec((1, FF_CHUNK, D_MODEL), lambda i, f, te, nv: (te[i], f, 0)),
        ],
        out_specs=pl.BlockSpec((MOE_TILE, D_MODEL), lambda i, f, te, nv: (i, 0)),
        scratch_shapes=[pltpu.VMEM((MOE_TILE, D_MODEL), F32)],
    )
    return pl.pallas_call(
        _expert_kernel,
        grid_spec=grid_spec,
        out_shape=jax.ShapeDtypeStruct((n_tiles * MOE_TILE, D_MODEL), BF16),
        compiler_params=_params("arbitrary", "arbitrary"),
        name="moe_experts",
    )(route["tile_expert"], route["n_valid"], xs, w1, w3, w2)


def _combine_kernel(win_ref, y_hbm, pos_ref, sel_ref, x_ref, mod_ref, fg_ref, o_ref, buf_ref, sem_ref, acc_ref,
                    *, final_norm):
    c = pl.program_id(0)

    def window_copy(e):
        start = pl.multiple_of(win_ref[c * N_EXPERTS + e], 16)
        return pltpu.make_async_copy(y_hbm.at[pl.ds(start, MOE_CMB_WIN), :], buf_ref.at[e % 2], sem_ref.at[e % 2])

    window_copy(0).start()
    pos = pos_ref[...]
    sel = sel_ref[...]
    lane = lax.broadcasted_iota(jnp.int32, (MOE_SRC, MOE_CMB_WIN), 1)
    acc_ref[...] = jnp.zeros_like(acc_ref)
    for e in range(N_EXPERTS):
        if e + 1 < N_EXPERTS:
            window_copy(e + 1).start()
        window_copy(e).wait()
        slot = win_ref[c * N_EXPERTS + e] + lane
        in0 = sel[:, 0:1] == float(e)
        in1 = sel[:, 1:2] == float(e)
        hit = ((pos[:, 0:1] == slot) & in0) | ((pos[:, 1:2] == slot) & in1)
        rows = _dot(jnp.where(hit, 1.0, 0.0).astype(BF16), buf_ref[e % 2])
        weight = jnp.where(in0, sel[:, 2:3], 0.0) + jnp.where(in1, sel[:, 3:4], 0.0)
        acc_ref[...] += weight * rows
    o_ref[...] = _residual_out(x_ref[...], mod_ref[0][5:6], acc_ref[...], final_norm, fg_ref[...])


def _combine_call(ys, sel, route, x, mod_l, final_g, first_tile, final_norm):
    n_tok = x.shape[0]
    rows = lambda width: pl.BlockSpec((MOE_SRC, width), lambda c, win: (c, 0))
    grid_spec = pltpu.PrefetchScalarGridSpec(
        num_scalar_prefetch=1,
        grid=(n_tok // MOE_SRC,),
        in_specs=[
            pl.BlockSpec(memory_space=pl.ANY),
            rows(TOP_K),
            rows(LANES),
            rows(D_MODEL),
            pl.BlockSpec((1, ADA_CHUNKS, D_MODEL), lambda c, win: (_tile_mod_row(c, first_tile, MOE_SRC), 0, 0)),
            pl.BlockSpec((1, D_MODEL), lambda c, win: (0, 0)),
        ],
        out_specs=rows(D_MODEL),
        scratch_shapes=[pltpu.VMEM((2, MOE_CMB_WIN, D_MODEL), BF16), pltpu.SemaphoreType.DMA((2,)),
                        pltpu.VMEM((MOE_SRC, D_MODEL), F32)],
    )
    return pl.pallas_call(
        functools.partial(_combine_kernel, final_norm=final_norm),
        grid_spec=grid_spec,
        out_shape=jax.ShapeDtypeStruct((n_tok, D_MODEL), F32),
        compiler_params=_params("arbitrary"),
        name="moe_combine",
    )(route["win"], ys, route["pos_cols"], sel, x, mod_l, final_g)


def _rope_tables():
    t = np.arange(SEQ)

    def angles(rot_dim):
        half = rot_dim // 2
        inv = ROPE_THETA ** (-jnp.arange(0, half, 2, dtype=F32) / half)
        ang = jnp.concatenate([jnp.asarray(t // GRID_W, F32)[:, None] * inv[None, :],
                               jnp.asarray(t % GRID_W, F32)[:, None] * inv[None, :]], axis=-1)
        return jnp.cos(ang), jnp.sin(ang)

    def pad_rows(a, fill):
        return jnp.concatenate([jnp.full((ROW_TILE, a.shape[1]), fill, F32), a], axis=0)

    c, s = angles(HEAD_DIM)
    cos64 = jnp.tile(jnp.concatenate([c, c], axis=-1), (1, 4))
    sin64 = jnp.tile(jnp.concatenate([s, s], axis=-1), (1, 4))
    c, s = angles(MLA_ROPE)
    one = jnp.ones((SEQ, MLA_NOPE), F32)
    zero = jnp.zeros((SEQ, MLA_NOPE), F32)
    tail1 = jnp.ones((SEQ, MLA_PAD - MLA_NOPE - MLA_ROPE), F32)
    tail0 = jnp.zeros((SEQ, MLA_PAD - MLA_NOPE - MLA_ROPE), F32)
    cosm = jnp.tile(jnp.concatenate([one, c, c, tail1], axis=-1), (1, MLA_HEADS))
    sinm = jnp.tile(jnp.concatenate([zero, s, s, tail0], axis=-1), (1, MLA_HEADS))
    blockdiag = jnp.asarray(np.kron(np.eye(4), np.ones((HEAD_DIM, HEAD_DIM))), BF16)
    place = np.zeros((256, MLA_HEADS * MLA_PAD), np.float32)
    for h in range(MLA_HEADS):
        for r in range(MLA_ROPE):
            place[MLA_KV_RANK + r, h * MLA_PAD + MLA_NOPE + r] = 1.0
    return {
        "cos64": pad_rows(cos64, 1.0), "sin64": pad_rows(sin64, 0.0),
        "cosm": pad_rows(cosm, 1.0), "sinm": pad_rows(sinm, 0.0),
        "blockdiag": blockdiag, "pe_place": jnp.asarray(place, BF16),
    }


def _pack_w_in(w):
    n_qkv = COL_DKVA + MLA_KV_RANK + MLA_ROPE
    pad = jnp.zeros(w.shape[:2] + (COL_GATE - n_qkv,), BF16)
    return jnp.concatenate([w[..., :n_qkv].astype(BF16), pad, w[..., n_qkv:].astype(BF16)], axis=-1)


def _pack_mla(wqb, wkvb):
    dqh = MLA_NOPE + MLA_ROPE
    q = wqb.reshape(MLA_Q_RANK, MLA_HEADS, dqh)
    q = jnp.pad(q, ((0, 0), (0, 0), (0, MLA_PAD - dqh))).reshape(MLA_Q_RANK, MLA_HEADS * MLA_PAD)
    kv = wkvb.reshape(MLA_KV_RANK, MLA_HEADS, MLA_NOPE + MLA_V)
    k = jnp.pad(kv[:, :, :MLA_NOPE], ((0, 0), (0, 0), (0, MLA_PAD - MLA_NOPE)))
    k = k.reshape(MLA_KV_RANK, MLA_HEADS * MLA_PAD)
    v = kv[:, :, MLA_NOPE:].reshape(MLA_KV_RANK, MLA_HEADS * MLA_V)
    return q.astype(BF16), k.astype(BF16), v.astype(BF16)


def _split2_host(w):
    hi = w.astype(BF16)
    return jnp.stack([hi, (w - hi.astype(F32)).astype(BF16)])


def kernel(x, c, ctx, c_ctx, norm1_g, norm2_g, w_ada, b_ada, w_in, na_rpb, gb_qnorm, gb_knorm, wc_sink,
           mla_qnorm, mla_kvnorm, mla_wqb, mla_wkvb, w_branch, w_out, ffn_w1, ffn_w3, ffn_w2,
           moe_router, moe_w1, moe_w3, moe_w2, final_g):
    n_batch = x.shape[0]
    assert x.shape[1:] == (SEQ, D_MODEL) and ctx.shape[1:] == (CTX_LEN, D_MODEL)
    assert n_batch * CTX_LEN <= CTX_ROWS and n_batch <= 8

    ctx_rows = ctx.reshape(n_batch * CTX_LEN, D_MODEL)
    if ctx_rows.shape[0] < CTX_ROWS:
        ctx_rows = jnp.pad(ctx_rows, ((0, CTX_ROWS - ctx_rows.shape[0]), (0, 0)))
    xt = jnp.concatenate([ctx_rows, x.reshape(n_batch * SEQ, D_MODEL)], axis=0)

    cvec = jnp.zeros((16, D_MODEL), F32).at[:n_batch].set(c).at[8].set(c_ctx)
    mod = _mod_call(cvec, w_ada, b_ada).reshape(DEPTH, 16, ADA_CHUNKS, D_MODEL)
    tabs = _rope_tables()
    w_in_p = _pack_w_in(w_in)
    mla_scale = (MLA_NOPE + MLA_ROPE) ** -0.5
    lat_tile0 = CTX_ROWS // ROW_TILE

    for l in range(DEPTH):
        with_ctx = l < DEPTH - 1
        wq, wk, wv = _pack_mla(mla_wqb[l], mla_wkvb[l])
        lw = {
            "gq": jnp.tile(gb_qnorm[l], 4)[None, :], "gk": jnp.tile(gb_knorm[l], 2)[None, :],
            "qn": mla_qnorm[l][None, :], "kvn": mla_kvnorm[l][None, :], "wqb": wq, "wk": wk, "wv": wv,
        }
        p, gate = _inproj_call(xt, mod[l], norm1_g[l][None, :], w_in_p, l)
        qb, kb, qc, kc, qd, kd, vd = _prep_call(p, tabs, lw)

        oa = _natten_call(n_batch, with_ctx, p, _natten_bias_table(na_rpb[l]))
        ob = _dense_attn_call(n_batch, with_ctx, (qb, 256, 0), (kb, 128, 0), (p, 128, COL_BV),
                              n_kv=GB_KV_HEADS, group=GB_Q_HEADS // GB_KV_HEADS, dq=HEAD_DIM, dv=HEAD_DIM,
                              name="attn_global")
        oc = _window_attn_call(n_batch, with_ctx, (qc, 256, 0), (kc, 128, 0), (p, 128, COL_CV), wc_sink[l])
        od = _dense_attn_call(n_batch, with_ctx, (qd, MLA_HEADS * MLA_PAD, 0), (kd, MLA_HEADS * MLA_PAD, 0),
                              (vd, 256, 0), n_kv=MLA_HEADS, group=1, dq=MLA_PAD, dv=MLA_V, scale=mla_scale,
                              name="attn_mla")

        is_moe = l % 2 == 1
        router3 = None
        if is_moe:
            router3 = _split2_host(jnp.pad(moe_router[l // 2], ((0, 0), (0, LANES - N_EXPERTS))))
        first_tile = 0 if with_ctx else lat_tile0
        res = _merge_call((oa, ob, oc, od), gate, xt, mod[l], norm2_g[l][None, :],
                          w_branch[l].astype(BF16), w_out[l].astype(BF16), router3, first_tile)
        x_mid, h2 = res[0], res[1]
        final_norm = l == DEPTH - 1
        if is_moe:
            w1, w3, w2 = (w[l // 2].astype(BF16) for w in (moe_w1, moe_w3, moe_w2))
            route = _moe_route(res[2], h2.shape[0])
            xs = _dispatch_call(h2, route)
            ys = _expert_call(xs, route, w1, w3, w2)
            xt = _combine_call(ys, res[2], route, x_mid, mod[l], final_g[None, :],
                               0 if with_ctx else CTX_ROWS // MOE_SRC, final_norm)
        else:
            w1, w3, w2 = (w[l // 2].astype(BF16) for w in (ffn_w1, ffn_w3, ffn_w2))
            xt = _ffn_call(h2, w1, w3, w2, x_mid, mod[l], final_g[None, :],
                           0 if with_ctx else CTX_ROWS // FFN_ROWS, final_norm)

    return xt.reshape(n_batch, SEQ, D_MODEL)
```

```python
import functools

import numpy as np
import jax
import jax.numpy as jnp
from jax import lax
from jax.experimental import pallas as pl
from jax.experimental.pallas import tpu as pltpu

F32 = jnp.float32
BF16 = jnp.bfloat16

D_MODEL = 1024
SEQ = 2048
DEPTH = 4
CTX_LEN = 256
GRID_W = 64
GRID_ROWS = SEQ // GRID_W
HEAD_DIM = 64
ROPE_THETA = 10000.0
NORM_EPS = 1e-6
NEG_INF = -1e30

NA_HEADS = 4
NA_KH = 8
NA_KW = 16
GB_Q_HEADS = 4
GB_KV_HEADS = 2
WC_Q_HEADS = 4
WC_KV_HEADS = 2
WC_WINDOW = 128
MLA_HEADS = 4
MLA_Q_RANK = 256
MLA_KV_RANK = 128
MLA_NOPE = 64
MLA_ROPE = 32
MLA_V = 64
MLA_PAD = 128
N_BRANCH = 4
BRANCH_W = 256
D_FF = 3584
N_EXPERTS = 8
ADA_CHUNKS = 6

VMEM_LIMIT_BYTES = 56 * 1024 * 1024
LANES = 128

CTX_ROWS = 2048
ROW_TILE = 512
Q_TILE = 256
NA_Q_ROWS = Q_TILE // GRID_W
NA_WIN_ROWS = 12
NA_WIN = NA_WIN_ROWS * GRID_W
WC_WIN = Q_TILE + 2 * WC_WINDOW

COL_AQ, COL_AK, COL_AV = 0, 256, 512
COL_BQ, COL_BK, COL_BV = 768, 1024, 1152
COL_CQ, COL_CK, COL_CV = 1280, 1536, 1664
COL_DQA, COL_DKVA, COL_GATE = 1792, 2048, 2304
GATE_W = N_BRANCH * D_MODEL
P_WIDTH = COL_GATE + GATE_W
QKV_CHUNK = COL_GATE // 2
FF_CHUNK = 512


def _params(*sem):
    return pltpu.CompilerParams(dimension_semantics=sem, vmem_limit_bytes=VMEM_LIMIT_BYTES)


def _dot(a, b):
    return jnp.dot(a, b, preferred_element_type=F32)


def _dot_nt(a, b):
    return lax.dot_general(a, b, (((1,), (1,)), ((), ())), preferred_element_type=F32)


def _split3(x):
    hi = x.astype(BF16)
    r1 = x - hi.astype(F32)
    mid = r1.astype(BF16)
    lo = (r1 - mid.astype(F32)).astype(BF16)
    return hi, mid, lo


def _sigmoid(x):
    return 1.0 / (1.0 + jnp.exp(-x))


ADA_TN = 1536


def _mod_kernel(c_ref, w_ref, b_ref, o_ref):
    c = c_ref[...]
    sc = (c * _sigmoid(c)).astype(BF16)
    o_ref[0] = _dot(sc, w_ref[0].astype(BF16)) + b_ref[0]


def _mod_call(cvec, w_ada, b_ada):
    n = ADA_CHUNKS * D_MODEL
    return pl.pallas_call(
        _mod_kernel,
        grid=(DEPTH, n // ADA_TN),
        in_specs=[
            pl.BlockSpec((16, D_MODEL), lambda l, j: (0, 0)),
            pl.BlockSpec((1, D_MODEL, ADA_TN), lambda l, j: (l, 0, j)),
            pl.BlockSpec((1, 1, ADA_TN), lambda l, j: (l, 0, j)),
        ],
        out_specs=pl.BlockSpec((1, 16, ADA_TN), lambda l, j: (l, 0, j)),
        out_shape=jax.ShapeDtypeStruct((DEPTH, 16, n), F32),
        compiler_params=_params("arbitrary", "arbitrary"),
        name="adaln_mod",
    )(cvec, w_ada, b_ada.reshape(DEPTH, 1, n))


def _mod_row(i, first_tile):
    n_ctx = CTX_ROWS // ROW_TILE
    t = i + first_tile
    return jnp.where(t < n_ctx, 8, (t - n_ctx) // (SEQ // ROW_TILE))


def _norm_mod(x, g, shift, scale):
    ms = jnp.mean(x * x, axis=-1, keepdims=True)
    y = x * lax.rsqrt(ms + NORM_EPS) * g
    return y * (1.0 + scale) + shift


def _inproj_kernel(x_ref, mod_ref, g_ref, w_ref, p_ref, gate_ref):
    m = mod_ref[0]
    hb = _norm_mod(x_ref[...], g_ref[...], m[0:1], m[1:2]).astype(BF16)
    for c0 in range(0, COL_GATE, QKV_CHUNK):
        p_ref[:, c0:c0 + QKV_CHUNK] = _dot(hb, w_ref[0, :, c0:c0 + QKV_CHUNK]).astype(BF16)
    for c0 in range(0, GATE_W, D_MODEL):
        gate_ref[:, c0:c0 + D_MODEL] = _dot(hb, w_ref[0, :, COL_GATE + c0:COL_GATE + c0 + D_MODEL]).astype(BF16)


def _inproj_call(x, mod_l, g, w_in_p, layer):
    t = x.shape[0]
    return pl.pallas_call(
        _inproj_kernel,
        grid=(t // ROW_TILE,),
        in_specs=[
            pl.BlockSpec((ROW_TILE, D_MODEL), lambda i: (i, 0)),
            pl.BlockSpec((1, ADA_CHUNKS, D_MODEL), lambda i: (_mod_row(i, 0), 0, 0)),
            pl.BlockSpec((1, D_MODEL), lambda i: (0, 0)),
            pl.BlockSpec((1, D_MODEL, P_WIDTH), lambda i: (layer, 0, 0), pipeline_mode=pl.Buffered(1)),
        ],
        out_specs=[pl.BlockSpec((ROW_TILE, COL_GATE), lambda i: (i, 0)),
                   pl.BlockSpec((ROW_TILE, GATE_W), lambda i: (i, 0))],
        out_shape=[jax.ShapeDtypeStruct((t, COL_GATE), BF16), jax.ShapeDtypeStruct((t, GATE_W), BF16)],
        compiler_params=_params("arbitrary"),
        name="inproj",
    )(x, mod_l, g, w_in_p)


def _rope(x, cos, sin, half, first_mask):
    w = x.shape[-1]
    fwd = pltpu.roll(x, w - half, 1)
    bwd = pltpu.roll(x, half, 1)
    rot = jnp.where(first_mask, -fwd, bwd)
    return x * cos + rot * sin


def _head_rms(x, gain, blockdiag):
    hi, mid, lo = _split3(x * x)
    ss = _dot(hi, blockdiag) + _dot(mid, blockdiag) + _dot(lo, blockdiag)
    return x * lax.rsqrt(ss * (1.0 / HEAD_DIM) + NORM_EPS) * gain


def _prep_kernel(bq_ref, bk_ref, cq_ref, ck_ref, dqa_ref, dkva_ref,
                 cos64_ref, sin64_ref, cosm_ref, sinm_ref,
                 gq_ref, gk_ref, bd_ref, qn_ref, kvn_ref, wqb_ref, wk_ref, wv_ref, pe_ref,
                 qb_ref, kb_ref, qc_ref, kc_ref, qd_ref, kd_ref, vd_ref):
    scale = HEAD_DIM ** -0.5
    cos64, sin64 = cos64_ref[...], sin64_ref[...]
    lane = lax.broadcasted_iota(jnp.int32, (1, 256), 1)
    first64 = (lane % HEAD_DIM) < (HEAD_DIM // 2)
    bd = bd_ref[...]

    bq = _head_rms(bq_ref[...].astype(F32), gq_ref[...], bd)
    qb_ref[...] = (_rope(bq, cos64, sin64, HEAD_DIM // 2, first64) * scale).astype(BF16)
    bk = _head_rms(bk_ref[...].astype(F32), gk_ref[...], bd[:128, :128])
    kb_ref[...] = _rope(bk, cos64[:, :128], sin64[:, :128], HEAD_DIM // 2, first64[:, :128]).astype(BF16)
    qc_ref[...] = (_rope(cq_ref[...].astype(F32), cos64, sin64, HEAD_DIM // 2, first64) * scale).astype(BF16)
    kc_ref[...] = _rope(ck_ref[...].astype(F32), cos64[:, :128], sin64[:, :128], HEAD_DIM // 2,
                        first64[:, :128]).astype(BF16)

    cosm, sinm = cosm_ref[...], sinm_ref[...]
    lane_m = lax.broadcasted_iota(jnp.int32, (1, MLA_HEADS * MLA_PAD), 1) % MLA_PAD
    first_m = lane_m < (MLA_NOPE + MLA_ROPE // 2)
    dqa = dqa_ref[...].astype(F32)
    qn = dqa * lax.rsqrt(jnp.mean(dqa * dqa, axis=-1, keepdims=True) + NORM_EPS) * qn_ref[...]
    dq = _dot(qn.astype(BF16), wqb_ref[...])
    qd_ref[...] = _rope(dq, cosm, sinm, MLA_ROPE // 2, first_m).astype(BF16)

    dkva = dkva_ref[...]
    kvc = dkva[:, :MLA_KV_RANK].astype(F32)
    kvn = (kvc * lax.rsqrt(jnp.mean(kvc * kvc, axis=-1, keepdims=True) + NORM_EPS) * kvn_ref[...]).astype(BF16)
    dk = _dot(kvn, wk_ref[...]) + _dot(dkva, pe_ref[...])
    kd_ref[...] = _rope(dk, cosm, sinm, MLA_ROPE // 2, first_m).astype(BF16)
    vd_ref[...] = _dot(kvn, wv_ref[...]).astype(BF16)


def _rope_row_block(i):
    n_ctx = CTX_ROWS // ROW_TILE
    return jnp.where(i < n_ctx, 0, 1 + (i - n_ctx) % (SEQ // ROW_TILE))


def _prep_call(p, tabs, lw):
    t = p.shape[0]
    mw = MLA_HEADS * MLA_PAD

    def pcol(width, col):
        return pl.BlockSpec((ROW_TILE, width), lambda i: (i, col // width))

    def full(shape):
        return pl.BlockSpec(shape, lambda i: (0,) * len(shape))

    def rows(width):
        return pl.BlockSpec((ROW_TILE, width), lambda i: (i, 0))

    def tab(width):
        return pl.BlockSpec((ROW_TILE, width), lambda i: (_rope_row_block(i), 0))

    outs = [(256, BF16), (128, BF16), (256, BF16), (128, BF16), (mw, BF16), (mw, BF16), (256, BF16)]
    return pl.pallas_call(
        _prep_kernel,
        grid=(t // ROW_TILE,),
        in_specs=[
            pcol(256, COL_BQ), pcol(128, COL_BK), pcol(256, COL_CQ), pcol(128, COL_CK),
            pcol(256, COL_DQA), pcol(256, COL_DKVA),
            tab(256), tab(256), tab(mw), tab(mw),
            full((1, 256)), full((1, 128)), full((256, 256)), full((1, MLA_Q_RANK)), full((1, MLA_KV_RANK)),
            full((MLA_Q_RANK, mw)), full((MLA_KV_RANK, mw)), full((MLA_KV_RANK, 256)), full((256, mw)),
        ],
        out_specs=[rows(w) for w, _ in outs],
        out_shape=[jax.ShapeDtypeStruct((t, w), dt) for w, dt in outs],
        compiler_params=_params("arbitrary"),
        name="mixer_prep",
    )(p, p, p, p, p, p, tabs["cos64"], tabs["sin64"], tabs["cosm"], tabs["sinm"],
      lw["gq"], lw["gk"], tabs["blockdiag"], lw["qn"], lw["kvn"], lw["wqb"], lw["wk"], lw["wv"], tabs["pe_place"])


def _attend(q, segs, sink=None, scale=None):
    scores = []
    for k, _, bias in segs:
        s = _dot_nt(q, k)
        if scale is not None:
            s = s * scale
        if bias is not None:
            s = s + bias
        scores.append(s)
    m = functools.reduce(jnp.maximum, [jnp.max(s, axis=-1, keepdims=True) for s in scores])
    if sink is not None:
        m = jnp.maximum(m, sink)
    denom = None
    out = None
    for s, (_, v, _) in zip(scores, segs):
        p = jnp.exp(s - m)
        ps = jnp.sum(p, axis=-1, keepdims=True)
        pv = _dot(p.astype(BF16), v)
        denom = ps if denom is None else denom + ps
        out = pv if out is None else out + pv
    if sink is not None:
        denom = denom + jnp.exp(sink - m)
    return out / denom


def _gqa_heads(q_ref, o_ref, sink_ref, n_kv, group, dq, dv, scale, seg_fn):
    tq = q_ref.shape[0]
    for h in range(n_kv):
        heads = [h * group + g for g in range(group)]
        q = jnp.concatenate([q_ref[:, a * dq:(a + 1) * dq] for a in heads], axis=0) if group > 1 \
            else q_ref[:, h * dq:(h + 1) * dq]
        sink = None
        if sink_ref is not None:
            sink = jnp.concatenate([jnp.full((tq, 1), sink_ref[a], F32) for a in heads], axis=0)
        o = _attend(q, seg_fn(h), sink=sink, scale=scale)
        for g, a in enumerate(heads):
            o_ref[:, a * dv:(a + 1) * dv] = o[g * tq:(g + 1) * tq].astype(BF16)


def _dense_attn_kernel(*refs, n_kv, group, dq, dv, scale, has_sink, ctx_tile, only_ctx_keys):
    if has_sink:
        sink_ref, refs = refs[0], refs[1:]
    else:
        sink_ref = None
    q_ref, kl_ref, vl_ref, kc_ref, vc_ref, o_ref = refs

    def run(with_latent):
        def segs(h):
            out = []
            if with_latent:
                out.append((kl_ref[:, h * dq:(h + 1) * dq], vl_ref[:, h * dv:(h + 1) * dv], None))
            out.append((kc_ref[:, h * dq:(h + 1) * dq], vc_ref[:, h * dv:(h + 1) * dv], None))
            return out
        _gqa_heads(q_ref, o_ref, sink_ref, n_kv, group, dq, dv, scale, segs)

    if only_ctx_keys:
        run(False)
    elif ctx_tile:
        j = pl.program_id(1)
        pl.when(j == 0)(lambda: run(False))
        pl.when(j > 0)(lambda: run(True))
    else:
        run(True)


def _q_row_block(b, j, n_batch, with_ctx):
    per_batch = SEQ // Q_TILE
    lat0 = CTX_ROWS // Q_TILE
    if with_ctx:
        return jnp.where(j == 0, b, lat0 + b * per_batch + j - 1)
    return lat0 + b * per_batch + j


def _attn_specs(n_batch, with_ctx, q, k, v, o_width, order_bj=True):
    def ix(f):
        return (lambda b, j: f(b, j)) if order_bj else (lambda j, b: f(b, j))

    def qspec(width, col):
        return pl.BlockSpec((Q_TILE, width), ix(lambda b, j: (_q_row_block(b, j, n_batch, with_ctx), col // width)))

    def lat(width, col):
        return pl.BlockSpec((SEQ, width), ix(lambda b, j: (CTX_ROWS // SEQ + b, col // width)))

    def ctx(width, col):
        return pl.BlockSpec((CTX_LEN, width), ix(lambda b, j: (b, col // width)))

    in_specs = [qspec(q[1], q[2]), lat(k[1], k[2]), lat(v[1], v[2]), ctx(k[1], k[2]), ctx(v[1], v[2])]
    args = [q[0], k[0], v[0], k[0], v[0]]
    if with_ctx:
        return in_specs, args, qspec(o_width, 0)
    out_spec = pl.BlockSpec((Q_TILE, o_width), ix(lambda b, j: (b * (SEQ // Q_TILE) + j, 0)))
    return in_specs, args, out_spec


def _attn_out_rows(t, with_ctx):
    return t if with_ctx else t - CTX_ROWS


def _dense_attn_call(n_batch, with_ctx, q, k, v, *, n_kv, group, dq, dv, scale=None, sink=None, name):
    t = q[0].shape[0]
    in_specs, args, out_spec = _attn_specs(n_batch, with_ctx, q, k, v, 256)
    if sink is not None:
        in_specs = [pl.BlockSpec(memory_space=pltpu.SMEM)] + in_specs
        args = [sink] + args
    kern = functools.partial(_dense_attn_kernel, n_kv=n_kv, group=group, dq=dq, dv=dv, scale=scale,
                             has_sink=sink is not None, ctx_tile=with_ctx, only_ctx_keys=False)
    return pl.pallas_call(
        kern,
        grid=(n_batch, SEQ // Q_TILE + (1 if with_ctx else 0)),
        in_specs=in_specs,
        out_specs=out_spec,
        out_shape=jax.ShapeDtypeStruct((_attn_out_rows(t, with_ctx), 256), BF16),
        compiler_params=_params("arbitrary", "arbitrary"),
        name=name,
    )(*args)


def _window_attn_kernel(sink_ref, q_ref, kl_ref, vl_ref, kc_ref, vc_ref, o_ref, *, ctx_tile):
    j = pl.program_id(1)
    hd = HEAD_DIM
    group = WC_Q_HEADS // WC_KV_HEADS

    def ctx_run():
        def segs(h):
            return [(kc_ref[:, h * hd:(h + 1) * hd], vc_ref[:, h * hd:(h + 1) * hd], None)]
        _gqa_heads(q_ref, o_ref, sink_ref, WC_KV_HEADS, group, hd, hd, None, segs)

    def lat_run():
        i = j - 1 if ctx_tile else j
        start = pl.multiple_of(jnp.clip(i * Q_TILE - WC_WINDOW, 0, SEQ - WC_WIN), WC_WINDOW)
        qpos = i * Q_TILE + lax.broadcasted_iota(jnp.int32, (group * Q_TILE, WC_WIN), 0) % Q_TILE
        kpos = start + lax.broadcasted_iota(jnp.int32, (group * Q_TILE, WC_WIN), 1)
        bias = jnp.where(jnp.abs(qpos - kpos) <= WC_WINDOW, 0.0, NEG_INF).astype(F32)

        def segs(h):
            return [(kl_ref[pl.ds(start, WC_WIN), h * hd:(h + 1) * hd],
                     vl_ref[pl.ds(start, WC_WIN), h * hd:(h + 1) * hd], bias),
                    (kc_ref[:, h * hd:(h + 1) * hd], vc_ref[:, h * hd:(h + 1) * hd], None)]
        _gqa_heads(q_ref, o_ref, sink_ref, WC_KV_HEADS, group, hd, hd, None, segs)

    if ctx_tile:
        pl.when(j == 0)(ctx_run)
        pl.when(j > 0)(lat_run)
    else:
        lat_run()


def _window_attn_call(n_batch, with_ctx, q, k, v, sink):
    t = q[0].shape[0]
    in_specs, args, out_spec = _attn_specs(n_batch, with_ctx, q, k, v, 256)
    return pl.pallas_call(
        functools.partial(_window_attn_kernel, ctx_tile=with_ctx),
        grid=(n_batch, SEQ // Q_TILE + (1 if with_ctx else 0)),
        in_specs=[pl.BlockSpec(memory_space=pltpu.SMEM)] + in_specs,
        out_specs=out_spec,
        out_shape=jax.ShapeDtypeStruct((_attn_out_rows(t, with_ctx), 256), BF16),
        compiler_params=_params("arbitrary", "arbitrary"),
        name="attn_window",
    )(sink, *args)


def _natten_kernel(q_ref, kl_ref, vl_ref, kc_ref, vc_ref, bias_ref, o_ref, *, ctx_tile):
    j = pl.program_id(0)
    hd = HEAD_DIM
    scale = HEAD_DIM ** -0.5

    def ctx_run():
        for h in range(NA_HEADS):
            sl = slice(h * hd, (h + 1) * hd)
            o = _attend(q_ref[:, sl] * scale, [(kc_ref[:, sl], vc_ref[:, sl], None)])
            o_ref[:, sl] = o.astype(BF16)

    def lat_run():
        i = j - 1 if ctx_tile else j
        ws = jnp.clip(NA_Q_ROWS * i - NA_KH // 2, 0, GRID_ROWS - NA_WIN_ROWS)
        start = pl.multiple_of(ws * GRID_W, GRID_W)
        for h in range(NA_HEADS):
            sl = slice(h * hd, (h + 1) * hd)
            segs = [(kl_ref[pl.ds(start, NA_WIN), sl], vl_ref[pl.ds(start, NA_WIN), sl], bias_ref[0, h]),
                    (kc_ref[:, sl], vc_ref[:, sl], None)]
            o_ref[:, sl] = _attend(q_ref[:, sl] * scale, segs).astype(BF16)

    if ctx_tile:
        pl.when(j == 0)(ctx_run)
        pl.when(j > 0)(lat_run)
    else:
        lat_run()


_NA_VARIANT = (0, 1, 2, 2, 2, 2, 3, 4)
_NA_VARIANT_TILE = (0, 1, 2, 6, 7)


def _natten_call(n_batch, with_ctx, p, bias_tab):
    t = p.shape[0]
    q, k, v = (p, 256, COL_AQ), (p, 256, COL_AK), (p, 256, COL_AV)
    in_specs, args, out_spec = _attn_specs(n_batch, with_ctx, q, k, v, 256, order_bj=False)

    def bias_ix(j, b):
        i = jnp.maximum(j - 1, 0) if with_ctx else j
        v_ix = sum(jnp.where(i == n, _NA_VARIANT[n], 0) for n in range(len(_NA_VARIANT)))
        return (v_ix, 0, 0, 0)

    in_specs.append(pl.BlockSpec((1, NA_HEADS, Q_TILE, NA_WIN), bias_ix))
    return pl.pallas_call(
        functools.partial(_natten_kernel, ctx_tile=with_ctx),
        grid=(SEQ // Q_TILE + (1 if with_ctx else 0), n_batch),
        in_specs=in_specs,
        out_specs=out_spec,
        out_shape=jax.ShapeDtypeStruct((_attn_out_rows(t, with_ctx), 256), BF16),
        compiler_params=_params("arbitrary", "arbitrary"),
        name="attn_natten",
    )(*args, bias_tab)


def _natten_bias_table(rpb):
    n_dr, n_dc = 2 * NA_KH - 1, 2 * NA_KW - 1
    col = np.arange(GRID_W)
    dc = np.clip(col[None, :] - col[:, None] + NA_KW - 1, 0, n_dc - 1)
    onehot = jnp.asarray(dc[None] == np.arange(n_dc)[:, None, None], F32)
    blocks = jnp.einsum('hdc,cqk->hdqk', rpb.astype(F32), onehot, precision=lax.Precision.HIGHEST)
    cstart = np.clip(col - NA_KW // 2, 0, GRID_W - NA_KW)
    col_ok = (col[None, :] >= cstart[:, None]) & (col[None, :] < cstart[:, None] + NA_KW)
    blocks = jnp.where(col_ok[None, None], blocks, NEG_INF)
    blocks = jnp.concatenate([blocks, jnp.full((NA_HEADS, 1, GRID_W, GRID_W), NEG_INF, F32)], axis=1)
    tiles = np.asarray(_NA_VARIANT_TILE)
    r = NA_Q_ROWS * tiles[:, None] + np.arange(NA_Q_ROWS)[None, :]
    ws = np.clip(NA_Q_ROWS * tiles - NA_KH // 2, 0, GRID_ROWS - NA_WIN_ROWS)
    kr = ws[:, None] + np.arange(NA_WIN_ROWS)[None, :]
    krow0 = np.clip(r - NA_KH // 2, 0, GRID_ROWS - NA_KH)
    row_ok = (kr[:, None, :] >= krow0[:, :, None]) & (kr[:, None, :] < krow0[:, :, None] + NA_KH)
    idx = np.where(row_ok, kr[:, None, :] - r[:, :, None] + NA_KH - 1, n_dr)
    tab = jnp.take(blocks, jnp.asarray(idx.reshape(-1), jnp.int32), axis=1)
    tab = tab.reshape(NA_HEADS, len(tiles), NA_Q_ROWS, NA_WIN_ROWS, GRID_W, GRID_W)
    tab = jnp.transpose(tab, (1, 0, 2, 4, 3, 5))
    return tab.reshape(len(tiles), NA_HEADS, Q_TILE, NA_WIN)


def _merge_kernel(*refs, with_router):
    if with_router:
        (oa_ref, ob_ref, oc_ref, od_ref, gate_ref, x_ref, mod_ref, g2_ref, wb_ref, wo_ref, rt_ref,
         xo_ref, h2_ref, comb_ref) = refs
    else:
        (oa_ref, ob_ref, oc_ref, od_ref, gate_ref, x_ref, mod_ref, g2_ref, wb_ref, wo_ref,
         xo_ref, h2_ref) = refs
    m = mod_ref[0]
    acc = None
    for n, o_ref in enumerate((oa_ref, ob_ref, oc_ref, od_ref)):
        y = _dot(o_ref[...], wb_ref[n])
        g = _sigmoid(gate_ref[:, n * D_MODEL:(n + 1) * D_MODEL].astype(F32))
        acc = g * y if acc is None else acc + g * y
    mix = _dot(acc.astype(BF16), wo_ref[...])
    x = x_ref[...] + m[2:3] * mix
    xo_ref[...] = x
    h2 = _norm_mod(x, g2_ref[...], m[3:4], m[4:5])
    h2_ref[...] = h2.astype(BF16)
    if with_router:
        hh, hm, _ = _split3(h2)
        rh, rm = rt_ref[0], rt_ref[1]
        logits = _dot(hh, rh) + (_dot(hh, rm) + _dot(hm, rh))
        lane = lax.broadcasted_iota(jnp.int32, logits.shape, 1).astype(F32)
        logits = jnp.where(lane < N_EXPERTS, logits, NEG_INF)
        m1 = jnp.max(logits, axis=-1, keepdims=True)
        i1 = jnp.min(jnp.where(logits == m1, lane, float(LANES)), axis=-1, keepdims=True)
        rest = jnp.where(lane == i1, NEG_INF, logits)
        m2 = jnp.max(rest, axis=-1, keepdims=True)
        i2 = jnp.min(jnp.where(rest == m2, lane, float(LANES)), axis=-1, keepdims=True)
        e = jnp.exp(m2 - m1)
        w1 = 1.0 / (1.0 + e)
        w2 = e / (1.0 + e)
        comb_ref[...] = (jnp.where(lane == 0.0, i1, 0.0) + jnp.where(lane == 1.0, i2, 0.0)
                         + jnp.where(lane == 2.0, w1, 0.0) + jnp.where(lane == 3.0, w2, 0.0))


def _merge_call(outs, gate, x, mod_l, g2, wb, wo, router3, first_tile):
    t = x.shape[0]
    n_rows = t - first_tile * ROW_TILE

    def rows(width, col=0):
        return pl.BlockSpec((ROW_TILE, width), lambda i: (i + first_tile, col // width))

    def orow(width):
        return pl.BlockSpec((ROW_TILE, width), lambda i: (i, 0))

    def full(shape):
        return pl.BlockSpec(shape, lambda i: (0,) * len(shape))

    in_specs = [orow(256)] * 4 + [
        rows(GATE_W),
        rows(D_MODEL),
        pl.BlockSpec((1, ADA_CHUNKS, D_MODEL), lambda i: (_mod_row(i, first_tile), 0, 0)),
        full((1, D_MODEL)), full((N_BRANCH, BRANCH_W, D_MODEL)), full((D_MODEL, D_MODEL)),
    ]
    args = list(outs) + [gate, x, mod_l, g2, wb, wo]
    out_specs = [orow(D_MODEL), orow(D_MODEL)]
    out_shape = [jax.ShapeDtypeStruct((n_rows, D_MODEL), F32), jax.ShapeDtypeStruct((n_rows, D_MODEL), BF16)]
    if router3 is not None:
        in_specs.append(full((2, D_MODEL, LANES)))
        args.append(router3)
        out_specs.append(orow(LANES))
        out_shape.append(jax.ShapeDtypeStruct((n_rows, LANES), F32))
    return pl.pallas_call(
        functools.partial(_merge_kernel, with_router=router3 is not None),
        grid=(n_rows // ROW_TILE,),
        in_specs=in_specs,
        out_specs=out_specs,
        out_shape=out_shape,
        compiler_params=_params("arbitrary"),
        name="merge",
    )(*args)


FFN_ROWS = 1024


def _swiglu_chunk(h, w1, w3, w2):
    a = _dot(h, w1)
    b = _dot(h, w3)
    return _dot((a * _sigmoid(a) * b).astype(BF16), w2)


def _residual_out(x, gate, update, final_norm, final_g):
    x = x + gate * update
    if final_norm:
        ms = jnp.mean(x * x, axis=-1, keepdims=True)
        x = x * lax.rsqrt(ms + NORM_EPS) * final_g
    return x


def _ffn_kernel(h_ref, w1_ref, w3_ref, w2_ref, x_ref, mod_ref, fg_ref, o_ref, acc_ref, *, final_norm):
    f = pl.program_id(1)

    @pl.when(f == 0)
    def _():
        acc_ref[...] = jnp.zeros_like(acc_ref)

    acc_ref[...] += _swiglu_chunk(h_ref[...], w1_ref[...], w3_ref[...], w2_ref[...])

    @pl.when(f == pl.num_programs(1) - 1)
    def _():
        o_ref[...] = _residual_out(x_ref[...], mod_ref[0][5:6], acc_ref[...], final_norm, fg_ref[...])


def _tile_mod_row(i, first_tile, tile_rows):
    n_ctx = CTX_ROWS // tile_rows
    t = i + first_tile
    return jnp.where(t < n_ctx, 8, (t - n_ctx) // (SEQ // tile_rows))


def _ffn_call(h2, w1, w3, w2, x, mod_l, final_g, first_row_tile, final_norm):
    t = x.shape[0]
    rows = lambda width: pl.BlockSpec((FFN_ROWS, width), lambda i, f: (i, 0))
    return pl.pallas_call(
        functools.partial(_ffn_kernel, final_norm=final_norm),
        grid=(t // FFN_ROWS, D_FF // FF_CHUNK),
        in_specs=[
            rows(D_MODEL),
            pl.BlockSpec((D_MODEL, FF_CHUNK), lambda i, f: (0, f)),
            pl.BlockSpec((D_MODEL, FF_CHUNK), lambda i, f: (0, f)),
            pl.BlockSpec((FF_CHUNK, D_MODEL), lambda i, f: (f, 0)),
            rows(D_MODEL),
            pl.BlockSpec((1, ADA_CHUNKS, D_MODEL),
                         lambda i, f: (_tile_mod_row(i, first_row_tile, FFN_ROWS), 0, 0)),
            pl.BlockSpec((1, D_MODEL), lambda i, f: (0, 0)),
        ],
        out_specs=rows(D_MODEL),
        out_shape=jax.ShapeDtypeStruct((t, D_MODEL), F32),
        scratch_shapes=[pltpu.VMEM((FFN_ROWS, D_MODEL), F32)],
        compiler_params=_params("arbitrary", "arbitrary"),
        name="ffn",
    )(h2, w1, w3, w2, x, mod_l, final_g)


MOE_TILE = 512
MOE_SRC = 256
MOE_CMB_WIN = MOE_SRC + 16
MOE_DSP_SMALL = 144
MOE_DSP_LARGE = MOE_SRC + 16
TOP_K = 2


def _moe_route(sel, n_tok):
    n_tiles = TOP_K * n_tok // MOE_TILE + N_EXPERTS
    e = sel[:, 0:TOP_K].astype(jnp.int32)
    flat_e = e.reshape(-1)
    onehot = (flat_e[:, None] == jnp.arange(N_EXPERTS)[None, :]).astype(jnp.int32)
    csum = jnp.cumsum(onehot, axis=0)
    rank = jnp.sum((csum - onehot) * onehot, axis=1)
    counts = csum[-1]
    padded = (counts + MOE_TILE - 1) // MOE_TILE * MOE_TILE
    seg_end = jnp.cumsum(padded)
    seg_start = seg_end - padded
    pos = jnp.sum(onehot * seg_start[None, :], axis=1) + rank
    pos2 = pos.reshape(n_tok, TOP_K)
    tile_ix = jnp.arange(n_tiles)
    tile_expert = jnp.minimum(jnp.sum(tile_ix[:, None] >= (seg_end // MOE_TILE)[None, :], axis=1), N_EXPERTS - 1)
    n_valid = seg_end[-1] // MOE_TILE
    n_src = n_tok // MOE_SRC
    per_chunk = MOE_SRC * TOP_K
    before = jnp.concatenate([jnp.zeros((1, N_EXPERTS), jnp.int32), csum[per_chunk - 1::per_chunk]], axis=0)
    run_start = seg_start[None, :] + before[:-1]
    run_cnt = before[1:] - before[:-1]
    cum = jnp.take(before, tile_expert, axis=1)
    local0 = tile_ix * MOE_TILE - jnp.take(seg_start, tile_expert)
    local1 = jnp.minimum(local0 + MOE_TILE, jnp.take(counts, tile_expert))
    c_lo = jnp.sum(cum[1:] <= local0[None, :], axis=0)
    c_hi = jnp.sum(cum[:-1] < local1[None, :], axis=0) - 1
    win = jnp.minimum(run_start // 16 * 16, n_tiles * MOE_TILE - MOE_CMB_WIN)
    def token_rows(a, dtype):
        rows = jnp.transpose(a.reshape(n_src, MOE_SRC, TOP_K), (0, 2, 1)).astype(dtype)
        return jnp.zeros((n_src, 8, MOE_SRC), dtype).at[:, :TOP_K, :].set(rows)

    i32 = lambda a: a.astype(jnp.int32)
    return {
        "n_tiles": n_tiles, "tile_expert": i32(tile_expert), "n_valid": i32(n_valid).reshape(1),
        "c_lo": i32(c_lo), "c_hi": i32(c_hi), "run_start": i32(run_start.reshape(-1)),
        "run_cnt": i32(run_cnt.reshape(-1)), "win": i32(win.reshape(-1)),
        "run_off": i32((run_start - win).reshape(-1)),
        "pos_rows": token_rows(pos2, jnp.int32), "w_rows": token_rows(sel[:, TOP_K:2 * TOP_K], F32),
        "pos_cols": i32(pos2),
    }


def _dispatch_kernel(clo_ref, chi_ref, te_ref, rs_ref, rc_ref, h_ref, pos_ref, w_ref, o_ref, sw_ref, acc_ref):
    i = pl.program_id(0)
    base = i * MOE_TILE
    expert = te_ref[i]
    acc_ref[...] = jnp.zeros_like(acc_ref)
    sw_ref[...] = jnp.zeros_like(sw_ref)

    def body(c, carry):
        run0 = rs_ref[c * N_EXPERTS + expert]
        lo = jnp.maximum(run0, base) - base
        hi = jnp.minimum(run0 + rc_ref[c * N_EXPERTS + expert], base + MOE_TILE) - base
        pos = pos_ref[c]
        wts = w_ref[c]
        h = h_ref[pl.ds(pl.multiple_of(c * MOE_SRC, MOE_SRC), MOE_SRC), :]

        def window(rows):
            ws = pl.multiple_of(jnp.minimum(lo // 16 * 16, MOE_TILE - rows), 16)
            slot = base + ws + lax.broadcasted_iota(jnp.int32, (rows, MOE_SRC), 0)
            hit0 = pos[0:1, :] == slot
            hit1 = pos[1:2, :] == slot
            acc_ref[pl.ds(ws, rows), :] += _dot(jnp.where(hit0 | hit1, 1.0, 0.0).astype(BF16), h)
            weight = jnp.sum(jnp.where(hit0, wts[0:1, :], 0.0) + jnp.where(hit1, wts[1:2, :], 0.0),
                             axis=1, keepdims=True)
            sw_ref[pl.ds(ws, rows), :] += jnp.broadcast_to(weight, (rows, LANES))

        small = hi - jnp.minimum(lo // 16 * 16, MOE_TILE - MOE_DSP_SMALL) <= MOE_DSP_SMALL
        pl.when((hi > lo) & small)(lambda: window(MOE_DSP_SMALL))
        pl.when((hi > lo) & jnp.logical_not(small))(lambda: window(MOE_DSP_LARGE))
        return carry

    lax.fori_loop(clo_ref[i], chi_ref[i] + 1, body, 0)
    o_ref[...] = acc_ref[...].astype(BF16)


def _dispatch_call(h2, route):
    n_tok = h2.shape[0]
    n_tiles = route["n_tiles"]
    grid_spec = pltpu.PrefetchScalarGridSpec(
        num_scalar_prefetch=5,
        grid=(n_tiles,),
        in_specs=[
            pl.BlockSpec((n_tok, D_MODEL), lambda i, *_: (0, 0), pipeline_mode=pl.Buffered(1)),
            pl.BlockSpec((n_tok // MOE_SRC, 8, MOE_SRC), lambda i, *_: (0, 0, 0), pipeline_mode=pl.Buffered(1)),
            pl.BlockSpec((n_tok // MOE_SRC, 8, MOE_SRC), lambda i, *_: (0, 0, 0), pipeline_mode=pl.Buffered(1)),
        ],
        out_specs=[pl.BlockSpec((MOE_TILE, D_MODEL), lambda i, *_: (i, 0)),
                   pl.BlockSpec((MOE_TILE, LANES), lambda i, *_: (i, 0))],
        scratch_shapes=[pltpu.VMEM((MOE_TILE, D_MODEL), F32)],
    )
    return pl.pallas_call(
        _dispatch_kernel,
        grid_spec=grid_spec,
        out_shape=[jax.ShapeDtypeStruct((n_tiles * MOE_TILE, D_MODEL), BF16),
                   jax.ShapeDtypeStruct((n_tiles * MOE_TILE, LANES), F32)],
        compiler_params=_params("arbitrary"),
        name="moe_dispatch",
    )(route["c_lo"], route["c_hi"], route["tile_expert"], route["run_start"], route["run_cnt"],
      h2, route["pos_rows"], route["w_rows"])


def _expert_kernel(te_ref, nv_ref, x_ref, w1_ref, w3_ref, w2_ref, sw_ref, o_ref, acc_ref):
    i = pl.program_id(0)
    f = pl.program_id(1)
    last = pl.num_programs(1) - 1
    valid = i < nv_ref[0]

    @pl.when(valid & (f == 0))
    def _():
        acc_ref[...] = jnp.zeros_like(acc_ref)

    @pl.when(valid)
    def _():
        acc_ref[...] += _swiglu_chunk(x_ref[...], w1_ref[0, 0], w3_ref[0, 0], w2_ref[0, 0])

    @pl.when(valid & (f == last))
    def _():
        o_ref[...] = (acc_ref[...] * sw_ref[:, 0:1]).astype(BF16)

    @pl.when(jnp.logical_not(valid) & (f == last))
    def _():
        o_ref[...] = jnp.zeros_like(o_ref)


def _expert_call(xs, slot_w, route, w1, w3, w2, layer):
    n_tiles = route["n_tiles"]
    grid_spec = pltpu.PrefetchScalarGridSpec(
        num_scalar_prefetch=2,
        grid=(n_tiles, D_FF // FF_CHUNK),
        in_specs=[
            pl.BlockSpec((MOE_TILE, D_MODEL), lambda i, f, te, nv: (i, 0)),
            pl.BlockSpec((1, 1, D_MODEL, FF_CHUNK), lambda i, f, te, nv: (layer, te[i], 0, f)),
            pl.BlockSpec((1, 1, D_MODEL, FF_CHUNK), lambda i, f, te, nv: (layer, te[i], 0, f)),
            pl.BlockSpec((1, 1, FF_CHUNK, D_MODEL), lambda i, f, te, nv: (layer, te[i], f, 0)),
            pl.BlockSpec((MOE_TILE, LANES), lambda i, f, te, nv: (i, 0)),
        ],
        out_specs=pl.BlockSpec((MOE_TILE, D_MODEL), lambda i, f, te, nv: (i, 0)),
        scratch_shapes=[pltpu.VMEM((MOE_TILE, D_MODEL), F32)],
    )
    return pl.pallas_call(
        _expert_kernel,
        grid_spec=grid_spec,
        out_shape=jax.ShapeDtypeStruct((n_tiles * MOE_TILE, D_MODEL), BF16),
        compiler_params=_params("arbitrary", "arbitrary"),
        name="moe_experts",
    )(route["tile_expert"], route["n_valid"], xs, w1, w3, w2, slot_w)


def _combine_kernel(win_ref, off_ref, cnt_ref, y_hbm, pos_ref, x_ref, mod_ref, fg_ref, o_ref, buf_ref, sem_ref,
                    *, final_norm):
    c = pl.program_id(0)
    cur = c % 2

    def window_copies(chunk, buf_set):
        return [pltpu.make_async_copy(
            y_hbm.at[pl.ds(pl.multiple_of(win_ref[chunk * N_EXPERTS + e], 16), MOE_CMB_WIN), :],
            buf_ref.at[buf_set, pl.ds(e * MOE_CMB_WIN, MOE_CMB_WIN), :],
            sem_ref.at[buf_set, e]) for e in range(N_EXPERTS)]

    @pl.when(c == 0)
    def _():
        for cp in window_copies(0, 0):
            cp.start()

    @pl.when(c + 1 < pl.num_programs(0))
    def _():
        for cp in window_copies(c + 1, 1 - cur):
            cp.start()

    row = lax.broadcasted_iota(jnp.int32, (1, N_EXPERTS * MOE_CMB_WIN), 1)
    slot = jnp.full((1, N_EXPERTS * MOE_CMB_WIN), -1, jnp.int32)
    for e in range(N_EXPERTS):
        local = row - e * MOE_CMB_WIN
        off = off_ref[c * N_EXPERTS + e]
        inside = (local >= off) & (local < off + cnt_ref[c * N_EXPERTS + e]) & (local < MOE_CMB_WIN)
        slot = jnp.where(inside, win_ref[c * N_EXPERTS + e] + local, slot)
    pos = pos_ref[...]
    hit = (pos[:, 0:1] == slot) | (pos[:, 1:2] == slot)
    onehot = jnp.where(hit, 1.0, 0.0).astype(BF16)
    for cp in window_copies(c, cur):
        cp.wait()
    update = _dot(onehot, buf_ref[cur])
    o_ref[...] = _residual_out(x_ref[...], mod_ref[0][5:6], update, final_norm, fg_ref[...])


def _combine_call(ys, route, x, mod_l, final_g, first_tile, final_norm):
    n_tok = x.shape[0]
    rows = lambda width: pl.BlockSpec((MOE_SRC, width), lambda c, *_: (c, 0))
    grid_spec = pltpu.PrefetchScalarGridSpec(
        num_scalar_prefetch=3,
        grid=(n_tok // MOE_SRC,),
        in_specs=[
            pl.BlockSpec(memory_space=pl.ANY),
            rows(TOP_K),
            rows(D_MODEL),
            pl.BlockSpec((1, ADA_CHUNKS, D_MODEL), lambda c, *_: (_tile_mod_row(c, first_tile, MOE_SRC), 0, 0)),
            pl.BlockSpec((1, D_MODEL), lambda c, *_: (0, 0)),
        ],
        out_specs=rows(D_MODEL),
        scratch_shapes=[pltpu.VMEM((2, N_EXPERTS * MOE_CMB_WIN, D_MODEL), BF16),
                        pltpu.SemaphoreType.DMA((2, N_EXPERTS))],
    )
    return pl.pallas_call(
        functools.partial(_combine_kernel, final_norm=final_norm),
        grid_spec=grid_spec,
        out_shape=jax.ShapeDtypeStruct((n_tok, D_MODEL), F32),
        compiler_params=_params("arbitrary"),
        name="moe_combine",
    )(route["win"], route["run_off"], route["run_cnt"], ys, route["pos_cols"], x, mod_l, final_g)


def _rope_tables():
    t = np.arange(SEQ)

    def angles(rot_dim):
        half = rot_dim // 2
        inv = ROPE_THETA ** (-jnp.arange(0, half, 2, dtype=F32) / half)
        ang = jnp.concatenate([jnp.asarray(t // GRID_W, F32)[:, None] * inv[None, :],
                               jnp.asarray(t % GRID_W, F32)[:, None] * inv[None, :]], axis=-1)
        return jnp.cos(ang), jnp.sin(ang)

    def pad_rows(a, fill):
        return jnp.concatenate([jnp.full((ROW_TILE, a.shape[1]), fill, F32), a], axis=0)

    c, s = angles(HEAD_DIM)
    cos64 = jnp.tile(jnp.concatenate([c, c], axis=-1), (1, 4))
    sin64 = jnp.tile(jnp.concatenate([s, s], axis=-1), (1, 4))
    c, s = angles(MLA_ROPE)
    one = jnp.ones((SEQ, MLA_NOPE), F32)
    zero = jnp.zeros((SEQ, MLA_NOPE), F32)
    tail1 = jnp.ones((SEQ, MLA_PAD - MLA_NOPE - MLA_ROPE), F32)
    tail0 = jnp.zeros((SEQ, MLA_PAD - MLA_NOPE - MLA_ROPE), F32)
    cosm = jnp.tile(jnp.concatenate([one, c, c, tail1], axis=-1), (1, MLA_HEADS))
    sinm = jnp.tile(jnp.concatenate([zero, s, s, tail0], axis=-1), (1, MLA_HEADS))
    blockdiag = jnp.asarray(np.kron(np.eye(4), np.ones((HEAD_DIM, HEAD_DIM))), BF16)
    place = np.zeros((256, MLA_HEADS * MLA_PAD), np.float32)
    for h in range(MLA_HEADS):
        for r in range(MLA_ROPE):
            place[MLA_KV_RANK + r, h * MLA_PAD + MLA_NOPE + r] = 1.0
    return {
        "cos64": pad_rows(cos64, 1.0), "sin64": pad_rows(sin64, 0.0),
        "cosm": pad_rows(cosm, 1.0), "sinm": pad_rows(sinm, 0.0),
        "blockdiag": blockdiag, "pe_place": jnp.asarray(place, BF16),
    }


def _pack_w_in(w):
    n_qkv = COL_DKVA + MLA_KV_RANK + MLA_ROPE
    pad = jnp.zeros(w.shape[:2] + (COL_GATE - n_qkv,), BF16)
    return jnp.concatenate([w[..., :n_qkv].astype(BF16), pad, w[..., n_qkv:].astype(BF16)], axis=-1)


def _pack_mla(wqb, wkvb):
    dqh = MLA_NOPE + MLA_ROPE
    q = wqb.reshape(MLA_Q_RANK, MLA_HEADS, dqh)
    q = jnp.pad(q, ((0, 0), (0, 0), (0, MLA_PAD - dqh))).reshape(MLA_Q_RANK, MLA_HEADS * MLA_PAD)
    kv = wkvb.reshape(MLA_KV_RANK, MLA_HEADS, MLA_NOPE + MLA_V)
    k = jnp.pad(kv[:, :, :MLA_NOPE], ((0, 0), (0, 0), (0, MLA_PAD - MLA_NOPE)))
    k = k.reshape(MLA_KV_RANK, MLA_HEADS * MLA_PAD)
    v = kv[:, :, MLA_NOPE:].reshape(MLA_KV_RANK, MLA_HEADS * MLA_V)
    return q.astype(BF16), k.astype(BF16), v.astype(BF16)


def _split2_host(w):
    hi = w.astype(BF16)
    return jnp.stack([hi, (w - hi.astype(F32)).astype(BF16)])


def kernel(x, c, ctx, c_ctx, norm1_g, norm2_g, w_ada, b_ada, w_in, na_rpb, gb_qnorm, gb_knorm, wc_sink,
           mla_qnorm, mla_kvnorm, mla_wqb, mla_wkvb, w_branch, w_out, ffn_w1, ffn_w3, ffn_w2,
           moe_router, moe_w1, moe_w3, moe_w2, final_g):
    n_batch = x.shape[0]
    assert x.shape[1:] == (SEQ, D_MODEL) and ctx.shape[1:] == (CTX_LEN, D_MODEL)
    assert n_batch * CTX_LEN <= CTX_ROWS and n_batch <= 8

    ctx_rows = ctx.reshape(n_batch * CTX_LEN, D_MODEL)
    if ctx_rows.shape[0] < CTX_ROWS:
        ctx_rows = jnp.pad(ctx_rows, ((0, CTX_ROWS - ctx_rows.shape[0]), (0, 0)))
    xt = jnp.concatenate([ctx_rows, x.reshape(n_batch * SEQ, D_MODEL)], axis=0)

    cvec = jnp.zeros((16, D_MODEL), F32).at[:n_batch].set(c).at[8].set(c_ctx)
    mod = _mod_call(cvec, w_ada, b_ada).reshape(DEPTH, 16, ADA_CHUNKS, D_MODEL)
    tabs = _rope_tables()
    w_in_p = _pack_w_in(w_in)
    moe_w = tuple(w.astype(BF16) for w in (moe_w1, moe_w3, moe_w2))
    mla_scale = (MLA_NOPE + MLA_ROPE) ** -0.5
    lat_tile0 = CTX_ROWS // ROW_TILE

    for l in range(DEPTH):
        with_ctx = l < DEPTH - 1
        wq, wk, wv = _pack_mla(mla_wqb[l], mla_wkvb[l])
        lw = {
            "gq": jnp.tile(gb_qnorm[l], 4)[None, :], "gk": jnp.tile(gb_knorm[l], 2)[None, :],
            "qn": mla_qnorm[l][None, :], "kvn": mla_kvnorm[l][None, :], "wqb": wq, "wk": wk, "wv": wv,
        }
        p, gate = _inproj_call(xt, mod[l], norm1_g[l][None, :], w_in_p, l)
        qb, kb, qc, kc, qd, kd, vd = _prep_call(p, tabs, lw)

        oa = _natten_call(n_batch, with_ctx, p, _natten_bias_table(na_rpb[l]))
        ob = _dense_attn_call(n_batch, with_ctx, (qb, 256, 0), (kb, 128, 0), (p, 128, COL_BV),
                              n_kv=GB_KV_HEADS, group=GB_Q_HEADS // GB_KV_HEADS, dq=HEAD_DIM, dv=HEAD_DIM,
                              name="attn_global")
        oc = _window_attn_call(n_batch, with_ctx, (qc, 256, 0), (kc, 128, 0), (p, 128, COL_CV), wc_sink[l])
        od = _dense_attn_call(n_batch, with_ctx, (qd, MLA_HEADS * MLA_PAD, 0), (kd, MLA_HEADS * MLA_PAD, 0),
                              (vd, 256, 0), n_kv=MLA_HEADS, group=1, dq=MLA_PAD, dv=MLA_V, scale=mla_scale,
                              name="attn_mla")

        is_moe = l % 2 == 1
        router3 = None
        if is_moe:
            router3 = _split2_host(jnp.pad(moe_router[l // 2], ((0, 0), (0, LANES - N_EXPERTS))))
        first_tile = 0 if with_ctx else lat_tile0
        res = _merge_call((oa, ob, oc, od), gate, xt, mod[l], norm2_g[l][None, :],
                          w_branch[l].astype(BF16), w_out[l].astype(BF16), router3, first_tile)
        x_mid, h2 = res[0], res[1]
        final_norm = l == DEPTH - 1
        if is_moe:
            route = _moe_route(res[2], h2.shape[0])
            xs, slot_w = _dispatch_call(h2, route)
            ys = _expert_call(xs, slot_w, route, *moe_w, l // 2)
            xt = _combine_call(ys, route, x_mid, mod[l], final_g[None, :],
                               0 if with_ctx else CTX_ROWS // MOE_SRC, final_norm)
        else:
            w1, w3, w2 = (w[l // 2].astype(BF16) for w in (ffn_w1, ffn_w3, ffn_w2))
            xt = _ffn_call(h2, w1, w3, w2, x_mid, mod[l], final_g[None, :],
                           0 if with_ctx else CTX_ROWS // FFN_ROWS, final_norm)

    return xt.reshape(n_batch, SEQ, D_MODEL)
```

```python
import functools

import numpy as np
import jax
import jax.numpy as jnp
from jax import lax
from jax.experimental import pallas as pl
from jax.experimental.pallas import tpu as pltpu

F32 = jnp.float32
BF16 = jnp.bfloat16

D_MODEL = 1024
SEQ = 2048
DEPTH = 4
CTX_LEN = 256
GRID_W = 64
GRID_ROWS = SEQ // GRID_W
HEAD_DIM = 64
ROPE_THETA = 10000.0
NORM_EPS = 1e-6
NEG_INF = -1e30

NA_HEADS = 4
NA_KH = 8
NA_KW = 16
GB_Q_HEADS = 4
GB_KV_HEADS = 2
WC_Q_HEADS = 4
WC_KV_HEADS = 2
WC_WINDOW = 128
MLA_HEADS = 4
MLA_Q_RANK = 256
MLA_KV_RANK = 128
MLA_NOPE = 64
MLA_ROPE = 32
MLA_V = 64
MLA_PAD = 128
N_BRANCH = 4
BRANCH_W = 256
D_FF = 3584
N_EXPERTS = 8
ADA_CHUNKS = 6

VMEM_LIMIT_BYTES = 56 * 1024 * 1024
LANES = 128

CTX_ROWS = 2048
ROW_TILE = 512
Q_TILE = 256
NA_Q_ROWS = Q_TILE // GRID_W
NA_WIN_ROWS = 12
NA_WIN = NA_WIN_ROWS * GRID_W
WC_WIN = Q_TILE + 2 * WC_WINDOW
ATTN_KEY_CHUNK = 512
FLASH_ROWS = 128
FLASH_KEYS = 256

COL_AQ, COL_AK, COL_AV = 0, 256, 512
COL_BQ, COL_BK, COL_BV = 768, 1024, 1152
COL_CQ, COL_CK, COL_CV = 1280, 1536, 1664
COL_DQA, COL_DKVA, COL_GATE = 1792, 2048, 2304
GATE_W = N_BRANCH * D_MODEL
P_WIDTH = COL_GATE + GATE_W
QKV_CHUNK = COL_GATE // 2
FF_CHUNK = 896
MOE_FF_CHUNK = 1792


def _params(*sem):
    return pltpu.CompilerParams(dimension_semantics=sem, vmem_limit_bytes=VMEM_LIMIT_BYTES)


def _dot(a, b):
    return jnp.dot(a, b, preferred_element_type=F32)


def _dot_nt(a, b):
    return lax.dot_general(a, b, (((1,), (1,)), ((), ())), preferred_element_type=F32)


def _split3(x):
    hi = x.astype(BF16)
    r1 = x - hi.astype(F32)
    mid = r1.astype(BF16)
    lo = (r1 - mid.astype(F32)).astype(BF16)
    return hi, mid, lo


def _sigmoid(x):
    return 1.0 / (1.0 + jnp.exp(-x))


ADA_TN = 1536


def _mod_kernel(c_ref, w_ref, b_ref, o_ref):
    c = c_ref[...]
    sc = (c * _sigmoid(c)).astype(BF16)
    o_ref[0] = _dot(sc, w_ref[0].astype(BF16)) + b_ref[0]


def _mod_call(cvec, w_ada, b_ada):
    n = ADA_CHUNKS * D_MODEL
    return pl.pallas_call(
        _mod_kernel,
        grid=(DEPTH, n // ADA_TN),
        in_specs=[
            pl.BlockSpec((16, D_MODEL), lambda l, j: (0, 0)),
            pl.BlockSpec((1, D_MODEL, ADA_TN), lambda l, j: (l, 0, j)),
            pl.BlockSpec((1, 1, ADA_TN), lambda l, j: (l, 0, j)),
        ],
        out_specs=pl.BlockSpec((1, 16, ADA_TN), lambda l, j: (l, 0, j)),
        out_shape=jax.ShapeDtypeStruct((DEPTH, 16, n), F32),
        compiler_params=_params("arbitrary", "arbitrary"),
        name="adaln_mod",
    )(cvec, w_ada, b_ada.reshape(DEPTH, 1, n))


def _mod_row(i, first_tile):
    n_ctx = CTX_ROWS // ROW_TILE
    t = i + first_tile
    return jnp.where(t < n_ctx, 8, (t - n_ctx) // (SEQ // ROW_TILE))


def _norm_mod(x, g, shift, scale):
    ms = jnp.mean(x * x, axis=-1, keepdims=True)
    y = x * lax.rsqrt(ms + NORM_EPS) * g
    return y * (1.0 + scale) + shift


def _inproj_kernel(x_ref, mod_ref, g_ref, w_ref, p_ref, gate_ref):
    m = mod_ref[0]
    hb = _norm_mod(x_ref[...], g_ref[...], m[0:1], m[1:2]).astype(BF16)
    for c0 in range(0, COL_GATE, QKV_CHUNK):
        p_ref[:, c0:c0 + QKV_CHUNK] = _dot(hb, w_ref[0, :, c0:c0 + QKV_CHUNK]).astype(BF16)
    for c0 in range(0, GATE_W, D_MODEL):
        gate_ref[:, c0:c0 + D_MODEL] = _dot(hb, w_ref[0, :, COL_GATE + c0:COL_GATE + c0 + D_MODEL]).astype(BF16)


def _inproj_call(x, mod_l, g, w_in_p, layer):
    t = x.shape[0]
    return pl.pallas_call(
        _inproj_kernel,
        grid=(t // ROW_TILE,),
        in_specs=[
            pl.BlockSpec((ROW_TILE, D_MODEL), lambda i: (i, 0)),
            pl.BlockSpec((1, ADA_CHUNKS, D_MODEL), lambda i: (_mod_row(i, 0), 0, 0)),
            pl.BlockSpec((1, D_MODEL), lambda i: (0, 0)),
            pl.BlockSpec((1, D_MODEL, P_WIDTH), lambda i: (layer, 0, 0), pipeline_mode=pl.Buffered(1)),
        ],
        out_specs=[pl.BlockSpec((ROW_TILE, COL_GATE), lambda i: (i, 0)),
                   pl.BlockSpec((ROW_TILE, GATE_W), lambda i: (i, 0))],
        out_shape=[jax.ShapeDtypeStruct((t, COL_GATE), BF16), jax.ShapeDtypeStruct((t, GATE_W), BF16)],
        compiler_params=_params("arbitrary"),
        name="inproj",
    )(x, mod_l, g, w_in_p)


def _rope(x, cos, sin, half, first_mask):
    w = x.shape[-1]
    fwd = pltpu.roll(x, w - half, 1)
    bwd = pltpu.roll(x, half, 1)
    rot = jnp.where(first_mask, -fwd, bwd)
    return x * cos + rot * sin


def _head_rms(x, gain, blockdiag):
    hi, mid, lo = _split3(x * x)
    ss = _dot(hi, blockdiag) + _dot(mid, blockdiag) + _dot(lo, blockdiag)
    return x * lax.rsqrt(ss * (1.0 / HEAD_DIM) + NORM_EPS) * gain


def _prep_kernel(bq_ref, bk_ref, cq_ref, ck_ref, dqa_ref, dkva_ref, bv_ref,
                 cos64_ref, sin64_ref, cosm_ref, sinm_ref,
                 gq_ref, gk_ref, bd_ref, qn_ref, kvn_ref, wqb_ref, wk_ref, wv_ref, pe_ref,
                 qb_ref, kb_ref, qc_ref, kc_ref, qd_ref, kd_ref, vd_ref, vb_ref):
    scale = HEAD_DIM ** -0.5
    cos64, sin64 = cos64_ref[...], sin64_ref[...]
    lane = lax.broadcasted_iota(jnp.int32, (1, 256), 1)
    first64 = (lane % HEAD_DIM) < (HEAD_DIM // 2)
    bd = bd_ref[...]

    bq = _head_rms(bq_ref[...].astype(F32), gq_ref[...], bd)
    qb_ref[...] = (_rope(bq, cos64, sin64, HEAD_DIM // 2, first64) * scale).astype(BF16)
    bk = _head_rms(bk_ref[...].astype(F32), gk_ref[...], bd[:128, :128])
    kb_ref[...] = _rope(bk, cos64[:, :128], sin64[:, :128], HEAD_DIM // 2, first64[:, :128]).T.astype(BF16)
    qc_ref[...] = (_rope(cq_ref[...].astype(F32), cos64, sin64, HEAD_DIM // 2, first64) * scale).astype(BF16)
    kc_ref[...] = _rope(ck_ref[...].astype(F32), cos64[:, :128], sin64[:, :128], HEAD_DIM // 2,
                        first64[:, :128]).astype(BF16)

    cosm, sinm = cosm_ref[...], sinm_ref[...]
    lane_m = lax.broadcasted_iota(jnp.int32, (1, MLA_HEADS * MLA_PAD), 1) % MLA_PAD
    first_m = lane_m < (MLA_NOPE + MLA_ROPE // 2)
    dqa = dqa_ref[...].astype(F32)
    qn = dqa * lax.rsqrt(jnp.mean(dqa * dqa, axis=-1, keepdims=True) + NORM_EPS) * qn_ref[...]
    dq = _dot(qn.astype(BF16), wqb_ref[...])
    qd_ref[...] = _rope(dq, cosm, sinm, MLA_ROPE // 2, first_m).astype(BF16)

    dkva = dkva_ref[...]
    kvc = dkva[:, :MLA_KV_RANK].astype(F32)
    kvn = (kvc * lax.rsqrt(jnp.mean(kvc * kvc, axis=-1, keepdims=True) + NORM_EPS) * kvn_ref[...]).astype(BF16)
    dk = _dot(kvn, wk_ref[...]) + _dot(dkva, pe_ref[...])
    kd_ref[...] = _rope(dk, cosm, sinm, MLA_ROPE // 2, first_m).T.astype(BF16)
    ones_half = lane_m >= HEAD_DIM
    vd_ref[...] = jnp.where(ones_half, 1.0, _dot(kvn, wv_ref[...])).astype(BF16)
    bv = bv_ref[...]
    vb_ref[:, :LANES] = jnp.where(ones_half[:, :LANES], 1.0, bv).astype(BF16)
    vb_ref[:, LANES:] = jnp.where(ones_half[:, :LANES], 1.0, pltpu.roll(bv.astype(F32), HEAD_DIM, 1)).astype(BF16)


def _rope_row_block(i):
    n_ctx = CTX_ROWS // ROW_TILE
    return jnp.where(i < n_ctx, 0, 1 + (i - n_ctx) % (SEQ // ROW_TILE))


def _prep_call(p, tabs, lw):
    t = p.shape[0]
    mw = MLA_HEADS * MLA_PAD

    def pcol(width, col):
        return pl.BlockSpec((ROW_TILE, width), lambda i: (i, col // width))

    def full(shape):
        return pl.BlockSpec(shape, lambda i: (0,) * len(shape))

    def rows(width):
        return pl.BlockSpec((ROW_TILE, width), lambda i: (i, 0))

    def tab(width):
        return pl.BlockSpec((ROW_TILE, width), lambda i: (_rope_row_block(i), 0))

    def cols(height):
        return pl.BlockSpec((height, ROW_TILE), lambda i: (0, i))

    outs = [(256, False), (128, True), (256, False), (128, False), (mw, False), (mw, True), (mw, False),
            (GB_KV_HEADS * LANES, False)]
    return pl.pallas_call(
        _prep_kernel,
        grid=(t // ROW_TILE,),
        in_specs=[
            pcol(256, COL_BQ), pcol(128, COL_BK), pcol(256, COL_CQ), pcol(128, COL_CK),
            pcol(256, COL_DQA), pcol(256, COL_DKVA), pcol(128, COL_BV),
            tab(256), tab(256), tab(mw), tab(mw),
            full((1, 256)), full((1, 128)), full((256, 256)), full((1, MLA_Q_RANK)), full((1, MLA_KV_RANK)),
            full((MLA_Q_RANK, mw)), full((MLA_KV_RANK, mw)), full((MLA_KV_RANK, mw)), full((256, mw)),
        ],
        out_specs=[cols(w) if tr else rows(w) for w, tr in outs],
        out_shape=[jax.ShapeDtypeStruct((w, t) if tr else (t, w), BF16) for w, tr in outs],
        compiler_params=_params("arbitrary"),
        name="mixer_prep",
    )(p, p, p, p, p, p, p, tabs["cos64"], tabs["sin64"], tabs["cosm"], tabs["sinm"],
      lw["gq"], lw["gk"], tabs["blockdiag"], lw["qn"], lw["kvn"], lw["wqb"], lw["wk"], lw["wv"], tabs["pe_place"])


def _attend(q, segs, sink=None, scale=None):
    chunks = []
    for k, v, bias in segs:
        for c0 in range(0, k.shape[0], ATTN_KEY_CHUNK):
            c1 = min(c0 + ATTN_KEY_CHUNK, k.shape[0])
            chunks.append((k[c0:c1], v[c0:c1], None if bias is None else bias[:, c0:c1]))
    scores = []
    m = None
    for k, _, bias in chunks:
        s = _dot_nt(q, k)
        if scale is not None:
            s = s * scale
        if bias is not None:
            s = s + bias
        scores.append(s)
        ms = jnp.max(s, axis=-1, keepdims=True)
        m = ms if m is None else jnp.maximum(m, ms)
    if sink is not None:
        m = jnp.maximum(m, sink)
    denom = None
    out = None
    for s, (_, v, _) in zip(scores, chunks):
        p = jnp.exp(s - m)
        ps = jnp.sum(p, axis=-1, keepdims=True)
        pv = _dot(p.astype(BF16), v)
        denom = ps if denom is None else denom + ps
        out = pv if out is None else out + pv
    if sink is not None:
        denom = denom + jnp.exp(sink - m)
    return out / denom


def _gqa_heads(q_ref, o_ref, sink_ref, n_kv, group, dq, dv, scale, seg_fn):
    tq = q_ref.shape[0]
    for h in range(n_kv):
        heads = [h * group + g for g in range(group)]
        q = jnp.concatenate([q_ref[:, a * dq:(a + 1) * dq] for a in heads], axis=0) if group > 1 \
            else q_ref[:, h * dq:(h + 1) * dq]
        sink = None
        if sink_ref is not None:
            sink = jnp.concatenate([jnp.full((tq, 1), sink_ref[a], F32) for a in heads], axis=0)
        o = _attend(q, seg_fn(h), sink=sink, scale=scale)
        for g, a in enumerate(heads):
            o_ref[:, a * dv:(a + 1) * dv] = o[g * tq:(g + 1) * tq].astype(BF16)


def _dense_attn_kernel(q_ref, ktl_ref, vl_ref, ktc_ref, vc_ref, o_ref, s_ref, *, n_kv, group, dq, scale, ctx_tile):
    tq = q_ref.shape[0]
    hd = HEAD_DIM

    def run(with_latent):
        for h in range(n_kv):
            heads = [h * group + g for g in range(group)]
            q = jnp.concatenate([q_ref[:, a * dq:(a + 1) * dq] for a in heads], axis=0) if group > 1 \
                else q_ref[:, h * dq:(h + 1) * dq]
            segs = [(ktc_ref[h * dq:(h + 1) * dq, :], vc_ref[:, h * LANES:(h + 1) * LANES])]
            if with_latent:
                segs = [(ktl_ref[h * dq:(h + 1) * dq, c0:c0 + ATTN_KEY_CHUNK],
                         vl_ref[c0:c0 + ATTN_KEY_CHUNK, h * LANES:(h + 1) * LANES])
                        for c0 in range(0, SEQ, ATTN_KEY_CHUNK)] + segs
            m = None
            c0 = 0
            for kt, _ in segs:
                s = _dot(q, kt)
                if scale is not None:
                    s = s * scale
                s_ref[h % 2, :, c0:c0 + kt.shape[1]] = s
                c0 += kt.shape[1]
                ms = jnp.max(s, axis=-1, keepdims=True)
                m = ms if m is None else jnp.maximum(m, ms)
            acc = None
            c0 = 0
            for kt, v in segs:
                s = s_ref[h % 2, :, c0:c0 + kt.shape[1]]
                c0 += kt.shape[1]
                pv = _dot(jnp.exp(s - m).astype(BF16), v)
                acc = pv if acc is None else acc + pv
            o = acc * pltpu.roll(1.0 / acc, hd, 1)
            for g, a in enumerate(heads):
                o_ref[:, a * hd:(a + 1) * hd] = o[g * tq:(g + 1) * tq, :hd].astype(BF16)

    if ctx_tile:
        j = pl.program_id(1)
        pl.when(j == 0)(lambda: run(False))
        pl.when(j > 0)(lambda: run(True))
    else:
        run(True)


def _q_row_block(b, j, n_batch, with_ctx):
    per_batch = SEQ // Q_TILE
    lat0 = CTX_ROWS // Q_TILE
    if with_ctx:
        return jnp.where(j == 0, b, lat0 + b * per_batch + j - 1)
    return lat0 + b * per_batch + j


def _attn_specs(n_batch, with_ctx, q, k, v, o_width, order_bj=True):
    def ix(f):
        return (lambda b, j: f(b, j)) if order_bj else (lambda j, b: f(b, j))

    def qspec(width, col):
        return pl.BlockSpec((Q_TILE, width), ix(lambda b, j: (_q_row_block(b, j, n_batch, with_ctx), col // width)))

    def lat(width, col):
        return pl.BlockSpec((SEQ, width), ix(lambda b, j: (CTX_ROWS // SEQ + b, col // width)))

    def ctx(width, col):
        return pl.BlockSpec((CTX_LEN, width), ix(lambda b, j: (b, col // width)))

    in_specs = [qspec(q[1], q[2]), lat(k[1], k[2]), lat(v[1], v[2]), ctx(k[1], k[2]), ctx(v[1], v[2])]
    args = [q[0], k[0], v[0], k[0], v[0]]
    if with_ctx:
        return in_specs, args, qspec(o_width, 0)
    out_spec = pl.BlockSpec((Q_TILE, o_width), ix(lambda b, j: (b * (SEQ // Q_TILE) + j, 0)))
    return in_specs, args, out_spec


def _attn_out_rows(t, with_ctx):
    return t if with_ctx else t - CTX_ROWS


def _dense_attn_call(n_batch, with_ctx, q, kt, v, *, n_kv, group, dq, scale=None, name):
    t = q[0].shape[0]
    in_specs, args, out_spec = _attn_specs(n_batch, with_ctx, q, v, v, 256)
    kt_rows = kt.shape[0]
    in_specs[1] = pl.BlockSpec((kt_rows, SEQ), lambda b, j: (0, CTX_ROWS // SEQ + b))
    in_specs[3] = pl.BlockSpec((kt_rows, CTX_LEN), lambda b, j: (0, b))
    args[1] = args[3] = kt
    kern = functools.partial(_dense_attn_kernel, n_kv=n_kv, group=group, dq=dq, scale=scale, ctx_tile=with_ctx)
    return pl.pallas_call(
        kern,
        grid=(n_batch, SEQ // Q_TILE + (1 if with_ctx else 0)),
        in_specs=in_specs,
        out_specs=out_spec,
        out_shape=jax.ShapeDtypeStruct((_attn_out_rows(t, with_ctx), 256), BF16),
        scratch_shapes=[pltpu.VMEM((2, group * Q_TILE, SEQ + CTX_LEN), F32)],
        compiler_params=_params("arbitrary", "arbitrary"),
        name=name,
    )(*args)


def _window_attn_kernel(sink_ref, q_ref, kl_ref, vl_ref, kc_ref, vc_ref, o_ref, *, ctx_tile):
    j = pl.program_id(1)
    hd = HEAD_DIM
    group = WC_Q_HEADS // WC_KV_HEADS

    def ctx_run():
        def segs(h):
            return [(kc_ref[:, h * hd:(h + 1) * hd], vc_ref[:, h * hd:(h + 1) * hd], None)]
        _gqa_heads(q_ref, o_ref, sink_ref, WC_KV_HEADS, group, hd, hd, None, segs)

    def lat_run():
        i = j - 1 if ctx_tile else j
        start = pl.multiple_of(jnp.clip(i * Q_TILE - WC_WINDOW, 0, SEQ - WC_WIN), WC_WINDOW)
        qpos = i * Q_TILE + lax.broadcasted_iota(jnp.int32, (group * Q_TILE, WC_WIN), 0) % Q_TILE
        kpos = start + lax.broadcasted_iota(jnp.int32, (group * Q_TILE, WC_WIN), 1)
        bias = jnp.where(jnp.abs(qpos - kpos) <= WC_WINDOW, 0.0, NEG_INF).astype(F32)

        def segs(h):
            return [(kl_ref[pl.ds(start, WC_WIN), h * hd:(h + 1) * hd],
                     vl_ref[pl.ds(start, WC_WIN), h * hd:(h + 1) * hd], bias),
                    (kc_ref[:, h * hd:(h + 1) * hd], vc_ref[:, h * hd:(h + 1) * hd], None)]
        _gqa_heads(q_ref, o_ref, sink_ref, WC_KV_HEADS, group, hd, hd, None, segs)

    if ctx_tile:
        pl.when(j == 0)(ctx_run)
        pl.when(j > 0)(lat_run)
    else:
        lat_run()


def _window_attn_call(n_batch, with_ctx, q, k, v, sink):
    t = q[0].shape[0]
    in_specs, args, out_spec = _attn_specs(n_batch, with_ctx, q, k, v, 256)
    return pl.pallas_call(
        functools.partial(_window_attn_kernel, ctx_tile=with_ctx),
        grid=(n_batch, SEQ // Q_TILE + (1 if with_ctx else 0)),
        in_specs=[pl.BlockSpec(memory_space=pltpu.SMEM)] + in_specs,
        out_specs=out_spec,
        out_shape=jax.ShapeDtypeStruct((_attn_out_rows(t, with_ctx), 256), BF16),
        compiler_params=_params("arbitrary", "arbitrary"),
        name="attn_window",
    )(sink, *args)


def _natten_kernel(q_ref, kl_ref, vl_ref, kc_ref, vc_ref, bias_ref, o_ref, *, ctx_tile):
    j = pl.program_id(0)
    hd = HEAD_DIM
    scale = HEAD_DIM ** -0.5

    def ctx_run():
        for h in range(NA_HEADS):
            sl = slice(h * hd, (h + 1) * hd)
            o = _attend(q_ref[:, sl] * scale, [(kc_ref[:, sl], vc_ref[:, sl], None)])
            o_ref[:, sl] = o.astype(BF16)

    def lat_run():
        i = j - 1 if ctx_tile else j
        ws = jnp.clip(NA_Q_ROWS * i - NA_KH // 2, 0, GRID_ROWS - NA_WIN_ROWS)
        start = pl.multiple_of(ws * GRID_W, GRID_W)
        for h in range(NA_HEADS):
            sl = slice(h * hd, (h + 1) * hd)
            segs = [(kl_ref[pl.ds(start, NA_WIN), sl], vl_ref[pl.ds(start, NA_WIN), sl], bias_ref[0, h]),
                    (kc_ref[:, sl], vc_ref[:, sl], None)]
            o_ref[:, sl] = _attend(q_ref[:, sl] * scale, segs).astype(BF16)

    if ctx_tile:
        pl.when(j == 0)(ctx_run)
        pl.when(j > 0)(lat_run)
    else:
        lat_run()


_NA_VARIANT = (0, 1, 2, 2, 2, 2, 3, 4)
_NA_VARIANT_TILE = (0, 1, 2, 6, 7)


def _natten_call(n_batch, with_ctx, p, bias_tab):
    t = p.shape[0]
    q, k, v = (p, 256, COL_AQ), (p, 256, COL_AK), (p, 256, COL_AV)
    in_specs, args, out_spec = _attn_specs(n_batch, with_ctx, q, k, v, 256, order_bj=False)

    def bias_ix(j, b):
        i = jnp.maximum(j - 1, 0) if with_ctx else j
        v_ix = sum(jnp.where(i == n, _NA_VARIANT[n], 0) for n in range(len(_NA_VARIANT)))
        return (v_ix, 0, 0, 0)

    in_specs.append(pl.BlockSpec((1, NA_HEADS, Q_TILE, NA_WIN), bias_ix))
    return pl.pallas_call(
        functools.partial(_natten_kernel, ctx_tile=with_ctx),
        grid=(SEQ // Q_TILE + (1 if with_ctx else 0), n_batch),
        in_specs=in_specs,
        out_specs=out_spec,
        out_shape=jax.ShapeDtypeStruct((_attn_out_rows(t, with_ctx), 256), BF16),
        compiler_params=_params("arbitrary", "arbitrary"),
        name="attn_natten",
    )(*args, bias_tab)


def _natten_bias_table(rpb):
    n_dr, n_dc = 2 * NA_KH - 1, 2 * NA_KW - 1
    col = np.arange(GRID_W)
    dc = np.clip(col[None, :] - col[:, None] + NA_KW - 1, 0, n_dc - 1)
    onehot = jnp.asarray(dc[None] == np.arange(n_dc)[:, None, None], F32)
    blocks = jnp.einsum('hdc,cqk->hdqk', rpb.astype(F32), onehot, precision=lax.Precision.HIGHEST)
    cstart = np.clip(col - NA_KW // 2, 0, GRID_W - NA_KW)
    col_ok = (col[None, :] >= cstart[:, None]) & (col[None, :] < cstart[:, None] + NA_KW)
    blocks = jnp.where(col_ok[None, None], blocks, NEG_INF)
    blocks = jnp.concatenate([blocks, jnp.full((NA_HEADS, 1, GRID_W, GRID_W), NEG_INF, F32)], axis=1)
    tiles = np.asarray(_NA_VARIANT_TILE)
    r = NA_Q_ROWS * tiles[:, None] + np.arange(NA_Q_ROWS)[None, :]
    ws = np.clip(NA_Q_ROWS * tiles - NA_KH // 2, 0, GRID_ROWS - NA_WIN_ROWS)
    kr = ws[:, None] + np.arange(NA_WIN_ROWS)[None, :]
    krow0 = np.clip(r - NA_KH // 2, 0, GRID_ROWS - NA_KH)
    row_ok = (kr[:, None, :] >= krow0[:, :, None]) & (kr[:, None, :] < krow0[:, :, None] + NA_KH)
    idx = np.where(row_ok, kr[:, None, :] - r[:, :, None] + NA_KH - 1, n_dr)
    tab = jnp.take(blocks, jnp.asarray(idx.reshape(-1), jnp.int32), axis=1)
    tab = tab.reshape(NA_HEADS, len(tiles), NA_Q_ROWS, NA_WIN_ROWS, GRID_W, GRID_W)
    tab = jnp.transpose(tab, (1, 0, 2, 4, 3, 5))
    return tab.reshape(len(tiles), NA_HEADS, Q_TILE, NA_WIN)


def _merge_kernel(*refs, with_router):
    if with_router:
        (oa_ref, ob_ref, oc_ref, od_ref, gate_ref, x_ref, mod_ref, g2_ref, wb_ref, wo_ref, rt_ref,
         xo_ref, h2_ref, comb_ref) = refs
    else:
        (oa_ref, ob_ref, oc_ref, od_ref, gate_ref, x_ref, mod_ref, g2_ref, wb_ref, wo_ref,
         xo_ref, h2_ref) = refs
    m = mod_ref[0]
    acc = None
    for n, o_ref in enumerate((oa_ref, ob_ref, oc_ref, od_ref)):
        y = _dot(o_ref[...], wb_ref[n])
        g = _sigmoid(gate_ref[:, n * D_MODEL:(n + 1) * D_MODEL].astype(F32))
        acc = g * y if acc is None else acc + g * y
    mix = _dot(acc.astype(BF16), wo_ref[...])
    x = x_ref[...] + m[2:3] * mix
    xo_ref[...] = x
    h2 = _norm_mod(x, g2_ref[...], m[3:4], m[4:5])
    h2_ref[...] = h2.astype(BF16)
    if with_router:
        hh, hm, _ = _split3(h2)
        rh, rm = rt_ref[0], rt_ref[1]
        logits = _dot(hh, rh) + (_dot(hh, rm) + _dot(hm, rh))
        lane = lax.broadcasted_iota(jnp.int32, logits.shape, 1).astype(F32)
        logits = jnp.where(lane < N_EXPERTS, logits, NEG_INF)
        m1 = jnp.max(logits, axis=-1, keepdims=True)
        i1 = jnp.min(jnp.where(logits == m1, lane, float(LANES)), axis=-1, keepdims=True)
        rest = jnp.where(lane == i1, NEG_INF, logits)
        m2 = jnp.max(rest, axis=-1, keepdims=True)
        i2 = jnp.min(jnp.where(rest == m2, lane, float(LANES)), axis=-1, keepdims=True)
        e = jnp.exp(m2 - m1)
        w1 = 1.0 / (1.0 + e)
        w2 = e / (1.0 + e)
        comb_ref[...] = (jnp.where(lane == 0.0, i1, 0.0) + jnp.where(lane == 1.0, i2, 0.0)
                         + jnp.where(lane == 2.0, w1, 0.0) + jnp.where(lane == 3.0, w2, 0.0))


def _merge_call(outs, gate, x, mod_l, g2, wb, wo, router3, first_tile):
    t = x.shape[0]
    n_rows = t - first_tile * ROW_TILE

    def rows(width, col=0):
        return pl.BlockSpec((ROW_TILE, width), lambda i: (i + first_tile, col // width))

    def orow(width):
        return pl.BlockSpec((ROW_TILE, width), lambda i: (i, 0))

    def full(shape):
        return pl.BlockSpec(shape, lambda i: (0,) * len(shape))

    in_specs = [orow(256)] * 4 + [
        rows(GATE_W),
        rows(D_MODEL),
        pl.BlockSpec((1, ADA_CHUNKS, D_MODEL), lambda i: (_mod_row(i, first_tile), 0, 0)),
        full((1, D_MODEL)), full((N_BRANCH, BRANCH_W, D_MODEL)), full((D_MODEL, D_MODEL)),
    ]
    args = list(outs) + [gate, x, mod_l, g2, wb, wo]
    out_specs = [orow(D_MODEL), orow(D_MODEL)]
    out_shape = [jax.ShapeDtypeStruct((n_rows, D_MODEL), F32), jax.ShapeDtypeStruct((n_rows, D_MODEL), BF16)]
    if router3 is not None:
        in_specs.append(full((2, D_MODEL, LANES)))
        args.append(router3)
        out_specs.append(orow(LANES))
        out_shape.append(jax.ShapeDtypeStruct((n_rows, LANES), F32))
    return pl.pallas_call(
        functools.partial(_merge_kernel, with_router=router3 is not None),
        grid=(n_rows // ROW_TILE,),
        in_specs=in_specs,
        out_specs=out_specs,
        out_shape=out_shape,
        compiler_params=_params("arbitrary"),
        name="merge",
    )(*args)


FFN_ROWS = 1024


def _swiglu_chunk(h, w1, w3, w2):
    a = _dot(h, w1)
    b = _dot(h, w3)
    return _dot((a * _sigmoid(a) * b).astype(BF16), w2)


def _residual_out(x, gate, update, final_norm, final_g):
    x = x + gate * update
    if final_norm:
        ms = jnp.mean(x * x, axis=-1, keepdims=True)
        x = x * lax.rsqrt(ms + NORM_EPS) * final_g
    return x


def _ffn_kernel(h_ref, w1_ref, w3_ref, w2_ref, x_ref, mod_ref, fg_ref, o_ref, acc_ref, *, final_norm):
    f = pl.program_id(1)

    @pl.when(f == 0)
    def _():
        acc_ref[...] = jnp.zeros_like(acc_ref)

    acc_ref[...] += _swiglu_chunk(h_ref[...], w1_ref[...], w3_ref[...], w2_ref[...])

    @pl.when(f == pl.num_programs(1) - 1)
    def _():
        o_ref[...] = _residual_out(x_ref[...], mod_ref[0][5:6], acc_ref[...], final_norm, fg_ref[...])


def _tile_mod_row(i, first_tile, tile_rows):
    n_ctx = CTX_ROWS // tile_rows
    t = i + first_tile
    return jnp.where(t < n_ctx, 8, (t - n_ctx) // (SEQ // tile_rows))


def _ffn_call(h2, w1, w3, w2, x, mod_l, final_g, first_row_tile, final_norm):
    t = x.shape[0]
    rows = lambda width: pl.BlockSpec((FFN_ROWS, width), lambda i, f: (i, 0))
    return pl.pallas_call(
        functools.partial(_ffn_kernel, final_norm=final_norm),
        grid=(t // FFN_ROWS, D_FF // FF_CHUNK),
        in_specs=[
            rows(D_MODEL),
            pl.BlockSpec((D_MODEL, FF_CHUNK), lambda i, f: (0, f)),
            pl.BlockSpec((D_MODEL, FF_CHUNK), lambda i, f: (0, f)),
            pl.BlockSpec((FF_CHUNK, D_MODEL), lambda i, f: (f, 0)),
            rows(D_MODEL),
            pl.BlockSpec((1, ADA_CHUNKS, D_MODEL),
                         lambda i, f: (_tile_mod_row(i, first_row_tile, FFN_ROWS), 0, 0)),
            pl.BlockSpec((1, D_MODEL), lambda i, f: (0, 0)),
        ],
        out_specs=rows(D_MODEL),
        out_shape=jax.ShapeDtypeStruct((t, D_MODEL), F32),
        scratch_shapes=[pltpu.VMEM((FFN_ROWS, D_MODEL), F32)],
        compiler_params=_params("arbitrary", "arbitrary"),
        name="ffn",
    )(h2, w1, w3, w2, x, mod_l, final_g)


MOE_TILE = 512
MOE_SRC = 256
MOE_CMB_WIN = MOE_SRC + 16
MOE_DSP_SMALL = 144
MOE_DSP_LARGE = MOE_SRC + 16
TOP_K = 2


def _moe_route(sel, n_tok):
    n_tiles = TOP_K * n_tok // MOE_TILE + N_EXPERTS
    e = sel[:, 0:TOP_K].astype(jnp.int32)
    flat_e = e.reshape(-1)
    onehot = (flat_e[:, None] == jnp.arange(N_EXPERTS)[None, :]).astype(jnp.int32)
    csum = jnp.cumsum(onehot, axis=0)
    rank = jnp.sum((csum - onehot) * onehot, axis=1)
    counts = csum[-1]
    padded = (counts + MOE_TILE - 1) // MOE_TILE * MOE_TILE
    seg_end = jnp.cumsum(padded)
    seg_start = seg_end - padded
    pos = jnp.sum(onehot * seg_start[None, :], axis=1) + rank
    pos2 = pos.reshape(n_tok, TOP_K)
    tile_ix = jnp.arange(n_tiles)
    tile_expert = jnp.minimum(jnp.sum(tile_ix[:, None] >= (seg_end // MOE_TILE)[None, :], axis=1), N_EXPERTS - 1)
    n_valid = seg_end[-1] // MOE_TILE
    n_src = n_tok // MOE_SRC
    per_chunk = MOE_SRC * TOP_K
    before = jnp.concatenate([jnp.zeros((1, N_EXPERTS), jnp.int32), csum[per_chunk - 1::per_chunk]], axis=0)
    run_start = seg_start[None, :] + before[:-1]
    run_cnt = before[1:] - before[:-1]
    cum = jnp.take(before, tile_expert, axis=1)
    local0 = tile_ix * MOE_TILE - jnp.take(seg_start, tile_expert)
    local1 = jnp.minimum(local0 + MOE_TILE, jnp.take(counts, tile_expert))
    c_lo = jnp.sum(cum[1:] <= local0[None, :], axis=0)
    c_hi = jnp.sum(cum[:-1] < local1[None, :], axis=0) - 1
    win = jnp.minimum(run_start // 16 * 16, n_tiles * MOE_TILE - MOE_CMB_WIN)
    def token_rows(a, dtype):
        rows = jnp.transpose(a.reshape(n_src, MOE_SRC, TOP_K), (0, 2, 1)).astype(dtype)
        return jnp.zeros((n_src, 8, MOE_SRC), dtype).at[:, :TOP_K, :].set(rows)

    i32 = lambda a: a.astype(jnp.int32)
    return {
        "n_tiles": n_tiles, "tile_expert": i32(tile_expert), "n_valid": i32(n_valid).reshape(1),
        "c_lo": i32(c_lo), "c_hi": i32(c_hi), "run_start": i32(run_start.reshape(-1)),
        "run_cnt": i32(run_cnt.reshape(-1)), "win": i32(win.reshape(-1)),
        "run_off": i32((run_start - win).reshape(-1)),
        "pos_rows": token_rows(pos2, jnp.int32), "w_rows": token_rows(sel[:, TOP_K:2 * TOP_K], F32),
        "pos_cols": i32(pos2),
    }


def _dispatch_kernel(clo_ref, chi_ref, te_ref, rs_ref, rc_ref, h_ref, pos_ref, w_ref, o_ref, sw_ref, acc_ref):
    i = pl.program_id(0)
    base = i * MOE_TILE
    expert = te_ref[i]
    acc_ref[...] = jnp.zeros_like(acc_ref)
    sw_ref[...] = jnp.zeros_like(sw_ref)

    def body(c, carry):
        run0 = rs_ref[c * N_EXPERTS + expert]
        lo = jnp.maximum(run0, base) - base
        hi = jnp.minimum(run0 + rc_ref[c * N_EXPERTS + expert], base + MOE_TILE) - base
        pos = pos_ref[c]
        wts = w_ref[c]
        h = h_ref[pl.ds(pl.multiple_of(c * MOE_SRC, MOE_SRC), MOE_SRC), :]

        def window(rows):
            ws = pl.multiple_of(jnp.minimum(lo // 16 * 16, MOE_TILE - rows), 16)
            slot = base + ws + lax.broadcasted_iota(jnp.int32, (rows, MOE_SRC), 0)
            hit0 = pos[0:1, :] == slot
            hit1 = pos[1:2, :] == slot
            acc_ref[pl.ds(ws, rows), :] += _dot(jnp.where(hit0 | hit1, 1.0, 0.0).astype(BF16), h)
            weight = jnp.sum(jnp.where(hit0, wts[0:1, :], 0.0) + jnp.where(hit1, wts[1:2, :], 0.0),
                             axis=1, keepdims=True)
            sw_ref[pl.ds(ws, rows), :] += jnp.broadcast_to(weight, (rows, LANES))

        small = hi - jnp.minimum(lo // 16 * 16, MOE_TILE - MOE_DSP_SMALL) <= MOE_DSP_SMALL
        pl.when((hi > lo) & small)(lambda: window(MOE_DSP_SMALL))
        pl.when((hi > lo) & jnp.logical_not(small))(lambda: window(MOE_DSP_LARGE))
        return carry

    lax.fori_loop(clo_ref[i], chi_ref[i] + 1, body, 0)
    o_ref[...] = acc_ref[...].astype(BF16)


def _dispatch_call(h2, route):
    n_tok = h2.shape[0]
    n_tiles = route["n_tiles"]
    grid_spec = pltpu.PrefetchScalarGridSpec(
        num_scalar_prefetch=5,
        grid=(n_tiles,),
        in_specs=[
            pl.BlockSpec((n_tok, D_MODEL), lambda i, *_: (0, 0), pipeline_mode=pl.Buffered(1)),
            pl.BlockSpec((n_tok // MOE_SRC, 8, MOE_SRC), lambda i, *_: (0, 0, 0), pipeline_mode=pl.Buffered(1)),
            pl.BlockSpec((n_tok // MOE_SRC, 8, MOE_SRC), lambda i, *_: (0, 0, 0), pipeline_mode=pl.Buffered(1)),
        ],
        out_specs=[pl.BlockSpec((MOE_TILE, D_MODEL), lambda i, *_: (i, 0)),
                   pl.BlockSpec((MOE_TILE, LANES), lambda i, *_: (i, 0))],
        scratch_shapes=[pltpu.VMEM((MOE_TILE, D_MODEL), F32)],
    )
    return pl.pallas_call(
        _dispatch_kernel,
        grid_spec=grid_spec,
        out_shape=[jax.ShapeDtypeStruct((n_tiles * MOE_TILE, D_MODEL), BF16),
                   jax.ShapeDtypeStruct((n_tiles * MOE_TILE, LANES), F32)],
        compiler_params=_params("arbitrary"),
        name="moe_dispatch",
    )(route["c_lo"], route["c_hi"], route["tile_expert"], route["run_start"], route["run_cnt"],
      h2, route["pos_rows"], route["w_rows"])


def _expert_kernel(te_ref, nv_ref, x_ref, w1_ref, w3_ref, w2_ref, sw_ref, o_ref, acc_ref):
    i = pl.program_id(0)
    f = pl.program_id(1)
    last = pl.num_programs(1) - 1
    valid = i < nv_ref[0]

    @pl.when(valid & (f == 0))
    def _():
        acc_ref[...] = jnp.zeros_like(acc_ref)

    @pl.when(valid)
    def _():
        acc_ref[...] += _swiglu_chunk(x_ref[...], w1_ref[0, 0], w3_ref[0, 0], w2_ref[0, 0])

    @pl.when(valid & (f == last))
    def _():
        o_ref[...] = (acc_ref[...] * sw_ref[:, 0:1]).astype(BF16)

    @pl.when(jnp.logical_not(valid) & (f == last))
    def _():
        o_ref[...] = jnp.zeros_like(o_ref)


def _expert_call(xs, slot_w, route, w1, w3, w2, layer):
    n_tiles = route["n_tiles"]
    grid_spec = pltpu.PrefetchScalarGridSpec(
        num_scalar_prefetch=2,
        grid=(n_tiles, D_FF // MOE_FF_CHUNK),
        in_specs=[
            pl.BlockSpec((MOE_TILE, D_MODEL), lambda i, f, te, nv: (i, 0)),
            pl.BlockSpec((1, 1, D_MODEL, MOE_FF_CHUNK), lambda i, f, te, nv: (layer, te[i], 0, f)),
            pl.BlockSpec((1, 1, D_MODEL, MOE_FF_CHUNK), lambda i, f, te, nv: (layer, te[i], 0, f)),
            pl.BlockSpec((1, 1, MOE_FF_CHUNK, D_MODEL), lambda i, f, te, nv: (layer, te[i], f, 0)),
            pl.BlockSpec((MOE_TILE, LANES), lambda i, f, te, nv: (i, 0)),
        ],
        out_specs=pl.BlockSpec((MOE_TILE, D_MODEL), lambda i, f, te, nv: (i, 0)),
        scratch_shapes=[pltpu.VMEM((MOE_TILE, D_MODEL), F32)],
    )
    return pl.pallas_call(
        _expert_kernel,
        grid_spec=grid_spec,
        out_shape=jax.ShapeDtypeStruct((n_tiles * MOE_TILE, D_MODEL), BF16),
        compiler_params=_params("arbitrary", "arbitrary"),
        name="moe_experts",
    )(route["tile_expert"], route["n_valid"], xs, w1, w3, w2, slot_w)


def _combine_kernel(win_ref, off_ref, cnt_ref, y_hbm, pos_ref, x_ref, mod_ref, fg_ref, o_ref, buf_ref, sem_ref,
                    *, final_norm):
    c = pl.program_id(0)
    cur = c % 2

    def window_copies(chunk, buf_set):
        return [pltpu.make_async_copy(
            y_hbm.at[pl.ds(pl.multiple_of(win_ref[chunk * N_EXPERTS + e], 16), MOE_CMB_WIN), :],
            buf_ref.at[buf_set, pl.ds(e * MOE_CMB_WIN, MOE_CMB_WIN), :],
            sem_ref.at[buf_set, e]) for e in range(N_EXPERTS)]

    @pl.when(c == 0)
    def _():
        for cp in window_copies(0, 0):
            cp.start()

    @pl.when(c + 1 < pl.num_programs(0))
    def _():
        for cp in window_copies(c + 1, 1 - cur):
            cp.start()

    row = lax.broadcasted_iota(jnp.int32, (1, N_EXPERTS * MOE_CMB_WIN), 1)
    slot = jnp.full((1, N_EXPERTS * MOE_CMB_WIN), -1, jnp.int32)
    for e in range(N_EXPERTS):
        local = row - e * MOE_CMB_WIN
        off = off_ref[c * N_EXPERTS + e]
        inside = (local >= off) & (local < off + cnt_ref[c * N_EXPERTS + e]) & (local < MOE_CMB_WIN)
        slot = jnp.where(inside, win_ref[c * N_EXPERTS + e] + local, slot)
    pos = pos_ref[...]
    hit = (pos[:, 0:1] == slot) | (pos[:, 1:2] == slot)
    onehot = jnp.where(hit, 1.0, 0.0).astype(BF16)
    for cp in window_copies(c, cur):
        cp.wait()
    update = _dot(onehot, buf_ref[cur])
    o_ref[...] = _residual_out(x_ref[...], mod_ref[0][5:6], update, final_norm, fg_ref[...])


def _combine_call(ys, route, x, mod_l, final_g, first_tile, final_norm):
    n_tok = x.shape[0]
    rows = lambda width: pl.BlockSpec((MOE_SRC, width), lambda c, *_: (c, 0))
    grid_spec = pltpu.PrefetchScalarGridSpec(
        num_scalar_prefetch=3,
        grid=(n_tok // MOE_SRC,),
        in_specs=[
            pl.BlockSpec(memory_space=pl.ANY),
            rows(TOP_K),
            rows(D_MODEL),
            pl.BlockSpec((1, ADA_CHUNKS, D_MODEL), lambda c, *_: (_tile_mod_row(c, first_tile, MOE_SRC), 0, 0)),
            pl.BlockSpec((1, D_MODEL), lambda c, *_: (0, 0)),
        ],
        out_specs=rows(D_MODEL),
        scratch_shapes=[pltpu.VMEM((2, N_EXPERTS * MOE_CMB_WIN, D_MODEL), BF16),
                        pltpu.SemaphoreType.DMA((2, N_EXPERTS))],
    )
    return pl.pallas_call(
        functools.partial(_combine_kernel, final_norm=final_norm),
        grid_spec=grid_spec,
        out_shape=jax.ShapeDtypeStruct((n_tok, D_MODEL), F32),
        compiler_params=_params("arbitrary"),
        name="moe_combine",
    )(route["win"], route["run_off"], route["run_cnt"], ys, route["pos_cols"], x, mod_l, final_g)


def _rope_tables():
    t = np.arange(SEQ)

    def angles(rot_dim):
        half = rot_dim // 2
        inv = ROPE_THETA ** (-jnp.arange(0, half, 2, dtype=F32) / half)
        ang = jnp.concatenate([jnp.asarray(t // GRID_W, F32)[:, None] * inv[None, :],
                               jnp.asarray(t % GRID_W, F32)[:, None] * inv[None, :]], axis=-1)
        return jnp.cos(ang), jnp.sin(ang)

    def pad_rows(a, fill):
        return jnp.concatenate([jnp.full((ROW_TILE, a.shape[1]), fill, F32), a], axis=0)

    c, s = angles(HEAD_DIM)
    cos64 = jnp.tile(jnp.concatenate([c, c], axis=-1), (1, 4))
    sin64 = jnp.tile(jnp.concatenate([s, s], axis=-1), (1, 4))
    c, s = angles(MLA_ROPE)
    one = jnp.ones((SEQ, MLA_NOPE), F32)
    zero = jnp.zeros((SEQ, MLA_NOPE), F32)
    tail1 = jnp.ones((SEQ, MLA_PAD - MLA_NOPE - MLA_ROPE), F32)
    tail0 = jnp.zeros((SEQ, MLA_PAD - MLA_NOPE - MLA_ROPE), F32)
    cosm = jnp.tile(jnp.concatenate([one, c, c, tail1], axis=-1), (1, MLA_HEADS))
    sinm = jnp.tile(jnp.concatenate([zero, s, s, tail0], axis=-1), (1, MLA_HEADS))
    blockdiag = jnp.asarray(np.kron(np.eye(4), np.ones((HEAD_DIM, HEAD_DIM))), BF16)
    place = np.zeros((256, MLA_HEADS * MLA_PAD), np.float32)
    for h in range(MLA_HEADS):
        for r in range(MLA_ROPE):
            place[MLA_KV_RANK + r, h * MLA_PAD + MLA_NOPE + r] = 1.0
    return {
        "cos64": pad_rows(cos64, 1.0), "sin64": pad_rows(sin64, 0.0),
        "cosm": pad_rows(cosm, 1.0), "sinm": pad_rows(sinm, 0.0),
        "blockdiag": blockdiag, "pe_place": jnp.asarray(place, BF16),
    }


def _pack_w_in(w):
    n_qkv = COL_DKVA + MLA_KV_RANK + MLA_ROPE
    pad = jnp.zeros(w.shape[:2] + (COL_GATE - n_qkv,), BF16)
    return jnp.concatenate([w[..., :n_qkv].astype(BF16), pad, w[..., n_qkv:].astype(BF16)], axis=-1)


def _pack_mla(wqb, wkvb):
    dqh = MLA_NOPE + MLA_ROPE
    q = wqb.reshape(MLA_Q_RANK, MLA_HEADS, dqh)
    q = jnp.pad(q, ((0, 0), (0, 0), (0, MLA_PAD - dqh))).reshape(MLA_Q_RANK, MLA_HEADS * MLA_PAD)
    kv = wkvb.reshape(MLA_KV_RANK, MLA_HEADS, MLA_NOPE + MLA_V)
    k = jnp.pad(kv[:, :, :MLA_NOPE], ((0, 0), (0, 0), (0, MLA_PAD - MLA_NOPE)))
    k = k.reshape(MLA_KV_RANK, MLA_HEADS * MLA_PAD)
    v = jnp.pad(kv[:, :, MLA_NOPE:], ((0, 0), (0, 0), (0, MLA_PAD - MLA_V))).reshape(MLA_KV_RANK, MLA_HEADS * MLA_PAD)
    return q.astype(BF16), k.astype(BF16), v.astype(BF16)


def _split2_host(w):
    hi = w.astype(BF16)
    return jnp.stack([hi, (w - hi.astype(F32)).astype(BF16)])


def kernel(x, c, ctx, c_ctx, norm1_g, norm2_g, w_ada, b_ada, w_in, na_rpb, gb_qnorm, gb_knorm, wc_sink,
           mla_qnorm, mla_kvnorm, mla_wqb, mla_wkvb, w_branch, w_out, ffn_w1, ffn_w3, ffn_w2,
           moe_router, moe_w1, moe_w3, moe_w2, final_g):
    n_batch = x.shape[0]
    assert x.shape[1:] == (SEQ, D_MODEL) and ctx.shape[1:] == (CTX_LEN, D_MODEL)
    assert n_batch * CTX_LEN <= CTX_ROWS and n_batch <= 8

    ctx_rows = ctx.reshape(n_batch * CTX_LEN, D_MODEL)
    if ctx_rows.shape[0] < CTX_ROWS:
        ctx_rows = jnp.pad(ctx_rows, ((0, CTX_ROWS - ctx_rows.shape[0]), (0, 0)))
    xt = jnp.concatenate([ctx_rows, x.reshape(n_batch * SEQ, D_MODEL)], axis=0)

    cvec = jnp.zeros((16, D_MODEL), F32).at[:n_batch].set(c).at[8].set(c_ctx)
    mod = _mod_call(cvec, w_ada, b_ada).reshape(DEPTH, 16, ADA_CHUNKS, D_MODEL)
    tabs = _rope_tables()
    w_in_p = _pack_w_in(w_in)
    moe_w = tuple(w.astype(BF16) for w in (moe_w1, moe_w3, moe_w2))
    mla_scale = (MLA_NOPE + MLA_ROPE) ** -0.5
    lat_tile0 = CTX_ROWS // ROW_TILE

    for l in range(DEPTH):
        with_ctx = l < DEPTH - 1
        wq, wk, wv = _pack_mla(mla_wqb[l], mla_wkvb[l])
        lw = {
            "gq": jnp.tile(gb_qnorm[l], 4)[None, :], "gk": jnp.tile(gb_knorm[l], 2)[None, :],
            "qn": mla_qnorm[l][None, :], "kvn": mla_kvnorm[l][None, :], "wqb": wq, "wk": wk, "wv": wv,
        }
        p, gate = _inproj_call(xt, mod[l], norm1_g[l][None, :], w_in_p, l)
        qb, kb, qc, kc, qd, kd, vd, vb = _prep_call(p, tabs, lw)

        oa = _natten_call(n_batch, with_ctx, p, _natten_bias_table(na_rpb[l]))
        ob = _dense_attn_call(n_batch, with_ctx, (qb, 256, 0), kb, (vb, GB_KV_HEADS * LANES, 0),
                              n_kv=GB_KV_HEADS, group=GB_Q_HEADS // GB_KV_HEADS, dq=HEAD_DIM, name="attn_global")
        oc = _window_attn_call(n_batch, with_ctx, (qc, 256, 0), (kc, 128, 0), (p, 128, COL_CV), wc_sink[l])
        od = _dense_attn_call(n_batch, with_ctx, (qd, MLA_HEADS * MLA_PAD, 0), kd, (vd, MLA_HEADS * MLA_PAD, 0),
                              n_kv=MLA_HEADS, group=1, dq=MLA_PAD, scale=mla_scale, name="attn_mla")

        is_moe = l % 2 == 1
        router3 = None
        if is_moe:
            router3 = _split2_host(jnp.pad(moe_router[l // 2], ((0, 0), (0, LANES - N_EXPERTS))))
        first_tile = 0 if with_ctx else lat_tile0
        res = _merge_call((oa, ob, oc, od), gate, xt, mod[l], norm2_g[l][None, :],
                          w_branch[l].astype(BF16), w_out[l].astype(BF16), router3, first_tile)
        x_mid, h2 = res[0], res[1]
        final_norm = l == DEPTH - 1
        if is_moe:
            route = _moe_route(res[2], h2.shape[0])
            xs, slot_w = _dispatch_call(h2, route)
            ys = _expert_call(xs, slot_w, route, *moe_w, l // 2)
            xt = _combine_call(ys, route, x_mid, mod[l], final_g[None, :],
                               0 if with_ctx else CTX_ROWS // MOE_SRC, final_norm)
        else:
            w1, w3, w2 = (w[l // 2].astype(BF16) for w in (ffn_w1, ffn_w3, ffn_w2))
            xt = _ffn_call(h2, w1, w3, w2, x_mid, mod[l], final_g[None, :],
                           0 if with_ctx else CTX_ROWS // FFN_ROWS, final_norm)

    return xt.reshape(n_batch, SEQ, D_MODEL)
```

```python
import functools

import numpy as np
import jax
import jax.numpy as jnp
from jax import lax
from jax.experimental import pallas as pl
from jax.experimental.pallas import tpu as pltpu

F32 = jnp.float32
BF16 = jnp.bfloat16

D_MODEL = 1024
SEQ = 2048
DEPTH = 4
CTX_LEN = 256
GRID_W = 64
GRID_ROWS = SEQ // GRID_W
HEAD_DIM = 64
ROPE_THETA = 10000.0
NORM_EPS = 1e-6
NEG_INF = -1e30

NA_HEADS = 4
NA_KH = 8
NA_KW = 16
GB_Q_HEADS = 4
GB_KV_HEADS = 2
WC_Q_HEADS = 4
WC_KV_HEADS = 2
WC_WINDOW = 128
MLA_HEADS = 4
MLA_Q_RANK = 256
MLA_KV_RANK = 128
MLA_NOPE = 64
MLA_ROPE = 32
MLA_V = 64
MLA_PAD = 128
N_BRANCH = 4
BRANCH_W = 256
D_FF = 3584
N_EXPERTS = 8
ADA_CHUNKS = 6

VMEM_LIMIT_BYTES = 56 * 1024 * 1024
LANES = 128

CTX_ROWS = 2048
ROW_TILE = 512
Q_TILE = 256
NA_Q_ROWS = Q_TILE // GRID_W
NA_WIN_ROWS = 12
NA_WIN = NA_WIN_ROWS * GRID_W
WC_WIN = Q_TILE + 2 * WC_WINDOW
ATTN_KEY_CHUNK = 512

COL_AQ, COL_AK, COL_AV = 0, 256, 512
COL_BQ, COL_BK, COL_BV = 768, 1024, 1152
COL_CQ, COL_CK, COL_CV = 1280, 1536, 1664
COL_DQA, COL_DKVA, COL_GATE = 1792, 2048, 2304
GATE_W = N_BRANCH * D_MODEL
P_WIDTH = COL_GATE + GATE_W
QKV_CHUNK = COL_GATE // 2
FF_CHUNK = 512
MOE_FF_CHUNK = 1792


def _params(*sem):
    return pltpu.CompilerParams(dimension_semantics=sem, vmem_limit_bytes=VMEM_LIMIT_BYTES)


def _dot(a, b):
    return jnp.dot(a, b, preferred_element_type=F32)


def _dot_nt(a, b):
    return lax.dot_general(a, b, (((1,), (1,)), ((), ())), preferred_element_type=F32)


def _split3(x):
    hi = x.astype(BF16)
    r1 = x - hi.astype(F32)
    mid = r1.astype(BF16)
    lo = (r1 - mid.astype(F32)).astype(BF16)
    return hi, mid, lo


def _sigmoid(x):
    return 0.5 * jnp.tanh(0.5 * x) + 0.5


ADA_TN = 1536


def _mod_kernel(c_ref, w_ref, b_ref, o_ref):
    c = c_ref[...]
    sc = (c * _sigmoid(c)).astype(BF16)
    o_ref[0] = _dot(sc, w_ref[0].astype(BF16)) + b_ref[0]


def _mod_call(cvec, w_ada, b_ada):
    n = ADA_CHUNKS * D_MODEL
    return pl.pallas_call(
        _mod_kernel,
        grid=(DEPTH, n // ADA_TN),
        in_specs=[
            pl.BlockSpec((16, D_MODEL), lambda l, j: (0, 0)),
            pl.BlockSpec((1, D_MODEL, ADA_TN), lambda l, j: (l, 0, j)),
            pl.BlockSpec((1, 1, ADA_TN), lambda l, j: (l, 0, j)),
        ],
        out_specs=pl.BlockSpec((1, 16, ADA_TN), lambda l, j: (l, 0, j)),
        out_shape=jax.ShapeDtypeStruct((DEPTH, 16, n), F32),
        compiler_params=_params("arbitrary", "arbitrary"),
        name="adaln_mod",
    )(cvec, w_ada, b_ada.reshape(DEPTH, 1, n))


def _mod_row(i, first_tile):
    n_ctx = CTX_ROWS // ROW_TILE
    t = i + first_tile
    return jnp.where(t < n_ctx, 8, (t - n_ctx) // (SEQ // ROW_TILE))


def _norm_mod(x, g, shift, scale):
    ms = jnp.mean(x * x, axis=-1, keepdims=True)
    y = x * lax.rsqrt(ms + NORM_EPS) * g
    return y * (1.0 + scale) + shift


def _inproj_kernel(x_ref, mod_ref, g_ref, w_ref, p_ref, gate_ref):
    m = mod_ref[0]
    hb = _norm_mod(x_ref[...], g_ref[...], m[0:1], m[1:2]).astype(BF16)
    for c0 in range(0, COL_GATE, QKV_CHUNK):
        p_ref[:, c0:c0 + QKV_CHUNK] = _dot(hb, w_ref[0, :, c0:c0 + QKV_CHUNK]).astype(BF16)
    for c0 in range(0, GATE_W, D_MODEL):
        gate_ref[:, c0:c0 + D_MODEL] = _dot(hb, w_ref[0, :, COL_GATE + c0:COL_GATE + c0 + D_MODEL]).astype(BF16)


def _inproj_call(x, mod_l, g, w_in_p, layer):
    t = x.shape[0]
    return pl.pallas_call(
        _inproj_kernel,
        grid=(t // ROW_TILE,),
        in_specs=[
            pl.BlockSpec((ROW_TILE, D_MODEL), lambda i: (i, 0)),
            pl.BlockSpec((1, ADA_CHUNKS, D_MODEL), lambda i: (_mod_row(i, 0), 0, 0)),
            pl.BlockSpec((1, D_MODEL), lambda i: (0, 0)),
            pl.BlockSpec((1, D_MODEL, P_WIDTH), lambda i: (layer, 0, 0), pipeline_mode=pl.Buffered(1)),
        ],
        out_specs=[pl.BlockSpec((ROW_TILE, COL_GATE), lambda i: (i, 0)),
                   pl.BlockSpec((ROW_TILE, GATE_W), lambda i: (i, 0))],
        out_shape=[jax.ShapeDtypeStruct((t, COL_GATE), BF16), jax.ShapeDtypeStruct((t, GATE_W), BF16)],
        compiler_params=_params("arbitrary"),
        name="inproj",
    )(x, mod_l, g, w_in_p)


def _rope(x, cos, sin, half, first_mask):
    w = x.shape[-1]
    fwd = pltpu.roll(x, w - half, 1)
    bwd = pltpu.roll(x, half, 1)
    rot = jnp.where(first_mask, -fwd, bwd)
    return x * cos + rot * sin


def _head_rms(x, gain, blockdiag):
    hi, mid, lo = _split3(x * x)
    ss = _dot(hi, blockdiag) + _dot(mid, blockdiag) + _dot(lo, blockdiag)
    return x * lax.rsqrt(ss * (1.0 / HEAD_DIM) + NORM_EPS) * gain


def _prep_kernel(bq_ref, bk_ref, cq_ref, ck_ref, dqa_ref, dkva_ref, bv_ref,
                 cos64_ref, sin64_ref, cosm_ref, sinm_ref,
                 gq_ref, gk_ref, bd_ref, qn_ref, kvn_ref, wqb_ref, wk_ref, wv_ref, pe_ref,
                 qb_ref, kb_ref, qc_ref, kc_ref, qd_ref, kd_ref, vd_ref, vb_ref):
    scale = HEAD_DIM ** -0.5
    cos64, sin64 = cos64_ref[...], sin64_ref[...]
    lane = lax.broadcasted_iota(jnp.int32, (1, 256), 1)
    first64 = (lane % HEAD_DIM) < (HEAD_DIM // 2)
    bd = bd_ref[...]

    bq = _head_rms(bq_ref[...].astype(F32), gq_ref[...], bd)
    qb_ref[...] = (_rope(bq, cos64, sin64, HEAD_DIM // 2, first64) * scale).astype(BF16)
    bk = _head_rms(bk_ref[...].astype(F32), gk_ref[...], bd[:128, :128])
    kb_ref[...] = _rope(bk, cos64[:, :128], sin64[:, :128], HEAD_DIM // 2, first64[:, :128]).T.astype(BF16)
    qc_ref[...] = (_rope(cq_ref[...].astype(F32), cos64, sin64, HEAD_DIM // 2, first64) * scale).astype(BF16)
    kc_ref[...] = _rope(ck_ref[...].astype(F32), cos64[:, :128], sin64[:, :128], HEAD_DIM // 2,
                        first64[:, :128]).astype(BF16)

    cosm, sinm = cosm_ref[...], sinm_ref[...]
    lane_m = lax.broadcasted_iota(jnp.int32, (1, MLA_HEADS * MLA_PAD), 1) % MLA_PAD
    first_m = lane_m < (MLA_NOPE + MLA_ROPE // 2)
    dqa = dqa_ref[...].astype(F32)
    qn = dqa * lax.rsqrt(jnp.mean(dqa * dqa, axis=-1, keepdims=True) + NORM_EPS) * qn_ref[...]
    dq = _dot(qn.astype(BF16), wqb_ref[...])
    qd_ref[...] = _rope(dq, cosm, sinm, MLA_ROPE // 2, first_m).astype(BF16)

    dkva = dkva_ref[...]
    kvc = dkva[:, :MLA_KV_RANK].astype(F32)
    kvn = (kvc * lax.rsqrt(jnp.mean(kvc * kvc, axis=-1, keepdims=True) + NORM_EPS) * kvn_ref[...]).astype(BF16)
    dk = _dot(kvn, wk_ref[...]) + _dot(dkva, pe_ref[...])
    kd_ref[...] = _rope(dk, cosm, sinm, MLA_ROPE // 2, first_m).T.astype(BF16)
    ones_half = lane_m >= HEAD_DIM
    vd_ref[...] = jnp.where(ones_half, 1.0, _dot(kvn, wv_ref[...])).astype(BF16)
    bv = bv_ref[...]
    vb_ref[:, :LANES] = jnp.where(ones_half[:, :LANES], 1.0, bv).astype(BF16)
    vb_ref[:, LANES:] = jnp.where(ones_half[:, :LANES], 1.0, pltpu.roll(bv.astype(F32), HEAD_DIM, 1)).astype(BF16)


def _rope_row_block(i):
    n_ctx = CTX_ROWS // ROW_TILE
    return jnp.where(i < n_ctx, 0, 1 + (i - n_ctx) % (SEQ // ROW_TILE))


def _prep_call(p, tabs, lw):
    t = p.shape[0]
    mw = MLA_HEADS * MLA_PAD

    def pcol(width, col):
        return pl.BlockSpec((ROW_TILE, width), lambda i: (i, col // width))

    def full(shape):
        return pl.BlockSpec(shape, lambda i: (0,) * len(shape))

    def rows(width):
        return pl.BlockSpec((ROW_TILE, width), lambda i: (i, 0))

    def tab(width):
        return pl.BlockSpec((ROW_TILE, width), lambda i: (_rope_row_block(i), 0))

    def cols(height):
        return pl.BlockSpec((height, ROW_TILE), lambda i: (0, i))

    outs = [(256, False), (128, True), (256, False), (128, False), (mw, False), (mw, True), (mw, False),
            (GB_KV_HEADS * LANES, False)]
    return pl.pallas_call(
        _prep_kernel,
        grid=(t // ROW_TILE,),
        in_specs=[
            pcol(256, COL_BQ), pcol(128, COL_BK), pcol(256, COL_CQ), pcol(128, COL_CK),
            pcol(256, COL_DQA), pcol(256, COL_DKVA), pcol(128, COL_BV),
            tab(256), tab(256), tab(mw), tab(mw),
            full((1, 256)), full((1, 128)), full((256, 256)), full((1, MLA_Q_RANK)), full((1, MLA_KV_RANK)),
            full((MLA_Q_RANK, mw)), full((MLA_KV_RANK, mw)), full((MLA_KV_RANK, mw)), full((256, mw)),
        ],
        out_specs=[cols(w) if tr else rows(w) for w, tr in outs],
        out_shape=[jax.ShapeDtypeStruct((w, t) if tr else (t, w), BF16) for w, tr in outs],
        compiler_params=_params("arbitrary"),
        name="mixer_prep",
    )(p, p, p, p, p, p, p, tabs["cos64"], tabs["sin64"], tabs["cosm"], tabs["sinm"],
      lw["gq"], lw["gk"], tabs["blockdiag"], lw["qn"], lw["kvn"], lw["wqb"], lw["wk"], lw["wv"], tabs["pe_place"])


def _attend(q, segs, sink=None, scale=None):
    chunks = segs
    scores = []
    m = None
    for k, _, bias in chunks:
        s = _dot_nt(q, k)
        if scale is not None:
            s = s * scale
        if bias is not None:
            s = s + bias
        scores.append(s)
        ms = jnp.max(s, axis=-1, keepdims=True)
        m = ms if m is None else jnp.maximum(m, ms)
    if sink is not None:
        m = jnp.maximum(m, sink)
    denom = None
    out = None
    for s, (_, v, _) in zip(scores, chunks):
        p = jnp.exp(s - m)
        ps = jnp.sum(p, axis=-1, keepdims=True)
        pv = _dot(p.astype(BF16), v)
        denom = ps if denom is None else denom + ps
        out = pv if out is None else out + pv
    if sink is not None:
        denom = denom + jnp.exp(sink - m)
    return out / denom


def _gqa_heads(q_ref, o_ref, sink_ref, n_kv, group, dq, dv, scale, seg_fn):
    tq = q_ref.shape[0]
    for h in range(n_kv):
        heads = [h * group + g for g in range(group)]
        q = jnp.concatenate([q_ref[:, a * dq:(a + 1) * dq] for a in heads], axis=0) if group > 1 \
            else q_ref[:, h * dq:(h + 1) * dq]
        sink = None
        if sink_ref is not None:
            sink = jnp.concatenate([jnp.full((tq, 1), sink_ref[a], F32) for a in heads], axis=0)
        o = _attend(q, seg_fn(h), sink=sink, scale=scale)
        for g, a in enumerate(heads):
            o_ref[:, a * dv:(a + 1) * dv] = o[g * tq:(g + 1) * tq].astype(BF16)


def _dense_attn_kernel(q_ref, ktl_ref, vl_ref, ktc_ref, vc_ref, o_ref, *, n_kv, group, dq, scale, ctx_tile):
    tq = q_ref.shape[0]
    hd = HEAD_DIM

    def run(with_latent):
        for h in range(n_kv):
            heads = [h * group + g for g in range(group)]
            q = jnp.concatenate([q_ref[:, a * dq:(a + 1) * dq] for a in heads], axis=0) if group > 1 \
                else q_ref[:, h * dq:(h + 1) * dq]
            segs = [(ktc_ref[h * dq:(h + 1) * dq, :], vc_ref[:, h * LANES:(h + 1) * LANES])]
            if with_latent:
                segs = [(ktl_ref[h * dq:(h + 1) * dq, c0:c0 + ATTN_KEY_CHUNK],
                         vl_ref[c0:c0 + ATTN_KEY_CHUNK, h * LANES:(h + 1) * LANES])
                        for c0 in range(0, SEQ, ATTN_KEY_CHUNK)] + segs
            scores = []
            m = None
            for kt, _ in segs:
                s = _dot(q, kt)
                if scale is not None:
                    s = s * scale
                scores.append(s)
                ms = jnp.max(s, axis=-1, keepdims=True)
                m = ms if m is None else jnp.maximum(m, ms)
            acc = None
            for s, (_, v) in zip(scores, segs):
                pv = _dot(jnp.exp(s - m).astype(BF16), v)
                acc = pv if acc is None else acc + pv
            o = acc * pltpu.roll(1.0 / acc, hd, 1)
            for g, a in enumerate(heads):
                o_ref[:, a * hd:(a + 1) * hd] = o[g * tq:(g + 1) * tq, :hd].astype(BF16)

    if ctx_tile:
        j = pl.program_id(1)
        pl.when(j == 0)(lambda: run(False))
        pl.when(j > 0)(lambda: run(True))
    else:
        run(True)


def _q_row_block(b, j, n_batch, with_ctx):
    per_batch = SEQ // Q_TILE
    lat0 = CTX_ROWS // Q_TILE
    if with_ctx:
        return jnp.where(j == 0, b, lat0 + b * per_batch + j - 1)
    return lat0 + b * per_batch + j


def _attn_specs(n_batch, with_ctx, q, k, v, o_width, order_bj=True):
    def ix(f):
        return (lambda b, j: f(b, j)) if order_bj else (lambda j, b: f(b, j))

    def qspec(width, col):
        return pl.BlockSpec((Q_TILE, width), ix(lambda b, j: (_q_row_block(b, j, n_batch, with_ctx), col // width)))

    def lat(width, col):
        return pl.BlockSpec((SEQ, width), ix(lambda b, j: (CTX_ROWS // SEQ + b, col // width)))

    def ctx(width, col):
        return pl.BlockSpec((CTX_LEN, width), ix(lambda b, j: (b, col // width)))

    in_specs = [qspec(q[1], q[2]), lat(k[1], k[2]), lat(v[1], v[2]), ctx(k[1], k[2]), ctx(v[1], v[2])]
    args = [q[0], k[0], v[0], k[0], v[0]]
    if with_ctx:
        return in_specs, args, qspec(o_width, 0)
    out_spec = pl.BlockSpec((Q_TILE, o_width), ix(lambda b, j: (b * (SEQ // Q_TILE) + j, 0)))
    return in_specs, args, out_spec


def _attn_out_rows(t, with_ctx):
    return t if with_ctx else t - CTX_ROWS


def _dense_attn_call(n_batch, with_ctx, q, kt, v, *, n_kv, group, dq, scale=None, name):
    t = q[0].shape[0]
    in_specs, args, out_spec = _attn_specs(n_batch, with_ctx, q, v, v, 256)
    kt_rows = kt.shape[0]
    in_specs[1] = pl.BlockSpec((kt_rows, SEQ), lambda b, j: (0, CTX_ROWS // SEQ + b))
    in_specs[3] = pl.BlockSpec((kt_rows, CTX_LEN), lambda b, j: (0, b))
    args[1] = args[3] = kt
    kern = functools.partial(_dense_attn_kernel, n_kv=n_kv, group=group, dq=dq, scale=scale, ctx_tile=with_ctx)
    return pl.pallas_call(
        kern,
        grid=(n_batch, SEQ // Q_TILE + (1 if with_ctx else 0)),
        in_specs=in_specs,
        out_specs=out_spec,
        out_shape=jax.ShapeDtypeStruct((_attn_out_rows(t, with_ctx), 256), BF16),
        compiler_params=_params("arbitrary", "arbitrary"),
        name=name,
    )(*args)


def _window_attn_kernel(sink_ref, q_ref, kl_ref, vl_ref, kc_ref, vc_ref, o_ref, *, ctx_tile):
    j = pl.program_id(1)
    hd = HEAD_DIM
    group = WC_Q_HEADS // WC_KV_HEADS

    def ctx_run():
        def segs(h):
            return [(kc_ref[:, h * hd:(h + 1) * hd], vc_ref[:, h * hd:(h + 1) * hd], None)]
        _gqa_heads(q_ref, o_ref, sink_ref, WC_KV_HEADS, group, hd, hd, None, segs)

    def lat_run():
        i = j - 1 if ctx_tile else j
        start = pl.multiple_of(jnp.clip(i * Q_TILE - WC_WINDOW, 0, SEQ - WC_WIN), WC_WINDOW)
        qpos = i * Q_TILE + lax.broadcasted_iota(jnp.int32, (group * Q_TILE, WC_WIN), 0) % Q_TILE
        kpos = start + lax.broadcasted_iota(jnp.int32, (group * Q_TILE, WC_WIN), 1)
        bias = jnp.where(jnp.abs(qpos - kpos) <= WC_WINDOW, 0.0, NEG_INF).astype(F32)

        def segs(h):
            return [(kl_ref[pl.ds(start, WC_WIN), h * hd:(h + 1) * hd],
                     vl_ref[pl.ds(start, WC_WIN), h * hd:(h + 1) * hd], bias),
                    (kc_ref[:, h * hd:(h + 1) * hd], vc_ref[:, h * hd:(h + 1) * hd], None)]
        _gqa_heads(q_ref, o_ref, sink_ref, WC_KV_HEADS, group, hd, hd, None, segs)

    if ctx_tile:
        pl.when(j == 0)(ctx_run)
        pl.when(j > 0)(lat_run)
    else:
        lat_run()


def _window_attn_call(n_batch, with_ctx, q, k, v, sink):
    t = q[0].shape[0]
    in_specs, args, out_spec = _attn_specs(n_batch, with_ctx, q, k, v, 256)
    return pl.pallas_call(
        functools.partial(_window_attn_kernel, ctx_tile=with_ctx),
        grid=(n_batch, SEQ // Q_TILE + (1 if with_ctx else 0)),
        in_specs=[pl.BlockSpec(memory_space=pltpu.SMEM)] + in_specs,
        out_specs=out_spec,
        out_shape=jax.ShapeDtypeStruct((_attn_out_rows(t, with_ctx), 256), BF16),
        compiler_params=_params("arbitrary", "arbitrary"),
        name="attn_window",
    )(sink, *args)


def _natten_kernel(q_ref, kl_ref, vl_ref, kc_ref, vc_ref, tab_ref, o_ref, bias_ref, *, ctx_tile):
    j = pl.program_id(0)
    hd = HEAD_DIM
    scale = HEAD_DIM ** -0.5

    def build_bias(i):
        ws = jnp.clip(NA_Q_ROWS * i - NA_KH // 2, 0, GRID_ROWS - NA_WIN_ROWS)
        for a in range(NA_Q_ROWS):
            r = NA_Q_ROWS * i + a
            krow0 = jnp.clip(r - NA_KH // 2, 0, GRID_ROWS - NA_KH)
            for pair in range(NA_WIN_ROWS // 2):
                idx = []
                for side in range(2):
                    kr = ws + 2 * pair + side
                    in_rows = (kr >= krow0) & (kr < krow0 + NA_KH)
                    idx.append(jnp.where(in_rows, kr - r + NA_KH - 1, NA_NO_ROW))
                for h in range(NA_HEADS):
                    bias_ref[h, a * GRID_W:(a + 1) * GRID_W, pair * LANES:(pair + 1) * LANES] = (
                        tab_ref[0, h, idx[0]] + tab_ref[1, h, idx[1]])

    def ctx_run():
        for h in range(NA_HEADS):
            sl = slice(h * hd, (h + 1) * hd)
            o = _attend(q_ref[:, sl] * scale, [(kc_ref[:, sl], vc_ref[:, sl], None)])
            o_ref[:, sl] = o.astype(BF16)

    def lat_run():
        i = j - 1 if ctx_tile else j
        ws = jnp.clip(NA_Q_ROWS * i - NA_KH // 2, 0, GRID_ROWS - NA_WIN_ROWS)
        start = pl.multiple_of(ws * GRID_W, GRID_W)
        pl.when(pl.program_id(1) == 0)(lambda: build_bias(i))
        for h in range(NA_HEADS):
            sl = slice(h * hd, (h + 1) * hd)
            segs = [(kl_ref[pl.ds(start, NA_WIN), sl], vl_ref[pl.ds(start, NA_WIN), sl], bias_ref[h]),
                    (kc_ref[:, sl], vc_ref[:, sl], None)]
            o_ref[:, sl] = _attend(q_ref[:, sl] * scale, segs).astype(BF16)

    if ctx_tile:
        pl.when(j == 0)(ctx_run)
        pl.when(j > 0)(lat_run)
    else:
        lat_run()


NA_NO_ROW = 2 * NA_KH - 1


def _natten_call(n_batch, with_ctx, p, bias_blocks):
    t = p.shape[0]
    q, k, v = (p, 256, COL_AQ), (p, 256, COL_AK), (p, 256, COL_AV)
    in_specs, args, out_spec = _attn_specs(n_batch, with_ctx, q, k, v, 256, order_bj=False)
    in_specs.append(pl.BlockSpec(bias_blocks.shape, lambda j, b: (0,) * bias_blocks.ndim))
    return pl.pallas_call(
        functools.partial(_natten_kernel, ctx_tile=with_ctx),
        grid=(SEQ // Q_TILE + (1 if with_ctx else 0), n_batch),
        in_specs=in_specs,
        out_specs=out_spec,
        out_shape=jax.ShapeDtypeStruct((_attn_out_rows(t, with_ctx), 256), BF16),
        scratch_shapes=[pltpu.VMEM((NA_HEADS, Q_TILE, NA_WIN), F32)],
        compiler_params=_params("arbitrary", "arbitrary"),
        name="attn_natten",
    )(*args, bias_blocks)


def _natten_bias_table(rpb):
    n_dr, n_dc = 2 * NA_KH - 1, 2 * NA_KW - 1
    col = np.arange(GRID_W)
    dc = np.clip(col[None, :] - col[:, None] + NA_KW - 1, 0, n_dc - 1)
    onehot = jnp.asarray(dc[None] == np.arange(n_dc)[:, None, None], F32)
    blocks = jnp.einsum('hdc,cqk->hdqk', rpb.astype(F32), onehot, precision=lax.Precision.HIGHEST)
    cstart = np.clip(col - NA_KW // 2, 0, GRID_W - NA_KW)
    col_ok = (col[None, :] >= cstart[:, None]) & (col[None, :] < cstart[:, None] + NA_KW)
    blocks = jnp.where(col_ok[None, None], blocks, NEG_INF)
    blocks = jnp.concatenate([blocks, jnp.full((NA_HEADS, 1, GRID_W, GRID_W), NEG_INF, F32)], axis=1)
    zero = jnp.zeros_like(blocks)
    return jnp.stack([jnp.concatenate([blocks, zero], axis=-1), jnp.concatenate([zero, blocks], axis=-1)])


def _merge_kernel(*refs, with_router):
    if with_router:
        (oa_ref, ob_ref, oc_ref, od_ref, gate_ref, x_ref, mod_ref, g2_ref, wb_ref, wo_ref, rt_ref,
         xo_ref, h2_ref, comb_ref) = refs
    else:
        (oa_ref, ob_ref, oc_ref, od_ref, gate_ref, x_ref, mod_ref, g2_ref, wb_ref, wo_ref,
         xo_ref, h2_ref) = refs
    m = mod_ref[0]
    acc = None
    for n, o_ref in enumerate((oa_ref, ob_ref, oc_ref, od_ref)):
        y = _dot(o_ref[...], wb_ref[n])
        g = _sigmoid(gate_ref[:, n * D_MODEL:(n + 1) * D_MODEL].astype(F32))
        acc = g * y if acc is None else acc + g * y
    mix = _dot(acc.astype(BF16), wo_ref[...])
    x = x_ref[...] + m[2:3] * mix
    xo_ref[...] = x
    h2 = _norm_mod(x, g2_ref[...], m[3:4], m[4:5])
    h2_ref[...] = h2.astype(BF16)
    if with_router:
        hh, hm, _ = _split3(h2)
        rh, rm = rt_ref[0], rt_ref[1]
        logits = _dot(hh, rh) + (_dot(hh, rm) + _dot(hm, rh))
        lane = lax.broadcasted_iota(jnp.int32, logits.shape, 1).astype(F32)
        logits = jnp.where(lane < N_EXPERTS, logits, NEG_INF)
        m1 = jnp.max(logits, axis=-1, keepdims=True)
        i1 = jnp.min(jnp.where(logits == m1, lane, float(LANES)), axis=-1, keepdims=True)
        rest = jnp.where(lane == i1, NEG_INF, logits)
        m2 = jnp.max(rest, axis=-1, keepdims=True)
        i2 = jnp.min(jnp.where(rest == m2, lane, float(LANES)), axis=-1, keepdims=True)
        e = jnp.exp(m2 - m1)
        w1 = 1.0 / (1.0 + e)
        w2 = e / (1.0 + e)
        comb_ref[...] = (jnp.where(lane == 0.0, i1, 0.0) + jnp.where(lane == 1.0, i2, 0.0)
                         + jnp.where(lane == 2.0, w1, 0.0) + jnp.where(lane == 3.0, w2, 0.0))


def _merge_call(outs, gate, x, mod_l, g2, wb, wo, router3, first_tile):
    t = x.shape[0]
    n_rows = t - first_tile * ROW_TILE

    def rows(width, col=0):
        return pl.BlockSpec((ROW_TILE, width), lambda i: (i + first_tile, col // width))

    def orow(width):
        return pl.BlockSpec((ROW_TILE, width), lambda i: (i, 0))

    def full(shape):
        return pl.BlockSpec(shape, lambda i: (0,) * len(shape))

    in_specs = [orow(256)] * 4 + [
        rows(GATE_W),
        rows(D_MODEL),
        pl.BlockSpec((1, ADA_CHUNKS, D_MODEL), lambda i: (_mod_row(i, first_tile), 0, 0)),
        full((1, D_MODEL)), full((N_BRANCH, BRANCH_W, D_MODEL)), full((D_MODEL, D_MODEL)),
    ]
    args = list(outs) + [gate, x, mod_l, g2, wb, wo]
    out_specs = [orow(D_MODEL), orow(D_MODEL)]
    out_shape = [jax.ShapeDtypeStruct((n_rows, D_MODEL), F32), jax.ShapeDtypeStruct((n_rows, D_MODEL), BF16)]
    if router3 is not None:
        in_specs.append(full((2, D_MODEL, LANES)))
        args.append(router3)
        out_specs.append(orow(LANES))
        out_shape.append(jax.ShapeDtypeStruct((n_rows, LANES), F32))
    return pl.pallas_call(
        functools.partial(_merge_kernel, with_router=router3 is not None),
        grid=(n_rows // ROW_TILE,),
        in_specs=in_specs,
        out_specs=out_specs,
        out_shape=out_shape,
        compiler_params=_params("arbitrary"),
        name="merge",
    )(*args)


FFN_ROWS = 1024


def _swiglu_chunk(h, w1, w3, w2):
    a = _dot(h, w1)
    b = _dot(h, w3)
    return _dot((a * _sigmoid(a) * b).astype(BF16), w2)


def _residual_out(x, gate, update, final_norm, final_g):
    x = x + gate * update
    if final_norm:
        ms = jnp.mean(x * x, axis=-1, keepdims=True)
        x = x * lax.rsqrt(ms + NORM_EPS) * final_g
    return x


def _ffn_kernel(h_ref, w1_ref, w3_ref, w2_ref, x_ref, mod_ref, fg_ref, o_ref, acc_ref, *, final_norm):
    f = pl.program_id(1)

    @pl.when(f == 0)
    def _():
        acc_ref[...] = jnp.zeros_like(acc_ref)

    acc_ref[...] += _swiglu_chunk(h_ref[...], w1_ref[...], w3_ref[...], w2_ref[...])

    @pl.when(f == pl.num_programs(1) - 1)
    def _():
        o_ref[...] = _residual_out(x_ref[...], mod_ref[0][5:6], acc_ref[...], final_norm, fg_ref[...])


def _tile_mod_row(i, first_tile, tile_rows):
    n_ctx = CTX_ROWS // tile_rows
    t = i + first_tile
    return jnp.where(t < n_ctx, 8, (t - n_ctx) // (SEQ // tile_rows))


def _ffn_call(h2, w1, w3, w2, x, mod_l, final_g, first_row_tile, final_norm):
    t = x.shape[0]
    rows = lambda width: pl.BlockSpec((FFN_ROWS, width), lambda i, f: (i, 0))
    return pl.pallas_call(
        functools.partial(_ffn_kernel, final_norm=final_norm),
        grid=(t // FFN_ROWS, D_FF // FF_CHUNK),
        in_specs=[
            rows(D_MODEL),
            pl.BlockSpec((D_MODEL, FF_CHUNK), lambda i, f: (0, f)),
            pl.BlockSpec((D_MODEL, FF_CHUNK), lambda i, f: (0, f)),
            pl.BlockSpec((FF_CHUNK, D_MODEL), lambda i, f: (f, 0)),
            rows(D_MODEL),
            pl.BlockSpec((1, ADA_CHUNKS, D_MODEL),
                         lambda i, f: (_tile_mod_row(i, first_row_tile, FFN_ROWS), 0, 0)),
            pl.BlockSpec((1, D_MODEL), lambda i, f: (0, 0)),
        ],
        out_specs=rows(D_MODEL),
        out_shape=jax.ShapeDtypeStruct((t, D_MODEL), F32),
        scratch_shapes=[pltpu.VMEM((FFN_ROWS, D_MODEL), F32)],
        compiler_params=_params("arbitrary", "arbitrary"),
        name="ffn",
    )(h2, w1, w3, w2, x, mod_l, final_g)


MOE_TILE = 512
MOE_SRC = 256
MOE_CMB_WIN = MOE_SRC + 16
MOE_DSP_SMALL = 144
MOE_DSP_LARGE = MOE_SRC + 16
TOP_K = 2


def _moe_route(sel, n_tok):
    n_tiles = TOP_K * n_tok // MOE_TILE + N_EXPERTS
    e = sel[:, 0:TOP_K].astype(jnp.int32)
    flat_e = e.reshape(-1)
    onehot = (flat_e[:, None] == jnp.arange(N_EXPERTS)[None, :]).astype(jnp.int32)
    csum = jnp.cumsum(onehot, axis=0)
    rank = jnp.sum((csum - onehot) * onehot, axis=1)
    counts = csum[-1]
    padded = (counts + MOE_TILE - 1) // MOE_TILE * MOE_TILE
    seg_end = jnp.cumsum(padded)
    seg_start = seg_end - padded
    pos = jnp.sum(onehot * seg_start[None, :], axis=1) + rank
    pos2 = pos.reshape(n_tok, TOP_K)
    tile_ix = jnp.arange(n_tiles)
    tile_expert = jnp.minimum(jnp.sum(tile_ix[:, None] >= (seg_end // MOE_TILE)[None, :], axis=1), N_EXPERTS - 1)
    n_valid = seg_end[-1] // MOE_TILE
    n_src = n_tok // MOE_SRC
    per_chunk = MOE_SRC * TOP_K
    before = jnp.concatenate([jnp.zeros((1, N_EXPERTS), jnp.int32), csum[per_chunk - 1::per_chunk]], axis=0)
    run_start = seg_start[None, :] + before[:-1]
    run_cnt = before[1:] - before[:-1]
    cum = jnp.take(before, tile_expert, axis=1)
    local0 = tile_ix * MOE_TILE - jnp.take(seg_start, tile_expert)
    local1 = jnp.minimum(local0 + MOE_TILE, jnp.take(counts, tile_expert))
    c_lo = jnp.sum(cum[1:] <= local0[None, :], axis=0)
    c_hi = jnp.sum(cum[:-1] < local1[None, :], axis=0) - 1
    win = jnp.minimum(run_start // 16 * 16, n_tiles * MOE_TILE - MOE_CMB_WIN)
    def token_rows(a, dtype):
        rows = jnp.transpose(a.reshape(n_src, MOE_SRC, TOP_K), (0, 2, 1)).astype(dtype)
        return jnp.zeros((n_src, 8, MOE_SRC), dtype).at[:, :TOP_K, :].set(rows)

    i32 = lambda a: a.astype(jnp.int32)
    return {
        "n_tiles": n_tiles, "tile_expert": i32(tile_expert), "n_valid": i32(n_valid).reshape(1),
        "c_lo": i32(c_lo), "c_hi": i32(c_hi), "run_start": i32(run_start.reshape(-1)),
        "run_cnt": i32(run_cnt.reshape(-1)), "win": i32(win.reshape(-1)),
        "run_off": i32((run_start - win).reshape(-1)),
        "pos_rows": token_rows(pos2, jnp.int32), "w_rows": token_rows(sel[:, TOP_K:2 * TOP_K], F32),
        "pos_cols": i32(pos2),
    }


def _dispatch_kernel(clo_ref, chi_ref, te_ref, rs_ref, rc_ref, h_ref, pos_ref, w_ref, o_ref, sw_ref, acc_ref):
    i = pl.program_id(0)
    base = i * MOE_TILE
    expert = te_ref[i]
    acc_ref[...] = jnp.zeros_like(acc_ref)
    sw_ref[...] = jnp.zeros_like(sw_ref)

    def body(c, carry):
        run0 = rs_ref[c * N_EXPERTS + expert]
        lo = jnp.maximum(run0, base) - base
        hi = jnp.minimum(run0 + rc_ref[c * N_EXPERTS + expert], base + MOE_TILE) - base
        pos = pos_ref[c]
        wts = w_ref[c]
        h = h_ref[pl.ds(pl.multiple_of(c * MOE_SRC, MOE_SRC), MOE_SRC), :]

        def window(rows):
            ws = pl.multiple_of(jnp.minimum(lo // 16 * 16, MOE_TILE - rows), 16)
            slot = base + ws + lax.broadcasted_iota(jnp.int32, (rows, MOE_SRC), 0)
            hit0 = pos[0:1, :] == slot
            hit1 = pos[1:2, :] == slot
            acc_ref[pl.ds(ws, rows), :] += _dot(jnp.where(hit0 | hit1, 1.0, 0.0).astype(BF16), h)
            weight = jnp.sum(jnp.where(hit0, wts[0:1, :], 0.0) + jnp.where(hit1, wts[1:2, :], 0.0),
                             axis=1, keepdims=True)
            sw_ref[pl.ds(ws, rows), :] += jnp.broadcast_to(weight, (rows, LANES))

        small = hi - jnp.minimum(lo // 16 * 16, MOE_TILE - MOE_DSP_SMALL) <= MOE_DSP_SMALL
        pl.when((hi > lo) & small)(lambda: window(MOE_DSP_SMALL))
        pl.when((hi > lo) & jnp.logical_not(small))(lambda: window(MOE_DSP_LARGE))
        return carry

    lax.fori_loop(clo_ref[i], chi_ref[i] + 1, body, 0)
    o_ref[...] = acc_ref[...].astype(BF16)


def _dispatch_call(h2, route):
    n_tok = h2.shape[0]
    n_tiles = route["n_tiles"]
    grid_spec = pltpu.PrefetchScalarGridSpec(
        num_scalar_prefetch=5,
        grid=(n_tiles,),
        in_specs=[
            pl.BlockSpec((n_tok, D_MODEL), lambda i, *_: (0, 0), pipeline_mode=pl.Buffered(1)),
            pl.BlockSpec((n_tok // MOE_SRC, 8, MOE_SRC), lambda i, *_: (0, 0, 0), pipeline_mode=pl.Buffered(1)),
            pl.BlockSpec((n_tok // MOE_SRC, 8, MOE_SRC), lambda i, *_: (0, 0, 0), pipeline_mode=pl.Buffered(1)),
        ],
        out_specs=[pl.BlockSpec((MOE_TILE, D_MODEL), lambda i, *_: (i, 0)),
                   pl.BlockSpec((MOE_TILE, LANES), lambda i, *_: (i, 0))],
        scratch_shapes=[pltpu.VMEM((MOE_TILE, D_MODEL), F32)],
    )
    return pl.pallas_call(
        _dispatch_kernel,
        grid_spec=grid_spec,
        out_shape=[jax.ShapeDtypeStruct((n_tiles * MOE_TILE, D_MODEL), BF16),
                   jax.ShapeDtypeStruct((n_tiles * MOE_TILE, LANES), F32)],
        compiler_params=_params("arbitrary"),
        name="moe_dispatch",
    )(route["c_lo"], route["c_hi"], route["tile_expert"], route["run_start"], route["run_cnt"],
      h2, route["pos_rows"], route["w_rows"])


def _expert_kernel(te_ref, nv_ref, x_ref, w1_ref, w3_ref, w2_ref, sw_ref, o_ref, acc_ref):
    i = pl.program_id(0)
    f = pl.program_id(1)
    last = pl.num_programs(1) - 1
    valid = i < nv_ref[0]

    @pl.when(valid & (f == 0))
    def _():
        acc_ref[...] = jnp.zeros_like(acc_ref)

    @pl.when(valid)
    def _():
        acc_ref[...] += _swiglu_chunk(x_ref[...], w1_ref[0, 0], w3_ref[0, 0], w2_ref[0, 0])

    @pl.when(valid & (f == last))
    def _():
        o_ref[...] = (acc_ref[...] * sw_ref[:, 0:1]).astype(BF16)

    @pl.when(jnp.logical_not(valid) & (f == last))
    def _():
        o_ref[...] = jnp.zeros_like(o_ref)


def _expert_call(xs, slot_w, route, w1, w3, w2, layer):
    n_tiles = route["n_tiles"]
    grid_spec = pltpu.PrefetchScalarGridSpec(
        num_scalar_prefetch=2,
        grid=(n_tiles, D_FF // MOE_FF_CHUNK),
        in_specs=[
            pl.BlockSpec((MOE_TILE, D_MODEL), lambda i, f, te, nv: (i, 0)),
            pl.BlockSpec((1, 1, D_MODEL, MOE_FF_CHUNK), lambda i, f, te, nv: (layer, te[i], 0, f)),
            pl.BlockSpec((1, 1, D_MODEL, MOE_FF_CHUNK), lambda i, f, te, nv: (layer, te[i], 0, f)),
            pl.BlockSpec((1, 1, MOE_FF_CHUNK, D_MODEL), lambda i, f, te, nv: (layer, te[i], f, 0)),
            pl.BlockSpec((MOE_TILE, LANES), lambda i, f, te, nv: (i, 0)),
        ],
        out_specs=pl.BlockSpec((MOE_TILE, D_MODEL), lambda i, f, te, nv: (i, 0)),
        scratch_shapes=[pltpu.VMEM((MOE_TILE, D_MODEL), F32)],
    )
    return pl.pallas_call(
        _expert_kernel,
        grid_spec=grid_spec,
        out_shape=jax.ShapeDtypeStruct((n_tiles * MOE_TILE, D_MODEL), BF16),
        compiler_params=_params("arbitrary", "arbitrary"),
        name="moe_experts",
    )(route["tile_expert"], route["n_valid"], xs, w1, w3, w2, slot_w)


def _combine_kernel(win_ref, off_ref, cnt_ref, y_hbm, pos_ref, x_ref, mod_ref, fg_ref, o_ref, buf_ref, sem_ref,
                    *, final_norm):
    c = pl.program_id(0)
    cur = c % 2

    def window_copies(chunk, buf_set):
        return [pltpu.make_async_copy(
            y_hbm.at[pl.ds(pl.multiple_of(win_ref[chunk * N_EXPERTS + e], 16), MOE_CMB_WIN), :],
            buf_ref.at[buf_set, pl.ds(e * MOE_CMB_WIN, MOE_CMB_WIN), :],
            sem_ref.at[buf_set, e]) for e in range(N_EXPERTS)]

    @pl.when(c == 0)
    def _():
        for cp in window_copies(0, 0):
            cp.start()

    @pl.when(c + 1 < pl.num_programs(0))
    def _():
        for cp in window_copies(c + 1, 1 - cur):
            cp.start()

    row = lax.broadcasted_iota(jnp.int32, (1, N_EXPERTS * MOE_CMB_WIN), 1)
    slot = jnp.full((1, N_EXPERTS * MOE_CMB_WIN), -1, jnp.int32)
    for e in range(N_EXPERTS):
        local = row - e * MOE_CMB_WIN
        off = off_ref[c * N_EXPERTS + e]
        inside = (local >= off) & (local < off + cnt_ref[c * N_EXPERTS + e]) & (local < MOE_CMB_WIN)
        slot = jnp.where(inside, win_ref[c * N_EXPERTS + e] + local, slot)
    pos = pos_ref[...]
    hit = (pos[:, 0:1] == slot) | (pos[:, 1:2] == slot)
    onehot = jnp.where(hit, 1.0, 0.0).astype(BF16)
    for cp in window_copies(c, cur):
        cp.wait()
    update = _dot(onehot, buf_ref[cur])
    o_ref[...] = _residual_out(x_ref[...], mod_ref[0][5:6], update, final_norm, fg_ref[...])


def _combine_call(ys, route, x, mod_l, final_g, first_tile, final_norm):
    n_tok = x.shape[0]
    rows = lambda width: pl.BlockSpec((MOE_SRC, width), lambda c, *_: (c, 0))
    grid_spec = pltpu.PrefetchScalarGridSpec(
        num_scalar_prefetch=3,
        grid=(n_tok // MOE_SRC,),
        in_specs=[
            pl.BlockSpec(memory_space=pl.ANY),
            rows(TOP_K),
            rows(D_MODEL),
            pl.BlockSpec((1, ADA_CHUNKS, D_MODEL), lambda c, *_: (_tile_mod_row(c, first_tile, MOE_SRC), 0, 0)),
            pl.BlockSpec((1, D_MODEL), lambda c, *_: (0, 0)),
        ],
        out_specs=rows(D_MODEL),
        scratch_shapes=[pltpu.VMEM((2, N_EXPERTS * MOE_CMB_WIN, D_MODEL), BF16),
                        pltpu.SemaphoreType.DMA((2, N_EXPERTS))],
    )
    return pl.pallas_call(
        functools.partial(_combine_kernel, final_norm=final_norm),
        grid_spec=grid_spec,
        out_shape=jax.ShapeDtypeStruct((n_tok, D_MODEL), F32),
        compiler_params=_params("arbitrary"),
        name="moe_combine",
    )(route["win"], route["run_off"], route["run_cnt"], ys, route["pos_cols"], x, mod_l, final_g)


def _rope_tables():
    t = np.arange(SEQ)

    def angles(rot_dim):
        half = rot_dim // 2
        inv = ROPE_THETA ** (-jnp.arange(0, half, 2, dtype=F32) / half)
        ang = jnp.concatenate([jnp.asarray(t // GRID_W, F32)[:, None] * inv[None, :],
                               jnp.asarray(t % GRID_W, F32)[:, None] * inv[None, :]], axis=-1)
        return jnp.cos(ang), jnp.sin(ang)

    def pad_rows(a, fill):
        return jnp.concatenate([jnp.full((ROW_TILE, a.shape[1]), fill, F32), a], axis=0)

    c, s = angles(HEAD_DIM)
    cos64 = jnp.tile(jnp.concatenate([c, c], axis=-1), (1, 4))
    sin64 = jnp.tile(jnp.concatenate([s, s], axis=-1), (1, 4))
    c, s = angles(MLA_ROPE)
    one = jnp.ones((SEQ, MLA_NOPE), F32)
    zero = jnp.zeros((SEQ, MLA_NOPE), F32)
    tail1 = jnp.ones((SEQ, MLA_PAD - MLA_NOPE - MLA_ROPE), F32)
    tail0 = jnp.zeros((SEQ, MLA_PAD - MLA_NOPE - MLA_ROPE), F32)
    cosm = jnp.tile(jnp.concatenate([one, c, c, tail1], axis=-1), (1, MLA_HEADS))
    sinm = jnp.tile(jnp.concatenate([zero, s, s, tail0], axis=-1), (1, MLA_HEADS))
    blockdiag = jnp.asarray(np.kron(np.eye(4), np.ones((HEAD_DIM, HEAD_DIM))), BF16)
    place = np.zeros((256, MLA_HEADS * MLA_PAD), np.float32)
    for h in range(MLA_HEADS):
        for r in range(MLA_ROPE):
            place[MLA_KV_RANK + r, h * MLA_PAD + MLA_NOPE + r] = 1.0
    return {
        "cos64": pad_rows(cos64, 1.0), "sin64": pad_rows(sin64, 0.0),
        "cosm": pad_rows(cosm, 1.0), "sinm": pad_rows(sinm, 0.0),
        "blockdiag": blockdiag, "pe_place": jnp.asarray(place, BF16),
    }


def _pack_w_in(w):
    n_qkv = COL_DKVA + MLA_KV_RANK + MLA_ROPE
    pad = jnp.zeros(w.shape[:2] + (COL_GATE - n_qkv,), BF16)
    return jnp.concatenate([w[..., :n_qkv].astype(BF16), pad, w[..., n_qkv:].astype(BF16)], axis=-1)


def _pack_mla(wqb, wkvb):
    dqh = MLA_NOPE + MLA_ROPE
    q = wqb.reshape(MLA_Q_RANK, MLA_HEADS, dqh)
    q = jnp.pad(q, ((0, 0), (0, 0), (0, MLA_PAD - dqh))).reshape(MLA_Q_RANK, MLA_HEADS * MLA_PAD)
    kv = wkvb.reshape(MLA_KV_RANK, MLA_HEADS, MLA_NOPE + MLA_V)
    k = jnp.pad(kv[:, :, :MLA_NOPE], ((0, 0), (0, 0), (0, MLA_PAD - MLA_NOPE)))
    k = k.reshape(MLA_KV_RANK, MLA_HEADS * MLA_PAD)
    v = jnp.pad(kv[:, :, MLA_NOPE:], ((0, 0), (0, 0), (0, MLA_PAD - MLA_V))).reshape(MLA_KV_RANK, MLA_HEADS * MLA_PAD)
    return q.astype(BF16), k.astype(BF16), v.astype(BF16)


def _split2_host(w):
    hi = w.astype(BF16)
    return jnp.stack([hi, (w - hi.astype(F32)).astype(BF16)])


def kernel(x, c, ctx, c_ctx, norm1_g, norm2_g, w_ada, b_ada, w_in, na_rpb, gb_qnorm, gb_knorm, wc_sink,
           mla_qnorm, mla_kvnorm, mla_wqb, mla_wkvb, w_branch, w_out, ffn_w1, ffn_w3, ffn_w2,
           moe_router, moe_w1, moe_w3, moe_w2, final_g):
    n_batch = x.shape[0]
    assert x.shape[1:] == (SEQ, D_MODEL) and ctx.shape[1:] == (CTX_LEN, D_MODEL)
    assert n_batch * CTX_LEN <= CTX_ROWS and n_batch <= 8

    ctx_rows = ctx.reshape(n_batch * CTX_LEN, D_MODEL)
    if ctx_rows.shape[0] < CTX_ROWS:
        ctx_rows = jnp.pad(ctx_rows, ((0, CTX_ROWS - ctx_rows.shape[0]), (0, 0)))
    xt = jnp.concatenate([ctx_rows, x.reshape(n_batch * SEQ, D_MODEL)], axis=0)

    cvec = jnp.zeros((16, D_MODEL), F32).at[:n_batch].set(c).at[8].set(c_ctx)
    mod = _mod_call(cvec, w_ada, b_ada).reshape(DEPTH, 16, ADA_CHUNKS, D_MODEL)
    tabs = _rope_tables()
    w_in_p = _pack_w_in(w_in)
    moe_w = tuple(w.astype(BF16) for w in (moe_w1, moe_w3, moe_w2))
    mla_scale = (MLA_NOPE + MLA_ROPE) ** -0.5
    lat_tile0 = CTX_ROWS // ROW_TILE

    for l in range(DEPTH):
        with_ctx = l < DEPTH - 1
        wq, wk, wv = _pack_mla(mla_wqb[l], mla_wkvb[l])
        lw = {
            "gq": jnp.tile(gb_qnorm[l], 4)[None, :], "gk": jnp.tile(gb_knorm[l], 2)[None, :],
            "qn": mla_qnorm[l][None, :], "kvn": mla_kvnorm[l][None, :], "wqb": wq, "wk": wk, "wv": wv,
        }
        p, gate = _inproj_call(xt, mod[l], norm1_g[l][None, :], w_in_p, l)
        qb, kb, qc, kc, qd, kd, vd, vb = _prep_call(p, tabs, lw)

        oa = _natten_call(n_batch, with_ctx, p, _natten_bias_table(na_rpb[l]))
        ob = _dense_attn_call(n_batch, with_ctx, (qb, 256, 0), kb, (vb, GB_KV_HEADS * LANES, 0),
                              n_kv=GB_KV_HEADS, group=GB_Q_HEADS // GB_KV_HEADS, dq=HEAD_DIM, name="attn_global")
        oc = _window_attn_call(n_batch, with_ctx, (qc, 256, 0), (kc, 128, 0), (p, 128, COL_CV), wc_sink[l])
        od = _dense_attn_call(n_batch, with_ctx, (qd, MLA_HEADS * MLA_PAD, 0), kd, (vd, MLA_HEADS * MLA_PAD, 0),
                              n_kv=MLA_HEADS, group=1, dq=MLA_PAD, scale=mla_scale, name="attn_mla")

        is_moe = l % 2 == 1
        router3 = None
        if is_moe:
            router3 = _split2_host(jnp.pad(moe_router[l // 2], ((0, 0), (0, LANES - N_EXPERTS))))
        first_tile = 0 if with_ctx else lat_tile0
        res = _merge_call((oa, ob, oc, od), gate, xt, mod[l], norm2_g[l][None, :],
                          w_branch[l].astype(BF16), w_out[l].astype(BF16), router3, first_tile)
        x_mid, h2 = res[0], res[1]
        final_norm = l == DEPTH - 1
        if is_moe:
            route = _moe_route(res[2], h2.shape[0])
            xs, slot_w = _dispatch_call(h2, route)
            ys = _expert_call(xs, slot_w, route, *moe_w, l // 2)
            xt = _combine_call(ys, route, x_mid, mod[l], final_g[None, :],
                               0 if with_ctx else CTX_ROWS // MOE_SRC, final_norm)
        else:
            w1, w3, w2 = (w[l // 2].astype(BF16) for w in (ffn_w1, ffn_w3, ffn_w2))
            xt = _ffn_call(h2, w1, w3, w2, x_mid, mod[l], final_g[None, :],
                           0 if with_ctx else CTX_ROWS // FFN_ROWS, final_norm)

    return xt.reshape(n_batch, SEQ, D_MODEL)
```

```python
import functools

import numpy as np
import jax
import jax.numpy as jnp
from jax import lax
from jax.experimental import pallas as pl
from jax.experimental.pallas import tpu as pltpu

F32 = jnp.float32
BF16 = jnp.bfloat16

D_MODEL = 1024
SEQ = 2048
DEPTH = 4
CTX_LEN = 256
GRID_W = 64
GRID_ROWS = SEQ // GRID_W
HEAD_DIM = 64
ROPE_THETA = 10000.0
NORM_EPS = 1e-6
NEG_INF = -1e30

NA_HEADS = 4
NA_KH = 8
NA_KW = 16
GB_Q_HEADS = 4
GB_KV_HEADS = 2
WC_Q_HEADS = 4
WC_KV_HEADS = 2
WC_WINDOW = 128
MLA_HEADS = 4
MLA_Q_RANK = 256
MLA_KV_RANK = 128
MLA_NOPE = 64
MLA_ROPE = 32
MLA_V = 64
MLA_PAD = 128
N_BRANCH = 4
BRANCH_W = 256
D_FF = 3584
N_EXPERTS = 8
ADA_CHUNKS = 6

VMEM_LIMIT_BYTES = 56 * 1024 * 1024
LANES = 128

CTX_ROWS = 2048
ROW_TILE = 512
Q_TILE = 256
NA_Q_ROWS = Q_TILE // GRID_W
NA_WIN_ROWS = 12
NA_WIN = NA_WIN_ROWS * GRID_W
WC_WIN = Q_TILE + 2 * WC_WINDOW
ATTN_KEY_CHUNK = 512

COL_AQ, COL_AK, COL_AV = 0, 256, 512
COL_BQ, COL_BK, COL_BV = 768, 1024, 1152
COL_CQ, COL_CK, COL_CV = 1280, 1536, 1664
COL_DQA, COL_DKVA, COL_GATE = 1792, 2048, 2304
GATE_W = N_BRANCH * D_MODEL
P_WIDTH = COL_GATE + GATE_W
QKV_CHUNK = COL_GATE // 2
FF_CHUNK = 512
MOE_FF_CHUNK = 1792


def _params(*sem):
    return pltpu.CompilerParams(dimension_semantics=sem, vmem_limit_bytes=VMEM_LIMIT_BYTES)


def _dot(a, b):
    return jnp.dot(a, b, preferred_element_type=F32)


def _dot_nt(a, b):
    return lax.dot_general(a, b, (((1,), (1,)), ((), ())), preferred_element_type=F32)


def _split3(x):
    hi = x.astype(BF16)
    r1 = x - hi.astype(F32)
    mid = r1.astype(BF16)
    lo = (r1 - mid.astype(F32)).astype(BF16)
    return hi, mid, lo


def _sigmoid(x):
    return 0.5 * jnp.tanh(0.5 * x) + 0.5


ADA_TN = 1536


def _mod_kernel(c_ref, w_ref, b_ref, o_ref):
    c = c_ref[...]
    sc = (c * _sigmoid(c)).astype(BF16)
    o_ref[0] = _dot(sc, w_ref[0].astype(BF16)) + b_ref[0]


def _mod_call(cvec, w_ada, b_ada):
    n = ADA_CHUNKS * D_MODEL
    return pl.pallas_call(
        _mod_kernel,
        grid=(DEPTH, n // ADA_TN),
        in_specs=[
            pl.BlockSpec((16, D_MODEL), lambda l, j: (0, 0)),
            pl.BlockSpec((1, D_MODEL, ADA_TN), lambda l, j: (l, 0, j)),
            pl.BlockSpec((1, 1, ADA_TN), lambda l, j: (l, 0, j)),
        ],
        out_specs=pl.BlockSpec((1, 16, ADA_TN), lambda l, j: (l, 0, j)),
        out_shape=jax.ShapeDtypeStruct((DEPTH, 16, n), F32),
        compiler_params=_params("arbitrary", "arbitrary"),
        name="adaln_mod",
    )(cvec, w_ada, b_ada.reshape(DEPTH, 1, n))


def _mod_row(i, first_tile):
    n_ctx = CTX_ROWS // ROW_TILE
    t = i + first_tile
    return jnp.where(t < n_ctx, 8, (t - n_ctx) // (SEQ // ROW_TILE))


def _norm_mod(x, g, shift, scale):
    ms = jnp.mean(x * x, axis=-1, keepdims=True)
    y = x * lax.rsqrt(ms + NORM_EPS) * g
    return y * (1.0 + scale) + shift


def _rope(x, cos, sin, half, first_mask):
    w = x.shape[-1]
    fwd = pltpu.roll(x, w - half, 1)
    bwd = pltpu.roll(x, half, 1)
    rot = jnp.where(first_mask, -fwd, bwd)
    return x * cos + rot * sin


def _head_rms(x, gain, blockdiag):
    hi, mid, lo = _split3(x * x)
    ss = _dot(hi, blockdiag) + _dot(mid, blockdiag) + _dot(lo, blockdiag)
    return x * lax.rsqrt(ss * (1.0 / HEAD_DIM) + NORM_EPS) * gain


def _inproj_kernel(x_ref, mod_ref, g_ref, w_ref,
                   cos64_ref, sin64_ref, cosm_ref, sinm_ref,
                   gq_ref, gk_ref, bd_ref, qn_ref, kvn_ref, wqb_ref, wk_ref, wv_ref, pe_ref,
                   pa_ref, cv_ref, gate_ref, qb_ref, kb_ref, qc_ref, kc_ref, qd_ref, kd_ref, vd_ref, vb_ref):
    m = mod_ref[0]
    hb = _norm_mod(x_ref[...], g_ref[...], m[0:1], m[1:2]).astype(BF16)

    def proj(c0, c1):
        return _dot(hb, w_ref[0, :, c0:c1])

    for c0 in range(0, GATE_W, D_MODEL):
        gate_ref[:, c0:c0 + D_MODEL] = proj(COL_GATE + c0, COL_GATE + c0 + D_MODEL).astype(BF16)
    pa_ref[...] = proj(COL_AQ, COL_BQ).astype(BF16)

    scale = HEAD_DIM ** -0.5
    half = HEAD_DIM // 2
    cos64, sin64 = cos64_ref[...], sin64_ref[...]
    lane = lax.broadcasted_iota(jnp.int32, (1, 256), 1)
    first64 = (lane % HEAD_DIM) < half
    bd = bd_ref[...]
    bc = proj(COL_BQ, COL_DQA)
    bq = _head_rms(bc[:, 0:256], gq_ref[...], bd)
    qb_ref[...] = (_rope(bq, cos64, sin64, half, first64) * scale).astype(BF16)
    bk = _head_rms(bc[:, 256:384], gk_ref[...], bd[:128, :128])
    kb_ref[...] = _rope(bk, cos64[:, :128], sin64[:, :128], half, first64[:, :128]).T.astype(BF16)
    qc_ref[...] = (_rope(bc[:, 512:768], cos64, sin64, half, first64) * scale).astype(BF16)
    kc_ref[...] = _rope(bc[:, 768:896], cos64[:, :128], sin64[:, :128], half, first64[:, :128]).astype(BF16)
    cv_ref[...] = bc[:, 896:1024].astype(BF16)

    cosm, sinm = cosm_ref[...], sinm_ref[...]
    lane_m = lax.broadcasted_iota(jnp.int32, (1, MLA_HEADS * MLA_PAD), 1) % MLA_PAD
    first_m = lane_m < (MLA_NOPE + MLA_ROPE // 2)
    d = proj(COL_DQA, COL_GATE)
    dqa = d[:, 0:MLA_Q_RANK]
    qn = dqa * lax.rsqrt(jnp.mean(dqa * dqa, axis=-1, keepdims=True) + NORM_EPS) * qn_ref[...]
    dq = _dot(qn.astype(BF16), wqb_ref[...])
    qd_ref[...] = _rope(dq, cosm, sinm, MLA_ROPE // 2, first_m).astype(BF16)

    dkva = d[:, MLA_Q_RANK:]
    kvc = dkva[:, :MLA_KV_RANK]
    kvn = (kvc * lax.rsqrt(jnp.mean(kvc * kvc, axis=-1, keepdims=True) + NORM_EPS) * kvn_ref[...]).astype(BF16)
    dk = _dot(kvn, wk_ref[...]) + _dot(dkva.astype(BF16), pe_ref[...])
    kd_ref[...] = _rope(dk, cosm, sinm, MLA_ROPE // 2, first_m).T.astype(BF16)
    ones_half = lane_m >= HEAD_DIM
    vd_ref[...] = jnp.where(ones_half, 1.0, _dot(kvn, wv_ref[...])).astype(BF16)
    bv = bc[:, 384:512]
    vb_ref[:, :LANES] = jnp.where(ones_half[:, :LANES], 1.0, bv).astype(BF16)
    vb_ref[:, LANES:] = jnp.where(ones_half[:, :LANES], 1.0, pltpu.roll(bv, HEAD_DIM, 1)).astype(BF16)


def _rope_row_block(i):
    n_ctx = CTX_ROWS // ROW_TILE
    return jnp.where(i < n_ctx, 0, 1 + (i - n_ctx) % (SEQ // ROW_TILE))


def _inproj_call(x, mod_l, g, w_in_p, layer, tabs, lw):
    t = x.shape[0]
    mw = MLA_HEADS * MLA_PAD

    def full(shape):
        return pl.BlockSpec(shape, lambda i: (0,) * len(shape))

    def rows(width):
        return pl.BlockSpec((ROW_TILE, width), lambda i: (i, 0))

    def tab(width):
        return pl.BlockSpec((ROW_TILE, width), lambda i: (_rope_row_block(i), 0))

    def cols(height):
        return pl.BlockSpec((height, ROW_TILE), lambda i: (0, i))

    outs = [(COL_BQ, False), (128, False), (GATE_W, False),
            (256, False), (128, True), (256, False), (128, False), (mw, False), (mw, True), (mw, False),
            (GB_KV_HEADS * LANES, False)]
    return pl.pallas_call(
        _inproj_kernel,
        grid=(t // ROW_TILE,),
        in_specs=[
            rows(D_MODEL),
            pl.BlockSpec((1, ADA_CHUNKS, D_MODEL), lambda i: (_mod_row(i, 0), 0, 0)),
            full((1, D_MODEL)),
            pl.BlockSpec((1, D_MODEL, P_WIDTH), lambda i: (layer, 0, 0), pipeline_mode=pl.Buffered(1)),
            tab(256), tab(256), tab(mw), tab(mw),
            full((1, 256)), full((1, 128)), full((256, 256)), full((1, MLA_Q_RANK)), full((1, MLA_KV_RANK)),
            full((MLA_Q_RANK, mw)), full((MLA_KV_RANK, mw)), full((MLA_KV_RANK, mw)), full((256, mw)),
        ],
        out_specs=[cols(w) if tr else rows(w) for w, tr in outs],
        out_shape=[jax.ShapeDtypeStruct((w, t) if tr else (t, w), BF16) for w, tr in outs],
        compiler_params=_params("arbitrary"),
        name="inproj",
    )(x, mod_l, g, w_in_p, tabs["cos64"], tabs["sin64"], tabs["cosm"], tabs["sinm"],
      lw["gq"], lw["gk"], tabs["blockdiag"], lw["qn"], lw["kvn"], lw["wqb"], lw["wk"], lw["wv"], tabs["pe_place"])


def _attend(q, segs, sink=None, scale=None):
    chunks = segs
    scores = []
    m = None
    for k, _, bias in chunks:
        s = _dot_nt(q, k)
        if scale is not None:
            s = s * scale
        if bias is not None:
            s = s + bias
        scores.append(s)
        ms = jnp.max(s, axis=-1, keepdims=True)
        m = ms if m is None else jnp.maximum(m, ms)
    if sink is not None:
        m = jnp.maximum(m, sink)
    denom = None
    out = None
    for s, (_, v, _) in zip(scores, chunks):
        p = jnp.exp(s - m)
        ps = jnp.sum(p, axis=-1, keepdims=True)
        pv = _dot(p.astype(BF16), v)
        denom = ps if denom is None else denom + ps
        out = pv if out is None else out + pv
    if sink is not None:
        denom = denom + jnp.exp(sink - m)
    return out / denom


def _gqa_heads(q_ref, o_ref, sink_ref, n_kv, group, dq, dv, scale, seg_fn):
    tq = q_ref.shape[0]
    for h in range(n_kv):
        heads = [h * group + g for g in range(group)]
        q = jnp.concatenate([q_ref[:, a * dq:(a + 1) * dq] for a in heads], axis=0) if group > 1 \
            else q_ref[:, h * dq:(h + 1) * dq]
        sink = None
        if sink_ref is not None:
            sink = jnp.concatenate([jnp.full((tq, 1), sink_ref[a], F32) for a in heads], axis=0)
        o = _attend(q, seg_fn(h), sink=sink, scale=scale)
        for g, a in enumerate(heads):
            o_ref[:, a * dv:(a + 1) * dv] = o[g * tq:(g + 1) * tq].astype(BF16)


def _dense_attn_kernel(q_ref, ktl_ref, vl_ref, ktc_ref, vc_ref, o_ref, *, n_kv, group, dq, scale, ctx_tile):
    tq = q_ref.shape[0]
    hd = HEAD_DIM

    def run(with_latent):
        for h in range(n_kv):
            heads = [h * group + g for g in range(group)]
            q = jnp.concatenate([q_ref[:, a * dq:(a + 1) * dq] for a in heads], axis=0) if group > 1 \
                else q_ref[:, h * dq:(h + 1) * dq]
            segs = [(ktc_ref[h * dq:(h + 1) * dq, :], vc_ref[:, h * LANES:(h + 1) * LANES])]
            if with_latent:
                segs = [(ktl_ref[h * dq:(h + 1) * dq, c0:c0 + ATTN_KEY_CHUNK],
                         vl_ref[c0:c0 + ATTN_KEY_CHUNK, h * LANES:(h + 1) * LANES])
                        for c0 in range(0, SEQ, ATTN_KEY_CHUNK)] + segs
            scores = []
            m = None
            for kt, _ in segs:
                s = _dot(q, kt)
                if scale is not None:
                    s = s * scale
                scores.append(s)
                ms = jnp.max(s, axis=-1, keepdims=True)
                m = ms if m is None else jnp.maximum(m, ms)
            acc = None
            for s, (_, v) in zip(scores, segs):
                pv = _dot(jnp.exp(s - m).astype(BF16), v)
                acc = pv if acc is None else acc + pv
            o = acc * pltpu.roll(1.0 / acc, hd, 1)
            for g, a in enumerate(heads):
                o_ref[:, a * hd:(a + 1) * hd] = o[g * tq:(g + 1) * tq, :hd].astype(BF16)

    if ctx_tile:
        j = pl.program_id(1)
        pl.when(j == 0)(lambda: run(False))
        pl.when(j > 0)(lambda: run(True))
    else:
        run(True)


def _q_row_block(b, j, n_batch, with_ctx):
    per_batch = SEQ // Q_TILE
    lat0 = CTX_ROWS // Q_TILE
    if with_ctx:
        return jnp.where(j == 0, b, lat0 + b * per_batch + j - 1)
    return lat0 + b * per_batch + j


def _attn_specs(n_batch, with_ctx, q, k, v, o_width, order_bj=True):
    def ix(f):
        return (lambda b, j: f(b, j)) if order_bj else (lambda j, b: f(b, j))

    def qspec(width, col):
        return pl.BlockSpec((Q_TILE, width), ix(lambda b, j: (_q_row_block(b, j, n_batch, with_ctx), col // width)))

    def lat(width, col):
        return pl.BlockSpec((SEQ, width), ix(lambda b, j: (CTX_ROWS // SEQ + b, col // width)))

    def ctx(width, col):
        return pl.BlockSpec((CTX_LEN, width), ix(lambda b, j: (b, col // width)))

    in_specs = [qspec(q[1], q[2]), lat(k[1], k[2]), lat(v[1], v[2]), ctx(k[1], k[2]), ctx(v[1], v[2])]
    args = [q[0], k[0], v[0], k[0], v[0]]
    if with_ctx:
        return in_specs, args, qspec(o_width, 0)
    out_spec = pl.BlockSpec((Q_TILE, o_width), ix(lambda b, j: (b * (SEQ // Q_TILE) + j, 0)))
    return in_specs, args, out_spec


def _attn_out_rows(t, with_ctx):
    return t if with_ctx else t - CTX_ROWS


def _dense_attn_call(n_batch, with_ctx, q, kt, v, *, n_kv, group, dq, scale=None, name):
    t = q[0].shape[0]
    in_specs, args, out_spec = _attn_specs(n_batch, with_ctx, q, v, v, 256)
    kt_rows = kt.shape[0]
    in_specs[1] = pl.BlockSpec((kt_rows, SEQ), lambda b, j: (0, CTX_ROWS // SEQ + b))
    in_specs[3] = pl.BlockSpec((kt_rows, CTX_LEN), lambda b, j: (0, b))
    args[1] = args[3] = kt
    kern = functools.partial(_dense_attn_kernel, n_kv=n_kv, group=group, dq=dq, scale=scale, ctx_tile=with_ctx)
    return pl.pallas_call(
        kern,
        grid=(n_batch, SEQ // Q_TILE + (1 if with_ctx else 0)),
        in_specs=in_specs,
        out_specs=out_spec,
        out_shape=jax.ShapeDtypeStruct((_attn_out_rows(t, with_ctx), 256), BF16),
        compiler_params=_params("arbitrary", "arbitrary"),
        name=name,
    )(*args)


def _window_attn_kernel(sink_ref, q_ref, kl_ref, vl_ref, kc_ref, vc_ref, o_ref, *, ctx_tile):
    j = pl.program_id(1)
    hd = HEAD_DIM
    group = WC_Q_HEADS // WC_KV_HEADS

    def ctx_run():
        def segs(h):
            return [(kc_ref[:, h * hd:(h + 1) * hd], vc_ref[:, h * hd:(h + 1) * hd], None)]
        _gqa_heads(q_ref, o_ref, sink_ref, WC_KV_HEADS, group, hd, hd, None, segs)

    def lat_run():
        i = j - 1 if ctx_tile else j
        start = pl.multiple_of(jnp.clip(i * Q_TILE - WC_WINDOW, 0, SEQ - WC_WIN), WC_WINDOW)
        qpos = i * Q_TILE + lax.broadcasted_iota(jnp.int32, (group * Q_TILE, WC_WIN), 0) % Q_TILE
        kpos = start + lax.broadcasted_iota(jnp.int32, (group * Q_TILE, WC_WIN), 1)
        bias = jnp.where(jnp.abs(qpos - kpos) <= WC_WINDOW, 0.0, NEG_INF).astype(F32)

        def segs(h):
            return [(kl_ref[pl.ds(start, WC_WIN), h * hd:(h + 1) * hd],
                     vl_ref[pl.ds(start, WC_WIN), h * hd:(h + 1) * hd], bias),
                    (kc_ref[:, h * hd:(h + 1) * hd], vc_ref[:, h * hd:(h + 1) * hd], None)]
        _gqa_heads(q_ref, o_ref, sink_ref, WC_KV_HEADS, group, hd, hd, None, segs)

    if ctx_tile:
        pl.when(j == 0)(ctx_run)
        pl.when(j > 0)(lat_run)
    else:
        lat_run()


def _window_attn_call(n_batch, with_ctx, q, k, v, sink):
    t = q[0].shape[0]
    in_specs, args, out_spec = _attn_specs(n_batch, with_ctx, q, k, v, 256)
    return pl.pallas_call(
        functools.partial(_window_attn_kernel, ctx_tile=with_ctx),
        grid=(n_batch, SEQ // Q_TILE + (1 if with_ctx else 0)),
        in_specs=[pl.BlockSpec(memory_space=pltpu.SMEM)] + in_specs,
        out_specs=out_spec,
        out_shape=jax.ShapeDtypeStruct((_attn_out_rows(t, with_ctx), 256), BF16),
        compiler_params=_params("arbitrary", "arbitrary"),
        name="attn_window",
    )(sink, *args)


def _natten_kernel(q_ref, kl_ref, vl_ref, kc_ref, vc_ref, tab_ref, o_ref, bias_ref, *, ctx_tile):
    j = pl.program_id(0)
    hd = HEAD_DIM
    scale = HEAD_DIM ** -0.5

    def build_bias(i):
        ws = jnp.clip(NA_Q_ROWS * i - NA_KH // 2, 0, GRID_ROWS - NA_WIN_ROWS)
        for a in range(NA_Q_ROWS):
            r = NA_Q_ROWS * i + a
            krow0 = jnp.clip(r - NA_KH // 2, 0, GRID_ROWS - NA_KH)
            for pair in range(NA_WIN_ROWS // 2):
                idx = []
                for side in range(2):
                    kr = ws + 2 * pair + side
                    in_rows = (kr >= krow0) & (kr < krow0 + NA_KH)
                    idx.append(jnp.where(in_rows, kr - r + NA_KH - 1, NA_NO_ROW))
                for h in range(NA_HEADS):
                    bias_ref[h, a * GRID_W:(a + 1) * GRID_W, pair * LANES:(pair + 1) * LANES] = (
                        tab_ref[0, h, idx[0]] + tab_ref[1, h, idx[1]])

    def ctx_run():
        for h in range(NA_HEADS):
            sl = slice(h * hd, (h + 1) * hd)
            o = _attend(q_ref[:, sl] * scale, [(kc_ref[:, sl], vc_ref[:, sl], None)])
            o_ref[:, sl] = o.astype(BF16)

    def lat_run():
        i = j - 1 if ctx_tile else j
        ws = jnp.clip(NA_Q_ROWS * i - NA_KH // 2, 0, GRID_ROWS - NA_WIN_ROWS)
        start = pl.multiple_of(ws * GRID_W, GRID_W)
        pl.when(pl.program_id(1) == 0)(lambda: build_bias(i))
        for h in range(NA_HEADS):
            sl = slice(h * hd, (h + 1) * hd)
            segs = [(kl_ref[pl.ds(start, NA_WIN), sl], vl_ref[pl.ds(start, NA_WIN), sl], bias_ref[h]),
                    (kc_ref[:, sl], vc_ref[:, sl], None)]
            o_ref[:, sl] = _attend(q_ref[:, sl] * scale, segs).astype(BF16)

    if ctx_tile:
        pl.when(j == 0)(ctx_run)
        pl.when(j > 0)(lat_run)
    else:
        lat_run()


NA_NO_ROW = 2 * NA_KH - 1


def _natten_call(n_batch, with_ctx, p, bias_blocks):
    t = p.shape[0]
    q, k, v = (p, 256, COL_AQ), (p, 256, COL_AK), (p, 256, COL_AV)
    in_specs, args, out_spec = _attn_specs(n_batch, with_ctx, q, k, v, 256, order_bj=False)
    in_specs.append(pl.BlockSpec(bias_blocks.shape, lambda j, b: (0,) * bias_blocks.ndim))
    return pl.pallas_call(
        functools.partial(_natten_kernel, ctx_tile=with_ctx),
        grid=(SEQ // Q_TILE + (1 if with_ctx else 0), n_batch),
        in_specs=in_specs,
        out_specs=out_spec,
        out_shape=jax.ShapeDtypeStruct((_attn_out_rows(t, with_ctx), 256), BF16),
        scratch_shapes=[pltpu.VMEM((NA_HEADS, Q_TILE, NA_WIN), F32)],
        compiler_params=_params("arbitrary", "arbitrary"),
        name="attn_natten",
    )(*args, bias_blocks)


def _natten_bias_table(rpb):
    n_dr, n_dc = 2 * NA_KH - 1, 2 * NA_KW - 1
    col = np.arange(GRID_W)
    dc = np.clip(col[None, :] - col[:, None] + NA_KW - 1, 0, n_dc - 1)
    onehot = jnp.asarray(dc[None] == np.arange(n_dc)[:, None, None], F32)
    blocks = jnp.einsum('hdc,cqk->hdqk', rpb.astype(F32), onehot, precision=lax.Precision.HIGHEST)
    cstart = np.clip(col - NA_KW // 2, 0, GRID_W - NA_KW)
    col_ok = (col[None, :] >= cstart[:, None]) & (col[None, :] < cstart[:, None] + NA_KW)
    blocks = jnp.where(col_ok[None, None], blocks, NEG_INF)
    blocks = jnp.concatenate([blocks, jnp.full((NA_HEADS, 1, GRID_W, GRID_W), NEG_INF, F32)], axis=1)
    zero = jnp.zeros_like(blocks)
    return jnp.stack([jnp.concatenate([blocks, zero], axis=-1), jnp.concatenate([zero, blocks], axis=-1)])


def _merge_kernel(*refs, with_router):
    if with_router:
        (oa_ref, ob_ref, oc_ref, od_ref, gate_ref, x_ref, mod_ref, g2_ref, wb_ref, wo_ref, rt_ref,
         xo_ref, h2_ref, comb_ref) = refs
    else:
        (oa_ref, ob_ref, oc_ref, od_ref, gate_ref, x_ref, mod_ref, g2_ref, wb_ref, wo_ref,
         xo_ref, h2_ref) = refs
    m = mod_ref[0]
    acc = None
    for n, o_ref in enumerate((oa_ref, ob_ref, oc_ref, od_ref)):
        y = _dot(o_ref[...], wb_ref[n])
        g = _sigmoid(gate_ref[:, n * D_MODEL:(n + 1) * D_MODEL].astype(F32))
        acc = g * y if acc is None else acc + g * y
    mix = _dot(acc.astype(BF16), wo_ref[...])
    x = x_ref[...] + m[2:3] * mix
    xo_ref[...] = x
    h2 = _norm_mod(x, g2_ref[...], m[3:4], m[4:5])
    h2_ref[...] = h2.astype(BF16)
    if with_router:
        hh, hm, _ = _split3(h2)
        rh, rm = rt_ref[0], rt_ref[1]
        logits = _dot(hh, rh) + (_dot(hh, rm) + _dot(hm, rh))
        lane = lax.broadcasted_iota(jnp.int32, logits.shape, 1).astype(F32)
        logits = jnp.where(lane < N_EXPERTS, logits, NEG_INF)
        m1 = jnp.max(logits, axis=-1, keepdims=True)
        i1 = jnp.min(jnp.where(logits == m1, lane, float(LANES)), axis=-1, keepdims=True)
        rest = jnp.where(lane == i1, NEG_INF, logits)
        m2 = jnp.max(rest, axis=-1, keepdims=True)
        i2 = jnp.min(jnp.where(rest == m2, lane, float(LANES)), axis=-1, keepdims=True)
        e = jnp.exp(m2 - m1)
        w1 = 1.0 / (1.0 + e)
        w2 = e / (1.0 + e)
        comb_ref[...] = (jnp.where(lane == 0.0, i1, 0.0) + jnp.where(lane == 1.0, i2, 0.0)
                         + jnp.where(lane == 2.0, w1, 0.0) + jnp.where(lane == 3.0, w2, 0.0))


def _merge_call(outs, gate, x, mod_l, g2, wb, wo, router3, first_tile):
    t = x.shape[0]
    n_rows = t - first_tile * ROW_TILE

    def rows(width, col=0):
        return pl.BlockSpec((ROW_TILE, width), lambda i: (i + first_tile, col // width))

    def orow(width):
        return pl.BlockSpec((ROW_TILE, width), lambda i: (i, 0))

    def full(shape):
        return pl.BlockSpec(shape, lambda i: (0,) * len(shape))

    in_specs = [orow(256)] * 4 + [
        rows(GATE_W),
        rows(D_MODEL),
        pl.BlockSpec((1, ADA_CHUNKS, D_MODEL), lambda i: (_mod_row(i, first_tile), 0, 0)),
        full((1, D_MODEL)), full((N_BRANCH, BRANCH_W, D_MODEL)), full((D_MODEL, D_MODEL)),
    ]
    args = list(outs) + [gate, x, mod_l, g2, wb, wo]
    out_specs = [orow(D_MODEL), orow(D_MODEL)]
    out_shape = [jax.ShapeDtypeStruct((n_rows, D_MODEL), F32), jax.ShapeDtypeStruct((n_rows, D_MODEL), BF16)]
    if router3 is not None:
        in_specs.append(full((2, D_MODEL, LANES)))
        args.append(router3)
        out_specs.append(orow(LANES))
        out_shape.append(jax.ShapeDtypeStruct((n_rows, LANES), F32))
    return pl.pallas_call(
        functools.partial(_merge_kernel, with_router=router3 is not None),
        grid=(n_rows // ROW_TILE,),
        in_specs=in_specs,
        out_specs=out_specs,
        out_shape=out_shape,
        compiler_params=_params("arbitrary"),
        name="merge",
    )(*args)


FFN_ROWS = 1024


def _swiglu_chunk(h, w1, w3, w2):
    a = _dot(h, w1)
    b = _dot(h, w3)
    return _dot((a * _sigmoid(a) * b).astype(BF16), w2)


def _residual_out(x, gate, update, final_norm, final_g):
    x = x + gate * update
    if final_norm:
        ms = jnp.mean(x * x, axis=-1, keepdims=True)
        x = x * lax.rsqrt(ms + NORM_EPS) * final_g
    return x


def _ffn_kernel(h_ref, w1_ref, w3_ref, w2_ref, x_ref, mod_ref, fg_ref, o_ref, acc_ref, *, final_norm):
    f = pl.program_id(1)

    @pl.when(f == 0)
    def _():
        acc_ref[...] = jnp.zeros_like(acc_ref)

    acc_ref[...] += _swiglu_chunk(h_ref[...], w1_ref[...], w3_ref[...], w2_ref[...])

    @pl.when(f == pl.num_programs(1) - 1)
    def _():
        o_ref[...] = _residual_out(x_ref[...], mod_ref[0][5:6], acc_ref[...], final_norm, fg_ref[...])


def _tile_mod_row(i, first_tile, tile_rows):
    n_ctx = CTX_ROWS // tile_rows
    t = i + first_tile
    return jnp.where(t < n_ctx, 8, (t - n_ctx) // (SEQ // tile_rows))


def _ffn_call(h2, w1, w3, w2, x, mod_l, final_g, first_row_tile, final_norm):
    t = x.shape[0]
    rows = lambda width: pl.BlockSpec((FFN_ROWS, width), lambda i, f: (i, 0))
    return pl.pallas_call(
        functools.partial(_ffn_kernel, final_norm=final_norm),
        grid=(t // FFN_ROWS, D_FF // FF_CHUNK),
        in_specs=[
            rows(D_MODEL),
            pl.BlockSpec((D_MODEL, FF_CHUNK), lambda i, f: (0, f)),
            pl.BlockSpec((D_MODEL, FF_CHUNK), lambda i, f: (0, f)),
            pl.BlockSpec((FF_CHUNK, D_MODEL), lambda i, f: (f, 0)),
            rows(D_MODEL),
            pl.BlockSpec((1, ADA_CHUNKS, D_MODEL),
                         lambda i, f: (_tile_mod_row(i, first_row_tile, FFN_ROWS), 0, 0)),
            pl.BlockSpec((1, D_MODEL), lambda i, f: (0, 0)),
        ],
        out_specs=rows(D_MODEL),
        out_shape=jax.ShapeDtypeStruct((t, D_MODEL), F32),
        scratch_shapes=[pltpu.VMEM((FFN_ROWS, D_MODEL), F32)],
        compiler_params=_params("arbitrary", "arbitrary"),
        name="ffn",
    )(h2, w1, w3, w2, x, mod_l, final_g)


MOE_TILE = 512
MOE_SRC = 256
MOE_CMB_WIN = MOE_SRC + 16
MOE_DSP_SMALL = 144
MOE_DSP_LARGE = MOE_SRC + 16
TOP_K = 2


def _moe_route(sel, n_tok):
    n_tiles = TOP_K * n_tok // MOE_TILE + N_EXPERTS
    e = sel[:, 0:TOP_K].astype(jnp.int32)
    flat_e = e.reshape(-1)
    onehot = (flat_e[:, None] == jnp.arange(N_EXPERTS)[None, :]).astype(jnp.int32)
    csum = jnp.cumsum(onehot, axis=0)
    rank = jnp.sum((csum - onehot) * onehot, axis=1)
    counts = csum[-1]
    padded = (counts + MOE_TILE - 1) // MOE_TILE * MOE_TILE
    seg_end = jnp.cumsum(padded)
    seg_start = seg_end - padded
    pos = jnp.sum(onehot * seg_start[None, :], axis=1) + rank
    pos2 = pos.reshape(n_tok, TOP_K)
    tile_ix = jnp.arange(n_tiles)
    tile_expert = jnp.minimum(jnp.sum(tile_ix[:, None] >= (seg_end // MOE_TILE)[None, :], axis=1), N_EXPERTS - 1)
    n_valid = seg_end[-1] // MOE_TILE
    n_src = n_tok // MOE_SRC
    per_chunk = MOE_SRC * TOP_K
    before = jnp.concatenate([jnp.zeros((1, N_EXPERTS), jnp.int32), csum[per_chunk - 1::per_chunk]], axis=0)
    run_start = seg_start[None, :] + before[:-1]
    run_cnt = before[1:] - before[:-1]
    cum = jnp.take(before, tile_expert, axis=1)
    local0 = tile_ix * MOE_TILE - jnp.take(seg_start, tile_expert)
    local1 = jnp.minimum(local0 + MOE_TILE, jnp.take(counts, tile_expert))
    c_lo = jnp.sum(cum[1:] <= local0[None, :], axis=0)
    c_hi = jnp.sum(cum[:-1] < local1[None, :], axis=0) - 1
    win = jnp.minimum(run_start // 16 * 16, n_tiles * MOE_TILE - MOE_CMB_WIN)
    def token_rows(a, dtype):
        rows = jnp.transpose(a.reshape(n_src, MOE_SRC, TOP_K), (0, 2, 1)).astype(dtype)
        return jnp.zeros((n_src, 8, MOE_SRC), dtype).at[:, :TOP_K, :].set(rows)

    i32 = lambda a: a.astype(jnp.int32)
    return {
        "n_tiles": n_tiles, "tile_expert": i32(tile_expert), "n_valid": i32(n_valid).reshape(1),
        "c_lo": i32(c_lo), "c_hi": i32(c_hi), "run_start": i32(run_start.reshape(-1)),
        "run_cnt": i32(run_cnt.reshape(-1)), "win": i32(win.reshape(-1)),
        "run_off": i32((run_start - win).reshape(-1)),
        "pos_rows": token_rows(pos2, jnp.int32), "w_rows": token_rows(sel[:, TOP_K:2 * TOP_K], F32),
        "pos_cols": i32(pos2),
    }


def _dispatch_kernel(clo_ref, chi_ref, te_ref, rs_ref, rc_ref, h_ref, pos_ref, w_ref, o_ref, sw_ref, acc_ref):
    i = pl.program_id(0)
    base = i * MOE_TILE
    expert = te_ref[i]
    acc_ref[...] = jnp.zeros_like(acc_ref)
    sw_ref[...] = jnp.zeros_like(sw_ref)

    def body(c, carry):
        run0 = rs_ref[c * N_EXPERTS + expert]
        lo = jnp.maximum(run0, base) - base
        hi = jnp.minimum(run0 + rc_ref[c * N_EXPERTS + expert], base + MOE_TILE) - base
        pos = pos_ref[c]
        wts = w_ref[c]
        h = h_ref[pl.ds(pl.multiple_of(c * MOE_SRC, MOE_SRC), MOE_SRC), :]

        def window(rows):
            ws = pl.multiple_of(jnp.minimum(lo // 16 * 16, MOE_TILE - rows), 16)
            slot = base + ws + lax.broadcasted_iota(jnp.int32, (rows, MOE_SRC), 0)
            hit0 = pos[0:1, :] == slot
            hit1 = pos[1:2, :] == slot
            acc_ref[pl.ds(ws, rows), :] += _dot(jnp.where(hit0 | hit1, 1.0, 0.0).astype(BF16), h)
            weight = jnp.sum(jnp.where(hit0, wts[0:1, :], 0.0) + jnp.where(hit1, wts[1:2, :], 0.0),
                             axis=1, keepdims=True)
            sw_ref[pl.ds(ws, rows), :] += jnp.broadcast_to(weight, (rows, LANES))

        small = hi - jnp.minimum(lo // 16 * 16, MOE_TILE - MOE_DSP_SMALL) <= MOE_DSP_SMALL
        pl.when((hi > lo) & small)(lambda: window(MOE_DSP_SMALL))
        pl.when((hi > lo) & jnp.logical_not(small))(lambda: window(MOE_DSP_LARGE))
        return carry

    lax.fori_loop(clo_ref[i], chi_ref[i] + 1, body, 0)
    o_ref[...] = acc_ref[...].astype(BF16)


def _dispatch_call(h2, route):
    n_tok = h2.shape[0]
    n_tiles = route["n_tiles"]
    grid_spec = pltpu.PrefetchScalarGridSpec(
        num_scalar_prefetch=5,
        grid=(n_tiles,),
        in_specs=[
            pl.BlockSpec((n_tok, D_MODEL), lambda i, *_: (0, 0), pipeline_mode=pl.Buffered(1)),
            pl.BlockSpec((n_tok // MOE_SRC, 8, MOE_SRC), lambda i, *_: (0, 0, 0), pipeline_mode=pl.Buffered(1)),
            pl.BlockSpec((n_tok // MOE_SRC, 8, MOE_SRC), lambda i, *_: (0, 0, 0), pipeline_mode=pl.Buffered(1)),
        ],
        out_specs=[pl.BlockSpec((MOE_TILE, D_MODEL), lambda i, *_: (i, 0)),
                   pl.BlockSpec((MOE_TILE, LANES), lambda i, *_: (i, 0))],
        scratch_shapes=[pltpu.VMEM((MOE_TILE, D_MODEL), F32)],
    )
    return pl.pallas_call(
        _dispatch_kernel,
        grid_spec=grid_spec,
        out_shape=[jax.ShapeDtypeStruct((n_tiles * MOE_TILE, D_MODEL), BF16),
                   jax.ShapeDtypeStruct((n_tiles * MOE_TILE, LANES), F32)],
        compiler_params=_params("arbitrary"),
        name="moe_dispatch",
    )(route["c_lo"], route["c_hi"], route["tile_expert"], route["run_start"], route["run_cnt"],
      h2, route["pos_rows"], route["w_rows"])


def _expert_kernel(te_ref, nv_ref, x_ref, w1_ref, w3_ref, w2_ref, sw_ref, o_ref, acc_ref):
    i = pl.program_id(0)
    f = pl.program_id(1)
    last = pl.num_programs(1) - 1
    valid = i < nv_ref[0]

    @pl.when(valid & (f == 0))
    def _():
        acc_ref[...] = jnp.zeros_like(acc_ref)

    @pl.when(valid)
    def _():
        acc_ref[...] += _swiglu_chunk(x_ref[...], w1_ref[0, 0], w3_ref[0, 0], w2_ref[0, 0])

    @pl.when(valid & (f == last))
    def _():
        o_ref[...] = (acc_ref[...] * sw_ref[:, 0:1]).astype(BF16)

    @pl.when(jnp.logical_not(valid) & (f == last))
    def _():
        o_ref[...] = jnp.zeros_like(o_ref)


def _expert_call(xs, slot_w, route, w1, w3, w2, layer):
    n_tiles = route["n_tiles"]
    grid_spec = pltpu.PrefetchScalarGridSpec(
        num_scalar_prefetch=2,
        grid=(n_tiles, D_FF // MOE_FF_CHUNK),
        in_specs=[
            pl.BlockSpec((MOE_TILE, D_MODEL), lambda i, f, te, nv: (i, 0)),
            pl.BlockSpec((1, 1, D_MODEL, MOE_FF_CHUNK), lambda i, f, te, nv: (layer, te[i], 0, f)),
            pl.BlockSpec((1, 1, D_MODEL, MOE_FF_CHUNK), lambda i, f, te, nv: (layer, te[i], 0, f)),
            pl.BlockSpec((1, 1, MOE_FF_CHUNK, D_MODEL), lambda i, f, te, nv: (layer, te[i], f, 0)),
            pl.BlockSpec((MOE_TILE, LANES), lambda i, f, te, nv: (i, 0)),
        ],
        out_specs=pl.BlockSpec((MOE_TILE, D_MODEL), lambda i, f, te, nv: (i, 0)),
        scratch_shapes=[pltpu.VMEM((MOE_TILE, D_MODEL), F32)],
    )
    return pl.pallas_call(
        _expert_kernel,
        grid_spec=grid_spec,
        out_shape=jax.ShapeDtypeStruct((n_tiles * MOE_TILE, D_MODEL), BF16),
        compiler_params=_params("arbitrary", "arbitrary"),
        name="moe_experts",
    )(route["tile_expert"], route["n_valid"], xs, w1, w3, w2, slot_w)


def _combine_kernel(win_ref, off_ref, cnt_ref, y_hbm, pos_ref, x_ref, mod_ref, fg_ref, o_ref, buf_ref, sem_ref,
                    *, final_norm):
    c = pl.program_id(0)
    cur = c % 2

    def window_copies(chunk, buf_set):
        return [pltpu.make_async_copy(
            y_hbm.at[pl.ds(pl.multiple_of(win_ref[chunk * N_EXPERTS + e], 16), MOE_CMB_WIN), :],
            buf_ref.at[buf_set, pl.ds(e * MOE_CMB_WIN, MOE_CMB_WIN), :],
            sem_ref.at[buf_set, e]) for e in range(N_EXPERTS)]

    @pl.when(c == 0)
    def _():
        for cp in window_copies(0, 0):
            cp.start()

    @pl.when(c + 1 < pl.num_programs(0))
    def _():
        for cp in window_copies(c + 1, 1 - cur):
            cp.start()

    row = lax.broadcasted_iota(jnp.int32, (1, N_EXPERTS * MOE_CMB_WIN), 1)
    slot = jnp.full((1, N_EXPERTS * MOE_CMB_WIN), -1, jnp.int32)
    for e in range(N_EXPERTS):
        local = row - e * MOE_CMB_WIN
        off = off_ref[c * N_EXPERTS + e]
        inside = (local >= off) & (local < off + cnt_ref[c * N_EXPERTS + e]) & (local < MOE_CMB_WIN)
        slot = jnp.where(inside, win_ref[c * N_EXPERTS + e] + local, slot)
    pos = pos_ref[...]
    hit = (pos[:, 0:1] == slot) | (pos[:, 1:2] == slot)
    onehot = jnp.where(hit, 1.0, 0.0).astype(BF16)
    for cp in window_copies(c, cur):
        cp.wait()
    update = _dot(onehot, buf_ref[cur])
    o_ref[...] = _residual_out(x_ref[...], mod_ref[0][5:6], update, final_norm, fg_ref[...])


def _combine_call(ys, route, x, mod_l, final_g, first_tile, final_norm):
    n_tok = x.shape[0]
    rows = lambda width: pl.BlockSpec((MOE_SRC, width), lambda c, *_: (c, 0))
    grid_spec = pltpu.PrefetchScalarGridSpec(
        num_scalar_prefetch=3,
        grid=(n_tok // MOE_SRC,),
        in_specs=[
            pl.BlockSpec(memory_space=pl.ANY),
            rows(TOP_K),
            rows(D_MODEL),
            pl.BlockSpec((1, ADA_CHUNKS, D_MODEL), lambda c, *_: (_tile_mod_row(c, first_tile, MOE_SRC), 0, 0)),
            pl.BlockSpec((1, D_MODEL), lambda c, *_: (0, 0)),
        ],
        out_specs=rows(D_MODEL),
        scratch_shapes=[pltpu.VMEM((2, N_EXPERTS * MOE_CMB_WIN, D_MODEL), BF16),
                        pltpu.SemaphoreType.DMA((2, N_EXPERTS))],
    )
    return pl.pallas_call(
        functools.partial(_combine_kernel, final_norm=final_norm),
        grid_spec=grid_spec,
        out_shape=jax.ShapeDtypeStruct((n_tok, D_MODEL), F32),
        compiler_params=_params("arbitrary"),
        name="moe_combine",
    )(route["win"], route["run_off"], route["run_cnt"], ys, route["pos_cols"], x, mod_l, final_g)


def _rope_tables():
    t = np.arange(SEQ)

    def angles(rot_dim):
        half = rot_dim // 2
        inv = ROPE_THETA ** (-jnp.arange(0, half, 2, dtype=F32) / half)
        ang = jnp.concatenate([jnp.asarray(t // GRID_W, F32)[:, None] * inv[None, :],
                               jnp.asarray(t % GRID_W, F32)[:, None] * inv[None, :]], axis=-1)
        return jnp.cos(ang), jnp.sin(ang)

    def pad_rows(a, fill):
        return jnp.concatenate([jnp.full((ROW_TILE, a.shape[1]), fill, F32), a], axis=0)

    c, s = angles(HEAD_DIM)
    cos64 = jnp.tile(jnp.concatenate([c, c], axis=-1), (1, 4))
    sin64 = jnp.tile(jnp.concatenate([s, s], axis=-1), (1, 4))
    c, s = angles(MLA_ROPE)
    one = jnp.ones((SEQ, MLA_NOPE), F32)
    zero = jnp.zeros((SEQ, MLA_NOPE), F32)
    tail1 = jnp.ones((SEQ, MLA_PAD - MLA_NOPE - MLA_ROPE), F32)
    tail0 = jnp.zeros((SEQ, MLA_PAD - MLA_NOPE - MLA_ROPE), F32)
    cosm = jnp.tile(jnp.concatenate([one, c, c, tail1], axis=-1), (1, MLA_HEADS))
    sinm = jnp.tile(jnp.concatenate([zero, s, s, tail0], axis=-1), (1, MLA_HEADS))
    blockdiag = jnp.asarray(np.kron(np.eye(4), np.ones((HEAD_DIM, HEAD_DIM))), BF16)
    place = np.zeros((256, MLA_HEADS * MLA_PAD), np.float32)
    for h in range(MLA_HEADS):
        for r in range(MLA_ROPE):
            place[MLA_KV_RANK + r, h * MLA_PAD + MLA_NOPE + r] = 1.0
    return {
        "cos64": pad_rows(cos64, 1.0), "sin64": pad_rows(sin64, 0.0),
        "cosm": pad_rows(cosm, 1.0), "sinm": pad_rows(sinm, 0.0),
        "blockdiag": blockdiag, "pe_place": jnp.asarray(place, BF16),
    }


def _pack_w_in(w):
    n_qkv = COL_DKVA + MLA_KV_RANK + MLA_ROPE
    pad = jnp.zeros(w.shape[:2] + (COL_GATE - n_qkv,), BF16)
    return jnp.concatenate([w[..., :n_qkv].astype(BF16), pad, w[..., n_qkv:].astype(BF16)], axis=-1)


def _pack_mla(wqb, wkvb):
    dqh = MLA_NOPE + MLA_ROPE
    q = wqb.reshape(MLA_Q_RANK, MLA_HEADS, dqh)
    q = jnp.pad(q, ((0, 0), (0, 0), (0, MLA_PAD - dqh))).reshape(MLA_Q_RANK, MLA_HEADS * MLA_PAD)
    kv = wkvb.reshape(MLA_KV_RANK, MLA_HEADS, MLA_NOPE + MLA_V)
    k = jnp.pad(kv[:, :, :MLA_NOPE], ((0, 0), (0, 0), (0, MLA_PAD - MLA_NOPE)))
    k = k.reshape(MLA_KV_RANK, MLA_HEADS * MLA_PAD)
    v = jnp.pad(kv[:, :, MLA_NOPE:], ((0, 0), (0, 0), (0, MLA_PAD - MLA_V))).reshape(MLA_KV_RANK, MLA_HEADS * MLA_PAD)
    return q.astype(BF16), k.astype(BF16), v.astype(BF16)


def _split2_host(w):
    hi = w.astype(BF16)
    return jnp.stack([hi, (w - hi.astype(F32)).astype(BF16)])


def kernel(x, c, ctx, c_ctx, norm1_g, norm2_g, w_ada, b_ada, w_in, na_rpb, gb_qnorm, gb_knorm, wc_sink,
           mla_qnorm, mla_kvnorm, mla_wqb, mla_wkvb, w_branch, w_out, ffn_w1, ffn_w3, ffn_w2,
           moe_router, moe_w1, moe_w3, moe_w2, final_g):
    n_batch = x.shape[0]
    assert x.shape[1:] == (SEQ, D_MODEL) and ctx.shape[1:] == (CTX_LEN, D_MODEL)
    assert n_batch * CTX_LEN <= CTX_ROWS and n_batch <= 8

    ctx_rows = ctx.reshape(n_batch * CTX_LEN, D_MODEL)
    if ctx_rows.shape[0] < CTX_ROWS:
        ctx_rows = jnp.pad(ctx_rows, ((0, CTX_ROWS - ctx_rows.shape[0]), (0, 0)))
    xt = jnp.concatenate([ctx_rows, x.reshape(n_batch * SEQ, D_MODEL)], axis=0)

    cvec = jnp.zeros((16, D_MODEL), F32).at[:n_batch].set(c).at[8].set(c_ctx)
    mod = _mod_call(cvec, w_ada, b_ada).reshape(DEPTH, 16, ADA_CHUNKS, D_MODEL)
    tabs = _rope_tables()
    w_in_p = _pack_w_in(w_in)
    moe_w = tuple(w.astype(BF16) for w in (moe_w1, moe_w3, moe_w2))
    mla_scale = (MLA_NOPE + MLA_ROPE) ** -0.5
    lat_tile0 = CTX_ROWS // ROW_TILE

    for l in range(DEPTH):
        with_ctx = l < DEPTH - 1
        wq, wk, wv = _pack_mla(mla_wqb[l], mla_wkvb[l])
        lw = {
            "gq": jnp.tile(gb_qnorm[l], 4)[None, :], "gk": jnp.tile(gb_knorm[l], 2)[None, :],
            "qn": mla_qnorm[l][None, :], "kvn": mla_kvnorm[l][None, :], "wqb": wq, "wk": wk, "wv": wv,
        }
        pa, cv, gate, qb, kb, qc, kc, qd, kd, vd, vb = _inproj_call(
            xt, mod[l], norm1_g[l][None, :], w_in_p, l, tabs, lw)

        oa = _natten_call(n_batch, with_ctx, pa, _natten_bias_table(na_rpb[l]))
        ob = _dense_attn_call(n_batch, with_ctx, (qb, 256, 0), kb, (vb, GB_KV_HEADS * LANES, 0),
                              n_kv=GB_KV_HEADS, group=GB_Q_HEADS // GB_KV_HEADS, dq=HEAD_DIM, name="attn_global")
        oc = _window_attn_call(n_batch, with_ctx, (qc, 256, 0), (kc, 128, 0), (cv, 128, 0), wc_sink[l])
        od = _dense_attn_call(n_batch, with_ctx, (qd, MLA_HEADS * MLA_PAD, 0), kd, (vd, MLA_HEADS * MLA_PAD, 0),
                              n_kv=MLA_HEADS, group=1, dq=MLA_PAD, scale=mla_scale, name="attn_mla")

        is_moe = l % 2 == 1
        router3 = None
        if is_moe:
            router3 = _split2_host(jnp.pad(moe_router[l // 2], ((0, 0), (0, LANES - N_EXPERTS))))
        first_tile = 0 if with_ctx else lat_tile0
        res = _merge_call((oa, ob, oc, od), gate, xt, mod[l], norm2_g[l][None, :],
                          w_branch[l].astype(BF16), w_out[l].astype(BF16), router3, first_tile)
        x_mid, h2 = res[0], res[1]
        final_norm = l == DEPTH - 1
        if is_moe:
            route = _moe_route(res[2], h2.shape[0])
            xs, slot_w = _dispatch_call(h2, route)
            ys = _expert_call(xs, slot_w, route, *moe_w, l // 2)
            xt = _combine_call(ys, route, x_mid, mod[l], final_g[None, :],
                               0 if with_ctx else CTX_ROWS // MOE_SRC, final_norm)
        else:
            w1, w3, w2 = (w[l // 2].astype(BF16) for w in (ffn_w1, ffn_w3, ffn_w2))
            xt = _ffn_call(h2, w1, w3, w2, x_mid, mod[l], final_g[None, :],
                           0 if with_ctx else CTX_ROWS // FFN_ROWS, final_norm)

    return xt.reshape(n_batch, SEQ, D_MODEL)
```

```python
import functools

import numpy as np
import jax
import jax.numpy as jnp
from jax import lax
from jax.experimental import pallas as pl
from jax.experimental.pallas import tpu as pltpu

F32 = jnp.float32
BF16 = jnp.bfloat16

D_MODEL = 1024
SEQ = 2048
DEPTH = 4
CTX_LEN = 256
GRID_W = 64
GRID_ROWS = SEQ // GRID_W
HEAD_DIM = 64
ROPE_THETA = 10000.0
NORM_EPS = 1e-6
NEG_INF = -1e30

NA_HEADS = 4
NA_KH = 8
NA_KW = 16
GB_Q_HEADS = 4
GB_KV_HEADS = 2
WC_Q_HEADS = 4
WC_KV_HEADS = 2
WC_WINDOW = 128
MLA_HEADS = 4
MLA_Q_RANK = 256
MLA_KV_RANK = 128
MLA_NOPE = 64
MLA_ROPE = 32
MLA_V = 64
MLA_PAD = 128
N_BRANCH = 4
BRANCH_W = 256
D_FF = 3584
N_EXPERTS = 8
ADA_CHUNKS = 6

VMEM_LIMIT_BYTES = 56 * 1024 * 1024
LANES = 128

CTX_ROWS = 2048
ROW_TILE = 512
Q_TILE = 256
NA_Q_ROWS = Q_TILE // GRID_W
NA_WIN_ROWS = 12
NA_WIN = NA_WIN_ROWS * GRID_W
WC_WIN = Q_TILE + 2 * WC_WINDOW
ATTN_KEY_CHUNK = 512

COL_AQ, COL_AK, COL_AV = 0, 256, 512
COL_BQ, COL_BK, COL_BV = 768, 1024, 1152
COL_CQ, COL_CK, COL_CV = 1280, 1536, 1664
COL_DQA, COL_DKVA, COL_GATE = 1792, 2048, 2304
GATE_W = N_BRANCH * D_MODEL
P_WIDTH = COL_GATE + GATE_W
QKV_CHUNK = COL_GATE // 2
FF_CHUNK = 512
MOE_FF_CHUNK = 1792


def _params(*sem):
    return pltpu.CompilerParams(dimension_semantics=sem, vmem_limit_bytes=VMEM_LIMIT_BYTES)


def _dot(a, b):
    return jnp.dot(a, b, preferred_element_type=F32)


def _dot_nt(a, b):
    return lax.dot_general(a, b, (((1,), (1,)), ((), ())), preferred_element_type=F32)


def _split3(x):
    hi = x.astype(BF16)
    r1 = x - hi.astype(F32)
    mid = r1.astype(BF16)
    lo = (r1 - mid.astype(F32)).astype(BF16)
    return hi, mid, lo


def _sigmoid(x):
    return 0.5 * jnp.tanh(0.5 * x) + 0.5


ADA_TN = 1536


def _mod_kernel(c_ref, w_ref, b_ref, o_ref):
    c = c_ref[...]
    sc = (c * _sigmoid(c)).astype(BF16)
    o_ref[0] = _dot(sc, w_ref[0].astype(BF16)) + b_ref[0]


def _mod_call(cvec, w_ada, b_ada):
    n = ADA_CHUNKS * D_MODEL
    return pl.pallas_call(
        _mod_kernel,
        grid=(DEPTH, n // ADA_TN),
        in_specs=[
            pl.BlockSpec((16, D_MODEL), lambda l, j: (0, 0)),
            pl.BlockSpec((1, D_MODEL, ADA_TN), lambda l, j: (l, 0, j)),
            pl.BlockSpec((1, 1, ADA_TN), lambda l, j: (l, 0, j)),
        ],
        out_specs=pl.BlockSpec((1, 16, ADA_TN), lambda l, j: (l, 0, j)),
        out_shape=jax.ShapeDtypeStruct((DEPTH, 16, n), F32),
        compiler_params=_params("arbitrary", "arbitrary"),
        name="adaln_mod",
    )(cvec, w_ada, b_ada.reshape(DEPTH, 1, n))


def _mod_row(i, first_tile):
    n_ctx = CTX_ROWS // ROW_TILE
    t = i + first_tile
    return jnp.where(t < n_ctx, 8, (t - n_ctx) // (SEQ // ROW_TILE))


def _norm_mod(x, g, shift, scale):
    ms = jnp.mean(x * x, axis=-1, keepdims=True)
    y = x * lax.rsqrt(ms + NORM_EPS) * g
    return y * (1.0 + scale) + shift


def _rope(x, cos, sin, half, first_mask):
    w = x.shape[-1]
    fwd = pltpu.roll(x, w - half, 1)
    bwd = pltpu.roll(x, half, 1)
    rot = jnp.where(first_mask, -fwd, bwd)
    return x * cos + rot * sin


def _head_rms(x, gain, blockdiag):
    hi, mid, lo = _split3(x * x)
    ss = _dot(hi, blockdiag) + _dot(mid, blockdiag) + _dot(lo, blockdiag)
    return x * lax.rsqrt(ss * (1.0 / HEAD_DIM) + NORM_EPS) * gain


def _inproj_kernel(x_ref, mod_ref, g_ref, w_ref,
                   cos64_ref, sin64_ref, cosm_ref, sinm_ref,
                   gq_ref, gk_ref, bd_ref, qn_ref, kvn_ref, wqb_ref, wk_ref, wv_ref, pe_ref,
                   pa_ref, cv_ref, gate_ref, qb_ref, kb_ref, qc_ref, kc_ref, qd_ref, kd_ref, vd_ref, vb_ref):
    m = mod_ref[0]
    hb = _norm_mod(x_ref[...], g_ref[...], m[0:1], m[1:2]).astype(BF16)

    def proj(c0, c1):
        return _dot(hb, w_ref[0, :, c0:c1])

    for c0 in range(0, GATE_W, D_MODEL):
        gate_ref[:, c0:c0 + D_MODEL] = proj(COL_GATE + c0, COL_GATE + c0 + D_MODEL).astype(BF16)
    pa_ref[...] = proj(COL_AQ, COL_BQ).astype(BF16)

    scale = HEAD_DIM ** -0.5
    half = HEAD_DIM // 2
    cos64, sin64 = cos64_ref[...], sin64_ref[...]
    lane = lax.broadcasted_iota(jnp.int32, (1, 256), 1)
    first64 = (lane % HEAD_DIM) < half
    bd = bd_ref[...]
    bc = proj(COL_BQ, COL_DQA)
    bq = _head_rms(bc[:, 0:256], gq_ref[...], bd)
    qb_ref[...] = (_rope(bq, cos64, sin64, half, first64) * scale).astype(BF16)
    bk = _head_rms(bc[:, 256:384], gk_ref[...], bd[:128, :128])
    kb_ref[...] = _rope(bk, cos64[:, :128], sin64[:, :128], half, first64[:, :128]).T.astype(BF16)
    qc_ref[...] = (_rope(bc[:, 512:768], cos64, sin64, half, first64) * scale).astype(BF16)
    kc_ref[...] = _rope(bc[:, 768:896], cos64[:, :128], sin64[:, :128], half, first64[:, :128]).astype(BF16)
    cv_ref[...] = bc[:, 896:1024].astype(BF16)

    cosm, sinm = cosm_ref[...], sinm_ref[...]
    lane_m = lax.broadcasted_iota(jnp.int32, (1, MLA_HEADS * MLA_PAD), 1) % MLA_PAD
    first_m = lane_m < (MLA_NOPE + MLA_ROPE // 2)
    d = proj(COL_DQA, COL_GATE)
    dqa = d[:, 0:MLA_Q_RANK]
    qn = dqa * lax.rsqrt(jnp.mean(dqa * dqa, axis=-1, keepdims=True) + NORM_EPS) * qn_ref[...]
    dq = _dot(qn.astype(BF16), wqb_ref[...])
    qd_ref[...] = _rope(dq, cosm, sinm, MLA_ROPE // 2, first_m).astype(BF16)

    dkva = d[:, MLA_Q_RANK:]
    kvc = dkva[:, :MLA_KV_RANK]
    kvn = (kvc * lax.rsqrt(jnp.mean(kvc * kvc, axis=-1, keepdims=True) + NORM_EPS) * kvn_ref[...]).astype(BF16)
    dk = _dot(kvn, wk_ref[...]) + _dot(dkva.astype(BF16), pe_ref[...])
    kd_ref[...] = _rope(dk, cosm, sinm, MLA_ROPE // 2, first_m).T.astype(BF16)
    ones_half = lane_m >= HEAD_DIM
    vd_ref[...] = jnp.where(ones_half, 1.0, _dot(kvn, wv_ref[...])).astype(BF16)
    bv = bc[:, 384:512]
    vb_ref[:, :LANES] = jnp.where(ones_half[:, :LANES], 1.0, bv).astype(BF16)
    vb_ref[:, LANES:] = jnp.where(ones_half[:, :LANES], 1.0, pltpu.roll(bv, HEAD_DIM, 1)).astype(BF16)


def _rope_row_block(i):
    n_ctx = CTX_ROWS // ROW_TILE
    return jnp.where(i < n_ctx, 0, 1 + (i - n_ctx) % (SEQ // ROW_TILE))


def _inproj_call(x, mod_l, g, w_in_p, layer, tabs, lw):
    t = x.shape[0]
    mw = MLA_HEADS * MLA_PAD

    def full(shape):
        return pl.BlockSpec(shape, lambda i: (0,) * len(shape))

    def rows(width):
        return pl.BlockSpec((ROW_TILE, width), lambda i: (i, 0))

    def tab(width):
        return pl.BlockSpec((ROW_TILE, width), lambda i: (_rope_row_block(i), 0))

    def cols(height):
        return pl.BlockSpec((height, ROW_TILE), lambda i: (0, i))

    outs = [(COL_BQ, False), (128, False), (GATE_W, False),
            (256, False), (128, True), (256, False), (128, False), (mw, False), (mw, True), (mw, False),
            (GB_KV_HEADS * LANES, False)]
    return pl.pallas_call(
        _inproj_kernel,
        grid=(t // ROW_TILE,),
        in_specs=[
            rows(D_MODEL),
            pl.BlockSpec((1, ADA_CHUNKS, D_MODEL), lambda i: (_mod_row(i, 0), 0, 0)),
            full((1, D_MODEL)),
            pl.BlockSpec((1, D_MODEL, P_WIDTH), lambda i: (layer, 0, 0), pipeline_mode=pl.Buffered(1)),
            tab(256), tab(256), tab(mw), tab(mw),
            full((1, 256)), full((1, 128)), full((256, 256)), full((1, MLA_Q_RANK)), full((1, MLA_KV_RANK)),
            full((MLA_Q_RANK, mw)), full((MLA_KV_RANK, mw)), full((MLA_KV_RANK, mw)), full((256, mw)),
        ],
        out_specs=[cols(w) if tr else rows(w) for w, tr in outs],
        out_shape=[jax.ShapeDtypeStruct((w, t) if tr else (t, w), BF16) for w, tr in outs],
        compiler_params=_params("arbitrary"),
        name="inproj",
    )(x, mod_l, g, w_in_p, tabs["cos64"], tabs["sin64"], tabs["cosm"], tabs["sinm"],
      lw["gq"], lw["gk"], tabs["blockdiag"], lw["qn"], lw["kvn"], lw["wqb"], lw["wk"], lw["wv"], tabs["pe_place"])


def _attend(q, segs, sink=None, scale=None):
    chunks = segs
    scores = []
    m = None
    for k, _, bias in chunks:
        s = _dot_nt(q, k)
        if scale is not None:
            s = s * scale
        if bias is not None:
            s = s + bias
        scores.append(s)
        ms = jnp.max(s, axis=-1, keepdims=True)
        m = ms if m is None else jnp.maximum(m, ms)
    if sink is not None:
        m = jnp.maximum(m, sink)
    denom = None
    out = None
    for s, (_, v, _) in zip(scores, chunks):
        p = jnp.exp(s - m)
        ps = jnp.sum(p, axis=-1, keepdims=True)
        pv = _dot(p.astype(BF16), v)
        denom = ps if denom is None else denom + ps
        out = pv if out is None else out + pv
    if sink is not None:
        denom = denom + jnp.exp(sink - m)
    return out / denom


def _gqa_heads(q_ref, o_ref, sink_ref, n_kv, group, dq, dv, scale, seg_fn):
    tq = q_ref.shape[0]
    for h in range(n_kv):
        heads = [h * group + g for g in range(group)]
        q = jnp.concatenate([q_ref[:, a * dq:(a + 1) * dq] for a in heads], axis=0) if group > 1 \
            else q_ref[:, h * dq:(h + 1) * dq]
        sink = None
        if sink_ref is not None:
            sink = jnp.concatenate([jnp.full((tq, 1), sink_ref[a], F32) for a in heads], axis=0)
        o = _attend(q, seg_fn(h), sink=sink, scale=scale)
        for g, a in enumerate(heads):
            o_ref[:, a * dv:(a + 1) * dv] = o[g * tq:(g + 1) * tq].astype(BF16)


def _dense_attn_kernel(q_ref, ktl_ref, vl_ref, ktc_ref, vc_ref, o_ref, *, n_kv, group, dq, scale, ctx_tile):
    tq = q_ref.shape[0]
    hd = HEAD_DIM

    def run(with_latent):
        for h in range(n_kv):
            heads = [h * group + g for g in range(group)]
            q = jnp.concatenate([q_ref[:, a * dq:(a + 1) * dq] for a in heads], axis=0) if group > 1 \
                else q_ref[:, h * dq:(h + 1) * dq]
            segs = [(ktc_ref[h * dq:(h + 1) * dq, :], vc_ref[:, h * LANES:(h + 1) * LANES])]
            if with_latent:
                segs = [(ktl_ref[h * dq:(h + 1) * dq, c0:c0 + ATTN_KEY_CHUNK],
                         vl_ref[c0:c0 + ATTN_KEY_CHUNK, h * LANES:(h + 1) * LANES])
                        for c0 in range(0, SEQ, ATTN_KEY_CHUNK)] + segs
            scores = []
            m = None
            for kt, _ in segs:
                s = _dot(q, kt)
                if scale is not None:
                    s = s * scale
                scores.append(s)
                ms = jnp.max(s, axis=-1, keepdims=True)
                m = ms if m is None else jnp.maximum(m, ms)
            acc = None
            for s, (_, v) in zip(scores, segs):
                pv = _dot(jnp.exp(s - m).astype(BF16), v)
                acc = pv if acc is None else acc + pv
            o = acc * pltpu.roll(1.0 / acc, hd, 1)
            for g, a in enumerate(heads):
                o_ref[:, a * hd:(a + 1) * hd] = o[g * tq:(g + 1) * tq, :hd].astype(BF16)

    if ctx_tile:
        j = pl.program_id(1)
        pl.when(j == 0)(lambda: run(False))
        pl.when(j > 0)(lambda: run(True))
    else:
        run(True)


def _q_row_block(b, j, n_batch, with_ctx):
    per_batch = SEQ // Q_TILE
    lat0 = CTX_ROWS // Q_TILE
    if with_ctx:
        return jnp.where(j == 0, b, lat0 + b * per_batch + j - 1)
    return lat0 + b * per_batch + j


def _attn_specs(n_batch, with_ctx, q, k, v, o_width, order_bj=True):
    def ix(f):
        return (lambda b, j: f(b, j)) if order_bj else (lambda j, b: f(b, j))

    def qspec(width, col):
        return pl.BlockSpec((Q_TILE, width), ix(lambda b, j: (_q_row_block(b, j, n_batch, with_ctx), col // width)))

    def lat(width, col):
        return pl.BlockSpec((SEQ, width), ix(lambda b, j: (CTX_ROWS // SEQ + b, col // width)))

    def ctx(width, col):
        return pl.BlockSpec((CTX_LEN, width), ix(lambda b, j: (b, col // width)))

    in_specs = [qspec(q[1], q[2]), lat(k[1], k[2]), lat(v[1], v[2]), ctx(k[1], k[2]), ctx(v[1], v[2])]
    args = [q[0], k[0], v[0], k[0], v[0]]
    if with_ctx:
        return in_specs, args, qspec(o_width, 0)
    out_spec = pl.BlockSpec((Q_TILE, o_width), ix(lambda b, j: (b * (SEQ // Q_TILE) + j, 0)))
    return in_specs, args, out_spec


def _attn_out_rows(t, with_ctx):
    return t if with_ctx else t - CTX_ROWS


def _dense_attn_call(n_batch, with_ctx, q, kt, v, *, n_kv, group, dq, scale=None, name):
    t = q[0].shape[0]
    in_specs, args, out_spec = _attn_specs(n_batch, with_ctx, q, v, v, 256)
    kt_rows = kt.shape[0]
    in_specs[1] = pl.BlockSpec((kt_rows, SEQ), lambda b, j: (0, CTX_ROWS // SEQ + b))
    in_specs[3] = pl.BlockSpec((kt_rows, CTX_LEN), lambda b, j: (0, b))
    args[1] = args[3] = kt
    kern = functools.partial(_dense_attn_kernel, n_kv=n_kv, group=group, dq=dq, scale=scale, ctx_tile=with_ctx)
    return pl.pallas_call(
        kern,
        grid=(n_batch, SEQ // Q_TILE + (1 if with_ctx else 0)),
        in_specs=in_specs,
        out_specs=out_spec,
        out_shape=jax.ShapeDtypeStruct((_attn_out_rows(t, with_ctx), 256), BF16),
        compiler_params=_params("arbitrary", "arbitrary"),
        name=name,
    )(*args)


def _window_attn_kernel(sink_ref, q_ref, kl_ref, vl_ref, kc_ref, vc_ref, o_ref, *, ctx_tile):
    j = pl.program_id(1)
    hd = HEAD_DIM
    group = WC_Q_HEADS // WC_KV_HEADS

    def ctx_run():
        def segs(h):
            return [(kc_ref[:, h * hd:(h + 1) * hd], vc_ref[:, h * hd:(h + 1) * hd], None)]
        _gqa_heads(q_ref, o_ref, sink_ref, WC_KV_HEADS, group, hd, hd, None, segs)

    def lat_run():
        i = j - 1 if ctx_tile else j
        start = pl.multiple_of(jnp.clip(i * Q_TILE - WC_WINDOW, 0, SEQ - WC_WIN), WC_WINDOW)
        qpos = i * Q_TILE + lax.broadcasted_iota(jnp.int32, (group * Q_TILE, WC_WIN), 0) % Q_TILE
        kpos = start + lax.broadcasted_iota(jnp.int32, (group * Q_TILE, WC_WIN), 1)
        bias = jnp.where(jnp.abs(qpos - kpos) <= WC_WINDOW, 0.0, NEG_INF).astype(F32)

        def segs(h):
            return [(kl_ref[pl.ds(start, WC_WIN), h * hd:(h + 1) * hd],
                     vl_ref[pl.ds(start, WC_WIN), h * hd:(h + 1) * hd], bias),
                    (kc_ref[:, h * hd:(h + 1) * hd], vc_ref[:, h * hd:(h + 1) * hd], None)]
        _gqa_heads(q_ref, o_ref, sink_ref, WC_KV_HEADS, group, hd, hd, None, segs)

    if ctx_tile:
        pl.when(j == 0)(ctx_run)
        pl.when(j > 0)(lat_run)
    else:
        lat_run()


def _window_attn_call(n_batch, with_ctx, q, k, v, sink):
    t = q[0].shape[0]
    in_specs, args, out_spec = _attn_specs(n_batch, with_ctx, q, k, v, 256)
    return pl.pallas_call(
        functools.partial(_window_attn_kernel, ctx_tile=with_ctx),
        grid=(n_batch, SEQ // Q_TILE + (1 if with_ctx else 0)),
        in_specs=[pl.BlockSpec(memory_space=pltpu.SMEM)] + in_specs,
        out_specs=out_spec,
        out_shape=jax.ShapeDtypeStruct((_attn_out_rows(t, with_ctx), 256), BF16),
        compiler_params=_params("arbitrary", "arbitrary"),
        name="attn_window",
    )(sink, *args)


def _natten_kernel(q_ref, kl_ref, vl_ref, kc_ref, vc_ref, tab_ref, o_ref, bias_ref, *, ctx_tile):
    j = pl.program_id(0)
    hd = HEAD_DIM
    scale = HEAD_DIM ** -0.5

    def build_bias(i):
        ws = jnp.clip(NA_Q_ROWS * i - NA_KH // 2, 0, GRID_ROWS - NA_WIN_ROWS)
        for a in range(NA_Q_ROWS):
            r = NA_Q_ROWS * i + a
            krow0 = jnp.clip(r - NA_KH // 2, 0, GRID_ROWS - NA_KH)
            for pair in range(NA_WIN_ROWS // 2):
                idx = []
                for side in range(2):
                    kr = ws + 2 * pair + side
                    in_rows = (kr >= krow0) & (kr < krow0 + NA_KH)
                    idx.append(jnp.where(in_rows, kr - r + NA_KH - 1, NA_NO_ROW))
                for h in range(NA_HEADS):
                    bias_ref[h, a * GRID_W:(a + 1) * GRID_W, pair * LANES:(pair + 1) * LANES] = (
                        tab_ref[0, h, idx[0]] + tab_ref[1, h, idx[1]])

    def ctx_run():
        for h in range(NA_HEADS):
            sl = slice(h * hd, (h + 1) * hd)
            o = _attend(q_ref[:, sl] * scale, [(kc_ref[:, sl], vc_ref[:, sl], None)])
            o_ref[:, sl] = o.astype(BF16)

    def lat_run():
        i = j - 1 if ctx_tile else j
        ws = jnp.clip(NA_Q_ROWS * i - NA_KH // 2, 0, GRID_ROWS - NA_WIN_ROWS)
        start = pl.multiple_of(ws * GRID_W, GRID_W)
        pl.when(pl.program_id(1) == 0)(lambda: build_bias(i))
        for h in range(NA_HEADS):
            sl = slice(h * hd, (h + 1) * hd)
            segs = [(kl_ref[pl.ds(start, NA_WIN), sl], vl_ref[pl.ds(start, NA_WIN), sl], bias_ref[h]),
                    (kc_ref[:, sl], vc_ref[:, sl], None)]
            o_ref[:, sl] = _attend(q_ref[:, sl] * scale, segs).astype(BF16)

    if ctx_tile:
        pl.when(j == 0)(ctx_run)
        pl.when(j > 0)(lat_run)
    else:
        lat_run()


NA_NO_ROW = 2 * NA_KH - 1


def _natten_call(n_batch, with_ctx, p, bias_blocks):
    t = p.shape[0]
    q, k, v = (p, 256, COL_AQ), (p, 256, COL_AK), (p, 256, COL_AV)
    in_specs, args, out_spec = _attn_specs(n_batch, with_ctx, q, k, v, 256, order_bj=False)
    in_specs.append(pl.BlockSpec(bias_blocks.shape, lambda j, b: (0,) * bias_blocks.ndim))
    return pl.pallas_call(
        functools.partial(_natten_kernel, ctx_tile=with_ctx),
        grid=(SEQ // Q_TILE + (1 if with_ctx else 0), n_batch),
        in_specs=in_specs,
        out_specs=out_spec,
        out_shape=jax.ShapeDtypeStruct((_attn_out_rows(t, with_ctx), 256), BF16),
        scratch_shapes=[pltpu.VMEM((NA_HEADS, Q_TILE, NA_WIN), F32)],
        compiler_params=_params("arbitrary", "arbitrary"),
        name="attn_natten",
    )(*args, bias_blocks)


def _natten_bias_table(rpb):
    n_dr, n_dc = 2 * NA_KH - 1, 2 * NA_KW - 1
    col = np.arange(GRID_W)
    dc = np.clip(col[None, :] - col[:, None] + NA_KW - 1, 0, n_dc - 1)
    onehot = jnp.asarray(dc[None] == np.arange(n_dc)[:, None, None], F32)
    blocks = jnp.einsum('hdc,cqk->hdqk', rpb.astype(F32), onehot, precision=lax.Precision.HIGHEST)
    cstart = np.clip(col - NA_KW // 2, 0, GRID_W - NA_KW)
    col_ok = (col[None, :] >= cstart[:, None]) & (col[None, :] < cstart[:, None] + NA_KW)
    blocks = jnp.where(col_ok[None, None], blocks, NEG_INF)
    blocks = jnp.concatenate([blocks, jnp.full((NA_HEADS, 1, GRID_W, GRID_W), NEG_INF, F32)], axis=1)
    zero = jnp.zeros_like(blocks)
    return jnp.stack([jnp.concatenate([blocks, zero], axis=-1), jnp.concatenate([zero, blocks], axis=-1)])


def _merge_kernel(*refs, with_router):
    if with_router:
        (oa_ref, ob_ref, oc_ref, od_ref, gate_ref, x_ref, mod_ref, g2_ref, wb_ref, wo_ref, rt_ref,
         xo_ref, h2_ref, comb_ref) = refs
    else:
        (oa_ref, ob_ref, oc_ref, od_ref, gate_ref, x_ref, mod_ref, g2_ref, wb_ref, wo_ref,
         xo_ref, h2_ref) = refs
    m = mod_ref[0]
    acc = None
    for n, o_ref in enumerate((oa_ref, ob_ref, oc_ref, od_ref)):
        y = _dot(o_ref[...], wb_ref[n])
        g = _sigmoid(gate_ref[:, n * D_MODEL:(n + 1) * D_MODEL].astype(F32))
        acc = g * y if acc is None else acc + g * y
    mix = _dot(acc.astype(BF16), wo_ref[...])
    x = x_ref[...] + m[2:3] * mix
    xo_ref[...] = x
    h2 = _norm_mod(x, g2_ref[...], m[3:4], m[4:5])
    h2_ref[...] = h2.astype(BF16)
    if with_router:
        hh, hm, _ = _split3(h2)
        rh, rm = rt_ref[0], rt_ref[1]
        logits = _dot(hh, rh) + (_dot(hh, rm) + _dot(hm, rh))
        lane = lax.broadcasted_iota(jnp.int32, logits.shape, 1).astype(F32)
        logits = jnp.where(lane < N_EXPERTS, logits, NEG_INF)
        m1 = jnp.max(logits, axis=-1, keepdims=True)
        i1 = jnp.min(jnp.where(logits == m1, lane, float(LANES)), axis=-1, keepdims=True)
        rest = jnp.where(lane == i1, NEG_INF, logits)
        m2 = jnp.max(rest, axis=-1, keepdims=True)
        i2 = jnp.min(jnp.where(rest == m2, lane, float(LANES)), axis=-1, keepdims=True)
        e = jnp.exp(m2 - m1)
        w1 = 1.0 / (1.0 + e)
        w2 = e / (1.0 + e)
        comb_ref[...] = (jnp.where(lane == 0.0, i1, 0.0) + jnp.where(lane == 1.0, i2, 0.0)
                         + jnp.where(lane == 2.0, w1, 0.0) + jnp.where(lane == 3.0, w2, 0.0))


def _merge_call(outs, gate, x, mod_l, g2, wb, wo, router3, first_tile):
    t = x.shape[0]
    n_rows = t - first_tile * ROW_TILE

    def rows(width, col=0):
        return pl.BlockSpec((ROW_TILE, width), lambda i: (i + first_tile, col // width))

    def orow(width):
        return pl.BlockSpec((ROW_TILE, width), lambda i: (i, 0))

    def full(shape):
        return pl.BlockSpec(shape, lambda i: (0,) * len(shape))

    in_specs = [orow(256)] * 4 + [
        rows(GATE_W),
        rows(D_MODEL),
        pl.BlockSpec((1, ADA_CHUNKS, D_MODEL), lambda i: (_mod_row(i, first_tile), 0, 0)),
        full((1, D_MODEL)), full((N_BRANCH, BRANCH_W, D_MODEL)), full((D_MODEL, D_MODEL)),
    ]
    args = list(outs) + [gate, x, mod_l, g2, wb, wo]
    out_specs = [orow(D_MODEL), orow(D_MODEL)]
    out_shape = [jax.ShapeDtypeStruct((n_rows, D_MODEL), F32), jax.ShapeDtypeStruct((n_rows, D_MODEL), BF16)]
    if router3 is not None:
        in_specs.append(full((2, D_MODEL, LANES)))
        args.append(router3)
        out_specs.append(orow(LANES))
        out_shape.append(jax.ShapeDtypeStruct((n_rows, LANES), F32))
    return pl.pallas_call(
        functools.partial(_merge_kernel, with_router=router3 is not None),
        grid=(n_rows // ROW_TILE,),
        in_specs=in_specs,
        out_specs=out_specs,
        out_shape=out_shape,
        compiler_params=_params("arbitrary"),
        name="merge",
    )(*args)


FFN_ROWS = 1024


def _swiglu_chunk(h, w1, w3, w2):
    a = _dot(h, w1)
    b = _dot(h, w3)
    return _dot((a * _sigmoid(a) * b).astype(BF16), w2)


def _residual_out(x, gate, update, final_norm, final_g):
    x = x + gate * update
    if final_norm:
        ms = jnp.mean(x * x, axis=-1, keepdims=True)
        x = x * lax.rsqrt(ms + NORM_EPS) * final_g
    return x


def _ffn_kernel(h_ref, w1_ref, w3_ref, w2_ref, x_ref, mod_ref, fg_ref, o_ref, acc_ref, *, final_norm):
    f = pl.program_id(1)

    @pl.when(f == 0)
    def _():
        acc_ref[...] = jnp.zeros_like(acc_ref)

    acc_ref[...] += _swiglu_chunk(h_ref[...], w1_ref[0].astype(BF16), w3_ref[0].astype(BF16),
                                  w2_ref[0].astype(BF16))

    @pl.when(f == pl.num_programs(1) - 1)
    def _():
        o_ref[...] = _residual_out(x_ref[...], mod_ref[0][5:6], acc_ref[...], final_norm, fg_ref[...])


def _tile_mod_row(i, first_tile, tile_rows):
    n_ctx = CTX_ROWS // tile_rows
    t = i + first_tile
    return jnp.where(t < n_ctx, 8, (t - n_ctx) // (SEQ // tile_rows))


def _ffn_call(h2, w1, w3, w2, layer, x, mod_l, final_g, first_row_tile, final_norm):
    t = x.shape[0]
    rows = lambda width: pl.BlockSpec((FFN_ROWS, width), lambda i, f: (i, 0))
    return pl.pallas_call(
        functools.partial(_ffn_kernel, final_norm=final_norm),
        grid=(t // FFN_ROWS, D_FF // FF_CHUNK),
        in_specs=[
            rows(D_MODEL),
            pl.BlockSpec((1, D_MODEL, FF_CHUNK), lambda i, f: (layer, 0, f)),
            pl.BlockSpec((1, D_MODEL, FF_CHUNK), lambda i, f: (layer, 0, f)),
            pl.BlockSpec((1, FF_CHUNK, D_MODEL), lambda i, f: (layer, f, 0)),
            rows(D_MODEL),
            pl.BlockSpec((1, ADA_CHUNKS, D_MODEL),
                         lambda i, f: (_tile_mod_row(i, first_row_tile, FFN_ROWS), 0, 0)),
            pl.BlockSpec((1, D_MODEL), lambda i, f: (0, 0)),
        ],
        out_specs=rows(D_MODEL),
        out_shape=jax.ShapeDtypeStruct((t, D_MODEL), F32),
        scratch_shapes=[pltpu.VMEM((FFN_ROWS, D_MODEL), F32)],
        compiler_params=_params("arbitrary", "arbitrary"),
        name="ffn",
    )(h2, w1, w3, w2, x, mod_l, final_g)


MOE_TILE = 512
MOE_SRC = 256
MOE_CMB_WIN = MOE_SRC + 16
MOE_DSP_SMALL = 144
MOE_DSP_LARGE = MOE_SRC + 16
TOP_K = 2


def _moe_route(sel, n_tok):
    n_tiles = TOP_K * n_tok // MOE_TILE + N_EXPERTS
    e = sel[:, 0:TOP_K].astype(jnp.int32)
    flat_e = e.reshape(-1)
    onehot = (flat_e[:, None] == jnp.arange(N_EXPERTS)[None, :]).astype(jnp.int32)
    csum = jnp.cumsum(onehot, axis=0)
    rank = jnp.sum((csum - onehot) * onehot, axis=1)
    counts = csum[-1]
    padded = (counts + MOE_TILE - 1) // MOE_TILE * MOE_TILE
    seg_end = jnp.cumsum(padded)
    seg_start = seg_end - padded
    pos = jnp.sum(onehot * seg_start[None, :], axis=1) + rank
    pos2 = pos.reshape(n_tok, TOP_K)
    tile_ix = jnp.arange(n_tiles)
    tile_expert = jnp.minimum(jnp.sum(tile_ix[:, None] >= (seg_end // MOE_TILE)[None, :], axis=1), N_EXPERTS - 1)
    n_valid = seg_end[-1] // MOE_TILE
    n_src = n_tok // MOE_SRC
    per_chunk = MOE_SRC * TOP_K
    before = jnp.concatenate([jnp.zeros((1, N_EXPERTS), jnp.int32), csum[per_chunk - 1::per_chunk]], axis=0)
    run_start = seg_start[None, :] + before[:-1]
    run_cnt = before[1:] - before[:-1]
    cum = jnp.take(before, tile_expert, axis=1)
    local0 = tile_ix * MOE_TILE - jnp.take(seg_start, tile_expert)
    local1 = jnp.minimum(local0 + MOE_TILE, jnp.take(counts, tile_expert))
    c_lo = jnp.sum(cum[1:] <= local0[None, :], axis=0)
    c_hi = jnp.sum(cum[:-1] < local1[None, :], axis=0) - 1
    win = jnp.minimum(run_start // 16 * 16, n_tiles * MOE_TILE - MOE_CMB_WIN)
    def token_rows(a, dtype):
        rows = jnp.transpose(a.reshape(n_src, MOE_SRC, TOP_K), (0, 2, 1)).astype(dtype)
        return jnp.zeros((n_src, 8, MOE_SRC), dtype).at[:, :TOP_K, :].set(rows)

    i32 = lambda a: a.astype(jnp.int32)
    return {
        "n_tiles": n_tiles, "tile_expert": i32(tile_expert), "n_valid": i32(n_valid).reshape(1),
        "c_lo": i32(c_lo), "c_hi": i32(c_hi), "run_start": i32(run_start.reshape(-1)),
        "run_cnt": i32(run_cnt.reshape(-1)), "win": i32(win.reshape(-1)),
        "run_off": i32((run_start - win).reshape(-1)),
        "pos_rows": token_rows(pos2, jnp.int32), "w_rows": token_rows(sel[:, TOP_K:2 * TOP_K], F32),
        "pos_cols": i32(pos2),
    }


def _dispatch_kernel(clo_ref, chi_ref, te_ref, rs_ref, rc_ref, h_ref, pos_ref, w_ref, o_ref, sw_ref, acc_ref):
    i = pl.program_id(0)
    base = i * MOE_TILE
    expert = te_ref[i]
    acc_ref[...] = jnp.zeros_like(acc_ref)
    sw_ref[...] = jnp.zeros_like(sw_ref)

    c_last = chi_ref[i]

    def item(c):
        run0 = rs_ref[c * N_EXPERTS + expert]
        lo = jnp.maximum(run0, base) - base
        hi = jnp.minimum(run0 + rc_ref[c * N_EXPERTS + expert], base + MOE_TILE) - base
        small = hi - jnp.minimum(lo // 16 * 16, MOE_TILE - MOE_DSP_SMALL) <= MOE_DSP_SMALL
        return lo, hi, small

    def window(c, lo, rows, live):
        pos = pos_ref[c]
        wts = w_ref[c]
        h = h_ref[pl.ds(pl.multiple_of(c * MOE_SRC, MOE_SRC), MOE_SRC), :]
        ws = pl.multiple_of(jnp.minimum(lo // 16 * 16, MOE_TILE - rows), 16)
        slot = base + ws + lax.broadcasted_iota(jnp.int32, (rows, MOE_SRC), 0)
        slot = jnp.where(live, slot, -1)
        hit0 = pos[0:1, :] == slot
        hit1 = pos[1:2, :] == slot
        acc_ref[pl.ds(ws, rows), :] += _dot(jnp.where(hit0 | hit1, 1.0, 0.0).astype(BF16), h)
        weight = jnp.sum(jnp.where(hit0, wts[0:1, :], 0.0) + jnp.where(hit1, wts[1:2, :], 0.0),
                         axis=1, keepdims=True)
        sw_ref[pl.ds(ws, rows), :] += jnp.broadcast_to(weight, (rows, LANES))

    def body(pair, carry):
        c0 = clo_ref[i] + 2 * pair
        c1 = jnp.minimum(c0 + 1, c_last)
        live1 = c0 + 1 <= c_last
        lo0, hi0, small0 = item(c0)
        lo1, hi1, small1 = item(c1)
        both_small = small0 & small1

        @pl.when(both_small)
        def _():
            window(c0, lo0, MOE_DSP_SMALL, hi0 > lo0)
            window(c1, lo1, MOE_DSP_SMALL, live1 & (hi1 > lo1))

        @pl.when(jnp.logical_not(both_small))
        def _():
            pl.when((hi0 > lo0) & small0)(lambda: window(c0, lo0, MOE_DSP_SMALL, True))
            pl.when((hi0 > lo0) & jnp.logical_not(small0))(lambda: window(c0, lo0, MOE_DSP_LARGE, True))
            pl.when(live1 & (hi1 > lo1) & small1)(lambda: window(c1, lo1, MOE_DSP_SMALL, True))
            pl.when(live1 & (hi1 > lo1) & jnp.logical_not(small1))(lambda: window(c1, lo1, MOE_DSP_LARGE, True))
        return carry

    lax.fori_loop(0, (c_last - clo_ref[i] + 2) // 2, body, 0)
    o_ref[...] = acc_ref[...].astype(BF16)


def _dispatch_call(h2, route):
    n_tok = h2.shape[0]
    n_tiles = route["n_tiles"]
    grid_spec = pltpu.PrefetchScalarGridSpec(
        num_scalar_prefetch=5,
        grid=(n_tiles,),
        in_specs=[
            pl.BlockSpec((n_tok, D_MODEL), lambda i, *_: (0, 0), pipeline_mode=pl.Buffered(1)),
            pl.BlockSpec((n_tok // MOE_SRC, 8, MOE_SRC), lambda i, *_: (0, 0, 0), pipeline_mode=pl.Buffered(1)),
            pl.BlockSpec((n_tok // MOE_SRC, 8, MOE_SRC), lambda i, *_: (0, 0, 0), pipeline_mode=pl.Buffered(1)),
        ],
        out_specs=[pl.BlockSpec((MOE_TILE, D_MODEL), lambda i, *_: (i, 0)),
                   pl.BlockSpec((MOE_TILE, LANES), lambda i, *_: (i, 0))],
        scratch_shapes=[pltpu.VMEM((MOE_TILE, D_MODEL), F32)],
    )
    return pl.pallas_call(
        _dispatch_kernel,
        grid_spec=grid_spec,
        out_shape=[jax.ShapeDtypeStruct((n_tiles * MOE_TILE, D_MODEL), BF16),
                   jax.ShapeDtypeStruct((n_tiles * MOE_TILE, LANES), F32)],
        compiler_params=_params("arbitrary"),
        name="moe_dispatch",
    )(route["c_lo"], route["c_hi"], route["tile_expert"], route["run_start"], route["run_cnt"],
      h2, route["pos_rows"], route["w_rows"])


def _expert_kernel(te_ref, nv_ref, x_ref, w1_ref, w3_ref, w2_ref, sw_ref, o_ref, acc_ref):
    i = pl.program_id(0)
    f = pl.program_id(1)
    last = pl.num_programs(1) - 1
    valid = i < nv_ref[0]

    @pl.when(valid & (f == 0))
    def _():
        acc_ref[...] = jnp.zeros_like(acc_ref)

    @pl.when(valid)
    def _():
        acc_ref[...] += _swiglu_chunk(x_ref[...], w1_ref[0, 0], w3_ref[0, 0], w2_ref[0, 0])

    @pl.when(valid & (f == last))
    def _():
        o_ref[...] = (acc_ref[...] * sw_ref[:, 0:1]).astype(BF16)

    @pl.when(jnp.logical_not(valid) & (f == last))
    def _():
        o_ref[...] = jnp.zeros_like(o_ref)


def _expert_call(xs, slot_w, route, w1, w3, w2, layer):
    n_tiles = route["n_tiles"]
    grid_spec = pltpu.PrefetchScalarGridSpec(
        num_scalar_prefetch=2,
        grid=(n_tiles, D_FF // MOE_FF_CHUNK),
        in_specs=[
            pl.BlockSpec((MOE_TILE, D_MODEL), lambda i, f, te, nv: (i, 0)),
            pl.BlockSpec((1, 1, D_MODEL, MOE_FF_CHUNK), lambda i, f, te, nv: (layer, te[i], 0, f)),
            pl.BlockSpec((1, 1, D_MODEL, MOE_FF_CHUNK), lambda i, f, te, nv: (layer, te[i], 0, f)),
            pl.BlockSpec((1, 1, MOE_FF_CHUNK, D_MODEL), lambda i, f, te, nv: (layer, te[i], f, 0)),
            pl.BlockSpec((MOE_TILE, LANES), lambda i, f, te, nv: (i, 0)),
        ],
        out_specs=pl.BlockSpec((MOE_TILE, D_MODEL), lambda i, f, te, nv: (i, 0)),
        scratch_shapes=[pltpu.VMEM((MOE_TILE, D_MODEL), F32)],
    )
    return pl.pallas_call(
        _expert_kernel,
        grid_spec=grid_spec,
        out_shape=jax.ShapeDtypeStruct((n_tiles * MOE_TILE, D_MODEL), BF16),
        compiler_params=_params("arbitrary", "arbitrary"),
        name="moe_experts",
    )(route["tile_expert"], route["n_valid"], xs, w1, w3, w2, slot_w)


def _combine_kernel(win_ref, off_ref, cnt_ref, y_hbm, pos_ref, x_ref, mod_ref, fg_ref, o_ref, buf_ref, sem_ref,
                    *, final_norm):
    c = pl.program_id(0)
    cur = c % 2

    def window_copies(chunk, buf_set):
        return [pltpu.make_async_copy(
            y_hbm.at[pl.ds(pl.multiple_of(win_ref[chunk * N_EXPERTS + e], 16), MOE_CMB_WIN), :],
            buf_ref.at[buf_set, pl.ds(e * MOE_CMB_WIN, MOE_CMB_WIN), :],
            sem_ref.at[buf_set, e]) for e in range(N_EXPERTS)]

    @pl.when(c == 0)
    def _():
        for cp in window_copies(0, 0):
            cp.start()

    @pl.when(c + 1 < pl.num_programs(0))
    def _():
        for cp in window_copies(c + 1, 1 - cur):
            cp.start()

    row = lax.broadcasted_iota(jnp.int32, (1, N_EXPERTS * MOE_CMB_WIN), 1)
    slot = jnp.full((1, N_EXPERTS * MOE_CMB_WIN), -1, jnp.int32)
    for e in range(N_EXPERTS):
        local = row - e * MOE_CMB_WIN
        off = off_ref[c * N_EXPERTS + e]
        inside = (local >= off) & (local < off + cnt_ref[c * N_EXPERTS + e]) & (local < MOE_CMB_WIN)
        slot = jnp.where(inside, win_ref[c * N_EXPERTS + e] + local, slot)
    pos = pos_ref[...]
    hit = (pos[:, 0:1] == slot) | (pos[:, 1:2] == slot)
    onehot = jnp.where(hit, 1.0, 0.0).astype(BF16)
    for cp in window_copies(c, cur):
        cp.wait()
    update = _dot(onehot, buf_ref[cur])
    o_ref[...] = _residual_out(x_ref[...], mod_ref[0][5:6], update, final_norm, fg_ref[...])


def _combine_call(ys, route, x, mod_l, final_g, first_tile, final_norm):
    n_tok = x.shape[0]
    rows = lambda width: pl.BlockSpec((MOE_SRC, width), lambda c, *_: (c, 0))
    grid_spec = pltpu.PrefetchScalarGridSpec(
        num_scalar_prefetch=3,
        grid=(n_tok // MOE_SRC,),
        in_specs=[
            pl.BlockSpec(memory_space=pl.ANY),
            rows(TOP_K),
            rows(D_MODEL),
            pl.BlockSpec((1, ADA_CHUNKS, D_MODEL), lambda c, *_: (_tile_mod_row(c, first_tile, MOE_SRC), 0, 0)),
            pl.BlockSpec((1, D_MODEL), lambda c, *_: (0, 0)),
        ],
        out_specs=rows(D_MODEL),
        scratch_shapes=[pltpu.VMEM((2, N_EXPERTS * MOE_CMB_WIN, D_MODEL), BF16),
                        pltpu.SemaphoreType.DMA((2, N_EXPERTS))],
    )
    return pl.pallas_call(
        functools.partial(_combine_kernel, final_norm=final_norm),
        grid_spec=grid_spec,
        out_shape=jax.ShapeDtypeStruct((n_tok, D_MODEL), F32),
        compiler_params=_params("arbitrary"),
        name="moe_combine",
    )(route["win"], route["run_off"], route["run_cnt"], ys, route["pos_cols"], x, mod_l, final_g)


def _rope_tables():
    t = np.arange(SEQ)

    def angles(rot_dim):
        half = rot_dim // 2
        inv = ROPE_THETA ** (-jnp.arange(0, half, 2, dtype=F32) / half)
        ang = jnp.concatenate([jnp.asarray(t // GRID_W, F32)[:, None] * inv[None, :],
                               jnp.asarray(t % GRID_W, F32)[:, None] * inv[None, :]], axis=-1)
        return jnp.cos(ang), jnp.sin(ang)

    def pad_rows(a, fill):
        return jnp.concatenate([jnp.full((ROW_TILE, a.shape[1]), fill, F32), a], axis=0)

    c, s = angles(HEAD_DIM)
    cos64 = jnp.tile(jnp.concatenate([c, c], axis=-1), (1, 4))
    sin64 = jnp.tile(jnp.concatenate([s, s], axis=-1), (1, 4))
    c, s = angles(MLA_ROPE)
    one = jnp.ones((SEQ, MLA_NOPE), F32)
    zero = jnp.zeros((SEQ, MLA_NOPE), F32)
    tail1 = jnp.ones((SEQ, MLA_PAD - MLA_NOPE - MLA_ROPE), F32)
    tail0 = jnp.zeros((SEQ, MLA_PAD - MLA_NOPE - MLA_ROPE), F32)
    cosm = jnp.tile(jnp.concatenate([one, c, c, tail1], axis=-1), (1, MLA_HEADS))
    sinm = jnp.tile(jnp.concatenate([zero, s, s, tail0], axis=-1), (1, MLA_HEADS))
    blockdiag = jnp.asarray(np.kron(np.eye(4), np.ones((HEAD_DIM, HEAD_DIM))), BF16)
    place = np.zeros((256, MLA_HEADS * MLA_PAD), np.float32)
    for h in range(MLA_HEADS):
        for r in range(MLA_ROPE):
            place[MLA_KV_RANK + r, h * MLA_PAD + MLA_NOPE + r] = 1.0
    return {
        "cos64": pad_rows(cos64, 1.0), "sin64": pad_rows(sin64, 0.0),
        "cosm": pad_rows(cosm, 1.0), "sinm": pad_rows(sinm, 0.0),
        "blockdiag": blockdiag, "pe_place": jnp.asarray(place, BF16),
    }


def _pack_w_in(w):
    n_qkv = COL_DKVA + MLA_KV_RANK + MLA_ROPE
    pad = jnp.zeros(w.shape[:2] + (COL_GATE - n_qkv,), BF16)
    return jnp.concatenate([w[..., :n_qkv].astype(BF16), pad, w[..., n_qkv:].astype(BF16)], axis=-1)


def _pack_mla(wqb, wkvb):
    dqh = MLA_NOPE + MLA_ROPE
    q = wqb.reshape(MLA_Q_RANK, MLA_HEADS, dqh)
    q = jnp.pad(q, ((0, 0), (0, 0), (0, MLA_PAD - dqh))).reshape(MLA_Q_RANK, MLA_HEADS * MLA_PAD)
    kv = wkvb.reshape(MLA_KV_RANK, MLA_HEADS, MLA_NOPE + MLA_V)
    k = jnp.pad(kv[:, :, :MLA_NOPE], ((0, 0), (0, 0), (0, MLA_PAD - MLA_NOPE)))
    k = k.reshape(MLA_KV_RANK, MLA_HEADS * MLA_PAD)
    v = jnp.pad(kv[:, :, MLA_NOPE:], ((0, 0), (0, 0), (0, MLA_PAD - MLA_V))).reshape(MLA_KV_RANK, MLA_HEADS * MLA_PAD)
    return q.astype(BF16), k.astype(BF16), v.astype(BF16)


def _split2_host(w):
    hi = w.astype(BF16)
    return jnp.stack([hi, (w - hi.astype(F32)).astype(BF16)])


def kernel(x, c, ctx, c_ctx, norm1_g, norm2_g, w_ada, b_ada, w_in, na_rpb, gb_qnorm, gb_knorm, wc_sink,
           mla_qnorm, mla_kvnorm, mla_wqb, mla_wkvb, w_branch, w_out, ffn_w1, ffn_w3, ffn_w2,
           moe_router, moe_w1, moe_w3, moe_w2, final_g):
    n_batch = x.shape[0]
    assert x.shape[1:] == (SEQ, D_MODEL) and ctx.shape[1:] == (CTX_LEN, D_MODEL)
    assert n_batch * CTX_LEN <= CTX_ROWS and n_batch <= 8

    ctx_rows = ctx.reshape(n_batch * CTX_LEN, D_MODEL)
    if ctx_rows.shape[0] < CTX_ROWS:
        ctx_rows = jnp.pad(ctx_rows, ((0, CTX_ROWS - ctx_rows.shape[0]), (0, 0)))
    xt = jnp.concatenate([ctx_rows, x.reshape(n_batch * SEQ, D_MODEL)], axis=0)

    cvec = jnp.zeros((16, D_MODEL), F32).at[:n_batch].set(c).at[8].set(c_ctx)
    mod = _mod_call(cvec, w_ada, b_ada).reshape(DEPTH, 16, ADA_CHUNKS, D_MODEL)
    tabs = _rope_tables()
    w_in_p = _pack_w_in(w_in)
    moe_w = tuple(w.astype(BF16) for w in (moe_w1, moe_w3, moe_w2))
    mla_scale = (MLA_NOPE + MLA_ROPE) ** -0.5
    lat_tile0 = CTX_ROWS // ROW_TILE

    for l in range(DEPTH):
        with_ctx = l < DEPTH - 1
        wq, wk, wv = _pack_mla(mla_wqb[l], mla_wkvb[l])
        lw = {
            "gq": jnp.tile(gb_qnorm[l], 4)[None, :], "gk": jnp.tile(gb_knorm[l], 2)[None, :],
            "qn": mla_qnorm[l][None, :], "kvn": mla_kvnorm[l][None, :], "wqb": wq, "wk": wk, "wv": wv,
        }
        pa, cv, gate, qb, kb, qc, kc, qd, kd, vd, vb = _inproj_call(
            xt, mod[l], norm1_g[l][None, :], w_in_p, l, tabs, lw)

        oa = _natten_call(n_batch, with_ctx, pa, _natten_bias_table(na_rpb[l]))
        ob = _dense_attn_call(n_batch, with_ctx, (qb, 256, 0), kb, (vb, GB_KV_HEADS * LANES, 0),
                              n_kv=GB_KV_HEADS, group=GB_Q_HEADS // GB_KV_HEADS, dq=HEAD_DIM, name="attn_global")
        oc = _window_attn_call(n_batch, with_ctx, (qc, 256, 0), (kc, 128, 0), (cv, 128, 0), wc_sink[l])
        od = _dense_attn_call(n_batch, with_ctx, (qd, MLA_HEADS * MLA_PAD, 0), kd, (vd, MLA_HEADS * MLA_PAD, 0),
                              n_kv=MLA_HEADS, group=1, dq=MLA_PAD, scale=mla_scale, name="attn_mla")

        is_moe = l % 2 == 1
        router3 = None
        if is_moe:
            router3 = _split2_host(jnp.pad(moe_router[l // 2], ((0, 0), (0, LANES - N_EXPERTS))))
        first_tile = 0 if with_ctx else lat_tile0
        res = _merge_call((oa, ob, oc, od), gate, xt, mod[l], norm2_g[l][None, :],
                          w_branch[l].astype(BF16), w_out[l].astype(BF16), router3, first_tile)
        x_mid, h2 = res[0], res[1]
        final_norm = l == DEPTH - 1
        if is_moe:
            route = _moe_route(res[2], h2.shape[0])
            xs, slot_w = _dispatch_call(h2, route)
            ys = _expert_call(xs, slot_w, route, *moe_w, l // 2)
            xt = _combine_call(ys, route, x_mid, mod[l], final_g[None, :],
                               0 if with_ctx else CTX_ROWS // MOE_SRC, final_norm)
        else:
            xt = _ffn_call(h2, ffn_w1, ffn_w3, ffn_w2, l // 2, x_mid, mod[l], final_g[None, :],
                           0 if with_ctx else CTX_ROWS // FFN_ROWS, final_norm)

    return xt.reshape(n_batch, SEQ, D_MODEL)
```

```python
import functools

import numpy as np
import jax
import jax.numpy as jnp
from jax import lax
from jax.experimental import pallas as pl
from jax.experimental.pallas import tpu as pltpu

F32 = jnp.float32
BF16 = jnp.bfloat16

D_MODEL = 1024
SEQ = 2048
DEPTH = 4
CTX_LEN = 256
GRID_W = 64
GRID_ROWS = SEQ // GRID_W
HEAD_DIM = 64
ROPE_THETA = 10000.0
NORM_EPS = 1e-6
NEG_INF = -1e30

NA_HEADS = 4
NA_KH = 8
NA_KW = 16
GB_Q_HEADS = 4
GB_KV_HEADS = 2
WC_Q_HEADS = 4
WC_KV_HEADS = 2
WC_WINDOW = 128
MLA_HEADS = 4
MLA_Q_RANK = 256
MLA_KV_RANK = 128
MLA_NOPE = 64
MLA_ROPE = 32
MLA_V = 64
MLA_PAD = 128
N_BRANCH = 4
BRANCH_W = 256
D_FF = 3584
N_EXPERTS = 8
ADA_CHUNKS = 6

VMEM_LIMIT_BYTES = 56 * 1024 * 1024
LANES = 128

CTX_ROWS = 2048
ROW_TILE = 512
Q_TILE = 256
NA_Q_ROWS = Q_TILE // GRID_W
NA_WIN_ROWS = 12
NA_WIN = NA_WIN_ROWS * GRID_W
WC_WIN = Q_TILE + 2 * WC_WINDOW
ATTN_KEY_CHUNK = 512

COL_AQ, COL_AK, COL_AV = 0, 256, 512
COL_BQ, COL_BK, COL_BV = 768, 1024, 1152
COL_CQ, COL_CK, COL_CV = 1280, 1536, 1664
COL_DQA, COL_DKVA, COL_GATE = 1792, 2048, 2304
GATE_W = N_BRANCH * D_MODEL
P_WIDTH = COL_GATE + GATE_W
QKV_CHUNK = COL_GATE // 2
FF_CHUNK = 512
MOE_FF_CHUNK = 1792


def _params(*sem):
    return pltpu.CompilerParams(dimension_semantics=sem, vmem_limit_bytes=VMEM_LIMIT_BYTES)


def _dot(a, b):
    return jnp.dot(a, b, preferred_element_type=F32)


def _dot_nt(a, b):
    return lax.dot_general(a, b, (((1,), (1,)), ((), ())), preferred_element_type=F32)


def _split3(x):
    hi = x.astype(BF16)
    r1 = x - hi.astype(F32)
    mid = r1.astype(BF16)
    lo = (r1 - mid.astype(F32)).astype(BF16)
    return hi, mid, lo


def _sigmoid(x):
    return 0.5 * jnp.tanh(0.5 * x) + 0.5


ADA_TN = 1536


def _mod_kernel(c_ref, w_ref, b_ref, o_ref):
    c = c_ref[...]
    sc = (c * _sigmoid(c)).astype(BF16)
    o_ref[0] = _dot(sc, w_ref[0].astype(BF16)) + b_ref[0]


def _mod_call(cvec, w_ada, b_ada):
    n = ADA_CHUNKS * D_MODEL
    return pl.pallas_call(
        _mod_kernel,
        grid=(DEPTH, n // ADA_TN),
        in_specs=[
            pl.BlockSpec((16, D_MODEL), lambda l, j: (0, 0)),
            pl.BlockSpec((1, D_MODEL, ADA_TN), lambda l, j: (l, 0, j)),
            pl.BlockSpec((1, 1, ADA_TN), lambda l, j: (l, 0, j)),
        ],
        out_specs=pl.BlockSpec((1, 16, ADA_TN), lambda l, j: (l, 0, j)),
        out_shape=jax.ShapeDtypeStruct((DEPTH, 16, n), F32),
        compiler_params=_params("arbitrary", "arbitrary"),
        name="adaln_mod",
    )(cvec, w_ada, b_ada.reshape(DEPTH, 1, n))


def _mod_row(i, first_tile):
    n_ctx = CTX_ROWS // ROW_TILE
    t = i + first_tile
    return jnp.where(t < n_ctx, 8, (t - n_ctx) // (SEQ // ROW_TILE))


def _norm_mod(x, g, shift, scale):
    ms = jnp.mean(x * x, axis=-1, keepdims=True)
    y = x * lax.rsqrt(ms + NORM_EPS) * g
    return y * (1.0 + scale) + shift


def _rope(x, cos, sin, half, first_mask):
    w = x.shape[-1]
    fwd = pltpu.roll(x, w - half, 1)
    bwd = pltpu.roll(x, half, 1)
    rot = jnp.where(first_mask, -fwd, bwd)
    return x * cos + rot * sin


def _head_rms(x, gain, blockdiag):
    hi, mid, lo = _split3(x * x)
    ss = _dot(hi, blockdiag) + _dot(mid, blockdiag) + _dot(lo, blockdiag)
    return x * lax.rsqrt(ss * (1.0 / HEAD_DIM) + NORM_EPS) * gain


def _inproj_kernel(x_ref, mod_ref, g_ref, w_ref,
                   cos64_ref, sin64_ref, cosm_ref, sinm_ref,
                   gq_ref, gk_ref, bd_ref, qn_ref, kvn_ref, wqb_ref, wk_ref, wv_ref, pe_ref,
                   pa_ref, cv_ref, gate_ref, qb_ref, kb_ref, qc_ref, kc_ref, qd_ref, kd_ref, vd_ref, vb_ref):
    m = mod_ref[0]
    hb = _norm_mod(x_ref[...], g_ref[...], m[0:1], m[1:2]).astype(BF16)

    def proj(c0, c1):
        return _dot(hb, w_ref[0, :, c0:c1])

    for c0 in range(0, GATE_W, D_MODEL):
        gate_ref[:, c0:c0 + D_MODEL] = proj(COL_GATE + c0, COL_GATE + c0 + D_MODEL).astype(BF16)
    pa_ref[...] = proj(COL_AQ, COL_BQ).astype(BF16)

    scale = HEAD_DIM ** -0.5
    half = HEAD_DIM // 2
    cos64, sin64 = cos64_ref[...], sin64_ref[...]
    lane = lax.broadcasted_iota(jnp.int32, (1, 256), 1)
    first64 = (lane % HEAD_DIM) < half
    bd = bd_ref[...]
    bc = proj(COL_BQ, COL_DQA)
    bq = _head_rms(bc[:, 0:256], gq_ref[...], bd)
    qb_ref[...] = (_rope(bq, cos64, sin64, half, first64) * scale).astype(BF16)
    bk = _head_rms(bc[:, 256:384], gk_ref[...], bd[:128, :128])
    kb_ref[...] = _rope(bk, cos64[:, :128], sin64[:, :128], half, first64[:, :128]).T.astype(BF16)
    qc_ref[...] = (_rope(bc[:, 512:768], cos64, sin64, half, first64) * scale).astype(BF16)
    kc_ref[...] = _rope(bc[:, 768:896], cos64[:, :128], sin64[:, :128], half, first64[:, :128]).astype(BF16)
    cv_ref[...] = bc[:, 896:1024].astype(BF16)

    cosm, sinm = cosm_ref[...], sinm_ref[...]
    lane_m = lax.broadcasted_iota(jnp.int32, (1, MLA_HEADS * MLA_PAD), 1) % MLA_PAD
    first_m = lane_m < (MLA_NOPE + MLA_ROPE // 2)
    d = proj(COL_DQA, COL_GATE)
    dqa = d[:, 0:MLA_Q_RANK]
    qn = dqa * lax.rsqrt(jnp.mean(dqa * dqa, axis=-1, keepdims=True) + NORM_EPS) * qn_ref[...]
    dq = _dot(qn.astype(BF16), wqb_ref[...])
    qd_ref[...] = _rope(dq, cosm, sinm, MLA_ROPE // 2, first_m).astype(BF16)

    dkva = d[:, MLA_Q_RANK:]
    kvc = dkva[:, :MLA_KV_RANK]
    kvn = (kvc * lax.rsqrt(jnp.mean(kvc * kvc, axis=-1, keepdims=True) + NORM_EPS) * kvn_ref[...]).astype(BF16)
    dk = _dot(kvn, wk_ref[...]) + _dot(dkva.astype(BF16), pe_ref[...])
    kd_ref[...] = _rope(dk, cosm, sinm, MLA_ROPE // 2, first_m).T.astype(BF16)
    ones_half = lane_m >= HEAD_DIM
    vd_ref[...] = jnp.where(ones_half, 1.0, _dot(kvn, wv_ref[...])).astype(BF16)
    bv = bc[:, 384:512]
    vb_ref[:, :LANES] = jnp.where(ones_half[:, :LANES], 1.0, bv).astype(BF16)
    vb_ref[:, LANES:] = jnp.where(ones_half[:, :LANES], 1.0, pltpu.roll(bv, HEAD_DIM, 1)).astype(BF16)


def _rope_row_block(i):
    n_ctx = CTX_ROWS // ROW_TILE
    return jnp.where(i < n_ctx, 0, 1 + (i - n_ctx) % (SEQ // ROW_TILE))


def _inproj_call(x, mod_l, g, w_in_p, layer, tabs, lw):
    t = x.shape[0]
    mw = MLA_HEADS * MLA_PAD

    def full(shape):
        return pl.BlockSpec(shape, lambda i: (0,) * len(shape))

    def rows(width):
        return pl.BlockSpec((ROW_TILE, width), lambda i: (i, 0))

    def tab(width):
        return pl.BlockSpec((ROW_TILE, width), lambda i: (_rope_row_block(i), 0))

    def cols(height):
        return pl.BlockSpec((height, ROW_TILE), lambda i: (0, i))

    outs = [(COL_BQ, False), (128, False), (GATE_W, False),
            (256, False), (128, True), (256, False), (128, False), (mw, False), (mw, True), (mw, False),
            (GB_KV_HEADS * LANES, False)]
    return pl.pallas_call(
        _inproj_kernel,
        grid=(t // ROW_TILE,),
        in_specs=[
            rows(D_MODEL),
            pl.BlockSpec((1, ADA_CHUNKS, D_MODEL), lambda i: (_mod_row(i, 0), 0, 0)),
            full((1, D_MODEL)),
            pl.BlockSpec((1, D_MODEL, P_WIDTH), lambda i: (layer, 0, 0), pipeline_mode=pl.Buffered(1)),
            tab(256), tab(256), tab(mw), tab(mw),
            full((1, 256)), full((1, 128)), full((256, 256)), full((1, MLA_Q_RANK)), full((1, MLA_KV_RANK)),
            full((MLA_Q_RANK, mw)), full((MLA_KV_RANK, mw)), full((MLA_KV_RANK, mw)), full((256, mw)),
        ],
        out_specs=[cols(w) if tr else rows(w) for w, tr in outs],
        out_shape=[jax.ShapeDtypeStruct((w, t) if tr else (t, w), BF16) for w, tr in outs],
        compiler_params=_params("arbitrary"),
        name="inproj",
    )(x, mod_l, g, w_in_p, tabs["cos64"], tabs["sin64"], tabs["cosm"], tabs["sinm"],
      lw["gq"], lw["gk"], tabs["blockdiag"], lw["qn"], lw["kvn"], lw["wqb"], lw["wk"], lw["wv"], tabs["pe_place"])


def _attend(q, segs, sink=None, scale=None):
    chunks = segs
    scores = []
    m = None
    for k, _, bias in chunks:
        s = _dot_nt(q, k)
        if scale is not None:
            s = s * scale
        if bias is not None:
            s = s + bias
        scores.append(s)
        ms = jnp.max(s, axis=-1, keepdims=True)
        m = ms if m is None else jnp.maximum(m, ms)
    if sink is not None:
        m = jnp.maximum(m, sink)
    denom = None
    out = None
    for s, (_, v, _) in zip(scores, chunks):
        p = jnp.exp(s - m)
        ps = jnp.sum(p, axis=-1, keepdims=True)
        pv = _dot(p.astype(BF16), v)
        denom = ps if denom is None else denom + ps
        out = pv if out is None else out + pv
    if sink is not None:
        denom = denom + jnp.exp(sink - m)
    return out / denom


def _gqa_heads(q_ref, o_ref, sink_ref, n_kv, group, dq, dv, scale, seg_fn):
    tq = q_ref.shape[0]
    for h in range(n_kv):
        heads = [h * group + g for g in range(group)]
        q = jnp.concatenate([q_ref[:, a * dq:(a + 1) * dq] for a in heads], axis=0) if group > 1 \
            else q_ref[:, h * dq:(h + 1) * dq]
        sink = None
        if sink_ref is not None:
            sink = jnp.concatenate([jnp.full((tq, 1), sink_ref[a], F32) for a in heads], axis=0)
        o = _attend(q, seg_fn(h), sink=sink, scale=scale)
        for g, a in enumerate(heads):
            o_ref[:, a * dv:(a + 1) * dv] = o[g * tq:(g + 1) * tq].astype(BF16)


def _dense_attn_kernel(q_ref, ktl_ref, vl_ref, ktc_ref, vc_ref, o_ref, *, n_kv, group, dq, scale, ctx_tile):
    tq = q_ref.shape[0]
    hd = HEAD_DIM

    def run(with_latent):
        for h in range(n_kv):
            heads = [h * group + g for g in range(group)]
            q = jnp.concatenate([q_ref[:, a * dq:(a + 1) * dq] for a in heads], axis=0) if group > 1 \
                else q_ref[:, h * dq:(h + 1) * dq]
            segs = [(ktc_ref[h * dq:(h + 1) * dq, :], vc_ref[:, h * LANES:(h + 1) * LANES])]
            if with_latent:
                segs = [(ktl_ref[h * dq:(h + 1) * dq, c0:c0 + ATTN_KEY_CHUNK],
                         vl_ref[c0:c0 + ATTN_KEY_CHUNK, h * LANES:(h + 1) * LANES])
                        for c0 in range(0, SEQ, ATTN_KEY_CHUNK)] + segs
            scores = []
            m = None
            for kt, _ in segs:
                s = _dot(q, kt)
                if scale is not None:
                    s = s * scale
                scores.append(s)
                ms = jnp.max(s, axis=-1, keepdims=True)
                m = ms if m is None else jnp.maximum(m, ms)
            acc = None
            for s, (_, v) in zip(scores, segs):
                pv = _dot(jnp.exp(s - m).astype(BF16), v)
                acc = pv if acc is None else acc + pv
            o = acc * pltpu.roll(1.0 / acc, hd, 1)
            for g, a in enumerate(heads):
                o_ref[:, a * hd:(a + 1) * hd] = o[g * tq:(g + 1) * tq, :hd].astype(BF16)

    if ctx_tile:
        j = pl.program_id(1)
        pl.when(j == 0)(lambda: run(False))
        pl.when(j > 0)(lambda: run(True))
    else:
        run(True)


def _q_row_block(b, j, n_batch, with_ctx):
    per_batch = SEQ // Q_TILE
    lat0 = CTX_ROWS // Q_TILE
    if with_ctx:
        return jnp.where(j == 0, b, lat0 + b * per_batch + j - 1)
    return lat0 + b * per_batch + j


def _attn_specs(n_batch, with_ctx, q, k, v, o_width, order_bj=True):
    def ix(f):
        return (lambda b, j: f(b, j)) if order_bj else (lambda j, b: f(b, j))

    def qspec(width, col):
        return pl.BlockSpec((Q_TILE, width), ix(lambda b, j: (_q_row_block(b, j, n_batch, with_ctx), col // width)))

    def lat(width, col):
        return pl.BlockSpec((SEQ, width), ix(lambda b, j: (CTX_ROWS // SEQ + b, col // width)))

    def ctx(width, col):
        return pl.BlockSpec((CTX_LEN, width), ix(lambda b, j: (b, col // width)))

    in_specs = [qspec(q[1], q[2]), lat(k[1], k[2]), lat(v[1], v[2]), ctx(k[1], k[2]), ctx(v[1], v[2])]
    args = [q[0], k[0], v[0], k[0], v[0]]
    if with_ctx:
        return in_specs, args, qspec(o_width, 0)
    out_spec = pl.BlockSpec((Q_TILE, o_width), ix(lambda b, j: (b * (SEQ // Q_TILE) + j, 0)))
    return in_specs, args, out_spec


def _attn_out_rows(t, with_ctx):
    return t if with_ctx else t - CTX_ROWS


def _dense_attn_call(n_batch, with_ctx, q, kt, v, *, n_kv, group, dq, scale=None, name):
    t = q[0].shape[0]
    in_specs, args, out_spec = _attn_specs(n_batch, with_ctx, q, v, v, 256)
    kt_rows = kt.shape[0]
    in_specs[1] = pl.BlockSpec((kt_rows, SEQ), lambda b, j: (0, CTX_ROWS // SEQ + b))
    in_specs[3] = pl.BlockSpec((kt_rows, CTX_LEN), lambda b, j: (0, b))
    args[1] = args[3] = kt
    kern = functools.partial(_dense_attn_kernel, n_kv=n_kv, group=group, dq=dq, scale=scale, ctx_tile=with_ctx)
    return pl.pallas_call(
        kern,
        grid=(n_batch, SEQ // Q_TILE + (1 if with_ctx else 0)),
        in_specs=in_specs,
        out_specs=out_spec,
        out_shape=jax.ShapeDtypeStruct((_attn_out_rows(t, with_ctx), 256), BF16),
        compiler_params=_params("arbitrary", "arbitrary"),
        name=name,
    )(*args)


def _window_attn_kernel(sink_ref, q_ref, kl_ref, vl_ref, kc_ref, vc_ref, o_ref, *, ctx_tile):
    j = pl.program_id(1)
    hd = HEAD_DIM
    group = WC_Q_HEADS // WC_KV_HEADS

    def ctx_run():
        def segs(h):
            return [(kc_ref[:, h * hd:(h + 1) * hd], vc_ref[:, h * hd:(h + 1) * hd], None)]
        _gqa_heads(q_ref, o_ref, sink_ref, WC_KV_HEADS, group, hd, hd, None, segs)

    def lat_run():
        i = j - 1 if ctx_tile else j
        start = pl.multiple_of(jnp.clip(i * Q_TILE - WC_WINDOW, 0, SEQ - WC_WIN), WC_WINDOW)
        qpos = i * Q_TILE + lax.broadcasted_iota(jnp.int32, (group * Q_TILE, WC_WIN), 0) % Q_TILE
        kpos = start + lax.broadcasted_iota(jnp.int32, (group * Q_TILE, WC_WIN), 1)
        bias = jnp.where(jnp.abs(qpos - kpos) <= WC_WINDOW, 0.0, NEG_INF).astype(F32)

        def segs(h):
            return [(kl_ref[pl.ds(start, WC_WIN), h * hd:(h + 1) * hd],
                     vl_ref[pl.ds(start, WC_WIN), h * hd:(h + 1) * hd], bias),
                    (kc_ref[:, h * hd:(h + 1) * hd], vc_ref[:, h * hd:(h + 1) * hd], None)]
        _gqa_heads(q_ref, o_ref, sink_ref, WC_KV_HEADS, group, hd, hd, None, segs)

    if ctx_tile:
        pl.when(j == 0)(ctx_run)
        pl.when(j > 0)(lat_run)
    else:
        lat_run()


def _window_attn_call(n_batch, with_ctx, q, k, v, sink):
    t = q[0].shape[0]
    in_specs, args, out_spec = _attn_specs(n_batch, with_ctx, q, k, v, 256)
    return pl.pallas_call(
        functools.partial(_window_attn_kernel, ctx_tile=with_ctx),
        grid=(n_batch, SEQ // Q_TILE + (1 if with_ctx else 0)),
        in_specs=[pl.BlockSpec(memory_space=pltpu.SMEM)] + in_specs,
        out_specs=out_spec,
        out_shape=jax.ShapeDtypeStruct((_attn_out_rows(t, with_ctx), 256), BF16),
        compiler_params=_params("arbitrary", "arbitrary"),
        name="attn_window",
    )(sink, *args)


def _natten_kernel(q_ref, kl_ref, vl_ref, kc_ref, vc_ref, tab_ref, o_ref, bias_ref, *, ctx_tile):
    j = pl.program_id(0)
    hd = HEAD_DIM
    scale = HEAD_DIM ** -0.5

    def build_bias(i):
        ws = jnp.clip(NA_Q_ROWS * i - NA_KH // 2, 0, GRID_ROWS - NA_WIN_ROWS)
        for a in range(NA_Q_ROWS):
            r = NA_Q_ROWS * i + a
            krow0 = jnp.clip(r - NA_KH // 2, 0, GRID_ROWS - NA_KH)
            for pair in range(NA_WIN_ROWS // 2):
                idx = []
                for side in range(2):
                    kr = ws + 2 * pair + side
                    in_rows = (kr >= krow0) & (kr < krow0 + NA_KH)
                    idx.append(jnp.where(in_rows, kr - r + NA_KH - 1, NA_NO_ROW))
                for h in range(NA_HEADS):
                    bias_ref[h, a * GRID_W:(a + 1) * GRID_W, pair * LANES:(pair + 1) * LANES] = (
                        tab_ref[0, h, idx[0]] + tab_ref[1, h, idx[1]])

    def ctx_run():
        for h in range(NA_HEADS):
            sl = slice(h * hd, (h + 1) * hd)
            o = _attend(q_ref[:, sl] * scale, [(kc_ref[:, sl], vc_ref[:, sl], None)])
            o_ref[:, sl] = o.astype(BF16)

    def lat_run():
        i = j - 1 if ctx_tile else j
        ws = jnp.clip(NA_Q_ROWS * i - NA_KH // 2, 0, GRID_ROWS - NA_WIN_ROWS)
        start = pl.multiple_of(ws * GRID_W, GRID_W)
        pl.when(pl.program_id(1) == 0)(lambda: build_bias(i))
        for h in range(NA_HEADS):
            sl = slice(h * hd, (h + 1) * hd)
            segs = [(kl_ref[pl.ds(start, NA_WIN), sl], vl_ref[pl.ds(start, NA_WIN), sl], bias_ref[h]),
                    (kc_ref[:, sl], vc_ref[:, sl], None)]
            o_ref[:, sl] = _attend(q_ref[:, sl] * scale, segs).astype(BF16)

    if ctx_tile:
        pl.when(j == 0)(ctx_run)
        pl.when(j > 0)(lat_run)
    else:
        lat_run()


NA_NO_ROW = 2 * NA_KH - 1


def _natten_call(n_batch, with_ctx, p, bias_blocks):
    t = p.shape[0]
    q, k, v = (p, 256, COL_AQ), (p, 256, COL_AK), (p, 256, COL_AV)
    in_specs, args, out_spec = _attn_specs(n_batch, with_ctx, q, k, v, 256, order_bj=False)
    in_specs.append(pl.BlockSpec(bias_blocks.shape, lambda j, b: (0,) * bias_blocks.ndim))
    return pl.pallas_call(
        functools.partial(_natten_kernel, ctx_tile=with_ctx),
        grid=(SEQ // Q_TILE + (1 if with_ctx else 0), n_batch),
        in_specs=in_specs,
        out_specs=out_spec,
        out_shape=jax.ShapeDtypeStruct((_attn_out_rows(t, with_ctx), 256), BF16),
        scratch_shapes=[pltpu.VMEM((NA_HEADS, Q_TILE, NA_WIN), F32)],
        compiler_params=_params("arbitrary", "arbitrary"),
        name="attn_natten",
    )(*args, bias_blocks)


def _natten_bias_table(rpb):
    n_dr, n_dc = 2 * NA_KH - 1, 2 * NA_KW - 1
    col = np.arange(GRID_W)
    dc = np.clip(col[None, :] - col[:, None] + NA_KW - 1, 0, n_dc - 1)
    onehot = jnp.asarray(dc[None] == np.arange(n_dc)[:, None, None], F32)
    blocks = jnp.einsum('hdc,cqk->hdqk', rpb.astype(F32), onehot, precision=lax.Precision.HIGHEST)
    cstart = np.clip(col - NA_KW // 2, 0, GRID_W - NA_KW)
    col_ok = (col[None, :] >= cstart[:, None]) & (col[None, :] < cstart[:, None] + NA_KW)
    blocks = jnp.where(col_ok[None, None], blocks, NEG_INF)
    blocks = jnp.concatenate([blocks, jnp.full((NA_HEADS, 1, GRID_W, GRID_W), NEG_INF, F32)], axis=1)
    zero = jnp.zeros_like(blocks)
    return jnp.stack([jnp.concatenate([blocks, zero], axis=-1), jnp.concatenate([zero, blocks], axis=-1)])


def _merge_kernel(*refs, with_router):
    if with_router:
        (oa_ref, ob_ref, oc_ref, od_ref, gate_ref, x_ref, mod_ref, g2_ref, wb_ref, wo_ref, rt_ref,
         xo_ref, h2_ref, comb_ref) = refs
    else:
        (oa_ref, ob_ref, oc_ref, od_ref, gate_ref, x_ref, mod_ref, g2_ref, wb_ref, wo_ref,
         xo_ref, h2_ref) = refs
    m = mod_ref[0]
    acc = None
    for n, o_ref in enumerate((oa_ref, ob_ref, oc_ref, od_ref)):
        half_y = _dot(o_ref[...], wb_ref[n])
        t = jnp.tanh(gate_ref[:, n * D_MODEL:(n + 1) * D_MODEL].astype(F32)) + 1.0
        acc = t * half_y if acc is None else acc + t * half_y
    mix = _dot(acc.astype(BF16), wo_ref[...])
    x = x_ref[...] + m[2:3] * mix
    xo_ref[...] = x
    h2 = _norm_mod(x, g2_ref[...], m[3:4], m[4:5])
    h2_ref[...] = h2.astype(BF16)
    if with_router:
        hh, hm, _ = _split3(h2)
        rh, rm = rt_ref[0], rt_ref[1]
        logits = _dot(hh, rh) + (_dot(hh, rm) + _dot(hm, rh))
        lane = lax.broadcasted_iota(jnp.int32, logits.shape, 1).astype(F32)
        logits = jnp.where(lane < N_EXPERTS, logits, NEG_INF)
        m1 = jnp.max(logits, axis=-1, keepdims=True)
        i1 = jnp.min(jnp.where(logits == m1, lane, float(LANES)), axis=-1, keepdims=True)
        rest = jnp.where(lane == i1, NEG_INF, logits)
        m2 = jnp.max(rest, axis=-1, keepdims=True)
        i2 = jnp.min(jnp.where(rest == m2, lane, float(LANES)), axis=-1, keepdims=True)
        e = jnp.exp(m2 - m1)
        w1 = 1.0 / (1.0 + e)
        w2 = e / (1.0 + e)
        comb_ref[...] = (jnp.where(lane == 0.0, i1, 0.0) + jnp.where(lane == 1.0, i2, 0.0)
                         + jnp.where(lane == 2.0, w1, 0.0) + jnp.where(lane == 3.0, w2, 0.0))


def _merge_call(outs, gate, x, mod_l, g2, wb, wo, router3, first_tile):
    t = x.shape[0]
    n_rows = t - first_tile * ROW_TILE

    def rows(width, col=0):
        return pl.BlockSpec((ROW_TILE, width), lambda i: (i + first_tile, col // width))

    def orow(width):
        return pl.BlockSpec((ROW_TILE, width), lambda i: (i, 0))

    def full(shape):
        return pl.BlockSpec(shape, lambda i: (0,) * len(shape))

    in_specs = [orow(256)] * 4 + [
        rows(GATE_W),
        rows(D_MODEL),
        pl.BlockSpec((1, ADA_CHUNKS, D_MODEL), lambda i: (_mod_row(i, first_tile), 0, 0)),
        full((1, D_MODEL)), full((N_BRANCH, BRANCH_W, D_MODEL)), full((D_MODEL, D_MODEL)),
    ]
    args = list(outs) + [gate, x, mod_l, g2, wb, wo]
    out_specs = [orow(D_MODEL), orow(D_MODEL)]
    out_shape = [jax.ShapeDtypeStruct((n_rows, D_MODEL), F32), jax.ShapeDtypeStruct((n_rows, D_MODEL), BF16)]
    if router3 is not None:
        in_specs.append(full((2, D_MODEL, LANES)))
        args.append(router3)
        out_specs.append(orow(LANES))
        out_shape.append(jax.ShapeDtypeStruct((n_rows, LANES), F32))
    return pl.pallas_call(
        functools.partial(_merge_kernel, with_router=router3 is not None),
        grid=(n_rows // ROW_TILE,),
        in_specs=in_specs,
        out_specs=out_specs,
        out_shape=out_shape,
        compiler_params=_params("arbitrary"),
        name="merge",
    )(*args)


FFN_ROWS = 1024


def _swiglu_chunk(h, w1, w3, w2):
    half_a = 0.5 * _dot(h, w1)
    b = _dot(h, w3)
    return _dot((half_a * (jnp.tanh(half_a) + 1.0) * b).astype(BF16), w2)


def _residual_out(x, gate, update, final_norm, final_g):
    x = x + gate * update
    if final_norm:
        ms = jnp.mean(x * x, axis=-1, keepdims=True)
        x = x * lax.rsqrt(ms + NORM_EPS) * final_g
    return x


def _ffn_kernel(h_ref, w1_ref, w3_ref, w2_ref, x_ref, mod_ref, fg_ref, o_ref, acc_ref, *, final_norm):
    f = pl.program_id(1)

    @pl.when(f == 0)
    def _():
        acc_ref[...] = jnp.zeros_like(acc_ref)

    acc_ref[...] += _swiglu_chunk(h_ref[...], w1_ref[0].astype(BF16), w3_ref[0].astype(BF16),
                                  w2_ref[0].astype(BF16))

    @pl.when(f == pl.num_programs(1) - 1)
    def _():
        o_ref[...] = _residual_out(x_ref[...], mod_ref[0][5:6], acc_ref[...], final_norm, fg_ref[...])


def _tile_mod_row(i, first_tile, tile_rows):
    n_ctx = CTX_ROWS // tile_rows
    t = i + first_tile
    return jnp.where(t < n_ctx, 8, (t - n_ctx) // (SEQ // tile_rows))


def _ffn_call(h2, w1, w3, w2, layer, x, mod_l, final_g, first_row_tile, final_norm):
    t = x.shape[0]
    rows = lambda width: pl.BlockSpec((FFN_ROWS, width), lambda i, f: (i, 0))
    return pl.pallas_call(
        functools.partial(_ffn_kernel, final_norm=final_norm),
        grid=(t // FFN_ROWS, D_FF // FF_CHUNK),
        in_specs=[
            rows(D_MODEL),
            pl.BlockSpec((1, D_MODEL, FF_CHUNK), lambda i, f: (layer, 0, f)),
            pl.BlockSpec((1, D_MODEL, FF_CHUNK), lambda i, f: (layer, 0, f)),
            pl.BlockSpec((1, FF_CHUNK, D_MODEL), lambda i, f: (layer, f, 0)),
            rows(D_MODEL),
            pl.BlockSpec((1, ADA_CHUNKS, D_MODEL),
                         lambda i, f: (_tile_mod_row(i, first_row_tile, FFN_ROWS), 0, 0)),
            pl.BlockSpec((1, D_MODEL), lambda i, f: (0, 0)),
        ],
        out_specs=rows(D_MODEL),
        out_shape=jax.ShapeDtypeStruct((t, D_MODEL), F32),
        scratch_shapes=[pltpu.VMEM((FFN_ROWS, D_MODEL), F32)],
        compiler_params=_params("arbitrary", "arbitrary"),
        name="ffn",
    )(h2, w1, w3, w2, x, mod_l, final_g)


MOE_TILE = 512
MOE_SRC = 256
MOE_CMB_WIN = MOE_SRC + 16
MOE_CMB_SMALL = 144
MOE_DSP_SMALL = 144
MOE_DSP_LARGE = MOE_SRC + 16
TOP_K = 2


def _moe_route(sel, n_tok):
    n_tiles = TOP_K * n_tok // MOE_TILE + N_EXPERTS
    e = sel[:, 0:TOP_K].astype(jnp.int32)
    flat_e = e.reshape(-1)
    onehot = (flat_e[:, None] == jnp.arange(N_EXPERTS)[None, :]).astype(jnp.int32)
    csum = jnp.cumsum(onehot, axis=0)
    rank = jnp.sum((csum - onehot) * onehot, axis=1)
    counts = csum[-1]
    padded = (counts + MOE_TILE - 1) // MOE_TILE * MOE_TILE
    seg_end = jnp.cumsum(padded)
    seg_start = seg_end - padded
    pos = jnp.sum(onehot * seg_start[None, :], axis=1) + rank
    pos2 = pos.reshape(n_tok, TOP_K)
    tile_ix = jnp.arange(n_tiles)
    tile_expert = jnp.minimum(jnp.sum(tile_ix[:, None] >= (seg_end // MOE_TILE)[None, :], axis=1), N_EXPERTS - 1)
    n_valid = seg_end[-1] // MOE_TILE
    n_src = n_tok // MOE_SRC
    per_chunk = MOE_SRC * TOP_K
    before = jnp.concatenate([jnp.zeros((1, N_EXPERTS), jnp.int32), csum[per_chunk - 1::per_chunk]], axis=0)
    run_start = seg_start[None, :] + before[:-1]
    run_cnt = before[1:] - before[:-1]
    cum = jnp.take(before, tile_expert, axis=1)
    local0 = tile_ix * MOE_TILE - jnp.take(seg_start, tile_expert)
    local1 = jnp.minimum(local0 + MOE_TILE, jnp.take(counts, tile_expert))
    c_lo = jnp.sum(cum[1:] <= local0[None, :], axis=0)
    c_hi = jnp.sum(cum[:-1] < local1[None, :], axis=0) - 1
    win = jnp.minimum(run_start // 16 * 16, n_tiles * MOE_TILE - MOE_CMB_WIN)
    win_s = jnp.minimum(run_start // 16 * 16, n_tiles * MOE_TILE - MOE_CMB_SMALL)
    cmb_small = jnp.all(run_start - win_s + run_cnt <= MOE_CMB_SMALL, axis=1)

    def token_rows(a, dtype):
        rows = jnp.transpose(a.reshape(n_src, MOE_SRC, TOP_K), (0, 2, 1)).astype(dtype)
        return jnp.zeros((n_src, 8, MOE_SRC), dtype).at[:, :TOP_K, :].set(rows)

    i32 = lambda a: a.astype(jnp.int32)
    return {
        "n_tiles": n_tiles, "tile_expert": i32(tile_expert), "n_valid": i32(n_valid).reshape(1),
        "c_lo": i32(c_lo), "c_hi": i32(c_hi), "run_start": i32(run_start.reshape(-1)),
        "run_cnt": i32(run_cnt.reshape(-1)), "win": i32(win.reshape(-1)),
        "run_off": i32((run_start - win).reshape(-1)),
        "win_s": i32(win_s.reshape(-1)), "run_off_s": i32((run_start - win_s).reshape(-1)),
        "cmb_small": i32(cmb_small),
        "pos_rows": token_rows(pos2, jnp.int32), "w_rows": token_rows(sel[:, TOP_K:2 * TOP_K], F32),
        "pos_cols": i32(pos2),
    }


def _dispatch_kernel(clo_ref, chi_ref, te_ref, rs_ref, rc_ref, h_ref, pos_ref, w_ref, o_ref, sw_ref, acc_ref):
    i = pl.program_id(0)
    base = i * MOE_TILE
    expert = te_ref[i]
    acc_ref[...] = jnp.zeros_like(acc_ref)
    sw_ref[...] = jnp.zeros_like(sw_ref)

    c_last = chi_ref[i]

    def item(c):
        run0 = rs_ref[c * N_EXPERTS + expert]
        lo = jnp.maximum(run0, base) - base
        hi = jnp.minimum(run0 + rc_ref[c * N_EXPERTS + expert], base + MOE_TILE) - base
        small = hi - jnp.minimum(lo // 16 * 16, MOE_TILE - MOE_DSP_SMALL) <= MOE_DSP_SMALL
        return lo, hi, small

    def window(c, lo, rows, live):
        pos = pos_ref[c]
        wts = w_ref[c]
        h = h_ref[pl.ds(pl.multiple_of(c * MOE_SRC, MOE_SRC), MOE_SRC), :]
        ws = pl.multiple_of(jnp.minimum(lo // 16 * 16, MOE_TILE - rows), 16)
        slot = base + ws + lax.broadcasted_iota(jnp.int32, (rows, MOE_SRC), 0)
        slot = jnp.where(live, slot, -1)
        hit0 = pos[0:1, :] == slot
        hit1 = pos[1:2, :] == slot
        acc_ref[pl.ds(ws, rows), :] += _dot(jnp.where(hit0 | hit1, 1.0, 0.0).astype(BF16), h)
        weight = jnp.sum(jnp.where(hit0, wts[0:1, :], 0.0) + jnp.where(hit1, wts[1:2, :], 0.0),
                         axis=1, keepdims=True)
        sw_ref[pl.ds(ws, rows), :] += jnp.broadcast_to(weight, (rows, LANES))

    def body(pair, carry):
        c0 = clo_ref[i] + 2 * pair
        c1 = jnp.minimum(c0 + 1, c_last)
        live1 = c0 + 1 <= c_last
        lo0, hi0, small0 = item(c0)
        lo1, hi1, small1 = item(c1)
        both_small = small0 & small1

        @pl.when(both_small)
        def _():
            window(c0, lo0, MOE_DSP_SMALL, hi0 > lo0)
            window(c1, lo1, MOE_DSP_SMALL, live1 & (hi1 > lo1))

        @pl.when(jnp.logical_not(both_small))
        def _():
            pl.when((hi0 > lo0) & small0)(lambda: window(c0, lo0, MOE_DSP_SMALL, True))
            pl.when((hi0 > lo0) & jnp.logical_not(small0))(lambda: window(c0, lo0, MOE_DSP_LARGE, True))
            pl.when(live1 & (hi1 > lo1) & small1)(lambda: window(c1, lo1, MOE_DSP_SMALL, True))
            pl.when(live1 & (hi1 > lo1) & jnp.logical_not(small1))(lambda: window(c1, lo1, MOE_DSP_LARGE, True))
        return carry

    lax.fori_loop(0, (c_last - clo_ref[i] + 2) // 2, body, 0)
    o_ref[...] = acc_ref[...].astype(BF16)


def _dispatch_call(h2, route):
    n_tok = h2.shape[0]
    n_tiles = route["n_tiles"]
    grid_spec = pltpu.PrefetchScalarGridSpec(
        num_scalar_prefetch=5,
        grid=(n_tiles,),
        in_specs=[
            pl.BlockSpec((n_tok, D_MODEL), lambda i, *_: (0, 0), pipeline_mode=pl.Buffered(1)),
            pl.BlockSpec((n_tok // MOE_SRC, 8, MOE_SRC), lambda i, *_: (0, 0, 0), pipeline_mode=pl.Buffered(1)),
            pl.BlockSpec((n_tok // MOE_SRC, 8, MOE_SRC), lambda i, *_: (0, 0, 0), pipeline_mode=pl.Buffered(1)),
        ],
        out_specs=[pl.BlockSpec((MOE_TILE, D_MODEL), lambda i, *_: (i, 0)),
                   pl.BlockSpec((MOE_TILE, LANES), lambda i, *_: (i, 0))],
        scratch_shapes=[pltpu.VMEM((MOE_TILE, D_MODEL), F32)],
    )
    return pl.pallas_call(
        _dispatch_kernel,
        grid_spec=grid_spec,
        out_shape=[jax.ShapeDtypeStruct((n_tiles * MOE_TILE, D_MODEL), BF16),
                   jax.ShapeDtypeStruct((n_tiles * MOE_TILE, LANES), F32)],
        compiler_params=_params("arbitrary"),
        name="moe_dispatch",
    )(route["c_lo"], route["c_hi"], route["tile_expert"], route["run_start"], route["run_cnt"],
      h2, route["pos_rows"], route["w_rows"])


def _expert_kernel(te_ref, nv_ref, x_ref, w1_ref, w3_ref, w2_ref, sw_ref, o_ref, acc_ref):
    i = pl.program_id(0)
    f = pl.program_id(1)
    last = pl.num_programs(1) - 1
    valid = i < nv_ref[0]

    @pl.when(valid & (f == 0))
    def _():
        acc_ref[...] = jnp.zeros_like(acc_ref)

    @pl.when(valid)
    def _():
        acc_ref[...] += _swiglu_chunk(x_ref[...], w1_ref[0, 0], w3_ref[0, 0], w2_ref[0, 0])

    @pl.when(valid & (f == last))
    def _():
        o_ref[...] = (acc_ref[...] * sw_ref[:, 0:1]).astype(BF16)

    @pl.when(jnp.logical_not(valid) & (f == last))
    def _():
        o_ref[...] = jnp.zeros_like(o_ref)


def _expert_call(xs, slot_w, route, w1, w3, w2, layer):
    n_tiles = route["n_tiles"]
    grid_spec = pltpu.PrefetchScalarGridSpec(
        num_scalar_prefetch=2,
        grid=(n_tiles, D_FF // MOE_FF_CHUNK),
        in_specs=[
            pl.BlockSpec((MOE_TILE, D_MODEL), lambda i, f, te, nv: (i, 0)),
            pl.BlockSpec((1, 1, D_MODEL, MOE_FF_CHUNK), lambda i, f, te, nv: (layer, te[i], 0, f)),
            pl.BlockSpec((1, 1, D_MODEL, MOE_FF_CHUNK), lambda i, f, te, nv: (layer, te[i], 0, f)),
            pl.BlockSpec((1, 1, MOE_FF_CHUNK, D_MODEL), lambda i, f, te, nv: (layer, te[i], f, 0)),
            pl.BlockSpec((MOE_TILE, LANES), lambda i, f, te, nv: (i, 0)),
        ],
        out_specs=pl.BlockSpec((MOE_TILE, D_MODEL), lambda i, f, te, nv: (i, 0)),
        scratch_shapes=[pltpu.VMEM((MOE_TILE, D_MODEL), F32)],
    )
    return pl.pallas_call(
        _expert_kernel,
        grid_spec=grid_spec,
        out_shape=jax.ShapeDtypeStruct((n_tiles * MOE_TILE, D_MODEL), BF16),
        compiler_params=_params("arbitrary", "arbitrary"),
        name="moe_experts",
    )(route["tile_expert"], route["n_valid"], xs, w1, w3, w2, slot_w)


def _combine_kernel(small_ref, cnt_ref, winl_ref, offl_ref, wins_ref, offs_ref,
                    y_hbm, pos_ref, x_ref, mod_ref, fg_ref, o_ref, buf_ref, sem_ref, *, final_norm):
    c = pl.program_id(0)
    cur = c % 2
    variants = ((MOE_CMB_SMALL, wins_ref, offs_ref), (MOE_CMB_WIN, winl_ref, offl_ref))

    def window_copies(chunk, buf_set, rows, win_ref):
        return [pltpu.make_async_copy(
            y_hbm.at[pl.ds(pl.multiple_of(win_ref[chunk * N_EXPERTS + e], 16), rows), :],
            buf_ref.at[buf_set, pl.ds(e * rows, rows), :],
            sem_ref.at[buf_set, e]) for e in range(N_EXPERTS)]

    def start(chunk, buf_set):
        for use, (rows, win_ref, _) in zip((small_ref[chunk] > 0, small_ref[chunk] == 0), variants):
            @pl.when(use)
            def _():
                for cp in window_copies(chunk, buf_set, rows, win_ref):
                    cp.start()

    pl.when(c == 0)(lambda: start(0, 0))
    pl.when(c + 1 < pl.num_programs(0))(lambda: start(jnp.minimum(c + 1, pl.num_programs(0) - 1), 1 - cur))

    def process(rows, win_ref, off_ref):
        row = lax.broadcasted_iota(jnp.int32, (1, N_EXPERTS * rows), 1)
        slot = jnp.full((1, N_EXPERTS * rows), -1, jnp.int32)
        for e in range(N_EXPERTS):
            local = row - e * rows
            off = off_ref[c * N_EXPERTS + e]
            inside = (local >= off) & (local < off + cnt_ref[c * N_EXPERTS + e]) & (local < rows)
            slot = jnp.where(inside, win_ref[c * N_EXPERTS + e] + local, slot)
        pos = pos_ref[...]
        hit = (pos[:, 0:1] == slot) | (pos[:, 1:2] == slot)
        onehot = jnp.where(hit, 1.0, 0.0).astype(BF16)
        for cp in window_copies(c, cur, rows, win_ref):
            cp.wait()
        update = _dot(onehot, buf_ref[cur, 0:N_EXPERTS * rows, :])
        o_ref[...] = _residual_out(x_ref[...], mod_ref[0][5:6], update, final_norm, fg_ref[...])

    for use, variant in zip((small_ref[c] > 0, small_ref[c] == 0), variants):
        pl.when(use)(functools.partial(process, *variant))


def _combine_call(ys, route, x, mod_l, final_g, first_tile, final_norm):
    n_tok = x.shape[0]
    rows = lambda width: pl.BlockSpec((MOE_SRC, width), lambda c, *_: (c, 0))
    grid_spec = pltpu.PrefetchScalarGridSpec(
        num_scalar_prefetch=6,
        grid=(n_tok // MOE_SRC,),
        in_specs=[
            pl.BlockSpec(memory_space=pl.ANY),
            rows(TOP_K),
            rows(D_MODEL),
            pl.BlockSpec((1, ADA_CHUNKS, D_MODEL), lambda c, *_: (_tile_mod_row(c, first_tile, MOE_SRC), 0, 0)),
            pl.BlockSpec((1, D_MODEL), lambda c, *_: (0, 0)),
        ],
        out_specs=rows(D_MODEL),
        scratch_shapes=[pltpu.VMEM((2, N_EXPERTS * MOE_CMB_WIN, D_MODEL), BF16),
                        pltpu.SemaphoreType.DMA((2, N_EXPERTS))],
    )
    return pl.pallas_call(
        functools.partial(_combine_kernel, final_norm=final_norm),
        grid_spec=grid_spec,
        out_shape=jax.ShapeDtypeStruct((n_tok, D_MODEL), F32),
        compiler_params=_params("arbitrary"),
        name="moe_combine",
    )(route["cmb_small"], route["run_cnt"], route["win"], route["run_off"], route["win_s"], route["run_off_s"],
      ys, route["pos_cols"], x, mod_l, final_g)


def _rope_tables():
    t = np.arange(SEQ)

    def angles(rot_dim):
        half = rot_dim // 2
        inv = ROPE_THETA ** (-jnp.arange(0, half, 2, dtype=F32) / half)
        ang = jnp.concatenate([jnp.asarray(t // GRID_W, F32)[:, None] * inv[None, :],
                               jnp.asarray(t % GRID_W, F32)[:, None] * inv[None, :]], axis=-1)
        return jnp.cos(ang), jnp.sin(ang)

    def pad_rows(a, fill):
        return jnp.concatenate([jnp.full((ROW_TILE, a.shape[1]), fill, F32), a], axis=0)

    c, s = angles(HEAD_DIM)
    cos64 = jnp.tile(jnp.concatenate([c, c], axis=-1), (1, 4))
    sin64 = jnp.tile(jnp.concatenate([s, s], axis=-1), (1, 4))
    c, s = angles(MLA_ROPE)
    one = jnp.ones((SEQ, MLA_NOPE), F32)
    zero = jnp.zeros((SEQ, MLA_NOPE), F32)
    tail1 = jnp.ones((SEQ, MLA_PAD - MLA_NOPE - MLA_ROPE), F32)
    tail0 = jnp.zeros((SEQ, MLA_PAD - MLA_NOPE - MLA_ROPE), F32)
    cosm = jnp.tile(jnp.concatenate([one, c, c, tail1], axis=-1), (1, MLA_HEADS))
    sinm = jnp.tile(jnp.concatenate([zero, s, s, tail0], axis=-1), (1, MLA_HEADS))
    blockdiag = jnp.asarray(np.kron(np.eye(4), np.ones((HEAD_DIM, HEAD_DIM))), BF16)
    place = np.zeros((256, MLA_HEADS * MLA_PAD), np.float32)
    for h in range(MLA_HEADS):
        for r in range(MLA_ROPE):
            place[MLA_KV_RANK + r, h * MLA_PAD + MLA_NOPE + r] = 1.0
    return {
        "cos64": pad_rows(cos64, 1.0), "sin64": pad_rows(sin64, 0.0),
        "cosm": pad_rows(cosm, 1.0), "sinm": pad_rows(sinm, 0.0),
        "blockdiag": blockdiag, "pe_place": jnp.asarray(place, BF16),
    }


def _pack_w_in(w):
    n_qkv = COL_DKVA + MLA_KV_RANK + MLA_ROPE
    pad = jnp.zeros(w.shape[:2] + (COL_GATE - n_qkv,), BF16)
    return jnp.concatenate([w[..., :n_qkv].astype(BF16), pad, (0.5 * w[..., n_qkv:]).astype(BF16)], axis=-1)


def _pack_mla(wqb, wkvb):
    dqh = MLA_NOPE + MLA_ROPE
    q = wqb.reshape(MLA_Q_RANK, MLA_HEADS, dqh)
    q = jnp.pad(q, ((0, 0), (0, 0), (0, MLA_PAD - dqh))).reshape(MLA_Q_RANK, MLA_HEADS * MLA_PAD)
    kv = wkvb.reshape(MLA_KV_RANK, MLA_HEADS, MLA_NOPE + MLA_V)
    k = jnp.pad(kv[:, :, :MLA_NOPE], ((0, 0), (0, 0), (0, MLA_PAD - MLA_NOPE)))
    k = k.reshape(MLA_KV_RANK, MLA_HEADS * MLA_PAD)
    v = jnp.pad(kv[:, :, MLA_NOPE:], ((0, 0), (0, 0), (0, MLA_PAD - MLA_V))).reshape(MLA_KV_RANK, MLA_HEADS * MLA_PAD)
    return q.astype(BF16), k.astype(BF16), v.astype(BF16)


def _split2_host(w):
    hi = w.astype(BF16)
    return jnp.stack([hi, (w - hi.astype(F32)).astype(BF16)])


def kernel(x, c, ctx, c_ctx, norm1_g, norm2_g, w_ada, b_ada, w_in, na_rpb, gb_qnorm, gb_knorm, wc_sink,
           mla_qnorm, mla_kvnorm, mla_wqb, mla_wkvb, w_branch, w_out, ffn_w1, ffn_w3, ffn_w2,
           moe_router, moe_w1, moe_w3, moe_w2, final_g):
    n_batch = x.shape[0]
    assert x.shape[1:] == (SEQ, D_MODEL) and ctx.shape[1:] == (CTX_LEN, D_MODEL)
    assert n_batch * CTX_LEN <= CTX_ROWS and n_batch <= 8

    ctx_rows = ctx.reshape(n_batch * CTX_LEN, D_MODEL)
    if ctx_rows.shape[0] < CTX_ROWS:
        ctx_rows = jnp.pad(ctx_rows, ((0, CTX_ROWS - ctx_rows.shape[0]), (0, 0)))
    xt = jnp.concatenate([ctx_rows, x.reshape(n_batch * SEQ, D_MODEL)], axis=0)

    cvec = jnp.zeros((16, D_MODEL), F32).at[:n_batch].set(c).at[8].set(c_ctx)
    mod = _mod_call(cvec, w_ada, b_ada).reshape(DEPTH, 16, ADA_CHUNKS, D_MODEL)
    tabs = _rope_tables()
    w_in_p = _pack_w_in(w_in)
    moe_w = tuple(w.astype(BF16) for w in (moe_w1, moe_w3, moe_w2))
    mla_scale = (MLA_NOPE + MLA_ROPE) ** -0.5
    lat_tile0 = CTX_ROWS // ROW_TILE

    for l in range(DEPTH):
        with_ctx = l < DEPTH - 1
        wq, wk, wv = _pack_mla(mla_wqb[l], mla_wkvb[l])
        lw = {
            "gq": jnp.tile(gb_qnorm[l], 4)[None, :], "gk": jnp.tile(gb_knorm[l], 2)[None, :],
            "qn": mla_qnorm[l][None, :], "kvn": mla_kvnorm[l][None, :], "wqb": wq, "wk": wk, "wv": wv,
        }
        pa, cv, gate, qb, kb, qc, kc, qd, kd, vd, vb = _inproj_call(
            xt, mod[l], norm1_g[l][None, :], w_in_p, l, tabs, lw)

        oa = _natten_call(n_batch, with_ctx, pa, _natten_bias_table(na_rpb[l]))
        ob = _dense_attn_call(n_batch, with_ctx, (qb, 256, 0), kb, (vb, GB_KV_HEADS * LANES, 0),
                              n_kv=GB_KV_HEADS, group=GB_Q_HEADS // GB_KV_HEADS, dq=HEAD_DIM, name="attn_global")
        oc = _window_attn_call(n_batch, with_ctx, (qc, 256, 0), (kc, 128, 0), (cv, 128, 0), wc_sink[l])
        od = _dense_attn_call(n_batch, with_ctx, (qd, MLA_HEADS * MLA_PAD, 0), kd, (vd, MLA_HEADS * MLA_PAD, 0),
                              n_kv=MLA_HEADS, group=1, dq=MLA_PAD, scale=mla_scale, name="attn_mla")

        is_moe = l % 2 == 1
        router3 = None
        if is_moe:
            router3 = _split2_host(jnp.pad(moe_router[l // 2], ((0, 0), (0, LANES - N_EXPERTS))))
        first_tile = 0 if with_ctx else lat_tile0
        res = _merge_call((oa, ob, oc, od), gate, xt, mod[l], norm2_g[l][None, :],
                          (0.5 * w_branch[l]).astype(BF16), w_out[l].astype(BF16), router3, first_tile)
        x_mid, h2 = res[0], res[1]
        final_norm = l == DEPTH - 1
        if is_moe:
            route = _moe_route(res[2], h2.shape[0])
            xs, slot_w = _dispatch_call(h2, route)
            ys = _expert_call(xs, slot_w, route, *moe_w, l // 2)
            xt = _combine_call(ys, route, x_mid, mod[l], final_g[None, :],
                               0 if with_ctx else CTX_ROWS // MOE_SRC, final_norm)
        else:
            xt = _ffn_call(h2, ffn_w1, ffn_w3, ffn_w2, l // 2, x_mid, mod[l], final_g[None, :],
                           0 if with_ctx else CTX_ROWS // FFN_ROWS, final_norm)

    return xt.reshape(n_batch, SEQ, D_MODEL)
```

```python
import functools

import numpy as np
import jax
import jax.numpy as jnp
from jax import lax
from jax.experimental import pallas as pl
from jax.experimental.pallas import tpu as pltpu

F32 = jnp.float32
BF16 = jnp.bfloat16

D_MODEL = 1024
SEQ = 2048
DEPTH = 4
CTX_LEN = 256
GRID_W = 64
GRID_ROWS = SEQ // GRID_W
HEAD_DIM = 64
ROPE_THETA = 10000.0
NORM_EPS = 1e-6
NEG_INF = -1e30

NA_HEADS = 4
NA_KH = 8
NA_KW = 16
GB_Q_HEADS = 4
GB_KV_HEADS = 2
WC_Q_HEADS = 4
WC_KV_HEADS = 2
WC_WINDOW = 128
MLA_HEADS = 4
MLA_Q_RANK = 256
MLA_KV_RANK = 128
MLA_NOPE = 64
MLA_ROPE = 32
MLA_V = 64
MLA_PAD = 128
N_BRANCH = 4
BRANCH_W = 256
D_FF = 3584
N_EXPERTS = 8
ADA_CHUNKS = 6

VMEM_LIMIT_BYTES = 56 * 1024 * 1024
LANES = 128

CTX_ROWS = 2048
ROW_TILE = 512
Q_TILE = 256
NA_Q_ROWS = Q_TILE // GRID_W
NA_WIN_ROWS = 12
NA_WIN = NA_WIN_ROWS * GRID_W
WC_WIN = Q_TILE + 2 * WC_WINDOW
ATTN_KEY_CHUNK = 512

COL_AQ, COL_AK, COL_AV = 0, 256, 512
COL_BQ, COL_BK, COL_BV = 768, 1024, 1152
COL_CQ, COL_CK, COL_CV = 1280, 1536, 1664
COL_DQA, COL_DKVA, COL_GATE = 1792, 2048, 2304
GATE_W = N_BRANCH * D_MODEL
P_WIDTH = COL_GATE + GATE_W
QKV_CHUNK = COL_GATE // 2
FF_CHUNK = 512
MOE_FF_CHUNK = 1792


def _params(*sem):
    return pltpu.CompilerParams(dimension_semantics=sem, vmem_limit_bytes=VMEM_LIMIT_BYTES)


def _dot(a, b):
    return jnp.dot(a, b, preferred_element_type=F32)


def _dot_nt(a, b):
    return lax.dot_general(a, b, (((1,), (1,)), ((), ())), preferred_element_type=F32)


def _split3(x):
    hi = x.astype(BF16)
    r1 = x - hi.astype(F32)
    mid = r1.astype(BF16)
    lo = (r1 - mid.astype(F32)).astype(BF16)
    return hi, mid, lo


def _sigmoid(x):
    return 0.5 * jnp.tanh(0.5 * x) + 0.5


ADA_TN = 1536


def _mod_kernel(c_ref, w_ref, b_ref, o_ref):
    c = c_ref[...]
    sc = (c * _sigmoid(c)).astype(BF16)
    o_ref[0] = _dot(sc, w_ref[0].astype(BF16)) + b_ref[0]


def _mod_call(cvec, w_ada, b_ada):
    n = ADA_CHUNKS * D_MODEL
    return pl.pallas_call(
        _mod_kernel,
        grid=(DEPTH, n // ADA_TN),
        in_specs=[
            pl.BlockSpec((16, D_MODEL), lambda l, j: (0, 0)),
            pl.BlockSpec((1, D_MODEL, ADA_TN), lambda l, j: (l, 0, j)),
            pl.BlockSpec((1, 1, ADA_TN), lambda l, j: (l, 0, j)),
        ],
        out_specs=pl.BlockSpec((1, 16, ADA_TN), lambda l, j: (l, 0, j)),
        out_shape=jax.ShapeDtypeStruct((DEPTH, 16, n), F32),
        compiler_params=_params("arbitrary", "arbitrary"),
        name="adaln_mod",
    )(cvec, w_ada, b_ada.reshape(DEPTH, 1, n))


def _mod_row(i, first_tile):
    n_ctx = CTX_ROWS // ROW_TILE
    t = i + first_tile
    return jnp.where(t < n_ctx, 8, (t - n_ctx) // (SEQ // ROW_TILE))


def _norm_mod(x, g, shift, scale):
    ms = jnp.mean(x * x, axis=-1, keepdims=True)
    y = x * lax.rsqrt(ms + NORM_EPS) * g
    return y * (1.0 + scale) + shift


def _rope(x, cos, sin, half, first_mask):
    w = x.shape[-1]
    fwd = pltpu.roll(x, w - half, 1)
    bwd = pltpu.roll(x, half, 1)
    rot = jnp.where(first_mask, -fwd, bwd)
    return x * cos + rot * sin


def _head_rms(x, gain, blockdiag):
    hi, mid, lo = _split3(x * x)
    ss = _dot(hi, blockdiag) + _dot(mid, blockdiag) + _dot(lo, blockdiag)
    return x * lax.rsqrt(ss * (1.0 / HEAD_DIM) + NORM_EPS) * gain


def _inproj_kernel(x_ref, mod_ref, g_ref, w_ref,
                   cos64_ref, sin64_ref, cosm_ref, sinm_ref,
                   gq_ref, gk_ref, bd_ref, qn_ref, kvn_ref, wqb_ref, wk_ref, wv_ref, pe_ref,
                   pa_ref, cv_ref, gate_ref, qb_ref, kb_ref, qc_ref, kc_ref, qd_ref, kd_ref, vd_ref, vb_ref):
    m = mod_ref[0]
    hb = _norm_mod(x_ref[...], g_ref[...], m[0:1], m[1:2]).astype(BF16)

    def proj(c0, c1):
        return _dot(hb, w_ref[0, :, c0:c1])

    for c0 in range(0, GATE_W, D_MODEL):
        gate_ref[:, c0:c0 + D_MODEL] = proj(COL_GATE + c0, COL_GATE + c0 + D_MODEL).astype(BF16)
    pa_ref[...] = proj(COL_AQ, COL_BQ).astype(BF16)

    scale = HEAD_DIM ** -0.5
    half = HEAD_DIM // 2
    cos64, sin64 = cos64_ref[...], sin64_ref[...]
    lane = lax.broadcasted_iota(jnp.int32, (1, 256), 1)
    first64 = (lane % HEAD_DIM) < half
    bd = bd_ref[...]
    bc = proj(COL_BQ, COL_DQA)
    bq = _head_rms(bc[:, 0:256], gq_ref[...], bd)
    qb_ref[...] = (_rope(bq, cos64, sin64, half, first64) * scale).astype(BF16)
    bk = _head_rms(bc[:, 256:384], gk_ref[...], bd[:128, :128])
    kb_ref[...] = _rope(bk, cos64[:, :128], sin64[:, :128], half, first64[:, :128]).T.astype(BF16)
    qc_ref[...] = (_rope(bc[:, 512:768], cos64, sin64, half, first64) * scale).astype(BF16)
    kc_ref[...] = _rope(bc[:, 768:896], cos64[:, :128], sin64[:, :128], half, first64[:, :128]).astype(BF16)
    cv_ref[...] = bc[:, 896:1024].astype(BF16)

    cosm, sinm = cosm_ref[...], sinm_ref[...]
    lane_m = lax.broadcasted_iota(jnp.int32, (1, MLA_HEADS * MLA_PAD), 1) % MLA_PAD
    first_m = lane_m < (MLA_NOPE + MLA_ROPE // 2)
    d = proj(COL_DQA, COL_GATE)
    dqa = d[:, 0:MLA_Q_RANK]
    qn = dqa * lax.rsqrt(jnp.mean(dqa * dqa, axis=-1, keepdims=True) + NORM_EPS) * qn_ref[...]
    dq = _dot(qn.astype(BF16), wqb_ref[...])
    qd_ref[...] = _rope(dq, cosm, sinm, MLA_ROPE // 2, first_m).astype(BF16)

    dkva = d[:, MLA_Q_RANK:]
    kvc = dkva[:, :MLA_KV_RANK]
    kvn = (kvc * lax.rsqrt(jnp.mean(kvc * kvc, axis=-1, keepdims=True) + NORM_EPS) * kvn_ref[...]).astype(BF16)
    dk = _dot(kvn, wk_ref[...]) + _dot(dkva.astype(BF16), pe_ref[...])
    kd_ref[...] = _rope(dk, cosm, sinm, MLA_ROPE // 2, first_m).T.astype(BF16)
    ones_half = lane_m >= HEAD_DIM
    vd_ref[...] = jnp.where(ones_half, 1.0, _dot(kvn, wv_ref[...])).astype(BF16)
    bv = bc[:, 384:512]
    vb_ref[:, :LANES] = jnp.where(ones_half[:, :LANES], 1.0, bv).astype(BF16)
    vb_ref[:, LANES:] = jnp.where(ones_half[:, :LANES], 1.0, pltpu.roll(bv, HEAD_DIM, 1)).astype(BF16)


def _rope_row_block(i):
    n_ctx = CTX_ROWS // ROW_TILE
    return jnp.where(i < n_ctx, 0, 1 + (i - n_ctx) % (SEQ // ROW_TILE))


def _inproj_call(x, mod_l, g, w_in_p, layer, tabs, lw):
    t = x.shape[0]
    mw = MLA_HEADS * MLA_PAD

    def full(shape):
        return pl.BlockSpec(shape, lambda i: (0,) * len(shape))

    def rows(width):
        return pl.BlockSpec((ROW_TILE, width), lambda i: (i, 0))

    def tab(width):
        return pl.BlockSpec((ROW_TILE, width), lambda i: (_rope_row_block(i), 0))

    def cols(height):
        return pl.BlockSpec((height, ROW_TILE), lambda i: (0, i))

    outs = [(COL_BQ, False), (128, False), (GATE_W, False),
            (256, False), (128, True), (256, False), (128, False), (mw, False), (mw, True), (mw, False),
            (GB_KV_HEADS * LANES, False)]
    return pl.pallas_call(
        _inproj_kernel,
        grid=(t // ROW_TILE,),
        in_specs=[
            rows(D_MODEL),
            pl.BlockSpec((1, ADA_CHUNKS, D_MODEL), lambda i: (_mod_row(i, 0), 0, 0)),
            full((1, D_MODEL)),
            pl.BlockSpec((1, D_MODEL, P_WIDTH), lambda i: (layer, 0, 0), pipeline_mode=pl.Buffered(1)),
            tab(256), tab(256), tab(mw), tab(mw),
            full((1, 256)), full((1, 128)), full((256, 256)), full((1, MLA_Q_RANK)), full((1, MLA_KV_RANK)),
            full((MLA_Q_RANK, mw)), full((MLA_KV_RANK, mw)), full((MLA_KV_RANK, mw)), full((256, mw)),
        ],
        out_specs=[cols(w) if tr else rows(w) for w, tr in outs],
        out_shape=[jax.ShapeDtypeStruct((w, t) if tr else (t, w), BF16) for w, tr in outs],
        compiler_params=_params("arbitrary"),
        name="inproj",
    )(x, mod_l, g, w_in_p, tabs["cos64"], tabs["sin64"], tabs["cosm"], tabs["sinm"],
      lw["gq"], lw["gk"], tabs["blockdiag"], lw["qn"], lw["kvn"], lw["wqb"], lw["wk"], lw["wv"], tabs["pe_place"])


def _attend(q, segs, sink=None, scale=None):
    chunks = segs
    scores = []
    m = None
    for k, _, bias in chunks:
        s = _dot_nt(q, k)
        if scale is not None:
            s = s * scale
        if bias is not None:
            s = s + bias
        scores.append(s)
        ms = jnp.max(s, axis=-1, keepdims=True)
        m = ms if m is None else jnp.maximum(m, ms)
    if sink is not None:
        m = jnp.maximum(m, sink)
    denom = None
    out = None
    for s, (_, v, _) in zip(scores, chunks):
        p = jnp.exp(s - m)
        ps = jnp.sum(p, axis=-1, keepdims=True)
        pv = _dot(p.astype(BF16), v)
        denom = ps if denom is None else denom + ps
        out = pv if out is None else out + pv
    if sink is not None:
        denom = denom + jnp.exp(sink - m)
    return out / denom


def _gqa_heads(q_ref, o_ref, sink_ref, n_kv, group, dq, dv, scale, seg_fn):
    tq = q_ref.shape[0]
    for h in range(n_kv):
        heads = [h * group + g for g in range(group)]
        q = jnp.concatenate([q_ref[:, a * dq:(a + 1) * dq] for a in heads], axis=0) if group > 1 \
            else q_ref[:, h * dq:(h + 1) * dq]
        sink = None
        if sink_ref is not None:
            sink = jnp.concatenate([jnp.full((tq, 1), sink_ref[a], F32) for a in heads], axis=0)
        o = _attend(q, seg_fn(h), sink=sink, scale=scale)
        for g, a in enumerate(heads):
            o_ref[:, a * dv:(a + 1) * dv] = o[g * tq:(g + 1) * tq].astype(BF16)


def _dense_attn_kernel(q_ref, ktl_ref, vl_ref, ktc_ref, vc_ref, o_ref, *, n_kv, group, dq, scale, ctx_tile):
    tq = q_ref.shape[0]
    hd = HEAD_DIM

    def run(with_latent):
        for h in range(n_kv):
            heads = [h * group + g for g in range(group)]
            q = jnp.concatenate([q_ref[:, a * dq:(a + 1) * dq] for a in heads], axis=0) if group > 1 \
                else q_ref[:, h * dq:(h + 1) * dq]
            segs = [(ktc_ref[h * dq:(h + 1) * dq, :], vc_ref[:, h * LANES:(h + 1) * LANES])]
            if with_latent:
                segs = [(ktl_ref[h * dq:(h + 1) * dq, c0:c0 + ATTN_KEY_CHUNK],
                         vl_ref[c0:c0 + ATTN_KEY_CHUNK, h * LANES:(h + 1) * LANES])
                        for c0 in range(0, SEQ, ATTN_KEY_CHUNK)] + segs
            scores = []
            m = None
            for kt, _ in segs:
                s = _dot(q, kt)
                if scale is not None:
                    s = s * scale
                scores.append(s)
                ms = jnp.max(s, axis=-1, keepdims=True)
                m = ms if m is None else jnp.maximum(m, ms)
            acc = None
            for s, (_, v) in zip(scores, segs):
                pv = _dot(jnp.exp(s - m).astype(BF16), v)
                acc = pv if acc is None else acc + pv
            o = acc * pltpu.roll(1.0 / acc, hd, 1)
            for g, a in enumerate(heads):
                o_ref[:, a * hd:(a + 1) * hd] = o[g * tq:(g + 1) * tq, :hd].astype(BF16)

    if ctx_tile:
        j = pl.program_id(1)
        pl.when(j == 0)(lambda: run(False))
        pl.when(j > 0)(lambda: run(True))
    else:
        run(True)


def _q_row_block(b, j, n_batch, with_ctx):
    per_batch = SEQ // Q_TILE
    lat0 = CTX_ROWS // Q_TILE
    if with_ctx:
        return jnp.where(j == 0, b, lat0 + b * per_batch + j - 1)
    return lat0 + b * per_batch + j


def _attn_specs(n_batch, with_ctx, q, k, v, o_width, order_bj=True):
    def ix(f):
        return (lambda b, j: f(b, j)) if order_bj else (lambda j, b: f(b, j))

    def qspec(width, col):
        return pl.BlockSpec((Q_TILE, width), ix(lambda b, j: (_q_row_block(b, j, n_batch, with_ctx), col // width)))

    def lat(width, col):
        return pl.BlockSpec((SEQ, width), ix(lambda b, j: (CTX_ROWS // SEQ + b, col // width)))

    def ctx(width, col):
        return pl.BlockSpec((CTX_LEN, width), ix(lambda b, j: (b, col // width)))

    in_specs = [qspec(q[1], q[2]), lat(k[1], k[2]), lat(v[1], v[2]), ctx(k[1], k[2]), ctx(v[1], v[2])]
    args = [q[0], k[0], v[0], k[0], v[0]]
    if with_ctx:
        return in_specs, args, qspec(o_width, 0)
    out_spec = pl.BlockSpec((Q_TILE, o_width), ix(lambda b, j: (b * (SEQ // Q_TILE) + j, 0)))
    return in_specs, args, out_spec


def _attn_out_rows(t, with_ctx):
    return t if with_ctx else t - CTX_ROWS


def _dense_attn_call(n_batch, with_ctx, q, kt, v, *, n_kv, group, dq, scale=None, name):
    t = q[0].shape[0]
    in_specs, args, out_spec = _attn_specs(n_batch, with_ctx, q, v, v, 256)
    kt_rows = kt.shape[0]
    in_specs[1] = pl.BlockSpec((kt_rows, SEQ), lambda b, j: (0, CTX_ROWS // SEQ + b))
    in_specs[3] = pl.BlockSpec((kt_rows, CTX_LEN), lambda b, j: (0, b))
    args[1] = args[3] = kt
    kern = functools.partial(_dense_attn_kernel, n_kv=n_kv, group=group, dq=dq, scale=scale, ctx_tile=with_ctx)
    return pl.pallas_call(
        kern,
        grid=(n_batch, SEQ // Q_TILE + (1 if with_ctx else 0)),
        in_specs=in_specs,
        out_specs=out_spec,
        out_shape=jax.ShapeDtypeStruct((_attn_out_rows(t, with_ctx), 256), BF16),
        compiler_params=_params("arbitrary", "arbitrary"),
        name=name,
    )(*args)


def _window_attn_kernel(sink_ref, q_ref, kl_ref, vl_ref, kc_ref, vc_ref, o_ref, *, ctx_tile):
    j = pl.program_id(1)
    hd = HEAD_DIM
    group = WC_Q_HEADS // WC_KV_HEADS

    def ctx_run():
        def segs(h):
            return [(kc_ref[:, h * hd:(h + 1) * hd], vc_ref[:, h * hd:(h + 1) * hd], None)]
        _gqa_heads(q_ref, o_ref, sink_ref, WC_KV_HEADS, group, hd, hd, None, segs)

    def lat_run():
        i = j - 1 if ctx_tile else j
        start = pl.multiple_of(jnp.clip(i * Q_TILE - WC_WINDOW, 0, SEQ - WC_WIN), WC_WINDOW)
        qpos = i * Q_TILE + lax.broadcasted_iota(jnp.int32, (group * Q_TILE, WC_WIN), 0) % Q_TILE
        kpos = start + lax.broadcasted_iota(jnp.int32, (group * Q_TILE, WC_WIN), 1)
        bias = jnp.where(jnp.abs(qpos - kpos) <= WC_WINDOW, 0.0, NEG_INF).astype(F32)

        def segs(h):
            return [(kl_ref[pl.ds(start, WC_WIN), h * hd:(h + 1) * hd],
                     vl_ref[pl.ds(start, WC_WIN), h * hd:(h + 1) * hd], bias),
                    (kc_ref[:, h * hd:(h + 1) * hd], vc_ref[:, h * hd:(h + 1) * hd], None)]
        _gqa_heads(q_ref, o_ref, sink_ref, WC_KV_HEADS, group, hd, hd, None, segs)

    if ctx_tile:
        pl.when(j == 0)(ctx_run)
        pl.when(j > 0)(lat_run)
    else:
        lat_run()


def _window_attn_call(n_batch, with_ctx, q, k, v, sink):
    t = q[0].shape[0]
    in_specs, args, out_spec = _attn_specs(n_batch, with_ctx, q, k, v, 256)
    return pl.pallas_call(
        functools.partial(_window_attn_kernel, ctx_tile=with_ctx),
        grid=(n_batch, SEQ // Q_TILE + (1 if with_ctx else 0)),
        in_specs=[pl.BlockSpec(memory_space=pltpu.SMEM)] + in_specs,
        out_specs=out_spec,
        out_shape=jax.ShapeDtypeStruct((_attn_out_rows(t, with_ctx), 256), BF16),
        compiler_params=_params("arbitrary", "arbitrary"),
        name="attn_window",
    )(sink, *args)


def _natten_kernel(q_ref, kl_ref, vl_ref, kc_ref, vc_ref, tab_ref, o_ref, bias_ref, *, ctx_tile):
    j = pl.program_id(0)
    hd = HEAD_DIM
    scale = HEAD_DIM ** -0.5

    def build_bias(i):
        ws = jnp.clip(NA_Q_ROWS * i - NA_KH // 2, 0, GRID_ROWS - NA_WIN_ROWS)
        for a in range(NA_Q_ROWS):
            r = NA_Q_ROWS * i + a
            krow0 = jnp.clip(r - NA_KH // 2, 0, GRID_ROWS - NA_KH)
            for pair in range(NA_WIN_ROWS // 2):
                idx = []
                for side in range(2):
                    kr = ws + 2 * pair + side
                    in_rows = (kr >= krow0) & (kr < krow0 + NA_KH)
                    idx.append(jnp.where(in_rows, kr - r + NA_KH - 1, NA_NO_ROW))
                for h in range(NA_HEADS):
                    bias_ref[h, a * GRID_W:(a + 1) * GRID_W, pair * LANES:(pair + 1) * LANES] = (
                        tab_ref[0, h, idx[0]] + tab_ref[1, h, idx[1]])

    def ctx_run():
        for h in range(NA_HEADS):
            sl = slice(h * hd, (h + 1) * hd)
            o = _attend(q_ref[:, sl] * scale, [(kc_ref[:, sl], vc_ref[:, sl], None)])
            o_ref[:, sl] = o.astype(BF16)

    def lat_run():
        i = j - 1 if ctx_tile else j
        ws = jnp.clip(NA_Q_ROWS * i - NA_KH // 2, 0, GRID_ROWS - NA_WIN_ROWS)
        start = pl.multiple_of(ws * GRID_W, GRID_W)
        pl.when(pl.program_id(1) == 0)(lambda: build_bias(i))
        for h in range(NA_HEADS):
            sl = slice(h * hd, (h + 1) * hd)
            segs = [(kl_ref[pl.ds(start, NA_WIN), sl], vl_ref[pl.ds(start, NA_WIN), sl], bias_ref[h]),
                    (kc_ref[:, sl], vc_ref[:, sl], None)]
            o_ref[:, sl] = _attend(q_ref[:, sl] * scale, segs).astype(BF16)

    if ctx_tile:
        pl.when(j == 0)(ctx_run)
        pl.when(j > 0)(lat_run)
    else:
        lat_run()


NA_NO_ROW = 2 * NA_KH - 1


def _natten_call(n_batch, with_ctx, p, bias_blocks):
    t = p.shape[0]
    q, k, v = (p, 256, COL_AQ), (p, 256, COL_AK), (p, 256, COL_AV)
    in_specs, args, out_spec = _attn_specs(n_batch, with_ctx, q, k, v, 256, order_bj=False)
    in_specs.append(pl.BlockSpec(bias_blocks.shape, lambda j, b: (0,) * bias_blocks.ndim))
    return pl.pallas_call(
        functools.partial(_natten_kernel, ctx_tile=with_ctx),
        grid=(SEQ // Q_TILE + (1 if with_ctx else 0), n_batch),
        in_specs=in_specs,
        out_specs=out_spec,
        out_shape=jax.ShapeDtypeStruct((_attn_out_rows(t, with_ctx), 256), BF16),
        scratch_shapes=[pltpu.VMEM((NA_HEADS, Q_TILE, NA_WIN), F32)],
        compiler_params=_params("arbitrary", "arbitrary"),
        name="attn_natten",
    )(*args, bias_blocks)


def _natten_bias_table(rpb):
    n_dr, n_dc = 2 * NA_KH - 1, 2 * NA_KW - 1
    col = np.arange(GRID_W)
    dc = np.clip(col[None, :] - col[:, None] + NA_KW - 1, 0, n_dc - 1)
    onehot = jnp.asarray(dc[None] == np.arange(n_dc)[:, None, None], F32)
    blocks = jnp.einsum('hdc,cqk->hdqk', rpb.astype(F32), onehot, precision=lax.Precision.HIGHEST)
    cstart = np.clip(col - NA_KW // 2, 0, GRID_W - NA_KW)
    col_ok = (col[None, :] >= cstart[:, None]) & (col[None, :] < cstart[:, None] + NA_KW)
    blocks = jnp.where(col_ok[None, None], blocks, NEG_INF)
    blocks = jnp.concatenate([blocks, jnp.full((NA_HEADS, 1, GRID_W, GRID_W), NEG_INF, F32)], axis=1)
    zero = jnp.zeros_like(blocks)
    return jnp.stack([jnp.concatenate([blocks, zero], axis=-1), jnp.concatenate([zero, blocks], axis=-1)])


def _merge_kernel(*refs, with_router):
    if with_router:
        (oa_ref, ob_ref, oc_ref, od_ref, gate_ref, x_ref, mod_ref, g2_ref, wb_ref, wo_ref, rt_ref,
         xo_ref, h2_ref, comb_ref) = refs
    else:
        (oa_ref, ob_ref, oc_ref, od_ref, gate_ref, x_ref, mod_ref, g2_ref, wb_ref, wo_ref,
         xo_ref, h2_ref) = refs
    m = mod_ref[0]
    acc = None
    for n, o_ref in enumerate((oa_ref, ob_ref, oc_ref, od_ref)):
        half_y = _dot(o_ref[...], wb_ref[n])
        t = jnp.tanh(gate_ref[:, n * D_MODEL:(n + 1) * D_MODEL].astype(F32)) + 1.0
        acc = t * half_y if acc is None else acc + t * half_y
    mix = _dot(acc.astype(BF16), wo_ref[...])
    x = x_ref[...] + m[2:3] * mix
    xo_ref[...] = x
    h2 = _norm_mod(x, g2_ref[...], m[3:4], m[4:5])
    h2_ref[...] = h2.astype(BF16)
    if with_router:
        hh, hm, _ = _split3(h2)
        rh, rm = rt_ref[0], rt_ref[1]
        logits = _dot(hh, rh) + (_dot(hh, rm) + _dot(hm, rh))
        lane = lax.broadcasted_iota(jnp.int32, logits.shape, 1).astype(F32)
        logits = jnp.where(lane < N_EXPERTS, logits, NEG_INF)
        m1 = jnp.max(logits, axis=-1, keepdims=True)
        i1 = jnp.min(jnp.where(logits == m1, lane, float(LANES)), axis=-1, keepdims=True)
        rest = jnp.where(lane == i1, NEG_INF, logits)
        m2 = jnp.max(rest, axis=-1, keepdims=True)
        i2 = jnp.min(jnp.where(rest == m2, lane, float(LANES)), axis=-1, keepdims=True)
        e = jnp.exp(m2 - m1)
        w1 = 1.0 / (1.0 + e)
        w2 = e / (1.0 + e)
        comb_ref[...] = (jnp.where(lane == 0.0, i1, 0.0) + jnp.where(lane == 1.0, i2, 0.0)
                         + jnp.where(lane == 2.0, w1, 0.0) + jnp.where(lane == 3.0, w2, 0.0))


def _merge_call(outs, gate, x, mod_l, g2, wb, wo, router3, first_tile):
    t = x.shape[0]
    n_rows = t - first_tile * ROW_TILE

    def rows(width, col=0):
        return pl.BlockSpec((ROW_TILE, width), lambda i: (i + first_tile, col // width))

    def orow(width):
        return pl.BlockSpec((ROW_TILE, width), lambda i: (i, 0))

    def full(shape):
        return pl.BlockSpec(shape, lambda i: (0,) * len(shape))

    in_specs = [orow(256)] * 4 + [
        rows(GATE_W),
        rows(D_MODEL),
        pl.BlockSpec((1, ADA_CHUNKS, D_MODEL), lambda i: (_mod_row(i, first_tile), 0, 0)),
        full((1, D_MODEL)), full((N_BRANCH, BRANCH_W, D_MODEL)), full((D_MODEL, D_MODEL)),
    ]
    args = list(outs) + [gate, x, mod_l, g2, wb, wo]
    out_specs = [orow(D_MODEL), orow(D_MODEL)]
    out_shape = [jax.ShapeDtypeStruct((n_rows, D_MODEL), F32), jax.ShapeDtypeStruct((n_rows, D_MODEL), BF16)]
    if router3 is not None:
        in_specs.append(full((2, D_MODEL, LANES)))
        args.append(router3)
        out_specs.append(orow(LANES))
        out_shape.append(jax.ShapeDtypeStruct((n_rows, LANES), F32))
    return pl.pallas_call(
        functools.partial(_merge_kernel, with_router=router3 is not None),
        grid=(n_rows // ROW_TILE,),
        in_specs=in_specs,
        out_specs=out_specs,
        out_shape=out_shape,
        compiler_params=_params("arbitrary"),
        name="merge",
    )(*args)


FFN_ROWS = 1024


def _swiglu_chunk(h, w1, w3, w2):
    half_a = 0.5 * _dot(h, w1)
    b = _dot(h, w3)
    return _dot((half_a * (jnp.tanh(half_a) + 1.0) * b).astype(BF16), w2)


def _residual_out(x, gate, update, final_norm, final_g):
    x = x + gate * update
    if final_norm:
        ms = jnp.mean(x * x, axis=-1, keepdims=True)
        x = x * lax.rsqrt(ms + NORM_EPS) * final_g
    return x


def _ffn_kernel(h_ref, w1_ref, w3_ref, w2_ref, x_ref, mod_ref, fg_ref, o_ref, acc_ref, *, final_norm):
    f = pl.program_id(1)

    @pl.when(f == 0)
    def _():
        acc_ref[...] = jnp.zeros_like(acc_ref)

    acc_ref[...] += _swiglu_chunk(h_ref[...], w1_ref[0].astype(BF16), w3_ref[0].astype(BF16),
                                  w2_ref[0].astype(BF16))

    @pl.when(f == pl.num_programs(1) - 1)
    def _():
        o_ref[...] = _residual_out(x_ref[...], mod_ref[0][5:6], acc_ref[...], final_norm, fg_ref[...])


def _tile_mod_row(i, first_tile, tile_rows):
    n_ctx = CTX_ROWS // tile_rows
    t = i + first_tile
    return jnp.where(t < n_ctx, 8, (t - n_ctx) // (SEQ // tile_rows))


def _ffn_call(h2, w1, w3, w2, layer, x, mod_l, final_g, first_row_tile, final_norm):
    t = x.shape[0]
    rows = lambda width: pl.BlockSpec((FFN_ROWS, width), lambda i, f: (i, 0))
    return pl.pallas_call(
        functools.partial(_ffn_kernel, final_norm=final_norm),
        grid=(t // FFN_ROWS, D_FF // FF_CHUNK),
        in_specs=[
            rows(D_MODEL),
            pl.BlockSpec((1, D_MODEL, FF_CHUNK), lambda i, f: (layer, 0, f)),
            pl.BlockSpec((1, D_MODEL, FF_CHUNK), lambda i, f: (layer, 0, f)),
            pl.BlockSpec((1, FF_CHUNK, D_MODEL), lambda i, f: (layer, f, 0)),
            rows(D_MODEL),
            pl.BlockSpec((1, ADA_CHUNKS, D_MODEL),
                         lambda i, f: (_tile_mod_row(i, first_row_tile, FFN_ROWS), 0, 0)),
            pl.BlockSpec((1, D_MODEL), lambda i, f: (0, 0)),
        ],
        out_specs=rows(D_MODEL),
        out_shape=jax.ShapeDtypeStruct((t, D_MODEL), F32),
        scratch_shapes=[pltpu.VMEM((FFN_ROWS, D_MODEL), F32)],
        compiler_params=_params("arbitrary", "arbitrary"),
        name="ffn",
    )(h2, w1, w3, w2, x, mod_l, final_g)


MOE_TILE = 512
MOE_SRC = 256
MOE_CMB_WIN = MOE_SRC + 16
MOE_CMB_SMALL = 144
MOE_DSP_SMALL = 144
MOE_DSP_LARGE = MOE_SRC + 16
TOP_K = 2


def _moe_route(sel, n_tok):
    n_tiles = TOP_K * n_tok // MOE_TILE + N_EXPERTS
    e = sel[:, 0:TOP_K].astype(jnp.int32)
    flat_e = e.reshape(-1)
    onehot = (flat_e[None, :] == jnp.arange(N_EXPERTS)[:, None]).astype(jnp.int32)
    csum = jnp.cumsum(onehot, axis=1)
    counts = csum[:, -1]
    padded = (counts + MOE_TILE - 1) // MOE_TILE * MOE_TILE
    seg_end = jnp.cumsum(padded)
    seg_start = seg_end - padded
    pos = jnp.sum(onehot * (seg_start[:, None] + csum - onehot), axis=0)
    pos2 = pos.reshape(n_tok, TOP_K)
    tile_ix = jnp.arange(n_tiles)
    tile_expert = jnp.minimum(jnp.sum(tile_ix[:, None] >= (seg_end // MOE_TILE)[None, :], axis=1), N_EXPERTS - 1)
    n_valid = seg_end[-1] // MOE_TILE
    n_src = n_tok // MOE_SRC
    per_chunk = MOE_SRC * TOP_K
    before = jnp.concatenate([jnp.zeros((1, N_EXPERTS), jnp.int32), csum[:, per_chunk - 1::per_chunk].T], axis=0)
    run_start = seg_start[None, :] + before[:-1]
    run_cnt = before[1:] - before[:-1]
    cum = jnp.take(before, tile_expert, axis=1)
    local0 = tile_ix * MOE_TILE - jnp.take(seg_start, tile_expert)
    local1 = jnp.minimum(local0 + MOE_TILE, jnp.take(counts, tile_expert))
    c_lo = jnp.sum(cum[1:] <= local0[None, :], axis=0)
    c_hi = jnp.sum(cum[:-1] < local1[None, :], axis=0) - 1
    win = jnp.minimum(run_start // 16 * 16, n_tiles * MOE_TILE - MOE_CMB_WIN)
    win_s = jnp.minimum(run_start // 16 * 16, n_tiles * MOE_TILE - MOE_CMB_SMALL)
    cmb_small = jnp.all(run_start - win_s + run_cnt <= MOE_CMB_SMALL, axis=1)

    def token_rows(a, dtype):
        rows = jnp.transpose(a.reshape(n_src, MOE_SRC, TOP_K), (0, 2, 1)).astype(dtype)
        return jnp.zeros((n_src, 8, MOE_SRC), dtype).at[:, :TOP_K, :].set(rows)

    i32 = lambda a: a.astype(jnp.int32)
    return {
        "n_tiles": n_tiles, "tile_expert": i32(tile_expert), "n_valid": i32(n_valid).reshape(1),
        "c_lo": i32(c_lo), "c_hi": i32(c_hi), "run_start": i32(run_start.reshape(-1)),
        "run_cnt": i32(run_cnt.reshape(-1)), "win": i32(win.reshape(-1)),
        "run_off": i32((run_start - win).reshape(-1)),
        "win_s": i32(win_s.reshape(-1)), "run_off_s": i32((run_start - win_s).reshape(-1)),
        "cmb_small": i32(cmb_small),
        "pos_rows": token_rows(pos2, jnp.int32), "w_rows": token_rows(sel[:, TOP_K:2 * TOP_K], F32),
        "pos_cols": i32(pos2),
    }


def _dispatch_kernel(clo_ref, chi_ref, te_ref, rs_ref, rc_ref, h_ref, pos_ref, w_ref, o_ref, sw_ref, acc_ref):
    i = pl.program_id(0)
    base = i * MOE_TILE
    expert = te_ref[i]
    acc_ref[...] = jnp.zeros_like(acc_ref)
    sw_ref[...] = jnp.zeros_like(sw_ref)

    c_last = chi_ref[i]

    def item(c):
        run0 = rs_ref[c * N_EXPERTS + expert]
        lo = jnp.maximum(run0, base) - base
        hi = jnp.minimum(run0 + rc_ref[c * N_EXPERTS + expert], base + MOE_TILE) - base
        small = hi - jnp.minimum(lo // 16 * 16, MOE_TILE - MOE_DSP_SMALL) <= MOE_DSP_SMALL
        return lo, hi, small

    def window(c, lo, rows, live):
        pos = pos_ref[c]
        wts = w_ref[c]
        h = h_ref[pl.ds(pl.multiple_of(c * MOE_SRC, MOE_SRC), MOE_SRC), :]
        ws = pl.multiple_of(jnp.minimum(lo // 16 * 16, MOE_TILE - rows), 16)
        slot = base + ws + lax.broadcasted_iota(jnp.int32, (rows, MOE_SRC), 0)
        slot = jnp.where(live, slot, -1)
        hit0 = pos[0:1, :] == slot
        hit1 = pos[1:2, :] == slot
        acc_ref[pl.ds(ws, rows), :] += _dot(jnp.where(hit0 | hit1, 1.0, 0.0).astype(BF16), h)
        weight = jnp.sum(jnp.where(hit0, wts[0:1, :], 0.0) + jnp.where(hit1, wts[1:2, :], 0.0),
                         axis=1, keepdims=True)
        sw_ref[pl.ds(ws, rows), :] += jnp.broadcast_to(weight, (rows, LANES))

    def body(pair, carry):
        c0 = clo_ref[i] + 2 * pair
        c1 = jnp.minimum(c0 + 1, c_last)
        live1 = c0 + 1 <= c_last
        lo0, hi0, small0 = item(c0)
        lo1, hi1, small1 = item(c1)
        both_small = small0 & small1

        @pl.when(both_small)
        def _():
            window(c0, lo0, MOE_DSP_SMALL, hi0 > lo0)
            window(c1, lo1, MOE_DSP_SMALL, live1 & (hi1 > lo1))

        @pl.when(jnp.logical_not(both_small))
        def _():
            pl.when((hi0 > lo0) & small0)(lambda: window(c0, lo0, MOE_DSP_SMALL, True))
            pl.when((hi0 > lo0) & jnp.logical_not(small0))(lambda: window(c0, lo0, MOE_DSP_LARGE, True))
            pl.when(live1 & (hi1 > lo1) & small1)(lambda: window(c1, lo1, MOE_DSP_SMALL, True))
            pl.when(live1 & (hi1 > lo1) & jnp.logical_not(small1))(lambda: window(c1, lo1, MOE_DSP_LARGE, True))
        return carry

    lax.fori_loop(0, (c_last - clo_ref[i] + 2) // 2, body, 0)
    o_ref[...] = acc_ref[...].astype(BF16)


def _dispatch_call(h2, route):
    n_tok = h2.shape[0]
    n_tiles = route["n_tiles"]
    grid_spec = pltpu.PrefetchScalarGridSpec(
        num_scalar_prefetch=5,
        grid=(n_tiles,),
        in_specs=[
            pl.BlockSpec((n_tok, D_MODEL), lambda i, *_: (0, 0), pipeline_mode=pl.Buffered(1)),
            pl.BlockSpec((n_tok // MOE_SRC, 8, MOE_SRC), lambda i, *_: (0, 0, 0), pipeline_mode=pl.Buffered(1)),
            pl.BlockSpec((n_tok // MOE_SRC, 8, MOE_SRC), lambda i, *_: (0, 0, 0), pipeline_mode=pl.Buffered(1)),
        ],
        out_specs=[pl.BlockSpec((MOE_TILE, D_MODEL), lambda i, *_: (i, 0)),
                   pl.BlockSpec((MOE_TILE, LANES), lambda i, *_: (i, 0))],
        scratch_shapes=[pltpu.VMEM((MOE_TILE, D_MODEL), F32)],
    )
    return pl.pallas_call(
        _dispatch_kernel,
        grid_spec=grid_spec,
        out_shape=[jax.ShapeDtypeStruct((n_tiles * MOE_TILE, D_MODEL), BF16),
                   jax.ShapeDtypeStruct((n_tiles * MOE_TILE, LANES), F32)],
        compiler_params=_params("arbitrary"),
        name="moe_dispatch",
    )(route["c_lo"], route["c_hi"], route["tile_expert"], route["run_start"], route["run_cnt"],
      h2, route["pos_rows"], route["w_rows"])


def _expert_kernel(te_ref, nv_ref, x_ref, w1_ref, w3_ref, w2_ref, sw_ref, o_ref, acc_ref):
    i = pl.program_id(0)
    f = pl.program_id(1)
    last = pl.num_programs(1) - 1
    valid = i < nv_ref[0]

    @pl.when(valid & (f == 0))
    def _():
        acc_ref[...] = jnp.zeros_like(acc_ref)

    @pl.when(valid)
    def _():
        acc_ref[...] += _swiglu_chunk(x_ref[...], w1_ref[0, 0], w3_ref[0, 0], w2_ref[0, 0])

    @pl.when(valid & (f == last))
    def _():
        o_ref[...] = (acc_ref[...] * sw_ref[:, 0:1]).astype(BF16)

    @pl.when(jnp.logical_not(valid) & (f == last))
    def _():
        o_ref[...] = jnp.zeros_like(o_ref)


def _expert_call(xs, slot_w, route, w1, w3, w2, layer):
    n_tiles = route["n_tiles"]
    grid_spec = pltpu.PrefetchScalarGridSpec(
        num_scalar_prefetch=2,
        grid=(n_tiles, D_FF // MOE_FF_CHUNK),
        in_specs=[
            pl.BlockSpec((MOE_TILE, D_MODEL), lambda i, f, te, nv: (i, 0)),
            pl.BlockSpec((1, 1, D_MODEL, MOE_FF_CHUNK), lambda i, f, te, nv: (layer, te[i], 0, f)),
            pl.BlockSpec((1, 1, D_MODEL, MOE_FF_CHUNK), lambda i, f, te, nv: (layer, te[i], 0, f)),
            pl.BlockSpec((1, 1, MOE_FF_CHUNK, D_MODEL), lambda i, f, te, nv: (layer, te[i], f, 0)),
            pl.BlockSpec((MOE_TILE, LANES), lambda i, f, te, nv: (i, 0)),
        ],
        out_specs=pl.BlockSpec((MOE_TILE, D_MODEL), lambda i, f, te, nv: (i, 0)),
        scratch_shapes=[pltpu.VMEM((MOE_TILE, D_MODEL), F32)],
    )
    return pl.pallas_call(
        _expert_kernel,
        grid_spec=grid_spec,
        out_shape=jax.ShapeDtypeStruct((n_tiles * MOE_TILE, D_MODEL), BF16),
        compiler_params=_params("arbitrary", "arbitrary"),
        name="moe_experts",
    )(route["tile_expert"], route["n_valid"], xs, w1, w3, w2, slot_w)


def _combine_kernel(small_ref, cnt_ref, winl_ref, offl_ref, wins_ref, offs_ref,
                    y_hbm, pos_ref, x_ref, mod_ref, fg_ref, o_ref, buf_ref, sem_ref, *, final_norm):
    c = pl.program_id(0)
    cur = c % 2
    variants = ((MOE_CMB_SMALL, wins_ref, offs_ref), (MOE_CMB_WIN, winl_ref, offl_ref))

    def window_copies(chunk, buf_set, rows, win_ref):
        return [pltpu.make_async_copy(
            y_hbm.at[pl.ds(pl.multiple_of(win_ref[chunk * N_EXPERTS + e], 16), rows), :],
            buf_ref.at[buf_set, pl.ds(e * rows, rows), :],
            sem_ref.at[buf_set, e]) for e in range(N_EXPERTS)]

    def start(chunk, buf_set):
        for use, (rows, win_ref, _) in zip((small_ref[chunk] > 0, small_ref[chunk] == 0), variants):
            @pl.when(use)
            def _():
                for cp in window_copies(chunk, buf_set, rows, win_ref):
                    cp.start()

    pl.when(c == 0)(lambda: start(0, 0))
    pl.when(c + 1 < pl.num_programs(0))(lambda: start(jnp.minimum(c + 1, pl.num_programs(0) - 1), 1 - cur))

    def process(rows, win_ref, off_ref):
        row = lax.broadcasted_iota(jnp.int32, (1, N_EXPERTS * rows), 1)
        slot = jnp.full((1, N_EXPERTS * rows), -1, jnp.int32)
        for e in range(N_EXPERTS):
            local = row - e * rows
            off = off_ref[c * N_EXPERTS + e]
            inside = (local >= off) & (local < off + cnt_ref[c * N_EXPERTS + e]) & (local < rows)
            slot = jnp.where(inside, win_ref[c * N_EXPERTS + e] + local, slot)
        pos = pos_ref[...]
        hit = (pos[:, 0:1] == slot) | (pos[:, 1:2] == slot)
        onehot = jnp.where(hit, 1.0, 0.0).astype(BF16)
        for cp in window_copies(c, cur, rows, win_ref):
            cp.wait()
        update = _dot(onehot, buf_ref[cur, 0:N_EXPERTS * rows, :])
        o_ref[...] = _residual_out(x_ref[...], mod_ref[0][5:6], update, final_norm, fg_ref[...])

    for use, variant in zip((small_ref[c] > 0, small_ref[c] == 0), variants):
        pl.when(use)(functools.partial(process, *variant))


def _combine_call(ys, route, x, mod_l, final_g, first_tile, final_norm):
    n_tok = x.shape[0]
    rows = lambda width: pl.BlockSpec((MOE_SRC, width), lambda c, *_: (c, 0))
    grid_spec = pltpu.PrefetchScalarGridSpec(
        num_scalar_prefetch=6,
        grid=(n_tok // MOE_SRC,),
        in_specs=[
            pl.BlockSpec(memory_space=pl.ANY),
            rows(TOP_K),
            rows(D_MODEL),
            pl.BlockSpec((1, ADA_CHUNKS, D_MODEL), lambda c, *_: (_tile_mod_row(c, first_tile, MOE_SRC), 0, 0)),
            pl.BlockSpec((1, D_MODEL), lambda c, *_: (0, 0)),
        ],
        out_specs=rows(D_MODEL),
        scratch_shapes=[pltpu.VMEM((2, N_EXPERTS * MOE_CMB_WIN, D_MODEL), BF16),
                        pltpu.SemaphoreType.DMA((2, N_EXPERTS))],
    )
    return pl.pallas_call(
        functools.partial(_combine_kernel, final_norm=final_norm),
        grid_spec=grid_spec,
        out_shape=jax.ShapeDtypeStruct((n_tok, D_MODEL), F32),
        compiler_params=_params("arbitrary"),
        name="moe_combine",
    )(route["cmb_small"], route["run_cnt"], route["win"], route["run_off"], route["win_s"], route["run_off_s"],
      ys, route["pos_cols"], x, mod_l, final_g)


def _rope_tables():
    t = np.arange(SEQ)

    def angles(rot_dim):
        half = rot_dim // 2
        inv = ROPE_THETA ** (-jnp.arange(0, half, 2, dtype=F32) / half)
        ang = jnp.concatenate([jnp.asarray(t // GRID_W, F32)[:, None] * inv[None, :],
                               jnp.asarray(t % GRID_W, F32)[:, None] * inv[None, :]], axis=-1)
        return jnp.cos(ang), jnp.sin(ang)

    def pad_rows(a, fill):
        return jnp.concatenate([jnp.full((ROW_TILE, a.shape[1]), fill, F32), a], axis=0)

    c, s = angles(HEAD_DIM)
    cos64 = jnp.tile(jnp.concatenate([c, c], axis=-1), (1, 4))
    sin64 = jnp.tile(jnp.concatenate([s, s], axis=-1), (1, 4))
    c, s = angles(MLA_ROPE)
    one = jnp.ones((SEQ, MLA_NOPE), F32)
    zero = jnp.zeros((SEQ, MLA_NOPE), F32)
    tail1 = jnp.ones((SEQ, MLA_PAD - MLA_NOPE - MLA_ROPE), F32)
    tail0 = jnp.zeros((SEQ, MLA_PAD - MLA_NOPE - MLA_ROPE), F32)
    cosm = jnp.tile(jnp.concatenate([one, c, c, tail1], axis=-1), (1, MLA_HEADS))
    sinm = jnp.tile(jnp.concatenate([zero, s, s, tail0], axis=-1), (1, MLA_HEADS))
    blockdiag = jnp.asarray(np.kron(np.eye(4), np.ones((HEAD_DIM, HEAD_DIM))), BF16)
    place = np.zeros((256, MLA_HEADS * MLA_PAD), np.float32)
    for h in range(MLA_HEADS):
        for r in range(MLA_ROPE):
            place[MLA_KV_RANK + r, h * MLA_PAD + MLA_NOPE + r] = 1.0
    return {
        "cos64": pad_rows(cos64, 1.0), "sin64": pad_rows(sin64, 0.0),
        "cosm": pad_rows(cosm, 1.0), "sinm": pad_rows(sinm, 0.0),
        "blockdiag": blockdiag, "pe_place": jnp.asarray(place, BF16),
    }


def _pack_w_in(w):
    n_qkv = COL_DKVA + MLA_KV_RANK + MLA_ROPE
    pad = jnp.zeros(w.shape[:2] + (COL_GATE - n_qkv,), BF16)
    return jnp.concatenate([w[..., :n_qkv].astype(BF16), pad, (0.5 * w[..., n_qkv:]).astype(BF16)], axis=-1)


def _pack_mla(wqb, wkvb):
    dqh = MLA_NOPE + MLA_ROPE
    q = wqb.reshape(MLA_Q_RANK, MLA_HEADS, dqh)
    q = jnp.pad(q, ((0, 0), (0, 0), (0, MLA_PAD - dqh))).reshape(MLA_Q_RANK, MLA_HEADS * MLA_PAD)
    kv = wkvb.reshape(MLA_KV_RANK, MLA_HEADS, MLA_NOPE + MLA_V)
    k = jnp.pad(kv[:, :, :MLA_NOPE], ((0, 0), (0, 0), (0, MLA_PAD - MLA_NOPE)))
    k = k.reshape(MLA_KV_RANK, MLA_HEADS * MLA_PAD)
    v = jnp.pad(kv[:, :, MLA_NOPE:], ((0, 0), (0, 0), (0, MLA_PAD - MLA_V))).reshape(MLA_KV_RANK, MLA_HEADS * MLA_PAD)
    return q.astype(BF16), k.astype(BF16), v.astype(BF16)


def _split2_host(w):
    hi = w.astype(BF16)
    return jnp.stack([hi, (w - hi.astype(F32)).astype(BF16)])


def kernel(x, c, ctx, c_ctx, norm1_g, norm2_g, w_ada, b_ada, w_in, na_rpb, gb_qnorm, gb_knorm, wc_sink,
           mla_qnorm, mla_kvnorm, mla_wqb, mla_wkvb, w_branch, w_out, ffn_w1, ffn_w3, ffn_w2,
           moe_router, moe_w1, moe_w3, moe_w2, final_g):
    n_batch = x.shape[0]
    assert x.shape[1:] == (SEQ, D_MODEL) and ctx.shape[1:] == (CTX_LEN, D_MODEL)
    assert n_batch * CTX_LEN <= CTX_ROWS and n_batch <= 8

    ctx_rows = ctx.reshape(n_batch * CTX_LEN, D_MODEL)
    if ctx_rows.shape[0] < CTX_ROWS:
        ctx_rows = jnp.pad(ctx_rows, ((0, CTX_ROWS - ctx_rows.shape[0]), (0, 0)))
    xt = jnp.concatenate([ctx_rows, x.reshape(n_batch * SEQ, D_MODEL)], axis=0)

    cvec = jnp.zeros((16, D_MODEL), F32).at[:n_batch].set(c).at[8].set(c_ctx)
    mod = _mod_call(cvec, w_ada, b_ada).reshape(DEPTH, 16, ADA_CHUNKS, D_MODEL)
    tabs = _rope_tables()
    w_in_p = _pack_w_in(w_in)
    moe_w = tuple(w.astype(BF16) for w in (moe_w1, moe_w3, moe_w2))
    mla_scale = (MLA_NOPE + MLA_ROPE) ** -0.5
    lat_tile0 = CTX_ROWS // ROW_TILE

    for l in range(DEPTH):
        with_ctx = l < DEPTH - 1
        wq, wk, wv = _pack_mla(mla_wqb[l], mla_wkvb[l])
        lw = {
            "gq": jnp.tile(gb_qnorm[l], 4)[None, :], "gk": jnp.tile(gb_knorm[l], 2)[None, :],
            "qn": mla_qnorm[l][None, :], "kvn": mla_kvnorm[l][None, :], "wqb": wq, "wk": wk, "wv": wv,
        }
        pa, cv, gate, qb, kb, qc, kc, qd, kd, vd, vb = _inproj_call(
            xt, mod[l], norm1_g[l][None, :], w_in_p, l, tabs, lw)

        oa = _natten_call(n_batch, with_ctx, pa, _natten_bias_table(na_rpb[l]))
        ob = _dense_attn_call(n_batch, with_ctx, (qb, 256, 0), kb, (vb, GB_KV_HEADS * LANES, 0),
                              n_kv=GB_KV_HEADS, group=GB_Q_HEADS // GB_KV_HEADS, dq=HEAD_DIM, name="attn_global")
        oc = _window_attn_call(n_batch, with_ctx, (qc, 256, 0), (kc, 128, 0), (cv, 128, 0), wc_sink[l])
        od = _dense_attn_call(n_batch, with_ctx, (qd, MLA_HEADS * MLA_PAD, 0), kd, (vd, MLA_HEADS * MLA_PAD, 0),
                              n_kv=MLA_HEADS, group=1, dq=MLA_PAD, scale=mla_scale, name="attn_mla")

        is_moe = l % 2 == 1
        router3 = None
        if is_moe:
            router3 = _split2_host(jnp.pad(moe_router[l // 2], ((0, 0), (0, LANES - N_EXPERTS))))
        first_tile = 0 if with_ctx else lat_tile0
        res = _merge_call((oa, ob, oc, od), gate, xt, mod[l], norm2_g[l][None, :],
                          (0.5 * w_branch[l]).astype(BF16), w_out[l].astype(BF16), router3, first_tile)
        x_mid, h2 = res[0], res[1]
        final_norm = l == DEPTH - 1
        if is_moe:
            route = _moe_route(res[2], h2.shape[0])
            xs, slot_w = _dispatch_call(h2, route)
            ys = _expert_call(xs, slot_w, route, *moe_w, l // 2)
            xt = _combine_call(ys, route, x_mid, mod[l], final_g[None, :],
                               0 if with_ctx else CTX_ROWS // MOE_SRC, final_norm)
        else:
            xt = _ffn_call(h2, ffn_w1, ffn_w3, ffn_w2, l // 2, x_mid, mod[l], final_g[None, :],
                           0 if with_ctx else CTX_ROWS // FFN_ROWS, final_norm)

    return xt.reshape(n_batch, SEQ, D_MODEL)
```

```python
import functools

import numpy as np
import jax
import jax.numpy as jnp
from jax import lax
from jax.experimental import pallas as pl
from jax.experimental.pallas import tpu as pltpu

F32 = jnp.float32
BF16 = jnp.bfloat16

D_MODEL = 1024
SEQ = 2048
DEPTH = 4
CTX_LEN = 256
GRID_W = 64
GRID_ROWS = SEQ // GRID_W
HEAD_DIM = 64
ROPE_THETA = 10000.0
NORM_EPS = 1e-6
NEG_INF = -1e30

NA_HEADS = 4
NA_KH = 8
NA_KW = 16
GB_Q_HEADS = 4
GB_KV_HEADS = 2
WC_Q_HEADS = 4
WC_KV_HEADS = 2
WC_WINDOW = 128
MLA_HEADS = 4
MLA_Q_RANK = 256
MLA_KV_RANK = 128
MLA_NOPE = 64
MLA_ROPE = 32
MLA_V = 64
MLA_PAD = 128
N_BRANCH = 4
BRANCH_W = 256
D_FF = 3584
N_EXPERTS = 8
ADA_CHUNKS = 6

VMEM_LIMIT_BYTES = 56 * 1024 * 1024
LANES = 128

CTX_ROWS = 2048
ROW_TILE = 512
Q_TILE = 256
NA_Q_ROWS = Q_TILE // GRID_W
NA_WIN_ROWS = 12
NA_WIN = NA_WIN_ROWS * GRID_W
WC_WIN = Q_TILE + 2 * WC_WINDOW
ATTN_KEY_CHUNK = 512

COL_AQ, COL_AK, COL_AV = 0, 256, 512
COL_BQ, COL_BK, COL_BV = 768, 1024, 1152
COL_CQ, COL_CK, COL_CV = 1280, 1536, 1664
COL_DQA, COL_DKVA, COL_GATE = 1792, 2048, 2304
GATE_W = N_BRANCH * D_MODEL
P_WIDTH = COL_GATE + GATE_W
QKV_CHUNK = COL_GATE // 2
FF_CHUNK = 512
MOE_FF_CHUNK = 1792


def _params(*sem):
    return pltpu.CompilerParams(dimension_semantics=sem, vmem_limit_bytes=VMEM_LIMIT_BYTES)


def _dot(a, b):
    return jnp.dot(a, b, preferred_element_type=F32)


def _dot_nt(a, b):
    return lax.dot_general(a, b, (((1,), (1,)), ((), ())), preferred_element_type=F32)


def _split3(x):
    hi = x.astype(BF16)
    r1 = x - hi.astype(F32)
    mid = r1.astype(BF16)
    lo = (r1 - mid.astype(F32)).astype(BF16)
    return hi, mid, lo


def _sigmoid(x):
    return 0.5 * jnp.tanh(0.5 * x) + 0.5


ADA_TN = 1536


def _mod_kernel(c_ref, w_ref, b_ref, o_ref):
    c = c_ref[...]
    sc = (c * _sigmoid(c)).astype(BF16)
    o_ref[0] = _dot(sc, w_ref[0].astype(BF16)) + b_ref[0]


def _mod_call(cvec, w_ada, b_ada):
    n = ADA_CHUNKS * D_MODEL
    return pl.pallas_call(
        _mod_kernel,
        grid=(DEPTH, n // ADA_TN),
        in_specs=[
            pl.BlockSpec((16, D_MODEL), lambda l, j: (0, 0)),
            pl.BlockSpec((1, D_MODEL, ADA_TN), lambda l, j: (l, 0, j)),
            pl.BlockSpec((1, 1, ADA_TN), lambda l, j: (l, 0, j)),
        ],
        out_specs=pl.BlockSpec((1, 16, ADA_TN), lambda l, j: (l, 0, j)),
        out_shape=jax.ShapeDtypeStruct((DEPTH, 16, n), F32),
        compiler_params=_params("arbitrary", "arbitrary"),
        name="adaln_mod",
    )(cvec, w_ada, b_ada.reshape(DEPTH, 1, n))


def _mod_row(i, first_tile):
    n_ctx = CTX_ROWS // ROW_TILE
    t = i + first_tile
    return jnp.where(t < n_ctx, 8, (t - n_ctx) // (SEQ // ROW_TILE))


def _norm_mod(x, g, shift, scale):
    ms = jnp.mean(x * x, axis=-1, keepdims=True)
    y = x * lax.rsqrt(ms + NORM_EPS) * g
    return y * (1.0 + scale) + shift


def _rope(x, cos, sin, half, first_mask):
    w = x.shape[-1]
    fwd = pltpu.roll(x, w - half, 1)
    bwd = pltpu.roll(x, half, 1)
    rot = jnp.where(first_mask, -fwd, bwd)
    return x * cos + rot * sin


def _head_rms(x, gain, blockdiag):
    hi, mid, lo = _split3(x * x)
    ss = _dot(hi, blockdiag) + _dot(mid, blockdiag) + _dot(lo, blockdiag)
    return x * lax.rsqrt(ss * (1.0 / HEAD_DIM) + NORM_EPS) * gain


def _inproj_kernel(x_ref, mod_ref, g_ref, w_ref,
                   cos64_ref, sin64_ref, cosm_ref, sinm_ref,
                   gq_ref, gk_ref, bd_ref, qn_ref, kvn_ref, wqb_ref, wk_ref, wv_ref, pe_ref,
                   pa_ref, cv_ref, gate_ref, qb_ref, kb_ref, qc_ref, kc_ref, qd_ref, kd_ref, vd_ref, vb_ref):
    m = mod_ref[0]
    hb = _norm_mod(x_ref[...], g_ref[...], m[0:1], m[1:2]).astype(BF16)

    def proj(c0, c1):
        return _dot(hb, w_ref[0, :, c0:c1])

    for c0 in range(0, GATE_W, D_MODEL):
        gate_ref[:, c0:c0 + D_MODEL] = proj(COL_GATE + c0, COL_GATE + c0 + D_MODEL).astype(BF16)
    pa_ref[...] = proj(COL_AQ, COL_BQ).astype(BF16)

    scale = HEAD_DIM ** -0.5
    half = HEAD_DIM // 2
    cos64, sin64 = cos64_ref[...], sin64_ref[...]
    lane = lax.broadcasted_iota(jnp.int32, (1, 256), 1)
    first64 = (lane % HEAD_DIM) < half
    bd = bd_ref[...]
    bc = proj(COL_BQ, COL_DQA)
    bq = _head_rms(bc[:, 0:256], gq_ref[...], bd)
    qb_ref[...] = (_rope(bq, cos64, sin64, half, first64) * scale).T.astype(BF16)
    bk = _head_rms(bc[:, 256:384], gk_ref[...], bd[:128, :128])
    kb_ref[...] = _rope(bk, cos64[:, :128], sin64[:, :128], half, first64[:, :128]).astype(BF16)
    qc_ref[...] = (_rope(bc[:, 512:768], cos64, sin64, half, first64) * scale).astype(BF16)
    kc_ref[...] = _rope(bc[:, 768:896], cos64[:, :128], sin64[:, :128], half, first64[:, :128]).astype(BF16)
    cv_ref[...] = bc[:, 896:1024].astype(BF16)

    cosm, sinm = cosm_ref[...], sinm_ref[...]
    lane_m = lax.broadcasted_iota(jnp.int32, (1, MLA_HEADS * MLA_PAD), 1) % MLA_PAD
    first_m = lane_m < (MLA_NOPE + MLA_ROPE // 2)
    d = proj(COL_DQA, COL_GATE)
    dqa = d[:, 0:MLA_Q_RANK]
    qn = dqa * lax.rsqrt(jnp.mean(dqa * dqa, axis=-1, keepdims=True) + NORM_EPS) * qn_ref[...]
    dq = _dot(qn.astype(BF16), wqb_ref[...])
    qd_ref[...] = _rope(dq, cosm, sinm, MLA_ROPE // 2, first_m).astype(BF16)

    dkva = d[:, MLA_Q_RANK:]
    kvc = dkva[:, :MLA_KV_RANK]
    kvn = (kvc * lax.rsqrt(jnp.mean(kvc * kvc, axis=-1, keepdims=True) + NORM_EPS) * kvn_ref[...]).astype(BF16)
    dk = _dot(kvn, wk_ref[...]) + _dot(dkva.astype(BF16), pe_ref[...])
    kd_ref[...] = _rope(dk, cosm, sinm, MLA_ROPE // 2, first_m).T.astype(BF16)
    ones_half = lane_m >= HEAD_DIM
    vd_ref[...] = jnp.where(ones_half, 1.0, _dot(kvn, wv_ref[...])).astype(BF16)
    bv = bc[:, 384:512]
    vb_ref[:LANES, :] = jnp.where(ones_half[:, :LANES], 1.0, bv).T.astype(BF16)
    vb_ref[LANES:, :] = jnp.where(ones_half[:, :LANES], 1.0, pltpu.roll(bv, HEAD_DIM, 1)).T.astype(BF16)


def _rope_row_block(i):
    n_ctx = CTX_ROWS // ROW_TILE
    return jnp.where(i < n_ctx, 0, 1 + (i - n_ctx) % (SEQ // ROW_TILE))


def _inproj_call(x, mod_l, g, w_in_p, layer, tabs, lw):
    t = x.shape[0]
    mw = MLA_HEADS * MLA_PAD

    def full(shape):
        return pl.BlockSpec(shape, lambda i: (0,) * len(shape))

    def rows(width):
        return pl.BlockSpec((ROW_TILE, width), lambda i: (i, 0))

    def tab(width):
        return pl.BlockSpec((ROW_TILE, width), lambda i: (_rope_row_block(i), 0))

    def cols(height):
        return pl.BlockSpec((height, ROW_TILE), lambda i: (0, i))

    outs = [(COL_BQ, False), (128, False), (GATE_W, False),
            (256, True), (128, False), (256, False), (128, False), (mw, False), (mw, True), (mw, False),
            (GB_KV_HEADS * LANES, True)]
    return pl.pallas_call(
        _inproj_kernel,
        grid=(t // ROW_TILE,),
        in_specs=[
            rows(D_MODEL),
            pl.BlockSpec((1, ADA_CHUNKS, D_MODEL), lambda i: (_mod_row(i, 0), 0, 0)),
            full((1, D_MODEL)),
            pl.BlockSpec((1, D_MODEL, P_WIDTH), lambda i: (layer, 0, 0), pipeline_mode=pl.Buffered(1)),
            tab(256), tab(256), tab(mw), tab(mw),
            full((1, 256)), full((1, 128)), full((256, 256)), full((1, MLA_Q_RANK)), full((1, MLA_KV_RANK)),
            full((MLA_Q_RANK, mw)), full((MLA_KV_RANK, mw)), full((MLA_KV_RANK, mw)), full((256, mw)),
        ],
        out_specs=[cols(w) if tr else rows(w) for w, tr in outs],
        out_shape=[jax.ShapeDtypeStruct((w, t) if tr else (t, w), BF16) for w, tr in outs],
        compiler_params=_params("arbitrary"),
        name="inproj",
    )(x, mod_l, g, w_in_p, tabs["cos64"], tabs["sin64"], tabs["cosm"], tabs["sinm"],
      lw["gq"], lw["gk"], tabs["blockdiag"], lw["qn"], lw["kvn"], lw["wqb"], lw["wk"], lw["wv"], tabs["pe_place"])


def _attend(q, segs, sink=None, scale=None):
    chunks = segs
    scores = []
    m = None
    for k, _, bias in chunks:
        s = _dot_nt(q, k)
        if scale is not None:
            s = s * scale
        if bias is not None:
            s = s + bias
        scores.append(s)
        ms = jnp.max(s, axis=-1, keepdims=True)
        m = ms if m is None else jnp.maximum(m, ms)
    if sink is not None:
        m = jnp.maximum(m, sink)
    denom = None
    out = None
    for s, (_, v, _) in zip(scores, chunks):
        p = jnp.exp(s - m)
        ps = jnp.sum(p, axis=-1, keepdims=True)
        pv = _dot(p.astype(BF16), v)
        denom = ps if denom is None else denom + ps
        out = pv if out is None else out + pv
    if sink is not None:
        denom = denom + jnp.exp(sink - m)
    return out / denom


def _gqa_heads(q_ref, o_ref, sink_ref, n_kv, group, dq, dv, scale, seg_fn):
    tq = q_ref.shape[0]
    for h in range(n_kv):
        heads = [h * group + g for g in range(group)]
        q = jnp.concatenate([q_ref[:, a * dq:(a + 1) * dq] for a in heads], axis=0) if group > 1 \
            else q_ref[:, h * dq:(h + 1) * dq]
        sink = None
        if sink_ref is not None:
            sink = jnp.concatenate([jnp.full((tq, 1), sink_ref[a], F32) for a in heads], axis=0)
        o = _attend(q, seg_fn(h), sink=sink, scale=scale)
        for g, a in enumerate(heads):
            o_ref[:, a * dv:(a + 1) * dv] = o[g * tq:(g + 1) * tq].astype(BF16)


def _dense_attn_kernel(q_ref, ktl_ref, vl_ref, ktc_ref, vc_ref, o_ref, *, n_kv, group, dq, scale, ctx_tile):
    tq = q_ref.shape[0]
    hd = HEAD_DIM

    def run(with_latent):
        for h in range(n_kv):
            heads = [h * group + g for g in range(group)]
            q = jnp.concatenate([q_ref[:, a * dq:(a + 1) * dq] for a in heads], axis=0) if group > 1 \
                else q_ref[:, h * dq:(h + 1) * dq]
            segs = [(ktc_ref[h * dq:(h + 1) * dq, :], vc_ref[:, h * LANES:(h + 1) * LANES])]
            if with_latent:
                segs = [(ktl_ref[h * dq:(h + 1) * dq, c0:c0 + ATTN_KEY_CHUNK],
                         vl_ref[c0:c0 + ATTN_KEY_CHUNK, h * LANES:(h + 1) * LANES])
                        for c0 in range(0, SEQ, ATTN_KEY_CHUNK)] + segs
            scores = []
            m = None
            for kt, _ in segs:
                s = _dot(q, kt)
                if scale is not None:
                    s = s * scale
                scores.append(s)
                ms = jnp.max(s, axis=-1, keepdims=True)
                m = ms if m is None else jnp.maximum(m, ms)
            acc = None
            for s, (_, v) in zip(scores, segs):
                pv = _dot(jnp.exp(s - m).astype(BF16), v)
                acc = pv if acc is None else acc + pv
            o = acc * pltpu.roll(1.0 / acc, hd, 1)
            for g, a in enumerate(heads):
                o_ref[:, a * hd:(a + 1) * hd] = o[g * tq:(g + 1) * tq, :hd].astype(BF16)

    if ctx_tile:
        j = pl.program_id(1)
        pl.when(j == 0)(lambda: run(False))
        pl.when(j > 0)(lambda: run(True))
    else:
        run(True)


def _gqa_attn_t_kernel(qt_ref, kl_ref, vtl_ref, kc_ref, vtc_ref, o_ref, *, n_kv, group, ctx_tile):
    tq = qt_ref.shape[1]
    hd = HEAD_DIM
    kw = kl_ref.shape[1]
    n = group * tq

    def run(with_latent):
        for h in range(n_kv):
            heads = [h * group + g for g in range(group)]
            qcat = jnp.concatenate([qt_ref[a * hd:(a + 1) * hd, :] for a in heads], axis=1)
            parts = [jnp.zeros((h * hd, n), BF16), qcat, jnp.zeros((kw - (h + 1) * hd, n), BF16)]
            wq = jnp.concatenate([p for p in parts if p.shape[0] > 0], axis=0)
            vrows = slice(h * LANES, (h + 1) * LANES)
            segs = [(kc_ref[...], vtc_ref[vrows, :])]
            if with_latent:
                segs = [(kl_ref[c0:c0 + ATTN_KEY_CHUNK, :], vtl_ref[vrows, c0:c0 + ATTN_KEY_CHUNK])
                        for c0 in range(0, SEQ, ATTN_KEY_CHUNK)] + segs
            scores = []
            m = None
            for k, _ in segs:
                s = _dot(k, wq)
                scores.append(s)
                ms = jnp.max(s, axis=0, keepdims=True)
                m = ms if m is None else jnp.maximum(m, ms)
            acc = None
            for s, (_, vt) in zip(scores, segs):
                pv = _dot(vt, jnp.exp(s - m).astype(BF16))
                acc = pv if acc is None else acc + pv
            o = acc[:hd] * (1.0 / acc[hd:hd + 1])
            for g, a in enumerate(heads):
                o_ref[:, a * hd:(a + 1) * hd] = o[:, g * tq:(g + 1) * tq].T.astype(BF16)

    if ctx_tile:
        j = pl.program_id(1)
        pl.when(j == 0)(lambda: run(False))
        pl.when(j > 0)(lambda: run(True))
    else:
        run(True)


def _q_row_block(b, j, n_batch, with_ctx):
    per_batch = SEQ // Q_TILE
    lat0 = CTX_ROWS // Q_TILE
    if with_ctx:
        return jnp.where(j == 0, b, lat0 + b * per_batch + j - 1)
    return lat0 + b * per_batch + j


def _attn_specs(n_batch, with_ctx, q, k, v, o_width, order_bj=True):
    def ix(f):
        return (lambda b, j: f(b, j)) if order_bj else (lambda j, b: f(b, j))

    def qspec(width, col):
        return pl.BlockSpec((Q_TILE, width), ix(lambda b, j: (_q_row_block(b, j, n_batch, with_ctx), col // width)))

    def lat(width, col):
        return pl.BlockSpec((SEQ, width), ix(lambda b, j: (CTX_ROWS // SEQ + b, col // width)))

    def ctx(width, col):
        return pl.BlockSpec((CTX_LEN, width), ix(lambda b, j: (b, col // width)))

    in_specs = [qspec(q[1], q[2]), lat(k[1], k[2]), lat(v[1], v[2]), ctx(k[1], k[2]), ctx(v[1], v[2])]
    args = [q[0], k[0], v[0], k[0], v[0]]
    if with_ctx:
        return in_specs, args, qspec(o_width, 0)
    out_spec = pl.BlockSpec((Q_TILE, o_width), ix(lambda b, j: (b * (SEQ // Q_TILE) + j, 0)))
    return in_specs, args, out_spec


def _attn_out_rows(t, with_ctx):
    return t if with_ctx else t - CTX_ROWS


def _dense_attn_call(n_batch, with_ctx, q, kt, v, *, n_kv, group, dq, scale=None, name):
    t = q[0].shape[0]
    in_specs, args, out_spec = _attn_specs(n_batch, with_ctx, q, v, v, 256)
    kt_rows = kt.shape[0]
    in_specs[1] = pl.BlockSpec((kt_rows, SEQ), lambda b, j: (0, CTX_ROWS // SEQ + b))
    in_specs[3] = pl.BlockSpec((kt_rows, CTX_LEN), lambda b, j: (0, b))
    args[1] = args[3] = kt
    kern = functools.partial(_dense_attn_kernel, n_kv=n_kv, group=group, dq=dq, scale=scale, ctx_tile=with_ctx)
    return pl.pallas_call(
        kern,
        grid=(n_batch, SEQ // Q_TILE + (1 if with_ctx else 0)),
        in_specs=in_specs,
        out_specs=out_spec,
        out_shape=jax.ShapeDtypeStruct((_attn_out_rows(t, with_ctx), 256), BF16),
        compiler_params=_params("arbitrary", "arbitrary"),
        name=name,
    )(*args)


def _gqa_attn_t_call(n_batch, with_ctx, qt, k, vt, *, n_kv, group, name):
    t = k.shape[0]
    kspec = (k, k.shape[1], 0)
    in_specs, args, out_spec = _attn_specs(n_batch, with_ctx, kspec, kspec, kspec, 256)
    in_specs[0] = pl.BlockSpec((qt.shape[0], Q_TILE), lambda b, j: (0, _q_row_block(b, j, n_batch, with_ctx)))
    in_specs[2] = pl.BlockSpec((vt.shape[0], SEQ), lambda b, j: (0, CTX_ROWS // SEQ + b))
    in_specs[4] = pl.BlockSpec((vt.shape[0], CTX_LEN), lambda b, j: (0, b))
    args[0] = qt
    args[2] = args[4] = vt
    return pl.pallas_call(
        functools.partial(_gqa_attn_t_kernel, n_kv=n_kv, group=group, ctx_tile=with_ctx),
        grid=(n_batch, SEQ // Q_TILE + (1 if with_ctx else 0)),
        in_specs=in_specs,
        out_specs=out_spec,
        out_shape=jax.ShapeDtypeStruct((_attn_out_rows(t, with_ctx), 256), BF16),
        compiler_params=_params("arbitrary", "arbitrary"),
        name=name,
    )(*args)


def _window_attn_kernel(sink_ref, q_ref, kl_ref, vl_ref, kc_ref, vc_ref, o_ref, *, ctx_tile):
    j = pl.program_id(1)
    hd = HEAD_DIM
    group = WC_Q_HEADS // WC_KV_HEADS

    def ctx_run():
        def segs(h):
            return [(kc_ref[:, h * hd:(h + 1) * hd], vc_ref[:, h * hd:(h + 1) * hd], None)]
        _gqa_heads(q_ref, o_ref, sink_ref, WC_KV_HEADS, group, hd, hd, None, segs)

    def lat_run():
        i = j - 1 if ctx_tile else j
        start = pl.multiple_of(jnp.clip(i * Q_TILE - WC_WINDOW, 0, SEQ - WC_WIN), WC_WINDOW)
        qpos = i * Q_TILE + lax.broadcasted_iota(jnp.int32, (group * Q_TILE, WC_WIN), 0) % Q_TILE
        kpos = start + lax.broadcasted_iota(jnp.int32, (group * Q_TILE, WC_WIN), 1)
        bias = jnp.where(jnp.abs(qpos - kpos) <= WC_WINDOW, 0.0, NEG_INF).astype(F32)

        def segs(h):
            return [(kl_ref[pl.ds(start, WC_WIN), h * hd:(h + 1) * hd],
                     vl_ref[pl.ds(start, WC_WIN), h * hd:(h + 1) * hd], bias),
                    (kc_ref[:, h * hd:(h + 1) * hd], vc_ref[:, h * hd:(h + 1) * hd], None)]
        _gqa_heads(q_ref, o_ref, sink_ref, WC_KV_HEADS, group, hd, hd, None, segs)

    if ctx_tile:
        pl.when(j == 0)(ctx_run)
        pl.when(j > 0)(lat_run)
    else:
        lat_run()


def _window_attn_call(n_batch, with_ctx, q, k, v, sink):
    t = q[0].shape[0]
    in_specs, args, out_spec = _attn_specs(n_batch, with_ctx, q, k, v, 256)
    return pl.pallas_call(
        functools.partial(_window_attn_kernel, ctx_tile=with_ctx),
        grid=(n_batch, SEQ // Q_TILE + (1 if with_ctx else 0)),
        in_specs=[pl.BlockSpec(memory_space=pltpu.SMEM)] + in_specs,
        out_specs=out_spec,
        out_shape=jax.ShapeDtypeStruct((_attn_out_rows(t, with_ctx), 256), BF16),
        compiler_params=_params("arbitrary", "arbitrary"),
        name="attn_window",
    )(sink, *args)


def _natten_kernel(q_ref, kl_ref, vl_ref, kc_ref, vc_ref, tab_ref, o_ref, bias_ref, *, ctx_tile):
    j = pl.program_id(0)
    hd = HEAD_DIM
    scale = HEAD_DIM ** -0.5

    def build_bias(i):
        ws = jnp.clip(NA_Q_ROWS * i - NA_KH // 2, 0, GRID_ROWS - NA_WIN_ROWS)
        for a in range(NA_Q_ROWS):
            r = NA_Q_ROWS * i + a
            krow0 = jnp.clip(r - NA_KH // 2, 0, GRID_ROWS - NA_KH)
            for pair in range(NA_WIN_ROWS // 2):
                idx = []
                for side in range(2):
                    kr = ws + 2 * pair + side
                    in_rows = (kr >= krow0) & (kr < krow0 + NA_KH)
                    idx.append(jnp.where(in_rows, kr - r + NA_KH - 1, NA_NO_ROW))
                for h in range(NA_HEADS):
                    bias_ref[h, a * GRID_W:(a + 1) * GRID_W, pair * LANES:(pair + 1) * LANES] = (
                        tab_ref[0, h, idx[0]] + tab_ref[1, h, idx[1]])

    def ctx_run():
        for h in range(NA_HEADS):
            sl = slice(h * hd, (h + 1) * hd)
            o = _attend(q_ref[:, sl] * scale, [(kc_ref[:, sl], vc_ref[:, sl], None)])
            o_ref[:, sl] = o.astype(BF16)

    def lat_run():
        i = j - 1 if ctx_tile else j
        ws = jnp.clip(NA_Q_ROWS * i - NA_KH // 2, 0, GRID_ROWS - NA_WIN_ROWS)
        start = pl.multiple_of(ws * GRID_W, GRID_W)
        pl.when(pl.program_id(1) == 0)(lambda: build_bias(i))
        for h in range(NA_HEADS):
            sl = slice(h * hd, (h + 1) * hd)
            segs = [(kl_ref[pl.ds(start, NA_WIN), sl], vl_ref[pl.ds(start, NA_WIN), sl], bias_ref[h]),
                    (kc_ref[:, sl], vc_ref[:, sl], None)]
            o_ref[:, sl] = _attend(q_ref[:, sl] * scale, segs).astype(BF16)

    if ctx_tile:
        pl.when(j == 0)(ctx_run)
        pl.when(j > 0)(lat_run)
    else:
        lat_run()


NA_NO_ROW = 2 * NA_KH - 1


def _natten_call(n_batch, with_ctx, p, bias_blocks):
    t = p.shape[0]
    q, k, v = (p, 256, COL_AQ), (p, 256, COL_AK), (p, 256, COL_AV)
    in_specs, args, out_spec = _attn_specs(n_batch, with_ctx, q, k, v, 256, order_bj=False)
    in_specs.append(pl.BlockSpec(bias_blocks.shape, lambda j, b: (0,) * bias_blocks.ndim))
    return pl.pallas_call(
        functools.partial(_natten_kernel, ctx_tile=with_ctx),
        grid=(SEQ // Q_TILE + (1 if with_ctx else 0), n_batch),
        in_specs=in_specs,
        out_specs=out_spec,
        out_shape=jax.ShapeDtypeStruct((_attn_out_rows(t, with_ctx), 256), BF16),
        scratch_shapes=[pltpu.VMEM((NA_HEADS, Q_TILE, NA_WIN), F32)],
        compiler_params=_params("arbitrary", "arbitrary"),
        name="attn_natten",
    )(*args, bias_blocks)


def _natten_bias_table(rpb):
    n_dr, n_dc = 2 * NA_KH - 1, 2 * NA_KW - 1
    col = np.arange(GRID_W)
    dc = np.clip(col[None, :] - col[:, None] + NA_KW - 1, 0, n_dc - 1)
    onehot = jnp.asarray(dc[None] == np.arange(n_dc)[:, None, None], F32)
    blocks = jnp.einsum('hdc,cqk->hdqk', rpb.astype(F32), onehot, precision=lax.Precision.HIGHEST)
    cstart = np.clip(col - NA_KW // 2, 0, GRID_W - NA_KW)
    col_ok = (col[None, :] >= cstart[:, None]) & (col[None, :] < cstart[:, None] + NA_KW)
    blocks = jnp.where(col_ok[None, None], blocks, NEG_INF)
    blocks = jnp.concatenate([blocks, jnp.full((NA_HEADS, 1, GRID_W, GRID_W), NEG_INF, F32)], axis=1)
    zero = jnp.zeros_like(blocks)
    return jnp.stack([jnp.concatenate([blocks, zero], axis=-1), jnp.concatenate([zero, blocks], axis=-1)])


def _merge_kernel(*refs, with_router):
    if with_router:
        (oa_ref, ob_ref, oc_ref, od_ref, gate_ref, x_ref, mod_ref, g2_ref, wb_ref, wo_ref, rt_ref,
         xo_ref, h2_ref, comb_ref) = refs
    else:
        (oa_ref, ob_ref, oc_ref, od_ref, gate_ref, x_ref, mod_ref, g2_ref, wb_ref, wo_ref,
         xo_ref, h2_ref) = refs
    m = mod_ref[0]
    acc = None
    for n, o_ref in enumerate((oa_ref, ob_ref, oc_ref, od_ref)):
        half_y = _dot(o_ref[...], wb_ref[n])
        t = jnp.tanh(gate_ref[:, n * D_MODEL:(n + 1) * D_MODEL].astype(F32)) + 1.0
        acc = t * half_y if acc is None else acc + t * half_y
    mix = _dot(acc.astype(BF16), wo_ref[...])
    x = x_ref[...] + m[2:3] * mix
    xo_ref[...] = x
    h2 = _norm_mod(x, g2_ref[...], m[3:4], m[4:5])
    h2_ref[...] = h2.astype(BF16)
    if with_router:
        hh, hm, _ = _split3(h2)
        rh, rm = rt_ref[0], rt_ref[1]
        logits = _dot(hh, rh) + (_dot(hh, rm) + _dot(hm, rh))
        lane = lax.broadcasted_iota(jnp.int32, logits.shape, 1).astype(F32)
        logits = jnp.where(lane < N_EXPERTS, logits, NEG_INF)
        m1 = jnp.max(logits, axis=-1, keepdims=True)
        i1 = jnp.min(jnp.where(logits == m1, lane, float(LANES)), axis=-1, keepdims=True)
        rest = jnp.where(lane == i1, NEG_INF, logits)
        m2 = jnp.max(rest, axis=-1, keepdims=True)
        i2 = jnp.min(jnp.where(rest == m2, lane, float(LANES)), axis=-1, keepdims=True)
        e = jnp.exp(m2 - m1)
        w1 = 1.0 / (1.0 + e)
        w2 = e / (1.0 + e)
        comb_ref[...] = (jnp.where(lane == 0.0, i1, 0.0) + jnp.where(lane == 1.0, i2, 0.0)
                         + jnp.where(lane == 2.0, w1, 0.0) + jnp.where(lane == 3.0, w2, 0.0))


def _merge_call(outs, gate, x, mod_l, g2, wb, wo, router3, first_tile):
    t = x.shape[0]
    n_rows = t - first_tile * ROW_TILE

    def rows(width, col=0):
        return pl.BlockSpec((ROW_TILE, width), lambda i: (i + first_tile, col // width))

    def orow(width):
        return pl.BlockSpec((ROW_TILE, width), lambda i: (i, 0))

    def full(shape):
        return pl.BlockSpec(shape, lambda i: (0,) * len(shape))

    in_specs = [orow(256)] * 4 + [
        rows(GATE_W),
        rows(D_MODEL),
        pl.BlockSpec((1, ADA_CHUNKS, D_MODEL), lambda i: (_mod_row(i, first_tile), 0, 0)),
        full((1, D_MODEL)), full((N_BRANCH, BRANCH_W, D_MODEL)), full((D_MODEL, D_MODEL)),
    ]
    args = list(outs) + [gate, x, mod_l, g2, wb, wo]
    out_specs = [orow(D_MODEL), orow(D_MODEL)]
    out_shape = [jax.ShapeDtypeStruct((n_rows, D_MODEL), F32), jax.ShapeDtypeStruct((n_rows, D_MODEL), BF16)]
    if router3 is not None:
        in_specs.append(full((2, D_MODEL, LANES)))
        args.append(router3)
        out_specs.append(orow(LANES))
        out_shape.append(jax.ShapeDtypeStruct((n_rows, LANES), F32))
    return pl.pallas_call(
        functools.partial(_merge_kernel, with_router=router3 is not None),
        grid=(n_rows // ROW_TILE,),
        in_specs=in_specs,
        out_specs=out_specs,
        out_shape=out_shape,
        compiler_params=_params("arbitrary"),
        name="merge",
    )(*args)


FFN_ROWS = 1024


def _swiglu_chunk(h, w1, w3, w2):
    half_a = 0.5 * _dot(h, w1)
    b = _dot(h, w3)
    return _dot((half_a * (jnp.tanh(half_a) + 1.0) * b).astype(BF16), w2)


def _residual_out(x, gate, update, final_norm, final_g):
    x = x + gate * update
    if final_norm:
        ms = jnp.mean(x * x, axis=-1, keepdims=True)
        x = x * lax.rsqrt(ms + NORM_EPS) * final_g
    return x


def _ffn_kernel(h_ref, w1_ref, w3_ref, w2_ref, x_ref, mod_ref, fg_ref, o_ref, acc_ref, *, final_norm):
    f = pl.program_id(1)

    @pl.when(f == 0)
    def _():
        acc_ref[...] = jnp.zeros_like(acc_ref)

    acc_ref[...] += _swiglu_chunk(h_ref[...], w1_ref[0].astype(BF16), w3_ref[0].astype(BF16),
                                  w2_ref[0].astype(BF16))

    @pl.when(f == pl.num_programs(1) - 1)
    def _():
        o_ref[...] = _residual_out(x_ref[...], mod_ref[0][5:6], acc_ref[...], final_norm, fg_ref[...])


def _tile_mod_row(i, first_tile, tile_rows):
    n_ctx = CTX_ROWS // tile_rows
    t = i + first_tile
    return jnp.where(t < n_ctx, 8, (t - n_ctx) // (SEQ // tile_rows))


def _ffn_call(h2, w1, w3, w2, layer, x, mod_l, final_g, first_row_tile, final_norm):
    t = x.shape[0]
    rows = lambda width: pl.BlockSpec((FFN_ROWS, width), lambda i, f: (i, 0))
    return pl.pallas_call(
        functools.partial(_ffn_kernel, final_norm=final_norm),
        grid=(t // FFN_ROWS, D_FF // FF_CHUNK),
        in_specs=[
            rows(D_MODEL),
            pl.BlockSpec((1, D_MODEL, FF_CHUNK), lambda i, f: (layer, 0, f)),
            pl.BlockSpec((1, D_MODEL, FF_CHUNK), lambda i, f: (layer, 0, f)),
            pl.BlockSpec((1, FF_CHUNK, D_MODEL), lambda i, f: (layer, f, 0)),
            rows(D_MODEL),
            pl.BlockSpec((1, ADA_CHUNKS, D_MODEL),
                         lambda i, f: (_tile_mod_row(i, first_row_tile, FFN_ROWS), 0, 0)),
            pl.BlockSpec((1, D_MODEL), lambda i, f: (0, 0)),
        ],
        out_specs=rows(D_MODEL),
        out_shape=jax.ShapeDtypeStruct((t, D_MODEL), F32),
        scratch_shapes=[pltpu.VMEM((FFN_ROWS, D_MODEL), F32)],
        compiler_params=_params("arbitrary", "arbitrary"),
        name="ffn",
    )(h2, w1, w3, w2, x, mod_l, final_g)


MOE_TILE = 512
MOE_SRC = 256
MOE_CMB_WIN = MOE_SRC + 16
MOE_CMB_SMALL = 144
MOE_DSP_SMALL = 144
MOE_DSP_LARGE = MOE_SRC + 16
TOP_K = 2


def _moe_route(sel, n_tok):
    n_tiles = TOP_K * n_tok // MOE_TILE + N_EXPERTS
    e = sel[:, 0:TOP_K].astype(jnp.int32)
    flat_e = e.reshape(-1)
    onehot = (flat_e[None, :] == jnp.arange(N_EXPERTS)[:, None]).astype(jnp.int32)
    csum = jnp.cumsum(onehot, axis=1)
    counts = csum[:, -1]
    padded = (counts + MOE_TILE - 1) // MOE_TILE * MOE_TILE
    seg_end = jnp.cumsum(padded)
    seg_start = seg_end - padded
    pos = jnp.sum(onehot * (seg_start[:, None] + csum - onehot), axis=0)
    pos2 = pos.reshape(n_tok, TOP_K)
    tile_ix = jnp.arange(n_tiles)
    tile_expert = jnp.minimum(jnp.sum(tile_ix[:, None] >= (seg_end // MOE_TILE)[None, :], axis=1), N_EXPERTS - 1)
    n_valid = seg_end[-1] // MOE_TILE
    n_src = n_tok // MOE_SRC
    per_chunk = MOE_SRC * TOP_K
    before = jnp.concatenate([jnp.zeros((1, N_EXPERTS), jnp.int32), csum[:, per_chunk - 1::per_chunk].T], axis=0)
    run_start = seg_start[None, :] + before[:-1]
    run_cnt = before[1:] - before[:-1]
    cum = jnp.take(before, tile_expert, axis=1)
    local0 = tile_ix * MOE_TILE - jnp.take(seg_start, tile_expert)
    local1 = jnp.minimum(local0 + MOE_TILE, jnp.take(counts, tile_expert))
    c_lo = jnp.sum(cum[1:] <= local0[None, :], axis=0)
    c_hi = jnp.sum(cum[:-1] < local1[None, :], axis=0) - 1
    win = jnp.minimum(run_start // 16 * 16, n_tiles * MOE_TILE - MOE_CMB_WIN)
    win_s = jnp.minimum(run_start // 16 * 16, n_tiles * MOE_TILE - MOE_CMB_SMALL)
    cmb_small = jnp.all(run_start - win_s + run_cnt <= MOE_CMB_SMALL, axis=1)

    def token_rows(a, dtype):
        rows = jnp.transpose(a.reshape(n_src, MOE_SRC, TOP_K), (0, 2, 1)).astype(dtype)
        return jnp.zeros((n_src, 8, MOE_SRC), dtype).at[:, :TOP_K, :].set(rows)

    i32 = lambda a: a.astype(jnp.int32)
    return {
        "n_tiles": n_tiles, "tile_expert": i32(tile_expert), "n_valid": i32(n_valid).reshape(1),
        "c_lo": i32(c_lo), "c_hi": i32(c_hi), "run_start": i32(run_start.reshape(-1)),
        "run_cnt": i32(run_cnt.reshape(-1)), "win": i32(win.reshape(-1)),
        "run_off": i32((run_start - win).reshape(-1)),
        "win_s": i32(win_s.reshape(-1)), "run_off_s": i32((run_start - win_s).reshape(-1)),
        "cmb_small": i32(cmb_small),
        "pos_rows": token_rows(pos2, jnp.int32), "w_rows": token_rows(sel[:, TOP_K:2 * TOP_K], F32),
        "pos_cols": i32(pos2),
    }


def _dispatch_kernel(clo_ref, chi_ref, te_ref, rs_ref, rc_ref, h_ref, pos_ref, w_ref, o_ref, sw_ref, acc_ref):
    i = pl.program_id(0)
    base = i * MOE_TILE
    expert = te_ref[i]
    acc_ref[...] = jnp.zeros_like(acc_ref)
    sw_ref[...] = jnp.zeros_like(sw_ref)

    c_last = chi_ref[i]

    def item(c):
        run0 = rs_ref[c * N_EXPERTS + expert]
        lo = jnp.maximum(run0, base) - base
        hi = jnp.minimum(run0 + rc_ref[c * N_EXPERTS + expert], base + MOE_TILE) - base
        small = hi - jnp.minimum(lo // 16 * 16, MOE_TILE - MOE_DSP_SMALL) <= MOE_DSP_SMALL
        return lo, hi, small

    def window(c, lo, rows, live):
        pos = pos_ref[c]
        wts = w_ref[c]
        h = h_ref[pl.ds(pl.multiple_of(c * MOE_SRC, MOE_SRC), MOE_SRC), :]
        ws = pl.multiple_of(jnp.minimum(lo // 16 * 16, MOE_TILE - rows), 16)
        slot = base + ws + lax.broadcasted_iota(jnp.int32, (rows, MOE_SRC), 0)
        slot = jnp.where(live, slot, -1)
        hit0 = pos[0:1, :] == slot
        hit1 = pos[1:2, :] == slot
        acc_ref[pl.ds(ws, rows), :] += _dot(jnp.where(hit0 | hit1, 1.0, 0.0).astype(BF16), h)
        weight = jnp.sum(jnp.where(hit0, wts[0:1, :], 0.0) + jnp.where(hit1, wts[1:2, :], 0.0),
                         axis=1, keepdims=True)
        sw_ref[pl.ds(ws, rows), :] += jnp.broadcast_to(weight, (rows, LANES))

    def body(pair, carry):
        c0 = clo_ref[i] + 2 * pair
        c1 = jnp.minimum(c0 + 1, c_last)
        live1 = c0 + 1 <= c_last
        lo0, hi0, small0 = item(c0)
        lo1, hi1, small1 = item(c1)
        both_small = small0 & small1

        @pl.when(both_small)
        def _():
            window(c0, lo0, MOE_DSP_SMALL, hi0 > lo0)
            window(c1, lo1, MOE_DSP_SMALL, live1 & (hi1 > lo1))

        @pl.when(jnp.logical_not(both_small))
        def _():
            pl.when((hi0 > lo0) & small0)(lambda: window(c0, lo0, MOE_DSP_SMALL, True))
            pl.when((hi0 > lo0) & jnp.logical_not(small0))(lambda: window(c0, lo0, MOE_DSP_LARGE, True))
            pl.when(live1 & (hi1 > lo1) & small1)(lambda: window(c1, lo1, MOE_DSP_SMALL, True))
            pl.when(live1 & (hi1 > lo1) & jnp.logical_not(small1))(lambda: window(c1, lo1, MOE_DSP_LARGE, True))
        return carry

    lax.fori_loop(0, (c_last - clo_ref[i] + 2) // 2, body, 0)
    o_ref[...] = acc_ref[...].astype(BF16)


def _dispatch_call(h2, route):
    n_tok = h2.shape[0]
    n_tiles = route["n_tiles"]
    grid_spec = pltpu.PrefetchScalarGridSpec(
        num_scalar_prefetch=5,
        grid=(n_tiles,),
        in_specs=[
            pl.BlockSpec((n_tok, D_MODEL), lambda i, *_: (0, 0), pipeline_mode=pl.Buffered(1)),
            pl.BlockSpec((n_tok // MOE_SRC, 8, MOE_SRC), lambda i, *_: (0, 0, 0), pipeline_mode=pl.Buffered(1)),
            pl.BlockSpec((n_tok // MOE_SRC, 8, MOE_SRC), lambda i, *_: (0, 0, 0), pipeline_mode=pl.Buffered(1)),
        ],
        out_specs=[pl.BlockSpec((MOE_TILE, D_MODEL), lambda i, *_: (i, 0)),
                   pl.BlockSpec((MOE_TILE, LANES), lambda i, *_: (i, 0))],
        scratch_shapes=[pltpu.VMEM((MOE_TILE, D_MODEL), F32)],
    )
    return pl.pallas_call(
        _dispatch_kernel,
        grid_spec=grid_spec,
        out_shape=[jax.ShapeDtypeStruct((n_tiles * MOE_TILE, D_MODEL), BF16),
                   jax.ShapeDtypeStruct((n_tiles * MOE_TILE, LANES), F32)],
        compiler_params=_params("arbitrary"),
        name="moe_dispatch",
    )(route["c_lo"], route["c_hi"], route["tile_expert"], route["run_start"], route["run_cnt"],
      h2, route["pos_rows"], route["w_rows"])


def _expert_kernel(te_ref, nv_ref, x_ref, w1_ref, w3_ref, w2_ref, sw_ref, o_ref, acc_ref):
    i = pl.program_id(0)
    f = pl.program_id(1)
    last = pl.num_programs(1) - 1
    valid = i < nv_ref[0]

    @pl.when(valid & (f == 0))
    def _():
        acc_ref[...] = jnp.zeros_like(acc_ref)

    @pl.when(valid)
    def _():
        acc_ref[...] += _swiglu_chunk(x_ref[...], w1_ref[0, 0], w3_ref[0, 0], w2_ref[0, 0])

    @pl.when(valid & (f == last))
    def _():
        o_ref[...] = (acc_ref[...] * sw_ref[:, 0:1]).astype(BF16)

    @pl.when(jnp.logical_not(valid) & (f == last))
    def _():
        o_ref[...] = jnp.zeros_like(o_ref)


def _expert_call(xs, slot_w, route, w1, w3, w2, layer):
    n_tiles = route["n_tiles"]
    grid_spec = pltpu.PrefetchScalarGridSpec(
        num_scalar_prefetch=2,
        grid=(n_tiles, D_FF // MOE_FF_CHUNK),
        in_specs=[
            pl.BlockSpec((MOE_TILE, D_MODEL), lambda i, f, te, nv: (i, 0)),
            pl.BlockSpec((1, 1, D_MODEL, MOE_FF_CHUNK), lambda i, f, te, nv: (layer, te[i], 0, f)),
            pl.BlockSpec((1, 1, D_MODEL, MOE_FF_CHUNK), lambda i, f, te, nv: (layer, te[i], 0, f)),
            pl.BlockSpec((1, 1, MOE_FF_CHUNK, D_MODEL), lambda i, f, te, nv: (layer, te[i], f, 0)),
            pl.BlockSpec((MOE_TILE, LANES), lambda i, f, te, nv: (i, 0)),
        ],
        out_specs=pl.BlockSpec((MOE_TILE, D_MODEL), lambda i, f, te, nv: (i, 0)),
        scratch_shapes=[pltpu.VMEM((MOE_TILE, D_MODEL), F32)],
    )
    return pl.pallas_call(
        _expert_kernel,
        grid_spec=grid_spec,
        out_shape=jax.ShapeDtypeStruct((n_tiles * MOE_TILE, D_MODEL), BF16),
        compiler_params=_params("arbitrary", "arbitrary"),
        name="moe_experts",
    )(route["tile_expert"], route["n_valid"], xs, w1, w3, w2, slot_w)


def _combine_kernel(small_ref, cnt_ref, winl_ref, offl_ref, wins_ref, offs_ref,
                    y_hbm, pos_ref, x_ref, mod_ref, fg_ref, o_ref, buf_ref, sem_ref, *, final_norm):
    c = pl.program_id(0)
    cur = c % 2
    variants = ((MOE_CMB_SMALL, wins_ref, offs_ref), (MOE_CMB_WIN, winl_ref, offl_ref))

    def window_copies(chunk, buf_set, rows, win_ref):
        return [pltpu.make_async_copy(
            y_hbm.at[pl.ds(pl.multiple_of(win_ref[chunk * N_EXPERTS + e], 16), rows), :],
            buf_ref.at[buf_set, pl.ds(e * rows, rows), :],
            sem_ref.at[buf_set, e]) for e in range(N_EXPERTS)]

    def start(chunk, buf_set):
        for use, (rows, win_ref, _) in zip((small_ref[chunk] > 0, small_ref[chunk] == 0), variants):
            @pl.when(use)
            def _():
                for cp in window_copies(chunk, buf_set, rows, win_ref):
                    cp.start()

    pl.when(c == 0)(lambda: start(0, 0))
    pl.when(c + 1 < pl.num_programs(0))(lambda: start(jnp.minimum(c + 1, pl.num_programs(0) - 1), 1 - cur))

    def process(rows, win_ref, off_ref):
        row = lax.broadcasted_iota(jnp.int32, (1, N_EXPERTS * rows), 1)
        slot = jnp.full((1, N_EXPERTS * rows), -1, jnp.int32)
        for e in range(N_EXPERTS):
            local = row - e * rows
            off = off_ref[c * N_EXPERTS + e]
            inside = (local >= off) & (local < off + cnt_ref[c * N_EXPERTS + e]) & (local < rows)
            slot = jnp.where(inside, win_ref[c * N_EXPERTS + e] + local, slot)
        pos = pos_ref[...]
        hit = (pos[:, 0:1] == slot) | (pos[:, 1:2] == slot)
        onehot = jnp.where(hit, 1.0, 0.0).astype(BF16)
        for cp in window_copies(c, cur, rows, win_ref):
            cp.wait()
        update = _dot(onehot, buf_ref[cur, 0:N_EXPERTS * rows, :])
        o_ref[...] = _residual_out(x_ref[...], mod_ref[0][5:6], update, final_norm, fg_ref[...])

    for use, variant in zip((small_ref[c] > 0, small_ref[c] == 0), variants):
        pl.when(use)(functools.partial(process, *variant))


def _combine_call(ys, route, x, mod_l, final_g, first_tile, final_norm):
    n_tok = x.shape[0]
    rows = lambda width: pl.BlockSpec((MOE_SRC, width), lambda c, *_: (c, 0))
    grid_spec = pltpu.PrefetchScalarGridSpec(
        num_scalar_prefetch=6,
        grid=(n_tok // MOE_SRC,),
        in_specs=[
            pl.BlockSpec(memory_space=pl.ANY),
            rows(TOP_K),
            rows(D_MODEL),
            pl.BlockSpec((1, ADA_CHUNKS, D_MODEL), lambda c, *_: (_tile_mod_row(c, first_tile, MOE_SRC), 0, 0)),
            pl.BlockSpec((1, D_MODEL), lambda c, *_: (0, 0)),
        ],
        out_specs=rows(D_MODEL),
        scratch_shapes=[pltpu.VMEM((2, N_EXPERTS * MOE_CMB_WIN, D_MODEL), BF16),
                        pltpu.SemaphoreType.DMA((2, N_EXPERTS))],
    )
    return pl.pallas_call(
        functools.partial(_combine_kernel, final_norm=final_norm),
        grid_spec=grid_spec,
        out_shape=jax.ShapeDtypeStruct((n_tok, D_MODEL), F32),
        compiler_params=_params("arbitrary"),
        name="moe_combine",
    )(route["cmb_small"], route["run_cnt"], route["win"], route["run_off"], route["win_s"], route["run_off_s"],
      ys, route["pos_cols"], x, mod_l, final_g)


def _rope_tables():
    t = np.arange(SEQ)

    def angles(rot_dim):
        half = rot_dim // 2
        inv = ROPE_THETA ** (-jnp.arange(0, half, 2, dtype=F32) / half)
        ang = jnp.concatenate([jnp.asarray(t // GRID_W, F32)[:, None] * inv[None, :],
                               jnp.asarray(t % GRID_W, F32)[:, None] * inv[None, :]], axis=-1)
        return jnp.cos(ang), jnp.sin(ang)

    def pad_rows(a, fill):
        return jnp.concatenate([jnp.full((ROW_TILE, a.shape[1]), fill, F32), a], axis=0)

    c, s = angles(HEAD_DIM)
    cos64 = jnp.tile(jnp.concatenate([c, c], axis=-1), (1, 4))
    sin64 = jnp.tile(jnp.concatenate([s, s], axis=-1), (1, 4))
    c, s = angles(MLA_ROPE)
    one = jnp.ones((SEQ, MLA_NOPE), F32)
    zero = jnp.zeros((SEQ, MLA_NOPE), F32)
    tail1 = jnp.ones((SEQ, MLA_PAD - MLA_NOPE - MLA_ROPE), F32)
    tail0 = jnp.zeros((SEQ, MLA_PAD - MLA_NOPE - MLA_ROPE), F32)
    cosm = jnp.tile(jnp.concatenate([one, c, c, tail1], axis=-1), (1, MLA_HEADS))
    sinm = jnp.tile(jnp.concatenate([zero, s, s, tail0], axis=-1), (1, MLA_HEADS))
    blockdiag = jnp.asarray(np.kron(np.eye(4), np.ones((HEAD_DIM, HEAD_DIM))), BF16)
    place = np.zeros((256, MLA_HEADS * MLA_PAD), np.float32)
    for h in range(MLA_HEADS):
        for r in range(MLA_ROPE):
            place[MLA_KV_RANK + r, h * MLA_PAD + MLA_NOPE + r] = 1.0
    return {
        "cos64": pad_rows(cos64, 1.0), "sin64": pad_rows(sin64, 0.0),
        "cosm": pad_rows(cosm, 1.0), "sinm": pad_rows(sinm, 0.0),
        "blockdiag": blockdiag, "pe_place": jnp.asarray(place, BF16),
    }


def _pack_w_in(w):
    n_qkv = COL_DKVA + MLA_KV_RANK + MLA_ROPE
    wb = w.astype(BF16)
    pad = jnp.zeros(w.shape[:2] + (COL_GATE - n_qkv,), BF16)
    return jnp.concatenate([wb[..., :n_qkv], pad, wb[..., n_qkv:] * 0.5], axis=-1)


def _pack_mla(wqb, wkvb):
    dqh = MLA_NOPE + MLA_ROPE
    q = wqb.reshape(MLA_Q_RANK, MLA_HEADS, dqh)
    q = jnp.pad(q, ((0, 0), (0, 0), (0, MLA_PAD - dqh))).reshape(MLA_Q_RANK, MLA_HEADS * MLA_PAD)
    kv = wkvb.reshape(MLA_KV_RANK, MLA_HEADS, MLA_NOPE + MLA_V)
    k = jnp.pad(kv[:, :, :MLA_NOPE], ((0, 0), (0, 0), (0, MLA_PAD - MLA_NOPE)))
    k = k.reshape(MLA_KV_RANK, MLA_HEADS * MLA_PAD)
    v = jnp.pad(kv[:, :, MLA_NOPE:], ((0, 0), (0, 0), (0, MLA_PAD - MLA_V))).reshape(MLA_KV_RANK, MLA_HEADS * MLA_PAD)
    return q.astype(BF16), k.astype(BF16), v.astype(BF16)


def _split2_host(w):
    hi = w.astype(BF16)
    return jnp.stack([hi, (w - hi.astype(F32)).astype(BF16)])


def kernel(x, c, ctx, c_ctx, norm1_g, norm2_g, w_ada, b_ada, w_in, na_rpb, gb_qnorm, gb_knorm, wc_sink,
           mla_qnorm, mla_kvnorm, mla_wqb, mla_wkvb, w_branch, w_out, ffn_w1, ffn_w3, ffn_w2,
           moe_router, moe_w1, moe_w3, moe_w2, final_g):
    n_batch = x.shape[0]
    assert x.shape[1:] == (SEQ, D_MODEL) and ctx.shape[1:] == (CTX_LEN, D_MODEL)
    assert n_batch * CTX_LEN <= CTX_ROWS and n_batch <= 8

    ctx_rows = ctx.reshape(n_batch * CTX_LEN, D_MODEL)
    if ctx_rows.shape[0] < CTX_ROWS:
        ctx_rows = jnp.pad(ctx_rows, ((0, CTX_ROWS - ctx_rows.shape[0]), (0, 0)))
    xt = jnp.concatenate([ctx_rows, x.reshape(n_batch * SEQ, D_MODEL)], axis=0)

    cvec = jnp.zeros((16, D_MODEL), F32).at[:n_batch].set(c).at[8].set(c_ctx)
    mod = _mod_call(cvec, w_ada, b_ada).reshape(DEPTH, 16, ADA_CHUNKS, D_MODEL)
    tabs = _rope_tables()
    w_in_p = _pack_w_in(w_in)
    moe_w = tuple(w.astype(BF16) for w in (moe_w1, moe_w3, moe_w2))
    mla_scale = (MLA_NOPE + MLA_ROPE) ** -0.5
    lat_tile0 = CTX_ROWS // ROW_TILE

    for l in range(DEPTH):
        with_ctx = l < DEPTH - 1
        wq, wk, wv = _pack_mla(mla_wqb[l], mla_wkvb[l])
        lw = {
            "gq": jnp.tile(gb_qnorm[l], 4)[None, :], "gk": jnp.tile(gb_knorm[l], 2)[None, :],
            "qn": mla_qnorm[l][None, :], "kvn": mla_kvnorm[l][None, :], "wqb": wq, "wk": wk, "wv": wv,
        }
        pa, cv, gate, qb, kb, qc, kc, qd, kd, vd, vb = _inproj_call(
            xt, mod[l], norm1_g[l][None, :], w_in_p, l, tabs, lw)

        oa = _natten_call(n_batch, with_ctx, pa, _natten_bias_table(na_rpb[l]))
        ob = _gqa_attn_t_call(n_batch, with_ctx, qb, kb, vb, n_kv=GB_KV_HEADS,
                              group=GB_Q_HEADS // GB_KV_HEADS, name="attn_global")
        oc = _window_attn_call(n_batch, with_ctx, (qc, 256, 0), (kc, 128, 0), (cv, 128, 0), wc_sink[l])
        od = _dense_attn_call(n_batch, with_ctx, (qd, MLA_HEADS * MLA_PAD, 0), kd, (vd, MLA_HEADS * MLA_PAD, 0),
                              n_kv=MLA_HEADS, group=1, dq=MLA_PAD, scale=mla_scale, name="attn_mla")

        is_moe = l % 2 == 1
        router3 = None
        if is_moe:
            router3 = _split2_host(jnp.pad(moe_router[l // 2], ((0, 0), (0, LANES - N_EXPERTS))))
        first_tile = 0 if with_ctx else lat_tile0
        res = _merge_call((oa, ob, oc, od), gate, xt, mod[l], norm2_g[l][None, :],
                          (0.5 * w_branch[l]).astype(BF16), w_out[l].astype(BF16), router3, first_tile)
        x_mid, h2 = res[0], res[1]
        final_norm = l == DEPTH - 1
        if is_moe:
            route = _moe_route(res[2], h2.shape[0])
            xs, slot_w = _dispatch_call(h2, route)
            ys = _expert_call(xs, slot_w, route, *moe_w, l // 2)
            xt = _combine_call(ys, route, x_mid, mod[l], final_g[None, :],
                               0 if with_ctx else CTX_ROWS // MOE_SRC, final_norm)
        else:
            xt = _ffn_call(h2, ffn_w1, ffn_w3, ffn_w2, l // 2, x_mid, mod[l], final_g[None, :],
                           0 if with_ctx else CTX_ROWS // FFN_ROWS, final_norm)

    return xt.reshape(n_batch, SEQ, D_MODEL)
```

```python
import functools

import numpy as np
import jax
import jax.numpy as jnp
from jax import lax
from jax.experimental import pallas as pl
from jax.experimental.pallas import tpu as pltpu

F32 = jnp.float32
BF16 = jnp.bfloat16

D_MODEL = 1024
SEQ = 2048
DEPTH = 4
CTX_LEN = 256
GRID_W = 64
GRID_ROWS = SEQ // GRID_W
HEAD_DIM = 64
ROPE_THETA = 10000.0
NORM_EPS = 1e-6
NEG_INF = -1e30

NA_HEADS = 4
NA_KH = 8
NA_KW = 16
GB_Q_HEADS = 4
GB_KV_HEADS = 2
WC_Q_HEADS = 4
WC_KV_HEADS = 2
WC_WINDOW = 128
MLA_HEADS = 4
MLA_Q_RANK = 256
MLA_KV_RANK = 128
MLA_NOPE = 64
MLA_ROPE = 32
MLA_V = 64
MLA_PAD = 128
N_BRANCH = 4
BRANCH_W = 256
D_FF = 3584
N_EXPERTS = 8
ADA_CHUNKS = 6

VMEM_LIMIT_BYTES = 56 * 1024 * 1024
LANES = 128

CTX_ROWS = 2048
ROW_TILE = 512
Q_TILE = 256
NA_Q_ROWS = Q_TILE // GRID_W
NA_WIN_ROWS = 12
NA_WIN = NA_WIN_ROWS * GRID_W
WC_WIN = Q_TILE + 2 * WC_WINDOW
ATTN_KEY_CHUNK = 512

COL_AQ, COL_AK, COL_AV = 0, 256, 512
COL_BQ, COL_BK, COL_BV = 768, 1024, 1152
COL_CQ, COL_CK, COL_CV = 1280, 1536, 1664
COL_DQA, COL_DKVA, COL_GATE = 1792, 2048, 2304
GATE_W = N_BRANCH * D_MODEL
P_WIDTH = COL_GATE + GATE_W
QKV_CHUNK = COL_GATE // 2
FF_CHUNK = 512
MOE_FF_CHUNK = 1792


def _params(*sem):
    return pltpu.CompilerParams(dimension_semantics=sem, vmem_limit_bytes=VMEM_LIMIT_BYTES)


def _dot(a, b):
    return jnp.dot(a, b, preferred_element_type=F32)


def _dot_nt(a, b):
    return lax.dot_general(a, b, (((1,), (1,)), ((), ())), preferred_element_type=F32)


def _split3(x):
    hi = x.astype(BF16)
    r1 = x - hi.astype(F32)
    mid = r1.astype(BF16)
    lo = (r1 - mid.astype(F32)).astype(BF16)
    return hi, mid, lo


def _sigmoid(x):
    return 0.5 * jnp.tanh(0.5 * x) + 0.5


ADA_TN = 1536


def _mod_kernel(c_ref, w_ref, b_ref, o_ref):
    c = c_ref[...]
    sc = (c * _sigmoid(c)).astype(BF16)
    o_ref[0] = _dot(sc, w_ref[0].astype(BF16)) + b_ref[0]


def _mod_call(cvec, w_ada, b_ada):
    n = ADA_CHUNKS * D_MODEL
    return pl.pallas_call(
        _mod_kernel,
        grid=(DEPTH, n // ADA_TN),
        in_specs=[
            pl.BlockSpec((16, D_MODEL), lambda l, j: (0, 0)),
            pl.BlockSpec((1, D_MODEL, ADA_TN), lambda l, j: (l, 0, j)),
            pl.BlockSpec((1, 1, ADA_TN), lambda l, j: (l, 0, j)),
        ],
        out_specs=pl.BlockSpec((1, 16, ADA_TN), lambda l, j: (l, 0, j)),
        out_shape=jax.ShapeDtypeStruct((DEPTH, 16, n), F32),
        compiler_params=_params("arbitrary", "arbitrary"),
        name="adaln_mod",
    )(cvec, w_ada, b_ada.reshape(DEPTH, 1, n))


def _mod_row(i, first_tile):
    n_ctx = CTX_ROWS // ROW_TILE
    t = i + first_tile
    return jnp.where(t < n_ctx, 8, (t - n_ctx) // (SEQ // ROW_TILE))


def _norm_mod(x, g, shift, scale):
    ms = jnp.mean(x * x, axis=-1, keepdims=True)
    y = x * lax.rsqrt(ms + NORM_EPS) * g
    return y * (1.0 + scale) + shift


def _rope(x, cos, sin, half, first_mask):
    w = x.shape[-1]
    fwd = pltpu.roll(x, w - half, 1)
    bwd = pltpu.roll(x, half, 1)
    rot = jnp.where(first_mask, -fwd, bwd)
    return x * cos + rot * sin


def _head_rms(x, gain, blockdiag):
    hi, mid, lo = _split3(x * x)
    ss = _dot(hi, blockdiag) + _dot(mid, blockdiag) + _dot(lo, blockdiag)
    return x * lax.rsqrt(ss * (1.0 / HEAD_DIM) + NORM_EPS) * gain


def _inproj_kernel(x_ref, mod_ref, g_ref, w_ref, wg_ref,
                   cos64_ref, sin64_ref, cosm_ref, sinm_ref,
                   gq_ref, gk_ref, bd_ref, qn_ref, kvn_ref, wqb_ref, wk_ref, wv_ref, pe_ref,
                   pa_ref, cv_ref, gate_ref, qb_ref, kb_ref, qc_ref, kc_ref, qd_ref, kd_ref, vd_ref, vb_ref):
    m = mod_ref[0]
    hb = _norm_mod(x_ref[...], g_ref[...], m[0:1], m[1:2]).astype(BF16)

    def proj(c0, c1):
        return _dot(hb, w_ref[0, :, c0:c1])

    for c0 in range(0, GATE_W, D_MODEL):
        gate_ref[:, c0:c0 + D_MODEL] = _dot(hb, wg_ref[0, :, c0:c0 + D_MODEL]).astype(BF16)
    pa_ref[...] = proj(COL_AQ, COL_BQ).astype(BF16)

    scale = HEAD_DIM ** -0.5
    half = HEAD_DIM // 2
    cos64, sin64 = cos64_ref[...], sin64_ref[...]
    lane = lax.broadcasted_iota(jnp.int32, (1, 256), 1)
    first64 = (lane % HEAD_DIM) < half
    bd = bd_ref[...]
    bc = proj(COL_BQ, COL_DQA)
    bq = _head_rms(bc[:, 0:256], gq_ref[...], bd)
    qb_ref[...] = (_rope(bq, cos64, sin64, half, first64) * scale).T.astype(BF16)
    bk = _head_rms(bc[:, 256:384], gk_ref[...], bd[:128, :128])
    kb_ref[...] = _rope(bk, cos64[:, :128], sin64[:, :128], half, first64[:, :128]).astype(BF16)
    qc_ref[...] = (_rope(bc[:, 512:768], cos64, sin64, half, first64) * scale).astype(BF16)
    kc_ref[...] = _rope(bc[:, 768:896], cos64[:, :128], sin64[:, :128], half, first64[:, :128]).astype(BF16)
    cv_ref[...] = bc[:, 896:1024].astype(BF16)

    cosm, sinm = cosm_ref[...], sinm_ref[...]
    lane_m = lax.broadcasted_iota(jnp.int32, (1, MLA_HEADS * MLA_PAD), 1) % MLA_PAD
    first_m = lane_m < (MLA_NOPE + MLA_ROPE // 2)
    d = proj(COL_DQA, COL_GATE)
    dqa = d[:, 0:MLA_Q_RANK]
    qn = dqa * lax.rsqrt(jnp.mean(dqa * dqa, axis=-1, keepdims=True) + NORM_EPS) * qn_ref[...]
    dq = _dot(qn.astype(BF16), wqb_ref[...])
    qd_ref[...] = _rope(dq, cosm, sinm, MLA_ROPE // 2, first_m).astype(BF16)

    dkva = d[:, MLA_Q_RANK:]
    kvc = dkva[:, :MLA_KV_RANK]
    kvn = (kvc * lax.rsqrt(jnp.mean(kvc * kvc, axis=-1, keepdims=True) + NORM_EPS) * kvn_ref[...]).astype(BF16)
    dk = _dot(kvn, wk_ref[...]) + _dot(dkva.astype(BF16), pe_ref[...])
    kd_ref[...] = _rope(dk, cosm, sinm, MLA_ROPE // 2, first_m).T.astype(BF16)
    ones_half = lane_m >= HEAD_DIM
    vd_ref[...] = jnp.where(ones_half, 1.0, _dot(kvn, wv_ref[...])).astype(BF16)
    bv = bc[:, 384:512]
    vb_ref[:LANES, :] = jnp.where(ones_half[:, :LANES], 1.0, bv).T.astype(BF16)
    vb_ref[LANES:, :] = jnp.where(ones_half[:, :LANES], 1.0, pltpu.roll(bv, HEAD_DIM, 1)).T.astype(BF16)


def _rope_row_block(i):
    n_ctx = CTX_ROWS // ROW_TILE
    return jnp.where(i < n_ctx, 0, 1 + (i - n_ctx) % (SEQ // ROW_TILE))


def _inproj_call(x, mod_l, g, w_in_p, layer, tabs, lw):
    t = x.shape[0]
    mw = MLA_HEADS * MLA_PAD

    def full(shape):
        return pl.BlockSpec(shape, lambda i: (0,) * len(shape))

    def rows(width):
        return pl.BlockSpec((ROW_TILE, width), lambda i: (i, 0))

    def tab(width):
        return pl.BlockSpec((ROW_TILE, width), lambda i: (_rope_row_block(i), 0))

    def cols(height):
        return pl.BlockSpec((height, ROW_TILE), lambda i: (0, i))

    outs = [(COL_BQ, False), (128, False), (GATE_W, False),
            (256, True), (128, False), (256, False), (128, False), (mw, False), (mw, True), (mw, False),
            (GB_KV_HEADS * LANES, True)]
    return pl.pallas_call(
        _inproj_kernel,
        grid=(t // ROW_TILE,),
        in_specs=[
            rows(D_MODEL),
            pl.BlockSpec((1, ADA_CHUNKS, D_MODEL), lambda i: (_mod_row(i, 0), 0, 0)),
            full((1, D_MODEL)),
            pl.BlockSpec((1, D_MODEL, COL_GATE), lambda i: (layer, 0, 0), pipeline_mode=pl.Buffered(1)),
            pl.BlockSpec((1, D_MODEL, GATE_W), lambda i: (layer, 0, 0), pipeline_mode=pl.Buffered(1)),
            tab(256), tab(256), tab(mw), tab(mw),
            full((1, 256)), full((1, 128)), full((256, 256)), full((1, MLA_Q_RANK)), full((1, MLA_KV_RANK)),
            full((MLA_Q_RANK, mw)), full((MLA_KV_RANK, mw)), full((MLA_KV_RANK, mw)), full((256, mw)),
        ],
        out_specs=[cols(w) if tr else rows(w) for w, tr in outs],
        out_shape=[jax.ShapeDtypeStruct((w, t) if tr else (t, w), BF16) for w, tr in outs],
        compiler_params=_params("arbitrary"),
        name="inproj",
    )(x, mod_l, g, w_in_p[0], w_in_p[1], tabs["cos64"], tabs["sin64"], tabs["cosm"], tabs["sinm"],
      lw["gq"], lw["gk"], tabs["blockdiag"], lw["qn"], lw["kvn"], lw["wqb"], lw["wk"], lw["wv"], tabs["pe_place"])


def _attend(q, segs, sink=None, scale=None):
    chunks = segs
    scores = []
    m = None
    for k, _, bias in chunks:
        s = _dot_nt(q, k)
        if scale is not None:
            s = s * scale
        if bias is not None:
            s = s + bias
        scores.append(s)
        ms = jnp.max(s, axis=-1, keepdims=True)
        m = ms if m is None else jnp.maximum(m, ms)
    if sink is not None:
        m = jnp.maximum(m, sink)
    denom = None
    out = None
    for s, (_, v, _) in zip(scores, chunks):
        p = jnp.exp(s - m)
        ps = jnp.sum(p, axis=-1, keepdims=True)
        pv = _dot(p.astype(BF16), v)
        denom = ps if denom is None else denom + ps
        out = pv if out is None else out + pv
    if sink is not None:
        denom = denom + jnp.exp(sink - m)
    return out / denom


def _gqa_heads(q_ref, o_ref, sink_ref, n_kv, group, dq, dv, scale, seg_fn):
    tq = q_ref.shape[0]
    for h in range(n_kv):
        heads = [h * group + g for g in range(group)]
        q = jnp.concatenate([q_ref[:, a * dq:(a + 1) * dq] for a in heads], axis=0) if group > 1 \
            else q_ref[:, h * dq:(h + 1) * dq]
        sink = None
        if sink_ref is not None:
            sink = jnp.concatenate([jnp.full((tq, 1), sink_ref[a], F32) for a in heads], axis=0)
        o = _attend(q, seg_fn(h), sink=sink, scale=scale)
        for g, a in enumerate(heads):
            o_ref[:, a * dv:(a + 1) * dv] = o[g * tq:(g + 1) * tq].astype(BF16)


def _dense_attn_kernel(q_ref, ktl_ref, vl_ref, ktc_ref, vc_ref, o_ref, *, n_kv, group, dq, scale, ctx_tile):
    tq = q_ref.shape[0]
    hd = HEAD_DIM

    def run(with_latent):
        for h in range(n_kv):
            heads = [h * group + g for g in range(group)]
            q = jnp.concatenate([q_ref[:, a * dq:(a + 1) * dq] for a in heads], axis=0) if group > 1 \
                else q_ref[:, h * dq:(h + 1) * dq]
            segs = [(ktc_ref[h * dq:(h + 1) * dq, :], vc_ref[:, h * LANES:(h + 1) * LANES])]
            if with_latent:
                segs = [(ktl_ref[h * dq:(h + 1) * dq, c0:c0 + ATTN_KEY_CHUNK],
                         vl_ref[c0:c0 + ATTN_KEY_CHUNK, h * LANES:(h + 1) * LANES])
                        for c0 in range(0, SEQ, ATTN_KEY_CHUNK)] + segs
            scores = []
            m = None
            for kt, _ in segs:
                s = _dot(q, kt)
                if scale is not None:
                    s = s * scale
                scores.append(s)
                ms = jnp.max(s, axis=-1, keepdims=True)
                m = ms if m is None else jnp.maximum(m, ms)
            acc = None
            for s, (_, v) in zip(scores, segs):
                pv = _dot(jnp.exp(s - m).astype(BF16), v)
                acc = pv if acc is None else acc + pv
            o = acc * pltpu.roll(1.0 / acc, hd, 1)
            for g, a in enumerate(heads):
                o_ref[:, a * hd:(a + 1) * hd] = o[g * tq:(g + 1) * tq, :hd].astype(BF16)

    if ctx_tile:
        j = pl.program_id(1)
        pl.when(j == 0)(lambda: run(False))
        pl.when(j > 0)(lambda: run(True))
    else:
        run(True)


def _gqa_attn_t_kernel(qt_ref, kl_ref, vtl_ref, kc_ref, vtc_ref, o_ref, *, n_kv, group, ctx_tile):
    tq = qt_ref.shape[1]
    hd = HEAD_DIM
    kw = kl_ref.shape[1]
    n = group * tq

    def run(with_latent):
        for h in range(n_kv):
            heads = [h * group + g for g in range(group)]
            qcat = jnp.concatenate([qt_ref[a * hd:(a + 1) * hd, :] for a in heads], axis=1)
            parts = [jnp.zeros((h * hd, n), BF16), qcat, jnp.zeros((kw - (h + 1) * hd, n), BF16)]
            wq = jnp.concatenate([p for p in parts if p.shape[0] > 0], axis=0)
            vrows = slice(h * LANES, (h + 1) * LANES)
            segs = [(kc_ref[...], vtc_ref[vrows, :])]
            if with_latent:
                segs = [(kl_ref[c0:c0 + ATTN_KEY_CHUNK, :], vtl_ref[vrows, c0:c0 + ATTN_KEY_CHUNK])
                        for c0 in range(0, SEQ, ATTN_KEY_CHUNK)] + segs
            scores = []
            m = None
            for k, _ in segs:
                s = _dot(k, wq)
                scores.append(s)
                ms = jnp.max(s, axis=0, keepdims=True)
                m = ms if m is None else jnp.maximum(m, ms)
            acc = None
            for s, (_, vt) in zip(scores, segs):
                pv = _dot(vt, jnp.exp(s - m).astype(BF16))
                acc = pv if acc is None else acc + pv
            o = acc[:hd] * (1.0 / acc[hd:hd + 1])
            for g, a in enumerate(heads):
                o_ref[:, a * hd:(a + 1) * hd] = o[:, g * tq:(g + 1) * tq].T.astype(BF16)

    if ctx_tile:
        j = pl.program_id(1)
        pl.when(j == 0)(lambda: run(False))
        pl.when(j > 0)(lambda: run(True))
    else:
        run(True)


def _q_row_block(b, j, n_batch, with_ctx):
    per_batch = SEQ // Q_TILE
    lat0 = CTX_ROWS // Q_TILE
    if with_ctx:
        return jnp.where(j == 0, b, lat0 + b * per_batch + j - 1)
    return lat0 + b * per_batch + j


def _attn_specs(n_batch, with_ctx, q, k, v, o_width, order_bj=True):
    def ix(f):
        return (lambda b, j: f(b, j)) if order_bj else (lambda j, b: f(b, j))

    def qspec(width, col):
        return pl.BlockSpec((Q_TILE, width), ix(lambda b, j: (_q_row_block(b, j, n_batch, with_ctx), col // width)))

    def lat(width, col):
        return pl.BlockSpec((SEQ, width), ix(lambda b, j: (CTX_ROWS // SEQ + b, col // width)))

    def ctx(width, col):
        return pl.BlockSpec((CTX_LEN, width), ix(lambda b, j: (b, col // width)))

    in_specs = [qspec(q[1], q[2]), lat(k[1], k[2]), lat(v[1], v[2]), ctx(k[1], k[2]), ctx(v[1], v[2])]
    args = [q[0], k[0], v[0], k[0], v[0]]
    if with_ctx:
        return in_specs, args, qspec(o_width, 0)
    out_spec = pl.BlockSpec((Q_TILE, o_width), ix(lambda b, j: (b * (SEQ // Q_TILE) + j, 0)))
    return in_specs, args, out_spec


def _attn_out_rows(t, with_ctx):
    return t if with_ctx else t - CTX_ROWS


def _dense_attn_call(n_batch, with_ctx, q, kt, v, *, n_kv, group, dq, scale=None, name):
    t = q[0].shape[0]
    in_specs, args, out_spec = _attn_specs(n_batch, with_ctx, q, v, v, 256)
    kt_rows = kt.shape[0]
    in_specs[1] = pl.BlockSpec((kt_rows, SEQ), lambda b, j: (0, CTX_ROWS // SEQ + b))
    in_specs[3] = pl.BlockSpec((kt_rows, CTX_LEN), lambda b, j: (0, b))
    args[1] = args[3] = kt
    kern = functools.partial(_dense_attn_kernel, n_kv=n_kv, group=group, dq=dq, scale=scale, ctx_tile=with_ctx)
    return pl.pallas_call(
        kern,
        grid=(n_batch, SEQ // Q_TILE + (1 if with_ctx else 0)),
        in_specs=in_specs,
        out_specs=out_spec,
        out_shape=jax.ShapeDtypeStruct((_attn_out_rows(t, with_ctx), 256), BF16),
        compiler_params=_params("arbitrary", "arbitrary"),
        name=name,
    )(*args)


def _gqa_attn_t_call(n_batch, with_ctx, qt, k, vt, *, n_kv, group, name):
    t = k.shape[0]
    kspec = (k, k.shape[1], 0)
    in_specs, args, out_spec = _attn_specs(n_batch, with_ctx, kspec, kspec, kspec, 256)
    in_specs[0] = pl.BlockSpec((qt.shape[0], Q_TILE), lambda b, j: (0, _q_row_block(b, j, n_batch, with_ctx)))
    in_specs[2] = pl.BlockSpec((vt.shape[0], SEQ), lambda b, j: (0, CTX_ROWS // SEQ + b))
    in_specs[4] = pl.BlockSpec((vt.shape[0], CTX_LEN), lambda b, j: (0, b))
    args[0] = qt
    args[2] = args[4] = vt
    return pl.pallas_call(
        functools.partial(_gqa_attn_t_kernel, n_kv=n_kv, group=group, ctx_tile=with_ctx),
        grid=(n_batch, SEQ // Q_TILE + (1 if with_ctx else 0)),
        in_specs=in_specs,
        out_specs=out_spec,
        out_shape=jax.ShapeDtypeStruct((_attn_out_rows(t, with_ctx), 256), BF16),
        compiler_params=_params("arbitrary", "arbitrary"),
        name=name,
    )(*args)


def _window_attn_kernel(sink_ref, q_ref, kl_ref, vl_ref, kc_ref, vc_ref, o_ref, *, ctx_tile):
    j = pl.program_id(1)
    hd = HEAD_DIM
    group = WC_Q_HEADS // WC_KV_HEADS

    def ctx_run():
        def segs(h):
            return [(kc_ref[:, h * hd:(h + 1) * hd], vc_ref[:, h * hd:(h + 1) * hd], None)]
        _gqa_heads(q_ref, o_ref, sink_ref, WC_KV_HEADS, group, hd, hd, None, segs)

    def lat_run():
        i = j - 1 if ctx_tile else j
        start = pl.multiple_of(jnp.clip(i * Q_TILE - WC_WINDOW, 0, SEQ - WC_WIN), WC_WINDOW)
        qpos = i * Q_TILE + lax.broadcasted_iota(jnp.int32, (group * Q_TILE, WC_WIN), 0) % Q_TILE
        kpos = start + lax.broadcasted_iota(jnp.int32, (group * Q_TILE, WC_WIN), 1)
        bias = jnp.where(jnp.abs(qpos - kpos) <= WC_WINDOW, 0.0, NEG_INF).astype(F32)

        def segs(h):
            return [(kl_ref[pl.ds(start, WC_WIN), h * hd:(h + 1) * hd],
                     vl_ref[pl.ds(start, WC_WIN), h * hd:(h + 1) * hd], bias),
                    (kc_ref[:, h * hd:(h + 1) * hd], vc_ref[:, h * hd:(h + 1) * hd], None)]
        _gqa_heads(q_ref, o_ref, sink_ref, WC_KV_HEADS, group, hd, hd, None, segs)

    if ctx_tile:
        pl.when(j == 0)(ctx_run)
        pl.when(j > 0)(lat_run)
    else:
        lat_run()


def _window_attn_call(n_batch, with_ctx, q, k, v, sink):
    t = q[0].shape[0]
    in_specs, args, out_spec = _attn_specs(n_batch, with_ctx, q, k, v, 256)
    return pl.pallas_call(
        functools.partial(_window_attn_kernel, ctx_tile=with_ctx),
        grid=(n_batch, SEQ // Q_TILE + (1 if with_ctx else 0)),
        in_specs=[pl.BlockSpec(memory_space=pltpu.SMEM)] + in_specs,
        out_specs=out_spec,
        out_shape=jax.ShapeDtypeStruct((_attn_out_rows(t, with_ctx), 256), BF16),
        compiler_params=_params("arbitrary", "arbitrary"),
        name="attn_window",
    )(sink, *args)


def _natten_kernel(q_ref, kl_ref, vl_ref, kc_ref, vc_ref, tab_ref, o_ref, bias_ref, *, ctx_tile):
    j = pl.program_id(0)
    hd = HEAD_DIM
    scale = HEAD_DIM ** -0.5

    def build_bias(i):
        ws = jnp.clip(NA_Q_ROWS * i - NA_KH // 2, 0, GRID_ROWS - NA_WIN_ROWS)
        for a in range(NA_Q_ROWS):
            r = NA_Q_ROWS * i + a
            krow0 = jnp.clip(r - NA_KH // 2, 0, GRID_ROWS - NA_KH)
            for pair in range(NA_WIN_ROWS // 2):
                idx = []
                for side in range(2):
                    kr = ws + 2 * pair + side
                    in_rows = (kr >= krow0) & (kr < krow0 + NA_KH)
                    idx.append(jnp.where(in_rows, kr - r + NA_KH - 1, NA_NO_ROW))
                for h in range(NA_HEADS):
                    bias_ref[h, a * GRID_W:(a + 1) * GRID_W, pair * LANES:(pair + 1) * LANES] = (
                        tab_ref[0, h, idx[0]] + tab_ref[1, h, idx[1]])

    def ctx_run():
        for h in range(NA_HEADS):
            sl = slice(h * hd, (h + 1) * hd)
            o = _attend(q_ref[:, sl] * scale, [(kc_ref[:, sl], vc_ref[:, sl], None)])
            o_ref[:, sl] = o.astype(BF16)

    def lat_run():
        i = j - 1 if ctx_tile else j
        ws = jnp.clip(NA_Q_ROWS * i - NA_KH // 2, 0, GRID_ROWS - NA_WIN_ROWS)
        start = pl.multiple_of(ws * GRID_W, GRID_W)
        pl.when(pl.program_id(1) == 0)(lambda: build_bias(i))
        for h in range(NA_HEADS):
            sl = slice(h * hd, (h + 1) * hd)
            segs = [(kl_ref[pl.ds(start, NA_WIN), sl], vl_ref[pl.ds(start, NA_WIN), sl], bias_ref[h]),
                    (kc_ref[:, sl], vc_ref[:, sl], None)]
            o_ref[:, sl] = _attend(q_ref[:, sl] * scale, segs).astype(BF16)

    if ctx_tile:
        pl.when(j == 0)(ctx_run)
        pl.when(j > 0)(lat_run)
    else:
        lat_run()


NA_NO_ROW = 2 * NA_KH - 1


def _natten_call(n_batch, with_ctx, p, bias_blocks):
    t = p.shape[0]
    q, k, v = (p, 256, COL_AQ), (p, 256, COL_AK), (p, 256, COL_AV)
    in_specs, args, out_spec = _attn_specs(n_batch, with_ctx, q, k, v, 256, order_bj=False)
    in_specs.append(pl.BlockSpec(bias_blocks.shape, lambda j, b: (0,) * bias_blocks.ndim))
    return pl.pallas_call(
        functools.partial(_natten_kernel, ctx_tile=with_ctx),
        grid=(SEQ // Q_TILE + (1 if with_ctx else 0), n_batch),
        in_specs=in_specs,
        out_specs=out_spec,
        out_shape=jax.ShapeDtypeStruct((_attn_out_rows(t, with_ctx), 256), BF16),
        scratch_shapes=[pltpu.VMEM((NA_HEADS, Q_TILE, NA_WIN), F32)],
        compiler_params=_params("arbitrary", "arbitrary"),
        name="attn_natten",
    )(*args, bias_blocks)


def _natten_bias_table(rpb):
    n_dr, n_dc = 2 * NA_KH - 1, 2 * NA_KW - 1
    col = np.arange(GRID_W)
    dc = np.clip(col[None, :] - col[:, None] + NA_KW - 1, 0, n_dc - 1)
    onehot = jnp.asarray(dc[None] == np.arange(n_dc)[:, None, None], F32)
    blocks = jnp.einsum('hdc,cqk->hdqk', rpb.astype(F32), onehot, precision=lax.Precision.HIGHEST)
    cstart = np.clip(col - NA_KW // 2, 0, GRID_W - NA_KW)
    col_ok = (col[None, :] >= cstart[:, None]) & (col[None, :] < cstart[:, None] + NA_KW)
    blocks = jnp.where(col_ok[None, None], blocks, NEG_INF)
    blocks = jnp.concatenate([blocks, jnp.full((NA_HEADS, 1, GRID_W, GRID_W), NEG_INF, F32)], axis=1)
    zero = jnp.zeros_like(blocks)
    return jnp.stack([jnp.concatenate([blocks, zero], axis=-1), jnp.concatenate([zero, blocks], axis=-1)])


def _merge_kernel(*refs, with_router):
    if with_router:
        (oa_ref, ob_ref, oc_ref, od_ref, gate_ref, x_ref, mod_ref, g2_ref, wb_ref, wo_ref, rt_ref,
         xo_ref, h2_ref, comb_ref) = refs
    else:
        (oa_ref, ob_ref, oc_ref, od_ref, gate_ref, x_ref, mod_ref, g2_ref, wb_ref, wo_ref,
         xo_ref, h2_ref) = refs
    m = mod_ref[0]
    acc = None
    for n, o_ref in enumerate((oa_ref, ob_ref, oc_ref, od_ref)):
        half_y = _dot(o_ref[...], wb_ref[n])
        t = jnp.tanh(gate_ref[:, n * D_MODEL:(n + 1) * D_MODEL].astype(F32)) + 1.0
        acc = t * half_y if acc is None else acc + t * half_y
    mix = _dot(acc.astype(BF16), wo_ref[...])
    x = x_ref[...] + m[2:3] * mix
    xo_ref[...] = x
    h2 = _norm_mod(x, g2_ref[...], m[3:4], m[4:5])
    h2_ref[...] = h2.astype(BF16)
    if with_router:
        hh, hm, _ = _split3(h2)
        rh, rm = rt_ref[0], rt_ref[1]
        logits = _dot(hh, rh) + (_dot(hh, rm) + _dot(hm, rh))
        lane = lax.broadcasted_iota(jnp.int32, logits.shape, 1).astype(F32)
        logits = jnp.where(lane < N_EXPERTS, logits, NEG_INF)
        m1 = jnp.max(logits, axis=-1, keepdims=True)
        i1 = jnp.min(jnp.where(logits == m1, lane, float(LANES)), axis=-1, keepdims=True)
        rest = jnp.where(lane == i1, NEG_INF, logits)
        m2 = jnp.max(rest, axis=-1, keepdims=True)
        i2 = jnp.min(jnp.where(rest == m2, lane, float(LANES)), axis=-1, keepdims=True)
        e = jnp.exp(m2 - m1)
        w1 = 1.0 / (1.0 + e)
        w2 = e / (1.0 + e)
        comb_ref[...] = (jnp.where(lane == 0.0, i1, 0.0) + jnp.where(lane == 1.0, i2, 0.0)
                         + jnp.where(lane == 2.0, w1, 0.0) + jnp.where(lane == 3.0, w2, 0.0))


def _merge_call(outs, gate, x, mod_l, g2, wb, wo, router3, first_tile):
    t = x.shape[0]
    n_rows = t - first_tile * ROW_TILE

    def rows(width, col=0):
        return pl.BlockSpec((ROW_TILE, width), lambda i: (i + first_tile, col // width))

    def orow(width):
        return pl.BlockSpec((ROW_TILE, width), lambda i: (i, 0))

    def full(shape):
        return pl.BlockSpec(shape, lambda i: (0,) * len(shape))

    in_specs = [orow(256)] * 4 + [
        rows(GATE_W),
        rows(D_MODEL),
        pl.BlockSpec((1, ADA_CHUNKS, D_MODEL), lambda i: (_mod_row(i, first_tile), 0, 0)),
        full((1, D_MODEL)), full((N_BRANCH, BRANCH_W, D_MODEL)), full((D_MODEL, D_MODEL)),
    ]
    args = list(outs) + [gate, x, mod_l, g2, wb, wo]
    out_specs = [orow(D_MODEL), orow(D_MODEL)]
    out_shape = [jax.ShapeDtypeStruct((n_rows, D_MODEL), F32), jax.ShapeDtypeStruct((n_rows, D_MODEL), BF16)]
    if router3 is not None:
        in_specs.append(full((2, D_MODEL, LANES)))
        args.append(router3)
        out_specs.append(orow(LANES))
        out_shape.append(jax.ShapeDtypeStruct((n_rows, LANES), F32))
    return pl.pallas_call(
        functools.partial(_merge_kernel, with_router=router3 is not None),
        grid=(n_rows // ROW_TILE,),
        in_specs=in_specs,
        out_specs=out_specs,
        out_shape=out_shape,
        compiler_params=_params("arbitrary"),
        name="merge",
    )(*args)


FFN_ROWS = 1024


def _swiglu_chunk(h, w1, w3, w2):
    half_a = 0.5 * _dot(h, w1)
    b = _dot(h, w3)
    return _dot((half_a * (jnp.tanh(half_a) + 1.0) * b).astype(BF16), w2)


def _residual_out(x, gate, update, final_norm, final_g):
    x = x + gate * update
    if final_norm:
        ms = jnp.mean(x * x, axis=-1, keepdims=True)
        x = x * lax.rsqrt(ms + NORM_EPS) * final_g
    return x


def _ffn_kernel(h_ref, w1_ref, w3_ref, w2_ref, x_ref, mod_ref, fg_ref, o_ref, acc_ref, *, final_norm):
    f = pl.program_id(1)

    @pl.when(f == 0)
    def _():
        acc_ref[...] = jnp.zeros_like(acc_ref)

    acc_ref[...] += _swiglu_chunk(h_ref[...], w1_ref[0].astype(BF16), w3_ref[0].astype(BF16),
                                  w2_ref[0].astype(BF16))

    @pl.when(f == pl.num_programs(1) - 1)
    def _():
        o_ref[...] = _residual_out(x_ref[...], mod_ref[0][5:6], acc_ref[...], final_norm, fg_ref[...])


def _tile_mod_row(i, first_tile, tile_rows):
    n_ctx = CTX_ROWS // tile_rows
    t = i + first_tile
    return jnp.where(t < n_ctx, 8, (t - n_ctx) // (SEQ // tile_rows))


def _ffn_call(h2, w1, w3, w2, layer, x, mod_l, final_g, first_row_tile, final_norm):
    t = x.shape[0]
    rows = lambda width: pl.BlockSpec((FFN_ROWS, width), lambda i, f: (i, 0))
    return pl.pallas_call(
        functools.partial(_ffn_kernel, final_norm=final_norm),
        grid=(t // FFN_ROWS, D_FF // FF_CHUNK),
        in_specs=[
            rows(D_MODEL),
            pl.BlockSpec((1, D_MODEL, FF_CHUNK), lambda i, f: (layer, 0, f)),
            pl.BlockSpec((1, D_MODEL, FF_CHUNK), lambda i, f: (layer, 0, f)),
            pl.BlockSpec((1, FF_CHUNK, D_MODEL), lambda i, f: (layer, f, 0)),
            rows(D_MODEL),
            pl.BlockSpec((1, ADA_CHUNKS, D_MODEL),
                         lambda i, f: (_tile_mod_row(i, first_row_tile, FFN_ROWS), 0, 0)),
            pl.BlockSpec((1, D_MODEL), lambda i, f: (0, 0)),
        ],
        out_specs=rows(D_MODEL),
        out_shape=jax.ShapeDtypeStruct((t, D_MODEL), F32),
        scratch_shapes=[pltpu.VMEM((FFN_ROWS, D_MODEL), F32)],
        compiler_params=_params("arbitrary", "arbitrary"),
        name="ffn",
    )(h2, w1, w3, w2, x, mod_l, final_g)


MOE_TILE = 512
MOE_SRC = 256
MOE_CMB_WIN = MOE_SRC + 16
MOE_CMB_SMALL = 144
MOE_DSP_SMALL = 144
MOE_DSP_LARGE = MOE_SRC + 16
TOP_K = 2


def _moe_route(sel, n_tok):
    n_tiles = TOP_K * n_tok // MOE_TILE + N_EXPERTS
    e = sel[:, 0:TOP_K].astype(jnp.int32)
    flat_e = e.reshape(-1)
    onehot = (flat_e[None, :] == jnp.arange(N_EXPERTS)[:, None]).astype(jnp.int32)
    csum = jnp.cumsum(onehot, axis=1)
    counts = csum[:, -1]
    padded = (counts + MOE_TILE - 1) // MOE_TILE * MOE_TILE
    seg_end = jnp.cumsum(padded)
    seg_start = seg_end - padded
    pos = jnp.sum(onehot * (seg_start[:, None] + csum - onehot), axis=0)
    pos2 = pos.reshape(n_tok, TOP_K)
    tile_ix = jnp.arange(n_tiles)
    tile_expert = jnp.minimum(jnp.sum(tile_ix[:, None] >= (seg_end // MOE_TILE)[None, :], axis=1), N_EXPERTS - 1)
    n_valid = seg_end[-1] // MOE_TILE
    n_src = n_tok // MOE_SRC
    per_chunk = MOE_SRC * TOP_K
    before = jnp.concatenate([jnp.zeros((1, N_EXPERTS), jnp.int32), csum[:, per_chunk - 1::per_chunk].T], axis=0)
    run_start = seg_start[None, :] + before[:-1]
    run_cnt = before[1:] - before[:-1]
    cum = jnp.take(before, tile_expert, axis=1)
    local0 = tile_ix * MOE_TILE - jnp.take(seg_start, tile_expert)
    local1 = jnp.minimum(local0 + MOE_TILE, jnp.take(counts, tile_expert))
    c_lo = jnp.sum(cum[1:] <= local0[None, :], axis=0)
    c_hi = jnp.sum(cum[:-1] < local1[None, :], axis=0) - 1
    win = jnp.minimum(run_start // 16 * 16, n_tiles * MOE_TILE - MOE_CMB_WIN)
    win_s = jnp.minimum(run_start // 16 * 16, n_tiles * MOE_TILE - MOE_CMB_SMALL)
    cmb_small = jnp.all(run_start - win_s + run_cnt <= MOE_CMB_SMALL, axis=1)

    def token_rows(a, dtype):
        rows = jnp.transpose(a.reshape(n_src, MOE_SRC, TOP_K), (0, 2, 1)).astype(dtype)
        return jnp.zeros((n_src, 8, MOE_SRC), dtype).at[:, :TOP_K, :].set(rows)

    i32 = lambda a: a.astype(jnp.int32)
    return {
        "n_tiles": n_tiles, "tile_expert": i32(tile_expert), "n_valid": i32(n_valid).reshape(1),
        "c_lo": i32(c_lo), "c_hi": i32(c_hi), "run_start": i32(run_start.reshape(-1)),
        "run_cnt": i32(run_cnt.reshape(-1)), "win": i32(win.reshape(-1)),
        "run_off": i32((run_start - win).reshape(-1)),
        "win_s": i32(win_s.reshape(-1)), "run_off_s": i32((run_start - win_s).reshape(-1)),
        "cmb_small": i32(cmb_small),
        "pos_rows": token_rows(pos2, jnp.int32), "w_rows": token_rows(sel[:, TOP_K:2 * TOP_K], F32),
        "pos_cols": i32(pos2),
    }


def _dispatch_kernel(clo_ref, chi_ref, te_ref, rs_ref, rc_ref, h_ref, pos_ref, w_ref, o_ref, sw_ref, acc_ref):
    i = pl.program_id(0)
    base = i * MOE_TILE
    expert = te_ref[i]
    acc_ref[...] = jnp.zeros_like(acc_ref)
    sw_ref[...] = jnp.zeros_like(sw_ref)

    c_last = chi_ref[i]

    def item(c):
        run0 = rs_ref[c * N_EXPERTS + expert]
        lo = jnp.maximum(run0, base) - base
        hi = jnp.minimum(run0 + rc_ref[c * N_EXPERTS + expert], base + MOE_TILE) - base
        small = hi - jnp.minimum(lo // 16 * 16, MOE_TILE - MOE_DSP_SMALL) <= MOE_DSP_SMALL
        return lo, hi, small

    def window(c, lo, rows, live):
        pos = pos_ref[c]
        wts = w_ref[c]
        h = h_ref[pl.ds(pl.multiple_of(c * MOE_SRC, MOE_SRC), MOE_SRC), :]
        ws = pl.multiple_of(jnp.minimum(lo // 16 * 16, MOE_TILE - rows), 16)
        slot = base + ws + lax.broadcasted_iota(jnp.int32, (rows, MOE_SRC), 0)
        slot = jnp.where(live, slot, -1)
        hit0 = pos[0:1, :] == slot
        hit1 = pos[1:2, :] == slot
        acc_ref[pl.ds(ws, rows), :] += _dot(jnp.where(hit0 | hit1, 1.0, 0.0).astype(BF16), h)
        weight = jnp.sum(jnp.where(hit0, wts[0:1, :], 0.0) + jnp.where(hit1, wts[1:2, :], 0.0),
                         axis=1, keepdims=True)
        sw_ref[pl.ds(ws, rows), :] += jnp.broadcast_to(weight, (rows, LANES))

    def body(pair, carry):
        c0 = clo_ref[i] + 2 * pair
        c1 = jnp.minimum(c0 + 1, c_last)
        live1 = c0 + 1 <= c_last
        lo0, hi0, small0 = item(c0)
        lo1, hi1, small1 = item(c1)
        both_small = small0 & small1

        @pl.when(both_small)
        def _():
            window(c0, lo0, MOE_DSP_SMALL, hi0 > lo0)
            window(c1, lo1, MOE_DSP_SMALL, live1 & (hi1 > lo1))

        @pl.when(jnp.logical_not(both_small))
        def _():
            pl.when((hi0 > lo0) & small0)(lambda: window(c0, lo0, MOE_DSP_SMALL, True))
            pl.when((hi0 > lo0) & jnp.logical_not(small0))(lambda: window(c0, lo0, MOE_DSP_LARGE, True))
            pl.when(live1 & (hi1 > lo1) & small1)(lambda: window(c1, lo1, MOE_DSP_SMALL, True))
            pl.when(live1 & (hi1 > lo1) & jnp.logical_not(small1))(lambda: window(c1, lo1, MOE_DSP_LARGE, True))
        return carry

    lax.fori_loop(0, (c_last - clo_ref[i] + 2) // 2, body, 0)
    o_ref[...] = acc_ref[...].astype(BF16)


def _dispatch_call(h2, route):
    n_tok = h2.shape[0]
    n_tiles = route["n_tiles"]
    grid_spec = pltpu.PrefetchScalarGridSpec(
        num_scalar_prefetch=5,
        grid=(n_tiles,),
        in_specs=[
            pl.BlockSpec((n_tok, D_MODEL), lambda i, *_: (0, 0), pipeline_mode=pl.Buffered(1)),
            pl.BlockSpec((n_tok // MOE_SRC, 8, MOE_SRC), lambda i, *_: (0, 0, 0), pipeline_mode=pl.Buffered(1)),
            pl.BlockSpec((n_tok // MOE_SRC, 8, MOE_SRC), lambda i, *_: (0, 0, 0), pipeline_mode=pl.Buffered(1)),
        ],
        out_specs=[pl.BlockSpec((MOE_TILE, D_MODEL), lambda i, *_: (i, 0)),
                   pl.BlockSpec((MOE_TILE, LANES), lambda i, *_: (i, 0))],
        scratch_shapes=[pltpu.VMEM((MOE_TILE, D_MODEL), F32)],
    )
    return pl.pallas_call(
        _dispatch_kernel,
        grid_spec=grid_spec,
        out_shape=[jax.ShapeDtypeStruct((n_tiles * MOE_TILE, D_MODEL), BF16),
                   jax.ShapeDtypeStruct((n_tiles * MOE_TILE, LANES), F32)],
        compiler_params=_params("arbitrary"),
        name="moe_dispatch",
    )(route["c_lo"], route["c_hi"], route["tile_expert"], route["run_start"], route["run_cnt"],
      h2, route["pos_rows"], route["w_rows"])


def _expert_kernel(te_ref, nv_ref, x_ref, w1_ref, w3_ref, w2_ref, sw_ref, o_ref, acc_ref):
    i = pl.program_id(0)
    f = pl.program_id(1)
    last = pl.num_programs(1) - 1
    valid = i < nv_ref[0]

    @pl.when(valid & (f == 0))
    def _():
        acc_ref[...] = jnp.zeros_like(acc_ref)

    @pl.when(valid)
    def _():
        acc_ref[...] += _swiglu_chunk(x_ref[...], w1_ref[0, 0], w3_ref[0, 0], w2_ref[0, 0])

    @pl.when(valid & (f == last))
    def _():
        o_ref[...] = (acc_ref[...] * sw_ref[:, 0:1]).astype(BF16)

    @pl.when(jnp.logical_not(valid) & (f == last))
    def _():
        o_ref[...] = jnp.zeros_like(o_ref)


def _expert_call(xs, slot_w, route, w1, w3, w2, layer):
    n_tiles = route["n_tiles"]
    grid_spec = pltpu.PrefetchScalarGridSpec(
        num_scalar_prefetch=2,
        grid=(n_tiles, D_FF // MOE_FF_CHUNK),
        in_specs=[
            pl.BlockSpec((MOE_TILE, D_MODEL), lambda i, f, te, nv: (i, 0)),
            pl.BlockSpec((1, 1, D_MODEL, MOE_FF_CHUNK), lambda i, f, te, nv: (layer, te[i], 0, f)),
            pl.BlockSpec((1, 1, D_MODEL, MOE_FF_CHUNK), lambda i, f, te, nv: (layer, te[i], 0, f)),
            pl.BlockSpec((1, 1, MOE_FF_CHUNK, D_MODEL), lambda i, f, te, nv: (layer, te[i], f, 0)),
            pl.BlockSpec((MOE_TILE, LANES), lambda i, f, te, nv: (i, 0)),
        ],
        out_specs=pl.BlockSpec((MOE_TILE, D_MODEL), lambda i, f, te, nv: (i, 0)),
        scratch_shapes=[pltpu.VMEM((MOE_TILE, D_MODEL), F32)],
    )
    return pl.pallas_call(
        _expert_kernel,
        grid_spec=grid_spec,
        out_shape=jax.ShapeDtypeStruct((n_tiles * MOE_TILE, D_MODEL), BF16),
        compiler_params=_params("arbitrary", "arbitrary"),
        name="moe_experts",
    )(route["tile_expert"], route["n_valid"], xs, w1, w3, w2, slot_w)


def _combine_kernel(small_ref, cnt_ref, winl_ref, offl_ref, wins_ref, offs_ref,
                    y_hbm, pos_ref, x_ref, mod_ref, fg_ref, o_ref, buf_ref, sem_ref, *, final_norm):
    c = pl.program_id(0)
    cur = c % 2
    variants = ((MOE_CMB_SMALL, wins_ref, offs_ref), (MOE_CMB_WIN, winl_ref, offl_ref))

    def window_copies(chunk, buf_set, rows, win_ref):
        return [pltpu.make_async_copy(
            y_hbm.at[pl.ds(pl.multiple_of(win_ref[chunk * N_EXPERTS + e], 16), rows), :],
            buf_ref.at[buf_set, pl.ds(e * rows, rows), :],
            sem_ref.at[buf_set, e]) for e in range(N_EXPERTS)]

    def start(chunk, buf_set):
        for use, (rows, win_ref, _) in zip((small_ref[chunk] > 0, small_ref[chunk] == 0), variants):
            @pl.when(use)
            def _():
                for cp in window_copies(chunk, buf_set, rows, win_ref):
                    cp.start()

    pl.when(c == 0)(lambda: start(0, 0))
    pl.when(c + 1 < pl.num_programs(0))(lambda: start(jnp.minimum(c + 1, pl.num_programs(0) - 1), 1 - cur))

    def process(rows, win_ref, off_ref):
        row = lax.broadcasted_iota(jnp.int32, (1, N_EXPERTS * rows), 1)
        slot = jnp.full((1, N_EXPERTS * rows), -1, jnp.int32)
        for e in range(N_EXPERTS):
            local = row - e * rows
            off = off_ref[c * N_EXPERTS + e]
            inside = (local >= off) & (local < off + cnt_ref[c * N_EXPERTS + e]) & (local < rows)
            slot = jnp.where(inside, win_ref[c * N_EXPERTS + e] + local, slot)
        pos = pos_ref[...]
        hit = (pos[:, 0:1] == slot) | (pos[:, 1:2] == slot)
        onehot = jnp.where(hit, 1.0, 0.0).astype(BF16)
        for cp in window_copies(c, cur, rows, win_ref):
            cp.wait()
        update = _dot(onehot, buf_ref[cur, 0:N_EXPERTS * rows, :])
        o_ref[...] = _residual_out(x_ref[...], mod_ref[0][5:6], update, final_norm, fg_ref[...])

    for use, variant in zip((small_ref[c] > 0, small_ref[c] == 0), variants):
        pl.when(use)(functools.partial(process, *variant))


def _combine_call(ys, route, x, mod_l, final_g, first_tile, final_norm):
    n_tok = x.shape[0]
    rows = lambda width: pl.BlockSpec((MOE_SRC, width), lambda c, *_: (c, 0))
    grid_spec = pltpu.PrefetchScalarGridSpec(
        num_scalar_prefetch=6,
        grid=(n_tok // MOE_SRC,),
        in_specs=[
            pl.BlockSpec(memory_space=pl.ANY),
            rows(TOP_K),
            rows(D_MODEL),
            pl.BlockSpec((1, ADA_CHUNKS, D_MODEL), lambda c, *_: (_tile_mod_row(c, first_tile, MOE_SRC), 0, 0)),
            pl.BlockSpec((1, D_MODEL), lambda c, *_: (0, 0)),
        ],
        out_specs=rows(D_MODEL),
        scratch_shapes=[pltpu.VMEM((2, N_EXPERTS * MOE_CMB_WIN, D_MODEL), BF16),
                        pltpu.SemaphoreType.DMA((2, N_EXPERTS))],
    )
    return pl.pallas_call(
        functools.partial(_combine_kernel, final_norm=final_norm),
        grid_spec=grid_spec,
        out_shape=jax.ShapeDtypeStruct((n_tok, D_MODEL), F32),
        compiler_params=_params("arbitrary"),
        name="moe_combine",
    )(route["cmb_small"], route["run_cnt"], route["win"], route["run_off"], route["win_s"], route["run_off_s"],
      ys, route["pos_cols"], x, mod_l, final_g)


def _rope_tables():
    t = np.arange(SEQ)

    def angles(rot_dim):
        half = rot_dim // 2
        inv = ROPE_THETA ** (-jnp.arange(0, half, 2, dtype=F32) / half)
        ang = jnp.concatenate([jnp.asarray(t // GRID_W, F32)[:, None] * inv[None, :],
                               jnp.asarray(t % GRID_W, F32)[:, None] * inv[None, :]], axis=-1)
        return jnp.cos(ang), jnp.sin(ang)

    def pad_rows(a, fill):
        return jnp.concatenate([jnp.full((ROW_TILE, a.shape[1]), fill, F32), a], axis=0)

    c, s = angles(HEAD_DIM)
    cos64 = jnp.tile(jnp.concatenate([c, c], axis=-1), (1, 4))
    sin64 = jnp.tile(jnp.concatenate([s, s], axis=-1), (1, 4))
    c, s = angles(MLA_ROPE)
    one = jnp.ones((SEQ, MLA_NOPE), F32)
    zero = jnp.zeros((SEQ, MLA_NOPE), F32)
    tail1 = jnp.ones((SEQ, MLA_PAD - MLA_NOPE - MLA_ROPE), F32)
    tail0 = jnp.zeros((SEQ, MLA_PAD - MLA_NOPE - MLA_ROPE), F32)
    cosm = jnp.tile(jnp.concatenate([one, c, c, tail1], axis=-1), (1, MLA_HEADS))
    sinm = jnp.tile(jnp.concatenate([zero, s, s, tail0], axis=-1), (1, MLA_HEADS))
    blockdiag = jnp.asarray(np.kron(np.eye(4), np.ones((HEAD_DIM, HEAD_DIM))), BF16)
    place = np.zeros((256, MLA_HEADS * MLA_PAD), np.float32)
    for h in range(MLA_HEADS):
        for r in range(MLA_ROPE):
            place[MLA_KV_RANK + r, h * MLA_PAD + MLA_NOPE + r] = 1.0
    return {
        "cos64": pad_rows(cos64, 1.0), "sin64": pad_rows(sin64, 0.0),
        "cosm": pad_rows(cosm, 1.0), "sinm": pad_rows(sinm, 0.0),
        "blockdiag": blockdiag, "pe_place": jnp.asarray(place, BF16),
    }


def _pack_w_in(w):
    n_qkv = COL_DKVA + MLA_KV_RANK + MLA_ROPE
    qkv = jnp.pad(w[..., :n_qkv].astype(BF16), ((0, 0), (0, 0), (0, COL_GATE - n_qkv)))
    return qkv, (0.5 * w[..., n_qkv:]).astype(BF16)


def _pack_mla(wqb, wkvb):
    dqh = MLA_NOPE + MLA_ROPE
    q = wqb.reshape(MLA_Q_RANK, MLA_HEADS, dqh)
    q = jnp.pad(q, ((0, 0), (0, 0), (0, MLA_PAD - dqh))).reshape(MLA_Q_RANK, MLA_HEADS * MLA_PAD)
    kv = wkvb.reshape(MLA_KV_RANK, MLA_HEADS, MLA_NOPE + MLA_V)
    k = jnp.pad(kv[:, :, :MLA_NOPE], ((0, 0), (0, 0), (0, MLA_PAD - MLA_NOPE)))
    k = k.reshape(MLA_KV_RANK, MLA_HEADS * MLA_PAD)
    v = jnp.pad(kv[:, :, MLA_NOPE:], ((0, 0), (0, 0), (0, MLA_PAD - MLA_V))).reshape(MLA_KV_RANK, MLA_HEADS * MLA_PAD)
    return q.astype(BF16), k.astype(BF16), v.astype(BF16)


def _split2_host(w):
    hi = w.astype(BF16)
    return jnp.stack([hi, (w - hi.astype(F32)).astype(BF16)])


def kernel(x, c, ctx, c_ctx, norm1_g, norm2_g, w_ada, b_ada, w_in, na_rpb, gb_qnorm, gb_knorm, wc_sink,
           mla_qnorm, mla_kvnorm, mla_wqb, mla_wkvb, w_branch, w_out, ffn_w1, ffn_w3, ffn_w2,
           moe_router, moe_w1, moe_w3, moe_w2, final_g):
    n_batch = x.shape[0]
    assert x.shape[1:] == (SEQ, D_MODEL) and ctx.shape[1:] == (CTX_LEN, D_MODEL)
    assert n_batch * CTX_LEN <= CTX_ROWS and n_batch <= 8

    ctx_rows = ctx.reshape(n_batch * CTX_LEN, D_MODEL)
    if ctx_rows.shape[0] < CTX_ROWS:
        ctx_rows = jnp.pad(ctx_rows, ((0, CTX_ROWS - ctx_rows.shape[0]), (0, 0)))
    xt = jnp.concatenate([ctx_rows, x.reshape(n_batch * SEQ, D_MODEL)], axis=0)

    cvec = jnp.zeros((16, D_MODEL), F32).at[:n_batch].set(c).at[8].set(c_ctx)
    mod = _mod_call(cvec, w_ada, b_ada).reshape(DEPTH, 16, ADA_CHUNKS, D_MODEL)
    tabs = _rope_tables()
    w_in_p = _pack_w_in(w_in)
    moe_w = tuple(w.astype(BF16) for w in (moe_w1, moe_w3, moe_w2))
    mla_scale = (MLA_NOPE + MLA_ROPE) ** -0.5
    lat_tile0 = CTX_ROWS // ROW_TILE

    for l in range(DEPTH):
        with_ctx = l < DEPTH - 1
        wq, wk, wv = _pack_mla(mla_wqb[l], mla_wkvb[l])
        lw = {
            "gq": jnp.tile(gb_qnorm[l], 4)[None, :], "gk": jnp.tile(gb_knorm[l], 2)[None, :],
            "qn": mla_qnorm[l][None, :], "kvn": mla_kvnorm[l][None, :], "wqb": wq, "wk": wk, "wv": wv,
        }
        pa, cv, gate, qb, kb, qc, kc, qd, kd, vd, vb = _inproj_call(
            xt, mod[l], norm1_g[l][None, :], w_in_p, l, tabs, lw)

        oa = _natten_call(n_batch, with_ctx, pa, _natten_bias_table(na_rpb[l]))
        ob = _gqa_attn_t_call(n_batch, with_ctx, qb, kb, vb, n_kv=GB_KV_HEADS,
                              group=GB_Q_HEADS // GB_KV_HEADS, name="attn_global")
        oc = _window_attn_call(n_batch, with_ctx, (qc, 256, 0), (kc, 128, 0), (cv, 128, 0), wc_sink[l])
        od = _dense_attn_call(n_batch, with_ctx, (qd, MLA_HEADS * MLA_PAD, 0), kd, (vd, MLA_HEADS * MLA_PAD, 0),
                              n_kv=MLA_HEADS, group=1, dq=MLA_PAD, scale=mla_scale, name="attn_mla")

        is_moe = l % 2 == 1
        router3 = None
        if is_moe:
            router3 = _split2_host(jnp.pad(moe_router[l // 2], ((0, 0), (0, LANES - N_EXPERTS))))
        first_tile = 0 if with_ctx else lat_tile0
        res = _merge_call((oa, ob, oc, od), gate, xt, mod[l], norm2_g[l][None, :],
                          (0.5 * w_branch[l]).astype(BF16), w_out[l].astype(BF16), router3, first_tile)
        x_mid, h2 = res[0], res[1]
        final_norm = l == DEPTH - 1
        if is_moe:
            route = _moe_route(res[2], h2.shape[0])
            xs, slot_w = _dispatch_call(h2, route)
            ys = _expert_call(xs, slot_w, route, *moe_w, l // 2)
            xt = _combine_call(ys, route, x_mid, mod[l], final_g[None, :],
                               0 if with_ctx else CTX_ROWS // MOE_SRC, final_norm)
        else:
            xt = _ffn_call(h2, ffn_w1, ffn_w3, ffn_w2, l // 2, x_mid, mod[l], final_g[None, :],
                           0 if with_ctx else CTX_ROWS // FFN_ROWS, final_norm)

    return xt.reshape(n_batch, SEQ, D_MODEL)
```

```python
import functools

import numpy as np
import jax
import jax.numpy as jnp
from jax import lax
from jax.experimental import pallas as pl
from jax.experimental.pallas import tpu as pltpu

F32 = jnp.float32
BF16 = jnp.bfloat16

D_MODEL = 1024
SEQ = 2048
DEPTH = 4
CTX_LEN = 256
GRID_W = 64
GRID_ROWS = SEQ // GRID_W
HEAD_DIM = 64
ROPE_THETA = 10000.0
NORM_EPS = 1e-6
NEG_INF = -1e30

NA_HEADS = 4
NA_KH = 8
NA_KW = 16
GB_Q_HEADS = 4
GB_KV_HEADS = 2
WC_Q_HEADS = 4
WC_KV_HEADS = 2
WC_WINDOW = 128
MLA_HEADS = 4
MLA_Q_RANK = 256
MLA_KV_RANK = 128
MLA_NOPE = 64
MLA_ROPE = 32
MLA_V = 64
MLA_PAD = 128
N_BRANCH = 4
BRANCH_W = 256
D_FF = 3584
N_EXPERTS = 8
ADA_CHUNKS = 6

VMEM_LIMIT_BYTES = 56 * 1024 * 1024
LANES = 128

CTX_ROWS = 2048
ROW_TILE = 512
Q_TILE = 256
NA_Q_ROWS = Q_TILE // GRID_W
NA_WIN_ROWS = 12
NA_WIN = NA_WIN_ROWS * GRID_W
WC_WIN = Q_TILE + 2 * WC_WINDOW
ATTN_KEY_CHUNK = 512

COL_AQ, COL_AK, COL_AV = 0, 256, 512
COL_BQ, COL_BK, COL_BV = 768, 1024, 1152
COL_CQ, COL_CK, COL_CV = 1280, 1536, 1664
COL_DQA, COL_DKVA, COL_GATE = 1792, 2048, 2304
GATE_W = N_BRANCH * D_MODEL
P_WIDTH = COL_GATE + GATE_W
QKV_CHUNK = COL_GATE // 2
FF_CHUNK = 512
MOE_FF_CHUNK = 1792


def _params(*sem):
    return pltpu.CompilerParams(dimension_semantics=sem, vmem_limit_bytes=VMEM_LIMIT_BYTES)


def _dot(a, b):
    return jnp.dot(a, b, preferred_element_type=F32)


def _dot_nt(a, b):
    return lax.dot_general(a, b, (((1,), (1,)), ((), ())), preferred_element_type=F32)


def _split3(x):
    hi = x.astype(BF16)
    r1 = x - hi.astype(F32)
    mid = r1.astype(BF16)
    lo = (r1 - mid.astype(F32)).astype(BF16)
    return hi, mid, lo


def _sigmoid(x):
    return 0.5 * jnp.tanh(0.5 * x) + 0.5


ADA_TN = 1536


def _mod_kernel(c_ref, w_ref, b_ref, o_ref):
    c = c_ref[...]
    sc = (c * _sigmoid(c)).astype(BF16)
    o_ref[0] = _dot(sc, w_ref[0].astype(BF16)) + b_ref[0]


def _mod_call(cvec, w_ada, b_ada):
    n = ADA_CHUNKS * D_MODEL
    return pl.pallas_call(
        _mod_kernel,
        grid=(DEPTH, n // ADA_TN),
        in_specs=[
            pl.BlockSpec((16, D_MODEL), lambda l, j: (0, 0)),
            pl.BlockSpec((1, D_MODEL, ADA_TN), lambda l, j: (l, 0, j)),
            pl.BlockSpec((1, 1, ADA_TN), lambda l, j: (l, 0, j)),
        ],
        out_specs=pl.BlockSpec((1, 16, ADA_TN), lambda l, j: (l, 0, j)),
        out_shape=jax.ShapeDtypeStruct((DEPTH, 16, n), F32),
        compiler_params=_params("arbitrary", "arbitrary"),
        name="adaln_mod",
    )(cvec, w_ada, b_ada.reshape(DEPTH, 1, n))


def _mod_row(i, first_tile):
    n_ctx = CTX_ROWS // ROW_TILE
    t = i + first_tile
    return jnp.where(t < n_ctx, 8, (t - n_ctx) // (SEQ // ROW_TILE))


def _norm_mod(x, g, shift, scale):
    ms = jnp.mean(x * x, axis=-1, keepdims=True)
    y = x * lax.rsqrt(ms + NORM_EPS) * g
    return y * (1.0 + scale) + shift


def _rope(x, cos, sin, half, first_mask):
    w = x.shape[-1]
    fwd = pltpu.roll(x, w - half, 1)
    bwd = pltpu.roll(x, half, 1)
    rot = jnp.where(first_mask, -fwd, bwd)
    return x * cos + rot * sin


def _head_rms(x, gain, blockdiag):
    hi, mid, lo = _split3(x * x)
    ss = _dot(hi, blockdiag) + _dot(mid, blockdiag) + _dot(lo, blockdiag)
    return x * lax.rsqrt(ss * (1.0 / HEAD_DIM) + NORM_EPS) * gain


def _inproj_kernel(x_ref, mod_ref, g_ref, w_ref,
                   cos64_ref, sin64_ref, cosm_ref, sinm_ref,
                   gq_ref, gk_ref, bd_ref, qn_ref, kvn_ref, wqb_ref, wk_ref, wv_ref, pe_ref,
                   pa_ref, cv_ref, gate_ref, qb_ref, kb_ref, qc_ref, kc_ref, qd_ref, kd_ref, vd_ref, vb_ref):
    m = mod_ref[0]
    hb = _norm_mod(x_ref[...], g_ref[...], m[0:1], m[1:2]).astype(BF16)

    def proj(c0, c1):
        return _dot(hb, w_ref[0, :, c0:c1])

    for c0 in range(0, GATE_W, D_MODEL):
        gate_ref[:, c0:c0 + D_MODEL] = proj(COL_GATE + c0, COL_GATE + c0 + D_MODEL).astype(BF16)
    pa_ref[...] = proj(COL_AQ, COL_BQ).astype(BF16)

    scale = HEAD_DIM ** -0.5
    half = HEAD_DIM // 2
    cos64, sin64 = cos64_ref[...], sin64_ref[...]
    lane = lax.broadcasted_iota(jnp.int32, (1, 256), 1)
    first64 = (lane % HEAD_DIM) < half
    bd = bd_ref[...]
    bc = proj(COL_BQ, COL_DQA)
    bq = _head_rms(bc[:, 0:256], gq_ref[...], bd)
    qb_ref[...] = (_rope(bq, cos64, sin64, half, first64) * scale).T.astype(BF16)
    bk = _head_rms(bc[:, 256:384], gk_ref[...], bd[:128, :128])
    kb_ref[...] = _rope(bk, cos64[:, :128], sin64[:, :128], half, first64[:, :128]).astype(BF16)
    qc_ref[...] = (_rope(bc[:, 512:768], cos64, sin64, half, first64) * scale).T.astype(BF16)
    kc_ref[...] = _rope(bc[:, 768:896], cos64[:, :128], sin64[:, :128], half, first64[:, :128]).astype(BF16)
    cv = bc[:, 896:1024]
    ones_c = (lane[:, :LANES] % LANES) >= HEAD_DIM
    cv_ref[:LANES, :] = jnp.where(ones_c, 1.0, cv).T.astype(BF16)
    cv_ref[LANES:, :] = jnp.where(ones_c, 1.0, pltpu.roll(cv, HEAD_DIM, 1)).T.astype(BF16)

    cosm, sinm = cosm_ref[...], sinm_ref[...]
    lane_m = lax.broadcasted_iota(jnp.int32, (1, MLA_HEADS * MLA_PAD), 1) % MLA_PAD
    first_m = lane_m < (MLA_NOPE + MLA_ROPE // 2)
    d = proj(COL_DQA, COL_GATE)
    dqa = d[:, 0:MLA_Q_RANK]
    qn = dqa * lax.rsqrt(jnp.mean(dqa * dqa, axis=-1, keepdims=True) + NORM_EPS) * qn_ref[...]
    dq = _dot(qn.astype(BF16), wqb_ref[...])
    qd_ref[...] = _rope(dq, cosm, sinm, MLA_ROPE // 2, first_m).astype(BF16)

    dkva = d[:, MLA_Q_RANK:]
    kvc = dkva[:, :MLA_KV_RANK]
    kvn = (kvc * lax.rsqrt(jnp.mean(kvc * kvc, axis=-1, keepdims=True) + NORM_EPS) * kvn_ref[...]).astype(BF16)
    dk = _dot(kvn, wk_ref[...]) + _dot(dkva.astype(BF16), pe_ref[...])
    kd_ref[...] = _rope(dk, cosm, sinm, MLA_ROPE // 2, first_m).T.astype(BF16)
    ones_half = lane_m >= HEAD_DIM
    vd_ref[...] = jnp.where(ones_half, 1.0, _dot(kvn, wv_ref[...])).astype(BF16)
    bv = bc[:, 384:512]
    vb_ref[:LANES, :] = jnp.where(ones_half[:, :LANES], 1.0, bv).T.astype(BF16)
    vb_ref[LANES:, :] = jnp.where(ones_half[:, :LANES], 1.0, pltpu.roll(bv, HEAD_DIM, 1)).T.astype(BF16)


def _rope_row_block(i):
    n_ctx = CTX_ROWS // ROW_TILE
    return jnp.where(i < n_ctx, 0, 1 + (i - n_ctx) % (SEQ // ROW_TILE))


def _inproj_call(x, mod_l, g, w_in_p, layer, tabs, lw):
    t = x.shape[0]
    mw = MLA_HEADS * MLA_PAD

    def full(shape):
        return pl.BlockSpec(shape, lambda i: (0,) * len(shape))

    def rows(width):
        return pl.BlockSpec((ROW_TILE, width), lambda i: (i, 0))

    def tab(width):
        return pl.BlockSpec((ROW_TILE, width), lambda i: (_rope_row_block(i), 0))

    def cols(height):
        return pl.BlockSpec((height, ROW_TILE), lambda i: (0, i))

    outs = [(COL_BQ, False), (WC_KV_HEADS * LANES, True), (GATE_W, False),
            (256, True), (128, False), (256, True), (128, False), (mw, False), (mw, True), (mw, False),
            (GB_KV_HEADS * LANES, True)]
    return pl.pallas_call(
        _inproj_kernel,
        grid=(t // ROW_TILE,),
        in_specs=[
            rows(D_MODEL),
            pl.BlockSpec((1, ADA_CHUNKS, D_MODEL), lambda i: (_mod_row(i, 0), 0, 0)),
            full((1, D_MODEL)),
            pl.BlockSpec((1, D_MODEL, P_WIDTH), lambda i: (layer, 0, 0), pipeline_mode=pl.Buffered(1)),
            tab(256), tab(256), tab(mw), tab(mw),
            full((1, 256)), full((1, 128)), full((256, 256)), full((1, MLA_Q_RANK)), full((1, MLA_KV_RANK)),
            full((MLA_Q_RANK, mw)), full((MLA_KV_RANK, mw)), full((MLA_KV_RANK, mw)), full((256, mw)),
        ],
        out_specs=[cols(w) if tr else rows(w) for w, tr in outs],
        out_shape=[jax.ShapeDtypeStruct((w, t) if tr else (t, w), BF16) for w, tr in outs],
        compiler_params=_params("arbitrary"),
        name="inproj",
    )(x, mod_l, g, w_in_p, tabs["cos64"], tabs["sin64"], tabs["cosm"], tabs["sinm"],
      lw["gq"], lw["gk"], tabs["blockdiag"], lw["qn"], lw["kvn"], lw["wqb"], lw["wk"], lw["wv"], tabs["pe_place"])


def _attend(q, segs, sink=None, scale=None):
    chunks = segs
    scores = []
    m = None
    for k, _, bias in chunks:
        s = _dot_nt(q, k)
        if scale is not None:
            s = s * scale
        if bias is not None:
            s = s + bias
        scores.append(s)
        ms = jnp.max(s, axis=-1, keepdims=True)
        m = ms if m is None else jnp.maximum(m, ms)
    if sink is not None:
        m = jnp.maximum(m, sink)
    denom = None
    out = None
    for s, (_, v, _) in zip(scores, chunks):
        p = jnp.exp(s - m)
        ps = jnp.sum(p, axis=-1, keepdims=True)
        pv = _dot(p.astype(BF16), v)
        denom = ps if denom is None else denom + ps
        out = pv if out is None else out + pv
    if sink is not None:
        denom = denom + jnp.exp(sink - m)
    return out / denom


def _gqa_heads(q_ref, o_ref, sink_ref, n_kv, group, dq, dv, scale, seg_fn):
    tq = q_ref.shape[0]
    for h in range(n_kv):
        heads = [h * group + g for g in range(group)]
        q = jnp.concatenate([q_ref[:, a * dq:(a + 1) * dq] for a in heads], axis=0) if group > 1 \
            else q_ref[:, h * dq:(h + 1) * dq]
        sink = None
        if sink_ref is not None:
            sink = jnp.concatenate([jnp.full((tq, 1), sink_ref[a], F32) for a in heads], axis=0)
        o = _attend(q, seg_fn(h), sink=sink, scale=scale)
        for g, a in enumerate(heads):
            o_ref[:, a * dv:(a + 1) * dv] = o[g * tq:(g + 1) * tq].astype(BF16)


def _dense_attn_kernel(q_ref, ktl_ref, vl_ref, ktc_ref, vc_ref, o_ref, *, n_kv, group, dq, scale, ctx_tile):
    tq = q_ref.shape[0]
    hd = HEAD_DIM

    def run(with_latent):
        for h in range(n_kv):
            heads = [h * group + g for g in range(group)]
            q = jnp.concatenate([q_ref[:, a * dq:(a + 1) * dq] for a in heads], axis=0) if group > 1 \
                else q_ref[:, h * dq:(h + 1) * dq]
            segs = [(ktc_ref[h * dq:(h + 1) * dq, :], vc_ref[:, h * LANES:(h + 1) * LANES])]
            if with_latent:
                segs = [(ktl_ref[h * dq:(h + 1) * dq, c0:c0 + ATTN_KEY_CHUNK],
                         vl_ref[c0:c0 + ATTN_KEY_CHUNK, h * LANES:(h + 1) * LANES])
                        for c0 in range(0, SEQ, ATTN_KEY_CHUNK)] + segs
            scores = []
            m = None
            for kt, _ in segs:
                s = _dot(q, kt)
                if scale is not None:
                    s = s * scale
                scores.append(s)
                ms = jnp.max(s, axis=-1, keepdims=True)
                m = ms if m is None else jnp.maximum(m, ms)
            acc = None
            for s, (_, v) in zip(scores, segs):
                pv = _dot(jnp.exp(s - m).astype(BF16), v)
                acc = pv if acc is None else acc + pv
            o = acc * pltpu.roll(1.0 / acc, hd, 1)
            for g, a in enumerate(heads):
                o_ref[:, a * hd:(a + 1) * hd] = o[g * tq:(g + 1) * tq, :hd].astype(BF16)

    if ctx_tile:
        j = pl.program_id(1)
        pl.when(j == 0)(lambda: run(False))
        pl.when(j > 0)(lambda: run(True))
    else:
        run(True)


def _gqa_attn_t_kernel(qt_ref, kl_ref, vtl_ref, kc_ref, vtc_ref, o_ref, *, n_kv, group, ctx_tile):
    tq = qt_ref.shape[1]
    hd = HEAD_DIM
    kw = kl_ref.shape[1]
    n = group * tq

    def run(with_latent):
        for h in range(n_kv):
            heads = [h * group + g for g in range(group)]
            qcat = jnp.concatenate([qt_ref[a * hd:(a + 1) * hd, :] for a in heads], axis=1)
            parts = [jnp.zeros((h * hd, n), BF16), qcat, jnp.zeros((kw - (h + 1) * hd, n), BF16)]
            wq = jnp.concatenate([p for p in parts if p.shape[0] > 0], axis=0)
            vrows = slice(h * LANES, (h + 1) * LANES)
            segs = [(kc_ref[...], vtc_ref[vrows, :])]
            if with_latent:
                segs = [(kl_ref[c0:c0 + ATTN_KEY_CHUNK, :], vtl_ref[vrows, c0:c0 + ATTN_KEY_CHUNK])
                        for c0 in range(0, SEQ, ATTN_KEY_CHUNK)] + segs
            scores = []
            m = None
            for k, _ in segs:
                s = _dot(k, wq)
                scores.append(s)
                ms = jnp.max(s, axis=0, keepdims=True)
                m = ms if m is None else jnp.maximum(m, ms)
            acc = None
            for s, (_, vt) in zip(scores, segs):
                pv = _dot(vt, jnp.exp(s - m).astype(BF16))
                acc = pv if acc is None else acc + pv
            o = acc[:hd] * (1.0 / acc[hd:hd + 1])
            for g, a in enumerate(heads):
                o_ref[:, a * hd:(a + 1) * hd] = o[:, g * tq:(g + 1) * tq].T.astype(BF16)

    if ctx_tile:
        j = pl.program_id(1)
        pl.when(j == 0)(lambda: run(False))
        pl.when(j > 0)(lambda: run(True))
    else:
        run(True)


def _q_row_block(b, j, n_batch, with_ctx):
    per_batch = SEQ // Q_TILE
    lat0 = CTX_ROWS // Q_TILE
    if with_ctx:
        return jnp.where(j == 0, b, lat0 + b * per_batch + j - 1)
    return lat0 + b * per_batch + j


def _attn_specs(n_batch, with_ctx, q, k, v, o_width, order_bj=True):
    def ix(f):
        return (lambda b, j: f(b, j)) if order_bj else (lambda j, b: f(b, j))

    def qspec(width, col):
        return pl.BlockSpec((Q_TILE, width), ix(lambda b, j: (_q_row_block(b, j, n_batch, with_ctx), col // width)))

    def lat(width, col):
        return pl.BlockSpec((SEQ, width), ix(lambda b, j: (CTX_ROWS // SEQ + b, col // width)))

    def ctx(width, col):
        return pl.BlockSpec((CTX_LEN, width), ix(lambda b, j: (b, col // width)))

    in_specs = [qspec(q[1], q[2]), lat(k[1], k[2]), lat(v[1], v[2]), ctx(k[1], k[2]), ctx(v[1], v[2])]
    args = [q[0], k[0], v[0], k[0], v[0]]
    if with_ctx:
        return in_specs, args, qspec(o_width, 0)
    out_spec = pl.BlockSpec((Q_TILE, o_width), ix(lambda b, j: (b * (SEQ // Q_TILE) + j, 0)))
    return in_specs, args, out_spec


def _attn_out_rows(t, with_ctx):
    return t if with_ctx else t - CTX_ROWS


def _dense_attn_call(n_batch, with_ctx, q, kt, v, *, n_kv, group, dq, scale=None, name):
    t = q[0].shape[0]
    in_specs, args, out_spec = _attn_specs(n_batch, with_ctx, q, v, v, 256)
    kt_rows = kt.shape[0]
    in_specs[1] = pl.BlockSpec((kt_rows, SEQ), lambda b, j: (0, CTX_ROWS // SEQ + b))
    in_specs[3] = pl.BlockSpec((kt_rows, CTX_LEN), lambda b, j: (0, b))
    args[1] = args[3] = kt
    kern = functools.partial(_dense_attn_kernel, n_kv=n_kv, group=group, dq=dq, scale=scale, ctx_tile=with_ctx)
    return pl.pallas_call(
        kern,
        grid=(n_batch, SEQ // Q_TILE + (1 if with_ctx else 0)),
        in_specs=in_specs,
        out_specs=out_spec,
        out_shape=jax.ShapeDtypeStruct((_attn_out_rows(t, with_ctx), 256), BF16),
        compiler_params=_params("arbitrary", "arbitrary"),
        name=name,
    )(*args)


def _gqa_attn_t_call(n_batch, with_ctx, qt, k, vt, *, n_kv, group, name):
    t = k.shape[0]
    kspec = (k, k.shape[1], 0)
    in_specs, args, out_spec = _attn_specs(n_batch, with_ctx, kspec, kspec, kspec, 256)
    in_specs[0] = pl.BlockSpec((qt.shape[0], Q_TILE), lambda b, j: (0, _q_row_block(b, j, n_batch, with_ctx)))
    in_specs[2] = pl.BlockSpec((vt.shape[0], SEQ), lambda b, j: (0, CTX_ROWS // SEQ + b))
    in_specs[4] = pl.BlockSpec((vt.shape[0], CTX_LEN), lambda b, j: (0, b))
    args[0] = qt
    args[2] = args[4] = vt
    return pl.pallas_call(
        functools.partial(_gqa_attn_t_kernel, n_kv=n_kv, group=group, ctx_tile=with_ctx),
        grid=(n_batch, SEQ // Q_TILE + (1 if with_ctx else 0)),
        in_specs=in_specs,
        out_specs=out_spec,
        out_shape=jax.ShapeDtypeStruct((_attn_out_rows(t, with_ctx), 256), BF16),
        compiler_params=_params("arbitrary", "arbitrary"),
        name=name,
    )(*args)


def _window_attn_kernel(sink_ref, qt_ref, kl_ref, vtl_ref, kc_ref, vtc_ref, o_ref, *, ctx_tile):
    j = pl.program_id(1)
    hd = HEAD_DIM
    group = WC_Q_HEADS // WC_KV_HEADS
    tq = qt_ref.shape[1]
    kw = kl_ref.shape[1]
    n = group * tq

    def run(with_latent):
        if with_latent:
            i = j - 1 if ctx_tile else j
            start = pl.multiple_of(jnp.clip(i * Q_TILE - WC_WINDOW, 0, SEQ - WC_WIN), WC_WINDOW)
            kpos = start + lax.broadcasted_iota(jnp.int32, (WC_WIN, n), 0)
            qpos = i * Q_TILE + lax.broadcasted_iota(jnp.int32, (WC_WIN, n), 1) % Q_TILE
            bias = jnp.where(jnp.abs(qpos - kpos) <= WC_WINDOW, 0.0, NEG_INF).astype(F32)
        lane = lax.broadcasted_iota(jnp.int32, (1, n), 1)
        for h in range(WC_KV_HEADS):
            heads = [h * group + g for g in range(group)]
            qcat = jnp.concatenate([qt_ref[a * hd:(a + 1) * hd, :] for a in heads], axis=1)
            parts = [jnp.zeros((h * hd, n), BF16), qcat, jnp.zeros((kw - (h + 1) * hd, n), BF16)]
            wq = jnp.concatenate([p for p in parts if p.shape[0] > 0], axis=0)
            vrows = slice(h * LANES, (h + 1) * LANES)
            sink = functools.reduce(lambda acc, ga: jnp.where(lane >= ga[0] * tq, sink_ref[ga[1]], acc),
                                    list(enumerate(heads)), jnp.zeros((1, n), F32))
            s_ctx = _dot(kc_ref[...], wq)
            m = jnp.maximum(jnp.max(s_ctx, axis=0, keepdims=True), sink)
            if with_latent:
                s_lat = _dot(kl_ref[pl.ds(start, WC_WIN), :], wq) + bias
                m = jnp.maximum(m, jnp.max(s_lat, axis=0, keepdims=True))
            acc = _dot(vtc_ref[vrows, :], jnp.exp(s_ctx - m).astype(BF16))
            if with_latent:
                acc = acc + _dot(vtl_ref[vrows, pl.ds(start, WC_WIN)], jnp.exp(s_lat - m).astype(BF16))
            o = acc[:hd] * (1.0 / (acc[hd:hd + 1] + jnp.exp(sink - m)))
            for g, a in enumerate(heads):
                o_ref[:, a * hd:(a + 1) * hd] = o[:, g * tq:(g + 1) * tq].T.astype(BF16)

    if ctx_tile:
        pl.when(j == 0)(lambda: run(False))
        pl.when(j > 0)(lambda: run(True))
    else:
        run(True)


def _window_attn_call(n_batch, with_ctx, qt, k, vt, sink):
    t = k.shape[0]
    kspec = (k, k.shape[1], 0)
    in_specs, args, out_spec = _attn_specs(n_batch, with_ctx, kspec, kspec, kspec, 256)
    in_specs[0] = pl.BlockSpec((qt.shape[0], Q_TILE), lambda b, j: (0, _q_row_block(b, j, n_batch, with_ctx)))
    in_specs[2] = pl.BlockSpec((vt.shape[0], SEQ), lambda b, j: (0, CTX_ROWS // SEQ + b))
    in_specs[4] = pl.BlockSpec((vt.shape[0], CTX_LEN), lambda b, j: (0, b))
    args[0] = qt
    args[2] = args[4] = vt
    return pl.pallas_call(
        functools.partial(_window_attn_kernel, ctx_tile=with_ctx),
        grid=(n_batch, SEQ // Q_TILE + (1 if with_ctx else 0)),
        in_specs=[pl.BlockSpec(memory_space=pltpu.SMEM)] + in_specs,
        out_specs=out_spec,
        out_shape=jax.ShapeDtypeStruct((_attn_out_rows(t, with_ctx), 256), BF16),
        compiler_params=_params("arbitrary", "arbitrary"),
        name="attn_window",
    )(sink, *args)


def _natten_kernel(q_ref, kl_ref, vl_ref, kc_ref, vc_ref, tab_ref, o_ref, bias_ref, *, ctx_tile):
    j = pl.program_id(0)
    hd = HEAD_DIM
    scale = HEAD_DIM ** -0.5

    def build_bias(i):
        ws = jnp.clip(NA_Q_ROWS * i - NA_KH // 2, 0, GRID_ROWS - NA_WIN_ROWS)
        for a in range(NA_Q_ROWS):
            r = NA_Q_ROWS * i + a
            krow0 = jnp.clip(r - NA_KH // 2, 0, GRID_ROWS - NA_KH)
            for pair in range(NA_WIN_ROWS // 2):
                idx = []
                for side in range(2):
                    kr = ws + 2 * pair + side
                    in_rows = (kr >= krow0) & (kr < krow0 + NA_KH)
                    idx.append(jnp.where(in_rows, kr - r + NA_KH - 1, NA_NO_ROW))
                for h in range(NA_HEADS):
                    bias_ref[h, a * GRID_W:(a + 1) * GRID_W, pair * LANES:(pair + 1) * LANES] = (
                        tab_ref[0, h, idx[0]] + tab_ref[1, h, idx[1]])

    def ctx_run():
        for h in range(NA_HEADS):
            sl = slice(h * hd, (h + 1) * hd)
            o = _attend(q_ref[:, sl] * scale, [(kc_ref[:, sl], vc_ref[:, sl], None)])
            o_ref[:, sl] = o.astype(BF16)

    def lat_run():
        i = j - 1 if ctx_tile else j
        ws = jnp.clip(NA_Q_ROWS * i - NA_KH // 2, 0, GRID_ROWS - NA_WIN_ROWS)
        start = pl.multiple_of(ws * GRID_W, GRID_W)
        pl.when(pl.program_id(1) == 0)(lambda: build_bias(i))
        for h in range(NA_HEADS):
            sl = slice(h * hd, (h + 1) * hd)
            segs = [(kl_ref[pl.ds(start, NA_WIN), sl], vl_ref[pl.ds(start, NA_WIN), sl], bias_ref[h]),
                    (kc_ref[:, sl], vc_ref[:, sl], None)]
            o_ref[:, sl] = _attend(q_ref[:, sl] * scale, segs).astype(BF16)

    if ctx_tile:
        pl.when(j == 0)(ctx_run)
        pl.when(j > 0)(lat_run)
    else:
        lat_run()


NA_NO_ROW = 2 * NA_KH - 1


def _natten_call(n_batch, with_ctx, p, bias_blocks):
    t = p.shape[0]
    q, k, v = (p, 256, COL_AQ), (p, 256, COL_AK), (p, 256, COL_AV)
    in_specs, args, out_spec = _attn_specs(n_batch, with_ctx, q, k, v, 256, order_bj=False)
    in_specs.append(pl.BlockSpec(bias_blocks.shape, lambda j, b: (0,) * bias_blocks.ndim))
    return pl.pallas_call(
        functools.partial(_natten_kernel, ctx_tile=with_ctx),
        grid=(SEQ // Q_TILE + (1 if with_ctx else 0), n_batch),
        in_specs=in_specs,
        out_specs=out_spec,
        out_shape=jax.ShapeDtypeStruct((_attn_out_rows(t, with_ctx), 256), BF16),
        scratch_shapes=[pltpu.VMEM((NA_HEADS, Q_TILE, NA_WIN), F32)],
        compiler_params=_params("arbitrary", "arbitrary"),
        name="attn_natten",
    )(*args, bias_blocks)


def _natten_bias_table(rpb):
    n_dr, n_dc = 2 * NA_KH - 1, 2 * NA_KW - 1
    col = np.arange(GRID_W)
    dc = np.clip(col[None, :] - col[:, None] + NA_KW - 1, 0, n_dc - 1)
    onehot = jnp.asarray(dc[None] == np.arange(n_dc)[:, None, None], F32)
    blocks = jnp.einsum('hdc,cqk->hdqk', rpb.astype(F32), onehot, precision=lax.Precision.HIGHEST)
    cstart = np.clip(col - NA_KW // 2, 0, GRID_W - NA_KW)
    col_ok = (col[None, :] >= cstart[:, None]) & (col[None, :] < cstart[:, None] + NA_KW)
    blocks = jnp.where(col_ok[None, None], blocks, NEG_INF)
    blocks = jnp.concatenate([blocks, jnp.full((NA_HEADS, 1, GRID_W, GRID_W), NEG_INF, F32)], axis=1)
    zero = jnp.zeros_like(blocks)
    return jnp.stack([jnp.concatenate([blocks, zero], axis=-1), jnp.concatenate([zero, blocks], axis=-1)])


def _merge_kernel(*refs, with_router):
    if with_router:
        (oa_ref, ob_ref, oc_ref, od_ref, gate_ref, x_ref, mod_ref, g2_ref, wb_ref, wo_ref, rt_ref,
         xo_ref, h2_ref, comb_ref) = refs
    else:
        (oa_ref, ob_ref, oc_ref, od_ref, gate_ref, x_ref, mod_ref, g2_ref, wb_ref, wo_ref,
         xo_ref, h2_ref) = refs
    m = mod_ref[0]
    acc = None
    for n, o_ref in enumerate((oa_ref, ob_ref, oc_ref, od_ref)):
        half_y = _dot(o_ref[...], wb_ref[n])
        t = jnp.tanh(gate_ref[:, n * D_MODEL:(n + 1) * D_MODEL].astype(F32)) + 1.0
        acc = t * half_y if acc is None else acc + t * half_y
    mix = _dot(acc.astype(BF16), wo_ref[...])
    x = x_ref[...] + m[2:3] * mix
    xo_ref[...] = x
    h2 = _norm_mod(x, g2_ref[...], m[3:4], m[4:5])
    h2_ref[...] = h2.astype(BF16)
    if with_router:
        hh, hm, _ = _split3(h2)
        rh, rm = rt_ref[0], rt_ref[1]
        logits = _dot(hh, rh) + (_dot(hh, rm) + _dot(hm, rh))
        lane = lax.broadcasted_iota(jnp.int32, logits.shape, 1).astype(F32)
        logits = jnp.where(lane < N_EXPERTS, logits, NEG_INF)
        m1 = jnp.max(logits, axis=-1, keepdims=True)
        i1 = jnp.min(jnp.where(logits == m1, lane, float(LANES)), axis=-1, keepdims=True)
        rest = jnp.where(lane == i1, NEG_INF, logits)
        m2 = jnp.max(rest, axis=-1, keepdims=True)
        i2 = jnp.min(jnp.where(rest == m2, lane, float(LANES)), axis=-1, keepdims=True)
        e = jnp.exp(m2 - m1)
        w1 = 1.0 / (1.0 + e)
        w2 = e / (1.0 + e)
        comb_ref[...] = (jnp.where(lane == 0.0, i1, 0.0) + jnp.where(lane == 1.0, i2, 0.0)
                         + jnp.where(lane == 2.0, w1, 0.0) + jnp.where(lane == 3.0, w2, 0.0))


def _merge_call(outs, gate, x, mod_l, g2, wb, wo, router3, first_tile):
    t = x.shape[0]
    n_rows = t - first_tile * ROW_TILE

    def rows(width, col=0):
        return pl.BlockSpec((ROW_TILE, width), lambda i: (i + first_tile, col // width))

    def orow(width):
        return pl.BlockSpec((ROW_TILE, width), lambda i: (i, 0))

    def full(shape):
        return pl.BlockSpec(shape, lambda i: (0,) * len(shape))

    in_specs = [orow(256)] * 4 + [
        rows(GATE_W),
        rows(D_MODEL),
        pl.BlockSpec((1, ADA_CHUNKS, D_MODEL), lambda i: (_mod_row(i, first_tile), 0, 0)),
        full((1, D_MODEL)), full((N_BRANCH, BRANCH_W, D_MODEL)), full((D_MODEL, D_MODEL)),
    ]
    args = list(outs) + [gate, x, mod_l, g2, wb, wo]
    out_specs = [orow(D_MODEL), orow(D_MODEL)]
    out_shape = [jax.ShapeDtypeStruct((n_rows, D_MODEL), F32), jax.ShapeDtypeStruct((n_rows, D_MODEL), BF16)]
    if router3 is not None:
        in_specs.append(full((2, D_MODEL, LANES)))
        args.append(router3)
        out_specs.append(orow(LANES))
        out_shape.append(jax.ShapeDtypeStruct((n_rows, LANES), F32))
    return pl.pallas_call(
        functools.partial(_merge_kernel, with_router=router3 is not None),
        grid=(n_rows // ROW_TILE,),
        in_specs=in_specs,
        out_specs=out_specs,
        out_shape=out_shape,
        compiler_params=_params("arbitrary"),
        name="merge",
    )(*args)


FFN_ROWS = 1024


def _swiglu_chunk(h, w1, w3, w2):
    half_a = 0.5 * _dot(h, w1)
    b = _dot(h, w3)
    return _dot((half_a * (jnp.tanh(half_a) + 1.0) * b).astype(BF16), w2)


def _residual_out(x, gate, update, final_norm, final_g):
    x = x + gate * update
    if final_norm:
        ms = jnp.mean(x * x, axis=-1, keepdims=True)
        x = x * lax.rsqrt(ms + NORM_EPS) * final_g
    return x


def _ffn_kernel(h_ref, w1_ref, w3_ref, w2_ref, x_ref, mod_ref, fg_ref, o_ref, acc_ref, *, final_norm):
    f = pl.program_id(1)

    @pl.when(f == 0)
    def _():
        acc_ref[...] = jnp.zeros_like(acc_ref)

    acc_ref[...] += _swiglu_chunk(h_ref[...], w1_ref[0].astype(BF16), w3_ref[0].astype(BF16),
                                  w2_ref[0].astype(BF16))

    @pl.when(f == pl.num_programs(1) - 1)
    def _():
        o_ref[...] = _residual_out(x_ref[...], mod_ref[0][5:6], acc_ref[...], final_norm, fg_ref[...])


def _tile_mod_row(i, first_tile, tile_rows):
    n_ctx = CTX_ROWS // tile_rows
    t = i + first_tile
    return jnp.where(t < n_ctx, 8, (t - n_ctx) // (SEQ // tile_rows))


def _ffn_call(h2, w1, w3, w2, layer, x, mod_l, final_g, first_row_tile, final_norm):
    t = x.shape[0]
    rows = lambda width: pl.BlockSpec((FFN_ROWS, width), lambda i, f: (i, 0))
    return pl.pallas_call(
        functools.partial(_ffn_kernel, final_norm=final_norm),
        grid=(t // FFN_ROWS, D_FF // FF_CHUNK),
        in_specs=[
            rows(D_MODEL),
            pl.BlockSpec((1, D_MODEL, FF_CHUNK), lambda i, f: (layer, 0, f)),
            pl.BlockSpec((1, D_MODEL, FF_CHUNK), lambda i, f: (layer, 0, f)),
            pl.BlockSpec((1, FF_CHUNK, D_MODEL), lambda i, f: (layer, f, 0)),
            rows(D_MODEL),
            pl.BlockSpec((1, ADA_CHUNKS, D_MODEL),
                         lambda i, f: (_tile_mod_row(i, first_row_tile, FFN_ROWS), 0, 0)),
            pl.BlockSpec((1, D_MODEL), lambda i, f: (0, 0)),
        ],
        out_specs=rows(D_MODEL),
        out_shape=jax.ShapeDtypeStruct((t, D_MODEL), F32),
        scratch_shapes=[pltpu.VMEM((FFN_ROWS, D_MODEL), F32)],
        compiler_params=_params("arbitrary", "arbitrary"),
        name="ffn",
    )(h2, w1, w3, w2, x, mod_l, final_g)


MOE_TILE = 512
MOE_SRC = 256
MOE_CMB_WIN = MOE_SRC + 16
MOE_CMB_SMALL = 144
MOE_DSP_SMALL = 144
MOE_DSP_LARGE = MOE_SRC + 16
TOP_K = 2


def _moe_route(sel, n_tok):
    n_tiles = TOP_K * n_tok // MOE_TILE + N_EXPERTS
    e = sel[:, 0:TOP_K].astype(jnp.int32)
    flat_e = e.reshape(-1)
    onehot = (flat_e[None, :] == jnp.arange(N_EXPERTS)[:, None]).astype(jnp.int32)
    csum = jnp.cumsum(onehot, axis=1)
    counts = csum[:, -1]
    padded = (counts + MOE_TILE - 1) // MOE_TILE * MOE_TILE
    seg_end = jnp.cumsum(padded)
    seg_start = seg_end - padded
    pos = jnp.sum(onehot * (seg_start[:, None] + csum - onehot), axis=0)
    pos2 = pos.reshape(n_tok, TOP_K)
    tile_ix = jnp.arange(n_tiles)
    tile_expert = jnp.minimum(jnp.sum(tile_ix[:, None] >= (seg_end // MOE_TILE)[None, :], axis=1), N_EXPERTS - 1)
    n_valid = seg_end[-1] // MOE_TILE
    n_src = n_tok // MOE_SRC
    per_chunk = MOE_SRC * TOP_K
    before = jnp.concatenate([jnp.zeros((1, N_EXPERTS), jnp.int32), csum[:, per_chunk - 1::per_chunk].T], axis=0)
    run_start = seg_start[None, :] + before[:-1]
    run_cnt = before[1:] - before[:-1]
    cum = jnp.take(before, tile_expert, axis=1)
    local0 = tile_ix * MOE_TILE - jnp.take(seg_start, tile_expert)
    local1 = jnp.minimum(local0 + MOE_TILE, jnp.take(counts, tile_expert))
    c_lo = jnp.sum(cum[1:] <= local0[None, :], axis=0)
    c_hi = jnp.sum(cum[:-1] < local1[None, :], axis=0) - 1
    win = jnp.minimum(run_start // 16 * 16, n_tiles * MOE_TILE - MOE_CMB_WIN)
    win_s = jnp.minimum(run_start // 16 * 16, n_tiles * MOE_TILE - MOE_CMB_SMALL)
    cmb_small = jnp.all(run_start - win_s + run_cnt <= MOE_CMB_SMALL, axis=1)

    def token_rows(a, dtype):
        rows = jnp.transpose(a.reshape(n_src, MOE_SRC, TOP_K), (0, 2, 1)).astype(dtype)
        return jnp.zeros((n_src, 8, MOE_SRC), dtype).at[:, :TOP_K, :].set(rows)

    i32 = lambda a: a.astype(jnp.int32)
    return {
        "n_tiles": n_tiles, "tile_expert": i32(tile_expert), "n_valid": i32(n_valid).reshape(1),
        "c_lo": i32(c_lo), "c_hi": i32(c_hi), "run_start": i32(run_start.reshape(-1)),
        "run_cnt": i32(run_cnt.reshape(-1)), "win": i32(win.reshape(-1)),
        "run_off": i32((run_start - win).reshape(-1)),
        "win_s": i32(win_s.reshape(-1)), "run_off_s": i32((run_start - win_s).reshape(-1)),
        "cmb_small": i32(cmb_small),
        "pos_rows": token_rows(pos2, jnp.int32), "w_rows": token_rows(sel[:, TOP_K:2 * TOP_K], F32),
        "pos_cols": i32(pos2),
    }


def _dispatch_kernel(clo_ref, chi_ref, te_ref, rs_ref, rc_ref, h_ref, pos_ref, w_ref, o_ref, sw_ref, acc_ref):
    i = pl.program_id(0)
    base = i * MOE_TILE
    expert = te_ref[i]
    acc_ref[...] = jnp.zeros_like(acc_ref)
    sw_ref[...] = jnp.zeros_like(sw_ref)

    c_last = chi_ref[i]

    def item(c):
        run0 = rs_ref[c * N_EXPERTS + expert]
        lo = jnp.maximum(run0, base) - base
        hi = jnp.minimum(run0 + rc_ref[c * N_EXPERTS + expert], base + MOE_TILE) - base
        small = hi - jnp.minimum(lo // 16 * 16, MOE_TILE - MOE_DSP_SMALL) <= MOE_DSP_SMALL
        return lo, hi, small

    def window(c, lo, rows, live):
        pos = pos_ref[c]
        wts = w_ref[c]
        h = h_ref[pl.ds(pl.multiple_of(c * MOE_SRC, MOE_SRC), MOE_SRC), :]
        ws = pl.multiple_of(jnp.minimum(lo // 16 * 16, MOE_TILE - rows), 16)
        slot = base + ws + lax.broadcasted_iota(jnp.int32, (rows, MOE_SRC), 0)
        slot = jnp.where(live, slot, -1)
        hit0 = pos[0:1, :] == slot
        hit1 = pos[1:2, :] == slot
        acc_ref[pl.ds(ws, rows), :] += _dot(jnp.where(hit0 | hit1, 1.0, 0.0).astype(BF16), h)
        weight = jnp.sum(jnp.where(hit0, wts[0:1, :], 0.0) + jnp.where(hit1, wts[1:2, :], 0.0),
                         axis=1, keepdims=True)
        sw_ref[pl.ds(ws, rows), :] += jnp.broadcast_to(weight, (rows, LANES))

    def body(pair, carry):
        c0 = clo_ref[i] + 2 * pair
        c1 = jnp.minimum(c0 + 1, c_last)
        live1 = c0 + 1 <= c_last
        lo0, hi0, small0 = item(c0)
        lo1, hi1, small1 = item(c1)
        both_small = small0 & small1

        @pl.when(both_small)
        def _():
            window(c0, lo0, MOE_DSP_SMALL, hi0 > lo0)
            window(c1, lo1, MOE_DSP_SMALL, live1 & (hi1 > lo1))

        @pl.when(jnp.logical_not(both_small))
        def _():
            pl.when((hi0 > lo0) & small0)(lambda: window(c0, lo0, MOE_DSP_SMALL, True))
            pl.when((hi0 > lo0) & jnp.logical_not(small0))(lambda: window(c0, lo0, MOE_DSP_LARGE, True))
            pl.when(live1 & (hi1 > lo1) & small1)(lambda: window(c1, lo1, MOE_DSP_SMALL, True))
            pl.when(live1 & (hi1 > lo1) & jnp.logical_not(small1))(lambda: window(c1, lo1, MOE_DSP_LARGE, True))
        return carry

    lax.fori_loop(0, (c_last - clo_ref[i] + 2) // 2, body, 0)
    o_ref[...] = acc_ref[...].astype(BF16)


def _dispatch_call(h2, route):
    n_tok = h2.shape[0]
    n_tiles = route["n_tiles"]
    grid_spec = pltpu.PrefetchScalarGridSpec(
        num_scalar_prefetch=5,
        grid=(n_tiles,),
        in_specs=[
            pl.BlockSpec((n_tok, D_MODEL), lambda i, *_: (0, 0), pipeline_mode=pl.Buffered(1)),
            pl.BlockSpec((n_tok // MOE_SRC, 8, MOE_SRC), lambda i, *_: (0, 0, 0), pipeline_mode=pl.Buffered(1)),
            pl.BlockSpec((n_tok // MOE_SRC, 8, MOE_SRC), lambda i, *_: (0, 0, 0), pipeline_mode=pl.Buffered(1)),
        ],
        out_specs=[pl.BlockSpec((MOE_TILE, D_MODEL), lambda i, *_: (i, 0)),
                   pl.BlockSpec((MOE_TILE, LANES), lambda i, *_: (i, 0))],
        scratch_shapes=[pltpu.VMEM((MOE_TILE, D_MODEL), F32)],
    )
    return pl.pallas_call(
        _dispatch_kernel,
        grid_spec=grid_spec,
        out_shape=[jax.ShapeDtypeStruct((n_tiles * MOE_TILE, D_MODEL), BF16),
                   jax.ShapeDtypeStruct((n_tiles * MOE_TILE, LANES), F32)],
        compiler_params=_params("arbitrary"),
        name="moe_dispatch",
    )(route["c_lo"], route["c_hi"], route["tile_expert"], route["run_start"], route["run_cnt"],
      h2, route["pos_rows"], route["w_rows"])


def _expert_kernel(te_ref, nv_ref, x_ref, w1_ref, w3_ref, w2_ref, sw_ref, o_ref, acc_ref):
    i = pl.program_id(0)
    f = pl.program_id(1)
    last = pl.num_programs(1) - 1
    valid = i < nv_ref[0]

    @pl.when(valid & (f == 0))
    def _():
        acc_ref[...] = jnp.zeros_like(acc_ref)

    @pl.when(valid)
    def _():
        acc_ref[...] += _swiglu_chunk(x_ref[...], w1_ref[0, 0], w3_ref[0, 0], w2_ref[0, 0])

    @pl.when(valid & (f == last))
    def _():
        o_ref[...] = (acc_ref[...] * sw_ref[:, 0:1]).astype(BF16)

    @pl.when(jnp.logical_not(valid) & (f == last))
    def _():
        o_ref[...] = jnp.zeros_like(o_ref)


def _expert_call(xs, slot_w, route, w1, w3, w2, layer):
    n_tiles = route["n_tiles"]
    grid_spec = pltpu.PrefetchScalarGridSpec(
        num_scalar_prefetch=2,
        grid=(n_tiles, D_FF // MOE_FF_CHUNK),
        in_specs=[
            pl.BlockSpec((MOE_TILE, D_MODEL), lambda i, f, te, nv: (i, 0)),
            pl.BlockSpec((1, 1, D_MODEL, MOE_FF_CHUNK), lambda i, f, te, nv: (layer, te[i], 0, f)),
            pl.BlockSpec((1, 1, D_MODEL, MOE_FF_CHUNK), lambda i, f, te, nv: (layer, te[i], 0, f)),
            pl.BlockSpec((1, 1, MOE_FF_CHUNK, D_MODEL), lambda i, f, te, nv: (layer, te[i], f, 0)),
            pl.BlockSpec((MOE_TILE, LANES), lambda i, f, te, nv: (i, 0)),
        ],
        out_specs=pl.BlockSpec((MOE_TILE, D_MODEL), lambda i, f, te, nv: (i, 0)),
        scratch_shapes=[pltpu.VMEM((MOE_TILE, D_MODEL), F32)],
    )
    return pl.pallas_call(
        _expert_kernel,
        grid_spec=grid_spec,
        out_shape=jax.ShapeDtypeStruct((n_tiles * MOE_TILE, D_MODEL), BF16),
        compiler_params=_params("arbitrary", "arbitrary"),
        name="moe_experts",
    )(route["tile_expert"], route["n_valid"], xs, w1, w3, w2, slot_w)


def _combine_kernel(small_ref, cnt_ref, winl_ref, offl_ref, wins_ref, offs_ref,
                    y_hbm, pos_ref, x_ref, mod_ref, fg_ref, o_ref, buf_ref, sem_ref, *, final_norm):
    c = pl.program_id(0)
    cur = c % 2
    variants = ((MOE_CMB_SMALL, wins_ref, offs_ref), (MOE_CMB_WIN, winl_ref, offl_ref))

    def window_copies(chunk, buf_set, rows, win_ref):
        return [pltpu.make_async_copy(
            y_hbm.at[pl.ds(pl.multiple_of(win_ref[chunk * N_EXPERTS + e], 16), rows), :],
            buf_ref.at[buf_set, pl.ds(e * rows, rows), :],
            sem_ref.at[buf_set, e]) for e in range(N_EXPERTS)]

    def start(chunk, buf_set):
        for use, (rows, win_ref, _) in zip((small_ref[chunk] > 0, small_ref[chunk] == 0), variants):
            @pl.when(use)
            def _():
                for cp in window_copies(chunk, buf_set, rows, win_ref):
                    cp.start()

    pl.when(c == 0)(lambda: start(0, 0))
    pl.when(c + 1 < pl.num_programs(0))(lambda: start(jnp.minimum(c + 1, pl.num_programs(0) - 1), 1 - cur))

    def process(rows, win_ref, off_ref):
        row = lax.broadcasted_iota(jnp.int32, (1, N_EXPERTS * rows), 1)
        slot = jnp.full((1, N_EXPERTS * rows), -1, jnp.int32)
        for e in range(N_EXPERTS):
            local = row - e * rows
            off = off_ref[c * N_EXPERTS + e]
            inside = (local >= off) & (local < off + cnt_ref[c * N_EXPERTS + e]) & (local < rows)
            slot = jnp.where(inside, win_ref[c * N_EXPERTS + e] + local, slot)
        pos = pos_ref[...]
        hit = (pos[:, 0:1] == slot) | (pos[:, 1:2] == slot)
        onehot = jnp.where(hit, 1.0, 0.0).astype(BF16)
        for cp in window_copies(c, cur, rows, win_ref):
            cp.wait()
        update = _dot(onehot, buf_ref[cur, 0:N_EXPERTS * rows, :])
        o_ref[...] = _residual_out(x_ref[...], mod_ref[0][5:6], update, final_norm, fg_ref[...])

    for use, variant in zip((small_ref[c] > 0, small_ref[c] == 0), variants):
        pl.when(use)(functools.partial(process, *variant))


def _combine_call(ys, route, x, mod_l, final_g, first_tile, final_norm):
    n_tok = x.shape[0]
    rows = lambda width: pl.BlockSpec((MOE_SRC, width), lambda c, *_: (c, 0))
    grid_spec = pltpu.PrefetchScalarGridSpec(
        num_scalar_prefetch=6,
        grid=(n_tok // MOE_SRC,),
        in_specs=[
            pl.BlockSpec(memory_space=pl.ANY),
            rows(TOP_K),
            rows(D_MODEL),
            pl.BlockSpec((1, ADA_CHUNKS, D_MODEL), lambda c, *_: (_tile_mod_row(c, first_tile, MOE_SRC), 0, 0)),
            pl.BlockSpec((1, D_MODEL), lambda c, *_: (0, 0)),
        ],
        out_specs=rows(D_MODEL),
        scratch_shapes=[pltpu.VMEM((2, N_EXPERTS * MOE_CMB_WIN, D_MODEL), BF16),
                        pltpu.SemaphoreType.DMA((2, N_EXPERTS))],
    )
    return pl.pallas_call(
        functools.partial(_combine_kernel, final_norm=final_norm),
        grid_spec=grid_spec,
        out_shape=jax.ShapeDtypeStruct((n_tok, D_MODEL), F32),
        compiler_params=_params("arbitrary"),
        name="moe_combine",
    )(route["cmb_small"], route["run_cnt"], route["win"], route["run_off"], route["win_s"], route["run_off_s"],
      ys, route["pos_cols"], x, mod_l, final_g)


def _rope_tables():
    t = np.arange(SEQ)

    def angles(rot_dim):
        half = rot_dim // 2
        inv = ROPE_THETA ** (-jnp.arange(0, half, 2, dtype=F32) / half)
        ang = jnp.concatenate([jnp.asarray(t // GRID_W, F32)[:, None] * inv[None, :],
                               jnp.asarray(t % GRID_W, F32)[:, None] * inv[None, :]], axis=-1)
        return jnp.cos(ang), jnp.sin(ang)

    def pad_rows(a, fill):
        return jnp.concatenate([jnp.full((ROW_TILE, a.shape[1]), fill, F32), a], axis=0)

    c, s = angles(HEAD_DIM)
    cos64 = jnp.tile(jnp.concatenate([c, c], axis=-1), (1, 4))
    sin64 = jnp.tile(jnp.concatenate([s, s], axis=-1), (1, 4))
    c, s = angles(MLA_ROPE)
    one = jnp.ones((SEQ, MLA_NOPE), F32)
    zero = jnp.zeros((SEQ, MLA_NOPE), F32)
    tail1 = jnp.ones((SEQ, MLA_PAD - MLA_NOPE - MLA_ROPE), F32)
    tail0 = jnp.zeros((SEQ, MLA_PAD - MLA_NOPE - MLA_ROPE), F32)
    cosm = jnp.tile(jnp.concatenate([one, c, c, tail1], axis=-1), (1, MLA_HEADS))
    sinm = jnp.tile(jnp.concatenate([zero, s, s, tail0], axis=-1), (1, MLA_HEADS))
    blockdiag = jnp.asarray(np.kron(np.eye(4), np.ones((HEAD_DIM, HEAD_DIM))), BF16)
    place = np.zeros((256, MLA_HEADS * MLA_PAD), np.float32)
    for h in range(MLA_HEADS):
        for r in range(MLA_ROPE):
            place[MLA_KV_RANK + r, h * MLA_PAD + MLA_NOPE + r] = 1.0
    return {
        "cos64": pad_rows(cos64, 1.0), "sin64": pad_rows(sin64, 0.0),
        "cosm": pad_rows(cosm, 1.0), "sinm": pad_rows(sinm, 0.0),
        "blockdiag": blockdiag, "pe_place": jnp.asarray(place, BF16),
    }


def _pack_w_in(w):
    n_qkv = COL_DKVA + MLA_KV_RANK + MLA_ROPE
    wb = w.astype(BF16)
    pad = jnp.zeros(w.shape[:2] + (COL_GATE - n_qkv,), BF16)
    return jnp.concatenate([wb[..., :n_qkv], pad, wb[..., n_qkv:] * 0.5], axis=-1)


def _pack_mla(wqb, wkvb):
    dqh = MLA_NOPE + MLA_ROPE
    q = wqb.reshape(MLA_Q_RANK, MLA_HEADS, dqh)
    q = jnp.pad(q, ((0, 0), (0, 0), (0, MLA_PAD - dqh))).reshape(MLA_Q_RANK, MLA_HEADS * MLA_PAD)
    kv = wkvb.reshape(MLA_KV_RANK, MLA_HEADS, MLA_NOPE + MLA_V)
    k = jnp.pad(kv[:, :, :MLA_NOPE], ((0, 0), (0, 0), (0, MLA_PAD - MLA_NOPE)))
    k = k.reshape(MLA_KV_RANK, MLA_HEADS * MLA_PAD)
    v = jnp.pad(kv[:, :, MLA_NOPE:], ((0, 0), (0, 0), (0, MLA_PAD - MLA_V))).reshape(MLA_KV_RANK, MLA_HEADS * MLA_PAD)
    return q.astype(BF16), k.astype(BF16), v.astype(BF16)


def _split2_host(w):
    hi = w.astype(BF16)
    return jnp.stack([hi, (w - hi.astype(F32)).astype(BF16)])


def kernel(x, c, ctx, c_ctx, norm1_g, norm2_g, w_ada, b_ada, w_in, na_rpb, gb_qnorm, gb_knorm, wc_sink,
           mla_qnorm, mla_kvnorm, mla_wqb, mla_wkvb, w_branch, w_out, ffn_w1, ffn_w3, ffn_w2,
           moe_router, moe_w1, moe_w3, moe_w2, final_g):
    n_batch = x.shape[0]
    assert x.shape[1:] == (SEQ, D_MODEL) and ctx.shape[1:] == (CTX_LEN, D_MODEL)
    assert n_batch * CTX_LEN <= CTX_ROWS and n_batch <= 8

    ctx_rows = ctx.reshape(n_batch * CTX_LEN, D_MODEL)
    if ctx_rows.shape[0] < CTX_ROWS:
        ctx_rows = jnp.pad(ctx_rows, ((0, CTX_ROWS - ctx_rows.shape[0]), (0, 0)))
    xt = jnp.concatenate([ctx_rows, x.reshape(n_batch * SEQ, D_MODEL)], axis=0)

    cvec = jnp.zeros((16, D_MODEL), F32).at[:n_batch].set(c).at[8].set(c_ctx)
    mod = _mod_call(cvec, w_ada, b_ada).reshape(DEPTH, 16, ADA_CHUNKS, D_MODEL)
    tabs = _rope_tables()
    w_in_p = _pack_w_in(w_in)
    moe_w = tuple(w.astype(BF16) for w in (moe_w1, moe_w3, moe_w2))
    mla_scale = (MLA_NOPE + MLA_ROPE) ** -0.5
    lat_tile0 = CTX_ROWS // ROW_TILE

    for l in range(DEPTH):
        with_ctx = l < DEPTH - 1
        wq, wk, wv = _pack_mla(mla_wqb[l], mla_wkvb[l])
        lw = {
            "gq": jnp.tile(gb_qnorm[l], 4)[None, :], "gk": jnp.tile(gb_knorm[l], 2)[None, :],
            "qn": mla_qnorm[l][None, :], "kvn": mla_kvnorm[l][None, :], "wqb": wq, "wk": wk, "wv": wv,
        }
        pa, cv, gate, qb, kb, qc, kc, qd, kd, vd, vb = _inproj_call(
            xt, mod[l], norm1_g[l][None, :], w_in_p, l, tabs, lw)

        oa = _natten_call(n_batch, with_ctx, pa, _natten_bias_table(na_rpb[l]))
        ob = _gqa_attn_t_call(n_batch, with_ctx, qb, kb, vb, n_kv=GB_KV_HEADS,
                              group=GB_Q_HEADS // GB_KV_HEADS, name="attn_global")
        oc = _window_attn_call(n_batch, with_ctx, qc, kc, cv, wc_sink[l])
        od = _dense_attn_call(n_batch, with_ctx, (qd, MLA_HEADS * MLA_PAD, 0), kd, (vd, MLA_HEADS * MLA_PAD, 0),
                              n_kv=MLA_HEADS, group=1, dq=MLA_PAD, scale=mla_scale, name="attn_mla")

        is_moe = l % 2 == 1
        router3 = None
        if is_moe:
            router3 = _split2_host(jnp.pad(moe_router[l // 2], ((0, 0), (0, LANES - N_EXPERTS))))
        first_tile = 0 if with_ctx else lat_tile0
        res = _merge_call((oa, ob, oc, od), gate, xt, mod[l], norm2_g[l][None, :],
                          (0.5 * w_branch[l]).astype(BF16), w_out[l].astype(BF16), router3, first_tile)
        x_mid, h2 = res[0], res[1]
        final_norm = l == DEPTH - 1
        if is_moe:
            route = _moe_route(res[2], h2.shape[0])
            xs, slot_w = _dispatch_call(h2, route)
            ys = _expert_call(xs, slot_w, route, *moe_w, l // 2)
            xt = _combine_call(ys, route, x_mid, mod[l], final_g[None, :],
                               0 if with_ctx else CTX_ROWS // MOE_SRC, final_norm)
        else:
            xt = _ffn_call(h2, ffn_w1, ffn_w3, ffn_w2, l // 2, x_mid, mod[l], final_g[None, :],
                           0 if with_ctx else CTX_ROWS // FFN_ROWS, final_norm)

    return xt.reshape(n_batch, SEQ, D_MODEL)
```

```python
import functools

import numpy as np
import jax
import jax.numpy as jnp
from jax import lax
from jax.experimental import pallas as pl
from jax.experimental.pallas import tpu as pltpu

F32 = jnp.float32
BF16 = jnp.bfloat16

D_MODEL = 1024
SEQ = 2048
DEPTH = 4
CTX_LEN = 256
GRID_W = 64
GRID_ROWS = SEQ // GRID_W
HEAD_DIM = 64
ROPE_THETA = 10000.0
NORM_EPS = 1e-6
NEG_INF = -1e30

NA_HEADS = 4
NA_KH = 8
NA_KW = 16
GB_Q_HEADS = 4
GB_KV_HEADS = 2
WC_Q_HEADS = 4
WC_KV_HEADS = 2
WC_WINDOW = 128
MLA_HEADS = 4
MLA_Q_RANK = 256
MLA_KV_RANK = 128
MLA_NOPE = 64
MLA_ROPE = 32
MLA_V = 64
MLA_PAD = 128
N_BRANCH = 4
BRANCH_W = 256
D_FF = 3584
N_EXPERTS = 8
ADA_CHUNKS = 6

VMEM_LIMIT_BYTES = 56 * 1024 * 1024
LANES = 128

CTX_ROWS = 2048
ROW_TILE = 512
Q_TILE = 256
NA_Q_ROWS = Q_TILE // GRID_W
NA_WIN_ROWS = 12
NA_WIN = NA_WIN_ROWS * GRID_W
WC_WIN = Q_TILE + 2 * WC_WINDOW
ATTN_KEY_CHUNK = 512

COL_AQ, COL_AK, COL_AV = 0, 256, 512
COL_BQ, COL_BK, COL_BV = 768, 1024, 1152
COL_CQ, COL_CK, COL_CV = 1280, 1536, 1664
COL_DQA, COL_DKVA, COL_GATE = 1792, 2048, 2304
GATE_W = N_BRANCH * D_MODEL
P_WIDTH = COL_GATE + GATE_W
QKV_CHUNK = COL_GATE // 2
FF_CHUNK = 512
MOE_FF_CHUNK = 1792


def _params(*sem):
    return pltpu.CompilerParams(dimension_semantics=sem, vmem_limit_bytes=VMEM_LIMIT_BYTES)


def _dot(a, b):
    return jnp.dot(a, b, preferred_element_type=F32)


def _dot_nt(a, b):
    return lax.dot_general(a, b, (((1,), (1,)), ((), ())), preferred_element_type=F32)


def _split3(x):
    hi = x.astype(BF16)
    r1 = x - hi.astype(F32)
    mid = r1.astype(BF16)
    lo = (r1 - mid.astype(F32)).astype(BF16)
    return hi, mid, lo


def _sigmoid(x):
    return 0.5 * jnp.tanh(0.5 * x) + 0.5


ADA_TN = 1536


def _mod_kernel(c_ref, w_ref, b_ref, o_ref):
    c = c_ref[...]
    sc = (c * _sigmoid(c)).astype(BF16)
    o_ref[0] = _dot(sc, w_ref[0].astype(BF16)) + b_ref[0]


def _mod_call(cvec, w_ada, b_ada):
    n = ADA_CHUNKS * D_MODEL
    return pl.pallas_call(
        _mod_kernel,
        grid=(DEPTH, n // ADA_TN),
        in_specs=[
            pl.BlockSpec((16, D_MODEL), lambda l, j: (0, 0)),
            pl.BlockSpec((1, D_MODEL, ADA_TN), lambda l, j: (l, 0, j)),
            pl.BlockSpec((1, 1, ADA_TN), lambda l, j: (l, 0, j)),
        ],
        out_specs=pl.BlockSpec((1, 16, ADA_TN), lambda l, j: (l, 0, j)),
        out_shape=jax.ShapeDtypeStruct((DEPTH, 16, n), F32),
        compiler_params=_params("arbitrary", "arbitrary"),
        name="adaln_mod",
    )(cvec, w_ada, b_ada.reshape(DEPTH, 1, n))


def _mod_row(i, first_tile):
    n_ctx = CTX_ROWS // ROW_TILE
    t = i + first_tile
    return jnp.where(t < n_ctx, 8, (t - n_ctx) // (SEQ // ROW_TILE))


def _norm_mod(x, g, shift, scale):
    ms = jnp.mean(x * x, axis=-1, keepdims=True)
    y = x * lax.rsqrt(ms + NORM_EPS) * g
    return y * (1.0 + scale) + shift


def _rope(x, cos, sin, half, first_mask):
    w = x.shape[-1]
    fwd = pltpu.roll(x, w - half, 1)
    bwd = pltpu.roll(x, half, 1)
    rot = jnp.where(first_mask, -fwd, bwd)
    return x * cos + rot * sin


def _head_rms(x, gain, blockdiag):
    hi, mid, lo = _split3(x * x)
    ss = _dot(hi, blockdiag) + _dot(mid, blockdiag) + _dot(lo, blockdiag)
    return x * lax.rsqrt(ss * (1.0 / HEAD_DIM) + NORM_EPS) * gain


def _inproj_kernel(xc_ref, xl_ref, mod_ref, g_ref, w_ref, wg_ref,
                   cos64_ref, sin64_ref, cosm_ref, sinm_ref,
                   gq_ref, gk_ref, bd_ref, qn_ref, kvn_ref, wqb_ref, wk_ref, wv_ref, pe_ref,
                   pa_ref, cv_ref, gate_ref, qb_ref, kb_ref, qc_ref, kc_ref, qd_ref, kd_ref, vd_ref, vb_ref):
    m = mod_ref[0]
    x = jnp.where(pl.program_id(0) < CTX_ROWS // ROW_TILE, xc_ref[...], xl_ref[...])
    hb = _norm_mod(x, g_ref[...], m[0:1], m[1:2]).astype(BF16)

    def proj(c0, c1):
        return _dot(hb, w_ref[0, :, c0:c1])

    for c0 in range(0, GATE_W, D_MODEL):
        gate_ref[:, c0:c0 + D_MODEL] = _dot(hb, wg_ref[0, :, c0:c0 + D_MODEL]).astype(BF16)
    pa_ref[...] = proj(COL_AQ, COL_BQ).astype(BF16)

    scale = HEAD_DIM ** -0.5
    half = HEAD_DIM // 2
    cos64, sin64 = cos64_ref[...], sin64_ref[...]
    lane = lax.broadcasted_iota(jnp.int32, (1, 256), 1)
    first64 = (lane % HEAD_DIM) < half
    bd = bd_ref[...]
    bc = proj(COL_BQ, COL_DQA)
    bq = _head_rms(bc[:, 0:256], gq_ref[...], bd)
    qb_ref[...] = (_rope(bq, cos64, sin64, half, first64) * scale).T.astype(BF16)
    bk = _head_rms(bc[:, 256:384], gk_ref[...], bd[:128, :128])
    kb_ref[...] = _rope(bk, cos64[:, :128], sin64[:, :128], half, first64[:, :128]).astype(BF16)
    qc_ref[...] = (_rope(bc[:, 512:768], cos64, sin64, half, first64) * scale).T.astype(BF16)
    kc_ref[...] = _rope(bc[:, 768:896], cos64[:, :128], sin64[:, :128], half, first64[:, :128]).astype(BF16)
    cv = bc[:, 896:1024]
    ones_c = (lane[:, :LANES] % LANES) >= HEAD_DIM
    cv_ref[:LANES, :] = jnp.where(ones_c, 1.0, cv).T.astype(BF16)
    cv_ref[LANES:, :] = jnp.where(ones_c, 1.0, pltpu.roll(cv, HEAD_DIM, 1)).T.astype(BF16)

    cosm, sinm = cosm_ref[...], sinm_ref[...]
    lane_m = lax.broadcasted_iota(jnp.int32, (1, MLA_HEADS * MLA_PAD), 1) % MLA_PAD
    first_m = lane_m < (MLA_NOPE + MLA_ROPE // 2)
    d = proj(COL_DQA, COL_GATE)
    dqa = d[:, 0:MLA_Q_RANK]
    qn = dqa * lax.rsqrt(jnp.mean(dqa * dqa, axis=-1, keepdims=True) + NORM_EPS) * qn_ref[...]
    dq = _dot(qn.astype(BF16), wqb_ref[...])
    qd_ref[...] = _rope(dq, cosm, sinm, MLA_ROPE // 2, first_m).astype(BF16)

    dkva = d[:, MLA_Q_RANK:]
    kvc = dkva[:, :MLA_KV_RANK]
    kvn = (kvc * lax.rsqrt(jnp.mean(kvc * kvc, axis=-1, keepdims=True) + NORM_EPS) * kvn_ref[...]).astype(BF16)
    dk = _dot(kvn, wk_ref[...]) + _dot(dkva.astype(BF16), pe_ref[...])
    kd_ref[...] = _rope(dk, cosm, sinm, MLA_ROPE // 2, first_m).T.astype(BF16)
    ones_half = lane_m >= HEAD_DIM
    vd_ref[...] = jnp.where(ones_half, 1.0, _dot(kvn, wv_ref[...])).astype(BF16)
    bv = bc[:, 384:512]
    vb_ref[:LANES, :] = jnp.where(ones_half[:, :LANES], 1.0, bv).T.astype(BF16)
    vb_ref[LANES:, :] = jnp.where(ones_half[:, :LANES], 1.0, pltpu.roll(bv, HEAD_DIM, 1)).T.astype(BF16)


def _rope_row_block(i):
    n_ctx = CTX_ROWS // ROW_TILE
    return jnp.where(i < n_ctx, 0, 1 + (i - n_ctx) % (SEQ // ROW_TILE))


def _token_specs(tile_rows, first_tile, lat_has_ctx_rows):
    n_ctx = CTX_ROWS // tile_rows
    lat_off = n_ctx if lat_has_ctx_rows else 0
    ctx_spec = pl.BlockSpec((tile_rows, D_MODEL), lambda i: (jnp.minimum(i + first_tile, n_ctx - 1), 0))
    lat_spec = pl.BlockSpec((tile_rows, D_MODEL),
                            lambda i: (jnp.maximum(i + first_tile, n_ctx) - n_ctx + lat_off, 0))
    return [ctx_spec, lat_spec]


def _inproj_call(x_ctx, x_lat, mod_l, g, w_in_p, layer, tabs, lw):
    combined = x_lat is x_ctx
    t = x_lat.shape[0] if combined else CTX_ROWS + x_lat.shape[0]
    mw = MLA_HEADS * MLA_PAD

    def full(shape):
        return pl.BlockSpec(shape, lambda i: (0,) * len(shape))

    def rows(width):
        return pl.BlockSpec((ROW_TILE, width), lambda i: (i, 0))

    def tab(width):
        return pl.BlockSpec((ROW_TILE, width), lambda i: (_rope_row_block(i), 0))

    def cols(height):
        return pl.BlockSpec((height, ROW_TILE), lambda i: (0, i))

    outs = [(COL_BQ, False), (WC_KV_HEADS * LANES, True), (GATE_W, False),
            (256, True), (128, False), (256, True), (128, False), (mw, False), (mw, True), (mw, False),
            (GB_KV_HEADS * LANES, True)]
    return pl.pallas_call(
        _inproj_kernel,
        grid=(t // ROW_TILE,),
        in_specs=_token_specs(ROW_TILE, 0, combined) + [
            pl.BlockSpec((1, ADA_CHUNKS, D_MODEL), lambda i: (_mod_row(i, 0), 0, 0)),
            full((1, D_MODEL)),
            pl.BlockSpec((1, D_MODEL, COL_GATE), lambda i: (layer, 0, 0), pipeline_mode=pl.Buffered(1)),
            pl.BlockSpec((1, D_MODEL, GATE_W), lambda i: (layer, 0, 0), pipeline_mode=pl.Buffered(1)),
            tab(256), tab(256), tab(mw), tab(mw),
            full((1, 256)), full((1, 128)), full((256, 256)), full((1, MLA_Q_RANK)), full((1, MLA_KV_RANK)),
            full((MLA_Q_RANK, mw)), full((MLA_KV_RANK, mw)), full((MLA_KV_RANK, mw)), full((256, mw)),
        ],
        out_specs=[cols(w) if tr else rows(w) for w, tr in outs],
        out_shape=[jax.ShapeDtypeStruct((w, t) if tr else (t, w), BF16) for w, tr in outs],
        compiler_params=_params("arbitrary"),
        name="inproj",
    )(x_ctx, x_lat, mod_l, g, w_in_p[0], w_in_p[1], tabs["cos64"], tabs["sin64"], tabs["cosm"], tabs["sinm"],
      lw["gq"], lw["gk"], tabs["blockdiag"], lw["qn"], lw["kvn"], lw["wqb"], lw["wk"], lw["wv"], tabs["pe_place"])


def _attend(q, segs, sink=None, scale=None):
    chunks = segs
    scores = []
    m = None
    for k, _, bias in chunks:
        s = _dot_nt(q, k)
        if scale is not None:
            s = s * scale
        if bias is not None:
            s = s + bias
        scores.append(s)
        ms = jnp.max(s, axis=-1, keepdims=True)
        m = ms if m is None else jnp.maximum(m, ms)
    if sink is not None:
        m = jnp.maximum(m, sink)
    denom = None
    out = None
    for s, (_, v, _) in zip(scores, chunks):
        p = jnp.exp(s - m)
        ps = jnp.sum(p, axis=-1, keepdims=True)
        pv = _dot(p.astype(BF16), v)
        denom = ps if denom is None else denom + ps
        out = pv if out is None else out + pv
    if sink is not None:
        denom = denom + jnp.exp(sink - m)
    return out / denom


def _gqa_heads(q_ref, o_ref, sink_ref, n_kv, group, dq, dv, scale, seg_fn):
    tq = q_ref.shape[0]
    for h in range(n_kv):
        heads = [h * group + g for g in range(group)]
        q = jnp.concatenate([q_ref[:, a * dq:(a + 1) * dq] for a in heads], axis=0) if group > 1 \
            else q_ref[:, h * dq:(h + 1) * dq]
        sink = None
        if sink_ref is not None:
            sink = jnp.concatenate([jnp.full((tq, 1), sink_ref[a], F32) for a in heads], axis=0)
        o = _attend(q, seg_fn(h), sink=sink, scale=scale)
        for g, a in enumerate(heads):
            o_ref[:, a * dv:(a + 1) * dv] = o[g * tq:(g + 1) * tq].astype(BF16)


def _dense_attn_kernel(q_ref, ktl_ref, vl_ref, ktc_ref, vc_ref, o_ref, *, n_kv, group, dq, scale, ctx_tile):
    tq = q_ref.shape[0]
    hd = HEAD_DIM

    def run(with_latent):
        for h in range(n_kv):
            heads = [h * group + g for g in range(group)]
            q = jnp.concatenate([q_ref[:, a * dq:(a + 1) * dq] for a in heads], axis=0) if group > 1 \
                else q_ref[:, h * dq:(h + 1) * dq]
            segs = [(ktc_ref[h * dq:(h + 1) * dq, :], vc_ref[:, h * LANES:(h + 1) * LANES])]
            if with_latent:
                segs = [(ktl_ref[h * dq:(h + 1) * dq, c0:c0 + ATTN_KEY_CHUNK],
                         vl_ref[c0:c0 + ATTN_KEY_CHUNK, h * LANES:(h + 1) * LANES])
                        for c0 in range(0, SEQ, ATTN_KEY_CHUNK)] + segs
            scores = []
            m = None
            for kt, _ in segs:
                s = _dot(q, kt)
                if scale is not None:
                    s = s * scale
                scores.append(s)
                ms = jnp.max(s, axis=-1, keepdims=True)
                m = ms if m is None else jnp.maximum(m, ms)
            acc = None
            for s, (_, v) in zip(scores, segs):
                pv = _dot(jnp.exp(s - m).astype(BF16), v)
                acc = pv if acc is None else acc + pv
            o = acc * pltpu.roll(1.0 / acc, hd, 1)
            for g, a in enumerate(heads):
                o_ref[:, a * hd:(a + 1) * hd] = o[g * tq:(g + 1) * tq, :hd].astype(BF16)

    if ctx_tile:
        j = pl.program_id(1)
        pl.when(j == 0)(lambda: run(False))
        pl.when(j > 0)(lambda: run(True))
    else:
        run(True)


def _gqa_attn_t_kernel(qt_ref, kl_ref, vtl_ref, kc_ref, vtc_ref, o_ref, *, n_kv, group, ctx_tile):
    tq = qt_ref.shape[1]
    hd = HEAD_DIM
    kw = kl_ref.shape[1]
    n = group * tq

    def run(with_latent):
        for h in range(n_kv):
            heads = [h * group + g for g in range(group)]
            qcat = jnp.concatenate([qt_ref[a * hd:(a + 1) * hd, :] for a in heads], axis=1)
            parts = [jnp.zeros((h * hd, n), BF16), qcat, jnp.zeros((kw - (h + 1) * hd, n), BF16)]
            wq = jnp.concatenate([p for p in parts if p.shape[0] > 0], axis=0)
            vrows = slice(h * LANES, (h + 1) * LANES)
            segs = [(kc_ref[...], vtc_ref[vrows, :])]
            if with_latent:
                segs = [(kl_ref[c0:c0 + ATTN_KEY_CHUNK, :], vtl_ref[vrows, c0:c0 + ATTN_KEY_CHUNK])
                        for c0 in range(0, SEQ, ATTN_KEY_CHUNK)] + segs
            scores = []
            m = None
            for k, _ in segs:
                s = _dot(k, wq)
                scores.append(s)
                ms = jnp.max(s, axis=0, keepdims=True)
                m = ms if m is None else jnp.maximum(m, ms)
            acc = None
            for s, (_, vt) in zip(scores, segs):
                pv = _dot(vt, jnp.exp(s - m).astype(BF16))
                acc = pv if acc is None else acc + pv
            o = acc[:hd] * (1.0 / acc[hd:hd + 1])
            for g, a in enumerate(heads):
                o_ref[:, a * hd:(a + 1) * hd] = o[:, g * tq:(g + 1) * tq].T.astype(BF16)

    if ctx_tile:
        j = pl.program_id(1)
        pl.when(j == 0)(lambda: run(False))
        pl.when(j > 0)(lambda: run(True))
    else:
        run(True)


def _q_row_block(b, j, n_batch, with_ctx):
    per_batch = SEQ // Q_TILE
    lat0 = CTX_ROWS // Q_TILE
    if with_ctx:
        return jnp.where(j == 0, b, lat0 + b * per_batch + j - 1)
    return lat0 + b * per_batch + j


def _attn_specs(n_batch, with_ctx, q, k, v, o_width, order_bj=True):
    def ix(f):
        return (lambda b, j: f(b, j)) if order_bj else (lambda j, b: f(b, j))

    def qspec(width, col):
        return pl.BlockSpec((Q_TILE, width), ix(lambda b, j: (_q_row_block(b, j, n_batch, with_ctx), col // width)))

    def lat(width, col):
        return pl.BlockSpec((SEQ, width), ix(lambda b, j: (CTX_ROWS // SEQ + b, col // width)))

    def ctx(width, col):
        return pl.BlockSpec((CTX_LEN, width), ix(lambda b, j: (b, col // width)))

    in_specs = [qspec(q[1], q[2]), lat(k[1], k[2]), lat(v[1], v[2]), ctx(k[1], k[2]), ctx(v[1], v[2])]
    args = [q[0], k[0], v[0], k[0], v[0]]
    if with_ctx:
        return in_specs, args, qspec(o_width, 0)
    out_spec = pl.BlockSpec((Q_TILE, o_width), ix(lambda b, j: (b * (SEQ // Q_TILE) + j, 0)))
    return in_specs, args, out_spec


def _attn_out_rows(t, with_ctx):
    return t if with_ctx else t - CTX_ROWS


def _dense_attn_call(n_batch, with_ctx, q, kt, v, *, n_kv, group, dq, scale=None, name):
    t = q[0].shape[0]
    in_specs, args, out_spec = _attn_specs(n_batch, with_ctx, q, v, v, 256)
    kt_rows = kt.shape[0]
    in_specs[1] = pl.BlockSpec((kt_rows, SEQ), lambda b, j: (0, CTX_ROWS // SEQ + b))
    in_specs[3] = pl.BlockSpec((kt_rows, CTX_LEN), lambda b, j: (0, b))
    args[1] = args[3] = kt
    kern = functools.partial(_dense_attn_kernel, n_kv=n_kv, group=group, dq=dq, scale=scale, ctx_tile=with_ctx)
    return pl.pallas_call(
        kern,
        grid=(n_batch, SEQ // Q_TILE + (1 if with_ctx else 0)),
        in_specs=in_specs,
        out_specs=out_spec,
        out_shape=jax.ShapeDtypeStruct((_attn_out_rows(t, with_ctx), 256), BF16),
        compiler_params=_params("arbitrary", "arbitrary"),
        name=name,
    )(*args)


def _gqa_attn_t_call(n_batch, with_ctx, qt, k, vt, *, n_kv, group, name):
    t = k.shape[0]
    kspec = (k, k.shape[1], 0)
    in_specs, args, out_spec = _attn_specs(n_batch, with_ctx, kspec, kspec, kspec, 256)
    in_specs[0] = pl.BlockSpec((qt.shape[0], Q_TILE), lambda b, j: (0, _q_row_block(b, j, n_batch, with_ctx)))
    in_specs[2] = pl.BlockSpec((vt.shape[0], SEQ), lambda b, j: (0, CTX_ROWS // SEQ + b))
    in_specs[4] = pl.BlockSpec((vt.shape[0], CTX_LEN), lambda b, j: (0, b))
    args[0] = qt
    args[2] = args[4] = vt
    return pl.pallas_call(
        functools.partial(_gqa_attn_t_kernel, n_kv=n_kv, group=group, ctx_tile=with_ctx),
        grid=(n_batch, SEQ // Q_TILE + (1 if with_ctx else 0)),
        in_specs=in_specs,
        out_specs=out_spec,
        out_shape=jax.ShapeDtypeStruct((_attn_out_rows(t, with_ctx), 256), BF16),
        compiler_params=_params("arbitrary", "arbitrary"),
        name=name,
    )(*args)


def _window_attn_kernel(sink_ref, qt_ref, kl_ref, vtl_ref, kc_ref, vtc_ref, o_ref, *, ctx_tile):
    j = pl.program_id(1)
    hd = HEAD_DIM
    group = WC_Q_HEADS // WC_KV_HEADS
    tq = qt_ref.shape[1]
    kw = kl_ref.shape[1]
    n = group * tq

    def run(with_latent):
        if with_latent:
            i = j - 1 if ctx_tile else j
            start = pl.multiple_of(jnp.clip(i * Q_TILE - WC_WINDOW, 0, SEQ - WC_WIN), WC_WINDOW)
            kpos = start + lax.broadcasted_iota(jnp.int32, (WC_WIN, n), 0)
            qpos = i * Q_TILE + lax.broadcasted_iota(jnp.int32, (WC_WIN, n), 1) % Q_TILE
            bias = jnp.where(jnp.abs(qpos - kpos) <= WC_WINDOW, 0.0, NEG_INF).astype(F32)
        lane = lax.broadcasted_iota(jnp.int32, (1, n), 1)
        for h in range(WC_KV_HEADS):
            heads = [h * group + g for g in range(group)]
            qcat = jnp.concatenate([qt_ref[a * hd:(a + 1) * hd, :] for a in heads], axis=1)
            parts = [jnp.zeros((h * hd, n), BF16), qcat, jnp.zeros((kw - (h + 1) * hd, n), BF16)]
            wq = jnp.concatenate([p for p in parts if p.shape[0] > 0], axis=0)
            vrows = slice(h * LANES, (h + 1) * LANES)
            sink = functools.reduce(lambda acc, ga: jnp.where(lane >= ga[0] * tq, sink_ref[ga[1]], acc),
                                    list(enumerate(heads)), jnp.zeros((1, n), F32))
            s_ctx = _dot(kc_ref[...], wq)
            m = jnp.maximum(jnp.max(s_ctx, axis=0, keepdims=True), sink)
            if with_latent:
                s_lat = _dot(kl_ref[pl.ds(start, WC_WIN), :], wq) + bias
                m = jnp.maximum(m, jnp.max(s_lat, axis=0, keepdims=True))
            acc = _dot(vtc_ref[vrows, :], jnp.exp(s_ctx - m).astype(BF16))
            if with_latent:
                acc = acc + _dot(vtl_ref[vrows, pl.ds(start, WC_WIN)], jnp.exp(s_lat - m).astype(BF16))
            o = acc[:hd] * (1.0 / (acc[hd:hd + 1] + jnp.exp(sink - m)))
            for g, a in enumerate(heads):
                o_ref[:, a * hd:(a + 1) * hd] = o[:, g * tq:(g + 1) * tq].T.astype(BF16)

    if ctx_tile:
        pl.when(j == 0)(lambda: run(False))
        pl.when(j > 0)(lambda: run(True))
    else:
        run(True)


def _window_attn_call(n_batch, with_ctx, qt, k, vt, sink):
    t = k.shape[0]
    kspec = (k, k.shape[1], 0)
    in_specs, args, out_spec = _attn_specs(n_batch, with_ctx, kspec, kspec, kspec, 256)
    in_specs[0] = pl.BlockSpec((qt.shape[0], Q_TILE), lambda b, j: (0, _q_row_block(b, j, n_batch, with_ctx)))
    in_specs[2] = pl.BlockSpec((vt.shape[0], SEQ), lambda b, j: (0, CTX_ROWS // SEQ + b))
    in_specs[4] = pl.BlockSpec((vt.shape[0], CTX_LEN), lambda b, j: (0, b))
    args[0] = qt
    args[2] = args[4] = vt
    return pl.pallas_call(
        functools.partial(_window_attn_kernel, ctx_tile=with_ctx),
        grid=(n_batch, SEQ // Q_TILE + (1 if with_ctx else 0)),
        in_specs=[pl.BlockSpec(memory_space=pltpu.SMEM)] + in_specs,
        out_specs=out_spec,
        out_shape=jax.ShapeDtypeStruct((_attn_out_rows(t, with_ctx), 256), BF16),
        compiler_params=_params("arbitrary", "arbitrary"),
        name="attn_window",
    )(sink, *args)


def _natten_kernel(q_ref, kl_ref, vl_ref, kc_ref, vc_ref, tab_ref, o_ref, bias_ref, *, ctx_tile):
    j = pl.program_id(0)
    hd = HEAD_DIM
    scale = HEAD_DIM ** -0.5

    def build_bias(i):
        ws = jnp.clip(NA_Q_ROWS * i - NA_KH // 2, 0, GRID_ROWS - NA_WIN_ROWS)
        for a in range(NA_Q_ROWS):
            r = NA_Q_ROWS * i + a
            krow0 = jnp.clip(r - NA_KH // 2, 0, GRID_ROWS - NA_KH)
            for pair in range(NA_WIN_ROWS // 2):
                idx = []
                for side in range(2):
                    kr = ws + 2 * pair + side
                    in_rows = (kr >= krow0) & (kr < krow0 + NA_KH)
                    idx.append(jnp.where(in_rows, kr - r + NA_KH - 1, NA_NO_ROW))
                for h in range(NA_HEADS):
                    bias_ref[h, a * GRID_W:(a + 1) * GRID_W, pair * LANES:(pair + 1) * LANES] = (
                        tab_ref[0, h, idx[0]] + tab_ref[1, h, idx[1]])

    def ctx_run():
        for h in range(NA_HEADS):
            sl = slice(h * hd, (h + 1) * hd)
            o = _attend(q_ref[:, sl] * scale, [(kc_ref[:, sl], vc_ref[:, sl], None)])
            o_ref[:, sl] = o.astype(BF16)

    def lat_run():
        i = j - 1 if ctx_tile else j
        ws = jnp.clip(NA_Q_ROWS * i - NA_KH // 2, 0, GRID_ROWS - NA_WIN_ROWS)
        start = pl.multiple_of(ws * GRID_W, GRID_W)
        pl.when(pl.program_id(1) == 0)(lambda: build_bias(i))
        for h in range(NA_HEADS):
            sl = slice(h * hd, (h + 1) * hd)
            segs = [(kl_ref[pl.ds(start, NA_WIN), sl], vl_ref[pl.ds(start, NA_WIN), sl], bias_ref[h]),
                    (kc_ref[:, sl], vc_ref[:, sl], None)]
            o_ref[:, sl] = _attend(q_ref[:, sl] * scale, segs).astype(BF16)

    if ctx_tile:
        pl.when(j == 0)(ctx_run)
        pl.when(j > 0)(lat_run)
    else:
        lat_run()


NA_NO_ROW = 2 * NA_KH - 1


def _natten_call(n_batch, with_ctx, p, bias_blocks):
    t = p.shape[0]
    q, k, v = (p, 256, COL_AQ), (p, 256, COL_AK), (p, 256, COL_AV)
    in_specs, args, out_spec = _attn_specs(n_batch, with_ctx, q, k, v, 256, order_bj=False)
    in_specs.append(pl.BlockSpec(bias_blocks.shape, lambda j, b: (0,) * bias_blocks.ndim))
    return pl.pallas_call(
        functools.partial(_natten_kernel, ctx_tile=with_ctx),
        grid=(SEQ // Q_TILE + (1 if with_ctx else 0), n_batch),
        in_specs=in_specs,
        out_specs=out_spec,
        out_shape=jax.ShapeDtypeStruct((_attn_out_rows(t, with_ctx), 256), BF16),
        scratch_shapes=[pltpu.VMEM((NA_HEADS, Q_TILE, NA_WIN), F32)],
        compiler_params=_params("arbitrary", "arbitrary"),
        name="attn_natten",
    )(*args, bias_blocks)


def _natten_bias_table(rpb):
    n_dr, n_dc = 2 * NA_KH - 1, 2 * NA_KW - 1
    col = np.arange(GRID_W)
    dc = np.clip(col[None, :] - col[:, None] + NA_KW - 1, 0, n_dc - 1)
    onehot = jnp.asarray(dc[None] == np.arange(n_dc)[:, None, None], F32)
    blocks = jnp.einsum('hdc,cqk->hdqk', rpb.astype(F32), onehot, precision=lax.Precision.HIGHEST)
    cstart = np.clip(col - NA_KW // 2, 0, GRID_W - NA_KW)
    col_ok = (col[None, :] >= cstart[:, None]) & (col[None, :] < cstart[:, None] + NA_KW)
    blocks = jnp.where(col_ok[None, None], blocks, NEG_INF)
    blocks = jnp.concatenate([blocks, jnp.full((NA_HEADS, 1, GRID_W, GRID_W), NEG_INF, F32)], axis=1)
    zero = jnp.zeros_like(blocks)
    return jnp.stack([jnp.concatenate([blocks, zero], axis=-1), jnp.concatenate([zero, blocks], axis=-1)])


def _merge_kernel(*refs, with_router, first_tile):
    if with_router:
        (oa_ref, ob_ref, oc_ref, od_ref, gate_ref, xc_ref, xl_ref, mod_ref, g2_ref, wb_ref, wo_ref, rt_ref,
         xo_ref, h2_ref, comb_ref) = refs
    else:
        (oa_ref, ob_ref, oc_ref, od_ref, gate_ref, xc_ref, xl_ref, mod_ref, g2_ref, wb_ref, wo_ref,
         xo_ref, h2_ref) = refs
    x_in = jnp.where(pl.program_id(0) + first_tile < CTX_ROWS // ROW_TILE, xc_ref[...], xl_ref[...])
    m = mod_ref[0]
    acc = None
    for n, o_ref in enumerate((oa_ref, ob_ref, oc_ref, od_ref)):
        half_y = _dot(o_ref[...], wb_ref[n])
        t = jnp.tanh(gate_ref[:, n * D_MODEL:(n + 1) * D_MODEL].astype(F32)) + 1.0
        acc = t * half_y if acc is None else acc + t * half_y
    mix = _dot(acc.astype(BF16), wo_ref[...])
    x = x_in + m[2:3] * mix
    xo_ref[...] = x
    h2 = _norm_mod(x, g2_ref[...], m[3:4], m[4:5])
    h2_ref[...] = h2.astype(BF16)
    if with_router:
        hh, hm, _ = _split3(h2)
        rh, rm = rt_ref[0], rt_ref[1]
        logits = _dot(hh, rh) + (_dot(hh, rm) + _dot(hm, rh))
        lane = lax.broadcasted_iota(jnp.int32, logits.shape, 1).astype(F32)
        logits = jnp.where(lane < N_EXPERTS, logits, NEG_INF)
        m1 = jnp.max(logits, axis=-1, keepdims=True)
        i1 = jnp.min(jnp.where(logits == m1, lane, float(LANES)), axis=-1, keepdims=True)
        rest = jnp.where(lane == i1, NEG_INF, logits)
        m2 = jnp.max(rest, axis=-1, keepdims=True)
        i2 = jnp.min(jnp.where(rest == m2, lane, float(LANES)), axis=-1, keepdims=True)
        e = jnp.exp(m2 - m1)
        w1 = 1.0 / (1.0 + e)
        w2 = e / (1.0 + e)
        comb_ref[...] = (jnp.where(lane == 0.0, i1, 0.0) + jnp.where(lane == 1.0, i2, 0.0)
                         + jnp.where(lane == 2.0, w1, 0.0) + jnp.where(lane == 3.0, w2, 0.0))


def _merge_call(outs, gate, x_ctx, x_lat, mod_l, g2, wb, wo, router3, first_tile):
    combined = x_lat is x_ctx
    t = x_lat.shape[0] if combined else CTX_ROWS + x_lat.shape[0]
    n_rows = t - first_tile * ROW_TILE

    def rows(width, col=0):
        return pl.BlockSpec((ROW_TILE, width), lambda i: (i + first_tile, col // width))

    def orow(width):
        return pl.BlockSpec((ROW_TILE, width), lambda i: (i, 0))

    def full(shape):
        return pl.BlockSpec(shape, lambda i: (0,) * len(shape))

    in_specs = [orow(256)] * 4 + [rows(GATE_W)] + _token_specs(ROW_TILE, first_tile, combined) + [
        pl.BlockSpec((1, ADA_CHUNKS, D_MODEL), lambda i: (_mod_row(i, first_tile), 0, 0)),
        full((1, D_MODEL)), full((N_BRANCH, BRANCH_W, D_MODEL)), full((D_MODEL, D_MODEL)),
    ]
    args = list(outs) + [gate, x_ctx, x_lat, mod_l, g2, wb, wo]
    out_specs = [orow(D_MODEL), orow(D_MODEL)]
    out_shape = [jax.ShapeDtypeStruct((n_rows, D_MODEL), F32), jax.ShapeDtypeStruct((n_rows, D_MODEL), BF16)]
    if router3 is not None:
        in_specs.append(full((2, D_MODEL, LANES)))
        args.append(router3)
        out_specs.append(orow(LANES))
        out_shape.append(jax.ShapeDtypeStruct((n_rows, LANES), F32))
    return pl.pallas_call(
        functools.partial(_merge_kernel, with_router=router3 is not None, first_tile=first_tile),
        grid=(n_rows // ROW_TILE,),
        in_specs=in_specs,
        out_specs=out_specs,
        out_shape=out_shape,
        compiler_params=_params("arbitrary"),
        name="merge",
    )(*args)


FFN_ROWS = 1024


def _swiglu_chunk(h, w1, w3, w2):
    half_a = 0.5 * _dot(h, w1)
    b = _dot(h, w3)
    return _dot((half_a * (jnp.tanh(half_a) + 1.0) * b).astype(BF16), w2)


def _residual_out(x, gate, update, final_norm, final_g):
    x = x + gate * update
    if final_norm:
        ms = jnp.mean(x * x, axis=-1, keepdims=True)
        x = x * lax.rsqrt(ms + NORM_EPS) * final_g
    return x


def _ffn_kernel(h_ref, w1_ref, w3_ref, w2_ref, x_ref, mod_ref, fg_ref, o_ref, acc_ref, *, final_norm):
    f = pl.program_id(1)

    @pl.when(f == 0)
    def _():
        acc_ref[...] = jnp.zeros_like(acc_ref)

    acc_ref[...] += _swiglu_chunk(h_ref[...], w1_ref[0].astype(BF16), w3_ref[0].astype(BF16),
                                  w2_ref[0].astype(BF16))

    @pl.when(f == pl.num_programs(1) - 1)
    def _():
        o_ref[...] = _residual_out(x_ref[...], mod_ref[0][5:6], acc_ref[...], final_norm, fg_ref[...])


def _tile_mod_row(i, first_tile, tile_rows):
    n_ctx = CTX_ROWS // tile_rows
    t = i + first_tile
    return jnp.where(t < n_ctx, 8, (t - n_ctx) // (SEQ // tile_rows))


def _ffn_call(h2, w1, w3, w2, layer, x, mod_l, final_g, first_row_tile, final_norm):
    t = x.shape[0]
    rows = lambda width: pl.BlockSpec((FFN_ROWS, width), lambda i, f: (i, 0))
    return pl.pallas_call(
        functools.partial(_ffn_kernel, final_norm=final_norm),
        grid=(t // FFN_ROWS, D_FF // FF_CHUNK),
        in_specs=[
            rows(D_MODEL),
            pl.BlockSpec((1, D_MODEL, FF_CHUNK), lambda i, f: (layer, 0, f)),
            pl.BlockSpec((1, D_MODEL, FF_CHUNK), lambda i, f: (layer, 0, f)),
            pl.BlockSpec((1, FF_CHUNK, D_MODEL), lambda i, f: (layer, f, 0)),
            rows(D_MODEL),
            pl.BlockSpec((1, ADA_CHUNKS, D_MODEL),
                         lambda i, f: (_tile_mod_row(i, first_row_tile, FFN_ROWS), 0, 0)),
            pl.BlockSpec((1, D_MODEL), lambda i, f: (0, 0)),
        ],
        out_specs=rows(D_MODEL),
        out_shape=jax.ShapeDtypeStruct((t, D_MODEL), F32),
        scratch_shapes=[pltpu.VMEM((FFN_ROWS, D_MODEL), F32)],
        compiler_params=_params("arbitrary", "arbitrary"),
        name="ffn",
    )(h2, w1, w3, w2, x, mod_l, final_g)


MOE_TILE = 512
MOE_SRC = 256
MOE_CMB_WIN = MOE_SRC + 16
MOE_CMB_SMALL = 144
MOE_DSP_SMALL = 144
MOE_DSP_LARGE = MOE_SRC + 16
TOP_K = 2


def _moe_route(sel, n_tok):
    n_tiles = TOP_K * n_tok // MOE_TILE + N_EXPERTS
    e = sel[:, 0:TOP_K].astype(jnp.int32)
    flat_e = e.reshape(-1)
    onehot = (flat_e[None, :] == jnp.arange(N_EXPERTS)[:, None]).astype(jnp.int32)
    csum = jnp.cumsum(onehot, axis=1)
    counts = csum[:, -1]
    padded = (counts + MOE_TILE - 1) // MOE_TILE * MOE_TILE
    seg_end = jnp.cumsum(padded)
    seg_start = seg_end - padded
    pos = jnp.sum(onehot * (seg_start[:, None] + csum - onehot), axis=0)
    pos2 = pos.reshape(n_tok, TOP_K)
    tile_ix = jnp.arange(n_tiles)
    tile_expert = jnp.minimum(jnp.sum(tile_ix[:, None] >= (seg_end // MOE_TILE)[None, :], axis=1), N_EXPERTS - 1)
    n_valid = seg_end[-1] // MOE_TILE
    n_src = n_tok // MOE_SRC
    per_chunk = MOE_SRC * TOP_K
    before = jnp.concatenate([jnp.zeros((1, N_EXPERTS), jnp.int32), csum[:, per_chunk - 1::per_chunk].T], axis=0)
    run_start = seg_start[None, :] + before[:-1]
    run_cnt = before[1:] - before[:-1]
    cum = jnp.take(before, tile_expert, axis=1)
    local0 = tile_ix * MOE_TILE - jnp.take(seg_start, tile_expert)
    local1 = jnp.minimum(local0 + MOE_TILE, jnp.take(counts, tile_expert))
    c_lo = jnp.sum(cum[1:] <= local0[None, :], axis=0)
    c_hi = jnp.sum(cum[:-1] < local1[None, :], axis=0) - 1
    win = jnp.minimum(run_start // 16 * 16, n_tiles * MOE_TILE - MOE_CMB_WIN)
    win_s = jnp.minimum(run_start // 16 * 16, n_tiles * MOE_TILE - MOE_CMB_SMALL)
    cmb_small = jnp.all(run_start - win_s + run_cnt <= MOE_CMB_SMALL, axis=1)

    def token_rows(a, dtype):
        rows = jnp.transpose(a.reshape(n_src, MOE_SRC, TOP_K), (0, 2, 1)).astype(dtype)
        return jnp.zeros((n_src, 8, MOE_SRC), dtype).at[:, :TOP_K, :].set(rows)

    i32 = lambda a: a.astype(jnp.int32)
    return {
        "n_tiles": n_tiles, "tile_expert": i32(tile_expert), "n_valid": i32(n_valid).reshape(1),
        "c_lo": i32(c_lo), "c_hi": i32(c_hi), "run_start": i32(run_start.reshape(-1)),
        "run_cnt": i32(run_cnt.reshape(-1)), "win": i32(win.reshape(-1)),
        "run_off": i32((run_start - win).reshape(-1)),
        "win_s": i32(win_s.reshape(-1)), "run_off_s": i32((run_start - win_s).reshape(-1)),
        "cmb_small": i32(cmb_small),
        "pos_rows": token_rows(pos2, jnp.int32), "w_rows": token_rows(sel[:, TOP_K:2 * TOP_K], F32),
        "pos_cols": i32(pos2),
    }


def _dispatch_kernel(clo_ref, chi_ref, te_ref, rs_ref, rc_ref, h_ref, pos_ref, w_ref, o_ref, sw_ref, acc_ref):
    i = pl.program_id(0)
    base = i * MOE_TILE
    expert = te_ref[i]
    acc_ref[...] = jnp.zeros_like(acc_ref)
    sw_ref[...] = jnp.zeros_like(sw_ref)

    c_last = chi_ref[i]

    def item(c):
        run0 = rs_ref[c * N_EXPERTS + expert]
        lo = jnp.maximum(run0, base) - base
        hi = jnp.minimum(run0 + rc_ref[c * N_EXPERTS + expert], base + MOE_TILE) - base
        small = hi - jnp.minimum(lo // 16 * 16, MOE_TILE - MOE_DSP_SMALL) <= MOE_DSP_SMALL
        return lo, hi, small

    def window(c, lo, rows, live):
        pos = pos_ref[c]
        wts = w_ref[c]
        h = h_ref[pl.ds(pl.multiple_of(c * MOE_SRC, MOE_SRC), MOE_SRC), :]
        ws = pl.multiple_of(jnp.minimum(lo // 16 * 16, MOE_TILE - rows), 16)
        slot = base + ws + lax.broadcasted_iota(jnp.int32, (rows, MOE_SRC), 0)
        slot = jnp.where(live, slot, -1)
        hit0 = pos[0:1, :] == slot
        hit1 = pos[1:2, :] == slot
        acc_ref[pl.ds(ws, rows), :] += _dot(jnp.where(hit0 | hit1, 1.0, 0.0).astype(BF16), h)
        weight = jnp.sum(jnp.where(hit0, wts[0:1, :], 0.0) + jnp.where(hit1, wts[1:2, :], 0.0),
                         axis=1, keepdims=True)
        sw_ref[pl.ds(ws, rows), :] += jnp.broadcast_to(weight, (rows, LANES))

    def body(pair, carry):
        c0 = clo_ref[i] + 2 * pair
        c1 = jnp.minimum(c0 + 1, c_last)
        live1 = c0 + 1 <= c_last
        lo0, hi0, small0 = item(c0)
        lo1, hi1, small1 = item(c1)
        both_small = small0 & small1

        @pl.when(both_small)
        def _():
            window(c0, lo0, MOE_DSP_SMALL, hi0 > lo0)
            window(c1, lo1, MOE_DSP_SMALL, live1 & (hi1 > lo1))

        @pl.when(jnp.logical_not(both_small))
        def _():
            pl.when((hi0 > lo0) & small0)(lambda: window(c0, lo0, MOE_DSP_SMALL, True))
            pl.when((hi0 > lo0) & jnp.logical_not(small0))(lambda: window(c0, lo0, MOE_DSP_LARGE, True))
            pl.when(live1 & (hi1 > lo1) & small1)(lambda: window(c1, lo1, MOE_DSP_SMALL, True))
            pl.when(live1 & (hi1 > lo1) & jnp.logical_not(small1))(lambda: window(c1, lo1, MOE_DSP_LARGE, True))
        return carry

    lax.fori_loop(0, (c_last - clo_ref[i] + 2) // 2, body, 0)
    o_ref[...] = acc_ref[...].astype(BF16)


def _dispatch_call(h2, route):
    n_tok = h2.shape[0]
    n_tiles = route["n_tiles"]
    grid_spec = pltpu.PrefetchScalarGridSpec(
        num_scalar_prefetch=5,
        grid=(n_tiles,),
        in_specs=[
            pl.BlockSpec((n_tok, D_MODEL), lambda i, *_: (0, 0), pipeline_mode=pl.Buffered(1)),
            pl.BlockSpec((n_tok // MOE_SRC, 8, MOE_SRC), lambda i, *_: (0, 0, 0), pipeline_mode=pl.Buffered(1)),
            pl.BlockSpec((n_tok // MOE_SRC, 8, MOE_SRC), lambda i, *_: (0, 0, 0), pipeline_mode=pl.Buffered(1)),
        ],
        out_specs=[pl.BlockSpec((MOE_TILE, D_MODEL), lambda i, *_: (i, 0)),
                   pl.BlockSpec((MOE_TILE, LANES), lambda i, *_: (i, 0))],
        scratch_shapes=[pltpu.VMEM((MOE_TILE, D_MODEL), F32)],
    )
    return pl.pallas_call(
        _dispatch_kernel,
        grid_spec=grid_spec,
        out_shape=[jax.ShapeDtypeStruct((n_tiles * MOE_TILE, D_MODEL), BF16),
                   jax.ShapeDtypeStruct((n_tiles * MOE_TILE, LANES), F32)],
        compiler_params=_params("arbitrary"),
        name="moe_dispatch",
    )(route["c_lo"], route["c_hi"], route["tile_expert"], route["run_start"], route["run_cnt"],
      h2, route["pos_rows"], route["w_rows"])


def _expert_kernel(te_ref, nv_ref, x_ref, w1_ref, w3_ref, w2_ref, sw_ref, o_ref, acc_ref):
    i = pl.program_id(0)
    f = pl.program_id(1)
    last = pl.num_programs(1) - 1
    valid = i < nv_ref[0]

    @pl.when(valid & (f == 0))
    def _():
        acc_ref[...] = jnp.zeros_like(acc_ref)

    @pl.when(valid)
    def _():
        acc_ref[...] += _swiglu_chunk(x_ref[...], w1_ref[0, 0], w3_ref[0, 0], w2_ref[0, 0])

    @pl.when(valid & (f == last))
    def _():
        o_ref[...] = (acc_ref[...] * sw_ref[:, 0:1]).astype(BF16)

    @pl.when(jnp.logical_not(valid) & (f == last))
    def _():
        o_ref[...] = jnp.zeros_like(o_ref)


def _expert_call(xs, slot_w, route, w1, w3, w2, layer):
    n_tiles = route["n_tiles"]
    grid_spec = pltpu.PrefetchScalarGridSpec(
        num_scalar_prefetch=2,
        grid=(n_tiles, D_FF // MOE_FF_CHUNK),
        in_specs=[
            pl.BlockSpec((MOE_TILE, D_MODEL), lambda i, f, te, nv: (i, 0)),
            pl.BlockSpec((1, 1, D_MODEL, MOE_FF_CHUNK), lambda i, f, te, nv: (layer, te[i], 0, f)),
            pl.BlockSpec((1, 1, D_MODEL, MOE_FF_CHUNK), lambda i, f, te, nv: (layer, te[i], 0, f)),
            pl.BlockSpec((1, 1, MOE_FF_CHUNK, D_MODEL), lambda i, f, te, nv: (layer, te[i], f, 0)),
            pl.BlockSpec((MOE_TILE, LANES), lambda i, f, te, nv: (i, 0)),
        ],
        out_specs=pl.BlockSpec((MOE_TILE, D_MODEL), lambda i, f, te, nv: (i, 0)),
        scratch_shapes=[pltpu.VMEM((MOE_TILE, D_MODEL), F32)],
    )
    return pl.pallas_call(
        _expert_kernel,
        grid_spec=grid_spec,
        out_shape=jax.ShapeDtypeStruct((n_tiles * MOE_TILE, D_MODEL), BF16),
        compiler_params=_params("arbitrary", "arbitrary"),
        name="moe_experts",
    )(route["tile_expert"], route["n_valid"], xs, w1, w3, w2, slot_w)


def _combine_kernel(small_ref, cnt_ref, winl_ref, offl_ref, wins_ref, offs_ref,
                    y_hbm, pos_ref, x_ref, mod_ref, fg_ref, o_ref, buf_ref, sem_ref, *, final_norm):
    c = pl.program_id(0)
    cur = c % 2
    variants = ((MOE_CMB_SMALL, wins_ref, offs_ref), (MOE_CMB_WIN, winl_ref, offl_ref))

    def window_copies(chunk, buf_set, rows, win_ref):
        return [pltpu.make_async_copy(
            y_hbm.at[pl.ds(pl.multiple_of(win_ref[chunk * N_EXPERTS + e], 16), rows), :],
            buf_ref.at[buf_set, pl.ds(e * rows, rows), :],
            sem_ref.at[buf_set, e]) for e in range(N_EXPERTS)]

    def start(chunk, buf_set):
        for use, (rows, win_ref, _) in zip((small_ref[chunk] > 0, small_ref[chunk] == 0), variants):
            @pl.when(use)
            def _():
                for cp in window_copies(chunk, buf_set, rows, win_ref):
                    cp.start()

    pl.when(c == 0)(lambda: start(0, 0))
    pl.when(c + 1 < pl.num_programs(0))(lambda: start(jnp.minimum(c + 1, pl.num_programs(0) - 1), 1 - cur))

    def process(rows, win_ref, off_ref):
        row = lax.broadcasted_iota(jnp.int32, (1, N_EXPERTS * rows), 1)
        slot = jnp.full((1, N_EXPERTS * rows), -1, jnp.int32)
        for e in range(N_EXPERTS):
            local = row - e * rows
            off = off_ref[c * N_EXPERTS + e]
            inside = (local >= off) & (local < off + cnt_ref[c * N_EXPERTS + e]) & (local < rows)
            slot = jnp.where(inside, win_ref[c * N_EXPERTS + e] + local, slot)
        pos = pos_ref[...]
        hit = (pos[:, 0:1] == slot) | (pos[:, 1:2] == slot)
        onehot = jnp.where(hit, 1.0, 0.0).astype(BF16)
        for cp in window_copies(c, cur, rows, win_ref):
            cp.wait()
        update = _dot(onehot, buf_ref[cur, 0:N_EXPERTS * rows, :])
        o_ref[...] = _residual_out(x_ref[...], mod_ref[0][5:6], update, final_norm, fg_ref[...])

    for use, variant in zip((small_ref[c] > 0, small_ref[c] == 0), variants):
        pl.when(use)(functools.partial(process, *variant))


def _combine_call(ys, route, x, mod_l, final_g, first_tile, final_norm):
    n_tok = x.shape[0]
    rows = lambda width: pl.BlockSpec((MOE_SRC, width), lambda c, *_: (c, 0))
    grid_spec = pltpu.PrefetchScalarGridSpec(
        num_scalar_prefetch=6,
        grid=(n_tok // MOE_SRC,),
        in_specs=[
            pl.BlockSpec(memory_space=pl.ANY),
            rows(TOP_K),
            rows(D_MODEL),
            pl.BlockSpec((1, ADA_CHUNKS, D_MODEL), lambda c, *_: (_tile_mod_row(c, first_tile, MOE_SRC), 0, 0)),
            pl.BlockSpec((1, D_MODEL), lambda c, *_: (0, 0)),
        ],
        out_specs=rows(D_MODEL),
        scratch_shapes=[pltpu.VMEM((2, N_EXPERTS * MOE_CMB_WIN, D_MODEL), BF16),
                        pltpu.SemaphoreType.DMA((2, N_EXPERTS))],
    )
    return pl.pallas_call(
        functools.partial(_combine_kernel, final_norm=final_norm),
        grid_spec=grid_spec,
        out_shape=jax.ShapeDtypeStruct((n_tok, D_MODEL), F32),
        compiler_params=_params("arbitrary"),
        name="moe_combine",
    )(route["cmb_small"], route["run_cnt"], route["win"], route["run_off"], route["win_s"], route["run_off_s"],
      ys, route["pos_cols"], x, mod_l, final_g)


def _rope_tables():
    t = np.arange(SEQ)

    def angles(rot_dim):
        half = rot_dim // 2
        inv = ROPE_THETA ** (-jnp.arange(0, half, 2, dtype=F32) / half)
        ang = jnp.concatenate([jnp.asarray(t // GRID_W, F32)[:, None] * inv[None, :],
                               jnp.asarray(t % GRID_W, F32)[:, None] * inv[None, :]], axis=-1)
        return jnp.cos(ang), jnp.sin(ang)

    def pad_rows(a, fill):
        return jnp.concatenate([jnp.full((ROW_TILE, a.shape[1]), fill, F32), a], axis=0)

    c, s = angles(HEAD_DIM)
    cos64 = jnp.tile(jnp.concatenate([c, c], axis=-1), (1, 4))
    sin64 = jnp.tile(jnp.concatenate([s, s], axis=-1), (1, 4))
    c, s = angles(MLA_ROPE)
    one = jnp.ones((SEQ, MLA_NOPE), F32)
    zero = jnp.zeros((SEQ, MLA_NOPE), F32)
    tail1 = jnp.ones((SEQ, MLA_PAD - MLA_NOPE - MLA_ROPE), F32)
    tail0 = jnp.zeros((SEQ, MLA_PAD - MLA_NOPE - MLA_ROPE), F32)
    cosm = jnp.tile(jnp.concatenate([one, c, c, tail1], axis=-1), (1, MLA_HEADS))
    sinm = jnp.tile(jnp.concatenate([zero, s, s, tail0], axis=-1), (1, MLA_HEADS))
    blockdiag = jnp.asarray(np.kron(np.eye(4), np.ones((HEAD_DIM, HEAD_DIM))), BF16)
    place = np.zeros((256, MLA_HEADS * MLA_PAD), np.float32)
    for h in range(MLA_HEADS):
        for r in range(MLA_ROPE):
            place[MLA_KV_RANK + r, h * MLA_PAD + MLA_NOPE + r] = 1.0
    return {
        "cos64": pad_rows(cos64, 1.0), "sin64": pad_rows(sin64, 0.0),
        "cosm": pad_rows(cosm, 1.0), "sinm": pad_rows(sinm, 0.0),
        "blockdiag": blockdiag, "pe_place": jnp.asarray(place, BF16),
    }


def _pack_w_in(w):
    n_qkv = COL_DKVA + MLA_KV_RANK + MLA_ROPE
    qkv = jnp.pad(w[..., :n_qkv].astype(BF16), ((0, 0), (0, 0), (0, COL_GATE - n_qkv)))
    return qkv, (0.5 * w[..., n_qkv:]).astype(BF16)


def _pack_mla(wqb, wkvb):
    dqh = MLA_NOPE + MLA_ROPE
    q = wqb.reshape(MLA_Q_RANK, MLA_HEADS, dqh)
    q = jnp.pad(q, ((0, 0), (0, 0), (0, MLA_PAD - dqh))).reshape(MLA_Q_RANK, MLA_HEADS * MLA_PAD)
    kv = wkvb.reshape(MLA_KV_RANK, MLA_HEADS, MLA_NOPE + MLA_V)
    k = jnp.pad(kv[:, :, :MLA_NOPE], ((0, 0), (0, 0), (0, MLA_PAD - MLA_NOPE)))
    k = k.reshape(MLA_KV_RANK, MLA_HEADS * MLA_PAD)
    v = jnp.pad(kv[:, :, MLA_NOPE:], ((0, 0), (0, 0), (0, MLA_PAD - MLA_V))).reshape(MLA_KV_RANK, MLA_HEADS * MLA_PAD)
    return q.astype(BF16), k.astype(BF16), v.astype(BF16)


def _split2_host(w):
    hi = w.astype(BF16)
    return jnp.stack([hi, (w - hi.astype(F32)).astype(BF16)])


def kernel(x, c, ctx, c_ctx, norm1_g, norm2_g, w_ada, b_ada, w_in, na_rpb, gb_qnorm, gb_knorm, wc_sink,
           mla_qnorm, mla_kvnorm, mla_wqb, mla_wkvb, w_branch, w_out, ffn_w1, ffn_w3, ffn_w2,
           moe_router, moe_w1, moe_w3, moe_w2, final_g):
    n_batch = x.shape[0]
    assert x.shape[1:] == (SEQ, D_MODEL) and ctx.shape[1:] == (CTX_LEN, D_MODEL)
    assert n_batch * CTX_LEN <= CTX_ROWS and n_batch <= 8

    ctx_rows = ctx.reshape(n_batch * CTX_LEN, D_MODEL)
    if ctx_rows.shape[0] < CTX_ROWS:
        ctx_rows = jnp.pad(ctx_rows, ((0, CTX_ROWS - ctx_rows.shape[0]), (0, 0)))
    x_ctx, x_lat = ctx_rows, x.reshape(n_batch * SEQ, D_MODEL)

    cvec = jnp.zeros((16, D_MODEL), F32).at[:n_batch].set(c).at[8].set(c_ctx)
    mod = _mod_call(cvec, w_ada, b_ada).reshape(DEPTH, 16, ADA_CHUNKS, D_MODEL)
    tabs = _rope_tables()
    w_in_p = _pack_w_in(w_in)
    moe_w = tuple(w.astype(BF16) for w in (moe_w1, moe_w3, moe_w2))
    mla_scale = (MLA_NOPE + MLA_ROPE) ** -0.5
    lat_tile0 = CTX_ROWS // ROW_TILE

    for l in range(DEPTH):
        with_ctx = l < DEPTH - 1
        wq, wk, wv = _pack_mla(mla_wqb[l], mla_wkvb[l])
        lw = {
            "gq": jnp.tile(gb_qnorm[l], 4)[None, :], "gk": jnp.tile(gb_knorm[l], 2)[None, :],
            "qn": mla_qnorm[l][None, :], "kvn": mla_kvnorm[l][None, :], "wqb": wq, "wk": wk, "wv": wv,
        }
        pa, cv, gate, qb, kb, qc, kc, qd, kd, vd, vb = _inproj_call(
            x_ctx, x_lat, mod[l], norm1_g[l][None, :], w_in_p, l, tabs, lw)

        oa = _natten_call(n_batch, with_ctx, pa, _natten_bias_table(na_rpb[l]))
        ob = _gqa_attn_t_call(n_batch, with_ctx, qb, kb, vb, n_kv=GB_KV_HEADS,
                              group=GB_Q_HEADS // GB_KV_HEADS, name="attn_global")
        oc = _window_attn_call(n_batch, with_ctx, qc, kc, cv, wc_sink[l])
        od = _dense_attn_call(n_batch, with_ctx, (qd, MLA_HEADS * MLA_PAD, 0), kd, (vd, MLA_HEADS * MLA_PAD, 0),
                              n_kv=MLA_HEADS, group=1, dq=MLA_PAD, scale=mla_scale, name="attn_mla")

        is_moe = l % 2 == 1
        router3 = None
        if is_moe:
            router3 = _split2_host(jnp.pad(moe_router[l // 2], ((0, 0), (0, LANES - N_EXPERTS))))
        first_tile = 0 if with_ctx else lat_tile0
        res = _merge_call((oa, ob, oc, od), gate, x_ctx, x_lat, mod[l], norm2_g[l][None, :],
                          (0.5 * w_branch[l]).astype(BF16), w_out[l].astype(BF16), router3, first_tile)
        x_mid, h2 = res[0], res[1]
        final_norm = l == DEPTH - 1
        if is_moe:
            route = _moe_route(res[2], h2.shape[0])
            xs, slot_w = _dispatch_call(h2, route)
            ys = _expert_call(xs, slot_w, route, *moe_w, l // 2)
            xt = _combine_call(ys, route, x_mid, mod[l], final_g[None, :],
                               0 if with_ctx else CTX_ROWS // MOE_SRC, final_norm)
        else:
            xt = _ffn_call(h2, ffn_w1, ffn_w3, ffn_w2, l // 2, x_mid, mod[l], final_g[None, :],
                           0 if with_ctx else CTX_ROWS // FFN_ROWS, final_norm)
        x_ctx = x_lat = xt

    return xt.reshape(n_batch, SEQ, D_MODEL)
```

```python
import functools

import numpy as np
import jax
import jax.numpy as jnp
from jax import lax
from jax.experimental import pallas as pl
from jax.experimental.pallas import tpu as pltpu

F32 = jnp.float32
BF16 = jnp.bfloat16

D_MODEL = 1024
SEQ = 2048
DEPTH = 4
CTX_LEN = 256
GRID_W = 64
GRID_ROWS = SEQ // GRID_W
HEAD_DIM = 64
ROPE_THETA = 10000.0
NORM_EPS = 1e-6
NEG_INF = -1e30

NA_HEADS = 4
NA_KH = 8
NA_KW = 16
GB_Q_HEADS = 4
GB_KV_HEADS = 2
WC_Q_HEADS = 4
WC_KV_HEADS = 2
WC_WINDOW = 128
MLA_HEADS = 4
MLA_Q_RANK = 256
MLA_KV_RANK = 128
MLA_NOPE = 64
MLA_ROPE = 32
MLA_V = 64
MLA_PAD = 128
N_BRANCH = 4
BRANCH_W = 256
D_FF = 3584
N_EXPERTS = 8
ADA_CHUNKS = 6

VMEM_LIMIT_BYTES = 56 * 1024 * 1024
LANES = 128

CTX_ROWS = 2048
ROW_TILE = 512
Q_TILE = 256
NA_Q_ROWS = Q_TILE // GRID_W
NA_WIN_ROWS = 12
NA_WIN = NA_WIN_ROWS * GRID_W
WC_WIN = Q_TILE + 2 * WC_WINDOW
ATTN_KEY_CHUNK = 512

COL_AQ, COL_AK, COL_AV = 0, 256, 512
COL_BQ, COL_BK, COL_BV = 768, 1024, 1152
COL_CQ, COL_CK, COL_CV = 1280, 1536, 1664
COL_DQA, COL_DKVA, COL_GATE = 1792, 2048, 2304
GATE_W = N_BRANCH * D_MODEL
P_WIDTH = COL_GATE + GATE_W
QKV_CHUNK = COL_GATE // 2
FF_CHUNK = 512
MOE_FF_CHUNK = 1792


def _params(*sem):
    return pltpu.CompilerParams(dimension_semantics=sem, vmem_limit_bytes=VMEM_LIMIT_BYTES)


def _dot(a, b):
    return jnp.dot(a, b, preferred_element_type=F32)


def _dot_nt(a, b):
    return lax.dot_general(a, b, (((1,), (1,)), ((), ())), preferred_element_type=F32)


def _split3(x):
    hi = x.astype(BF16)
    r1 = x - hi.astype(F32)
    mid = r1.astype(BF16)
    lo = (r1 - mid.astype(F32)).astype(BF16)
    return hi, mid, lo


def _sigmoid(x):
    return 0.5 * jnp.tanh(0.5 * x) + 0.5


ADA_TN = 1536


def _mod_kernel(c_ref, w_ref, b_ref, o_ref):
    c = c_ref[...]
    sc = (c * _sigmoid(c)).astype(BF16)
    o_ref[0] = _dot(sc, w_ref[0].astype(BF16)) + b_ref[0]


def _mod_call(cvec, w_ada, b_ada):
    n = ADA_CHUNKS * D_MODEL
    return pl.pallas_call(
        _mod_kernel,
        grid=(DEPTH, n // ADA_TN),
        in_specs=[
            pl.BlockSpec((16, D_MODEL), lambda l, j: (0, 0)),
            pl.BlockSpec((1, D_MODEL, ADA_TN), lambda l, j: (l, 0, j)),
            pl.BlockSpec((1, 1, ADA_TN), lambda l, j: (l, 0, j)),
        ],
        out_specs=pl.BlockSpec((1, 16, ADA_TN), lambda l, j: (l, 0, j)),
        out_shape=jax.ShapeDtypeStruct((DEPTH, 16, n), F32),
        compiler_params=_params("arbitrary", "arbitrary"),
        name="adaln_mod",
    )(cvec, w_ada, b_ada.reshape(DEPTH, 1, n))


def _mod_row(i, first_tile):
    n_ctx = CTX_ROWS // ROW_TILE
    t = i + first_tile
    return jnp.where(t < n_ctx, 8, (t - n_ctx) // (SEQ // ROW_TILE))


def _norm_mod(x, g, shift, scale):
    ms = jnp.mean(x * x, axis=-1, keepdims=True)
    y = x * lax.rsqrt(ms + NORM_EPS) * g
    return y * (1.0 + scale) + shift


def _rope(x, cos, sin, half, first_mask):
    w = x.shape[-1]
    fwd = pltpu.roll(x, w - half, 1)
    bwd = pltpu.roll(x, half, 1)
    rot = jnp.where(first_mask, -fwd, bwd)
    return x * cos + rot * sin


def _head_rms(x, gain, blockdiag):
    hi, mid, lo = _split3(x * x)
    ss = _dot(hi, blockdiag) + _dot(mid, blockdiag) + _dot(lo, blockdiag)
    return x * lax.rsqrt(ss * (1.0 / HEAD_DIM) + NORM_EPS) * gain


def _inproj_kernel(x_ref, mod_ref, g_ref, w_ref,
                   cos64_ref, sin64_ref, cosm_ref, sinm_ref,
                   gq_ref, gk_ref, bd_ref, qn_ref, kvn_ref, wqb_ref, wk_ref, wv_ref, pe_ref,
                   pa_ref, cv_ref, gate_ref, qb_ref, kb_ref, qc_ref, kc_ref, qd_ref, kd_ref, vd_ref, vb_ref):
    m = mod_ref[0]
    hb = _norm_mod(x_ref[...], g_ref[...], m[0:1], m[1:2]).astype(BF16)

    def proj(c0, c1):
        return _dot(hb, w_ref[0, :, c0:c1])

    for c0 in range(0, GATE_W, D_MODEL):
        gate_ref[:, c0:c0 + D_MODEL] = proj(COL_GATE + c0, COL_GATE + c0 + D_MODEL).astype(BF16)
    pa_ref[...] = proj(COL_AQ, COL_BQ).astype(BF16)

    scale = HEAD_DIM ** -0.5
    half = HEAD_DIM // 2
    cos64, sin64 = cos64_ref[...], sin64_ref[...]
    lane = lax.broadcasted_iota(jnp.int32, (1, 256), 1)
    first64 = (lane % HEAD_DIM) < half
    bd = bd_ref[...]
    bc = proj(COL_BQ, COL_DQA)
    bq = _head_rms(bc[:, 0:256], gq_ref[...], bd)
    qb_ref[...] = (_rope(bq, cos64, sin64, half, first64) * scale).T.astype(BF16)
    bk = _head_rms(bc[:, 256:384], gk_ref[...], bd[:128, :128])
    kb_ref[...] = _rope(bk, cos64[:, :128], sin64[:, :128], half, first64[:, :128]).astype(BF16)
    qc_ref[...] = (_rope(bc[:, 512:768], cos64, sin64, half, first64) * scale).T.astype(BF16)
    kc_ref[...] = _rope(bc[:, 768:896], cos64[:, :128], sin64[:, :128], half, first64[:, :128]).astype(BF16)
    cv = bc[:, 896:1024]
    ones_c = (lane[:, :LANES] % LANES) >= HEAD_DIM
    cv_ref[:LANES, :] = jnp.where(ones_c, 1.0, cv).T.astype(BF16)
    cv_ref[LANES:, :] = jnp.where(ones_c, 1.0, pltpu.roll(cv, HEAD_DIM, 1)).T.astype(BF16)

    cosm, sinm = cosm_ref[...], sinm_ref[...]
    lane_m = lax.broadcasted_iota(jnp.int32, (1, MLA_HEADS * MLA_PAD), 1) % MLA_PAD
    first_m = lane_m < (MLA_NOPE + MLA_ROPE // 2)
    d = proj(COL_DQA, COL_GATE)
    dqa = d[:, 0:MLA_Q_RANK]
    qn = dqa * lax.rsqrt(jnp.mean(dqa * dqa, axis=-1, keepdims=True) + NORM_EPS) * qn_ref[...]
    dq = _dot(qn.astype(BF16), wqb_ref[...])
    qd_ref[...] = _rope(dq, cosm, sinm, MLA_ROPE // 2, first_m).astype(BF16)

    dkva = d[:, MLA_Q_RANK:]
    kvc = dkva[:, :MLA_KV_RANK]
    kvn = (kvc * lax.rsqrt(jnp.mean(kvc * kvc, axis=-1, keepdims=True) + NORM_EPS) * kvn_ref[...]).astype(BF16)
    dk = _dot(kvn, wk_ref[...]) + _dot(dkva.astype(BF16), pe_ref[...])
    kd_ref[...] = _rope(dk, cosm, sinm, MLA_ROPE // 2, first_m).T.astype(BF16)
    ones_half = lane_m >= HEAD_DIM
    vd_ref[...] = jnp.where(ones_half, 1.0, _dot(kvn, wv_ref[...])).astype(BF16)
    bv = bc[:, 384:512]
    vb_ref[:LANES, :] = jnp.where(ones_half[:, :LANES], 1.0, bv).T.astype(BF16)
    vb_ref[LANES:, :] = jnp.where(ones_half[:, :LANES], 1.0, pltpu.roll(bv, HEAD_DIM, 1)).T.astype(BF16)


def _rope_row_block(i):
    n_ctx = CTX_ROWS // ROW_TILE
    return jnp.where(i < n_ctx, 0, 1 + (i - n_ctx) % (SEQ // ROW_TILE))


def _inproj_call(x, mod_l, g, w_in_p, layer, tabs, lw):
    t = x.shape[0]
    mw = MLA_HEADS * MLA_PAD

    def full(shape):
        return pl.BlockSpec(shape, lambda i: (0,) * len(shape))

    def rows(width):
        return pl.BlockSpec((ROW_TILE, width), lambda i: (i, 0))

    def tab(width):
        return pl.BlockSpec((ROW_TILE, width), lambda i: (_rope_row_block(i), 0))

    def cols(height):
        return pl.BlockSpec((height, ROW_TILE), lambda i: (0, i))

    outs = [(COL_BQ, False), (WC_KV_HEADS * LANES, True), (GATE_W, False),
            (256, True), (128, False), (256, True), (128, False), (mw, False), (mw, True), (mw, False),
            (GB_KV_HEADS * LANES, True)]
    return pl.pallas_call(
        _inproj_kernel,
        grid=(t // ROW_TILE,),
        in_specs=[
            rows(D_MODEL),
            pl.BlockSpec((1, ADA_CHUNKS, D_MODEL), lambda i: (_mod_row(i, 0), 0, 0)),
            full((1, D_MODEL)),
            pl.BlockSpec((1, D_MODEL, P_WIDTH), lambda i: (layer, 0, 0), pipeline_mode=pl.Buffered(1)),
            tab(256), tab(256), tab(mw), tab(mw),
            full((1, 256)), full((1, 128)), full((256, 256)), full((1, MLA_Q_RANK)), full((1, MLA_KV_RANK)),
            full((MLA_Q_RANK, mw)), full((MLA_KV_RANK, mw)), full((MLA_KV_RANK, mw)), full((256, mw)),
        ],
        out_specs=[cols(w) if tr else rows(w) for w, tr in outs],
        out_shape=[jax.ShapeDtypeStruct((w, t) if tr else (t, w), BF16) for w, tr in outs],
        compiler_params=_params("arbitrary"),
        name="inproj",
    )(x, mod_l, g, w_in_p, tabs["cos64"], tabs["sin64"], tabs["cosm"], tabs["sinm"],
      lw["gq"], lw["gk"], tabs["blockdiag"], lw["qn"], lw["kvn"], lw["wqb"], lw["wk"], lw["wv"], tabs["pe_place"])


def _attend(q, segs, sink=None, scale=None):
    chunks = segs
    scores = []
    m = None
    for k, _, bias in chunks:
        s = _dot_nt(q, k)
        if scale is not None:
            s = s * scale
        if bias is not None:
            s = s + bias
        scores.append(s)
        ms = jnp.max(s, axis=-1, keepdims=True)
        m = ms if m is None else jnp.maximum(m, ms)
    if sink is not None:
        m = jnp.maximum(m, sink)
    denom = None
    out = None
    for s, (_, v, _) in zip(scores, chunks):
        p = jnp.exp(s - m)
        ps = jnp.sum(p, axis=-1, keepdims=True)
        pv = _dot(p.astype(BF16), v)
        denom = ps if denom is None else denom + ps
        out = pv if out is None else out + pv
    if sink is not None:
        denom = denom + jnp.exp(sink - m)
    return out / denom


def _gqa_heads(q_ref, o_ref, sink_ref, n_kv, group, dq, dv, scale, seg_fn):
    tq = q_ref.shape[0]
    for h in range(n_kv):
        heads = [h * group + g for g in range(group)]
        q = jnp.concatenate([q_ref[:, a * dq:(a + 1) * dq] for a in heads], axis=0) if group > 1 \
            else q_ref[:, h * dq:(h + 1) * dq]
        sink = None
        if sink_ref is not None:
            sink = jnp.concatenate([jnp.full((tq, 1), sink_ref[a], F32) for a in heads], axis=0)
        o = _attend(q, seg_fn(h), sink=sink, scale=scale)
        for g, a in enumerate(heads):
            o_ref[:, a * dv:(a + 1) * dv] = o[g * tq:(g + 1) * tq].astype(BF16)


def _dense_attn_kernel(q_ref, ktl_ref, vl_ref, ktc_ref, vc_ref, o_ref, *, n_kv, group, dq, scale, ctx_tile):
    tq = q_ref.shape[0]
    hd = HEAD_DIM

    def run(with_latent):
        for h in range(n_kv):
            heads = [h * group + g for g in range(group)]
            q = jnp.concatenate([q_ref[:, a * dq:(a + 1) * dq] for a in heads], axis=0) if group > 1 \
                else q_ref[:, h * dq:(h + 1) * dq]
            segs = [(ktc_ref[h * dq:(h + 1) * dq, :], vc_ref[:, h * LANES:(h + 1) * LANES])]
            if with_latent:
                segs = [(ktl_ref[h * dq:(h + 1) * dq, c0:c0 + ATTN_KEY_CHUNK],
                         vl_ref[c0:c0 + ATTN_KEY_CHUNK, h * LANES:(h + 1) * LANES])
                        for c0 in range(0, SEQ, ATTN_KEY_CHUNK)] + segs
            scores = []
            m = None
            for kt, _ in segs:
                s = _dot(q, kt)
                if scale is not None:
                    s = s * scale
                scores.append(s)
                ms = jnp.max(s, axis=-1, keepdims=True)
                m = ms if m is None else jnp.maximum(m, ms)
            acc = None
            for s, (_, v) in zip(scores, segs):
                pv = _dot(jnp.exp(s - m).astype(BF16), v)
                acc = pv if acc is None else acc + pv
            o = acc * pltpu.roll(1.0 / acc, hd, 1)
            for g, a in enumerate(heads):
                o_ref[:, a * hd:(a + 1) * hd] = o[g * tq:(g + 1) * tq, :hd].astype(BF16)

    if ctx_tile:
        j = pl.program_id(1)
        pl.when(j == 0)(lambda: run(False))
        pl.when(j > 0)(lambda: run(True))
    else:
        run(True)


def _gqa_attn_t_kernel(qt_ref, kl_ref, vtl_ref, kc_ref, vtc_ref, o_ref, *, n_kv, group, ctx_tile):
    tq = qt_ref.shape[1]
    hd = HEAD_DIM
    kw = kl_ref.shape[1]
    n = group * tq

    def run(with_latent):
        for h in range(n_kv):
            heads = [h * group + g for g in range(group)]
            qcat = jnp.concatenate([qt_ref[a * hd:(a + 1) * hd, :] for a in heads], axis=1)
            parts = [jnp.zeros((h * hd, n), BF16), qcat, jnp.zeros((kw - (h + 1) * hd, n), BF16)]
            wq = jnp.concatenate([p for p in parts if p.shape[0] > 0], axis=0)
            vrows = slice(h * LANES, (h + 1) * LANES)
            segs = [(kc_ref[...], vtc_ref[vrows, :])]
            if with_latent:
                segs = [(kl_ref[c0:c0 + ATTN_KEY_CHUNK, :], vtl_ref[vrows, c0:c0 + ATTN_KEY_CHUNK])
                        for c0 in range(0, SEQ, ATTN_KEY_CHUNK)] + segs
            scores = []
            m = None
            for k, _ in segs:
                s = _dot(k, wq)
                scores.append(s)
                ms = jnp.max(s, axis=0, keepdims=True)
                m = ms if m is None else jnp.maximum(m, ms)
            acc = None
            for s, (_, vt) in zip(scores, segs):
                pv = _dot(vt, jnp.exp(s - m).astype(BF16))
                acc = pv if acc is None else acc + pv
            o = acc[:hd] * (1.0 / acc[hd:hd + 1])
            for g, a in enumerate(heads):
                o_ref[:, a * hd:(a + 1) * hd] = o[:, g * tq:(g + 1) * tq].T.astype(BF16)

    if ctx_tile:
        j = pl.program_id(1)
        pl.when(j == 0)(lambda: run(False))
        pl.when(j > 0)(lambda: run(True))
    else:
        run(True)


def _q_row_block(b, j, n_batch, with_ctx):
    per_batch = SEQ // Q_TILE
    lat0 = CTX_ROWS // Q_TILE
    if with_ctx:
        return jnp.where(j == 0, b, lat0 + b * per_batch + j - 1)
    return lat0 + b * per_batch + j


def _attn_specs(n_batch, with_ctx, q, k, v, o_width, order_bj=True):
    def ix(f):
        return (lambda b, j: f(b, j)) if order_bj else (lambda j, b: f(b, j))

    def qspec(width, col):
        return pl.BlockSpec((Q_TILE, width), ix(lambda b, j: (_q_row_block(b, j, n_batch, with_ctx), col // width)))

    def lat(width, col):
        return pl.BlockSpec((SEQ, width), ix(lambda b, j: (CTX_ROWS // SEQ + b, col // width)))

    def ctx(width, col):
        return pl.BlockSpec((CTX_LEN, width), ix(lambda b, j: (b, col // width)))

    in_specs = [qspec(q[1], q[2]), lat(k[1], k[2]), lat(v[1], v[2]), ctx(k[1], k[2]), ctx(v[1], v[2])]
    args = [q[0], k[0], v[0], k[0], v[0]]
    if with_ctx:
        return in_specs, args, qspec(o_width, 0)
    out_spec = pl.BlockSpec((Q_TILE, o_width), ix(lambda b, j: (b * (SEQ // Q_TILE) + j, 0)))
    return in_specs, args, out_spec


def _attn_out_rows(t, with_ctx):
    return t if with_ctx else t - CTX_ROWS


def _dense_attn_call(n_batch, with_ctx, q, kt, v, *, n_kv, group, dq, scale=None, name):
    t = q[0].shape[0]
    in_specs, args, out_spec = _attn_specs(n_batch, with_ctx, q, v, v, 256)
    kt_rows = kt.shape[0]
    in_specs[1] = pl.BlockSpec((kt_rows, SEQ), lambda b, j: (0, CTX_ROWS // SEQ + b))
    in_specs[3] = pl.BlockSpec((kt_rows, CTX_LEN), lambda b, j: (0, b))
    args[1] = args[3] = kt
    kern = functools.partial(_dense_attn_kernel, n_kv=n_kv, group=group, dq=dq, scale=scale, ctx_tile=with_ctx)
    return pl.pallas_call(
        kern,
        grid=(n_batch, SEQ // Q_TILE + (1 if with_ctx else 0)),
        in_specs=in_specs,
        out_specs=out_spec,
        out_shape=jax.ShapeDtypeStruct((_attn_out_rows(t, with_ctx), 256), BF16),
        compiler_params=_params("arbitrary", "arbitrary"),
        name=name,
    )(*args)


def _gqa_attn_t_call(n_batch, with_ctx, qt, k, vt, *, n_kv, group, name):
    t = k.shape[0]
    kspec = (k, k.shape[1], 0)
    in_specs, args, out_spec = _attn_specs(n_batch, with_ctx, kspec, kspec, kspec, 256)
    in_specs[0] = pl.BlockSpec((qt.shape[0], Q_TILE), lambda b, j: (0, _q_row_block(b, j, n_batch, with_ctx)))
    in_specs[2] = pl.BlockSpec((vt.shape[0], SEQ), lambda b, j: (0, CTX_ROWS // SEQ + b))
    in_specs[4] = pl.BlockSpec((vt.shape[0], CTX_LEN), lambda b, j: (0, b))
    args[0] = qt
    args[2] = args[4] = vt
    return pl.pallas_call(
        functools.partial(_gqa_attn_t_kernel, n_kv=n_kv, group=group, ctx_tile=with_ctx),
        grid=(n_batch, SEQ // Q_TILE + (1 if with_ctx else 0)),
        in_specs=in_specs,
        out_specs=out_spec,
        out_shape=jax.ShapeDtypeStruct((_attn_out_rows(t, with_ctx), 256), BF16),
        compiler_params=_params("arbitrary", "arbitrary"),
        name=name,
    )(*args)


def _window_attn_kernel(sink_ref, qt_ref, kl_ref, vtl_ref, kc_ref, vtc_ref, o_ref, *, ctx_tile):
    j = pl.program_id(1)
    hd = HEAD_DIM
    group = WC_Q_HEADS // WC_KV_HEADS
    tq = qt_ref.shape[1]
    kw = kl_ref.shape[1]
    n = group * tq

    def run(with_latent):
        if with_latent:
            i = j - 1 if ctx_tile else j
            start = pl.multiple_of(jnp.clip(i * Q_TILE - WC_WINDOW, 0, SEQ - WC_WIN), WC_WINDOW)
            kpos = start + lax.broadcasted_iota(jnp.int32, (WC_WIN, n), 0)
            qpos = i * Q_TILE + lax.broadcasted_iota(jnp.int32, (WC_WIN, n), 1) % Q_TILE
            bias = jnp.where(jnp.abs(qpos - kpos) <= WC_WINDOW, 0.0, NEG_INF).astype(F32)
        lane = lax.broadcasted_iota(jnp.int32, (1, n), 1)
        for h in range(WC_KV_HEADS):
            heads = [h * group + g for g in range(group)]
            qcat = jnp.concatenate([qt_ref[a * hd:(a + 1) * hd, :] for a in heads], axis=1)
            parts = [jnp.zeros((h * hd, n), BF16), qcat, jnp.zeros((kw - (h + 1) * hd, n), BF16)]
            wq = jnp.concatenate([p for p in parts if p.shape[0] > 0], axis=0)
            vrows = slice(h * LANES, (h + 1) * LANES)
            sink = functools.reduce(lambda acc, ga: jnp.where(lane >= ga[0] * tq, sink_ref[ga[1]], acc),
                                    list(enumerate(heads)), jnp.zeros((1, n), F32))
            s_ctx = _dot(kc_ref[...], wq)
            m = jnp.maximum(jnp.max(s_ctx, axis=0, keepdims=True), sink)
            if with_latent:
                s_lat = _dot(kl_ref[pl.ds(start, WC_WIN), :], wq) + bias
                m = jnp.maximum(m, jnp.max(s_lat, axis=0, keepdims=True))
            acc = _dot(vtc_ref[vrows, :], jnp.exp(s_ctx - m).astype(BF16))
            if with_latent:
                acc = acc + _dot(vtl_ref[vrows, pl.ds(start, WC_WIN)], jnp.exp(s_lat - m).astype(BF16))
            o = acc[:hd] * (1.0 / (acc[hd:hd + 1] + jnp.exp(sink - m)))
            for g, a in enumerate(heads):
                o_ref[:, a * hd:(a + 1) * hd] = o[:, g * tq:(g + 1) * tq].T.astype(BF16)

    if ctx_tile:
        pl.when(j == 0)(lambda: run(False))
        pl.when(j > 0)(lambda: run(True))
    else:
        run(True)


def _window_attn_call(n_batch, with_ctx, qt, k, vt, sink):
    t = k.shape[0]
    kspec = (k, k.shape[1], 0)
    in_specs, args, out_spec = _attn_specs(n_batch, with_ctx, kspec, kspec, kspec, 256)
    in_specs[0] = pl.BlockSpec((qt.shape[0], Q_TILE), lambda b, j: (0, _q_row_block(b, j, n_batch, with_ctx)))
    in_specs[2] = pl.BlockSpec((vt.shape[0], SEQ), lambda b, j: (0, CTX_ROWS // SEQ + b))
    in_specs[4] = pl.BlockSpec((vt.shape[0], CTX_LEN), lambda b, j: (0, b))
    args[0] = qt
    args[2] = args[4] = vt
    return pl.pallas_call(
        functools.partial(_window_attn_kernel, ctx_tile=with_ctx),
        grid=(n_batch, SEQ // Q_TILE + (1 if with_ctx else 0)),
        in_specs=[pl.BlockSpec(memory_space=pltpu.SMEM)] + in_specs,
        out_specs=out_spec,
        out_shape=jax.ShapeDtypeStruct((_attn_out_rows(t, with_ctx), 256), BF16),
        compiler_params=_params("arbitrary", "arbitrary"),
        name="attn_window",
    )(sink, *args)


def _natten_kernel(q_ref, kl_ref, vl_ref, kc_ref, vc_ref, tab_ref, o_ref, bias_ref, *, ctx_tile):
    j = pl.program_id(0)
    hd = HEAD_DIM
    scale = HEAD_DIM ** -0.5

    def build_bias(i):
        ws = jnp.clip(NA_Q_ROWS * i - NA_KH // 2, 0, GRID_ROWS - NA_WIN_ROWS)
        for a in range(NA_Q_ROWS):
            r = NA_Q_ROWS * i + a
            krow0 = jnp.clip(r - NA_KH // 2, 0, GRID_ROWS - NA_KH)
            for pair in range(NA_WIN_ROWS // 2):
                idx = []
                for side in range(2):
                    kr = ws + 2 * pair + side
                    in_rows = (kr >= krow0) & (kr < krow0 + NA_KH)
                    idx.append(jnp.where(in_rows, kr - r + NA_KH - 1, NA_NO_ROW))
                for h in range(NA_HEADS):
                    bias_ref[h, a * GRID_W:(a + 1) * GRID_W, pair * LANES:(pair + 1) * LANES] = (
                        tab_ref[0, h, idx[0]] + tab_ref[1, h, idx[1]])

    def ctx_run():
        for h in range(NA_HEADS):
            sl = slice(h * hd, (h + 1) * hd)
            o = _attend(q_ref[:, sl] * scale, [(kc_ref[:, sl], vc_ref[:, sl], None)])
            o_ref[:, sl] = o.astype(BF16)

    def lat_run():
        i = j - 1 if ctx_tile else j
        ws = jnp.clip(NA_Q_ROWS * i - NA_KH // 2, 0, GRID_ROWS - NA_WIN_ROWS)
        start = pl.multiple_of(ws * GRID_W, GRID_W)
        pl.when(pl.program_id(1) == 0)(lambda: build_bias(i))
        for h in range(NA_HEADS):
            sl = slice(h * hd, (h + 1) * hd)
            segs = [(kl_ref[pl.ds(start, NA_WIN), sl], vl_ref[pl.ds(start, NA_WIN), sl], bias_ref[h]),
                    (kc_ref[:, sl], vc_ref[:, sl], None)]
            o_ref[:, sl] = _attend(q_ref[:, sl] * scale, segs).astype(BF16)

    if ctx_tile:
        pl.when(j == 0)(ctx_run)
        pl.when(j > 0)(lat_run)
    else:
        lat_run()


NA_NO_ROW = 2 * NA_KH - 1


def _natten_call(n_batch, with_ctx, p, bias_blocks):
    t = p.shape[0]
    q, k, v = (p, 256, COL_AQ), (p, 256, COL_AK), (p, 256, COL_AV)
    in_specs, args, out_spec = _attn_specs(n_batch, with_ctx, q, k, v, 256, order_bj=False)
    in_specs.append(pl.BlockSpec(bias_blocks.shape, lambda j, b: (0,) * bias_blocks.ndim))
    return pl.pallas_call(
        functools.partial(_natten_kernel, ctx_tile=with_ctx),
        grid=(SEQ // Q_TILE + (1 if with_ctx else 0), n_batch),
        in_specs=in_specs,
        out_specs=out_spec,
        out_shape=jax.ShapeDtypeStruct((_attn_out_rows(t, with_ctx), 256), BF16),
        scratch_shapes=[pltpu.VMEM((NA_HEADS, Q_TILE, NA_WIN), F32)],
        compiler_params=_params("arbitrary", "arbitrary"),
        name="attn_natten",
    )(*args, bias_blocks)


def _natten_bias_table(rpb):
    n_dr, n_dc = 2 * NA_KH - 1, 2 * NA_KW - 1
    col = np.arange(GRID_W)
    dc = np.clip(col[None, :] - col[:, None] + NA_KW - 1, 0, n_dc - 1)
    onehot = jnp.asarray(dc[None] == np.arange(n_dc)[:, None, None], F32)
    blocks = jnp.einsum('hdc,cqk->hdqk', rpb.astype(F32), onehot, precision=lax.Precision.HIGHEST)
    cstart = np.clip(col - NA_KW // 2, 0, GRID_W - NA_KW)
    col_ok = (col[None, :] >= cstart[:, None]) & (col[None, :] < cstart[:, None] + NA_KW)
    blocks = jnp.where(col_ok[None, None], blocks, NEG_INF)
    blocks = jnp.concatenate([blocks, jnp.full((NA_HEADS, 1, GRID_W, GRID_W), NEG_INF, F32)], axis=1)
    zero = jnp.zeros_like(blocks)
    return jnp.stack([jnp.concatenate([blocks, zero], axis=-1), jnp.concatenate([zero, blocks], axis=-1)])


def _merge_kernel(*refs, with_router):
    if with_router:
        (oa_ref, ob_ref, oc_ref, od_ref, gate_ref, x_ref, mod_ref, g2_ref, wb_ref, wo_ref, rt_ref,
         xo_ref, h2_ref, comb_ref) = refs
    else:
        (oa_ref, ob_ref, oc_ref, od_ref, gate_ref, x_ref, mod_ref, g2_ref, wb_ref, wo_ref,
         xo_ref, h2_ref) = refs
    m = mod_ref[0]
    acc = None
    for n, o_ref in enumerate((oa_ref, ob_ref, oc_ref, od_ref)):
        half_y = _dot(o_ref[...], wb_ref[n])
        t = jnp.tanh(gate_ref[:, n * D_MODEL:(n + 1) * D_MODEL].astype(F32)) + 1.0
        acc = t * half_y if acc is None else acc + t * half_y
    mix = _dot(acc.astype(BF16), wo_ref[...])
    x = x_ref[...] + m[2:3] * mix
    xo_ref[...] = x
    h2 = _norm_mod(x, g2_ref[...], m[3:4], m[4:5])
    h2_ref[...] = h2.astype(BF16)
    if with_router:
        hh, hm, _ = _split3(h2)
        rh, rm = rt_ref[0], rt_ref[1]
        logits = _dot(hh, rh) + (_dot(hh, rm) + _dot(hm, rh))
        lane = lax.broadcasted_iota(jnp.int32, logits.shape, 1).astype(F32)
        logits = jnp.where(lane < N_EXPERTS, logits, NEG_INF)
        m1 = jnp.max(logits, axis=-1, keepdims=True)
        i1 = jnp.min(jnp.where(logits == m1, lane, float(LANES)), axis=-1, keepdims=True)
        rest = jnp.where(lane == i1, NEG_INF, logits)
        m2 = jnp.max(rest, axis=-1, keepdims=True)
        i2 = jnp.min(jnp.where(rest == m2, lane, float(LANES)), axis=-1, keepdims=True)
        e = jnp.exp(m2 - m1)
        w1 = 1.0 / (1.0 + e)
        w2 = e / (1.0 + e)
        comb_ref[...] = (jnp.where(lane == 0.0, i1, 0.0) + jnp.where(lane == 1.0, i2, 0.0)
                         + jnp.where(lane == 2.0, w1, 0.0) + jnp.where(lane == 3.0, w2, 0.0))


def _merge_call(outs, gate, x, mod_l, g2, wb, wo, router3, first_tile):
    t = x.shape[0]
    n_rows = t - first_tile * ROW_TILE

    def rows(width, col=0):
        return pl.BlockSpec((ROW_TILE, width), lambda i: (i + first_tile, col // width))

    def orow(width):
        return pl.BlockSpec((ROW_TILE, width), lambda i: (i, 0))

    def full(shape):
        return pl.BlockSpec(shape, lambda i: (0,) * len(shape))

    in_specs = [orow(256)] * 4 + [
        rows(GATE_W),
        rows(D_MODEL),
        pl.BlockSpec((1, ADA_CHUNKS, D_MODEL), lambda i: (_mod_row(i, first_tile), 0, 0)),
        full((1, D_MODEL)), full((N_BRANCH, BRANCH_W, D_MODEL)), full((D_MODEL, D_MODEL)),
    ]
    args = list(outs) + [gate, x, mod_l, g2, wb, wo]
    out_specs = [orow(D_MODEL), orow(D_MODEL)]
    out_shape = [jax.ShapeDtypeStruct((n_rows, D_MODEL), F32), jax.ShapeDtypeStruct((n_rows, D_MODEL), BF16)]
    if router3 is not None:
        in_specs.append(full((2, D_MODEL, LANES)))
        args.append(router3)
        out_specs.append(orow(LANES))
        out_shape.append(jax.ShapeDtypeStruct((n_rows, LANES), F32))
    return pl.pallas_call(
        functools.partial(_merge_kernel, with_router=router3 is not None),
        grid=(n_rows // ROW_TILE,),
        in_specs=in_specs,
        out_specs=out_specs,
        out_shape=out_shape,
        compiler_params=_params("arbitrary"),
        name="merge",
    )(*args)


FFN_ROWS = 1024


def _swiglu_chunk(h, w1, w3, w2):
    half_a = 0.5 * _dot(h, w1)
    b = _dot(h, w3)
    return _dot((half_a * (jnp.tanh(half_a) + 1.0) * b).astype(BF16), w2)


def _residual_out(x, gate, update, final_norm, final_g):
    x = x + gate * update
    if final_norm:
        ms = jnp.mean(x * x, axis=-1, keepdims=True)
        x = x * lax.rsqrt(ms + NORM_EPS) * final_g
    return x


def _ffn_kernel(h_ref, w1_ref, w3_ref, w2_ref, x_ref, mod_ref, fg_ref, o_ref, acc_ref, *, final_norm):
    f = pl.program_id(1)

    @pl.when(f == 0)
    def _():
        acc_ref[...] = jnp.zeros_like(acc_ref)

    acc_ref[...] += _swiglu_chunk(h_ref[...], w1_ref[0].astype(BF16), w3_ref[0].astype(BF16),
                                  w2_ref[0].astype(BF16))

    @pl.when(f == pl.num_programs(1) - 1)
    def _():
        o_ref[...] = _residual_out(x_ref[...], mod_ref[0][5:6], acc_ref[...], final_norm, fg_ref[...])


def _tile_mod_row(i, first_tile, tile_rows):
    n_ctx = CTX_ROWS // tile_rows
    t = i + first_tile
    return jnp.where(t < n_ctx, 8, (t - n_ctx) // (SEQ // tile_rows))


def _ffn_call(h2, w1, w3, w2, layer, x, mod_l, final_g, first_row_tile, final_norm):
    t = x.shape[0]
    rows = lambda width: pl.BlockSpec((FFN_ROWS, width), lambda i, f: (i, 0))
    return pl.pallas_call(
        functools.partial(_ffn_kernel, final_norm=final_norm),
        grid=(t // FFN_ROWS, D_FF // FF_CHUNK),
        in_specs=[
            rows(D_MODEL),
            pl.BlockSpec((1, D_MODEL, FF_CHUNK), lambda i, f: (layer, 0, f)),
            pl.BlockSpec((1, D_MODEL, FF_CHUNK), lambda i, f: (layer, 0, f)),
            pl.BlockSpec((1, FF_CHUNK, D_MODEL), lambda i, f: (layer, f, 0)),
            rows(D_MODEL),
            pl.BlockSpec((1, ADA_CHUNKS, D_MODEL),
                         lambda i, f: (_tile_mod_row(i, first_row_tile, FFN_ROWS), 0, 0)),
            pl.BlockSpec((1, D_MODEL), lambda i, f: (0, 0)),
        ],
        out_specs=rows(D_MODEL),
        out_shape=jax.ShapeDtypeStruct((t, D_MODEL), F32),
        scratch_shapes=[pltpu.VMEM((FFN_ROWS, D_MODEL), F32)],
        compiler_params=_params("arbitrary", "arbitrary"),
        name="ffn",
    )(h2, w1, w3, w2, x, mod_l, final_g)


MOE_TILE = 512
MOE_SRC = 256
MOE_CMB_WIN = MOE_SRC + 16
MOE_CMB_SMALL = 96
MOE_DSP_SMALL = 96
MOE_DSP_LARGE = MOE_SRC + 16
TOP_K = 2


def _moe_route(sel, n_tok):
    n_tiles = TOP_K * n_tok // MOE_TILE + N_EXPERTS
    e = sel[:, 0:TOP_K].astype(jnp.int32)
    flat_e = e.reshape(-1)
    onehot = (flat_e[None, :] == jnp.arange(N_EXPERTS)[:, None]).astype(jnp.int32)
    csum = jnp.cumsum(onehot, axis=1)
    counts = csum[:, -1]
    padded = (counts + MOE_TILE - 1) // MOE_TILE * MOE_TILE
    seg_end = jnp.cumsum(padded)
    seg_start = seg_end - padded
    pos = jnp.sum(onehot * (seg_start[:, None] + csum - onehot), axis=0)
    pos2 = pos.reshape(n_tok, TOP_K)
    tile_ix = jnp.arange(n_tiles)
    tile_expert = jnp.minimum(jnp.sum(tile_ix[:, None] >= (seg_end // MOE_TILE)[None, :], axis=1), N_EXPERTS - 1)
    n_valid = seg_end[-1] // MOE_TILE
    n_src = n_tok // MOE_SRC
    per_chunk = MOE_SRC * TOP_K
    before = jnp.concatenate([jnp.zeros((1, N_EXPERTS), jnp.int32), csum[:, per_chunk - 1::per_chunk].T], axis=0)
    run_start = seg_start[None, :] + before[:-1]
    run_cnt = before[1:] - before[:-1]
    cum = jnp.take(before, tile_expert, axis=1)
    local0 = tile_ix * MOE_TILE - jnp.take(seg_start, tile_expert)
    local1 = jnp.minimum(local0 + MOE_TILE, jnp.take(counts, tile_expert))
    c_lo = jnp.sum(cum[1:] <= local0[None, :], axis=0)
    c_hi = jnp.sum(cum[:-1] < local1[None, :], axis=0) - 1
    win = jnp.minimum(run_start // 16 * 16, n_tiles * MOE_TILE - MOE_CMB_WIN)
    win_s = jnp.minimum(run_start // 16 * 16, n_tiles * MOE_TILE - MOE_CMB_SMALL)
    cmb_small = jnp.all(run_start - win_s + run_cnt <= MOE_CMB_SMALL, axis=1)

    def token_rows(a, dtype):
        rows = jnp.transpose(a.reshape(n_src, MOE_SRC, TOP_K), (0, 2, 1)).astype(dtype)
        return jnp.zeros((n_src, 8, MOE_SRC), dtype).at[:, :TOP_K, :].set(rows)

    i32 = lambda a: a.astype(jnp.int32)
    return {
        "n_tiles": n_tiles, "tile_expert": i32(tile_expert), "n_valid": i32(n_valid).reshape(1),
        "c_lo": i32(c_lo), "c_hi": i32(c_hi), "run_start": i32(run_start.reshape(-1)),
        "run_cnt": i32(run_cnt.reshape(-1)), "win": i32(win.reshape(-1)),
        "run_off": i32((run_start - win).reshape(-1)),
        "win_s": i32(win_s.reshape(-1)), "run_off_s": i32((run_start - win_s).reshape(-1)),
        "cmb_small": i32(cmb_small),
        "pos_rows": token_rows(pos2, jnp.int32), "w_rows": token_rows(sel[:, TOP_K:2 * TOP_K], F32),
        "pos_cols": i32(pos2),
    }


def _dispatch_kernel(clo_ref, chi_ref, te_ref, rs_ref, rc_ref, h_ref, pos_ref, w_ref, o_ref, sw_ref, acc_ref):
    i = pl.program_id(0)
    base = i * MOE_TILE
    expert = te_ref[i]
    acc_ref[...] = jnp.zeros_like(acc_ref)
    sw_ref[...] = jnp.zeros_like(sw_ref)

    c_last = chi_ref[i]

    def item(c):
        run0 = rs_ref[c * N_EXPERTS + expert]
        lo = jnp.maximum(run0, base) - base
        hi = jnp.minimum(run0 + rc_ref[c * N_EXPERTS + expert], base + MOE_TILE) - base
        small = hi - jnp.minimum(lo // 16 * 16, MOE_TILE - MOE_DSP_SMALL) <= MOE_DSP_SMALL
        return lo, hi, small

    def window(c, lo, rows, live):
        pos = pos_ref[c]
        wts = w_ref[c]
        h = h_ref[pl.ds(pl.multiple_of(c * MOE_SRC, MOE_SRC), MOE_SRC), :]
        ws = pl.multiple_of(jnp.minimum(lo // 16 * 16, MOE_TILE - rows), 16)
        slot = base + ws + lax.broadcasted_iota(jnp.int32, (rows, MOE_SRC), 0)
        slot = jnp.where(live, slot, -1)
        hit0 = pos[0:1, :] == slot
        hit1 = pos[1:2, :] == slot
        acc_ref[pl.ds(ws, rows), :] += _dot(jnp.where(hit0 | hit1, 1.0, 0.0).astype(BF16), h)
        weight = jnp.sum(jnp.where(hit0, wts[0:1, :], 0.0) + jnp.where(hit1, wts[1:2, :], 0.0),
                         axis=1, keepdims=True)
        sw_ref[pl.ds(ws, rows), :] += jnp.broadcast_to(weight, (rows, LANES))

    def body(pair, carry):
        c0 = clo_ref[i] + 2 * pair
        c1 = jnp.minimum(c0 + 1, c_last)
        live1 = c0 + 1 <= c_last
        lo0, hi0, small0 = item(c0)
        lo1, hi1, small1 = item(c1)
        both_small = small0 & small1

        @pl.when(both_small)
        def _():
            window(c0, lo0, MOE_DSP_SMALL, hi0 > lo0)
            window(c1, lo1, MOE_DSP_SMALL, live1 & (hi1 > lo1))

        @pl.when(jnp.logical_not(both_small))
        def _():
            pl.when((hi0 > lo0) & small0)(lambda: window(c0, lo0, MOE_DSP_SMALL, True))
            pl.when((hi0 > lo0) & jnp.logical_not(small0))(lambda: window(c0, lo0, MOE_DSP_LARGE, True))
            pl.when(live1 & (hi1 > lo1) & small1)(lambda: window(c1, lo1, MOE_DSP_SMALL, True))
            pl.when(live1 & (hi1 > lo1) & jnp.logical_not(small1))(lambda: window(c1, lo1, MOE_DSP_LARGE, True))
        return carry

    lax.fori_loop(0, (c_last - clo_ref[i] + 2) // 2, body, 0)
    o_ref[...] = acc_ref[...].astype(BF16)


def _dispatch_call(h2, route):
    n_tok = h2.shape[0]
    n_tiles = route["n_tiles"]
    grid_spec = pltpu.PrefetchScalarGridSpec(
        num_scalar_prefetch=5,
        grid=(n_tiles,),
        in_specs=[
            pl.BlockSpec((n_tok, D_MODEL), lambda i, *_: (0, 0), pipeline_mode=pl.Buffered(1)),
            pl.BlockSpec((n_tok // MOE_SRC, 8, MOE_SRC), lambda i, *_: (0, 0, 0), pipeline_mode=pl.Buffered(1)),
            pl.BlockSpec((n_tok // MOE_SRC, 8, MOE_SRC), lambda i, *_: (0, 0, 0), pipeline_mode=pl.Buffered(1)),
        ],
        out_specs=[pl.BlockSpec((MOE_TILE, D_MODEL), lambda i, *_: (i, 0)),
                   pl.BlockSpec((MOE_TILE, LANES), lambda i, *_: (i, 0))],
        scratch_shapes=[pltpu.VMEM((MOE_TILE, D_MODEL), F32)],
    )
    return pl.pallas_call(
        _dispatch_kernel,
        grid_spec=grid_spec,
        out_shape=[jax.ShapeDtypeStruct((n_tiles * MOE_TILE, D_MODEL), BF16),
                   jax.ShapeDtypeStruct((n_tiles * MOE_TILE, LANES), F32)],
        compiler_params=_params("arbitrary"),
        name="moe_dispatch",
    )(route["c_lo"], route["c_hi"], route["tile_expert"], route["run_start"], route["run_cnt"],
      h2, route["pos_rows"], route["w_rows"])


def _expert_kernel(te_ref, nv_ref, x_ref, w1_ref, w3_ref, w2_ref, sw_ref, o_ref, acc_ref):
    i = pl.program_id(0)
    f = pl.program_id(1)
    last = pl.num_programs(1) - 1
    valid = i < nv_ref[0]

    @pl.when(valid & (f == 0))
    def _():
        acc_ref[...] = jnp.zeros_like(acc_ref)

    @pl.when(valid)
    def _():
        acc_ref[...] += _swiglu_chunk(x_ref[...], w1_ref[0, 0], w3_ref[0, 0], w2_ref[0, 0])

    @pl.when(valid & (f == last))
    def _():
        o_ref[...] = (acc_ref[...] * sw_ref[:, 0:1]).astype(BF16)

    @pl.when(jnp.logical_not(valid) & (f == last))
    def _():
        o_ref[...] = jnp.zeros_like(o_ref)


def _expert_call(xs, slot_w, route, w1, w3, w2, layer):
    n_tiles = route["n_tiles"]
    grid_spec = pltpu.PrefetchScalarGridSpec(
        num_scalar_prefetch=2,
        grid=(n_tiles, D_FF // MOE_FF_CHUNK),
        in_specs=[
            pl.BlockSpec((MOE_TILE, D_MODEL), lambda i, f, te, nv: (i, 0)),
            pl.BlockSpec((1, 1, D_MODEL, MOE_FF_CHUNK), lambda i, f, te, nv: (layer, te[i], 0, f)),
            pl.BlockSpec((1, 1, D_MODEL, MOE_FF_CHUNK), lambda i, f, te, nv: (layer, te[i], 0, f)),
            pl.BlockSpec((1, 1, MOE_FF_CHUNK, D_MODEL), lambda i, f, te, nv: (layer, te[i], f, 0)),
            pl.BlockSpec((MOE_TILE, LANES), lambda i, f, te, nv: (i, 0)),
        ],
        out_specs=pl.BlockSpec((MOE_TILE, D_MODEL), lambda i, f, te, nv: (i, 0)),
        scratch_shapes=[pltpu.VMEM((MOE_TILE, D_MODEL), F32)],
    )
    return pl.pallas_call(
        _expert_kernel,
        grid_spec=grid_spec,
        out_shape=jax.ShapeDtypeStruct((n_tiles * MOE_TILE, D_MODEL), BF16),
        compiler_params=_params("arbitrary", "arbitrary"),
        name="moe_experts",
    )(route["tile_expert"], route["n_valid"], xs, w1, w3, w2, slot_w)


def _combine_kernel(small_ref, cnt_ref, winl_ref, offl_ref, wins_ref, offs_ref,
                    y_hbm, pos_ref, x_ref, mod_ref, fg_ref, o_ref, buf_ref, sem_ref, *, final_norm):
    c = pl.program_id(0)
    cur = c % 2
    variants = ((MOE_CMB_SMALL, wins_ref, offs_ref), (MOE_CMB_WIN, winl_ref, offl_ref))

    def window_copies(chunk, buf_set, rows, win_ref):
        return [pltpu.make_async_copy(
            y_hbm.at[pl.ds(pl.multiple_of(win_ref[chunk * N_EXPERTS + e], 16), rows), :],
            buf_ref.at[buf_set, pl.ds(e * rows, rows), :],
            sem_ref.at[buf_set, e]) for e in range(N_EXPERTS)]

    def start(chunk, buf_set):
        for use, (rows, win_ref, _) in zip((small_ref[chunk] > 0, small_ref[chunk] == 0), variants):
            @pl.when(use)
            def _():
                for cp in window_copies(chunk, buf_set, rows, win_ref):
                    cp.start()

    pl.when(c == 0)(lambda: start(0, 0))
    pl.when(c + 1 < pl.num_programs(0))(lambda: start(jnp.minimum(c + 1, pl.num_programs(0) - 1), 1 - cur))

    def process(rows, win_ref, off_ref):
        row = lax.broadcasted_iota(jnp.int32, (1, N_EXPERTS * rows), 1)
        slot = jnp.full((1, N_EXPERTS * rows), -1, jnp.int32)
        for e in range(N_EXPERTS):
            local = row - e * rows
            off = off_ref[c * N_EXPERTS + e]
            inside = (local >= off) & (local < off + cnt_ref[c * N_EXPERTS + e]) & (local < rows)
            slot = jnp.where(inside, win_ref[c * N_EXPERTS + e] + local, slot)
        pos = pos_ref[...]
        hit = (pos[:, 0:1] == slot) | (pos[:, 1:2] == slot)
        onehot = jnp.where(hit, 1.0, 0.0).astype(BF16)
        for cp in window_copies(c, cur, rows, win_ref):
            cp.wait()
        update = _dot(onehot, buf_ref[cur, 0:N_EXPERTS * rows, :])
        o_ref[...] = _residual_out(x_ref[...], mod_ref[0][5:6], update, final_norm, fg_ref[...])

    for use, variant in zip((small_ref[c] > 0, small_ref[c] == 0), variants):
        pl.when(use)(functools.partial(process, *variant))


def _combine_call(ys, route, x, mod_l, final_g, first_tile, final_norm):
    n_tok = x.shape[0]
    rows = lambda width: pl.BlockSpec((MOE_SRC, width), lambda c, *_: (c, 0))
    grid_spec = pltpu.PrefetchScalarGridSpec(
        num_scalar_prefetch=6,
        grid=(n_tok // MOE_SRC,),
        in_specs=[
            pl.BlockSpec(memory_space=pl.ANY),
            rows(TOP_K),
            rows(D_MODEL),
            pl.BlockSpec((1, ADA_CHUNKS, D_MODEL), lambda c, *_: (_tile_mod_row(c, first_tile, MOE_SRC), 0, 0)),
            pl.BlockSpec((1, D_MODEL), lambda c, *_: (0, 0)),
        ],
        out_specs=rows(D_MODEL),
        scratch_shapes=[pltpu.VMEM((2, N_EXPERTS * MOE_CMB_WIN, D_MODEL), BF16),
                        pltpu.SemaphoreType.DMA((2, N_EXPERTS))],
    )
    return pl.pallas_call(
        functools.partial(_combine_kernel, final_norm=final_norm),
        grid_spec=grid_spec,
        out_shape=jax.ShapeDtypeStruct((n_tok, D_MODEL), F32),
        compiler_params=_params("arbitrary"),
        name="moe_combine",
    )(route["cmb_small"], route["run_cnt"], route["win"], route["run_off"], route["win_s"], route["run_off_s"],
      ys, route["pos_cols"], x, mod_l, final_g)


def _rope_tables():
    t = np.arange(SEQ)

    def angles(rot_dim):
        half = rot_dim // 2
        inv = ROPE_THETA ** (-jnp.arange(0, half, 2, dtype=F32) / half)
        ang = jnp.concatenate([jnp.asarray(t // GRID_W, F32)[:, None] * inv[None, :],
                               jnp.asarray(t % GRID_W, F32)[:, None] * inv[None, :]], axis=-1)
        return jnp.cos(ang), jnp.sin(ang)

    def pad_rows(a, fill):
        return jnp.concatenate([jnp.full((ROW_TILE, a.shape[1]), fill, F32), a], axis=0)

    c, s = angles(HEAD_DIM)
    cos64 = jnp.tile(jnp.concatenate([c, c], axis=-1), (1, 4))
    sin64 = jnp.tile(jnp.concatenate([s, s], axis=-1), (1, 4))
    c, s = angles(MLA_ROPE)
    one = jnp.ones((SEQ, MLA_NOPE), F32)
    zero = jnp.zeros((SEQ, MLA_NOPE), F32)
    tail1 = jnp.ones((SEQ, MLA_PAD - MLA_NOPE - MLA_ROPE), F32)
    tail0 = jnp.zeros((SEQ, MLA_PAD - MLA_NOPE - MLA_ROPE), F32)
    cosm = jnp.tile(jnp.concatenate([one, c, c, tail1], axis=-1), (1, MLA_HEADS))
    sinm = jnp.tile(jnp.concatenate([zero, s, s, tail0], axis=-1), (1, MLA_HEADS))
    blockdiag = jnp.asarray(np.kron(np.eye(4), np.ones((HEAD_DIM, HEAD_DIM))), BF16)
    place = np.zeros((256, MLA_HEADS * MLA_PAD), np.float32)
    for h in range(MLA_HEADS):
        for r in range(MLA_ROPE):
            place[MLA_KV_RANK + r, h * MLA_PAD + MLA_NOPE + r] = 1.0
    return {
        "cos64": pad_rows(cos64, 1.0), "sin64": pad_rows(sin64, 0.0),
        "cosm": pad_rows(cosm, 1.0), "sinm": pad_rows(sinm, 0.0),
        "blockdiag": blockdiag, "pe_place": jnp.asarray(place, BF16),
    }


def _pack_w_in(w):
    n_qkv = COL_DKVA + MLA_KV_RANK + MLA_ROPE
    wb = w.astype(BF16)
    pad = jnp.zeros(w.shape[:2] + (COL_GATE - n_qkv,), BF16)
    return jnp.concatenate([wb[..., :n_qkv], pad, wb[..., n_qkv:] * 0.5], axis=-1)


def _pack_mla(wqb, wkvb):
    dqh = MLA_NOPE + MLA_ROPE
    q = wqb.reshape(MLA_Q_RANK, MLA_HEADS, dqh)
    q = jnp.pad(q, ((0, 0), (0, 0), (0, MLA_PAD - dqh))).reshape(MLA_Q_RANK, MLA_HEADS * MLA_PAD)
    kv = wkvb.reshape(MLA_KV_RANK, MLA_HEADS, MLA_NOPE + MLA_V)
    k = jnp.pad(kv[:, :, :MLA_NOPE], ((0, 0), (0, 0), (0, MLA_PAD - MLA_NOPE)))
    k = k.reshape(MLA_KV_RANK, MLA_HEADS * MLA_PAD)
    v = jnp.pad(kv[:, :, MLA_NOPE:], ((0, 0), (0, 0), (0, MLA_PAD - MLA_V))).reshape(MLA_KV_RANK, MLA_HEADS * MLA_PAD)
    return q.astype(BF16), k.astype(BF16), v.astype(BF16)


def _split2_host(w):
    hi = w.astype(BF16)
    return jnp.stack([hi, (w - hi.astype(F32)).astype(BF16)])


def kernel(x, c, ctx, c_ctx, norm1_g, norm2_g, w_ada, b_ada, w_in, na_rpb, gb_qnorm, gb_knorm, wc_sink,
           mla_qnorm, mla_kvnorm, mla_wqb, mla_wkvb, w_branch, w_out, ffn_w1, ffn_w3, ffn_w2,
           moe_router, moe_w1, moe_w3, moe_w2, final_g):
    n_batch = x.shape[0]
    assert x.shape[1:] == (SEQ, D_MODEL) and ctx.shape[1:] == (CTX_LEN, D_MODEL)
    assert n_batch * CTX_LEN <= CTX_ROWS and n_batch <= 8

    ctx_rows = ctx.reshape(n_batch * CTX_LEN, D_MODEL)
    if ctx_rows.shape[0] < CTX_ROWS:
        ctx_rows = jnp.pad(ctx_rows, ((0, CTX_ROWS - ctx_rows.shape[0]), (0, 0)))
    xt = jnp.concatenate([ctx_rows, x.reshape(n_batch * SEQ, D_MODEL)], axis=0)

    cvec = jnp.zeros((16, D_MODEL), F32).at[:n_batch].set(c).at[8].set(c_ctx)
    mod = _mod_call(cvec, w_ada, b_ada).reshape(DEPTH, 16, ADA_CHUNKS, D_MODEL)
    tabs = _rope_tables()
    w_in_p = _pack_w_in(w_in)
    moe_w = tuple(w.astype(BF16) for w in (moe_w1, moe_w3, moe_w2))
    mla_scale = (MLA_NOPE + MLA_ROPE) ** -0.5
    lat_tile0 = CTX_ROWS // ROW_TILE

    for l in range(DEPTH):
        with_ctx = l < DEPTH - 1
        wq, wk, wv = _pack_mla(mla_wqb[l], mla_wkvb[l])
        lw = {
            "gq": jnp.tile(gb_qnorm[l], 4)[None, :], "gk": jnp.tile(gb_knorm[l], 2)[None, :],
            "qn": mla_qnorm[l][None, :], "kvn": mla_kvnorm[l][None, :], "wqb": wq, "wk": wk, "wv": wv,
        }
        pa, cv, gate, qb, kb, qc, kc, qd, kd, vd, vb = _inproj_call(
            xt, mod[l], norm1_g[l][None, :], w_in_p, l, tabs, lw)

        oa = _natten_call(n_batch, with_ctx, pa, _natten_bias_table(na_rpb[l]))
        ob = _gqa_attn_t_call(n_batch, with_ctx, qb, kb, vb, n_kv=GB_KV_HEADS,
                              group=GB_Q_HEADS // GB_KV_HEADS, name="attn_global")
        oc = _window_attn_call(n_batch, with_ctx, qc, kc, cv, wc_sink[l])
        od = _dense_attn_call(n_batch, with_ctx, (qd, MLA_HEADS * MLA_PAD, 0), kd, (vd, MLA_HEADS * MLA_PAD, 0),
                              n_kv=MLA_HEADS, group=1, dq=MLA_PAD, scale=mla_scale, name="attn_mla")

        is_moe = l % 2 == 1
        router3 = None
        if is_moe:
            router3 = _split2_host(jnp.pad(moe_router[l // 2], ((0, 0), (0, LANES - N_EXPERTS))))
        first_tile = 0 if with_ctx else lat_tile0
        res = _merge_call((oa, ob, oc, od), gate, xt, mod[l], norm2_g[l][None, :],
                          (0.5 * w_branch[l]).astype(BF16), w_out[l].astype(BF16), router3, first_tile)
        x_mid, h2 = res[0], res[1]
        final_norm = l == DEPTH - 1
        if is_moe:
            route = _moe_route(res[2], h2.shape[0])
            xs, slot_w = _dispatch_call(h2, route)
            ys = _expert_call(xs, slot_w, route, *moe_w, l // 2)
            xt = _combine_call(ys, route, x_mid, mod[l], final_g[None, :],
                               0 if with_ctx else CTX_ROWS // MOE_SRC, final_norm)
        else:
            xt = _ffn_call(h2, ffn_w1, ffn_w3, ffn_w2, l // 2, x_mid, mod[l], final_g[None, :],
                           0 if with_ctx else CTX_ROWS // FFN_ROWS, final_norm)

    return xt.reshape(n_batch, SEQ, D_MODEL)
```

```python
import functools

import numpy as np
import jax
import jax.numpy as jnp
from jax import lax
from jax.experimental import pallas as pl
from jax.experimental.pallas import tpu as pltpu

F32 = jnp.float32
BF16 = jnp.bfloat16

D_MODEL = 1024
SEQ = 2048
DEPTH = 4
CTX_LEN = 256
GRID_W = 64
GRID_ROWS = SEQ // GRID_W
HEAD_DIM = 64
ROPE_THETA = 10000.0
NORM_EPS = 1e-6
NEG_INF = -1e30

NA_HEADS = 4
NA_KH = 8
NA_KW = 16
GB_Q_HEADS = 4
GB_KV_HEADS = 2
WC_Q_HEADS = 4
WC_KV_HEADS = 2
WC_WINDOW = 128
MLA_HEADS = 4
MLA_Q_RANK = 256
MLA_KV_RANK = 128
MLA_NOPE = 64
MLA_ROPE = 32
MLA_V = 64
MLA_PAD = 128
N_BRANCH = 4
BRANCH_W = 256
D_FF = 3584
N_EXPERTS = 8
ADA_CHUNKS = 6

VMEM_LIMIT_BYTES = 56 * 1024 * 1024
LANES = 128

CTX_ROWS = 2048
ROW_TILE = 512
Q_TILE = 256
NA_Q_ROWS = Q_TILE // GRID_W
NA_WIN_ROWS = 12
NA_WIN = NA_WIN_ROWS * GRID_W
WC_WIN = Q_TILE + 2 * WC_WINDOW
ATTN_KEY_CHUNK = 512

COL_AQ, COL_AK, COL_AV = 0, 256, 512
COL_BQ, COL_BK, COL_BV = 768, 1024, 1152
COL_CQ, COL_CK, COL_CV = 1280, 1536, 1664
COL_DQA, COL_DKVA, COL_GATE = 1792, 2048, 2304
GATE_W = N_BRANCH * D_MODEL
P_WIDTH = COL_GATE + GATE_W
QKV_CHUNK = COL_GATE // 2
FF_CHUNK = 512
MOE_FF_CHUNK = 1792


def _params(*sem):
    return pltpu.CompilerParams(dimension_semantics=sem, vmem_limit_bytes=VMEM_LIMIT_BYTES)


def _dot(a, b):
    return jnp.dot(a, b, preferred_element_type=F32)


def _dot_nt(a, b):
    return lax.dot_general(a, b, (((1,), (1,)), ((), ())), preferred_element_type=F32)


def _split3(x):
    hi = x.astype(BF16)
    r1 = x - hi.astype(F32)
    mid = r1.astype(BF16)
    lo = (r1 - mid.astype(F32)).astype(BF16)
    return hi, mid, lo


def _sigmoid(x):
    return 0.5 * jnp.tanh(0.5 * x) + 0.5


ADA_TN = 1536


def _mod_kernel(c_ref, w_ref, b_ref, o_ref):
    c = c_ref[...]
    sc = (c * _sigmoid(c)).astype(BF16)
    o_ref[0] = _dot(sc, w_ref[0].astype(BF16)) + b_ref[0]


def _mod_call(cvec, w_ada, b_ada):
    n = ADA_CHUNKS * D_MODEL
    return pl.pallas_call(
        _mod_kernel,
        grid=(DEPTH, n // ADA_TN),
        in_specs=[
            pl.BlockSpec((16, D_MODEL), lambda l, j: (0, 0)),
            pl.BlockSpec((1, D_MODEL, ADA_TN), lambda l, j: (l, 0, j)),
            pl.BlockSpec((1, 1, ADA_TN), lambda l, j: (l, 0, j)),
        ],
        out_specs=pl.BlockSpec((1, 16, ADA_TN), lambda l, j: (l, 0, j)),
        out_shape=jax.ShapeDtypeStruct((DEPTH, 16, n), F32),
        compiler_params=_params("arbitrary", "arbitrary"),
        name="adaln_mod",
    )(cvec, w_ada, b_ada.reshape(DEPTH, 1, n))


def _mod_row(i, first_tile):
    n_ctx = CTX_ROWS // ROW_TILE
    t = i + first_tile
    return jnp.where(t < n_ctx, 8, (t - n_ctx) // (SEQ // ROW_TILE))


def _norm_mod(x, g, shift, scale):
    ms = jnp.mean(x * x, axis=-1, keepdims=True)
    y = x * lax.rsqrt(ms + NORM_EPS) * g
    return y * (1.0 + scale) + shift


def _rope(x, cos, sin, half, first_mask):
    w = x.shape[-1]
    fwd = pltpu.roll(x, w - half, 1)
    bwd = pltpu.roll(x, half, 1)
    rot = jnp.where(first_mask, -fwd, bwd)
    return x * cos + rot * sin


def _head_rms(x, gain, blockdiag):
    hi, mid, lo = _split3(x * x)
    ss = _dot(hi, blockdiag) + _dot(mid, blockdiag) + _dot(lo, blockdiag)
    return x * lax.rsqrt(ss * (1.0 / HEAD_DIM) + NORM_EPS) * gain


def _inproj_kernel(x_ref, mod_ref, g_ref, w_ref,
                   cos64_ref, sin64_ref, cosm_ref, sinm_ref,
                   gq_ref, gk_ref, bd_ref, qn_ref, kvn_ref, wqb_ref, wk_ref, wv_ref, pe_ref,
                   pa_ref, cv_ref, gate_ref, qb_ref, kb_ref, qc_ref, kc_ref, qd_ref, kd_ref, vd_ref, vb_ref):
    m = mod_ref[0]
    hb = _norm_mod(x_ref[...], g_ref[...], m[0:1], m[1:2]).astype(BF16)

    def proj(c0, c1):
        return _dot(hb, w_ref[0, :, c0:c1])

    for c0 in range(0, GATE_W, D_MODEL):
        gate_ref[:, c0:c0 + D_MODEL] = proj(COL_GATE + c0, COL_GATE + c0 + D_MODEL).astype(BF16)
    pa_ref[...] = proj(COL_AQ, COL_BQ).astype(BF16)

    scale = HEAD_DIM ** -0.5
    half = HEAD_DIM // 2
    cos64, sin64 = cos64_ref[...], sin64_ref[...]
    lane = lax.broadcasted_iota(jnp.int32, (1, 256), 1)
    first64 = (lane % HEAD_DIM) < half
    bd = bd_ref[...]
    bc = proj(COL_BQ, COL_DQA)
    bq = _head_rms(bc[:, 0:256], gq_ref[...], bd)
    qb_ref[...] = (_rope(bq, cos64, sin64, half, first64) * scale).T.astype(BF16)
    bk = _head_rms(bc[:, 256:384], gk_ref[...], bd[:128, :128])
    kb_ref[...] = _rope(bk, cos64[:, :128], sin64[:, :128], half, first64[:, :128]).astype(BF16)
    qc_ref[...] = (_rope(bc[:, 512:768], cos64, sin64, half, first64) * scale).T.astype(BF16)
    kc_ref[...] = _rope(bc[:, 768:896], cos64[:, :128], sin64[:, :128], half, first64[:, :128]).astype(BF16)
    cv = bc[:, 896:1024]
    ones_c = (lane[:, :LANES] % LANES) >= HEAD_DIM
    cv_ref[:LANES, :] = jnp.where(ones_c, 1.0, cv).T.astype(BF16)
    cv_ref[LANES:, :] = jnp.where(ones_c, 1.0, pltpu.roll(cv, HEAD_DIM, 1)).T.astype(BF16)

    cosm, sinm = cosm_ref[...], sinm_ref[...]
    lane_m = lax.broadcasted_iota(jnp.int32, (1, MLA_HEADS * MLA_PAD), 1) % MLA_PAD
    first_m = lane_m < (MLA_NOPE + MLA_ROPE // 2)
    d = proj(COL_DQA, COL_GATE)
    dqa = d[:, 0:MLA_Q_RANK]
    qn = dqa * lax.rsqrt(jnp.mean(dqa * dqa, axis=-1, keepdims=True) + NORM_EPS) * qn_ref[...]
    dq = _dot(qn.astype(BF16), wqb_ref[...])
    qd_ref[...] = _rope(dq, cosm, sinm, MLA_ROPE // 2, first_m).astype(BF16)

    dkva = d[:, MLA_Q_RANK:]
    kvc = dkva[:, :MLA_KV_RANK]
    kvn = (kvc * lax.rsqrt(jnp.mean(kvc * kvc, axis=-1, keepdims=True) + NORM_EPS) * kvn_ref[...]).astype(BF16)
    dk = _dot(kvn, wk_ref[...]) + _dot(dkva.astype(BF16), pe_ref[...])
    kd_ref[...] = _rope(dk, cosm, sinm, MLA_ROPE // 2, first_m).T.astype(BF16)
    ones_half = lane_m >= HEAD_DIM
    vd_ref[...] = jnp.where(ones_half, 1.0, _dot(kvn, wv_ref[...])).astype(BF16)
    bv = bc[:, 384:512]
    vb_ref[:LANES, :] = jnp.where(ones_half[:, :LANES], 1.0, bv).T.astype(BF16)
    vb_ref[LANES:, :] = jnp.where(ones_half[:, :LANES], 1.0, pltpu.roll(bv, HEAD_DIM, 1)).T.astype(BF16)


def _rope_row_block(i):
    n_ctx = CTX_ROWS // ROW_TILE
    return jnp.where(i < n_ctx, 0, 1 + (i - n_ctx) % (SEQ // ROW_TILE))


def _inproj_call(x, mod_l, g, w_in_p, layer, tabs, lw):
    t = x.shape[0]
    mw = MLA_HEADS * MLA_PAD

    def full(shape):
        return pl.BlockSpec(shape, lambda i: (0,) * len(shape))

    def rows(width):
        return pl.BlockSpec((ROW_TILE, width), lambda i: (i, 0))

    def tab(width):
        return pl.BlockSpec((ROW_TILE, width), lambda i: (_rope_row_block(i), 0))

    def cols(height):
        return pl.BlockSpec((height, ROW_TILE), lambda i: (0, i))

    outs = [(COL_BQ, False), (WC_KV_HEADS * LANES, True), (GATE_W, False),
            (256, True), (128, False), (256, True), (128, False), (mw, False), (mw, True), (mw, False),
            (GB_KV_HEADS * LANES, True)]
    return pl.pallas_call(
        _inproj_kernel,
        grid=(t // ROW_TILE,),
        in_specs=[
            rows(D_MODEL),
            pl.BlockSpec((1, ADA_CHUNKS, D_MODEL), lambda i: (_mod_row(i, 0), 0, 0)),
            full((1, D_MODEL)),
            pl.BlockSpec((1, D_MODEL, P_WIDTH), lambda i: (layer, 0, 0), pipeline_mode=pl.Buffered(1)),
            tab(256), tab(256), tab(mw), tab(mw),
            full((1, 256)), full((1, 128)), full((256, 256)), full((1, MLA_Q_RANK)), full((1, MLA_KV_RANK)),
            full((MLA_Q_RANK, mw)), full((MLA_KV_RANK, mw)), full((MLA_KV_RANK, mw)), full((256, mw)),
        ],
        out_specs=[cols(w) if tr else rows(w) for w, tr in outs],
        out_shape=[jax.ShapeDtypeStruct((w, t) if tr else (t, w), BF16) for w, tr in outs],
        compiler_params=_params("arbitrary"),
        name="inproj",
    )(x, mod_l, g, w_in_p, tabs["cos64"], tabs["sin64"], tabs["cosm"], tabs["sinm"],
      lw["gq"], lw["gk"], tabs["blockdiag"], lw["qn"], lw["kvn"], lw["wqb"], lw["wk"], lw["wv"], tabs["pe_place"])


def _attend(q, segs, sink=None, scale=None):
    chunks = segs
    scores = []
    m = None
    for k, _, bias in chunks:
        s = _dot_nt(q, k)
        if scale is not None:
            s = s * scale
        if bias is not None:
            s = s + bias
        scores.append(s)
        ms = jnp.max(s, axis=-1, keepdims=True)
        m = ms if m is None else jnp.maximum(m, ms)
    if sink is not None:
        m = jnp.maximum(m, sink)
    denom = None
    out = None
    for s, (_, v, _) in zip(scores, chunks):
        p = jnp.exp(s - m)
        ps = jnp.sum(p, axis=-1, keepdims=True)
        pv = _dot(p.astype(BF16), v)
        denom = ps if denom is None else denom + ps
        out = pv if out is None else out + pv
    if sink is not None:
        denom = denom + jnp.exp(sink - m)
    return out / denom


def _gqa_heads(q_ref, o_ref, sink_ref, n_kv, group, dq, dv, scale, seg_fn):
    tq = q_ref.shape[0]
    for h in range(n_kv):
        heads = [h * group + g for g in range(group)]
        q = jnp.concatenate([q_ref[:, a * dq:(a + 1) * dq] for a in heads], axis=0) if group > 1 \
            else q_ref[:, h * dq:(h + 1) * dq]
        sink = None
        if sink_ref is not None:
            sink = jnp.concatenate([jnp.full((tq, 1), sink_ref[a], F32) for a in heads], axis=0)
        o = _attend(q, seg_fn(h), sink=sink, scale=scale)
        for g, a in enumerate(heads):
            o_ref[:, a * dv:(a + 1) * dv] = o[g * tq:(g + 1) * tq].astype(BF16)


def _dense_attn_kernel(q_ref, ktl_ref, vl_ref, ktc_ref, vc_ref, o_ref, *, n_kv, group, dq, scale, ctx_tile):
    tq = q_ref.shape[0]
    hd = HEAD_DIM

    def run(with_latent):
        for h in range(n_kv):
            heads = [h * group + g for g in range(group)]
            q = jnp.concatenate([q_ref[:, a * dq:(a + 1) * dq] for a in heads], axis=0) if group > 1 \
                else q_ref[:, h * dq:(h + 1) * dq]
            segs = [(ktc_ref[h * dq:(h + 1) * dq, :], vc_ref[:, h * LANES:(h + 1) * LANES])]
            if with_latent:
                segs = [(ktl_ref[h * dq:(h + 1) * dq, c0:c0 + ATTN_KEY_CHUNK],
                         vl_ref[c0:c0 + ATTN_KEY_CHUNK, h * LANES:(h + 1) * LANES])
                        for c0 in range(0, SEQ, ATTN_KEY_CHUNK)] + segs
            scores = []
            m = None
            for kt, _ in segs:
                s = _dot(q, kt)
                if scale is not None:
                    s = s * scale
                scores.append(s)
                ms = jnp.max(s, axis=-1, keepdims=True)
                m = ms if m is None else jnp.maximum(m, ms)
            acc = None
            for s, (_, v) in zip(scores, segs):
                pv = _dot(jnp.exp(s - m).astype(BF16), v)
                acc = pv if acc is None else acc + pv
            o = acc * pltpu.roll(1.0 / acc, hd, 1)
            for g, a in enumerate(heads):
                o_ref[:, a * hd:(a + 1) * hd] = o[g * tq:(g + 1) * tq, :hd].astype(BF16)

    if ctx_tile:
        j = pl.program_id(1)
        pl.when(j == 0)(lambda: run(False))
        pl.when(j > 0)(lambda: run(True))
    else:
        run(True)


def _gqa_attn_t_kernel(qt_ref, kl_ref, vtl_ref, kc_ref, vtc_ref, o_ref, *, n_kv, group, ctx_tile):
    tq = qt_ref.shape[1]
    hd = HEAD_DIM
    kw = kl_ref.shape[1]
    n = group * tq

    def run(with_latent):
        for h in range(n_kv):
            heads = [h * group + g for g in range(group)]
            qcat = jnp.concatenate([qt_ref[a * hd:(a + 1) * hd, :] for a in heads], axis=1)
            parts = [jnp.zeros((h * hd, n), BF16), qcat, jnp.zeros((kw - (h + 1) * hd, n), BF16)]
            wq = jnp.concatenate([p for p in parts if p.shape[0] > 0], axis=0)
            vrows = slice(h * LANES, (h + 1) * LANES)
            segs = [(kc_ref[...], vtc_ref[vrows, :])]
            if with_latent:
                segs = [(kl_ref[c0:c0 + ATTN_KEY_CHUNK, :], vtl_ref[vrows, c0:c0 + ATTN_KEY_CHUNK])
                        for c0 in range(0, SEQ, ATTN_KEY_CHUNK)] + segs
            scores = []
            m = None
            for k, _ in segs:
                s = _dot(k, wq)
                scores.append(s)
                ms = jnp.max(s, axis=0, keepdims=True)
                m = ms if m is None else jnp.maximum(m, ms)
            acc = None
            for s, (_, vt) in zip(scores, segs):
                pv = _dot(vt, jnp.exp(s - m).astype(BF16))
                acc = pv if acc is None else acc + pv
            o = acc[:hd] * (1.0 / acc[hd:hd + 1])
            for g, a in enumerate(heads):
                o_ref[:, a * hd:(a + 1) * hd] = o[:, g * tq:(g + 1) * tq].T.astype(BF16)

    if ctx_tile:
        j = pl.program_id(1)
        pl.when(j == 0)(lambda: run(False))
        pl.when(j > 0)(lambda: run(True))
    else:
        run(True)


def _q_row_block(b, j, n_batch, with_ctx):
    per_batch = SEQ // Q_TILE
    lat0 = CTX_ROWS // Q_TILE
    if with_ctx:
        return jnp.where(j == 0, b, lat0 + b * per_batch + j - 1)
    return lat0 + b * per_batch + j


def _attn_specs(n_batch, with_ctx, q, k, v, o_width, order_bj=True):
    def ix(f):
        return (lambda b, j: f(b, j)) if order_bj else (lambda j, b: f(b, j))

    def qspec(width, col):
        return pl.BlockSpec((Q_TILE, width), ix(lambda b, j: (_q_row_block(b, j, n_batch, with_ctx), col // width)))

    def lat(width, col):
        return pl.BlockSpec((SEQ, width), ix(lambda b, j: (CTX_ROWS // SEQ + b, col // width)))

    def ctx(width, col):
        return pl.BlockSpec((CTX_LEN, width), ix(lambda b, j: (b, col // width)))

    in_specs = [qspec(q[1], q[2]), lat(k[1], k[2]), lat(v[1], v[2]), ctx(k[1], k[2]), ctx(v[1], v[2])]
    args = [q[0], k[0], v[0], k[0], v[0]]
    if with_ctx:
        return in_specs, args, qspec(o_width, 0)
    out_spec = pl.BlockSpec((Q_TILE, o_width), ix(lambda b, j: (b * (SEQ // Q_TILE) + j, 0)))
    return in_specs, args, out_spec


def _attn_out_rows(t, with_ctx):
    return t if with_ctx else t - CTX_ROWS


def _dense_attn_call(n_batch, with_ctx, q, kt, v, *, n_kv, group, dq, scale=None, name):
    t = q[0].shape[0]
    in_specs, args, out_spec = _attn_specs(n_batch, with_ctx, q, v, v, 256)
    kt_rows = kt.shape[0]
    in_specs[1] = pl.BlockSpec((kt_rows, SEQ), lambda b, j: (0, CTX_ROWS // SEQ + b))
    in_specs[3] = pl.BlockSpec((kt_rows, CTX_LEN), lambda b, j: (0, b))
    args[1] = args[3] = kt
    kern = functools.partial(_dense_attn_kernel, n_kv=n_kv, group=group, dq=dq, scale=scale, ctx_tile=with_ctx)
    return pl.pallas_call(
        kern,
        grid=(n_batch, SEQ // Q_TILE + (1 if with_ctx else 0)),
        in_specs=in_specs,
        out_specs=out_spec,
        out_shape=jax.ShapeDtypeStruct((_attn_out_rows(t, with_ctx), 256), BF16),
        compiler_params=_params("arbitrary", "arbitrary"),
        name=name,
    )(*args)


def _gqa_attn_t_call(n_batch, with_ctx, qt, k, vt, *, n_kv, group, name):
    t = k.shape[0]
    kspec = (k, k.shape[1], 0)
    in_specs, args, out_spec = _attn_specs(n_batch, with_ctx, kspec, kspec, kspec, 256)
    in_specs[0] = pl.BlockSpec((qt.shape[0], Q_TILE), lambda b, j: (0, _q_row_block(b, j, n_batch, with_ctx)))
    in_specs[2] = pl.BlockSpec((vt.shape[0], SEQ), lambda b, j: (0, CTX_ROWS // SEQ + b))
    in_specs[4] = pl.BlockSpec((vt.shape[0], CTX_LEN), lambda b, j: (0, b))
    args[0] = qt
    args[2] = args[4] = vt
    return pl.pallas_call(
        functools.partial(_gqa_attn_t_kernel, n_kv=n_kv, group=group, ctx_tile=with_ctx),
        grid=(n_batch, SEQ // Q_TILE + (1 if with_ctx else 0)),
        in_specs=in_specs,
        out_specs=out_spec,
        out_shape=jax.ShapeDtypeStruct((_attn_out_rows(t, with_ctx), 256), BF16),
        compiler_params=_params("arbitrary", "arbitrary"),
        name=name,
    )(*args)


def _window_attn_kernel(sink_ref, qt_ref, kl_ref, vtl_ref, kc_ref, vtc_ref, o_ref, *, ctx_tile):
    j = pl.program_id(1)
    hd = HEAD_DIM
    group = WC_Q_HEADS // WC_KV_HEADS
    tq = qt_ref.shape[1]
    kw = kl_ref.shape[1]
    n = group * tq

    def run(with_latent):
        if with_latent:
            i = j - 1 if ctx_tile else j
            start = pl.multiple_of(jnp.clip(i * Q_TILE - WC_WINDOW, 0, SEQ - WC_WIN), WC_WINDOW)
            kpos = start + lax.broadcasted_iota(jnp.int32, (WC_WIN, n), 0)
            qpos = i * Q_TILE + lax.broadcasted_iota(jnp.int32, (WC_WIN, n), 1) % Q_TILE
            bias = jnp.where(jnp.abs(qpos - kpos) <= WC_WINDOW, 0.0, NEG_INF).astype(F32)
        lane = lax.broadcasted_iota(jnp.int32, (1, n), 1)
        for h in range(WC_KV_HEADS):
            heads = [h * group + g for g in range(group)]
            qcat = jnp.concatenate([qt_ref[a * hd:(a + 1) * hd, :] for a in heads], axis=1)
            parts = [jnp.zeros((h * hd, n), BF16), qcat, jnp.zeros((kw - (h + 1) * hd, n), BF16)]
            wq = jnp.concatenate([p for p in parts if p.shape[0] > 0], axis=0)
            vrows = slice(h * LANES, (h + 1) * LANES)
            sink = functools.reduce(lambda acc, ga: jnp.where(lane >= ga[0] * tq, sink_ref[ga[1]], acc),
                                    list(enumerate(heads)), jnp.zeros((1, n), F32))
            s_ctx = _dot(kc_ref[...], wq)
            m = jnp.maximum(jnp.max(s_ctx, axis=0, keepdims=True), sink)
            if with_latent:
                s_lat = _dot(kl_ref[pl.ds(start, WC_WIN), :], wq) + bias
                m = jnp.maximum(m, jnp.max(s_lat, axis=0, keepdims=True))
            acc = _dot(vtc_ref[vrows, :], jnp.exp(s_ctx - m).astype(BF16))
            if with_latent:
                acc = acc + _dot(vtl_ref[vrows, pl.ds(start, WC_WIN)], jnp.exp(s_lat - m).astype(BF16))
            o = acc[:hd] * (1.0 / (acc[hd:hd + 1] + jnp.exp(sink - m)))
            for g, a in enumerate(heads):
                o_ref[:, a * hd:(a + 1) * hd] = o[:, g * tq:(g + 1) * tq].T.astype(BF16)

    if ctx_tile:
        pl.when(j == 0)(lambda: run(False))
        pl.when(j > 0)(lambda: run(True))
    else:
        run(True)


def _window_attn_call(n_batch, with_ctx, qt, k, vt, sink):
    t = k.shape[0]
    kspec = (k, k.shape[1], 0)
    in_specs, args, out_spec = _attn_specs(n_batch, with_ctx, kspec, kspec, kspec, 256)
    in_specs[0] = pl.BlockSpec((qt.shape[0], Q_TILE), lambda b, j: (0, _q_row_block(b, j, n_batch, with_ctx)))
    in_specs[2] = pl.BlockSpec((vt.shape[0], SEQ), lambda b, j: (0, CTX_ROWS // SEQ + b))
    in_specs[4] = pl.BlockSpec((vt.shape[0], CTX_LEN), lambda b, j: (0, b))
    args[0] = qt
    args[2] = args[4] = vt
    return pl.pallas_call(
        functools.partial(_window_attn_kernel, ctx_tile=with_ctx),
        grid=(n_batch, SEQ // Q_TILE + (1 if with_ctx else 0)),
        in_specs=[pl.BlockSpec(memory_space=pltpu.SMEM)] + in_specs,
        out_specs=out_spec,
        out_shape=jax.ShapeDtypeStruct((_attn_out_rows(t, with_ctx), 256), BF16),
        compiler_params=_params("arbitrary", "arbitrary"),
        name="attn_window",
    )(sink, *args)


def _natten_kernel(q_ref, kl_ref, vl_ref, kc_ref, vc_ref, tab_ref, o_ref, bias_ref, *, ctx_tile):
    j = pl.program_id(0)
    hd = HEAD_DIM
    scale = HEAD_DIM ** -0.5

    def build_bias(i):
        ws = jnp.clip(NA_Q_ROWS * i - NA_KH // 2, 0, GRID_ROWS - NA_WIN_ROWS)
        for a in range(NA_Q_ROWS):
            r = NA_Q_ROWS * i + a
            krow0 = jnp.clip(r - NA_KH // 2, 0, GRID_ROWS - NA_KH)
            for pair in range(NA_WIN_ROWS // 2):
                idx = []
                for side in range(2):
                    kr = ws + 2 * pair + side
                    in_rows = (kr >= krow0) & (kr < krow0 + NA_KH)
                    idx.append(jnp.where(in_rows, kr - r + NA_KH - 1, NA_NO_ROW))
                for h in range(NA_HEADS):
                    bias_ref[h, a * GRID_W:(a + 1) * GRID_W, pair * LANES:(pair + 1) * LANES] = (
                        tab_ref[0, h, idx[0]] + tab_ref[1, h, idx[1]])

    def ctx_run():
        for h in range(NA_HEADS):
            sl = slice(h * hd, (h + 1) * hd)
            o = _attend(q_ref[:, sl] * scale, [(kc_ref[:, sl], vc_ref[:, sl], None)])
            o_ref[:, sl] = o.astype(BF16)

    def lat_run():
        i = j - 1 if ctx_tile else j
        ws = jnp.clip(NA_Q_ROWS * i - NA_KH // 2, 0, GRID_ROWS - NA_WIN_ROWS)
        start = pl.multiple_of(ws * GRID_W, GRID_W)
        pl.when(pl.program_id(1) == 0)(lambda: build_bias(i))
        q = q_ref[...] * scale
        k_win = kl_ref[pl.ds(start, NA_WIN), :]
        v_win = vl_ref[pl.ds(start, NA_WIN), :]
        k_ctx, v_ctx = kc_ref[...], vc_ref[...]
        lane = lax.broadcasted_iota(jnp.int32, (1, NA_HEADS * hd), 1)
        out = None
        for h in range(NA_HEADS):
            own = (lane >= h * hd) & (lane < (h + 1) * hd)
            qh = jnp.where(own, q, 0.0).astype(BF16)
            o = _attend(qh, [(k_win, v_win, bias_ref[h]), (k_ctx, v_ctx, None)])
            o = jnp.where(own, o, 0.0)
            out = o if out is None else out + o
        o_ref[...] = out.astype(BF16)

    if ctx_tile:
        pl.when(j == 0)(ctx_run)
        pl.when(j > 0)(lat_run)
    else:
        lat_run()


NA_NO_ROW = 2 * NA_KH - 1


def _natten_call(n_batch, with_ctx, p, bias_blocks):
    t = p.shape[0]
    q, k, v = (p, 256, COL_AQ), (p, 256, COL_AK), (p, 256, COL_AV)
    in_specs, args, out_spec = _attn_specs(n_batch, with_ctx, q, k, v, 256, order_bj=False)
    in_specs.append(pl.BlockSpec(bias_blocks.shape, lambda j, b: (0,) * bias_blocks.ndim))
    return pl.pallas_call(
        functools.partial(_natten_kernel, ctx_tile=with_ctx),
        grid=(SEQ // Q_TILE + (1 if with_ctx else 0), n_batch),
        in_specs=in_specs,
        out_specs=out_spec,
        out_shape=jax.ShapeDtypeStruct((_attn_out_rows(t, with_ctx), 256), BF16),
        scratch_shapes=[pltpu.VMEM((NA_HEADS, Q_TILE, NA_WIN), F32)],
        compiler_params=_params("arbitrary", "arbitrary"),
        name="attn_natten",
    )(*args, bias_blocks)


def _natten_bias_table(rpb):
    n_dr, n_dc = 2 * NA_KH - 1, 2 * NA_KW - 1
    col = np.arange(GRID_W)
    dc = np.clip(col[None, :] - col[:, None] + NA_KW - 1, 0, n_dc - 1)
    onehot = jnp.asarray(dc[None] == np.arange(n_dc)[:, None, None], F32)
    blocks = jnp.einsum('hdc,cqk->hdqk', rpb.astype(F32), onehot, precision=lax.Precision.HIGHEST)
    cstart = np.clip(col - NA_KW // 2, 0, GRID_W - NA_KW)
    col_ok = (col[None, :] >= cstart[:, None]) & (col[None, :] < cstart[:, None] + NA_KW)
    blocks = jnp.where(col_ok[None, None], blocks, NEG_INF)
    blocks = jnp.concatenate([blocks, jnp.full((NA_HEADS, 1, GRID_W, GRID_W), NEG_INF, F32)], axis=1)
    zero = jnp.zeros_like(blocks)
    return jnp.stack([jnp.concatenate([blocks, zero], axis=-1), jnp.concatenate([zero, blocks], axis=-1)])


def _merge_kernel(*refs, with_router):
    if with_router:
        (oa_ref, ob_ref, oc_ref, od_ref, gate_ref, x_ref, mod_ref, g2_ref, wb_ref, wo_ref, rt_ref,
         xo_ref, h2_ref, comb_ref) = refs
    else:
        (oa_ref, ob_ref, oc_ref, od_ref, gate_ref, x_ref, mod_ref, g2_ref, wb_ref, wo_ref,
         xo_ref, h2_ref) = refs
    m = mod_ref[0]
    acc = None
    for n, o_ref in enumerate((oa_ref, ob_ref, oc_ref, od_ref)):
        half_y = _dot(o_ref[...], wb_ref[n])
        t = jnp.tanh(gate_ref[:, n * D_MODEL:(n + 1) * D_MODEL].astype(F32)) + 1.0
        acc = t * half_y if acc is None else acc + t * half_y
    mix = _dot(acc.astype(BF16), wo_ref[...])
    x = x_ref[...] + m[2:3] * mix
    xo_ref[...] = x
    h2 = _norm_mod(x, g2_ref[...], m[3:4], m[4:5])
    h2_ref[...] = h2.astype(BF16)
    if with_router:
        hh, hm, _ = _split3(h2)
        rh, rm = rt_ref[0], rt_ref[1]
        logits = _dot(hh, rh) + (_dot(hh, rm) + _dot(hm, rh))
        lane = lax.broadcasted_iota(jnp.int32, logits.shape, 1).astype(F32)
        logits = jnp.where(lane < N_EXPERTS, logits, NEG_INF)
        m1 = jnp.max(logits, axis=-1, keepdims=True)
        i1 = jnp.min(jnp.where(logits == m1, lane, float(LANES)), axis=-1, keepdims=True)
        rest = jnp.where(lane == i1, NEG_INF, logits)
        m2 = jnp.max(rest, axis=-1, keepdims=True)
        i2 = jnp.min(jnp.where(rest == m2, lane, float(LANES)), axis=-1, keepdims=True)
        e = jnp.exp(m2 - m1)
        w1 = 1.0 / (1.0 + e)
        w2 = e / (1.0 + e)
        comb_ref[...] = (jnp.where(lane == 0.0, i1, 0.0) + jnp.where(lane == 1.0, i2, 0.0)
                         + jnp.where(lane == 2.0, w1, 0.0) + jnp.where(lane == 3.0, w2, 0.0))


def _merge_call(outs, gate, x, mod_l, g2, wb, wo, router3, first_tile):
    t = x.shape[0]
    n_rows = t - first_tile * ROW_TILE

    def rows(width, col=0):
        return pl.BlockSpec((ROW_TILE, width), lambda i: (i + first_tile, col // width))

    def orow(width):
        return pl.BlockSpec((ROW_TILE, width), lambda i: (i, 0))

    def full(shape):
        return pl.BlockSpec(shape, lambda i: (0,) * len(shape))

    in_specs = [orow(256)] * 4 + [
        rows(GATE_W),
        rows(D_MODEL),
        pl.BlockSpec((1, ADA_CHUNKS, D_MODEL), lambda i: (_mod_row(i, first_tile), 0, 0)),
        full((1, D_MODEL)), full((N_BRANCH, BRANCH_W, D_MODEL)), full((D_MODEL, D_MODEL)),
    ]
    args = list(outs) + [gate, x, mod_l, g2, wb, wo]
    out_specs = [orow(D_MODEL), orow(D_MODEL)]
    out_shape = [jax.ShapeDtypeStruct((n_rows, D_MODEL), F32), jax.ShapeDtypeStruct((n_rows, D_MODEL), BF16)]
    if router3 is not None:
        in_specs.append(full((2, D_MODEL, LANES)))
        args.append(router3)
        out_specs.append(orow(LANES))
        out_shape.append(jax.ShapeDtypeStruct((n_rows, LANES), F32))
    return pl.pallas_call(
        functools.partial(_merge_kernel, with_router=router3 is not None),
        grid=(n_rows // ROW_TILE,),
        in_specs=in_specs,
        out_specs=out_specs,
        out_shape=out_shape,
        compiler_params=_params("arbitrary"),
        name="merge",
    )(*args)


FFN_ROWS = 1024


def _swiglu_chunk(h, w1, w3, w2):
    half_a = 0.5 * _dot(h, w1)
    b = _dot(h, w3)
    return _dot((half_a * (jnp.tanh(half_a) + 1.0) * b).astype(BF16), w2)


def _residual_out(x, gate, update, final_norm, final_g):
    x = x + gate * update
    if final_norm:
        ms = jnp.mean(x * x, axis=-1, keepdims=True)
        x = x * lax.rsqrt(ms + NORM_EPS) * final_g
    return x


def _ffn_kernel(h_ref, w1_ref, w3_ref, w2_ref, x_ref, mod_ref, fg_ref, o_ref, acc_ref, *, final_norm):
    f = pl.program_id(1)

    @pl.when(f == 0)
    def _():
        acc_ref[...] = jnp.zeros_like(acc_ref)

    acc_ref[...] += _swiglu_chunk(h_ref[...], w1_ref[0].astype(BF16), w3_ref[0].astype(BF16),
                                  w2_ref[0].astype(BF16))

    @pl.when(f == pl.num_programs(1) - 1)
    def _():
        o_ref[...] = _residual_out(x_ref[...], mod_ref[0][5:6], acc_ref[...], final_norm, fg_ref[...])


def _tile_mod_row(i, first_tile, tile_rows):
    n_ctx = CTX_ROWS // tile_rows
    t = i + first_tile
    return jnp.where(t < n_ctx, 8, (t - n_ctx) // (SEQ // tile_rows))


def _ffn_call(h2, w1, w3, w2, layer, x, mod_l, final_g, first_row_tile, final_norm):
    t = x.shape[0]
    rows = lambda width: pl.BlockSpec((FFN_ROWS, width), lambda i, f: (i, 0))
    return pl.pallas_call(
        functools.partial(_ffn_kernel, final_norm=final_norm),
        grid=(t // FFN_ROWS, D_FF // FF_CHUNK),
        in_specs=[
            rows(D_MODEL),
            pl.BlockSpec((1, D_MODEL, FF_CHUNK), lambda i, f: (layer, 0, f)),
            pl.BlockSpec((1, D_MODEL, FF_CHUNK), lambda i, f: (layer, 0, f)),
            pl.BlockSpec((1, FF_CHUNK, D_MODEL), lambda i, f: (layer, f, 0)),
            rows(D_MODEL),
            pl.BlockSpec((1, ADA_CHUNKS, D_MODEL),
                         lambda i, f: (_tile_mod_row(i, first_row_tile, FFN_ROWS), 0, 0)),
            pl.BlockSpec((1, D_MODEL), lambda i, f: (0, 0)),
        ],
        out_specs=rows(D_MODEL),
        out_shape=jax.ShapeDtypeStruct((t, D_MODEL), F32),
        scratch_shapes=[pltpu.VMEM((FFN_ROWS, D_MODEL), F32)],
        compiler_params=_params("arbitrary", "arbitrary"),
        name="ffn",
    )(h2, w1, w3, w2, x, mod_l, final_g)


MOE_TILE = 512
MOE_SRC = 256
MOE_CMB_WIN = MOE_SRC + 16
MOE_CMB_SMALL = 144
MOE_DSP_SMALL = 144
MOE_DSP_LARGE = MOE_SRC + 16
TOP_K = 2


def _moe_route(sel, n_tok):
    n_tiles = TOP_K * n_tok // MOE_TILE + N_EXPERTS
    e = sel[:, 0:TOP_K].astype(jnp.int32)
    flat_e = e.reshape(-1)
    onehot = (flat_e[None, :] == jnp.arange(N_EXPERTS)[:, None]).astype(jnp.int32)
    csum = jnp.cumsum(onehot, axis=1)
    counts = csum[:, -1]
    padded = (counts + MOE_TILE - 1) // MOE_TILE * MOE_TILE
    seg_end = jnp.cumsum(padded)
    seg_start = seg_end - padded
    pos = jnp.sum(onehot * (seg_start[:, None] + csum - onehot), axis=0)
    pos2 = pos.reshape(n_tok, TOP_K)
    tile_ix = jnp.arange(n_tiles)
    tile_expert = jnp.minimum(jnp.sum(tile_ix[:, None] >= (seg_end // MOE_TILE)[None, :], axis=1), N_EXPERTS - 1)
    n_valid = seg_end[-1] // MOE_TILE
    n_src = n_tok // MOE_SRC
    per_chunk = MOE_SRC * TOP_K
    before = jnp.concatenate([jnp.zeros((1, N_EXPERTS), jnp.int32), csum[:, per_chunk - 1::per_chunk].T], axis=0)
    run_start = seg_start[None, :] + before[:-1]
    run_cnt = before[1:] - before[:-1]
    cum = jnp.take(before, tile_expert, axis=1)
    local0 = tile_ix * MOE_TILE - jnp.take(seg_start, tile_expert)
    local1 = jnp.minimum(local0 + MOE_TILE, jnp.take(counts, tile_expert))
    c_lo = jnp.sum(cum[1:] <= local0[None, :], axis=0)
    c_hi = jnp.sum(cum[:-1] < local1[None, :], axis=0) - 1
    win = jnp.minimum(run_start // 16 * 16, n_tiles * MOE_TILE - MOE_CMB_WIN)
    win_s = jnp.minimum(run_start // 16 * 16, n_tiles * MOE_TILE - MOE_CMB_SMALL)
    cmb_small = jnp.all(run_start - win_s + run_cnt <= MOE_CMB_SMALL, axis=1)

    def token_rows(a, dtype):
        rows = jnp.transpose(a.reshape(n_src, MOE_SRC, TOP_K), (0, 2, 1)).astype(dtype)
        return jnp.zeros((n_src, 8, MOE_SRC), dtype).at[:, :TOP_K, :].set(rows)

    i32 = lambda a: a.astype(jnp.int32)
    return {
        "n_tiles": n_tiles, "tile_expert": i32(tile_expert), "n_valid": i32(n_valid).reshape(1),
        "c_lo": i32(c_lo), "c_hi": i32(c_hi), "run_start": i32(run_start.reshape(-1)),
        "run_cnt": i32(run_cnt.reshape(-1)), "win": i32(win.reshape(-1)),
        "run_off": i32((run_start - win).reshape(-1)),
        "win_s": i32(win_s.reshape(-1)), "run_off_s": i32((run_start - win_s).reshape(-1)),
        "cmb_small": i32(cmb_small),
        "pos_rows": token_rows(pos2, jnp.int32), "w_rows": token_rows(sel[:, TOP_K:2 * TOP_K], F32),
        "pos_cols": i32(pos2),
    }


def _dispatch_kernel(clo_ref, chi_ref, te_ref, rs_ref, rc_ref, h_ref, pos_ref, w_ref, o_ref, sw_ref, acc_ref):
    i = pl.program_id(0)
    base = i * MOE_TILE
    expert = te_ref[i]
    acc_ref[...] = jnp.zeros_like(acc_ref)
    sw_ref[...] = jnp.zeros_like(sw_ref)

    c_last = chi_ref[i]

    def item(c):
        run0 = rs_ref[c * N_EXPERTS + expert]
        lo = jnp.maximum(run0, base) - base
        hi = jnp.minimum(run0 + rc_ref[c * N_EXPERTS + expert], base + MOE_TILE) - base
        small = hi - jnp.minimum(lo // 16 * 16, MOE_TILE - MOE_DSP_SMALL) <= MOE_DSP_SMALL
        return lo, hi, small

    def window(c, lo, rows, live):
        pos = pos_ref[c]
        wts = w_ref[c]
        h = h_ref[pl.ds(pl.multiple_of(c * MOE_SRC, MOE_SRC), MOE_SRC), :]
        ws = pl.multiple_of(jnp.minimum(lo // 16 * 16, MOE_TILE - rows), 16)
        slot = base + ws + lax.broadcasted_iota(jnp.int32, (rows, MOE_SRC), 0)
        slot = jnp.where(live, slot, -1)
        hit0 = pos[0:1, :] == slot
        hit1 = pos[1:2, :] == slot
        acc_ref[pl.ds(ws, rows), :] += _dot(jnp.where(hit0 | hit1, 1.0, 0.0).astype(BF16), h)
        weight = jnp.sum(jnp.where(hit0, wts[0:1, :], 0.0) + jnp.where(hit1, wts[1:2, :], 0.0),
                         axis=1, keepdims=True)
        sw_ref[pl.ds(ws, rows), :] += jnp.broadcast_to(weight, (rows, LANES))

    def body(pair, carry):
        c0 = clo_ref[i] + 2 * pair
        c1 = jnp.minimum(c0 + 1, c_last)
        live1 = c0 + 1 <= c_last
        lo0, hi0, small0 = item(c0)
        lo1, hi1, small1 = item(c1)
        both_small = small0 & small1

        @pl.when(both_small)
        def _():
            window(c0, lo0, MOE_DSP_SMALL, hi0 > lo0)
            window(c1, lo1, MOE_DSP_SMALL, live1 & (hi1 > lo1))

        @pl.when(jnp.logical_not(both_small))
        def _():
            pl.when((hi0 > lo0) & small0)(lambda: window(c0, lo0, MOE_DSP_SMALL, True))
            pl.when((hi0 > lo0) & jnp.logical_not(small0))(lambda: window(c0, lo0, MOE_DSP_LARGE, True))
            pl.when(live1 & (hi1 > lo1) & small1)(lambda: window(c1, lo1, MOE_DSP_SMALL, True))
            pl.when(live1 & (hi1 > lo1) & jnp.logical_not(small1))(lambda: window(c1, lo1, MOE_DSP_LARGE, True))
        return carry

    lax.fori_loop(0, (c_last - clo_ref[i] + 2) // 2, body, 0)
    o_ref[...] = acc_ref[...].astype(BF16)


def _dispatch_call(h2, route):
    n_tok = h2.shape[0]
    n_tiles = route["n_tiles"]
    grid_spec = pltpu.PrefetchScalarGridSpec(
        num_scalar_prefetch=5,
        grid=(n_tiles,),
        in_specs=[
            pl.BlockSpec((n_tok, D_MODEL), lambda i, *_: (0, 0), pipeline_mode=pl.Buffered(1)),
            pl.BlockSpec((n_tok // MOE_SRC, 8, MOE_SRC), lambda i, *_: (0, 0, 0), pipeline_mode=pl.Buffered(1)),
            pl.BlockSpec((n_tok // MOE_SRC, 8, MOE_SRC), lambda i, *_: (0, 0, 0), pipeline_mode=pl.Buffered(1)),
        ],
        out_specs=[pl.BlockSpec((MOE_TILE, D_MODEL), lambda i, *_: (i, 0)),
                   pl.BlockSpec((MOE_TILE, LANES), lambda i, *_: (i, 0))],
        scratch_shapes=[pltpu.VMEM((MOE_TILE, D_MODEL), F32)],
    )
    return pl.pallas_call(
        _dispatch_kernel,
        grid_spec=grid_spec,
        out_shape=[jax.ShapeDtypeStruct((n_tiles * MOE_TILE, D_MODEL), BF16),
                   jax.ShapeDtypeStruct((n_tiles * MOE_TILE, LANES), F32)],
        compiler_params=_params("arbitrary"),
        name="moe_dispatch",
    )(route["c_lo"], route["c_hi"], route["tile_expert"], route["run_start"], route["run_cnt"],
      h2, route["pos_rows"], route["w_rows"])


def _expert_kernel(te_ref, nv_ref, x_ref, w1_ref, w3_ref, w2_ref, sw_ref, o_ref, acc_ref):
    i = pl.program_id(0)
    f = pl.program_id(1)
    last = pl.num_programs(1) - 1
    valid = i < nv_ref[0]

    @pl.when(valid & (f == 0))
    def _():
        acc_ref[...] = jnp.zeros_like(acc_ref)

    @pl.when(valid)
    def _():
        acc_ref[...] += _swiglu_chunk(x_ref[...], w1_ref[0, 0], w3_ref[0, 0], w2_ref[0, 0])

    @pl.when(valid & (f == last))
    def _():
        o_ref[...] = (acc_ref[...] * sw_ref[:, 0:1]).astype(BF16)

    @pl.when(jnp.logical_not(valid) & (f == last))
    def _():
        o_ref[...] = jnp.zeros_like(o_ref)


def _expert_call(xs, slot_w, route, w1, w3, w2, layer):
    n_tiles = route["n_tiles"]
    grid_spec = pltpu.PrefetchScalarGridSpec(
        num_scalar_prefetch=2,
        grid=(n_tiles, D_FF // MOE_FF_CHUNK),
        in_specs=[
            pl.BlockSpec((MOE_TILE, D_MODEL), lambda i, f, te, nv: (i, 0)),
            pl.BlockSpec((1, 1, D_MODEL, MOE_FF_CHUNK), lambda i, f, te, nv: (layer, te[i], 0, f)),
            pl.BlockSpec((1, 1, D_MODEL, MOE_FF_CHUNK), lambda i, f, te, nv: (layer, te[i], 0, f)),
            pl.BlockSpec((1, 1, MOE_FF_CHUNK, D_MODEL), lambda i, f, te, nv: (layer, te[i], f, 0)),
            pl.BlockSpec((MOE_TILE, LANES), lambda i, f, te, nv: (i, 0)),
        ],
        out_specs=pl.BlockSpec((MOE_TILE, D_MODEL), lambda i, f, te, nv: (i, 0)),
        scratch_shapes=[pltpu.VMEM((MOE_TILE, D_MODEL), F32)],
    )
    return pl.pallas_call(
        _expert_kernel,
        grid_spec=grid_spec,
        out_shape=jax.ShapeDtypeStruct((n_tiles * MOE_TILE, D_MODEL), BF16),
        compiler_params=_params("arbitrary", "arbitrary"),
        name="moe_experts",
    )(route["tile_expert"], route["n_valid"], xs, w1, w3, w2, slot_w)


def _combine_kernel(small_ref, cnt_ref, winl_ref, offl_ref, wins_ref, offs_ref,
                    y_hbm, pos_ref, x_ref, mod_ref, fg_ref, o_ref, buf_ref, sem_ref, *, final_norm):
    c = pl.program_id(0)
    cur = c % 2
    variants = ((MOE_CMB_SMALL, wins_ref, offs_ref), (MOE_CMB_WIN, winl_ref, offl_ref))

    def window_copies(chunk, buf_set, rows, win_ref):
        return [pltpu.make_async_copy(
            y_hbm.at[pl.ds(pl.multiple_of(win_ref[chunk * N_EXPERTS + e], 16), rows), :],
            buf_ref.at[buf_set, pl.ds(e * rows, rows), :],
            sem_ref.at[buf_set, e]) for e in range(N_EXPERTS)]

    def start(chunk, buf_set):
        for use, (rows, win_ref, _) in zip((small_ref[chunk] > 0, small_ref[chunk] == 0), variants):
            @pl.when(use)
            def _():
                for cp in window_copies(chunk, buf_set, rows, win_ref):
                    cp.start()

    pl.when(c == 0)(lambda: start(0, 0))
    pl.when(c + 1 < pl.num_programs(0))(lambda: start(jnp.minimum(c + 1, pl.num_programs(0) - 1), 1 - cur))

    def process(rows, win_ref, off_ref):
        row = lax.broadcasted_iota(jnp.int32, (1, N_EXPERTS * rows), 1)
        slot = jnp.full((1, N_EXPERTS * rows), -1, jnp.int32)
        for e in range(N_EXPERTS):
            local = row - e * rows
            off = off_ref[c * N_EXPERTS + e]
            inside = (local >= off) & (local < off + cnt_ref[c * N_EXPERTS + e]) & (local < rows)
            slot = jnp.where(inside, win_ref[c * N_EXPERTS + e] + local, slot)
        pos = pos_ref[...]
        hit = (pos[:, 0:1] == slot) | (pos[:, 1:2] == slot)
        onehot = jnp.where(hit, 1.0, 0.0).astype(BF16)
        for cp in window_copies(c, cur, rows, win_ref):
            cp.wait()
        update = _dot(onehot, buf_ref[cur, 0:N_EXPERTS * rows, :])
        o_ref[...] = _residual_out(x_ref[...], mod_ref[0][5:6], update, final_norm, fg_ref[...])

    for use, variant in zip((small_ref[c] > 0, small_ref[c] == 0), variants):
        pl.when(use)(functools.partial(process, *variant))


def _combine_call(ys, route, x, mod_l, final_g, first_tile, final_norm):
    n_tok = x.shape[0]
    rows = lambda width: pl.BlockSpec((MOE_SRC, width), lambda c, *_: (c, 0))
    grid_spec = pltpu.PrefetchScalarGridSpec(
        num_scalar_prefetch=6,
        grid=(n_tok // MOE_SRC,),
        in_specs=[
            pl.BlockSpec(memory_space=pl.ANY),
            rows(TOP_K),
            rows(D_MODEL),
            pl.BlockSpec((1, ADA_CHUNKS, D_MODEL), lambda c, *_: (_tile_mod_row(c, first_tile, MOE_SRC), 0, 0)),
            pl.BlockSpec((1, D_MODEL), lambda c, *_: (0, 0)),
        ],
        out_specs=rows(D_MODEL),
        scratch_shapes=[pltpu.VMEM((2, N_EXPERTS * MOE_CMB_WIN, D_MODEL), BF16),
                        pltpu.SemaphoreType.DMA((2, N_EXPERTS))],
    )
    return pl.pallas_call(
        functools.partial(_combine_kernel, final_norm=final_norm),
        grid_spec=grid_spec,
        out_shape=jax.ShapeDtypeStruct((n_tok, D_MODEL), F32),
        compiler_params=_params("arbitrary"),
        name="moe_combine",
    )(route["cmb_small"], route["run_cnt"], route["win"], route["run_off"], route["win_s"], route["run_off_s"],
      ys, route["pos_cols"], x, mod_l, final_g)


def _rope_tables():
    t = np.arange(SEQ)

    def angles(rot_dim):
        half = rot_dim // 2
        inv = ROPE_THETA ** (-jnp.arange(0, half, 2, dtype=F32) / half)
        ang = jnp.concatenate([jnp.asarray(t // GRID_W, F32)[:, None] * inv[None, :],
                               jnp.asarray(t % GRID_W, F32)[:, None] * inv[None, :]], axis=-1)
        return jnp.cos(ang), jnp.sin(ang)

    def pad_rows(a, fill):
        return jnp.concatenate([jnp.full((ROW_TILE, a.shape[1]), fill, F32), a], axis=0)

    c, s = angles(HEAD_DIM)
    cos64 = jnp.tile(jnp.concatenate([c, c], axis=-1), (1, 4))
    sin64 = jnp.tile(jnp.concatenate([s, s], axis=-1), (1, 4))
    c, s = angles(MLA_ROPE)
    one = jnp.ones((SEQ, MLA_NOPE), F32)
    zero = jnp.zeros((SEQ, MLA_NOPE), F32)
    tail1 = jnp.ones((SEQ, MLA_PAD - MLA_NOPE - MLA_ROPE), F32)
    tail0 = jnp.zeros((SEQ, MLA_PAD - MLA_NOPE - MLA_ROPE), F32)
    cosm = jnp.tile(jnp.concatenate([one, c, c, tail1], axis=-1), (1, MLA_HEADS))
    sinm = jnp.tile(jnp.concatenate([zero, s, s, tail0], axis=-1), (1, MLA_HEADS))
    blockdiag = jnp.asarray(np.kron(np.eye(4), np.ones((HEAD_DIM, HEAD_DIM))), BF16)
    place = np.zeros((256, MLA_HEADS * MLA_PAD), np.float32)
    for h in range(MLA_HEADS):
        for r in range(MLA_ROPE):
            place[MLA_KV_RANK + r, h * MLA_PAD + MLA_NOPE + r] = 1.0
    return {
        "cos64": pad_rows(cos64, 1.0), "sin64": pad_rows(sin64, 0.0),
        "cosm": pad_rows(cosm, 1.0), "sinm": pad_rows(sinm, 0.0),
        "blockdiag": blockdiag, "pe_place": jnp.asarray(place, BF16),
    }


def _pack_w_in(w):
    n_qkv = COL_DKVA + MLA_KV_RANK + MLA_ROPE
    wb = w.astype(BF16)
    pad = jnp.zeros(w.shape[:2] + (COL_GATE - n_qkv,), BF16)
    return jnp.concatenate([wb[..., :n_qkv], pad, wb[..., n_qkv:] * 0.5], axis=-1)


def _pack_mla(wqb, wkvb):
    dqh = MLA_NOPE + MLA_ROPE
    q = wqb.reshape(MLA_Q_RANK, MLA_HEADS, dqh)
    q = jnp.pad(q, ((0, 0), (0, 0), (0, MLA_PAD - dqh))).reshape(MLA_Q_RANK, MLA_HEADS * MLA_PAD)
    kv = wkvb.reshape(MLA_KV_RANK, MLA_HEADS, MLA_NOPE + MLA_V)
    k = jnp.pad(kv[:, :, :MLA_NOPE], ((0, 0), (0, 0), (0, MLA_PAD - MLA_NOPE)))
    k = k.reshape(MLA_KV_RANK, MLA_HEADS * MLA_PAD)
    v = jnp.pad(kv[:, :, MLA_NOPE:], ((0, 0), (0, 0), (0, MLA_PAD - MLA_V))).reshape(MLA_KV_RANK, MLA_HEADS * MLA_PAD)
    return q.astype(BF16), k.astype(BF16), v.astype(BF16)


def _split2_host(w):
    hi = w.astype(BF16)
    return jnp.stack([hi, (w - hi.astype(F32)).astype(BF16)])


def kernel(x, c, ctx, c_ctx, norm1_g, norm2_g, w_ada, b_ada, w_in, na_rpb, gb_qnorm, gb_knorm, wc_sink,
           mla_qnorm, mla_kvnorm, mla_wqb, mla_wkvb, w_branch, w_out, ffn_w1, ffn_w3, ffn_w2,
           moe_router, moe_w1, moe_w3, moe_w2, final_g):
    n_batch = x.shape[0]
    assert x.shape[1:] == (SEQ, D_MODEL) and ctx.shape[1:] == (CTX_LEN, D_MODEL)
    assert n_batch * CTX_LEN <= CTX_ROWS and n_batch <= 8

    ctx_rows = ctx.reshape(n_batch * CTX_LEN, D_MODEL)
    if ctx_rows.shape[0] < CTX_ROWS:
        ctx_rows = jnp.pad(ctx_rows, ((0, CTX_ROWS - ctx_rows.shape[0]), (0, 0)))
    xt = jnp.concatenate([ctx_rows, x.reshape(n_batch * SEQ, D_MODEL)], axis=0)

    cvec = jnp.zeros((16, D_MODEL), F32).at[:n_batch].set(c).at[8].set(c_ctx)
    mod = _mod_call(cvec, w_ada, b_ada).reshape(DEPTH, 16, ADA_CHUNKS, D_MODEL)
    tabs = _rope_tables()
    w_in_p = _pack_w_in(w_in)
    moe_w = tuple(w.astype(BF16) for w in (moe_w1, moe_w3, moe_w2))
    mla_scale = (MLA_NOPE + MLA_ROPE) ** -0.5
    lat_tile0 = CTX_ROWS // ROW_TILE

    for l in range(DEPTH):
        with_ctx = l < DEPTH - 1
        wq, wk, wv = _pack_mla(mla_wqb[l], mla_wkvb[l])
        lw = {
            "gq": jnp.tile(gb_qnorm[l], 4)[None, :], "gk": jnp.tile(gb_knorm[l], 2)[None, :],
            "qn": mla_qnorm[l][None, :], "kvn": mla_kvnorm[l][None, :], "wqb": wq, "wk": wk, "wv": wv,
        }
        pa, cv, gate, qb, kb, qc, kc, qd, kd, vd, vb = _inproj_call(
            xt, mod[l], norm1_g[l][None, :], w_in_p, l, tabs, lw)

        oa = _natten_call(n_batch, with_ctx, pa, _natten_bias_table(na_rpb[l]))
        ob = _gqa_attn_t_call(n_batch, with_ctx, qb, kb, vb, n_kv=GB_KV_HEADS,
                              group=GB_Q_HEADS // GB_KV_HEADS, name="attn_global")
        oc = _window_attn_call(n_batch, with_ctx, qc, kc, cv, wc_sink[l])
        od = _dense_attn_call(n_batch, with_ctx, (qd, MLA_HEADS * MLA_PAD, 0), kd, (vd, MLA_HEADS * MLA_PAD, 0),
                              n_kv=MLA_HEADS, group=1, dq=MLA_PAD, scale=mla_scale, name="attn_mla")

        is_moe = l % 2 == 1
        router3 = None
        if is_moe:
            router3 = _split2_host(jnp.pad(moe_router[l // 2], ((0, 0), (0, LANES - N_EXPERTS))))
        first_tile = 0 if with_ctx else lat_tile0
        res = _merge_call((oa, ob, oc, od), gate, xt, mod[l], norm2_g[l][None, :],
                          (0.5 * w_branch[l]).astype(BF16), w_out[l].astype(BF16), router3, first_tile)
        x_mid, h2 = res[0], res[1]
        final_norm = l == DEPTH - 1
        if is_moe:
            route = _moe_route(res[2], h2.shape[0])
            xs, slot_w = _dispatch_call(h2, route)
            ys = _expert_call(xs, slot_w, route, *moe_w, l // 2)
            xt = _combine_call(ys, route, x_mid, mod[l], final_g[None, :],
                               0 if with_ctx else CTX_ROWS // MOE_SRC, final_norm)
        else:
            xt = _ffn_call(h2, ffn_w1, ffn_w3, ffn_w2, l // 2, x_mid, mod[l], final_g[None, :],
                           0 if with_ctx else CTX_ROWS // FFN_ROWS, final_norm)

    return xt.reshape(n_batch, SEQ, D_MODEL)
```
